```python
import jax, jax.numpy as jnp
from jax import lax
import numpy as np

D_MODEL = 2048
BATCH = 8
SEQ = 4096
DEPTH = 1

D_CONV = D_MODEL // 2
CONV_GROUPS = 8
CONV_WIDTH = 3
GLA_HEADS = 4
D_GLA_V = D_MODEL // 2
D_GLA_K = D_GLA_V // 2
HEAD_K = D_GLA_K // GLA_HEADS
HEAD_V = D_GLA_V // GLA_HEADS
GATE_RANK = 16
GATE_TAU = 16.0
CHUNK = 64
D_FF = 4 * D_MODEL
LN_EPS = 1e-5
RMS_EPS = 1e-6
DN_ALPHA = (2.0 * DEPTH) ** 0.25
DN_BETA = (8.0 * DEPTH) ** -0.25

PROJ_SIZES = (D_CONV, D_CONV, D_CONV, D_GLA_K, D_GLA_K, D_GLA_V, D_GLA_V, GATE_RANK)
D_IN_PROJ = sum(PROJ_SIZES)
PROJ_SPLITS = tuple(int(s) for s in np.cumsum(PROJ_SIZES)[:-1])

kernel_name = "hymba_conv_gla_deepnorm_block"


def layer_norm(x, g, b):
    xf = x.astype(jnp.float32)
    mu = jnp.mean(xf, axis=-1, keepdims=True)
    var = jnp.mean(jnp.square(xf - mu), axis=-1, keepdims=True)
    y = (xf - mu) * lax.rsqrt(var + LN_EPS)
    return (y * g.astype(jnp.float32) + b.astype(jnp.float32)).astype(x.dtype)


def group_rms_norm(x, g, groups):
    shp = x.shape
    xf = x.astype(jnp.float32).reshape(shp[:-1] + (groups, shp[-1] // groups))
    xf = xf * lax.rsqrt(jnp.mean(jnp.square(xf), axis=-1, keepdims=True) + RMS_EPS)
    return (xf.reshape(shp) * g.astype(jnp.float32)).astype(x.dtype)


def short_gated_conv(b_gate, c_gate, u, conv_w):
    h = c_gate * u
    y = lax.conv_general_dilated(
        h, conv_w[:, None, :].astype(h.dtype), window_strides=(1,),
        padding=[(CONV_WIDTH - 1, 0)], dimension_numbers=("NWC", "WIO", "NWC"),
        feature_group_count=h.shape[-1])
    return b_gate * y


def gla_chunked(q, k, v, log_a):
    bsz, seq = q.shape[0], q.shape[1]
    n_chunks = seq // CHUNK

    def to_chunks(t):
        return t.reshape(bsz, n_chunks, CHUNK, GLA_HEADS, t.shape[-1]).transpose(0, 3, 1, 2, 4).astype(jnp.float32)

    qc = to_chunks(q) * (HEAD_K ** -0.5)
    kc, vc, gc = to_chunks(k), to_chunks(v), to_chunks(log_a)
    bcum = jnp.cumsum(gc, axis=3)
    b_last = bcum[:, :, :, -1:, :]
    q_dec = qc * jnp.exp(bcum)
    k_inv = kc * jnp.exp(-bcum)
    k_end = kc * jnp.exp(b_last - bcum)

    causal = jnp.tril(jnp.ones((CHUNK, CHUNK), dtype=bool))
    scores = jnp.where(causal, jnp.einsum("bhncd,bhnsd->bhncs", q_dec, k_inv), 0.0)
    o_intra = jnp.einsum("bhncs,bhnse->bhnce", scores, vc)

    delta = jnp.einsum("bhncd,bhnce->bhnde", k_end, vc)
    decay = jnp.exp(b_last[:, :, :, 0, :])

    def step(state, inp):
        dec, dlt = inp
        return dec[..., None] * state + dlt, state

    init = jnp.zeros((bsz, GLA_HEADS, HEAD_K, HEAD_V), jnp.float32)
    _, states = lax.scan(step, init, (jnp.moveaxis(decay, 2, 0), jnp.moveaxis(delta, 2, 0)))
    states = jnp.moveaxis(states, 0, 2)
    o_inter = jnp.einsum("bhncd,bhnde->bhnce", q_dec, states)
    o = o_intra + o_inter
    return o.transpose(0, 2, 3, 1, 4).reshape(bsz, seq, GLA_HEADS * HEAD_V)


def _fwd_setup_inputs(seed: int = 0) -> dict:
    key = jax.random.key(seed)
    ks = jax.random.split(key, 16)
    f32 = jnp.float32
    x = jax.random.normal(ks[0], (BATCH, SEQ, D_MODEL), f32)
    col_scale = jnp.concatenate([
        jnp.full((D_CONV,), 1.0, f32), jnp.full((D_CONV,), 1.0, f32), jnp.full((D_CONV,), DN_BETA, f32),
        jnp.full((D_GLA_K,), 1.0, f32), jnp.full((D_GLA_K,), 1.0, f32), jnp.full((D_GLA_V,), DN_BETA, f32),
        jnp.full((D_GLA_V,), 1.0, f32), jnp.full((GATE_RANK,), 1.0, f32)])
    w_in = jax.random.normal(ks[1], (DEPTH, D_MODEL, D_IN_PROJ), f32) * (D_MODEL ** -0.5) * col_scale
    conv_w = jax.random.normal(ks[2], (DEPTH, CONV_WIDTH, D_CONV), f32) * (CONV_WIDTH ** -0.5)
    conv_norm_g = 1.0 + 0.02 * jax.random.normal(ks[3], (DEPTH, D_CONV), f32)
    w_gate_up = jax.random.normal(ks[4], (DEPTH, GATE_RANK, D_GLA_K), f32) * (GATE_RANK ** -0.5)
    gate_bias = 0.1 * jax.random.normal(ks[5], (DEPTH, D_GLA_K), f32)
    gla_norm_g = 1.0 + 0.02 * jax.random.normal(ks[6], (DEPTH, D_GLA_V), f32)
    w_out = jax.random.normal(ks[7], (DEPTH, D_CONV + D_GLA_V, D_MODEL), f32) * ((D_CONV + D_GLA_V) ** -0.5) * DN_BETA
    ln1_g = 1.0 + 0.02 * jax.random.normal(ks[8], (DEPTH, D_MODEL), f32)
    ln1_b = 0.02 * jax.random.normal(ks[9], (DEPTH, D_MODEL), f32)
    w_ff_up = jax.random.normal(ks[10], (DEPTH, D_MODEL, D_FF), f32) * (D_MODEL ** -0.5) * DN_BETA
    w_ff_down = jax.random.normal(ks[11], (DEPTH, D_FF, D_MODEL), f32) * (D_FF ** -0.5) * DN_BETA
    ln2_g = 1.0 + 0.02 * jax.random.normal(ks[12], (DEPTH, D_MODEL), f32)
    ln2_b = 0.02 * jax.random.normal(ks[13], (DEPTH, D_MODEL), f32)
    return {"x": x, "w_in": w_in, "conv_w": conv_w, "conv_norm_g": conv_norm_g,
            "w_gate_up": w_gate_up, "gate_bias": gate_bias, "gla_norm_g": gla_norm_g,
            "w_out": w_out, "ln1_g": ln1_g, "ln1_b": ln1_b, "w_ff_up": w_ff_up,
            "w_ff_down": w_ff_down, "ln2_g": ln2_g, "ln2_b": ln2_b}


def _fwd_reference(x, w_in, conv_w, conv_norm_g, w_gate_up, gate_bias, gla_norm_g, w_out,
              ln1_g, ln1_b, w_ff_up, w_ff_down, ln2_g, ln2_b):
    bsz, seq = x.shape[0], x.shape[1]
    for l in range(DEPTH):
        proj = x @ w_in[l]
        b_gate, c_gate, u, q, k, v, r, z_low = jnp.split(proj, PROJ_SPLITS, axis=-1)

        y_conv = short_gated_conv(b_gate, c_gate, u, conv_w[l])
        y_conv = group_rms_norm(y_conv, conv_norm_g[l], CONV_GROUPS)

        z = (z_low @ w_gate_up[l] + gate_bias[l]).astype(jnp.float32)
        log_a = jax.nn.log_sigmoid(z) / GATE_TAU
        hk = (bsz, seq, GLA_HEADS, HEAD_K)
        o = gla_chunked(q.reshape(hk), k.reshape(hk), v.reshape(bsz, seq, GLA_HEADS, HEAD_V), log_a.reshape(hk))
        o = group_rms_norm(o.astype(x.dtype), gla_norm_g[l], GLA_HEADS)
        y_gla = o * jax.nn.silu(r)

        mix = jnp.concatenate([y_conv, y_gla], axis=-1) @ w_out[l]
        x = layer_norm(DN_ALPHA * x + mix, ln1_g[l], ln1_b[l])

        ff = jnp.square(jax.nn.relu(x @ w_ff_up[l])) @ w_ff_down[l]
        x = layer_norm(DN_ALPHA * x + ff, ln2_g[l], ln2_b[l])
    return x


import jax as _jax
import jax.numpy as _jnp

TWIN_FORMAT = 'train_step'
FWD_PARAMS = ['x', 'w_in', 'conv_w', 'conv_norm_g', 'w_gate_up', 'gate_bias', 'gla_norm_g', 'w_out', 'ln1_g', 'ln1_b', 'w_ff_up', 'w_ff_down', 'ln2_g', 'ln2_b']
TWIN_WEIGHTS = ['w_in', 'conv_w', 'conv_norm_g', 'w_gate_up', 'gate_bias', 'gla_norm_g', 'w_out', 'ln1_g', 'ln1_b', 'w_ff_up', 'w_ff_down', 'ln2_g', 'ln2_b']
TWIN_DIFF_INPUT = 'x'
TWIN_INPUTS = ['x', 'w_in', 'conv_w', 'conv_norm_g', 'w_gate_up', 'gate_bias', 'gla_norm_g', 'w_out', 'ln1_g', 'ln1_b', 'w_ff_up', 'w_ff_down', 'ln2_g', 'ln2_b', 'loss_target', 'm_w_in', 'm_conv_w', 'm_conv_norm_g', 'm_w_gate_up', 'm_gate_bias', 'm_gla_norm_g', 'm_w_out', 'm_ln1_g', 'm_ln1_b', 'm_w_ff_up', 'm_w_ff_down', 'm_ln2_g', 'm_ln2_b', 'v_w_in', 'v_conv_w', 'v_conv_norm_g', 'v_w_gate_up', 'v_gate_bias', 'v_gla_norm_g', 'v_w_out', 'v_ln1_g', 'v_ln1_b', 'v_w_ff_up', 'v_w_ff_down', 'v_ln2_g', 'v_ln2_b']
TWIN_OUTPUTS = ['loss', 'grad_x', 'grad_w_in', 'grad_conv_w', 'grad_conv_norm_g', 'grad_w_gate_up', 'grad_gate_bias', 'grad_gla_norm_g', 'grad_w_out', 'grad_ln1_g', 'grad_ln1_b', 'grad_w_ff_up', 'grad_w_ff_down', 'grad_ln2_g', 'grad_ln2_b', 'delta_w_in', 'delta_conv_w', 'delta_conv_norm_g', 'delta_w_gate_up', 'delta_gate_bias', 'delta_gla_norm_g', 'delta_w_out', 'delta_ln1_g', 'delta_ln1_b', 'delta_w_ff_up', 'delta_w_ff_down', 'delta_ln2_g', 'delta_ln2_b', 'new_m_w_in', 'new_m_conv_w', 'new_m_conv_norm_g', 'new_m_w_gate_up', 'new_m_gate_bias', 'new_m_gla_norm_g', 'new_m_w_out', 'new_m_ln1_g', 'new_m_ln1_b', 'new_m_w_ff_up', 'new_m_w_ff_down', 'new_m_ln2_g', 'new_m_ln2_b', 'new_v_w_in', 'new_v_conv_w', 'new_v_conv_norm_g', 'new_v_w_gate_up', 'new_v_gate_bias', 'new_v_gla_norm_g', 'new_v_w_out', 'new_v_ln1_g', 'new_v_ln1_b', 'new_v_w_ff_up', 'new_v_w_ff_down', 'new_v_ln2_g', 'new_v_ln2_b']
TWIN_LEAF_KINDS = {'loss': 'loss', 'grad_x': 'grad_x', 'grad_w_in': 'grad_w', 'grad_conv_w': 'grad_w', 'grad_conv_norm_g': 'grad_w', 'grad_w_gate_up': 'grad_w', 'grad_gate_bias': 'grad_w', 'grad_gla_norm_g': 'grad_w', 'grad_w_out': 'grad_w', 'grad_ln1_g': 'grad_w', 'grad_ln1_b': 'grad_w', 'grad_w_ff_up': 'grad_w', 'grad_w_ff_down': 'grad_w', 'grad_ln2_g': 'grad_w', 'grad_ln2_b': 'grad_w', 'delta_w_in': 'delta_w', 'delta_conv_w': 'delta_w', 'delta_conv_norm_g': 'delta_w', 'delta_w_gate_up': 'delta_w', 'delta_gate_bias': 'delta_w', 'delta_gla_norm_g': 'delta_w', 'delta_w_out': 'delta_w', 'delta_ln1_g': 'delta_w', 'delta_ln1_b': 'delta_w', 'delta_w_ff_up': 'delta_w', 'delta_w_ff_down': 'delta_w', 'delta_ln2_g': 'delta_w', 'delta_ln2_b': 'delta_w', 'new_m_w_in': 'new_m', 'new_m_conv_w': 'new_m', 'new_m_conv_norm_g': 'new_m', 'new_m_w_gate_up': 'new_m', 'new_m_gate_bias': 'new_m', 'new_m_gla_norm_g': 'new_m', 'new_m_w_out': 'new_m', 'new_m_ln1_g': 'new_m', 'new_m_ln1_b': 'new_m', 'new_m_w_ff_up': 'new_m', 'new_m_w_ff_down': 'new_m', 'new_m_ln2_g': 'new_m', 'new_m_ln2_b': 'new_m', 'new_v_w_in': 'new_v', 'new_v_conv_w': 'new_v', 'new_v_conv_norm_g': 'new_v', 'new_v_w_gate_up': 'new_v', 'new_v_gate_bias': 'new_v', 'new_v_gla_norm_g': 'new_v', 'new_v_w_out': 'new_v', 'new_v_ln1_g': 'new_v', 'new_v_ln1_b': 'new_v', 'new_v_w_ff_up': 'new_v', 'new_v_w_ff_down': 'new_v', 'new_v_ln2_g': 'new_v', 'new_v_ln2_b': 'new_v'}


def _forward(args):
    return _fwd_reference(*[args[k] for k in FWD_PARAMS])


def _output_shape():
    def fwd():
        inp = _fwd_setup_inputs(0)
        return _fwd_reference(*[inp[k] for k in FWD_PARAMS])
    out = _jax.eval_shape(fwd)
    return out.shape, out.dtype

N_MICROBATCH = 1
ADAM_LR = 0.001
ADAM_B1 = 0.9
ADAM_B2 = 0.999
ADAM_EPS = 1e-08
ADAM_WD = 0.01
ADAM_STEP = 10
PER_EXAMPLE_BATCH_AXIS = {'x': 0, 'loss_target': 0}
SHARED_INPUTS = []
_WEIGHT_DTYPES = {'w_in': _jnp.float32, 'conv_w': _jnp.float32, 'conv_norm_g': _jnp.float32, 'w_gate_up': _jnp.float32, 'gate_bias': _jnp.float32, 'gla_norm_g': _jnp.float32, 'w_out': _jnp.float32, 'ln1_g': _jnp.float32, 'ln1_b': _jnp.float32, 'w_ff_up': _jnp.float32, 'w_ff_down': _jnp.float32, 'ln2_g': _jnp.float32, 'ln2_b': _jnp.float32}
MOMENT_SCALE = {'w_in': 7.073191e-02, 'conv_w': 4.254479e-02, 'conv_norm_g': 4.112151e-02, 'w_gate_up': 3.983022e-03, 'gate_bias': 1.637571e-02, 'gla_norm_g': 2.619056e-02, 'w_out': 5.729768e-02, 'ln1_g': 5.585808e-01, 'ln1_b': 2.961143e-01, 'w_ff_up': 1.825817e-02, 'w_ff_down': 4.764970e-02, 'ln2_g': 1.601075e+01, 'ln2_b': 1.412681e+00}


def _to_microbatches(a, axis):
    t = _jnp.moveaxis(a, axis, 0)
    t = t.reshape((N_MICROBATCH, t.shape[0] // N_MICROBATCH) + t.shape[1:])
    return _jnp.moveaxis(t, 1, axis + 1)


def setup_inputs(seed: int = 0) -> dict:
    inp = _fwd_setup_inputs(seed)
    key = _jax.random.fold_in(_jax.random.key(seed), 7919)
    shape, _ = _output_shape()
    out = dict(inp)
    out["loss_target"] = _jax.random.normal(_jax.random.fold_in(key, 0), shape, _jnp.float32)
    for i, name in enumerate(TWIN_WEIGHTS):
        w = inp[name].astype(_jnp.float32)
        if MOMENT_SCALE is None:
            s = _jnp.sqrt(_jnp.mean(_jnp.square(w)) + 1e-30)
        else:
            s = MOMENT_SCALE[name]
        km, kv = _jax.random.split(_jax.random.fold_in(key, i + 1))
        out[name] = w
        out["m_" + name] = s * _jax.random.normal(km, w.shape, _jnp.float32)
        out["v_" + name] = (s * s) * _jax.random.uniform(kv, w.shape, _jnp.float32, 0.5, 1.5)
    if N_MICROBATCH > 1:
        for name, axis in PER_EXAMPLE_BATCH_AXIS.items():
            out[name] = _to_microbatches(out[name], axis)
    return {'x': out['x'], 'w_in': out['w_in'], 'conv_w': out['conv_w'], 'conv_norm_g': out['conv_norm_g'], 'w_gate_up': out['w_gate_up'], 'gate_bias': out['gate_bias'], 'gla_norm_g': out['gla_norm_g'], 'w_out': out['w_out'], 'ln1_g': out['ln1_g'], 'ln1_b': out['ln1_b'], 'w_ff_up': out['w_ff_up'], 'w_ff_down': out['w_ff_down'], 'ln2_g': out['ln2_g'], 'ln2_b': out['ln2_b'], 'loss_target': out['loss_target'], 'm_w_in': out['m_w_in'], 'm_conv_w': out['m_conv_w'], 'm_conv_norm_g': out['m_conv_norm_g'], 'm_w_gate_up': out['m_w_gate_up'], 'm_gate_bias': out['m_gate_bias'], 'm_gla_norm_g': out['m_gla_norm_g'], 'm_w_out': out['m_w_out'], 'm_ln1_g': out['m_ln1_g'], 'm_ln1_b': out['m_ln1_b'], 'm_w_ff_up': out['m_w_ff_up'], 'm_w_ff_down': out['m_w_ff_down'], 'm_ln2_g': out['m_ln2_g'], 'm_ln2_b': out['m_ln2_b'], 'v_w_in': out['v_w_in'], 'v_conv_w': out['v_conv_w'], 'v_conv_norm_g': out['v_conv_norm_g'], 'v_w_gate_up': out['v_w_gate_up'], 'v_gate_bias': out['v_gate_bias'], 'v_gla_norm_g': out['v_gla_norm_g'], 'v_w_out': out['v_w_out'], 'v_ln1_g': out['v_ln1_g'], 'v_ln1_b': out['v_ln1_b'], 'v_w_ff_up': out['v_w_ff_up'], 'v_w_ff_down': out['v_w_ff_down'], 'v_ln2_g': out['v_ln2_g'], 'v_ln2_b': out['v_ln2_b']}


def _loss(weights, diff, rest, loss_target):
    with _jax.named_scope("forward"):
        args = {**rest, TWIN_DIFF_INPUT: diff, **{k: w.astype(_WEIGHT_DTYPES[k]) for k, w in weights.items()}}
        y = _forward(args)
    with _jax.named_scope("loss_head"):
        err = _jnp.square(y.astype(_jnp.float32) - loss_target)
        return 0.5 * _jnp.sum(_jnp.mean(err, axis=-1)) if err.ndim else 0.5 * err


def _adamw(w, g, m, v):
    m = ADAM_B1 * m + (1.0 - ADAM_B1) * g
    v = ADAM_B2 * v + (1.0 - ADAM_B2) * _jnp.square(g)
    m_hat = m / (1.0 - ADAM_B1 ** ADAM_STEP)
    v_hat = v / (1.0 - ADAM_B2 ** ADAM_STEP)
    delta = -ADAM_LR * (m_hat / (_jnp.sqrt(v_hat) + ADAM_EPS) + ADAM_WD * w)
    return delta, m, v


def reference(x, w_in, conv_w, conv_norm_g, w_gate_up, gate_bias, gla_norm_g, w_out, ln1_g, ln1_b, w_ff_up, w_ff_down, ln2_g, ln2_b, loss_target, m_w_in, m_conv_w, m_conv_norm_g, m_w_gate_up, m_gate_bias, m_gla_norm_g, m_w_out, m_ln1_g, m_ln1_b, m_w_ff_up, m_w_ff_down, m_ln2_g, m_ln2_b, v_w_in, v_conv_w, v_conv_norm_g, v_w_gate_up, v_gate_bias, v_gla_norm_g, v_w_out, v_ln1_g, v_ln1_b, v_w_ff_up, v_w_ff_down, v_ln2_g, v_ln2_b):
    given = dict(x=x, w_in=w_in, conv_w=conv_w, conv_norm_g=conv_norm_g, w_gate_up=w_gate_up, gate_bias=gate_bias, gla_norm_g=gla_norm_g, w_out=w_out, ln1_g=ln1_g, ln1_b=ln1_b, w_ff_up=w_ff_up, w_ff_down=w_ff_down, ln2_g=ln2_g, ln2_b=ln2_b, loss_target=loss_target, m_w_in=m_w_in, m_conv_w=m_conv_w, m_conv_norm_g=m_conv_norm_g, m_w_gate_up=m_w_gate_up, m_gate_bias=m_gate_bias, m_gla_norm_g=m_gla_norm_g, m_w_out=m_w_out, m_ln1_g=m_ln1_g, m_ln1_b=m_ln1_b, m_w_ff_up=m_w_ff_up, m_w_ff_down=m_w_ff_down, m_ln2_g=m_ln2_g, m_ln2_b=m_ln2_b, v_w_in=v_w_in, v_conv_w=v_conv_w, v_conv_norm_g=v_conv_norm_g, v_w_gate_up=v_w_gate_up, v_gate_bias=v_gate_bias, v_gla_norm_g=v_gla_norm_g, v_w_out=v_w_out, v_ln1_g=v_ln1_g, v_ln1_b=v_ln1_b, v_w_ff_up=v_w_ff_up, v_w_ff_down=v_w_ff_down, v_ln2_g=v_ln2_g, v_ln2_b=v_ln2_b)
    weights = {n: given[n] for n in TWIN_WEIGHTS}
    shared = {n: given[n] for n in SHARED_INPUTS}
    per_example = {n: given[n] for n in ['x']}
    grad_fn = _jax.value_and_grad(_loss, argnums=(0, 1))

    def one_microbatch(ex, loss_target):
        ex = dict(ex)
        diff = ex.pop(TWIN_DIFF_INPUT)
        return grad_fn(weights, diff, {**shared, **ex}, loss_target)

    if N_MICROBATCH == 1:
        loss, (grad_w, grad_x) = one_microbatch(per_example, given["loss_target"])
    else:
        def body(carry, xs):
            loss_sum, grad_sum = carry
            l_k, (gw_k, gx_k) = one_microbatch(xs[0], xs[1])
            with _jax.named_scope("update"):
                return (loss_sum + l_k, _jax.tree.map(_jnp.add, grad_sum, gw_k)), gx_k

        init = (_jnp.zeros((), _jnp.float32), _jax.tree.map(_jnp.zeros_like, weights))
        (loss, grad_w), grad_x = _jax.lax.scan(body, init, (per_example, given["loss_target"]))
    with _jax.named_scope("update"):
        delta_w, new_m, new_v = {}, {}, {}
        for n in TWIN_WEIGHTS:
            delta_w[n], new_m[n], new_v[n] = _adamw(weights[n], grad_w[n], given["m_" + n], given["v_" + n])
    return (loss, grad_x, *[grad_w[n] for n in TWIN_WEIGHTS], *[delta_w[n] for n in TWIN_WEIGHTS],
            *[new_m[n] for n in TWIN_WEIGHTS], *[new_v[n] for n in TWIN_WEIGHTS])
```

```python
import jax
import jax.numpy as jnp
from jax import lax
from jax.experimental import pallas as pl
from jax.experimental.pallas import tpu as pltpu

F32 = jnp.float32
BF16 = jnp.bfloat16

D_CONV = 1024
CONV_GROUPS = 8
GLA_HEADS = 4
HEAD_K = 128
HEAD_V = 256
D_GLA_K = 512
D_GLA_V = 1024
GATE_RANK = 16
GATE_TAU = 16.0
CHUNK = 64
LN_EPS = 1e-5
RMS_EPS = 1e-6
DN_ALPHA = 2.0 ** 0.25
D_IN_PROJ = 6160
ADAM_LR = 0.001
ADAM_B1 = 0.9
ADAM_B2 = 0.999
ADAM_EPS = 1e-08
ADAM_WD = 0.01
ADAM_STEP = 10

N_DEV = 8
N_CHIP = 4
LANE = 128
HALF_P = 3200
P_INT = 2 * HALF_P
CONV_COLS = 3 * D_CONV
GLA_COLS = D_IN_PROJ - CONV_COLS
SP_ROWS = 32
SP_COLS = 1024
VMEM_LIMIT = 56 * 1024 * 1024

NN = ((1,), (0,))
NT = ((1,), (1,))
TN = ((0,), (0,))
MESH = pl.DeviceIdType.MESH


def _dot(a, b, dims, precision=None):
    return lax.dot_general(a, b, (dims, ((), ())), preferred_element_type=F32, precision=precision)


def _tile(n, pref):
    if n <= pref:
        return n
    t = (pref // LANE) * LANE
    while t > 0 and n % t:
        t -= LANE
    assert t > 0, (n, pref)
    return t


def _params(n_axes):
    return pltpu.CompilerParams(dimension_semantics=("arbitrary",) * n_axes, vmem_limit_bytes=VMEM_LIMIT)


def _full(shape):
    nd = len(shape)
    return pl.BlockSpec(shape, lambda *_: (0,) * nd)


def _mm(name, mode, a, b, *, M, N, K, tm, tn, tk, outs, epilogue, extras=(), a_fn=None, a_spec=None, b_spec=None):
    ni, nj, nk = M // tm, N // tn, K // tk
    assert ni * tm == M and nj * tn == N and nk * tk == K, (name, M, N, K, tm, tn, tk)
    if a_spec is None:
        a_spec = (pl.BlockSpec((tk, tm), lambda i, j, k: (k, i)) if mode == "tn"
                  else pl.BlockSpec((tm, tk), lambda i, j, k: (i, k)))
    if b_spec is None:
        b_spec = (pl.BlockSpec((tn, tk), lambda i, j, k: (j, k)) if mode == "nt"
                  else pl.BlockSpec((tk, tn), lambda i, j, k: (k, j)))
    dims = {"nn": NN, "nt": NT, "tn": TN}[mode]
    n_ex, n_out = len(extras), len(outs)

    def body(*refs):
        a_ref, b_ref = refs[0], refs[1]
        ex = refs[2:2 + n_ex]
        o = refs[2 + n_ex:2 + n_ex + n_out]
        acc_ref = refs[2 + n_ex + n_out]
        i, j, k = pl.program_id(0), pl.program_id(1), pl.program_id(2)
        av = a_ref[...]
        if a_fn is not None:
            av = a_fn(av)
        part = _dot(av, b_ref[...], dims)
        if nk == 1:
            acc_ref[...] = part
            epilogue(acc_ref, ex, o, i, j)
        else:
            @pl.when(k == 0)
            def _():
                acc_ref[...] = part

            @pl.when(k > 0)
            def _():
                acc_ref[...] += part

            @pl.when(k == nk - 1)
            def _():
                epilogue(acc_ref, ex, o, i, j)

    return pl.pallas_call(
        body,
        name=name,
        grid=(ni, nj, nk),
        in_specs=[a_spec, b_spec] + [s for _, s in extras],
        out_specs=[s for _, s in outs],
        out_shape=[s for s, _ in outs],
        scratch_shapes=[pltpu.VMEM((tm, tn), F32)],
        compiler_params=_params(3),
    )(a, b, *[x for x, _ in extras])


def _store(dtype):
    def ep(acc_ref, ex, o, i, j):
        o[0][...] = acc_ref[...].astype(dtype)
    return ep


def _to_bf16(v):
    return v.astype(BF16)


def _row_chunks(tm):
    rc = 64 if tm % 64 == 0 else tm
    return rc, tm // rc


def _ln_bwd(dy, xhat, rstd, g):
    dxh = dy * g
    m1 = jnp.mean(dxh, axis=-1, keepdims=True)
    m2 = jnp.mean(dxh * xhat, axis=-1, keepdims=True)
    return rstd * (dxh - m1 - xhat * m2)


def _ln_fwd(h):
    mu = jnp.mean(h, axis=-1, keepdims=True)
    xc = h - mu
    var = jnp.mean(xc * xc, axis=-1, keepdims=True)
    rstd = lax.rsqrt(var + LN_EPS)
    return xc * rstd, rstd


def _proj_fwd(x, w_full):
    T, D = x.shape
    P = w_full.shape[1]
    tm, tn = _tile(T, 512), _tile(P, 1280)
    return _mm("proj_fwd", "nn", x, w_full, M=T, N=P, K=D, tm=tm, tn=tn, tk=D,
               outs=[(jax.ShapeDtypeStruct((T, P), F32), pl.BlockSpec((tm, tn), lambda i, j, k: (i, j)))],
               epilogue=_store(F32), a_fn=_to_bf16)[0]


def _conv_shift(h, hp):
    row = lax.broadcasted_iota(jnp.int32, h.shape, 0)
    hm1 = hp[7:8, :]
    hm2 = hp[6:7, :]
    h1 = jnp.where(row == 0, hm1, pltpu.roll(h, 1, 0))
    h2 = jnp.where(row == 0, hm2, jnp.where(row == 1, hm1, pltpu.roll(h, 2, 0)))
    return h1, h2


def _conv_fwd(proj, conv_w8, conv_g):
    T = proj.shape[0]
    tt = _tile(T, 256)
    nt = T // tt
    t8 = tt // 8

    def body(b_ref, c_ref, u_ref, cp_ref, up_ref, w_ref, g_ref, yin_ref):
        i = pl.program_id(0)
        h = c_ref[...] * u_ref[...]
        hp = jnp.where(i > 0, cp_ref[...] * up_ref[...], 0.0)
        h1, h2 = _conv_shift(h, hp)
        w = w_ref[...]
        y = w[0:1, :] * h2 + w[1:2, :] * h1 + w[2:3, :] * h
        p = b_ref[...] * y
        parts = []
        for gi in range(CONV_GROUPS):
            pg = p[:, gi * LANE:(gi + 1) * LANE]
            r = lax.rsqrt(jnp.mean(pg * pg, axis=-1, keepdims=True) + RMS_EPS)
            parts.append(pg * r)
        yn = jnp.concatenate(parts, axis=1) * g_ref[...]
        yin_ref[...] = yn.astype(BF16)

    def col(cidx):
        return pl.BlockSpec((tt, D_CONV), lambda i: (i, cidx))

    def prev(cidx):
        return pl.BlockSpec((8, D_CONV), lambda i: (jnp.maximum(i * t8 - 1, 0), cidx))

    return pl.pallas_call(
        body,
        name="conv_fwd",
        grid=(nt,),
        in_specs=[col(0), col(1), col(2), prev(1), prev(2), _full((8, D_CONV)), _full((1, D_CONV))],
        out_specs=pl.BlockSpec((tt, D_CONV), lambda i: (i, 0)),
        out_shape=jax.ShapeDtypeStruct((T, 2 * D_CONV), BF16),
        compiler_params=_params(1),
    )(proj, proj, proj, proj, proj, conv_w8, conv_g)


def _log_sigmoid(z):
    return jnp.minimum(z, 0.0) - jnp.log(1.0 + jnp.exp(-jnp.abs(z)))


def _gla_chunk_terms(blk, wg_ref, gb_ref):
    q = blk[:, 0:512]
    k = blk[:, 512:1024]
    zl = blk[:, 3072:3200]
    z = _dot(zl.astype(BF16), wg_ref[...], NN) + gb_ref[...]
    log_a = _log_sigmoid(z) * (1.0 / GATE_TAU)
    ri = lax.broadcasted_iota(jnp.int32, (CHUNK, CHUNK), 0)
    ci = lax.broadcasted_iota(jnp.int32, (CHUNK, CHUNK), 1)
    causal = ri >= ci
    lower = causal.astype(F32)
    bcum = _dot(lower, log_a, NN, precision=lax.Precision.HIGHEST)
    return q, k, zl, z, bcum, causal


def _gla_head_terms(q, k, bcum, h):
    sl = slice(h * HEAD_K, (h + 1) * HEAD_K)
    bh = bcum[:, sl]
    bl = bh[CHUNK - 1:CHUNK, :]
    eb = jnp.exp(bh)
    enb = jnp.exp(-bh)
    eend = jnp.exp(bl - bh)
    dec = jnp.exp(bl)
    qd = q[:, sl] * (HEAD_K ** -0.5) * eb
    ki = k[:, sl] * enb
    ke = k[:, sl] * eend
    return eb, enb, eend, dec, qd, ki, ke


def _sigmoid(x):
    return 1.0 / (1.0 + jnp.exp(-x))


def _gla_fwd(proj, wg128, gbias, gng, yin):
    T = proj.shape[0]
    nch = T // CHUNK

    def body(p_ref, wg_ref, gb_ref, gn_ref, yin_in_ref, o_ref, st_ref, yin_ref, s_ref):
        n = pl.program_id(0)

        @pl.when(n == 0)
        def _():
            s_ref[...] = jnp.zeros_like(s_ref)

        blk = p_ref[...]
        q, k, _, _, bcum, causal = _gla_chunk_terms(blk, wg_ref, gb_ref)
        v = blk[:, 1024:2048]
        r = blk[:, 2048:3072]
        gn = gn_ref[...]
        for h in range(GLA_HEADS):
            _, _, _, dec, qd, ki, ke = _gla_head_terms(q, k, bcum, h)
            vs = slice(h * HEAD_V, (h + 1) * HEAD_V)
            vb = v[:, vs].astype(BF16)
            qdb = qd.astype(BF16)
            a = jnp.where(causal, _dot(qdb, ki.astype(BF16), NT), 0.0)
            st = s_ref[h]
            o = _dot(a.astype(BF16), vb, NN) + _dot(qdb, st.astype(BF16), NT)
            st_ref[h] = st
            s_ref[h] = dec * st + _dot(vb, ke.astype(BF16), TN)
            o_ref[:, vs] = o
            rinv = lax.rsqrt(jnp.mean(o * o, axis=-1, keepdims=True) + RMS_EPS)
            rh = r[:, vs]
            yin_ref[:, vs] = (o * rinv * gn[:, vs] * (rh * _sigmoid(rh))).astype(BF16)

    return pl.pallas_call(
        body,
        name="gla_fwd",
        grid=(nch,),
        in_specs=[pl.BlockSpec((CHUNK, HALF_P), lambda n: (n, 1)), _full((LANE, D_GLA_K)), _full((1, D_GLA_K)),
                  _full((1, D_GLA_V)), pl.BlockSpec(memory_space=pl.ANY)],
        out_specs=[pl.BlockSpec((CHUNK, D_GLA_V), lambda n: (n, 0)),
                   pl.BlockSpec((None, GLA_HEADS, HEAD_V, HEAD_K), lambda n: (n, 0, 0, 0)),
                   pl.BlockSpec((CHUNK, D_GLA_V), lambda n: (n, 1))],
        out_shape=[jax.ShapeDtypeStruct((T, D_GLA_V), F32),
                   jax.ShapeDtypeStruct((nch, GLA_HEADS, HEAD_V, HEAD_K), F32),
                   jax.ShapeDtypeStruct(yin.shape, BF16)],
        scratch_shapes=[pltpu.VMEM((GLA_HEADS, HEAD_V, HEAD_K), F32)],
        input_output_aliases={4: 2},
        compiler_params=_params(1),
    )(proj, wg128, gbias, gng, yin)


def _mix_ln1(yin, w_out, x, ln_g, ln_b):
    T, D = x.shape
    KY = yin.shape[1]
    tm = _tile(T, 512)
    rc, nrc = _row_chunks(tm)

    def ep(acc_ref, ex, o, i, j):
        x_ref, g_ref, b_ref = ex
        xhat_ref, x1_ref, rstd_ref = o
        g, b = g_ref[...], b_ref[...]

        def chunk(r, carry):
            rows = pl.ds(pl.multiple_of(r * rc, rc), rc)
            xhat, rstd = _ln_fwd(DN_ALPHA * x_ref[rows, :] + acc_ref[rows, :])
            xhat_ref[rows, :] = xhat
            x1_ref[rows, :] = (xhat * g + b).astype(BF16)
            rstd_ref[rows, :] = rstd
            return carry

        lax.fori_loop(0, nrc, chunk, 0)

    row = lambda i, j, k: (i, 0)
    vec = _full((1, D))
    return _mm("mix_ln1", "nn", yin, w_out, M=T, N=D, K=KY, tm=tm, tn=D, tk=_tile(KY, 1024),
               outs=[(jax.ShapeDtypeStruct((T, D), F32), pl.BlockSpec((tm, D), row)),
                     (jax.ShapeDtypeStruct((T, D), BF16), pl.BlockSpec((tm, D), row)),
                     (jax.ShapeDtypeStruct((T, 1), F32), pl.BlockSpec((tm, 1), row))],
               extras=[(x, pl.BlockSpec((tm, D), row)), (ln_g, vec), (ln_b, vec)],
               epilogue=ep)


def _ff_up(x1, w_up_blk):
    T, D = x1.shape
    nb, _, fb = w_up_blk.shape
    tm = _tile(T, 1024)

    def ep(acc_ref, ex, o, i, j):
        ra = jnp.maximum(acc_ref[...], 0.0)
        o[0][...] = ra.astype(BF16)
        o[1][...] = (ra * ra).astype(BF16)

    blk = pl.BlockSpec((tm, fb), lambda i, j, k: (i, j))
    shp = jax.ShapeDtypeStruct((T, nb * fb), BF16)
    return _mm("ff_up", "nn", x1, w_up_blk, M=T, N=nb * fb, K=D, tm=tm, tn=fb, tk=D,
               b_spec=pl.BlockSpec((None, D, fb), lambda i, j, k: (j, 0, 0)),
               outs=[(shp, blk), (shp, blk)], epilogue=ep)


def _ff_down_loss(h2, w_down, xhat1, target, g1, b1, g2, b2):
    T, F = h2.shape
    D = w_down.shape[1]
    tm = _tile(T, 512)
    rc, nrc = _row_chunks(tm)
    inv_d = 1.0 / D

    def ep(acc_ref, ex, o, i, j):
        xh1_ref, t_ref, g1_ref, b1_ref, g2_ref, b2_ref = ex
        dh_ref, dhb_ref, gg_ref, gb_ref, loss_ref = o
        g1v, b1v, g2v, b2v = g1_ref[...], b1_ref[...], g2_ref[...], b2_ref[...]

        @pl.when(i == 0)
        def _():
            gg_ref[...] = jnp.zeros_like(gg_ref)
            gb_ref[...] = jnp.zeros_like(gb_ref)
            loss_ref[...] = jnp.zeros_like(loss_ref)

        def chunk(r, carry):
            sg, sb, sl = carry
            rows = pl.ds(pl.multiple_of(r * rc, rc), rc)
            x1 = xh1_ref[rows, :] * g1v + b1v
            xhat, rstd = _ln_fwd(DN_ALPHA * x1 + acc_ref[rows, :])
            e = xhat * g2v + b2v - t_ref[rows, :]
            dy = e * inv_d
            dh = _ln_bwd(dy, xhat, rstd, g2v)
            dh_ref[rows, :] = dh
            dhb_ref[rows, :] = dh.astype(BF16)
            sg = sg + jnp.sum(dy * xhat, axis=0, keepdims=True)
            sb = sb + jnp.sum(dy, axis=0, keepdims=True)
            sl = sl + jnp.sum(e * e, axis=0, keepdims=True)
            return sg, sb, sl

        z = jnp.zeros((1, D), F32)
        sg, sb, sl = lax.fori_loop(0, nrc, chunk, (z, z, z))
        gg_ref[...] += sg
        gb_ref[...] += sb
        loss_ref[...] += sl * (0.5 * inv_d)

    row = lambda i, j, k: (i, 0)
    vec = _full((1, D))
    vshape = jax.ShapeDtypeStruct((1, D), F32)
    return _mm("ff_down_loss", "nn", h2, w_down, M=T, N=D, K=F, tm=tm, tn=D, tk=_tile(F, 1024),
               outs=[(jax.ShapeDtypeStruct((T, D), F32), pl.BlockSpec((tm, D), row)),
                     (jax.ShapeDtypeStruct((T, D), BF16), pl.BlockSpec((tm, D), row)),
                     (vshape, vec), (vshape, vec), (vshape, vec)],
               extras=[(xhat1, pl.BlockSpec((tm, D), row)), (target, pl.BlockSpec((tm, D), row)),
                       (g1, vec), (b1, vec), (g2, vec), (b2, vec)],
               epilogue=ep)


def _ff_down_bwd_act(dh3b, w_down, ra):
    T, D = dh3b.shape
    F = w_down.shape[0]
    tm, tn = _tile(T, 1024), _tile(F, 1024)

    def ep(acc_ref, ex, o, i, j):
        o[0][...] = (acc_ref[...] * (2.0 * ex[0][...].astype(F32))).astype(BF16)

    blk = pl.BlockSpec((tm, tn), lambda i, j, k: (i, j))
    return _mm("ff_down_bwd_act", "nt", dh3b, w_down, M=T, N=F, K=D, tm=tm, tn=tn, tk=D,
               outs=[(jax.ShapeDtypeStruct((T, F), BF16), blk)], extras=[(ra, blk)], epilogue=ep)[0]


def _grad_w(name, a, b, *, a_fn=None, tm_pref=1024, tn_pref=1024):
    T, M = a.shape
    N = b.shape[1]
    tm, tn, tk = _tile(M, tm_pref), _tile(N, tn_pref), _tile(T, 1024)
    return _mm(name, "tn", a, b, M=M, N=N, K=T, tm=tm, tn=tn, tk=tk, a_fn=a_fn,
               outs=[(jax.ShapeDtypeStruct((M, N), F32), pl.BlockSpec((tm, tn), lambda i, j, k: (i, j)))],
               epilogue=_store(F32))[0]


def _grad_w_up_blk(x1, da, nb):
    T, D = x1.shape
    F = da.shape[1]
    fb = F // nb
    tm, tk = _tile(D, 1024), _tile(T, 1024)
    return _mm("grad_w_up", "tn", x1, da, M=D, N=F, K=T, tm=tm, tn=fb, tk=tk,
               outs=[(jax.ShapeDtypeStruct((nb, D, fb), F32),
                      pl.BlockSpec((None, tm, fb), lambda i, j, k: (j, i, 0)))],
               epilogue=_store(F32))[0]


def _ff_up_bwd_ln1(da, w_up_blk, dh3, xhat1, rstd1, g1):
    T, F = da.shape
    nb, D, fb = w_up_blk.shape
    tm = _tile(T, 512)
    rc, nrc = _row_chunks(tm)

    def ep(acc_ref, ex, o, i, j):
        dh3_ref, xh_ref, rstd_ref, g_ref = ex
        dh_ref, dhb_ref, gg_ref, gb_ref = o
        g = g_ref[...]

        @pl.when(i == 0)
        def _():
            gg_ref[...] = jnp.zeros_like(gg_ref)
            gb_ref[...] = jnp.zeros_like(gb_ref)

        def chunk(r, carry):
            sg, sb = carry
            rows = pl.ds(pl.multiple_of(r * rc, rc), rc)
            dx1 = DN_ALPHA * dh3_ref[rows, :] + acc_ref[rows, :]
            xhat = xh_ref[rows, :]
            dh = _ln_bwd(dx1, xhat, rstd_ref[rows, :], g)
            dh_ref[rows, :] = dh
            dhb_ref[rows, :] = dh.astype(BF16)
            return sg + jnp.sum(dx1 * xhat, axis=0, keepdims=True), sb + jnp.sum(dx1, axis=0, keepdims=True)

        z = jnp.zeros((1, D), F32)
        sg, sb = lax.fori_loop(0, nrc, chunk, (z, z))
        gg_ref[...] += sg
        gb_ref[...] += sb

    row = lambda i, j, k: (i, 0)
    vec = _full((1, D))
    vshape = jax.ShapeDtypeStruct((1, D), F32)
    return _mm("ff_up_bwd_ln1", "nt", da, w_up_blk, M=T, N=D, K=F, tm=tm, tn=D, tk=fb,
               b_spec=pl.BlockSpec((None, D, fb), lambda i, j, k: (k, 0, 0)),
               outs=[(jax.ShapeDtypeStruct((T, D), F32), pl.BlockSpec((tm, D), row)),
                     (jax.ShapeDtypeStruct((T, D), BF16), pl.BlockSpec((tm, D), row)),
                     (vshape, vec), (vshape, vec)],
               extras=[(dh3, pl.BlockSpec((tm, D), row)), (xhat1, pl.BlockSpec((tm, D), row)),
                       (rstd1, pl.BlockSpec((tm, 1), row)), (g1, vec)],
               epilogue=ep)


def _mix_bwd(dh1b, w_out):
    T, D = dh1b.shape
    KY = w_out.shape[0]
    tm, tn = _tile(T, 1024), _tile(KY, 1024)
    return _mm("mix_bwd", "nt", dh1b, w_out, M=T, N=KY, K=D, tm=tm, tn=tn, tk=D,
               outs=[(jax.ShapeDtypeStruct((T, KY), F32), pl.BlockSpec((tm, tn), lambda i, j, k: (i, j)))],
               epilogue=_store(F32))[0]


def _conv_bwd(proj, dyin, conv_w8, conv_g):
    T = proj.shape[0]
    tt = _tile(T, 256)
    nt = T // tt
    t8 = tt // 8
    nx = tt + 8

    def body(b_ref, c_ref, u_ref, d_ref, bn_ref, cn_ref, un_ref, dn_ref, cp_ref, up_ref, w_ref, g_ref,
             dp_ref, dw_ref, dg_ref):
        i = pl.program_id(0)

        @pl.when(i == 0)
        def _():
            dw_ref[...] = jnp.zeros_like(dw_ref)
            dg_ref[...] = jnp.zeros_like(dg_ref)

        more = i < nt - 1

        def ext(cur_ref, nxt_ref):
            return jnp.concatenate([cur_ref[...], jnp.where(more, nxt_ref[...], 0.0)], axis=0)

        bx, cx, ux, dx = ext(b_ref, bn_ref), ext(c_ref, cn_ref), ext(u_ref, un_ref), ext(d_ref, dn_ref)
        hx = cx * ux
        hp = jnp.where(i > 0, cp_ref[...] * up_ref[...], 0.0)
        h1, h2 = _conv_shift(hx, hp)
        w = w_ref[...]
        g = g_ref[...]
        yx = w[0:1, :] * h2 + w[1:2, :] * h1 + w[2:3, :] * hx
        px = bx * yx
        dps, dgs = [], []
        for gi in range(CONV_GROUPS):
            sl = slice(gi * LANE, (gi + 1) * LANE)
            pg, dg_ = px[:, sl], dx[:, sl]
            r = lax.rsqrt(jnp.mean(pg * pg, axis=-1, keepdims=True) + RMS_EPS)
            gd = g[:, sl] * dg_
            dps.append(r * gd - pg * (r * r * r) * jnp.mean(pg * gd, axis=-1, keepdims=True))
            dgs.append(jnp.sum((dg_ * pg * r)[:tt, :], axis=0, keepdims=True))
        dpx = jnp.concatenate(dps, axis=1)
        dg_ref[...] += jnp.concatenate(dgs, axis=1)
        dyx = dpx * bx
        dyc = dyx[:tt, :]
        dh = (w[2:3, :] * dyx + w[1:2, :] * pltpu.roll(dyx, nx - 1, 0) + w[0:1, :] * pltpu.roll(dyx, nx - 2, 0))[:tt, :]
        dw_ref[0:1, :] += jnp.sum(dyc * h2[:tt, :], axis=0, keepdims=True)
        dw_ref[1:2, :] += jnp.sum(dyc * h1[:tt, :], axis=0, keepdims=True)
        dw_ref[2:3, :] += jnp.sum(dyc * hx[:tt, :], axis=0, keepdims=True)
        dp_ref[:, 0:D_CONV] = (dpx * yx)[:tt, :].astype(BF16)
        dp_ref[:, D_CONV:2 * D_CONV] = (dh * u_ref[...]).astype(BF16)
        dp_ref[:, 2 * D_CONV:3 * D_CONV] = (dh * c_ref[...]).astype(BF16)
        dp_ref[:, 3 * D_CONV:HALF_P] = jnp.zeros((tt, HALF_P - 3 * D_CONV), BF16)

    def col(cidx):
        return pl.BlockSpec((tt, D_CONV), lambda i: (i, cidx))

    def nxt(cidx):
        return pl.BlockSpec((8, D_CONV), lambda i: (jnp.minimum((i + 1) * t8, T // 8 - 1), cidx))

    def prev(cidx):
        return pl.BlockSpec((8, D_CONV), lambda i: (jnp.maximum(i * t8 - 1, 0), cidx))

    return pl.pallas_call(
        body,
        name="conv_bwd",
        grid=(nt,),
        in_specs=[col(0), col(1), col(2), col(0), nxt(0), nxt(1), nxt(2), nxt(0), prev(1), prev(2),
                  _full((8, D_CONV)), _full((1, D_CONV))],
        out_specs=[pl.BlockSpec((tt, HALF_P), lambda i: (i, 0)), _full((8, D_CONV)), _full((1, D_CONV))],
        out_shape=[jax.ShapeDtypeStruct((T, P_INT), BF16), jax.ShapeDtypeStruct((8, D_CONV), F32),
                   jax.ShapeDtypeStruct((1, D_CONV), F32)],
        compiler_params=_params(1),
    )(proj, proj, proj, dyin, proj, proj, proj, dyin, proj, proj, conv_w8, conv_g)


def _gla_bwd(proj, wg128, gbias, gng, o_all, states, dyin, dproj):
    T = proj.shape[0]
    nch = T // CHUNK

    def body(p_ref, wg_ref, gb_ref, gn_ref, o_ref, st_ref, d_ref, dp_in_ref,
             dp_ref, dwg_ref, dgb_ref, dgn_ref, ds_ref):
        n = pl.program_id(0)

        @pl.when(n == 0)
        def _():
            ds_ref[...] = jnp.zeros_like(ds_ref)
            dwg_ref[...] = jnp.zeros_like(dwg_ref)
            dgb_ref[...] = jnp.zeros_like(dgb_ref)
            dgn_ref[...] = jnp.zeros_like(dgn_ref)

        blk = p_ref[...]
        q, k, zl, z, bcum, causal = _gla_chunk_terms(blk, wg_ref, gb_ref)
        v = blk[:, 1024:2048]
        r = blk[:, 2048:3072]
        gn = gn_ref[...]
        upper = (lax.broadcasted_iota(jnp.int32, (CHUNK, CHUNK), 0)
                 <= lax.broadcasted_iota(jnp.int32, (CHUNK, CHUNK), 1)).astype(F32)
        dlog_parts = []
        for h in range(GLA_HEADS):
            eb, enb, eend, dec, qd, ki, ke = _gla_head_terms(q, k, bcum, h)
            vs = slice(h * HEAD_V, (h + 1) * HEAD_V)
            ks = slice(h * HEAD_K, (h + 1) * HEAD_K)
            o = o_ref[:, vs]
            rh = r[:, vs]
            dyg = d_ref[:, vs]
            rinv = lax.rsqrt(jnp.mean(o * o, axis=-1, keepdims=True) + RMS_EPS)
            sg = _sigmoid(rh)
            on = o * rinv
            dr = dyg * (on * gn[:, vs]) * (sg * (1.0 + rh * (1.0 - sg)))
            don = dyg * (rh * sg)
            dgn_ref[:, vs] += jnp.sum(don * on, axis=0, keepdims=True)
            t = don * gn[:, vs]
            do = rinv * t - o * (rinv * rinv * rinv) * jnp.mean(o * t, axis=-1, keepdims=True)
            dob = do.astype(BF16)
            vb = v[:, vs].astype(BF16)
            qdb, kib, keb = qd.astype(BF16), ki.astype(BF16), ke.astype(BF16)
            a = jnp.where(causal, _dot(qdb, kib, NT), 0.0)
            st = st_ref[h]
            dst = ds_ref[h]
            dstb = dst.astype(BF16)
            da = jnp.where(causal, _dot(dob, vb, NT), 0.0)
            dab = da.astype(BF16)
            dv = _dot(a.astype(BF16), dob, TN) + _dot(keb, dstb, NT)
            dqd = _dot(dab, kib, NN) + _dot(dob, st.astype(BF16), NN)
            dki = _dot(dab, qdb, TN)
            dke = _dot(vb, dstb, NN)
            ddec = jnp.sum(st * dst, axis=0, keepdims=True)
            ds_ref[h] = dec * dst + _dot(dob, qdb, TN)
            dq = dqd * eb * (HEAD_K ** -0.5)
            dk = dki * enb + dke * eend
            db = dqd * qd - dki * ki - dke * ke
            dbl = jnp.sum(dke * ke, axis=0, keepdims=True) + dec * ddec
            dlog_parts.append(_dot(upper, db, NN, precision=lax.Precision.HIGHEST) + dbl)
            dp_ref[:, ks] = dq.astype(BF16)
            dp_ref[:, D_GLA_K + h * HEAD_K:D_GLA_K + (h + 1) * HEAD_K] = dk.astype(BF16)
            dp_ref[:, 1024 + h * HEAD_V:1024 + (h + 1) * HEAD_V] = dv.astype(BF16)
            dp_ref[:, 2048 + h * HEAD_V:2048 + (h + 1) * HEAD_V] = dr.astype(BF16)
        dlog = jnp.concatenate(dlog_parts, axis=1)
        dz = dlog * (1.0 / GATE_TAU) * (1.0 / (1.0 + jnp.exp(z)))
        dzb = dz.astype(BF16)
        dp_ref[:, 3072:3200] = _dot(dzb, wg_ref[...], NT).astype(BF16)
        dwg_ref[...] += _dot(zl.astype(BF16), dzb, TN)
        dgb_ref[...] += jnp.sum(dz, axis=0, keepdims=True)

    rev = lambda n: nch - 1 - n
    return pl.pallas_call(
        body,
        name="gla_bwd",
        grid=(nch,),
        in_specs=[pl.BlockSpec((CHUNK, HALF_P), lambda n: (rev(n), 1)), _full((LANE, D_GLA_K)), _full((1, D_GLA_K)),
                  _full((1, D_GLA_V)), pl.BlockSpec((CHUNK, D_GLA_V), lambda n: (rev(n), 0)),
                  pl.BlockSpec((None, GLA_HEADS, HEAD_V, HEAD_K), lambda n: (rev(n), 0, 0, 0)),
                  pl.BlockSpec((CHUNK, D_GLA_V), lambda n: (rev(n), 1)), pl.BlockSpec(memory_space=pl.ANY)],
        out_specs=[pl.BlockSpec((CHUNK, HALF_P), lambda n: (rev(n), 1)), _full((LANE, D_GLA_K)),
                   _full((1, D_GLA_K)), _full((1, D_GLA_V))],
        out_shape=[jax.ShapeDtypeStruct(dproj.shape, BF16), jax.ShapeDtypeStruct((LANE, D_GLA_K), F32),
                   jax.ShapeDtypeStruct((1, D_GLA_K), F32), jax.ShapeDtypeStruct((1, D_GLA_V), F32)],
        scratch_shapes=[pltpu.VMEM((GLA_HEADS, HEAD_V, HEAD_K), F32)],
        input_output_aliases={7: 0},
        compiler_params=_params(1),
    )(proj, wg128, gbias, gng, o_all, states, dyin, dproj)


def _proj_bwd_x(dproj, w_full, dh1):
    T, P = dproj.shape
    D = w_full.shape[0]
    tm, tk = _tile(T, 512), _tile(P, 1280)

    def ep(acc_ref, ex, o, i, j):
        o[0][...] = DN_ALPHA * ex[0][...] + acc_ref[...]

    row = pl.BlockSpec((tm, D), lambda i, j, k: (i, 0))
    return _mm("proj_bwd_x", "nt", dproj, w_full, M=T, N=D, K=P, tm=tm, tn=D, tk=tk,
               outs=[(jax.ShapeDtypeStruct((T, D), F32), row)], extras=[(dh1, row)], epilogue=ep)[0]


def _place():
    x, y, c = lax.axis_index("x"), lax.axis_index("y"), lax.axis_index("c")
    chips = [(1 - x, y), (x, 1 - y), (1 - x, 1 - y)]
    return x, y, c, chips


def _rcopy(src, dst, ssem, rsem, dev):
    return pltpu.make_async_remote_copy(src_ref=src, dst_ref=dst, send_sem=ssem, recv_sem=rsem,
                                        device_id=dev, device_id_type=MESH)


def _hbm_specs(n):
    return [pl.BlockSpec(memory_space=pl.ANY)] * n


def _all_gather(shards):
    n = len(shards)

    def body(*refs):
        ins, outs = refs[:n], refs[n:2 * n]
        ssem, rsem, lsem = refs[2 * n:]
        x, y, c, chips = _place()
        me, sib = (x, y, c), (x, y, 1 - c)

        def slot(w, px, py, pc):
            return outs[w].at[4 * px + 2 * py + pc]

        started = []
        for w in range(n):
            lc = pltpu.make_async_copy(ins[w], slot(w, *me), lsem.at[w])
            lc.start()
            started.append(lc)
        sends = []
        for w in range(n):
            cp = _rcopy(ins[w], slot(w, *me), ssem.at[7 * w], rsem.at[7 * w], sib)
            cp.start()
            sends.append(cp)
            for jx, chip in enumerate(chips):
                cp = _rcopy(ins[w], slot(w, *me), ssem.at[7 * w + 1 + jx], rsem.at[7 * w + 1 + jx], (*chip, c))
                cp.start()
                sends.append(cp)
        for w in range(n):
            for jx, chip in enumerate(chips):
                blk = slot(w, *chip, c)
                _rcopy(blk, blk, ssem.at[7 * w + 1 + jx], rsem.at[7 * w + 1 + jx], me).wait_recv()
                cp = _rcopy(blk, blk, ssem.at[7 * w + 4 + jx], rsem.at[7 * w + 4 + jx], sib)
                cp.start()
                sends.append(cp)
        for w in range(n):
            blk = slot(w, x, y, 1 - c)
            _rcopy(blk, blk, ssem.at[7 * w], rsem.at[7 * w], me).wait_recv()
            for jx, chip in enumerate(chips):
                blk = slot(w, *chip, 1 - c)
                _rcopy(blk, blk, ssem.at[7 * w + 4 + jx], rsem.at[7 * w + 4 + jx], me).wait_recv()
        for cp in sends:
            cp.wait_send()
        for lc in started:
            lc.wait()

    return pl.pallas_call(
        body,
        name="all_gather_weights",
        in_specs=_hbm_specs(n),
        out_specs=_hbm_specs(n),
        out_shape=[jax.ShapeDtypeStruct((N_DEV,) + s.shape, s.dtype) for s in shards],
        scratch_shapes=[pltpu.SemaphoreType.DMA((7 * n,)), pltpu.SemaphoreType.DMA((7 * n,)),
                        pltpu.SemaphoreType.DMA((n,))],
    )(*shards)


def _exchange_core(grads):
    n = len(grads)

    def body(*refs):
        ins, outs = refs[:n], refs[n:2 * n]
        ssem, rsem = refs[2 * n:]
        x, y, c, _ = _place()
        sib = (x, y, 1 - c)
        cps = []
        for w in range(n):
            for q in range(N_CHIP):
                src = ins[w].at[2 * q + (1 - c)]
                cp = _rcopy(src, outs[w].at[q], ssem.at[4 * w + q], rsem.at[4 * w + q], sib)
                cp.start()
                cps.append(cp)
        for cp in cps:
            cp.wait()

    return pl.pallas_call(
        body,
        name="exchange_core",
        in_specs=_hbm_specs(n),
        out_specs=_hbm_specs(n),
        out_shape=[jax.ShapeDtypeStruct((N_CHIP,) + g.shape[1:], g.dtype) for g in grads],
        scratch_shapes=[pltpu.SemaphoreType.DMA((4 * n,)), pltpu.SemaphoreType.DMA((4 * n,))],
    )(*grads)


def _chip_sums(ids, grad, recv):
    _, R, C = grad.shape
    tr = _tile(R, 256)

    def body(ids_ref, g_ref, r_ref, o_ref):
        o_ref[...] = (g_ref[...] + r_ref[...]).astype(BF16)

    return pl.pallas_call(
        body,
        name="chip_sums",
        grid_spec=pltpu.PrefetchScalarGridSpec(
            num_scalar_prefetch=1,
            grid=(N_CHIP, R // tr),
            in_specs=[pl.BlockSpec((None, tr, C), lambda q, r, ids: (2 * q + ids[2], r, 0)),
                      pl.BlockSpec((None, tr, C), lambda q, r, ids: (q, r, 0))],
            out_specs=pl.BlockSpec((None, tr, C), lambda q, r, ids: (q, r, 0)),
        ),
        out_shape=jax.ShapeDtypeStruct((N_CHIP, R, C), BF16),
        compiler_params=_params(2),
    )(ids, grad, recv)


def _exchange_chips(partials, pack):
    n = len(partials)

    def body(*refs):
        ins, pk = refs[:n], refs[n]
        outs, pks = refs[n + 1:2 * n + 1], refs[2 * n + 1]
        ssem, rsem, psend, precv, lsem = refs[2 * n + 2:]
        x, y, c, chips = _place()
        me_slot = 4 * x + 2 * y + c
        lc = pltpu.make_async_copy(pk, pks.at[me_slot], lsem)
        lc.start()
        peers = []
        for m in range(1, N_DEV):
            dx, dy, dc = (m >> 2) & 1, (m >> 1) & 1, m & 1
            peers.append(((1 - x) if dx else x, (1 - y) if dy else y, (1 - c) if dc else c))
        cps = []
        for m, peer in enumerate(peers):
            cp = _rcopy(pk, pks.at[me_slot], psend.at[m], precv.at[m], peer)
            cp.start()
            cps.append(cp)
        for w in range(n):
            for jx, (px, py) in enumerate(chips):
                cp = _rcopy(ins[w].at[2 * px + py], outs[w].at[2 * x + y], ssem.at[3 * w + jx], rsem.at[3 * w + jx],
                            (px, py, c))
                cp.start()
                cps.append(cp)
        for m, (px, py, pc) in enumerate(peers):
            blk = pks.at[4 * px + 2 * py + pc]
            _rcopy(blk, blk, psend.at[m], precv.at[m], (px, py, pc)).wait_recv()
        for w in range(n):
            for jx, (px, py) in enumerate(chips):
                blk = outs[w].at[2 * px + py]
                _rcopy(blk, blk, ssem.at[3 * w + jx], rsem.at[3 * w + jx], (px, py, c)).wait_recv()
        for cp in cps:
            cp.wait_send()
        lc.wait()

    return pl.pallas_call(
        body,
        name="exchange_chips",
        in_specs=_hbm_specs(n + 1),
        out_specs=_hbm_specs(n + 1),
        out_shape=[jax.ShapeDtypeStruct(p.shape, p.dtype) for p in partials]
        + [jax.ShapeDtypeStruct((N_DEV,) + pack.shape, pack.dtype)],
        scratch_shapes=[pltpu.SemaphoreType.DMA((3 * n,)), pltpu.SemaphoreType.DMA((3 * n,)),
                        pltpu.SemaphoreType.DMA((N_DEV - 1,)), pltpu.SemaphoreType.DMA((N_DEV - 1,)),
                        pltpu.SemaphoreType.DMA],
    )(*partials, pack)


def _adamw(w, g, m, v):
    m = ADAM_B1 * m + (1.0 - ADAM_B1) * g
    v = ADAM_B2 * v + (1.0 - ADAM_B2) * (g * g)
    m_hat = m / (1.0 - ADAM_B1 ** ADAM_STEP)
    v_hat = v / (1.0 - ADAM_B2 ** ADAM_STEP)
    delta = -ADAM_LR * (m_hat / (jnp.sqrt(v_hat) + ADAM_EPS) + ADAM_WD * w)
    return delta, m, v


def _reduce_adamw(name, ids, grad, recv, landed, w, m, v):
    _, R, C = grad.shape
    tr = _tile(R, 256)

    def body(ids_ref, g_ref, r_ref, l1_ref, l2_ref, l3_ref, w_ref, m_ref, v_ref, go_ref, do_ref, mo_ref, vo_ref):
        g = g_ref[...] + r_ref[...]
        g = g + l1_ref[...].astype(F32)
        g = g + l2_ref[...].astype(F32)
        g = g + l3_ref[...].astype(F32)
        delta, mn, vn = _adamw(w_ref[...], g, m_ref[...], v_ref[...])
        go_ref[...] = g
        do_ref[...] = delta
        mo_ref[...] = mn
        vo_ref[...] = vn

    def pick(k):
        return pl.BlockSpec((None, tr, C), lambda r, ids: (ids[k], r, 0))

    flat = pl.BlockSpec((tr, C), lambda r, ids: (r, 0))
    shp = jax.ShapeDtypeStruct((R, C), F32)
    return pl.pallas_call(
        body,
        name=name,
        grid_spec=pltpu.PrefetchScalarGridSpec(
            num_scalar_prefetch=1,
            grid=(R // tr,),
            in_specs=[pick(0), pick(1), pick(3), pick(4), pick(5), flat, flat, flat],
            out_specs=[flat, flat, flat, flat],
        ),
        out_shape=[shp, shp, shp, shp],
        compiler_params=_params(1),
    )(ids, grad, recv, landed, landed, landed, w, m, v)


def _small_adamw(packs, w, m, v):
    def body(p_ref, w_ref, m_ref, v_ref, g_ref, d_ref, mo_ref, vo_ref):
        g = p_ref[0]
        for dvc in range(1, N_DEV):
            g = g + p_ref[dvc]
        delta, mn, vn = _adamw(w_ref[...], g, m_ref[...], v_ref[...])
        g_ref[...] = g
        d_ref[...] = delta
        mo_ref[...] = mn
        vo_ref[...] = vn

    shp = jax.ShapeDtypeStruct(w.shape, F32)
    return pl.pallas_call(
        body,
        name="small_adamw",
        in_specs=[_full(packs.shape), _full(w.shape), _full(w.shape), _full(w.shape)],
        out_specs=[_full(w.shape)] * 4,
        out_shape=[shp] * 4,
        grid=(1,),
        compiler_params=_params(1),
    )(packs, w, m, v)


def _w_in_full(gathered):
    nb, D, cs = gathered.shape
    w = gathered.transpose(1, 0, 2).reshape(D, nb * cs)
    return jnp.concatenate([w[:, :CONV_COLS], jnp.zeros((D, HALF_P - CONV_COLS), w.dtype),
                            w[:, CONV_COLS:], jnp.zeros((D, HALF_P - GLA_COLS), w.dtype)], axis=1)


def _w_in_blocks(dw):
    D = dw.shape[0]
    g = jnp.concatenate([dw[:, :CONV_COLS], dw[:, HALF_P:HALF_P + GLA_COLS]], axis=1)
    return g.reshape(D, N_DEV, D_IN_PROJ // N_DEV).transpose(1, 0, 2)


def _rows(vec, n_rows):
    flat = jnp.pad(vec.reshape(-1), (0, n_rows * SP_COLS - vec.size))
    return flat.reshape(n_rows, SP_COLS)


def _pad_cols(a):
    return jnp.pad(a, ((0, 0), (0, SP_COLS - a.shape[1])))


R_CONV_W, R_CONV_G, R_GATE_B, R_GLA_G, R_LN1_G, R_LN1_B, R_LN2_G, R_LN2_B, R_LOSS, R_GATE_W = 0, 3, 4, 5, 6, 8, 10, 12, 14, 16


def _pack(conv_w, conv_g, gate_b, gla_g, ln1_g, ln1_b, ln2_g, ln2_b, loss, gate_w):
    z = jnp.zeros((1, SP_COLS), F32)
    parts = [_pad_cols(conv_w), _pad_cols(conv_g), _pad_cols(gate_b), _pad_cols(gla_g),
             _rows(ln1_g, 2), _rows(ln1_b, 2), _rows(ln2_g, 2), _rows(ln2_b, 2),
             z if loss is None else _pad_cols(jnp.sum(loss, axis=1, keepdims=True)), z, _pad_cols(gate_w)]
    return jnp.concatenate(parts, axis=0)


def _unpack(p, D, conv_cols, gate_cols):
    return dict(
        conv_w=p[R_CONV_W:R_CONV_W + 3, :conv_cols], conv_norm_g=p[R_CONV_G:R_CONV_G + 1, :D_CONV],
        gate_bias=p[R_GATE_B:R_GATE_B + 1, :D_GLA_K], gla_norm_g=p[R_GLA_G:R_GLA_G + 1, :D_GLA_V],
        ln1_g=p[R_LN1_G:R_LN1_G + 2].reshape(1, -1)[:, :D], ln1_b=p[R_LN1_B:R_LN1_B + 2].reshape(1, -1)[:, :D],
        ln2_g=p[R_LN2_G:R_LN2_G + 2].reshape(1, -1)[:, :D], ln2_b=p[R_LN2_B:R_LN2_B + 2].reshape(1, -1)[:, :D],
        w_gate_up=p[R_GATE_W:R_GATE_W + GATE_RANK, :gate_cols])


def _local_step(x, target, w_full, conv_w, conv_g, gate_w, gate_b, gla_g, w_out, ln1_g, ln1_b, w_up_blk, w_down,
                ln2_g, ln2_b):
    nb = w_up_blk.shape[0]
    conv_w8 = jnp.pad(conv_w, ((0, 5), (0, 0)))
    wg128 = jnp.pad(gate_w, ((0, LANE - GATE_RANK), (0, 0))).astype(BF16)

    proj = _proj_fwd(x, w_full)
    yin = _conv_fwd(proj, conv_w8, conv_g)
    o_all, states, yin = _gla_fwd(proj, wg128, gate_b, gla_g, yin)
    xhat1, x1, rstd1 = _mix_ln1(yin, w_out, x, ln1_g, ln1_b)
    ra, h2 = _ff_up(x1, w_up_blk)
    dh3, dh3b, g_ln2_g, g_ln2_b, loss = _ff_down_loss(h2, w_down, xhat1, target, ln1_g, ln1_b, ln2_g, ln2_b)

    da = _ff_down_bwd_act(dh3b, w_down, ra)
    g_w_down = _grad_w("grad_w_down", h2, dh3b)
    g_w_up = _grad_w_up_blk(x1, da, nb)
    dh1, dh1b, g_ln1_g, g_ln1_b = _ff_up_bwd_ln1(da, w_up_blk, dh3, xhat1, rstd1, ln1_g)
    dyin = _mix_bwd(dh1b, w_out)
    g_w_out = _grad_w("grad_w_out", yin, dh1b)
    dproj, g_conv_w, g_conv_g = _conv_bwd(proj, dyin, conv_w8, conv_g)
    dproj, g_gate_w, g_gate_b, g_gla_g = _gla_bwd(proj, wg128, gate_b, gla_g, o_all, states, dyin, dproj)
    grad_x = _proj_bwd_x(dproj, w_full, dh1)
    g_w_in = _grad_w("grad_w_in", x, dproj, a_fn=_to_bf16, tn_pref=1280)

    pack = _pack(g_conv_w[:3], g_conv_g, g_gate_b, g_gla_g, g_ln1_g, g_ln1_b, g_ln2_g, g_ln2_b, loss,
                 g_gate_w[:GATE_RANK])
    return grad_x, g_w_in, g_w_out, g_w_up, g_w_down, pack


def kernel(x, w_in, conv_w, conv_norm_g, w_gate_up, gate_bias, gla_norm_g, w_out, ln1_g, ln1_b, w_ff_up, w_ff_down, ln2_g, ln2_b, loss_target, m_w_in, m_conv_w, m_conv_norm_g, m_w_gate_up, m_gate_bias, m_gla_norm_g, m_w_out, m_ln1_g, m_ln1_b, m_w_ff_up, m_w_ff_down, m_ln2_g, m_ln2_b, v_w_in, v_conv_w, v_conv_norm_g, v_w_gate_up, v_gate_bias, v_gla_norm_g, v_w_out, v_ln1_g, v_ln1_b, v_w_ff_up, v_w_ff_down, v_ln2_g, v_ln2_b):
    T, D = x.shape[1], x.shape[2]
    xi, yi, ci = lax.axis_index("x"), lax.axis_index("y"), lax.axis_index("c")
    chip = 2 * xi + yi
    dev = 2 * chip + ci
    others = [jnp.where(chip <= q, q + 1, q) for q in range(N_CHIP - 1)]
    ids = jnp.stack([dev, chip, ci] + others).astype(jnp.int32)

    conv_cols, gate_cols = conv_w.shape[2], w_gate_up.shape[2]
    z1 = jnp.zeros((1, 1), F32)
    fwd_pack = _pack(conv_w[0], z1, z1, z1, z1, z1, z1, z1, None, w_gate_up[0])
    g_in, g_out, g_up, g_down, g_pack = _all_gather(
        [w_in[0].astype(BF16), w_out[0].astype(BF16), w_ff_up[0].astype(BF16), w_ff_down[0].astype(BF16), fwd_pack])
    w_full = _w_in_full(g_in)
    w_out_full = g_out.reshape(-1, D)
    w_down_full = g_down.reshape(-1, D)
    conv_w_full = g_pack[:, R_CONV_W:R_CONV_W + 3, :conv_cols].transpose(1, 0, 2).reshape(3, -1)
    gate_w_full = g_pack[:, R_GATE_W:R_GATE_W + GATE_RANK, :gate_cols].transpose(1, 0, 2).reshape(GATE_RANK, -1)

    grad_x, gw_in, gw_out, gw_up, gw_down, pack = _local_step(
        x[0], loss_target[0], w_full, conv_w_full, conv_norm_g, gate_w_full, gate_bias, gla_norm_g, w_out_full,
        ln1_g, ln1_b, g_up, w_down_full, ln2_g, ln2_b)

    grads = [_w_in_blocks(gw_in), gw_out.reshape(N_DEV, -1, D), gw_up, gw_down.reshape(N_DEV, -1, D)]
    recvs = _exchange_core(grads)
    partials = [_chip_sums(ids, g, r) for g, r in zip(grads, recvs)]
    *landed, packs = _exchange_chips(partials, pack)

    big = []
    for name, g, r, l, w, m, v in zip(
            ("adamw_w_in", "adamw_w_out", "adamw_w_ff_up", "adamw_w_ff_down"), grads, recvs, landed,
            (w_in, w_out, w_ff_up, w_ff_down), (m_w_in, m_w_out, m_w_ff_up, m_w_ff_down),
            (v_w_in, v_w_out, v_w_ff_up, v_w_ff_down)):
        res = _reduce_adamw(name, ids, g, r, l, w[0], m[0], v[0])
        big.append([a[None] for a in res])

    def own_cols(row, n_rows, width):
        cut = lax.dynamic_slice(packs, (0, row, dev * width), (N_DEV, n_rows, width))
        return jnp.pad(cut, ((0, 0), (0, 0), (0, SP_COLS - width)))

    packs_own = jnp.concatenate([own_cols(R_CONV_W, 3, conv_cols), packs[:, R_CONV_W + 3:R_GATE_W],
                                 own_cols(R_GATE_W, GATE_RANK, gate_cols)], axis=1)

    def small_pack(cw, cg, gw, gb, gg, l1g, l1b, l2g, l2b):
        return _pack(cw[0], cg, gb, gg, l1g, l1b, l2g, l2b, None, gw[0])

    w_s = small_pack(conv_w, conv_norm_g, w_gate_up, gate_bias, gla_norm_g, ln1_g, ln1_b, ln2_g, ln2_b)
    m_s = small_pack(m_conv_w, m_conv_norm_g, m_w_gate_up, m_gate_bias, m_gla_norm_g, m_ln1_g, m_ln1_b, m_ln2_g, m_ln2_b)
    v_s = small_pack(v_conv_w, v_conv_norm_g, v_w_gate_up, v_gate_bias, v_gla_norm_g, v_ln1_g, v_ln1_b, v_ln2_g, v_ln2_b)
    g_s, d_s, mn_s, vn_s = _small_adamw(packs_own, w_s, m_s, v_s)
    loss = g_s[R_LOSS, 0]

    small = [_unpack(p, D, conv_cols, gate_cols) for p in (g_s, d_s, mn_s, vn_s)]

    def leaf(kind, name):
        if name in ("w_in", "w_out", "w_ff_up", "w_ff_down"):
            return big[("w_in", "w_out", "w_ff_up", "w_ff_down").index(name)][kind]
        a = small[kind][name]
        return a[None] if name in ("conv_w", "w_gate_up") else a

    order = ("w_in", "conv_w", "conv_norm_g", "w_gate_up", "gate_bias", "gla_norm_g", "w_out", "ln1_g", "ln1_b",
             "w_ff_up", "w_ff_down", "ln2_g", "ln2_b")
    out = [loss, grad_x[None]]
    for kind in range(4):
        out += [leaf(kind, nm) for nm in order]
    return tuple(out)
```

```python
import jax
import jax.numpy as jnp
from jax import lax
from jax.experimental import pallas as pl
from jax.experimental.pallas import tpu as pltpu

F32 = jnp.float32
BF16 = jnp.bfloat16

D_CONV = 1024
CONV_GROUPS = 8
GLA_HEADS = 4
HEAD_K = 128
HEAD_V = 256
D_GLA_K = 512
D_GLA_V = 1024
GATE_RANK = 16
GATE_TAU = 16.0
CHUNK = 64
LN_EPS = 1e-5
RMS_EPS = 1e-6
DN_ALPHA = 2.0 ** 0.25
D_IN_PROJ = 6160
ADAM_LR = 0.001
ADAM_B1 = 0.9
ADAM_B2 = 0.999
ADAM_EPS = 1e-08
ADAM_WD = 0.01
ADAM_STEP = 10

N_DEV = 8
N_CHIP = 4
LANE = 128
HALF_P = 3200
P_INT = 2 * HALF_P
CONV_COLS = 3 * D_CONV
GLA_COLS = D_IN_PROJ - CONV_COLS
SP_ROWS = 32
SP_COLS = 1024
VMEM_LIMIT = 56 * 1024 * 1024

NN = ((1,), (0,))
NT = ((1,), (1,))
TN = ((0,), (0,))
MESH = pl.DeviceIdType.MESH


def _dot(a, b, dims, precision=None):
    return lax.dot_general(a, b, (dims, ((), ())), preferred_element_type=F32, precision=precision)


def _tile(n, pref):
    if n <= pref:
        return n
    t = (pref // LANE) * LANE
    while t > 0 and n % t:
        t -= LANE
    assert t > 0, (n, pref)
    return t


def _params(n_axes):
    return pltpu.CompilerParams(dimension_semantics=("arbitrary",) * n_axes, vmem_limit_bytes=VMEM_LIMIT)


def _full(shape):
    nd = len(shape)
    return pl.BlockSpec(shape, lambda *_: (0,) * nd)


def _hbm_specs(n):
    return [pl.BlockSpec(memory_space=pl.ANY)] * n


def _mm(name, mode, a, b, *, M, N, K, tm, tn, tk, outs, epilogue, extras=(), a_fn=None, a_spec=None, b_spec=None,
        deps=()):
    ni, nj, nk = M // tm, N // tn, K // tk
    assert ni * tm == M and nj * tn == N and nk * tk == K, (name, M, N, K, tm, tn, tk)
    if a_spec is None:
        a_spec = (pl.BlockSpec((tk, tm), lambda i, j, k: (k, i)) if mode == "tn"
                  else pl.BlockSpec((tm, tk), lambda i, j, k: (i, k)))
    if b_spec is None:
        b_spec = (pl.BlockSpec((tn, tk), lambda i, j, k: (j, k)) if mode == "nt"
                  else pl.BlockSpec((tk, tn), lambda i, j, k: (k, j)))
    dims = {"nn": NN, "nt": NT, "tn": TN}[mode]
    n_ex, n_out, n_dep = len(extras), len(outs), len(deps)

    def body(*refs):
        a_ref, b_ref = refs[0], refs[1]
        ex = refs[2:2 + n_ex]
        o = refs[2 + n_ex + n_dep:2 + n_ex + n_dep + n_out]
        acc_ref = refs[2 + n_ex + n_dep + n_out]
        i, j, k = pl.program_id(0), pl.program_id(1), pl.program_id(2)
        av = a_ref[...]
        if a_fn is not None:
            av = a_fn(av)
        part = _dot(av, b_ref[...], dims)
        if nk == 1:
            acc_ref[...] = part
            epilogue(acc_ref, ex, o, i, j)
        else:
            @pl.when(k == 0)
            def _():
                acc_ref[...] = part

            @pl.when(k > 0)
            def _():
                acc_ref[...] += part

            @pl.when(k == nk - 1)
            def _():
                epilogue(acc_ref, ex, o, i, j)

    return pl.pallas_call(
        body,
        name=name,
        grid=(ni, nj, nk),
        in_specs=[a_spec, b_spec] + [s for _, s in extras] + _hbm_specs(n_dep),
        out_specs=[s for _, s in outs],
        out_shape=[s for s, _ in outs],
        scratch_shapes=[pltpu.VMEM((tm, tn), F32)],
        compiler_params=_params(3),
    )(a, b, *[x for x, _ in extras], *deps)


def _store(dtype):
    def ep(acc_ref, ex, o, i, j):
        o[0][...] = acc_ref[...].astype(dtype)
    return ep


def _to_bf16(v):
    return v.astype(BF16)


def _row_chunks(tm):
    rc = 64 if tm % 64 == 0 else tm
    return rc, tm // rc


def _ln_bwd(dy, xhat, rstd, g):
    dxh = dy * g
    m1 = jnp.mean(dxh, axis=-1, keepdims=True)
    m2 = jnp.mean(dxh * xhat, axis=-1, keepdims=True)
    return rstd * (dxh - m1 - xhat * m2)


def _ln_fwd(h):
    mu = jnp.mean(h, axis=-1, keepdims=True)
    xc = h - mu
    var = jnp.mean(xc * xc, axis=-1, keepdims=True)
    rstd = lax.rsqrt(var + LN_EPS)
    return xc * rstd, rstd


def _proj_fwd(x, w_full, deps=()):
    T, D = x.shape
    P = w_full.shape[1]
    tm, tn = _tile(T, 512), _tile(P, 1280)
    return _mm("proj_fwd", "nn", x, w_full, M=T, N=P, K=D, tm=tm, tn=tn, tk=D,
               outs=[(jax.ShapeDtypeStruct((T, P), F32), pl.BlockSpec((tm, tn), lambda i, j, k: (i, j)))],
               epilogue=_store(F32), a_fn=_to_bf16, deps=deps)[0]


def _conv_shift(h, hp):
    row = lax.broadcasted_iota(jnp.int32, h.shape, 0)
    hm1 = hp[7:8, :]
    hm2 = hp[6:7, :]
    h1 = jnp.where(row == 0, hm1, pltpu.roll(h, 1, 0))
    h2 = jnp.where(row == 0, hm2, jnp.where(row == 1, hm1, pltpu.roll(h, 2, 0)))
    return h1, h2


def _conv_fwd(proj, conv_w8, conv_g):
    T = proj.shape[0]
    tt = _tile(T, 256)
    nt = T // tt
    t8 = tt // 8

    def body(b_ref, c_ref, u_ref, cp_ref, up_ref, w_ref, g_ref, yin_ref):
        i = pl.program_id(0)
        h = c_ref[...] * u_ref[...]
        hp = jnp.where(i > 0, cp_ref[...] * up_ref[...], 0.0)
        h1, h2 = _conv_shift(h, hp)
        w = w_ref[...]
        y = w[0:1, :] * h2 + w[1:2, :] * h1 + w[2:3, :] * h
        p = b_ref[...] * y
        parts = []
        for gi in range(CONV_GROUPS):
            pg = p[:, gi * LANE:(gi + 1) * LANE]
            r = lax.rsqrt(jnp.mean(pg * pg, axis=-1, keepdims=True) + RMS_EPS)
            parts.append(pg * r)
        yn = jnp.concatenate(parts, axis=1) * g_ref[...]
        yin_ref[...] = yn.astype(BF16)

    def col(cidx):
        return pl.BlockSpec((tt, D_CONV), lambda i: (i, cidx))

    def prev(cidx):
        return pl.BlockSpec((8, D_CONV), lambda i: (jnp.maximum(i * t8 - 1, 0), cidx))

    return pl.pallas_call(
        body,
        name="conv_fwd",
        grid=(nt,),
        in_specs=[col(0), col(1), col(2), prev(1), prev(2), _full((8, D_CONV)), _full((1, D_CONV))],
        out_specs=pl.BlockSpec((tt, D_CONV), lambda i: (i, 0)),
        out_shape=jax.ShapeDtypeStruct((T, 2 * D_CONV), BF16),
        compiler_params=_params(1),
    )(proj, proj, proj, proj, proj, conv_w8, conv_g)


def _log_sigmoid(z):
    return jnp.minimum(z, 0.0) - jnp.log(1.0 + jnp.exp(-jnp.abs(z)))


def _gla_chunk_terms(blk, wg_ref, gb_ref):
    q = blk[:, 0:512]
    k = blk[:, 512:1024]
    zl = blk[:, 3072:3200]
    z = _dot(zl.astype(BF16), wg_ref[...], NN) + gb_ref[...]
    log_a = _log_sigmoid(z) * (1.0 / GATE_TAU)
    ri = lax.broadcasted_iota(jnp.int32, (CHUNK, CHUNK), 0)
    ci = lax.broadcasted_iota(jnp.int32, (CHUNK, CHUNK), 1)
    causal = ri >= ci
    lower = causal.astype(F32)
    bcum = _dot(lower, log_a, NN, precision=lax.Precision.HIGHEST)
    return q, k, zl, z, bcum, causal


def _gla_head_terms(q, k, bcum, h):
    sl = slice(h * HEAD_K, (h + 1) * HEAD_K)
    bh = bcum[:, sl]
    bl = bh[CHUNK - 1:CHUNK, :]
    eb = jnp.exp(bh)
    enb = jnp.exp(-bh)
    eend = jnp.exp(bl - bh)
    dec = jnp.exp(bl)
    qd = q[:, sl] * (HEAD_K ** -0.5) * eb
    ki = k[:, sl] * enb
    ke = k[:, sl] * eend
    return eb, enb, eend, dec, qd, ki, ke


def _sigmoid(x):
    return 1.0 / (1.0 + jnp.exp(-x))


def _gla_fwd(proj, wg128, gbias, gng, yin, deps=()):
    T = proj.shape[0]
    nch = T // CHUNK

    def body(p_ref, wg_ref, gb_ref, gn_ref, yin_in_ref, *rest):
        o_ref, st_ref, yin_ref, s_ref = rest[len(deps):]
        n = pl.program_id(0)

        @pl.when(n == 0)
        def _():
            s_ref[...] = jnp.zeros_like(s_ref)

        blk = p_ref[...]
        q, k, _, _, bcum, causal = _gla_chunk_terms(blk, wg_ref, gb_ref)
        v = blk[:, 1024:2048]
        r = blk[:, 2048:3072]
        gn = gn_ref[...]
        for h in range(GLA_HEADS):
            _, _, _, dec, qd, ki, ke = _gla_head_terms(q, k, bcum, h)
            vs = slice(h * HEAD_V, (h + 1) * HEAD_V)
            vb = v[:, vs].astype(BF16)
            qdb = qd.astype(BF16)
            a = jnp.where(causal, _dot(qdb, ki.astype(BF16), NT), 0.0)
            st = s_ref[h]
            o = _dot(a.astype(BF16), vb, NN) + _dot(qdb, st.astype(BF16), NT)
            st_ref[h] = st
            s_ref[h] = dec * st + _dot(vb, ke.astype(BF16), TN)
            o_ref[:, vs] = o
            rinv = lax.rsqrt(jnp.mean(o * o, axis=-1, keepdims=True) + RMS_EPS)
            rh = r[:, vs]
            yin_ref[:, vs] = (o * rinv * gn[:, vs] * (rh * _sigmoid(rh))).astype(BF16)

    return pl.pallas_call(
        body,
        name="gla_fwd",
        grid=(nch,),
        in_specs=[pl.BlockSpec((CHUNK, HALF_P), lambda n: (n, 1)), _full((LANE, D_GLA_K)), _full((1, D_GLA_K)),
                  _full((1, D_GLA_V)), pl.BlockSpec(memory_space=pl.ANY)] + _hbm_specs(len(deps)),
        out_specs=[pl.BlockSpec((CHUNK, D_GLA_V), lambda n: (n, 0)),
                   pl.BlockSpec((None, GLA_HEADS, HEAD_V, HEAD_K), lambda n: (n, 0, 0, 0)),
                   pl.BlockSpec((CHUNK, D_GLA_V), lambda n: (n, 1))],
        out_shape=[jax.ShapeDtypeStruct((T, D_GLA_V), F32),
                   jax.ShapeDtypeStruct((nch, GLA_HEADS, HEAD_V, HEAD_K), F32),
                   jax.ShapeDtypeStruct(yin.shape, BF16)],
        scratch_shapes=[pltpu.VMEM((GLA_HEADS, HEAD_V, HEAD_K), F32)],
        input_output_aliases={4: 2},
        compiler_params=_params(1),
    )(proj, wg128, gbias, gng, yin, *deps)


def _mix_ln1(yin, w_out, x, ln_g, ln_b, deps=()):
    T, D = x.shape
    KY = yin.shape[1]
    tm = _tile(T, 512)
    rc, nrc = _row_chunks(tm)

    def ep(acc_ref, ex, o, i, j):
        x_ref, g_ref, b_ref = ex
        xhat_ref, x1_ref, rstd_ref = o
        g, b = g_ref[...], b_ref[...]

        def chunk(r, carry):
            rows = pl.ds(pl.multiple_of(r * rc, rc), rc)
            xhat, rstd = _ln_fwd(DN_ALPHA * x_ref[rows, :] + acc_ref[rows, :])
            xhat_ref[rows, :] = xhat
            x1_ref[rows, :] = (xhat * g + b).astype(BF16)
            rstd_ref[rows, :] = rstd
            return carry

        lax.fori_loop(0, nrc, chunk, 0)

    row = lambda i, j, k: (i, 0)
    vec = _full((1, D))
    return _mm("mix_ln1", "nn", yin, w_out, M=T, N=D, K=KY, tm=tm, tn=D, tk=_tile(KY, 1024),
               outs=[(jax.ShapeDtypeStruct((T, D), F32), pl.BlockSpec((tm, D), row)),
                     (jax.ShapeDtypeStruct((T, D), BF16), pl.BlockSpec((tm, D), row)),
                     (jax.ShapeDtypeStruct((T, 1), F32), pl.BlockSpec((tm, 1), row))],
               extras=[(x, pl.BlockSpec((tm, D), row)), (ln_g, vec), (ln_b, vec)],
               epilogue=ep, deps=deps)


def _ff_up(x1, w_up_blk, deps=()):
    T, D = x1.shape
    nb, _, fb = w_up_blk.shape
    tm = _tile(T, 1024)

    def ep(acc_ref, ex, o, i, j):
        ra = jnp.maximum(acc_ref[...], 0.0)
        o[0][...] = ra.astype(BF16)
        o[1][...] = (ra * ra).astype(BF16)

    blk = pl.BlockSpec((tm, fb), lambda i, j, k: (i, j))
    shp = jax.ShapeDtypeStruct((T, nb * fb), BF16)
    return _mm("ff_up", "nn", x1, w_up_blk, M=T, N=nb * fb, K=D, tm=tm, tn=fb, tk=D,
               b_spec=pl.BlockSpec((None, D, fb), lambda i, j, k: (j, 0, 0)),
               outs=[(shp, blk), (shp, blk)], epilogue=ep, deps=deps)


def _ff_down_loss(h2, w_down, xhat1, target, g1, b1, g2, b2):
    T, F = h2.shape
    D = w_down.shape[1]
    tm = _tile(T, 512)
    rc, nrc = _row_chunks(tm)
    inv_d = 1.0 / D

    def ep(acc_ref, ex, o, i, j):
        xh1_ref, t_ref, g1_ref, b1_ref, g2_ref, b2_ref = ex
        dh_ref, dhb_ref, gg_ref, gb_ref, loss_ref = o
        g1v, b1v, g2v, b2v = g1_ref[...], b1_ref[...], g2_ref[...], b2_ref[...]

        @pl.when(i == 0)
        def _():
            gg_ref[...] = jnp.zeros_like(gg_ref)
            gb_ref[...] = jnp.zeros_like(gb_ref)
            loss_ref[...] = jnp.zeros_like(loss_ref)

        def chunk(r, carry):
            sg, sb, sl = carry
            rows = pl.ds(pl.multiple_of(r * rc, rc), rc)
            x1 = xh1_ref[rows, :] * g1v + b1v
            xhat, rstd = _ln_fwd(DN_ALPHA * x1 + acc_ref[rows, :])
            e = xhat * g2v + b2v - t_ref[rows, :]
            dy = e * inv_d
            dh = _ln_bwd(dy, xhat, rstd, g2v)
            dh_ref[rows, :] = dh
            dhb_ref[rows, :] = dh.astype(BF16)
            sg = sg + jnp.sum(dy * xhat, axis=0, keepdims=True)
            sb = sb + jnp.sum(dy, axis=0, keepdims=True)
            sl = sl + jnp.sum(e * e, axis=0, keepdims=True)
            return sg, sb, sl

        z = jnp.zeros((1, D), F32)
        sg, sb, sl = lax.fori_loop(0, nrc, chunk, (z, z, z))
        gg_ref[...] += sg
        gb_ref[...] += sb
        loss_ref[...] += sl * (0.5 * inv_d)

    row = lambda i, j, k: (i, 0)
    vec = _full((1, D))
    vshape = jax.ShapeDtypeStruct((1, D), F32)
    return _mm("ff_down_loss", "nn", h2, w_down, M=T, N=D, K=F, tm=tm, tn=D, tk=_tile(F, 1024),
               outs=[(jax.ShapeDtypeStruct((T, D), F32), pl.BlockSpec((tm, D), row)),
                     (jax.ShapeDtypeStruct((T, D), BF16), pl.BlockSpec((tm, D), row)),
                     (vshape, vec), (vshape, vec), (vshape, vec)],
               extras=[(xhat1, pl.BlockSpec((tm, D), row)), (target, pl.BlockSpec((tm, D), row)),
                       (g1, vec), (b1, vec), (g2, vec), (b2, vec)],
               epilogue=ep)


def _ff_down_bwd_act(dh3b, w_down, ra):
    T, D = dh3b.shape
    F = w_down.shape[0]
    tm, tn = _tile(T, 1024), _tile(F, 1024)

    def ep(acc_ref, ex, o, i, j):
        o[0][...] = (acc_ref[...] * (2.0 * ex[0][...].astype(F32))).astype(BF16)

    blk = pl.BlockSpec((tm, tn), lambda i, j, k: (i, j))
    return _mm("ff_down_bwd_act", "nt", dh3b, w_down, M=T, N=F, K=D, tm=tm, tn=tn, tk=D,
               outs=[(jax.ShapeDtypeStruct((T, F), BF16), blk)], extras=[(ra, blk)], epilogue=ep)[0]


def _grad_w(name, a, b, *, a_fn=None, tm_pref=1024, tn_pref=1024, deps=()):
    T, M = a.shape
    N = b.shape[1]
    tm, tn, tk = _tile(M, tm_pref), _tile(N, tn_pref), _tile(T, 1024)
    return _mm(name, "tn", a, b, M=M, N=N, K=T, tm=tm, tn=tn, tk=tk, a_fn=a_fn, deps=deps,
               outs=[(jax.ShapeDtypeStruct((M, N), F32), pl.BlockSpec((tm, tn), lambda i, j, k: (i, j)))],
               epilogue=_store(F32))[0]


def _grad_w_up_blk(x1, da, nb, deps=()):
    T, D = x1.shape
    F = da.shape[1]
    fb = F // nb
    tm, tk = _tile(D, 1024), _tile(T, 1024)
    return _mm("grad_w_up", "tn", x1, da, M=D, N=F, K=T, tm=tm, tn=fb, tk=tk, deps=deps,
               outs=[(jax.ShapeDtypeStruct((nb, D, fb), F32),
                      pl.BlockSpec((None, tm, fb), lambda i, j, k: (j, i, 0)))],
               epilogue=_store(F32))[0]


def _ff_up_bwd_ln1(da, w_up_blk, dh3, xhat1, rstd1, g1, deps=()):
    T, F = da.shape
    nb, D, fb = w_up_blk.shape
    tm = _tile(T, 512)
    rc, nrc = _row_chunks(tm)

    def ep(acc_ref, ex, o, i, j):
        dh3_ref, xh_ref, rstd_ref, g_ref = ex
        dh_ref, dhb_ref, gg_ref, gb_ref = o
        g = g_ref[...]

        @pl.when(i == 0)
        def _():
            gg_ref[...] = jnp.zeros_like(gg_ref)
            gb_ref[...] = jnp.zeros_like(gb_ref)

        def chunk(r, carry):
            sg, sb = carry
            rows = pl.ds(pl.multiple_of(r * rc, rc), rc)
            dx1 = DN_ALPHA * dh3_ref[rows, :] + acc_ref[rows, :]
            xhat = xh_ref[rows, :]
            dh = _ln_bwd(dx1, xhat, rstd_ref[rows, :], g)
            dh_ref[rows, :] = dh
            dhb_ref[rows, :] = dh.astype(BF16)
            return sg + jnp.sum(dx1 * xhat, axis=0, keepdims=True), sb + jnp.sum(dx1, axis=0, keepdims=True)

        z = jnp.zeros((1, D), F32)
        sg, sb = lax.fori_loop(0, nrc, chunk, (z, z))
        gg_ref[...] += sg
        gb_ref[...] += sb

    row = lambda i, j, k: (i, 0)
    vec = _full((1, D))
    vshape = jax.ShapeDtypeStruct((1, D), F32)
    return _mm("ff_up_bwd_ln1", "nt", da, w_up_blk, M=T, N=D, K=F, tm=tm, tn=D, tk=fb,
               b_spec=pl.BlockSpec((None, D, fb), lambda i, j, k: (k, 0, 0)),
               outs=[(jax.ShapeDtypeStruct((T, D), F32), pl.BlockSpec((tm, D), row)),
                     (jax.ShapeDtypeStruct((T, D), BF16), pl.BlockSpec((tm, D), row)),
                     (vshape, vec), (vshape, vec)],
               extras=[(dh3, pl.BlockSpec((tm, D), row)), (xhat1, pl.BlockSpec((tm, D), row)),
                       (rstd1, pl.BlockSpec((tm, 1), row)), (g1, vec)],
               epilogue=ep, deps=deps)


def _mix_bwd(dh1b, w_out, deps=()):
    T, D = dh1b.shape
    KY = w_out.shape[0]
    tm, tn = _tile(T, 1024), _tile(KY, 1024)
    return _mm("mix_bwd", "nt", dh1b, w_out, M=T, N=KY, K=D, tm=tm, tn=tn, tk=D, deps=deps,
               outs=[(jax.ShapeDtypeStruct((T, KY), F32), pl.BlockSpec((tm, tn), lambda i, j, k: (i, j)))],
               epilogue=_store(F32))[0]


def _conv_bwd(proj, dyin, conv_w8, conv_g):
    T = proj.shape[0]
    tt = _tile(T, 256)
    nt = T // tt
    t8 = tt // 8
    nx = tt + 8

    def body(b_ref, c_ref, u_ref, d_ref, bn_ref, cn_ref, un_ref, dn_ref, cp_ref, up_ref, w_ref, g_ref,
             dp_ref, dw_ref, dg_ref):
        i = pl.program_id(0)

        @pl.when(i == 0)
        def _():
            dw_ref[...] = jnp.zeros_like(dw_ref)
            dg_ref[...] = jnp.zeros_like(dg_ref)

        more = i < nt - 1

        def ext(cur_ref, nxt_ref):
            return jnp.concatenate([cur_ref[...], jnp.where(more, nxt_ref[...], 0.0)], axis=0)

        bx, cx, ux, dx = ext(b_ref, bn_ref), ext(c_ref, cn_ref), ext(u_ref, un_ref), ext(d_ref, dn_ref)
        hx = cx * ux
        hp = jnp.where(i > 0, cp_ref[...] * up_ref[...], 0.0)
        h1, h2 = _conv_shift(hx, hp)
        w = w_ref[...]
        g = g_ref[...]
        yx = w[0:1, :] * h2 + w[1:2, :] * h1 + w[2:3, :] * hx
        px = bx * yx
        dps, dgs = [], []
        for gi in range(CONV_GROUPS):
            sl = slice(gi * LANE, (gi + 1) * LANE)
            pg, dg_ = px[:, sl], dx[:, sl]
            r = lax.rsqrt(jnp.mean(pg * pg, axis=-1, keepdims=True) + RMS_EPS)
            gd = g[:, sl] * dg_
            dps.append(r * gd - pg * (r * r * r) * jnp.mean(pg * gd, axis=-1, keepdims=True))
            dgs.append(jnp.sum((dg_ * pg * r)[:tt, :], axis=0, keepdims=True))
        dpx = jnp.concatenate(dps, axis=1)
        dg_ref[...] += jnp.concatenate(dgs, axis=1)
        dyx = dpx * bx
        dyc = dyx[:tt, :]
        dh = (w[2:3, :] * dyx + w[1:2, :] * pltpu.roll(dyx, nx - 1, 0) + w[0:1, :] * pltpu.roll(dyx, nx - 2, 0))[:tt, :]
        dw_ref[0:1, :] += jnp.sum(dyc * h2[:tt, :], axis=0, keepdims=True)
        dw_ref[1:2, :] += jnp.sum(dyc * h1[:tt, :], axis=0, keepdims=True)
        dw_ref[2:3, :] += jnp.sum(dyc * hx[:tt, :], axis=0, keepdims=True)
        dp_ref[:, 0:D_CONV] = (dpx * yx)[:tt, :].astype(BF16)
        dp_ref[:, D_CONV:2 * D_CONV] = (dh * u_ref[...]).astype(BF16)
        dp_ref[:, 2 * D_CONV:3 * D_CONV] = (dh * c_ref[...]).astype(BF16)
        dp_ref[:, 3 * D_CONV:HALF_P] = jnp.zeros((tt, HALF_P - 3 * D_CONV), BF16)

    def col(cidx):
        return pl.BlockSpec((tt, D_CONV), lambda i: (i, cidx))

    def nxt(cidx):
        return pl.BlockSpec((8, D_CONV), lambda i: (jnp.minimum((i + 1) * t8, T // 8 - 1), cidx))

    def prev(cidx):
        return pl.BlockSpec((8, D_CONV), lambda i: (jnp.maximum(i * t8 - 1, 0), cidx))

    return pl.pallas_call(
        body,
        name="conv_bwd",
        grid=(nt,),
        in_specs=[col(0), col(1), col(2), col(0), nxt(0), nxt(1), nxt(2), nxt(0), prev(1), prev(2),
                  _full((8, D_CONV)), _full((1, D_CONV))],
        out_specs=[pl.BlockSpec((tt, HALF_P), lambda i: (i, 0)), _full((8, D_CONV)), _full((1, D_CONV))],
        out_shape=[jax.ShapeDtypeStruct((T, P_INT), BF16), jax.ShapeDtypeStruct((8, D_CONV), F32),
                   jax.ShapeDtypeStruct((1, D_CONV), F32)],
        compiler_params=_params(1),
    )(proj, proj, proj, dyin, proj, proj, proj, dyin, proj, proj, conv_w8, conv_g)


def _gla_bwd(proj, wg128, gbias, gng, o_all, states, dyin, dproj):
    T = proj.shape[0]
    nch = T // CHUNK

    def body(p_ref, wg_ref, gb_ref, gn_ref, o_ref, st_ref, d_ref, dp_in_ref,
             dp_ref, dwg_ref, dgb_ref, dgn_ref, ds_ref):
        n = pl.program_id(0)

        @pl.when(n == 0)
        def _():
            ds_ref[...] = jnp.zeros_like(ds_ref)
            dwg_ref[...] = jnp.zeros_like(dwg_ref)
            dgb_ref[...] = jnp.zeros_like(dgb_ref)
            dgn_ref[...] = jnp.zeros_like(dgn_ref)

        blk = p_ref[...]
        q, k, zl, z, bcum, causal = _gla_chunk_terms(blk, wg_ref, gb_ref)
        v = blk[:, 1024:2048]
        r = blk[:, 2048:3072]
        gn = gn_ref[...]
        upper = (lax.broadcasted_iota(jnp.int32, (CHUNK, CHUNK), 0)
                 <= lax.broadcasted_iota(jnp.int32, (CHUNK, CHUNK), 1)).astype(F32)
        dlog_parts = []
        for h in range(GLA_HEADS):
            eb, enb, eend, dec, qd, ki, ke = _gla_head_terms(q, k, bcum, h)
            vs = slice(h * HEAD_V, (h + 1) * HEAD_V)
            ks = slice(h * HEAD_K, (h + 1) * HEAD_K)
            o = o_ref[:, vs]
            rh = r[:, vs]
            dyg = d_ref[:, vs]
            rinv = lax.rsqrt(jnp.mean(o * o, axis=-1, keepdims=True) + RMS_EPS)
            sg = _sigmoid(rh)
            on = o * rinv
            dr = dyg * (on * gn[:, vs]) * (sg * (1.0 + rh * (1.0 - sg)))
            don = dyg * (rh * sg)
            dgn_ref[:, vs] += jnp.sum(don * on, axis=0, keepdims=True)
            t = don * gn[:, vs]
            do = rinv * t - o * (rinv * rinv * rinv) * jnp.mean(o * t, axis=-1, keepdims=True)
            dob = do.astype(BF16)
            vb = v[:, vs].astype(BF16)
            qdb, kib, keb = qd.astype(BF16), ki.astype(BF16), ke.astype(BF16)
            a = jnp.where(causal, _dot(qdb, kib, NT), 0.0)
            st = st_ref[h]
            dst = ds_ref[h]
            dstb = dst.astype(BF16)
            da = jnp.where(causal, _dot(dob, vb, NT), 0.0)
            dab = da.astype(BF16)
            dv = _dot(a.astype(BF16), dob, TN) + _dot(keb, dstb, NT)
            dqd = _dot(dab, kib, NN) + _dot(dob, st.astype(BF16), NN)
            dki = _dot(dab, qdb, TN)
            dke = _dot(vb, dstb, NN)
            ddec = jnp.sum(st * dst, axis=0, keepdims=True)
            ds_ref[h] = dec * dst + _dot(dob, qdb, TN)
            dq = dqd * eb * (HEAD_K ** -0.5)
            dk = dki * enb + dke * eend
            db = dqd * qd - dki * ki - dke * ke
            dbl = jnp.sum(dke * ke, axis=0, keepdims=True) + dec * ddec
            dlog_parts.append(_dot(upper, db, NN, precision=lax.Precision.HIGHEST) + dbl)
            dp_ref[:, ks] = dq.astype(BF16)
            dp_ref[:, D_GLA_K + h * HEAD_K:D_GLA_K + (h + 1) * HEAD_K] = dk.astype(BF16)
            dp_ref[:, 1024 + h * HEAD_V:1024 + (h + 1) * HEAD_V] = dv.astype(BF16)
            dp_ref[:, 2048 + h * HEAD_V:2048 + (h + 1) * HEAD_V] = dr.astype(BF16)
        dlog = jnp.concatenate(dlog_parts, axis=1)
        dz = dlog * (1.0 / GATE_TAU) * (1.0 / (1.0 + jnp.exp(z)))
        dzb = dz.astype(BF16)
        dp_ref[:, 3072:3200] = _dot(dzb, wg_ref[...], NT).astype(BF16)
        dwg_ref[...] += _dot(zl.astype(BF16), dzb, TN)
        dgb_ref[...] += jnp.sum(dz, axis=0, keepdims=True)

    rev = lambda n: nch - 1 - n
    return pl.pallas_call(
        body,
        name="gla_bwd",
        grid=(nch,),
        in_specs=[pl.BlockSpec((CHUNK, HALF_P), lambda n: (rev(n), 1)), _full((LANE, D_GLA_K)), _full((1, D_GLA_K)),
                  _full((1, D_GLA_V)), pl.BlockSpec((CHUNK, D_GLA_V), lambda n: (rev(n), 0)),
                  pl.BlockSpec((None, GLA_HEADS, HEAD_V, HEAD_K), lambda n: (rev(n), 0, 0, 0)),
                  pl.BlockSpec((CHUNK, D_GLA_V), lambda n: (rev(n), 1)), pl.BlockSpec(memory_space=pl.ANY)],
        out_specs=[pl.BlockSpec((CHUNK, HALF_P), lambda n: (rev(n), 1)), _full((LANE, D_GLA_K)),
                   _full((1, D_GLA_K)), _full((1, D_GLA_V))],
        out_shape=[jax.ShapeDtypeStruct(dproj.shape, BF16), jax.ShapeDtypeStruct((LANE, D_GLA_K), F32),
                   jax.ShapeDtypeStruct((1, D_GLA_K), F32), jax.ShapeDtypeStruct((1, D_GLA_V), F32)],
        scratch_shapes=[pltpu.VMEM((GLA_HEADS, HEAD_V, HEAD_K), F32)],
        input_output_aliases={7: 0},
        compiler_params=_params(1),
    )(proj, wg128, gbias, gng, o_all, states, dyin, dproj)


def _proj_bwd_x(dproj, w_full, dh1, deps=()):
    T, P = dproj.shape
    D = w_full.shape[0]
    tm, tk = _tile(T, 512), _tile(P, 1280)

    def ep(acc_ref, ex, o, i, j):
        o[0][...] = DN_ALPHA * ex[0][...] + acc_ref[...]

    row = pl.BlockSpec((tm, D), lambda i, j, k: (i, 0))
    return _mm("proj_bwd_x", "nt", dproj, w_full, M=T, N=D, K=P, tm=tm, tn=D, tk=tk,
               outs=[(jax.ShapeDtypeStruct((T, D), F32), row)], extras=[(dh1, row)], epilogue=ep, deps=deps)[0]


def _place():
    x, y, c = lax.axis_index("x"), lax.axis_index("y"), lax.axis_index("c")
    chips = [(1 - x, y), (x, 1 - y), (1 - x, 1 - y)]
    return x, y, c, chips


def _rcopy(src, dst, ssem, rsem, dev):
    return pltpu.make_async_remote_copy(src_ref=src, dst_ref=dst, send_sem=ssem, recv_sem=rsem,
                                        device_id=dev, device_id_type=MESH)


def _all_gather(name, shards, deps=()):
    n = len(shards)

    def body(*refs):
        ins, outs = refs[:n], refs[n + len(deps):2 * n + len(deps)]
        ssem, rsem, lsem = refs[2 * n + len(deps):]
        x, y, c, chips = _place()
        me, sib = (x, y, c), (x, y, 1 - c)

        def slot(w, px, py, pc):
            return outs[w].at[4 * px + 2 * py + pc]

        started = []
        for w in range(n):
            lc = pltpu.make_async_copy(ins[w], slot(w, *me), lsem.at[w])
            lc.start()
            started.append(lc)
        sends = []
        for w in range(n):
            cp = _rcopy(ins[w], slot(w, *me), ssem.at[7 * w], rsem.at[7 * w], sib)
            cp.start()
            sends.append(cp)
            for jx, chip in enumerate(chips):
                cp = _rcopy(ins[w], slot(w, *me), ssem.at[7 * w + 1 + jx], rsem.at[7 * w + 1 + jx], (*chip, c))
                cp.start()
                sends.append(cp)
        for w in range(n):
            for jx, chip in enumerate(chips):
                blk = slot(w, *chip, c)
                _rcopy(blk, blk, ssem.at[7 * w + 1 + jx], rsem.at[7 * w + 1 + jx], me).wait_recv()
                cp = _rcopy(blk, blk, ssem.at[7 * w + 4 + jx], rsem.at[7 * w + 4 + jx], sib)
                cp.start()
                sends.append(cp)
        for w in range(n):
            blk = slot(w, x, y, 1 - c)
            _rcopy(blk, blk, ssem.at[7 * w], rsem.at[7 * w], me).wait_recv()
            for jx, chip in enumerate(chips):
                blk = slot(w, *chip, 1 - c)
                _rcopy(blk, blk, ssem.at[7 * w + 4 + jx], rsem.at[7 * w + 4 + jx], me).wait_recv()
        for cp in sends:
            cp.wait_send()
        for lc in started:
            lc.wait()

    return pl.pallas_call(
        body,
        name=name,
        in_specs=_hbm_specs(n + len(deps)),
        out_specs=_hbm_specs(n),
        out_shape=[jax.ShapeDtypeStruct((N_DEV,) + s.shape, s.dtype) for s in shards],
        scratch_shapes=[pltpu.SemaphoreType.DMA((7 * n,)), pltpu.SemaphoreType.DMA((7 * n,)),
                        pltpu.SemaphoreType.DMA((n,))],
    )(*shards, *deps)


HBM_SPEC = pl.BlockSpec(memory_space=pltpu.HBM)
SEM_SPEC = pl.BlockSpec(memory_space=pltpu.SEMAPHORE)
SIDE_EFFECT = pltpu.SideEffectType.DATAFLOW_SIDE_EFFECTING


def _place_own(name, shards, dep):
    n = len(shards)

    def body(*refs):
        ins, outs, sem = refs[:n], refs[n + 1:2 * n + 1], refs[2 * n + 1]
        x, y, c, _ = _place()
        cps = [pltpu.make_async_copy(ins[w], outs[w].at[4 * x + 2 * y + c], sem.at[w]) for w in range(n)]
        for cp in cps:
            cp.start()
        for cp in cps:
            cp.wait()

    return pl.pallas_call(
        body,
        name=name,
        in_specs=_hbm_specs(n + 1),
        out_specs=_hbm_specs(n),
        out_shape=[jax.ShapeDtypeStruct((N_DEV,) + s.shape, s.dtype) for s in shards],
        scratch_shapes=[pltpu.SemaphoreType.DMA((n,))],
    )(*shards, dep)


def _xfer_start(name, bufs, plan, n):
    nb = len(bufs)

    def body(*refs):
        ins = refs[:nb]
        ssem, rsem = refs[nb], refs[nb + 1]
        token = refs[2 * nb + 2]
        x, y, c, chips = _place()
        for k, (src, dst, dev, _) in enumerate(plan(ins, x, y, c, chips)):
            _rcopy(src, dst, ssem.at[k], rsem.at[k], dev).start()
        token[...] = jnp.zeros_like(token)

    res = pl.pallas_call(
        body,
        name=name,
        out_shape=(pltpu.SemaphoreType.DMA((n,)), pltpu.SemaphoreType.DMA((n,)),
                   *[pltpu.HBM(b.shape, b.dtype) for b in bufs], jax.ShapeDtypeStruct((8, LANE), F32)),
        in_specs=[HBM_SPEC] * nb,
        out_specs=(SEM_SPEC, SEM_SPEC, *[HBM_SPEC] * nb, pl.BlockSpec(memory_space=pltpu.VMEM)),
        input_output_aliases={i: 2 + i for i in range(nb)},
        compiler_params=pltpu.CompilerParams(has_side_effects=SIDE_EFFECT),
    )(*[pltpu.with_memory_space_constraint(b, pltpu.HBM) for b in bufs])
    return dict(sems=res[:2], bufs=list(res[2:2 + nb]), token=res[2 + nb], plan=plan, n=n)


def _xfer_wait(name, started, after):
    bufs, plan = started["bufs"], started["plan"]
    nb = len(bufs)

    def body(*refs):
        ins = refs[:nb]
        ssem, rsem = refs[nb], refs[nb + 1]
        x, y, c, chips = _place()
        for k, (src, _, dev, land) in enumerate(plan(ins, x, y, c, chips)):
            cp = _rcopy(src, land, ssem.at[k], rsem.at[k], dev)
            cp.wait_send()
            cp.wait_recv()

    res = pl.pallas_call(
        body,
        name=name,
        out_shape=tuple(pltpu.HBM(b.shape, b.dtype) for b in bufs),
        in_specs=[HBM_SPEC] * nb + [SEM_SPEC, SEM_SPEC, pl.BlockSpec(memory_space=pl.ANY)],
        out_specs=tuple([HBM_SPEC] * nb),
        input_output_aliases={i: i for i in range(nb)},
        compiler_params=pltpu.CompilerParams(has_side_effects=SIDE_EFFECT),
    )(*bufs, *started["sems"], after)
    return list(res)


def _plan_gather_chips(refs, x, y, c, chips):
    shard, land = refs
    mine = land.at[4 * x + 2 * y + c]
    plan = [(shard, mine, (x, y, 1 - c), land.at[4 * x + 2 * y + (1 - c)])]
    for px, py in chips:
        plan.append((shard, mine, (px, py, c), land.at[4 * px + 2 * py + c]))
    return plan


def _plan_gather_pass(refs, x, y, c, chips):
    (land,) = refs
    return [(land.at[4 * px + 2 * py + c], land.at[4 * px + 2 * py + c], (x, y, 1 - c),
             land.at[4 * px + 2 * py + (1 - c)]) for px, py in chips]


def _plan_reduce_core(refs, x, y, c, chips):
    grad, recv = refs
    return [(grad.at[2 * q + (1 - c)], recv.at[q], (x, y, 1 - c), recv.at[q]) for q in range(N_CHIP)]


def _plan_reduce_chips(refs, x, y, c, chips):
    part, land = refs
    return [(part.at[2 * px + py], land.at[2 * x + y], (px, py, c), land.at[2 * px + py]) for px, py in chips]


def _chip_sums(name, ids, grad, recv):
    _, R, C = grad.shape
    tr = _tile(R, 256)

    def body(ids_ref, g_ref, r_ref, o_ref):
        o_ref[...] = (g_ref[...] + r_ref[...]).astype(BF16)

    return pl.pallas_call(
        body,
        name=name,
        grid_spec=pltpu.PrefetchScalarGridSpec(
            num_scalar_prefetch=1,
            grid=(N_CHIP, R // tr),
            in_specs=[pl.BlockSpec((None, tr, C), lambda q, r, ids: (2 * q + ids[2], r, 0)),
                      pl.BlockSpec((None, tr, C), lambda q, r, ids: (q, r, 0))],
            out_specs=pl.BlockSpec((None, tr, C), lambda q, r, ids: (q, r, 0)),
        ),
        out_shape=jax.ShapeDtypeStruct((N_CHIP, R, C), BF16),
        compiler_params=_params(2),
    )(ids, grad, recv)


def _adamw(w, g, m, v):
    m = ADAM_B1 * m + (1.0 - ADAM_B1) * g
    v = ADAM_B2 * v + (1.0 - ADAM_B2) * (g * g)
    m_hat = m / (1.0 - ADAM_B1 ** ADAM_STEP)
    v_hat = v / (1.0 - ADAM_B2 ** ADAM_STEP)
    delta = -ADAM_LR * (m_hat / (jnp.sqrt(v_hat) + ADAM_EPS) + ADAM_WD * w)
    return delta, m, v


def _reduce_adamw(name, ids, grad, recv, landed, w, m, v):
    _, R, C = grad.shape
    tr = _tile(R, 256)

    def body(ids_ref, g_ref, r_ref, l1_ref, l2_ref, l3_ref, w_ref, m_ref, v_ref, go_ref, do_ref, mo_ref, vo_ref):
        g = g_ref[...] + r_ref[...]
        g = g + l1_ref[...].astype(F32)
        g = g + l2_ref[...].astype(F32)
        g = g + l3_ref[...].astype(F32)
        delta, mn, vn = _adamw(w_ref[...], g, m_ref[...], v_ref[...])
        go_ref[...] = g
        do_ref[...] = delta
        mo_ref[...] = mn
        vo_ref[...] = vn

    def pick(k):
        return pl.BlockSpec((None, tr, C), lambda r, ids: (ids[k], r, 0))

    flat = pl.BlockSpec((tr, C), lambda r, ids: (r, 0))
    shp = jax.ShapeDtypeStruct((R, C), F32)
    return pl.pallas_call(
        body,
        name=name,
        grid_spec=pltpu.PrefetchScalarGridSpec(
            num_scalar_prefetch=1,
            grid=(R // tr,),
            in_specs=[pick(0), pick(1), pick(3), pick(4), pick(5), flat, flat, flat],
            out_specs=[flat, flat, flat, flat],
        ),
        out_shape=[shp, shp, shp, shp],
        compiler_params=_params(1),
    )(ids, grad, recv, landed, landed, landed, w, m, v)


def _small_adamw(packs, w, m, v):
    def body(p_ref, w_ref, m_ref, v_ref, g_ref, d_ref, mo_ref, vo_ref):
        g = p_ref[0]
        for dvc in range(1, N_DEV):
            g = g + p_ref[dvc]
        delta, mn, vn = _adamw(w_ref[...], g, m_ref[...], v_ref[...])
        g_ref[...] = g
        d_ref[...] = delta
        mo_ref[...] = mn
        vo_ref[...] = vn

    shp = jax.ShapeDtypeStruct(w.shape, F32)
    return pl.pallas_call(
        body,
        name="small_adamw",
        in_specs=[_full(packs.shape), _full(w.shape), _full(w.shape), _full(w.shape)],
        out_specs=[_full(w.shape)] * 4,
        out_shape=[shp] * 4,
        grid=(1,),
        compiler_params=_params(1),
    )(packs, w, m, v)


def _w_in_full(gathered):
    nb, D, cs = gathered.shape
    w = gathered.transpose(1, 0, 2).reshape(D, nb * cs)
    return jnp.concatenate([w[:, :CONV_COLS], jnp.zeros((D, HALF_P - CONV_COLS), w.dtype),
                            w[:, CONV_COLS:], jnp.zeros((D, HALF_P - GLA_COLS), w.dtype)], axis=1)


def _w_in_blocks(dw):
    D = dw.shape[0]
    g = jnp.concatenate([dw[:, :CONV_COLS], dw[:, HALF_P:HALF_P + GLA_COLS]], axis=1)
    return g.reshape(D, N_DEV, D_IN_PROJ // N_DEV).transpose(1, 0, 2)


def _rows(vec, n_rows):
    flat = jnp.pad(vec.reshape(-1), (0, n_rows * SP_COLS - vec.size))
    return flat.reshape(n_rows, SP_COLS)


def _pad_cols(a):
    return jnp.pad(a, ((0, 0), (0, SP_COLS - a.shape[1])))


R_CONV_W, R_CONV_G, R_GATE_B, R_GLA_G, R_LN1_G, R_LN1_B, R_LN2_G, R_LN2_B, R_LOSS, R_GATE_W = 0, 3, 4, 5, 6, 8, 10, 12, 14, 16


def _pack(conv_w, conv_g, gate_b, gla_g, ln1_g, ln1_b, ln2_g, ln2_b, loss, gate_w):
    z = jnp.zeros((1, SP_COLS), F32)
    parts = [_pad_cols(conv_w), _pad_cols(conv_g), _pad_cols(gate_b), _pad_cols(gla_g),
             _rows(ln1_g, 2), _rows(ln1_b, 2), _rows(ln2_g, 2), _rows(ln2_b, 2),
             z if loss is None else _pad_cols(jnp.sum(loss, axis=1, keepdims=True)), z, _pad_cols(gate_w)]
    return jnp.concatenate(parts, axis=0)


def _unpack(p, D, conv_cols, gate_cols):
    return dict(
        conv_w=p[R_CONV_W:R_CONV_W + 3, :conv_cols], conv_norm_g=p[R_CONV_G:R_CONV_G + 1, :D_CONV],
        gate_bias=p[R_GATE_B:R_GATE_B + 1, :D_GLA_K], gla_norm_g=p[R_GLA_G:R_GLA_G + 1, :D_GLA_V],
        ln1_g=p[R_LN1_G:R_LN1_G + 2].reshape(1, -1)[:, :D], ln1_b=p[R_LN1_B:R_LN1_B + 2].reshape(1, -1)[:, :D],
        ln2_g=p[R_LN2_G:R_LN2_G + 2].reshape(1, -1)[:, :D], ln2_b=p[R_LN2_B:R_LN2_B + 2].reshape(1, -1)[:, :D],
        w_gate_up=p[R_GATE_W:R_GATE_W + GATE_RANK, :gate_cols])


BIG = ("w_in", "w_out", "w_ff_up", "w_ff_down")
ORDER = ("w_in", "conv_w", "conv_norm_g", "w_gate_up", "gate_bias", "gla_norm_g", "w_out", "ln1_g", "ln1_b",
         "w_ff_up", "w_ff_down", "ln2_g", "ln2_b")


def kernel(x, w_in, conv_w, conv_norm_g, w_gate_up, gate_bias, gla_norm_g, w_out, ln1_g, ln1_b, w_ff_up, w_ff_down, ln2_g, ln2_b, loss_target, m_w_in, m_conv_w, m_conv_norm_g, m_w_gate_up, m_gate_bias, m_gla_norm_g, m_w_out, m_ln1_g, m_ln1_b, m_w_ff_up, m_w_ff_down, m_ln2_g, m_ln2_b, v_w_in, v_conv_w, v_conv_norm_g, v_w_gate_up, v_gate_bias, v_gla_norm_g, v_w_out, v_ln1_g, v_ln1_b, v_w_ff_up, v_w_ff_down, v_ln2_g, v_ln2_b):
    T, D = x.shape[1], x.shape[2]
    xs, target = x[0], loss_target[0]
    xi, yi, ci = lax.axis_index("x"), lax.axis_index("y"), lax.axis_index("c")
    chip = 2 * xi + yi
    dev = 2 * chip + ci
    others = [jnp.where(chip <= q, q + 1, q) for q in range(N_CHIP - 1)]
    ids = jnp.stack([dev, chip, ci] + others).astype(jnp.int32)
    conv_cols, gate_cols = conv_w.shape[2], w_gate_up.shape[2]

    z1 = jnp.zeros((1, 1), F32)
    fwd_pack = _pack(conv_w[0], z1, z1, z1, z1, z1, z1, z1, None, w_gate_up[0])
    g_in, g_pack = _all_gather("gather_w_in", [w_in[0].astype(BF16), fwd_pack])
    conv_w_full = g_pack[:, R_CONV_W:R_CONV_W + 3, :conv_cols].transpose(1, 0, 2).reshape(3, -1)
    gate_w_full = g_pack[:, R_GATE_W:R_GATE_W + GATE_RANK, :gate_cols].transpose(1, 0, 2).reshape(GATE_RANK, -1)
    conv_w8 = jnp.pad(conv_w_full, ((0, 5), (0, 0)))
    wg128 = jnp.pad(gate_w_full, ((0, LANE - GATE_RANK), (0, 0))).astype(BF16)

    shards = [w_out[0].astype(BF16), w_ff_up[0].astype(BF16), w_ff_down[0].astype(BF16)]
    lands = _place_own("place_own", shards, g_pack)
    ga = [_xfer_start("gather_chips_" + nm, [sh, ld], _plan_gather_chips, 4)
          for nm, sh, ld in zip(BIG[1:], shards, lands)]

    def pass_on(nm, started, after):
        (_, land) = _xfer_wait("gather_chips_wait_" + nm, started, after)
        return _xfer_start("gather_pass_" + nm, [land], _plan_gather_pass, 3)

    def landed(nm, started, after):
        return _xfer_wait("gather_pass_wait_" + nm, started, after)[0]

    w_full = _w_in_full(g_in)
    proj = _proj_fwd(xs, w_full, deps=[g["token"] for g in ga])
    yin = _conv_fwd(proj, conv_w8, conv_norm_g)
    gp_out = pass_on("w_out", ga[0], yin)
    o_all, states, yin = _gla_fwd(proj, wg128, gate_bias, gla_norm_g, yin, deps=[gp_out["token"]])
    w_out_full = landed("w_out", gp_out, o_all).reshape(-1, D)
    gp_up = pass_on("w_ff_up", ga[1], o_all)
    xhat1, x1, rstd1 = _mix_ln1(yin, w_out_full, xs, ln1_g, ln1_b, deps=[gp_up["token"]])
    w_up_blk = landed("w_ff_up", gp_up, x1)
    gp_down = pass_on("w_ff_down", ga[2], x1)
    ra, h2 = _ff_up(x1, w_up_blk, deps=[gp_down["token"]])
    w_down_full = landed("w_ff_down", gp_down, ra).reshape(-1, D)
    dh3, dh3b, g_ln2_g, g_ln2_b, loss = _ff_down_loss(h2, w_down_full, xhat1, target, ln1_g, ln1_b, ln2_g, ln2_b)

    def to_core(nm, grad):
        recv = lax.empty((N_CHIP,) + grad.shape[1:], F32)
        return _xfer_start("reduce_core_" + nm, [grad, recv], _plan_reduce_core, N_CHIP)

    def to_chips(nm, started, after):
        grad, recv = _xfer_wait("reduce_core_wait_" + nm, started, after)
        part = _chip_sums("chip_sums_" + nm, ids, grad, recv)
        land = lax.empty(part.shape, BF16)
        return grad, recv, _xfer_start("reduce_chips_" + nm, [part, land], _plan_reduce_chips, N_CHIP - 1)

    da = _ff_down_bwd_act(dh3b, w_down_full, ra)
    gw_down = _grad_w("grad_w_down", h2, dh3b).reshape(N_DEV, -1, D)
    rc_down = to_core("w_ff_down", gw_down)
    gw_up = _grad_w_up_blk(x1, da, N_DEV, deps=[rc_down["token"]])
    gw_down, rv_down, rs_down = to_chips("w_ff_down", rc_down, gw_up)
    rc_up = to_core("w_ff_up", gw_up)
    dh1, dh1b, g_ln1_g, g_ln1_b = _ff_up_bwd_ln1(da, w_up_blk, dh3, xhat1, rstd1, ln1_g,
                                                 deps=[rs_down["token"], rc_up["token"]])
    gw_up, rv_up, rs_up = to_chips("w_ff_up", rc_up, dh1b)
    dyin = _mix_bwd(dh1b, w_out_full, deps=[rs_up["token"]])
    gw_out = _grad_w("grad_w_out", yin, dh1b).reshape(N_DEV, -1, D)
    rc_out = to_core("w_out", gw_out)
    dproj, g_conv_w, g_conv_g = _conv_bwd(proj, dyin, conv_w8, conv_norm_g)
    dproj, g_gate_w, g_gate_b, g_gla_g = _gla_bwd(proj, wg128, gate_bias, gla_norm_g, o_all, states, dyin, dproj)
    gw_in = _w_in_blocks(_grad_w("grad_w_in", xs, dproj, a_fn=_to_bf16, tn_pref=1280, deps=[rc_out["token"]]))
    rc_in = to_core("w_in", gw_in)
    gw_out, rv_out, rs_out = to_chips("w_out", rc_out, gw_in)
    grad_x = _proj_bwd_x(dproj, w_full, dh1, deps=[rc_in["token"], rs_out["token"]])

    pack = _pack(g_conv_w[:3], g_conv_g, g_gate_b, g_gla_g, g_ln1_g, g_ln1_b, g_ln2_g, g_ln2_b, loss,
                 g_gate_w[:GATE_RANK])
    (packs,) = _all_gather("gather_small_grads", [pack], deps=[grad_x])
    gw_in, rv_in, rs_in = to_chips("w_in", rc_in, packs)

    big = {}
    after = packs
    for nm, grad, recv, started, w, m, v in (
            ("w_ff_down", gw_down, rv_down, rs_down, w_ff_down, m_w_ff_down, v_w_ff_down),
            ("w_ff_up", gw_up, rv_up, rs_up, w_ff_up, m_w_ff_up, v_w_ff_up),
            ("w_out", gw_out, rv_out, rs_out, w_out, m_w_out, v_w_out),
            ("w_in", gw_in, rv_in, rs_in, w_in, m_w_in, v_w_in)):
        _, land = _xfer_wait("reduce_chips_wait_" + nm, started, after)
        res = _reduce_adamw("adamw_" + nm, ids, grad, recv, land, w[0], m[0], v[0])
        big[nm] = [a[None] for a in res]
        after = res[0]

    def own_cols(row, n_rows, width):
        cut = lax.dynamic_slice(packs, (0, row, dev * width), (N_DEV, n_rows, width))
        return jnp.pad(cut, ((0, 0), (0, 0), (0, SP_COLS - width)))

    packs_own = jnp.concatenate([own_cols(R_CONV_W, 3, conv_cols), packs[:, R_CONV_W + 3:R_GATE_W],
                                 own_cols(R_GATE_W, GATE_RANK, gate_cols)], axis=1)

    def small_pack(cw, cg, gw, gb, gg, l1g, l1b, l2g, l2b):
        return _pack(cw[0], cg, gb, gg, l1g, l1b, l2g, l2b, None, gw[0])

    w_s = small_pack(conv_w, conv_norm_g, w_gate_up, gate_bias, gla_norm_g, ln1_g, ln1_b, ln2_g, ln2_b)
    m_s = small_pack(m_conv_w, m_conv_norm_g, m_w_gate_up, m_gate_bias, m_gla_norm_g, m_ln1_g, m_ln1_b, m_ln2_g, m_ln2_b)
    v_s = small_pack(v_conv_w, v_conv_norm_g, v_w_gate_up, v_gate_bias, v_gla_norm_g, v_ln1_g, v_ln1_b, v_ln2_g, v_ln2_b)
    g_s, d_s, mn_s, vn_s = _small_adamw(packs_own, w_s, m_s, v_s)
    small = [_unpack(p, D, conv_cols, gate_cols) for p in (g_s, d_s, mn_s, vn_s)]

    def leaf(kind, name):
        if name in BIG:
            return big[name][kind]
        a = small[kind][name]
        return a[None] if name in ("conv_w", "w_gate_up") else a

    out = [g_s[R_LOSS, 0], grad_x[None]]
    for kind in range(4):
        out += [leaf(kind, nm) for nm in ORDER]
    return tuple(out)
```

```python
import jax
import jax.numpy as jnp
from jax import lax
from jax.experimental import pallas as pl
from jax.experimental.pallas import tpu as pltpu

F32 = jnp.float32
BF16 = jnp.bfloat16

D_CONV = 1024
CONV_GROUPS = 8
GLA_HEADS = 4
HEAD_K = 128
HEAD_V = 256
D_GLA_K = 512
D_GLA_V = 1024
GATE_RANK = 16
GATE_TAU = 16.0
CHUNK = 64
LN_EPS = 1e-5
RMS_EPS = 1e-6
DN_ALPHA = 2.0 ** 0.25
D_IN_PROJ = 6160
ADAM_LR = 0.001
ADAM_B1 = 0.9
ADAM_B2 = 0.999
ADAM_EPS = 1e-08
ADAM_WD = 0.01
ADAM_STEP = 10

N_DEV = 8
N_CHIP = 4
LANE = 128
HALF_P = 3200
P_INT = 2 * HALF_P
CONV_COLS = 3 * D_CONV
GLA_COLS = D_IN_PROJ - CONV_COLS
SP_ROWS = 32
SP_COLS = 1024
VMEM_LIMIT = 56 * 1024 * 1024

NN = ((1,), (0,))
NT = ((1,), (1,))
TN = ((0,), (0,))
MESH = pl.DeviceIdType.MESH


def _dot(a, b, dims, precision=None):
    return lax.dot_general(a, b, (dims, ((), ())), preferred_element_type=F32, precision=precision)


def _tile(n, pref):
    if n <= pref:
        return n
    t = (pref // LANE) * LANE
    while t > 0 and n % t:
        t -= LANE
    assert t > 0, (n, pref)
    return t


def _params(n_axes):
    return pltpu.CompilerParams(dimension_semantics=("arbitrary",) * n_axes, vmem_limit_bytes=VMEM_LIMIT)


def _full(shape):
    nd = len(shape)
    return pl.BlockSpec(shape, lambda *_: (0,) * nd)


def _hbm_specs(n):
    return [pl.BlockSpec(memory_space=pl.ANY)] * n


def _mm(name, mode, a, b, *, M, N, K, tm, tn, tk, outs, epilogue, extras=(), a_fn=None, a_spec=None, b_spec=None,
        deps=()):
    ni, nj, nk = M // tm, N // tn, K // tk
    assert ni * tm == M and nj * tn == N and nk * tk == K, (name, M, N, K, tm, tn, tk)
    if a_spec is None:
        a_spec = (pl.BlockSpec((tk, tm), lambda i, j, k: (k, i)) if mode == "tn"
                  else pl.BlockSpec((tm, tk), lambda i, j, k: (i, k)))
    if b_spec is None:
        b_spec = (pl.BlockSpec((tn, tk), lambda i, j, k: (j, k)) if mode == "nt"
                  else pl.BlockSpec((tk, tn), lambda i, j, k: (k, j)))
    dims = {"nn": NN, "nt": NT, "tn": TN}[mode]
    n_ex, n_out, n_dep = len(extras), len(outs), len(deps)

    def body(*refs):
        a_ref, b_ref = refs[0], refs[1]
        ex = refs[2:2 + n_ex]
        o = refs[2 + n_ex + n_dep:2 + n_ex + n_dep + n_out]
        acc_ref = refs[2 + n_ex + n_dep + n_out]
        i, j, k = pl.program_id(0), pl.program_id(1), pl.program_id(2)
        av = a_ref[...]
        if a_fn is not None:
            av = a_fn(av)
        part = _dot(av, b_ref[...], dims)
        if nk == 1:
            acc_ref[...] = part
            epilogue(acc_ref, ex, o, i, j)
        else:
            @pl.when(k == 0)
            def _():
                acc_ref[...] = part

            @pl.when(k > 0)
            def _():
                acc_ref[...] += part

            @pl.when(k == nk - 1)
            def _():
                epilogue(acc_ref, ex, o, i, j)

    return pl.pallas_call(
        body,
        name=name,
        grid=(ni, nj, nk),
        in_specs=[a_spec, b_spec] + [s for _, s in extras] + _hbm_specs(n_dep),
        out_specs=[s for _, s in outs],
        out_shape=[s for s, _ in outs],
        scratch_shapes=[pltpu.VMEM((tm, tn), F32)],
        compiler_params=_params(3),
    )(a, b, *[x for x, _ in extras], *deps)


def _store(dtype):
    def ep(acc_ref, ex, o, i, j):
        o[0][...] = acc_ref[...].astype(dtype)
    return ep


def _to_bf16(v):
    return v.astype(BF16)


def _row_chunks(tm):
    rc = 64 if tm % 64 == 0 else tm
    return rc, tm // rc


def _ln_bwd(dy, xhat, rstd, g):
    dxh = dy * g
    m1 = jnp.mean(dxh, axis=-1, keepdims=True)
    m2 = jnp.mean(dxh * xhat, axis=-1, keepdims=True)
    return rstd * (dxh - m1 - xhat * m2)


def _ln_fwd(h):
    mu = jnp.mean(h, axis=-1, keepdims=True)
    xc = h - mu
    var = jnp.mean(xc * xc, axis=-1, keepdims=True)
    rstd = lax.rsqrt(var + LN_EPS)
    return xc * rstd, rstd


def _proj_fwd(x, w_full, deps=()):
    T, D = x.shape
    P = w_full.shape[1]
    tm, tn = _tile(T, 512), _tile(P, 1280)
    return _mm("proj_fwd", "nn", x, w_full, M=T, N=P, K=D, tm=tm, tn=tn, tk=D,
               outs=[(jax.ShapeDtypeStruct((T, P), F32), pl.BlockSpec((tm, tn), lambda i, j, k: (i, j)))],
               epilogue=_store(F32), a_fn=_to_bf16, deps=deps)[0]


def _conv_shift(h, hp):
    row = lax.broadcasted_iota(jnp.int32, h.shape, 0)
    hm1 = hp[7:8, :]
    hm2 = hp[6:7, :]
    h1 = jnp.where(row == 0, hm1, pltpu.roll(h, 1, 0))
    h2 = jnp.where(row == 0, hm2, jnp.where(row == 1, hm1, pltpu.roll(h, 2, 0)))
    return h1, h2


def _conv_fwd(proj, conv_w8, conv_g):
    T = proj.shape[0]
    tt = _tile(T, 256)
    nt = T // tt
    t8 = tt // 8

    def body(b_ref, c_ref, u_ref, cp_ref, up_ref, w_ref, g_ref, yin_ref):
        i = pl.program_id(0)
        h = c_ref[...] * u_ref[...]
        hp = jnp.where(i > 0, cp_ref[...] * up_ref[...], 0.0)
        h1, h2 = _conv_shift(h, hp)
        w = w_ref[...]
        y = w[0:1, :] * h2 + w[1:2, :] * h1 + w[2:3, :] * h
        p = b_ref[...] * y
        parts = []
        for gi in range(CONV_GROUPS):
            pg = p[:, gi * LANE:(gi + 1) * LANE]
            r = lax.rsqrt(jnp.mean(pg * pg, axis=-1, keepdims=True) + RMS_EPS)
            parts.append(pg * r)
        yn = jnp.concatenate(parts, axis=1) * g_ref[...]
        yin_ref[...] = yn.astype(BF16)

    def col(cidx):
        return pl.BlockSpec((tt, D_CONV), lambda i: (i, cidx))

    def prev(cidx):
        return pl.BlockSpec((8, D_CONV), lambda i: (jnp.maximum(i * t8 - 1, 0), cidx))

    return pl.pallas_call(
        body,
        name="conv_fwd",
        grid=(nt,),
        in_specs=[col(0), col(1), col(2), prev(1), prev(2), _full((8, D_CONV)), _full((1, D_CONV))],
        out_specs=pl.BlockSpec((tt, D_CONV), lambda i: (i, 0)),
        out_shape=jax.ShapeDtypeStruct((T, 2 * D_CONV), BF16),
        compiler_params=_params(1),
    )(proj, proj, proj, proj, proj, conv_w8, conv_g)


def _log_sigmoid(z):
    return jnp.minimum(z, 0.0) - jnp.log(1.0 + jnp.exp(-jnp.abs(z)))


def _gla_chunk_terms(blk, wg_ref, gb_ref):
    q = blk[:, 0:512]
    k = blk[:, 512:1024]
    zl = blk[:, 3072:3200]
    z = _dot(zl.astype(BF16), wg_ref[...], NN) + gb_ref[...]
    log_a = _log_sigmoid(z) * (1.0 / GATE_TAU)
    ri = lax.broadcasted_iota(jnp.int32, (CHUNK, CHUNK), 0)
    ci = lax.broadcasted_iota(jnp.int32, (CHUNK, CHUNK), 1)
    causal = ri >= ci
    lower = causal.astype(F32)
    bcum = _dot(lower, log_a, NN, precision=lax.Precision.HIGHEST)
    return q, k, zl, z, bcum, causal


def _gla_head_terms(q, k, bcum, h):
    sl = slice(h * HEAD_K, (h + 1) * HEAD_K)
    bh = bcum[:, sl]
    bl = bh[CHUNK - 1:CHUNK, :]
    eb = jnp.exp(bh)
    enb = jnp.exp(-bh)
    eend = jnp.exp(bl - bh)
    dec = jnp.exp(bl)
    qd = q[:, sl] * (HEAD_K ** -0.5) * eb
    ki = k[:, sl] * enb
    ke = k[:, sl] * eend
    return eb, enb, eend, dec, qd, ki, ke


def _sigmoid(x):
    return 1.0 / (1.0 + jnp.exp(-x))


def _gla_fwd(proj, wg128, gbias, gng, yin, deps=()):
    T = proj.shape[0]
    nch = T // CHUNK

    def body(p_ref, wg_ref, gb_ref, gn_ref, yin_in_ref, *rest):
        o_ref, st_ref, yin_ref, s_ref = rest[len(deps):]
        n = pl.program_id(0)

        @pl.when(n == 0)
        def _():
            s_ref[...] = jnp.zeros_like(s_ref)

        blk = p_ref[...]
        q, k, _, _, bcum, causal = _gla_chunk_terms(blk, wg_ref, gb_ref)
        v = blk[:, 1024:2048]
        r = blk[:, 2048:3072]
        gn = gn_ref[...]
        for h in range(GLA_HEADS):
            _, _, _, dec, qd, ki, ke = _gla_head_terms(q, k, bcum, h)
            vs = slice(h * HEAD_V, (h + 1) * HEAD_V)
            vb = v[:, vs].astype(BF16)
            qdb = qd.astype(BF16)
            a = jnp.where(causal, _dot(qdb, ki.astype(BF16), NT), 0.0)
            st = s_ref[h]
            o = _dot(a.astype(BF16), vb, NN) + _dot(qdb, st.astype(BF16), NT)
            st_ref[h] = st
            s_ref[h] = dec * st + _dot(vb, ke.astype(BF16), TN)
            o_ref[:, vs] = o
            rinv = lax.rsqrt(jnp.mean(o * o, axis=-1, keepdims=True) + RMS_EPS)
            rh = r[:, vs]
            yin_ref[:, vs] = (o * rinv * gn[:, vs] * (rh * _sigmoid(rh))).astype(BF16)

    return pl.pallas_call(
        body,
        name="gla_fwd",
        grid=(nch,),
        in_specs=[pl.BlockSpec((CHUNK, HALF_P), lambda n: (n, 1)), _full((LANE, D_GLA_K)), _full((1, D_GLA_K)),
                  _full((1, D_GLA_V)), pl.BlockSpec(memory_space=pl.ANY)] + _hbm_specs(len(deps)),
        out_specs=[pl.BlockSpec((CHUNK, D_GLA_V), lambda n: (n, 0)),
                   pl.BlockSpec((None, GLA_HEADS, HEAD_V, HEAD_K), lambda n: (n, 0, 0, 0)),
                   pl.BlockSpec((CHUNK, D_GLA_V), lambda n: (n, 1))],
        out_shape=[jax.ShapeDtypeStruct((T, D_GLA_V), F32),
                   jax.ShapeDtypeStruct((nch, GLA_HEADS, HEAD_V, HEAD_K), F32),
                   jax.ShapeDtypeStruct(yin.shape, BF16)],
        scratch_shapes=[pltpu.VMEM((GLA_HEADS, HEAD_V, HEAD_K), F32)],
        input_output_aliases={4: 2},
        compiler_params=_params(1),
    )(proj, wg128, gbias, gng, yin, *deps)


def _mix_ln1(yin, w_out, x, ln_g, ln_b, deps=()):
    T, D = x.shape
    KY = yin.shape[1]
    tm = _tile(T, 512)
    rc, nrc = _row_chunks(tm)

    def ep(acc_ref, ex, o, i, j):
        x_ref, g_ref, b_ref = ex
        xhat_ref, x1_ref, rstd_ref = o
        g, b = g_ref[...], b_ref[...]

        def chunk(r, carry):
            rows = pl.ds(pl.multiple_of(r * rc, rc), rc)
            xhat, rstd = _ln_fwd(DN_ALPHA * x_ref[rows, :] + acc_ref[rows, :])
            xhat_ref[rows, :] = xhat
            x1_ref[rows, :] = (xhat * g + b).astype(BF16)
            rstd_ref[rows, :] = rstd
            return carry

        lax.fori_loop(0, nrc, chunk, 0)

    row = lambda i, j, k: (i, 0)
    vec = _full((1, D))
    return _mm("mix_ln1", "nn", yin, w_out, M=T, N=D, K=KY, tm=tm, tn=D, tk=_tile(KY, 1024),
               outs=[(jax.ShapeDtypeStruct((T, D), F32), pl.BlockSpec((tm, D), row)),
                     (jax.ShapeDtypeStruct((T, D), BF16), pl.BlockSpec((tm, D), row)),
                     (jax.ShapeDtypeStruct((T, 1), F32), pl.BlockSpec((tm, 1), row))],
               extras=[(x, pl.BlockSpec((tm, D), row)), (ln_g, vec), (ln_b, vec)],
               epilogue=ep, deps=deps)


def _ff_up(x1, w_up_blk, deps=()):
    T, D = x1.shape
    nb, _, fb = w_up_blk.shape
    tm = _tile(T, 1024)

    def ep(acc_ref, ex, o, i, j):
        ra = jnp.maximum(acc_ref[...], 0.0)
        o[0][...] = ra.astype(BF16)
        o[1][...] = (ra * ra).astype(BF16)

    blk = pl.BlockSpec((tm, fb), lambda i, j, k: (i, j))
    shp = jax.ShapeDtypeStruct((T, nb * fb), BF16)
    return _mm("ff_up", "nn", x1, w_up_blk, M=T, N=nb * fb, K=D, tm=tm, tn=fb, tk=D,
               b_spec=pl.BlockSpec((None, D, fb), lambda i, j, k: (j, 0, 0)),
               outs=[(shp, blk), (shp, blk)], epilogue=ep, deps=deps)


def _ff_down_loss(h2, w_down, xhat1, target, g1, b1, g2, b2):
    T, F = h2.shape
    D = w_down.shape[1]
    tm = _tile(T, 512)
    rc, nrc = _row_chunks(tm)
    inv_d = 1.0 / D

    def ep(acc_ref, ex, o, i, j):
        xh1_ref, t_ref, g1_ref, b1_ref, g2_ref, b2_ref = ex
        dh_ref, dhb_ref, gg_ref, gb_ref, loss_ref = o
        g1v, b1v, g2v, b2v = g1_ref[...], b1_ref[...], g2_ref[...], b2_ref[...]

        @pl.when(i == 0)
        def _():
            gg_ref[...] = jnp.zeros_like(gg_ref)
            gb_ref[...] = jnp.zeros_like(gb_ref)
            loss_ref[...] = jnp.zeros_like(loss_ref)

        def chunk(r, carry):
            sg, sb, sl = carry
            rows = pl.ds(pl.multiple_of(r * rc, rc), rc)
            x1 = xh1_ref[rows, :] * g1v + b1v
            xhat, rstd = _ln_fwd(DN_ALPHA * x1 + acc_ref[rows, :])
            e = xhat * g2v + b2v - t_ref[rows, :]
            dy = e * inv_d
            dh = _ln_bwd(dy, xhat, rstd, g2v)
            dh_ref[rows, :] = dh
            dhb_ref[rows, :] = dh.astype(BF16)
            sg = sg + jnp.sum(dy * xhat, axis=0, keepdims=True)
            sb = sb + jnp.sum(dy, axis=0, keepdims=True)
            sl = sl + jnp.sum(e * e, axis=0, keepdims=True)
            return sg, sb, sl

        z = jnp.zeros((1, D), F32)
        sg, sb, sl = lax.fori_loop(0, nrc, chunk, (z, z, z))
        gg_ref[...] += sg
        gb_ref[...] += sb
        loss_ref[...] += sl * (0.5 * inv_d)

    row = lambda i, j, k: (i, 0)
    vec = _full((1, D))
    vshape = jax.ShapeDtypeStruct((1, D), F32)
    return _mm("ff_down_loss", "nn", h2, w_down, M=T, N=D, K=F, tm=tm, tn=D, tk=_tile(F, 1024),
               outs=[(jax.ShapeDtypeStruct((T, D), F32), pl.BlockSpec((tm, D), row)),
                     (jax.ShapeDtypeStruct((T, D), BF16), pl.BlockSpec((tm, D), row)),
                     (vshape, vec), (vshape, vec), (vshape, vec)],
               extras=[(xhat1, pl.BlockSpec((tm, D), row)), (target, pl.BlockSpec((tm, D), row)),
                       (g1, vec), (b1, vec), (g2, vec), (b2, vec)],
               epilogue=ep)


def _ff_down_bwd_act(dh3b, w_down, ra):
    T, D = dh3b.shape
    F = w_down.shape[0]
    tm, tn = _tile(T, 1024), _tile(F, 1024)

    def ep(acc_ref, ex, o, i, j):
        o[0][...] = (acc_ref[...] * (2.0 * ex[0][...].astype(F32))).astype(BF16)

    blk = pl.BlockSpec((tm, tn), lambda i, j, k: (i, j))
    return _mm("ff_down_bwd_act", "nt", dh3b, w_down, M=T, N=F, K=D, tm=tm, tn=tn, tk=D,
               outs=[(jax.ShapeDtypeStruct((T, F), BF16), blk)], extras=[(ra, blk)], epilogue=ep)[0]


def _grad_w(name, a, b, *, a_fn=None, tm_pref=1024, tn_pref=1024, deps=()):
    T, M = a.shape
    N = b.shape[1]
    tm, tn, tk = _tile(M, tm_pref), _tile(N, tn_pref), _tile(T, 1024)
    return _mm(name, "tn", a, b, M=M, N=N, K=T, tm=tm, tn=tn, tk=tk, a_fn=a_fn, deps=deps,
               outs=[(jax.ShapeDtypeStruct((M, N), F32), pl.BlockSpec((tm, tn), lambda i, j, k: (i, j)))],
               epilogue=_store(F32))[0]


def _grad_w_up_blk(x1, da, nb, deps=()):
    T, D = x1.shape
    F = da.shape[1]
    fb = F // nb
    tm, tk = _tile(D, 1024), _tile(T, 1024)
    return _mm("grad_w_up", "tn", x1, da, M=D, N=F, K=T, tm=tm, tn=fb, tk=tk, deps=deps,
               outs=[(jax.ShapeDtypeStruct((nb, D, fb), F32),
                      pl.BlockSpec((None, tm, fb), lambda i, j, k: (j, i, 0)))],
               epilogue=_store(F32))[0]


def _ff_up_bwd_ln1(da, w_up_blk, dh3, xhat1, rstd1, g1, deps=()):
    T, F = da.shape
    nb, D, fb = w_up_blk.shape
    tm = _tile(T, 512)
    rc, nrc = _row_chunks(tm)

    def ep(acc_ref, ex, o, i, j):
        dh3_ref, xh_ref, rstd_ref, g_ref = ex
        dh_ref, dhb_ref, gg_ref, gb_ref = o
        g = g_ref[...]

        @pl.when(i == 0)
        def _():
            gg_ref[...] = jnp.zeros_like(gg_ref)
            gb_ref[...] = jnp.zeros_like(gb_ref)

        def chunk(r, carry):
            sg, sb = carry
            rows = pl.ds(pl.multiple_of(r * rc, rc), rc)
            dx1 = DN_ALPHA * dh3_ref[rows, :] + acc_ref[rows, :]
            xhat = xh_ref[rows, :]
            dh = _ln_bwd(dx1, xhat, rstd_ref[rows, :], g)
            dh_ref[rows, :] = dh
            dhb_ref[rows, :] = dh.astype(BF16)
            return sg + jnp.sum(dx1 * xhat, axis=0, keepdims=True), sb + jnp.sum(dx1, axis=0, keepdims=True)

        z = jnp.zeros((1, D), F32)
        sg, sb = lax.fori_loop(0, nrc, chunk, (z, z))
        gg_ref[...] += sg
        gb_ref[...] += sb

    row = lambda i, j, k: (i, 0)
    vec = _full((1, D))
    vshape = jax.ShapeDtypeStruct((1, D), F32)
    return _mm("ff_up_bwd_ln1", "nt", da, w_up_blk, M=T, N=D, K=F, tm=tm, tn=D, tk=fb,
               b_spec=pl.BlockSpec((None, D, fb), lambda i, j, k: (k, 0, 0)),
               outs=[(jax.ShapeDtypeStruct((T, D), F32), pl.BlockSpec((tm, D), row)),
                     (jax.ShapeDtypeStruct((T, D), BF16), pl.BlockSpec((tm, D), row)),
                     (vshape, vec), (vshape, vec)],
               extras=[(dh3, pl.BlockSpec((tm, D), row)), (xhat1, pl.BlockSpec((tm, D), row)),
                       (rstd1, pl.BlockSpec((tm, 1), row)), (g1, vec)],
               epilogue=ep, deps=deps)


def _mix_bwd(dh1b, w_out, deps=()):
    T, D = dh1b.shape
    KY = w_out.shape[0]
    tm, tn = _tile(T, 1024), _tile(KY, 1024)
    return _mm("mix_bwd", "nt", dh1b, w_out, M=T, N=KY, K=D, tm=tm, tn=tn, tk=D, deps=deps,
               outs=[(jax.ShapeDtypeStruct((T, KY), F32), pl.BlockSpec((tm, tn), lambda i, j, k: (i, j)))],
               epilogue=_store(F32))[0]


def _conv_bwd(proj, dyin, conv_w8, conv_g):
    T = proj.shape[0]
    tt = _tile(T, 256)
    nt = T // tt
    t8 = tt // 8
    nx = tt + 8

    def body(b_ref, c_ref, u_ref, d_ref, bn_ref, cn_ref, un_ref, dn_ref, cp_ref, up_ref, w_ref, g_ref,
             dp_ref, dw_ref, dg_ref):
        i = pl.program_id(0)

        @pl.when(i == 0)
        def _():
            dw_ref[...] = jnp.zeros_like(dw_ref)
            dg_ref[...] = jnp.zeros_like(dg_ref)

        more = i < nt - 1

        def ext(cur_ref, nxt_ref):
            return jnp.concatenate([cur_ref[...], jnp.where(more, nxt_ref[...], 0.0)], axis=0)

        bx, cx, ux, dx = ext(b_ref, bn_ref), ext(c_ref, cn_ref), ext(u_ref, un_ref), ext(d_ref, dn_ref)
        hx = cx * ux
        hp = jnp.where(i > 0, cp_ref[...] * up_ref[...], 0.0)
        h1, h2 = _conv_shift(hx, hp)
        w = w_ref[...]
        g = g_ref[...]
        yx = w[0:1, :] * h2 + w[1:2, :] * h1 + w[2:3, :] * hx
        px = bx * yx
        dps, dgs = [], []
        for gi in range(CONV_GROUPS):
            sl = slice(gi * LANE, (gi + 1) * LANE)
            pg, dg_ = px[:, sl], dx[:, sl]
            r = lax.rsqrt(jnp.mean(pg * pg, axis=-1, keepdims=True) + RMS_EPS)
            gd = g[:, sl] * dg_
            dps.append(r * gd - pg * (r * r * r) * jnp.mean(pg * gd, axis=-1, keepdims=True))
            dgs.append(jnp.sum((dg_ * pg * r)[:tt, :], axis=0, keepdims=True))
        dpx = jnp.concatenate(dps, axis=1)
        dg_ref[...] += jnp.concatenate(dgs, axis=1)
        dyx = dpx * bx
        dyc = dyx[:tt, :]
        dh = (w[2:3, :] * dyx + w[1:2, :] * pltpu.roll(dyx, nx - 1, 0) + w[0:1, :] * pltpu.roll(dyx, nx - 2, 0))[:tt, :]
        dw_ref[0:1, :] += jnp.sum(dyc * h2[:tt, :], axis=0, keepdims=True)
        dw_ref[1:2, :] += jnp.sum(dyc * h1[:tt, :], axis=0, keepdims=True)
        dw_ref[2:3, :] += jnp.sum(dyc * hx[:tt, :], axis=0, keepdims=True)
        dp_ref[:, 0:D_CONV] = (dpx * yx)[:tt, :].astype(BF16)
        dp_ref[:, D_CONV:2 * D_CONV] = (dh * u_ref[...]).astype(BF16)
        dp_ref[:, 2 * D_CONV:3 * D_CONV] = (dh * c_ref[...]).astype(BF16)
        dp_ref[:, 3 * D_CONV:HALF_P] = jnp.zeros((tt, HALF_P - 3 * D_CONV), BF16)

    def col(cidx):
        return pl.BlockSpec((tt, D_CONV), lambda i: (i, cidx))

    def nxt(cidx):
        return pl.BlockSpec((8, D_CONV), lambda i: (jnp.minimum((i + 1) * t8, T // 8 - 1), cidx))

    def prev(cidx):
        return pl.BlockSpec((8, D_CONV), lambda i: (jnp.maximum(i * t8 - 1, 0), cidx))

    return pl.pallas_call(
        body,
        name="conv_bwd",
        grid=(nt,),
        in_specs=[col(0), col(1), col(2), col(0), nxt(0), nxt(1), nxt(2), nxt(0), prev(1), prev(2),
                  _full((8, D_CONV)), _full((1, D_CONV))],
        out_specs=[pl.BlockSpec((tt, HALF_P), lambda i: (i, 0)), _full((8, D_CONV)), _full((1, D_CONV))],
        out_shape=[jax.ShapeDtypeStruct((T, P_INT), BF16), jax.ShapeDtypeStruct((8, D_CONV), F32),
                   jax.ShapeDtypeStruct((1, D_CONV), F32)],
        compiler_params=_params(1),
    )(proj, proj, proj, dyin, proj, proj, proj, dyin, proj, proj, conv_w8, conv_g)


def _gla_bwd(proj, wg128, gbias, gng, o_all, states, dyin, dproj):
    T = proj.shape[0]
    nch = T // CHUNK

    def body(p_ref, wg_ref, gb_ref, gn_ref, o_ref, st_ref, d_ref, dp_in_ref,
             dp_ref, dwg_ref, dgb_ref, dgn_ref, ds_ref):
        n = pl.program_id(0)

        @pl.when(n == 0)
        def _():
            ds_ref[...] = jnp.zeros_like(ds_ref)
            dwg_ref[...] = jnp.zeros_like(dwg_ref)
            dgb_ref[...] = jnp.zeros_like(dgb_ref)
            dgn_ref[...] = jnp.zeros_like(dgn_ref)

        blk = p_ref[...]
        q, k, zl, z, bcum, causal = _gla_chunk_terms(blk, wg_ref, gb_ref)
        v = blk[:, 1024:2048]
        r = blk[:, 2048:3072]
        gn = gn_ref[...]
        upper = (lax.broadcasted_iota(jnp.int32, (CHUNK, CHUNK), 0)
                 <= lax.broadcasted_iota(jnp.int32, (CHUNK, CHUNK), 1)).astype(F32)
        dlog_parts = []
        for h in range(GLA_HEADS):
            eb, enb, eend, dec, qd, ki, ke = _gla_head_terms(q, k, bcum, h)
            vs = slice(h * HEAD_V, (h + 1) * HEAD_V)
            ks = slice(h * HEAD_K, (h + 1) * HEAD_K)
            o = o_ref[:, vs]
            rh = r[:, vs]
            dyg = d_ref[:, vs]
            rinv = lax.rsqrt(jnp.mean(o * o, axis=-1, keepdims=True) + RMS_EPS)
            sg = _sigmoid(rh)
            on = o * rinv
            dr = dyg * (on * gn[:, vs]) * (sg * (1.0 + rh * (1.0 - sg)))
            don = dyg * (rh * sg)
            dgn_ref[:, vs] += jnp.sum(don * on, axis=0, keepdims=True)
            t = don * gn[:, vs]
            do = rinv * t - o * (rinv * rinv * rinv) * jnp.mean(o * t, axis=-1, keepdims=True)
            dob = do.astype(BF16)
            vb = v[:, vs].astype(BF16)
            qdb, kib, keb = qd.astype(BF16), ki.astype(BF16), ke.astype(BF16)
            a = jnp.where(causal, _dot(qdb, kib, NT), 0.0)
            st = st_ref[h]
            dst = ds_ref[h]
            dstb = dst.astype(BF16)
            da = jnp.where(causal, _dot(dob, vb, NT), 0.0)
            dab = da.astype(BF16)
            dv = _dot(a.astype(BF16), dob, TN) + _dot(keb, dstb, NT)
            dqd = _dot(dab, kib, NN) + _dot(dob, st.astype(BF16), NN)
            dki = _dot(dab, qdb, TN)
            dke = _dot(vb, dstb, NN)
            ddec = jnp.sum(st * dst, axis=0, keepdims=True)
            ds_ref[h] = dec * dst + _dot(dob, qdb, TN)
            dq = dqd * eb * (HEAD_K ** -0.5)
            dk = dki * enb + dke * eend
            db = dqd * qd - dki * ki - dke * ke
            dbl = jnp.sum(dke * ke, axis=0, keepdims=True) + dec * ddec
            dlog_parts.append(_dot(upper, db, NN, precision=lax.Precision.HIGHEST) + dbl)
            dp_ref[:, ks] = dq.astype(BF16)
            dp_ref[:, D_GLA_K + h * HEAD_K:D_GLA_K + (h + 1) * HEAD_K] = dk.astype(BF16)
            dp_ref[:, 1024 + h * HEAD_V:1024 + (h + 1) * HEAD_V] = dv.astype(BF16)
            dp_ref[:, 2048 + h * HEAD_V:2048 + (h + 1) * HEAD_V] = dr.astype(BF16)
        dlog = jnp.concatenate(dlog_parts, axis=1)
        dz = dlog * (1.0 / GATE_TAU) * (1.0 / (1.0 + jnp.exp(z)))
        dzb = dz.astype(BF16)
        dp_ref[:, 3072:3200] = _dot(dzb, wg_ref[...], NT).astype(BF16)
        dwg_ref[...] += _dot(zl.astype(BF16), dzb, TN)
        dgb_ref[...] += jnp.sum(dz, axis=0, keepdims=True)

    rev = lambda n: nch - 1 - n
    return pl.pallas_call(
        body,
        name="gla_bwd",
        grid=(nch,),
        in_specs=[pl.BlockSpec((CHUNK, HALF_P), lambda n: (rev(n), 1)), _full((LANE, D_GLA_K)), _full((1, D_GLA_K)),
                  _full((1, D_GLA_V)), pl.BlockSpec((CHUNK, D_GLA_V), lambda n: (rev(n), 0)),
                  pl.BlockSpec((None, GLA_HEADS, HEAD_V, HEAD_K), lambda n: (rev(n), 0, 0, 0)),
                  pl.BlockSpec((CHUNK, D_GLA_V), lambda n: (rev(n), 1)), pl.BlockSpec(memory_space=pl.ANY)],
        out_specs=[pl.BlockSpec((CHUNK, HALF_P), lambda n: (rev(n), 1)), _full((LANE, D_GLA_K)),
                   _full((1, D_GLA_K)), _full((1, D_GLA_V))],
        out_shape=[jax.ShapeDtypeStruct(dproj.shape, BF16), jax.ShapeDtypeStruct((LANE, D_GLA_K), F32),
                   jax.ShapeDtypeStruct((1, D_GLA_K), F32), jax.ShapeDtypeStruct((1, D_GLA_V), F32)],
        scratch_shapes=[pltpu.VMEM((GLA_HEADS, HEAD_V, HEAD_K), F32)],
        input_output_aliases={7: 0},
        compiler_params=_params(1),
    )(proj, wg128, gbias, gng, o_all, states, dyin, dproj)


def _proj_bwd_x(dproj, w_full, dh1, deps=()):
    T, P = dproj.shape
    D = w_full.shape[0]
    tm, tk = _tile(T, 512), _tile(P, 1280)

    def ep(acc_ref, ex, o, i, j):
        o[0][...] = DN_ALPHA * ex[0][...] + acc_ref[...]

    row = pl.BlockSpec((tm, D), lambda i, j, k: (i, 0))
    return _mm("proj_bwd_x", "nt", dproj, w_full, M=T, N=D, K=P, tm=tm, tn=D, tk=tk,
               outs=[(jax.ShapeDtypeStruct((T, D), F32), row)], extras=[(dh1, row)], epilogue=ep, deps=deps)[0]


def _place():
    x, y, c = lax.axis_index("x"), lax.axis_index("y"), lax.axis_index("c")
    chips = [(1 - x, y), (x, 1 - y), (1 - x, 1 - y)]
    return x, y, c, chips


def _rcopy(src, dst, ssem, rsem, dev):
    return pltpu.make_async_remote_copy(src_ref=src, dst_ref=dst, send_sem=ssem, recv_sem=rsem,
                                        device_id=dev, device_id_type=MESH)


def _all_gather(name, shards, deps=()):
    n = len(shards)

    def body(*refs):
        ins, outs = refs[:n], refs[n + len(deps):2 * n + len(deps)]
        ssem, rsem, lsem = refs[2 * n + len(deps):]
        x, y, c, chips = _place()
        me, sib = (x, y, c), (x, y, 1 - c)

        def slot(w, px, py, pc):
            return outs[w].at[4 * px + 2 * py + pc]

        started = []
        for w in range(n):
            lc = pltpu.make_async_copy(ins[w], slot(w, *me), lsem.at[w])
            lc.start()
            started.append(lc)
        sends = []
        for w in range(n):
            cp = _rcopy(ins[w], slot(w, *me), ssem.at[7 * w], rsem.at[7 * w], sib)
            cp.start()
            sends.append(cp)
            for jx, chip in enumerate(chips):
                cp = _rcopy(ins[w], slot(w, *me), ssem.at[7 * w + 1 + jx], rsem.at[7 * w + 1 + jx], (*chip, c))
                cp.start()
                sends.append(cp)
        for w in range(n):
            for jx, chip in enumerate(chips):
                blk = slot(w, *chip, c)
                _rcopy(blk, blk, ssem.at[7 * w + 1 + jx], rsem.at[7 * w + 1 + jx], me).wait_recv()
                cp = _rcopy(blk, blk, ssem.at[7 * w + 4 + jx], rsem.at[7 * w + 4 + jx], sib)
                cp.start()
                sends.append(cp)
        for w in range(n):
            blk = slot(w, x, y, 1 - c)
            _rcopy(blk, blk, ssem.at[7 * w], rsem.at[7 * w], me).wait_recv()
            for jx, chip in enumerate(chips):
                blk = slot(w, *chip, 1 - c)
                _rcopy(blk, blk, ssem.at[7 * w + 4 + jx], rsem.at[7 * w + 4 + jx], me).wait_recv()
        for cp in sends:
            cp.wait_send()
        for lc in started:
            lc.wait()

    return pl.pallas_call(
        body,
        name=name,
        in_specs=_hbm_specs(n + len(deps)),
        out_specs=_hbm_specs(n),
        out_shape=[jax.ShapeDtypeStruct((N_DEV,) + s.shape, s.dtype) for s in shards],
        scratch_shapes=[pltpu.SemaphoreType.DMA((7 * n,)), pltpu.SemaphoreType.DMA((7 * n,)),
                        pltpu.SemaphoreType.DMA((n,))],
    )(*shards, *deps)


HBM_SPEC = pl.BlockSpec(memory_space=pltpu.HBM)
SEM_SPEC = pl.BlockSpec(memory_space=pltpu.SEMAPHORE)
SIDE_EFFECT = pltpu.SideEffectType.DATAFLOW_SIDE_EFFECTING


def _cast_place(name, ids, w, dep):
    R, C = w.shape
    tr = _tile(R, 256)

    def body(ids_ref, w_ref, dep_ref, o_ref):
        o_ref[...] = w_ref[...].astype(BF16)

    return pl.pallas_call(
        body,
        name=name,
        grid_spec=pltpu.PrefetchScalarGridSpec(
            num_scalar_prefetch=1,
            grid=(R // tr,),
            in_specs=[pl.BlockSpec((tr, C), lambda r, ids: (r, 0)), pl.BlockSpec(memory_space=pl.ANY)],
            out_specs=pl.BlockSpec((None, tr, C), lambda r, ids: (ids[0], r, 0)),
        ),
        out_shape=jax.ShapeDtypeStruct((N_DEV, R, C), BF16),
        compiler_params=_params(1),
    )(ids, w, dep)


def _xfer_start(name, bufs, plan, n):
    nb = len(bufs)

    def body(*refs):
        ins = refs[:nb]
        ssem, rsem = refs[nb], refs[nb + 1]
        token = refs[2 * nb + 2]
        x, y, c, chips = _place()
        for k, (src, dst, dev, _) in enumerate(plan(ins, x, y, c, chips)):
            _rcopy(src, dst, ssem.at[k], rsem.at[k], dev).start()
        token[...] = jnp.zeros_like(token)

    res = pl.pallas_call(
        body,
        name=name,
        out_shape=(pltpu.SemaphoreType.DMA((n,)), pltpu.SemaphoreType.DMA((n,)),
                   *[pltpu.HBM(b.shape, b.dtype) for b in bufs], jax.ShapeDtypeStruct((8, LANE), F32)),
        in_specs=[HBM_SPEC] * nb,
        out_specs=(SEM_SPEC, SEM_SPEC, *[HBM_SPEC] * nb, pl.BlockSpec(memory_space=pltpu.VMEM)),
        input_output_aliases={i: 2 + i for i in range(nb)},
        compiler_params=pltpu.CompilerParams(has_side_effects=SIDE_EFFECT),
    )(*[pltpu.with_memory_space_constraint(b, pltpu.HBM) for b in bufs])
    return dict(sems=res[:2], bufs=list(res[2:2 + nb]), token=res[2 + nb], plan=plan, n=n)


def _xfer_wait(name, started, after):
    bufs, plan = started["bufs"], started["plan"]
    nb = len(bufs)

    def body(*refs):
        ins = refs[:nb]
        ssem, rsem = refs[nb], refs[nb + 1]
        x, y, c, chips = _place()
        for k, (src, _, dev, land) in enumerate(plan(ins, x, y, c, chips)):
            cp = _rcopy(src, land, ssem.at[k], rsem.at[k], dev)
            cp.wait_send()
            cp.wait_recv()

    res = pl.pallas_call(
        body,
        name=name,
        out_shape=tuple(pltpu.HBM(b.shape, b.dtype) for b in bufs),
        in_specs=[HBM_SPEC] * nb + [SEM_SPEC, SEM_SPEC, pl.BlockSpec(memory_space=pl.ANY)],
        out_specs=tuple([HBM_SPEC] * nb),
        input_output_aliases={i: i for i in range(nb)},
        compiler_params=pltpu.CompilerParams(has_side_effects=SIDE_EFFECT),
    )(*bufs, *started["sems"], after)
    return list(res)


def _plan_gather_chips(refs, x, y, c, chips):
    (land,) = refs
    mine = land.at[4 * x + 2 * y + c]
    plan = [(mine, mine, (x, y, 1 - c), land.at[4 * x + 2 * y + (1 - c)])]
    for px, py in chips:
        plan.append((mine, mine, (px, py, c), land.at[4 * px + 2 * py + c]))
    return plan


def _plan_gather_pass(refs, x, y, c, chips):
    (land,) = refs
    return [(land.at[4 * px + 2 * py + c], land.at[4 * px + 2 * py + c], (x, y, 1 - c),
             land.at[4 * px + 2 * py + (1 - c)]) for px, py in chips]


def _plan_reduce_core(refs, x, y, c, chips):
    grad, recv = refs
    return [(grad.at[2 * q + (1 - c)], recv.at[q], (x, y, 1 - c), recv.at[q]) for q in range(N_CHIP)]


def _plan_reduce_chips(refs, x, y, c, chips):
    part, land = refs
    return [(part.at[2 * px + py], land.at[2 * x + y], (px, py, c), land.at[2 * px + py]) for px, py in chips]


def _chip_sums(name, ids, grad, recv):
    _, R, C = grad.shape
    tr = _tile(R, 256)

    def body(ids_ref, g_ref, r_ref, o_ref):
        o_ref[...] = (g_ref[...] + r_ref[...]).astype(BF16)

    return pl.pallas_call(
        body,
        name=name,
        grid_spec=pltpu.PrefetchScalarGridSpec(
            num_scalar_prefetch=1,
            grid=(N_CHIP, R // tr),
            in_specs=[pl.BlockSpec((None, tr, C), lambda q, r, ids: (2 * q + ids[2], r, 0)),
                      pl.BlockSpec((None, tr, C), lambda q, r, ids: (q, r, 0))],
            out_specs=pl.BlockSpec((None, tr, C), lambda q, r, ids: (q, r, 0)),
        ),
        out_shape=jax.ShapeDtypeStruct((N_CHIP, R, C), BF16),
        compiler_params=_params(2),
    )(ids, grad, recv)


def _adamw(w, g, m, v):
    m = ADAM_B1 * m + (1.0 - ADAM_B1) * g
    v = ADAM_B2 * v + (1.0 - ADAM_B2) * (g * g)
    m_hat = m / (1.0 - ADAM_B1 ** ADAM_STEP)
    v_hat = v / (1.0 - ADAM_B2 ** ADAM_STEP)
    delta = -ADAM_LR * (m_hat / (jnp.sqrt(v_hat) + ADAM_EPS) + ADAM_WD * w)
    return delta, m, v


def _reduce_adamw(name, ids, grad, recv, landed, w, m, v):
    _, R, C = grad.shape
    tr = _tile(R, 256)

    def body(ids_ref, g_ref, r_ref, l1_ref, l2_ref, l3_ref, w_ref, m_ref, v_ref, go_ref, do_ref, mo_ref, vo_ref):
        g = g_ref[...] + r_ref[...]
        g = g + l1_ref[...].astype(F32)
        g = g + l2_ref[...].astype(F32)
        g = g + l3_ref[...].astype(F32)
        delta, mn, vn = _adamw(w_ref[...], g, m_ref[...], v_ref[...])
        go_ref[...] = g
        do_ref[...] = delta
        mo_ref[...] = mn
        vo_ref[...] = vn

    def pick(k):
        return pl.BlockSpec((None, tr, C), lambda r, ids: (ids[k], r, 0))

    flat = pl.BlockSpec((tr, C), lambda r, ids: (r, 0))
    shp = jax.ShapeDtypeStruct((R, C), F32)
    return pl.pallas_call(
        body,
        name=name,
        grid_spec=pltpu.PrefetchScalarGridSpec(
            num_scalar_prefetch=1,
            grid=(R // tr,),
            in_specs=[pick(0), pick(1), pick(3), pick(4), pick(5), flat, flat, flat],
            out_specs=[flat, flat, flat, flat],
        ),
        out_shape=[shp, shp, shp, shp],
        compiler_params=_params(1),
    )(ids, grad, recv, landed, landed, landed, w, m, v)


def _small_adamw(packs, w, m, v):
    def body(p_ref, w_ref, m_ref, v_ref, g_ref, d_ref, mo_ref, vo_ref):
        g = p_ref[0]
        for dvc in range(1, N_DEV):
            g = g + p_ref[dvc]
        delta, mn, vn = _adamw(w_ref[...], g, m_ref[...], v_ref[...])
        g_ref[...] = g
        d_ref[...] = delta
        mo_ref[...] = mn
        vo_ref[...] = vn

    shp = jax.ShapeDtypeStruct(w.shape, F32)
    return pl.pallas_call(
        body,
        name="small_adamw",
        in_specs=[_full(packs.shape), _full(w.shape), _full(w.shape), _full(w.shape)],
        out_specs=[_full(w.shape)] * 4,
        out_shape=[shp] * 4,
        grid=(1,),
        compiler_params=_params(1),
    )(packs, w, m, v)


def _w_in_full(gathered):
    nb, D, cs = gathered.shape
    w = gathered.transpose(1, 0, 2).reshape(D, nb * cs)
    return jnp.concatenate([w[:, :CONV_COLS], jnp.zeros((D, HALF_P - CONV_COLS), w.dtype),
                            w[:, CONV_COLS:], jnp.zeros((D, HALF_P - GLA_COLS), w.dtype)], axis=1)


def _w_in_blocks(dw):
    D = dw.shape[0]
    g = jnp.concatenate([dw[:, :CONV_COLS], dw[:, HALF_P:HALF_P + GLA_COLS]], axis=1)
    return g.reshape(D, N_DEV, D_IN_PROJ // N_DEV).transpose(1, 0, 2)


def _rows(vec, n_rows):
    flat = jnp.pad(vec.reshape(-1), (0, n_rows * SP_COLS - vec.size))
    return flat.reshape(n_rows, SP_COLS)


def _pad_cols(a):
    return jnp.pad(a, ((0, 0), (0, SP_COLS - a.shape[1])))


R_CONV_W, R_CONV_G, R_GATE_B, R_GLA_G, R_LN1_G, R_LN1_B, R_LN2_G, R_LN2_B, R_LOSS, R_GATE_W = 0, 3, 4, 5, 6, 8, 10, 12, 14, 16


def _pack(conv_w, conv_g, gate_b, gla_g, ln1_g, ln1_b, ln2_g, ln2_b, loss, gate_w):
    z = jnp.zeros((1, SP_COLS), F32)
    parts = [_pad_cols(conv_w), _pad_cols(conv_g), _pad_cols(gate_b), _pad_cols(gla_g),
             _rows(ln1_g, 2), _rows(ln1_b, 2), _rows(ln2_g, 2), _rows(ln2_b, 2),
             z if loss is None else _pad_cols(jnp.sum(loss, axis=1, keepdims=True)), z, _pad_cols(gate_w)]
    return jnp.concatenate(parts, axis=0)


def _unpack(p, D, conv_cols, gate_cols):
    return dict(
        conv_w=p[R_CONV_W:R_CONV_W + 3, :conv_cols], conv_norm_g=p[R_CONV_G:R_CONV_G + 1, :D_CONV],
        gate_bias=p[R_GATE_B:R_GATE_B + 1, :D_GLA_K], gla_norm_g=p[R_GLA_G:R_GLA_G + 1, :D_GLA_V],
        ln1_g=p[R_LN1_G:R_LN1_G + 2].reshape(1, -1)[:, :D], ln1_b=p[R_LN1_B:R_LN1_B + 2].reshape(1, -1)[:, :D],
        ln2_g=p[R_LN2_G:R_LN2_G + 2].reshape(1, -1)[:, :D], ln2_b=p[R_LN2_B:R_LN2_B + 2].reshape(1, -1)[:, :D],
        w_gate_up=p[R_GATE_W:R_GATE_W + GATE_RANK, :gate_cols])


BIG = ("w_in", "w_out", "w_ff_up", "w_ff_down")
ORDER = ("w_in", "conv_w", "conv_norm_g", "w_gate_up", "gate_bias", "gla_norm_g", "w_out", "ln1_g", "ln1_b",
         "w_ff_up", "w_ff_down", "ln2_g", "ln2_b")


def kernel(x, w_in, conv_w, conv_norm_g, w_gate_up, gate_bias, gla_norm_g, w_out, ln1_g, ln1_b, w_ff_up, w_ff_down, ln2_g, ln2_b, loss_target, m_w_in, m_conv_w, m_conv_norm_g, m_w_gate_up, m_gate_bias, m_gla_norm_g, m_w_out, m_ln1_g, m_ln1_b, m_w_ff_up, m_w_ff_down, m_ln2_g, m_ln2_b, v_w_in, v_conv_w, v_conv_norm_g, v_w_gate_up, v_gate_bias, v_gla_norm_g, v_w_out, v_ln1_g, v_ln1_b, v_w_ff_up, v_w_ff_down, v_ln2_g, v_ln2_b):
    T, D = x.shape[1], x.shape[2]
    xs, target = x[0], loss_target[0]
    xi, yi, ci = lax.axis_index("x"), lax.axis_index("y"), lax.axis_index("c")
    chip = 2 * xi + yi
    dev = 2 * chip + ci
    others = [jnp.where(chip <= q, q + 1, q) for q in range(N_CHIP - 1)]
    ids = jnp.stack([dev, chip, ci] + others).astype(jnp.int32)
    conv_cols, gate_cols = conv_w.shape[2], w_gate_up.shape[2]

    z1 = jnp.zeros((1, 1), F32)
    fwd_pack = _pack(conv_w[0], z1, z1, z1, z1, z1, z1, z1, None, w_gate_up[0])
    g_in, g_pack = _all_gather("gather_w_in", [w_in[0].astype(BF16), fwd_pack])
    conv_w_full = g_pack[:, R_CONV_W:R_CONV_W + 3, :conv_cols].transpose(1, 0, 2).reshape(3, -1)
    gate_w_full = g_pack[:, R_GATE_W:R_GATE_W + GATE_RANK, :gate_cols].transpose(1, 0, 2).reshape(GATE_RANK, -1)
    conv_w8 = jnp.pad(conv_w_full, ((0, 5), (0, 0)))
    wg128 = jnp.pad(gate_w_full, ((0, LANE - GATE_RANK), (0, 0))).astype(BF16)

    ga = [_xfer_start("gather_chips_" + nm, [_cast_place("cast_place_" + nm, ids, w[0], g_pack)], _plan_gather_chips, 4)
          for nm, w in zip(BIG[1:], (w_out, w_ff_up, w_ff_down))]

    def pass_on(nm, started, after):
        (land,) = _xfer_wait("gather_chips_wait_" + nm, started, after)
        return _xfer_start("gather_pass_" + nm, [land], _plan_gather_pass, 3)

    def landed(nm, started, after):
        return _xfer_wait("gather_pass_wait_" + nm, started, after)[0]

    w_full = _w_in_full(g_in)
    proj = _proj_fwd(xs, w_full, deps=[g["token"] for g in ga])
    yin = _conv_fwd(proj, conv_w8, conv_norm_g)
    gp_out = pass_on("w_out", ga[0], yin)
    o_all, states, yin = _gla_fwd(proj, wg128, gate_bias, gla_norm_g, yin, deps=[gp_out["token"]])
    w_out_full = landed("w_out", gp_out, o_all).reshape(-1, D)
    gp_up = pass_on("w_ff_up", ga[1], o_all)
    xhat1, x1, rstd1 = _mix_ln1(yin, w_out_full, xs, ln1_g, ln1_b, deps=[gp_up["token"]])
    w_up_blk = landed("w_ff_up", gp_up, x1)
    gp_down = pass_on("w_ff_down", ga[2], x1)
    ra, h2 = _ff_up(x1, w_up_blk, deps=[gp_down["token"]])
    w_down_full = landed("w_ff_down", gp_down, ra).reshape(-1, D)
    dh3, dh3b, g_ln2_g, g_ln2_b, loss = _ff_down_loss(h2, w_down_full, xhat1, target, ln1_g, ln1_b, ln2_g, ln2_b)

    def to_core(nm, grad):
        recv = lax.empty((N_CHIP,) + grad.shape[1:], F32)
        return _xfer_start("reduce_core_" + nm, [grad, recv], _plan_reduce_core, N_CHIP)

    def to_chips(nm, started, after):
        grad, recv = _xfer_wait("reduce_core_wait_" + nm, started, after)
        part = _chip_sums("chip_sums_" + nm, ids, grad, recv)
        land = lax.empty(part.shape, BF16)
        return grad, recv, _xfer_start("reduce_chips_" + nm, [part, land], _plan_reduce_chips, N_CHIP - 1)

    da = _ff_down_bwd_act(dh3b, w_down_full, ra)
    gw_down = _grad_w("grad_w_down", h2, dh3b).reshape(N_DEV, -1, D)
    rc_down = to_core("w_ff_down", gw_down)
    gw_up = _grad_w_up_blk(x1, da, N_DEV, deps=[rc_down["token"]])
    gw_down, rv_down, rs_down = to_chips("w_ff_down", rc_down, gw_up)
    rc_up = to_core("w_ff_up", gw_up)
    dh1, dh1b, g_ln1_g, g_ln1_b = _ff_up_bwd_ln1(da, w_up_blk, dh3, xhat1, rstd1, ln1_g,
                                                 deps=[rs_down["token"], rc_up["token"]])
    gw_up, rv_up, rs_up = to_chips("w_ff_up", rc_up, dh1b)
    dyin = _mix_bwd(dh1b, w_out_full, deps=[rs_up["token"]])
    gw_out = _grad_w("grad_w_out", yin, dh1b).reshape(N_DEV, -1, D)
    rc_out = to_core("w_out", gw_out)
    dproj, g_conv_w, g_conv_g = _conv_bwd(proj, dyin, conv_w8, conv_norm_g)
    dproj, g_gate_w, g_gate_b, g_gla_g = _gla_bwd(proj, wg128, gate_bias, gla_norm_g, o_all, states, dyin, dproj)
    gw_out, rv_out, rs_out = to_chips("w_out", rc_out, dproj)
    gw_in = _w_in_blocks(_grad_w("grad_w_in", xs, dproj, a_fn=_to_bf16, tn_pref=1280, deps=[rs_out["token"]]))
    rc_in = to_core("w_in", gw_in)

    big = {}

    def finish(nm, grad, recv, started, w, m, v, after):
        _, land = _xfer_wait("reduce_chips_wait_" + nm, started, after)
        res = _reduce_adamw("adamw_" + nm, ids, grad, recv, land, w[0], m[0], v[0])
        big[nm] = [a[None] for a in res]
        return res[0]

    done = finish("w_ff_down", gw_down, rv_down, rs_down, w_ff_down, m_w_ff_down, v_w_ff_down, rc_in["token"])
    done = finish("w_ff_up", gw_up, rv_up, rs_up, w_ff_up, m_w_ff_up, v_w_ff_up, done)
    gw_in, rv_in, rs_in = to_chips("w_in", rc_in, done)
    grad_x = _proj_bwd_x(dproj, w_full, dh1, deps=[rs_in["token"]])

    pack = _pack(g_conv_w[:3], g_conv_g, g_gate_b, g_gla_g, g_ln1_g, g_ln1_b, g_ln2_g, g_ln2_b, loss,
                 g_gate_w[:GATE_RANK])
    (packs,) = _all_gather("gather_small_grads", [pack], deps=[grad_x])
    done = finish("w_out", gw_out, rv_out, rs_out, w_out, m_w_out, v_w_out, packs)
    finish("w_in", gw_in, rv_in, rs_in, w_in, m_w_in, v_w_in, done)

    def own_cols(row, n_rows, width):
        cut = lax.dynamic_slice(packs, (0, row, dev * width), (N_DEV, n_rows, width))
        return jnp.pad(cut, ((0, 0), (0, 0), (0, SP_COLS - width)))

    packs_own = jnp.concatenate([own_cols(R_CONV_W, 3, conv_cols), packs[:, R_CONV_W + 3:R_GATE_W],
                                 own_cols(R_GATE_W, GATE_RANK, gate_cols)], axis=1)

    def small_pack(cw, cg, gw, gb, gg, l1g, l1b, l2g, l2b):
        return _pack(cw[0], cg, gb, gg, l1g, l1b, l2g, l2b, None, gw[0])

    w_s = small_pack(conv_w, conv_norm_g, w_gate_up, gate_bias, gla_norm_g, ln1_g, ln1_b, ln2_g, ln2_b)
    m_s = small_pack(m_conv_w, m_conv_norm_g, m_w_gate_up, m_gate_bias, m_gla_norm_g, m_ln1_g, m_ln1_b, m_ln2_g, m_ln2_b)
    v_s = small_pack(v_conv_w, v_conv_norm_g, v_w_gate_up, v_gate_bias, v_gla_norm_g, v_ln1_g, v_ln1_b, v_ln2_g, v_ln2_b)
    g_s, d_s, mn_s, vn_s = _small_adamw(packs_own, w_s, m_s, v_s)
    small = [_unpack(p, D, conv_cols, gate_cols) for p in (g_s, d_s, mn_s, vn_s)]

    def leaf(kind, name):
        if name in BIG:
            return big[name][kind]
        a = small[kind][name]
        return a[None] if name in ("conv_w", "w_gate_up") else a

    out = [g_s[R_LOSS, 0], grad_x[None]]
    for kind in range(4):
        out += [leaf(kind, nm) for nm in ORDER]
    return tuple(out)
```

```python
import jax
import jax.numpy as jnp
from jax import lax
from jax.experimental import pallas as pl
from jax.experimental.pallas import tpu as pltpu

F32 = jnp.float32
BF16 = jnp.bfloat16

D_CONV = 1024
CONV_GROUPS = 8
GLA_HEADS = 4
HEAD_K = 128
HEAD_V = 256
D_GLA_K = 512
D_GLA_V = 1024
GATE_RANK = 16
GATE_TAU = 16.0
CHUNK = 64
LN_EPS = 1e-5
RMS_EPS = 1e-6
DN_ALPHA = 2.0 ** 0.25
D_IN_PROJ = 6160
ADAM_LR = 0.001
ADAM_B1 = 0.9
ADAM_B2 = 0.999
ADAM_EPS = 1e-08
ADAM_WD = 0.01
ADAM_STEP = 10

N_DEV = 8
N_CHIP = 4
LANE = 128
HALF_P = 3200
P_INT = 2 * HALF_P
CONV_COLS = 3 * D_CONV
GLA_COLS = D_IN_PROJ - CONV_COLS
SP_ROWS = 32
SP_COLS = 1024
VMEM_LIMIT = 56 * 1024 * 1024

NN = ((1,), (0,))
NT = ((1,), (1,))
TN = ((0,), (0,))
MESH = pl.DeviceIdType.MESH


def _dot(a, b, dims, precision=None):
    return lax.dot_general(a, b, (dims, ((), ())), preferred_element_type=F32, precision=precision)


def _tile(n, pref):
    if n <= pref:
        return n
    t = (pref // LANE) * LANE
    while t > 0 and n % t:
        t -= LANE
    assert t > 0, (n, pref)
    return t


def _params(n_axes):
    return pltpu.CompilerParams(dimension_semantics=("arbitrary",) * n_axes, vmem_limit_bytes=VMEM_LIMIT)


def _full(shape):
    nd = len(shape)
    return pl.BlockSpec(shape, lambda *_: (0,) * nd)


def _hbm_specs(n):
    return [pl.BlockSpec(memory_space=pl.ANY)] * n


def _mm(name, mode, a, b, *, M, N, K, tm, tn, tk, outs, epilogue, extras=(), a_fn=None, a_spec=None, b_spec=None,
        deps=()):
    ni, nj, nk = M // tm, N // tn, K // tk
    assert ni * tm == M and nj * tn == N and nk * tk == K, (name, M, N, K, tm, tn, tk)
    if a_spec is None:
        a_spec = (pl.BlockSpec((tk, tm), lambda i, j, k: (k, i)) if mode == "tn"
                  else pl.BlockSpec((tm, tk), lambda i, j, k: (i, k)))
    if b_spec is None:
        b_spec = (pl.BlockSpec((tn, tk), lambda i, j, k: (j, k)) if mode == "nt"
                  else pl.BlockSpec((tk, tn), lambda i, j, k: (k, j)))
    dims = {"nn": NN, "nt": NT, "tn": TN}[mode]
    n_ex, n_out, n_dep = len(extras), len(outs), len(deps)

    def body(*refs):
        a_ref, b_ref = refs[0], refs[1]
        ex = refs[2:2 + n_ex]
        o = refs[2 + n_ex + n_dep:2 + n_ex + n_dep + n_out]
        acc_ref = refs[2 + n_ex + n_dep + n_out]
        i, j, k = pl.program_id(0), pl.program_id(1), pl.program_id(2)
        av = a_ref[...]
        if a_fn is not None:
            av = a_fn(av)
        part = _dot(av, b_ref[...], dims)
        if nk == 1 and epilogue is None:
            o[0][...] = part.astype(o[0].dtype)
        elif nk == 1:
            acc_ref[...] = part
            epilogue(acc_ref, ex, o, i, j)
        else:
            @pl.when(k == 0)
            def _():
                acc_ref[...] = part

            @pl.when(k > 0)
            def _():
                acc_ref[...] += part

            @pl.when(k == nk - 1)
            def _():
                if epilogue is None:
                    o[0][...] = acc_ref[...].astype(o[0].dtype)
                else:
                    epilogue(acc_ref, ex, o, i, j)

    return pl.pallas_call(
        body,
        name=name,
        grid=(ni, nj, nk),
        in_specs=[a_spec, b_spec] + [s for _, s in extras] + _hbm_specs(n_dep),
        out_specs=[s for _, s in outs],
        out_shape=[s for s, _ in outs],
        scratch_shapes=[pltpu.VMEM((8, LANE) if nk == 1 and epilogue is None else (tm, tn), F32)],
        compiler_params=_params(3),
    )(a, b, *[x for x, _ in extras], *deps)


def _to_bf16(v):
    return v.astype(BF16)


def _row_chunks(tm):
    rc = 64 if tm % 64 == 0 else tm
    return rc, tm // rc


def _ln_bwd(dy, xhat, rstd, g):
    dxh = dy * g
    m1 = jnp.mean(dxh, axis=-1, keepdims=True)
    m2 = jnp.mean(dxh * xhat, axis=-1, keepdims=True)
    return rstd * (dxh - m1 - xhat * m2)


def _ln_fwd(h):
    mu = jnp.mean(h, axis=-1, keepdims=True)
    xc = h - mu
    var = jnp.mean(xc * xc, axis=-1, keepdims=True)
    rstd = lax.rsqrt(var + LN_EPS)
    return xc * rstd, rstd


def _proj_fwd(x, w_full, deps=()):
    T, D = x.shape
    P = w_full.shape[1]
    tm, tn = _tile(T, 512), _tile(P, 1280)
    return _mm("proj_fwd", "nn", x, w_full, M=T, N=P, K=D, tm=tm, tn=tn, tk=D,
               outs=[(jax.ShapeDtypeStruct((T, P), F32), pl.BlockSpec((tm, tn), lambda i, j, k: (i, j)))],
               epilogue=None, a_fn=_to_bf16, deps=deps)[0]


def _conv_shift(h, hp):
    row = lax.broadcasted_iota(jnp.int32, h.shape, 0)
    hm1 = hp[7:8, :]
    hm2 = hp[6:7, :]
    h1 = jnp.where(row == 0, hm1, pltpu.roll(h, 1, 0))
    h2 = jnp.where(row == 0, hm2, jnp.where(row == 1, hm1, pltpu.roll(h, 2, 0)))
    return h1, h2


def _conv_fwd(proj, conv_w8, conv_g):
    T = proj.shape[0]
    tt = _tile(T, 256)
    nt = T // tt
    t8 = tt // 8

    def body(b_ref, c_ref, u_ref, cp_ref, up_ref, w_ref, g_ref, yin_ref):
        i = pl.program_id(0)
        h = c_ref[...] * u_ref[...]
        hp = jnp.where(i > 0, cp_ref[...] * up_ref[...], 0.0)
        h1, h2 = _conv_shift(h, hp)
        w = w_ref[...]
        y = w[0:1, :] * h2 + w[1:2, :] * h1 + w[2:3, :] * h
        p = b_ref[...] * y
        parts = []
        for gi in range(CONV_GROUPS):
            pg = p[:, gi * LANE:(gi + 1) * LANE]
            r = lax.rsqrt(jnp.mean(pg * pg, axis=-1, keepdims=True) + RMS_EPS)
            parts.append(pg * r)
        yn = jnp.concatenate(parts, axis=1) * g_ref[...]
        yin_ref[...] = yn.astype(BF16)

    def col(cidx):
        return pl.BlockSpec((tt, D_CONV), lambda i: (i, cidx))

    def prev(cidx):
        return pl.BlockSpec((8, D_CONV), lambda i: (jnp.maximum(i * t8 - 1, 0), cidx))

    return pl.pallas_call(
        body,
        name="conv_fwd",
        grid=(nt,),
        in_specs=[col(0), col(1), col(2), prev(1), prev(2), _full((8, D_CONV)), _full((1, D_CONV))],
        out_specs=pl.BlockSpec((tt, D_CONV), lambda i: (i, 0)),
        out_shape=jax.ShapeDtypeStruct((T, 2 * D_CONV), BF16),
        compiler_params=_params(1),
    )(proj, proj, proj, proj, proj, conv_w8, conv_g)


def _log_sigmoid(z):
    return jnp.minimum(z, 0.0) - jnp.log(1.0 + jnp.exp(-jnp.abs(z)))


def _gla_chunk_terms(blk, wg_ref, gb_ref):
    q = blk[:, 0:512]
    k = blk[:, 512:1024]
    zl = blk[:, 3072:3200]
    z = _dot(zl.astype(BF16), wg_ref[...], NN) + gb_ref[...]
    log_a = _log_sigmoid(z) * (1.0 / GATE_TAU)
    ri = lax.broadcasted_iota(jnp.int32, (CHUNK, CHUNK), 0)
    ci = lax.broadcasted_iota(jnp.int32, (CHUNK, CHUNK), 1)
    causal = ri >= ci
    lower = causal.astype(F32)
    bcum = _dot(lower, log_a, NN, precision=lax.Precision.HIGHEST)
    return q, k, zl, z, bcum, causal


def _gla_head_terms(q, k, bcum, h):
    sl = slice(h * HEAD_K, (h + 1) * HEAD_K)
    bh = bcum[:, sl]
    bl = bh[CHUNK - 1:CHUNK, :]
    eb = jnp.exp(bh)
    enb = jnp.exp(-bh)
    eend = jnp.exp(bl - bh)
    dec = jnp.exp(bl)
    qd = q[:, sl] * (HEAD_K ** -0.5) * eb
    ki = k[:, sl] * enb
    ke = k[:, sl] * eend
    return eb, enb, eend, dec, qd, ki, ke


def _sigmoid(x):
    return 1.0 / (1.0 + jnp.exp(-x))


def _gla_fwd(proj, wg128, gbias, gng, yin, deps=()):
    T = proj.shape[0]
    nch = T // CHUNK

    def body(p_ref, wg_ref, gb_ref, gn_ref, yin_in_ref, *rest):
        o_ref, st_ref, yin_ref, s_ref = rest[len(deps):]
        n = pl.program_id(0)

        @pl.when(n == 0)
        def _():
            s_ref[...] = jnp.zeros_like(s_ref)

        blk = p_ref[...]
        q, k, _, _, bcum, causal = _gla_chunk_terms(blk, wg_ref, gb_ref)
        v = blk[:, 1024:2048]
        r = blk[:, 2048:3072]
        gn = gn_ref[...]
        for h in range(GLA_HEADS):
            _, _, _, dec, qd, ki, ke = _gla_head_terms(q, k, bcum, h)
            vs = slice(h * HEAD_V, (h + 1) * HEAD_V)
            vb = v[:, vs].astype(BF16)
            qdb = qd.astype(BF16)
            a = jnp.where(causal, _dot(qdb, ki.astype(BF16), NT), 0.0)
            st = s_ref[h]
            o = _dot(a.astype(BF16), vb, NN) + _dot(qdb, st.astype(BF16), NT)
            st_ref[h] = st
            s_ref[h] = dec * st + _dot(vb, ke.astype(BF16), TN)
            o_ref[:, vs] = o
            rinv = lax.rsqrt(jnp.mean(o * o, axis=-1, keepdims=True) + RMS_EPS)
            rh = r[:, vs]
            yin_ref[:, vs] = (o * rinv * gn[:, vs] * (rh * _sigmoid(rh))).astype(BF16)

    return pl.pallas_call(
        body,
        name="gla_fwd",
        grid=(nch,),
        in_specs=[pl.BlockSpec((CHUNK, HALF_P), lambda n: (n, 1)), _full((LANE, D_GLA_K)), _full((1, D_GLA_K)),
                  _full((1, D_GLA_V)), pl.BlockSpec(memory_space=pl.ANY)] + _hbm_specs(len(deps)),
        out_specs=[pl.BlockSpec((CHUNK, D_GLA_V), lambda n: (n, 0)),
                   pl.BlockSpec((None, GLA_HEADS, HEAD_V, HEAD_K), lambda n: (n, 0, 0, 0)),
                   pl.BlockSpec((CHUNK, D_GLA_V), lambda n: (n, 1))],
        out_shape=[jax.ShapeDtypeStruct((T, D_GLA_V), F32),
                   jax.ShapeDtypeStruct((nch, GLA_HEADS, HEAD_V, HEAD_K), F32),
                   jax.ShapeDtypeStruct(yin.shape, BF16)],
        scratch_shapes=[pltpu.VMEM((GLA_HEADS, HEAD_V, HEAD_K), F32)],
        input_output_aliases={4: 2},
        compiler_params=_params(1),
    )(proj, wg128, gbias, gng, yin, *deps)


def _mix_ln1(yin, w_out, x, ln_g, ln_b, deps=()):
    T, D = x.shape
    KY = yin.shape[1]
    tm = _tile(T, 512)
    rc, nrc = _row_chunks(tm)

    def ep(acc_ref, ex, o, i, j):
        x_ref, g_ref, b_ref = ex
        xhat_ref, x1_ref, rstd_ref = o
        g, b = g_ref[...], b_ref[...]

        def chunk(r, carry):
            rows = pl.ds(pl.multiple_of(r * rc, rc), rc)
            xhat, rstd = _ln_fwd(DN_ALPHA * x_ref[rows, :] + acc_ref[rows, :])
            xhat_ref[rows, :] = xhat
            x1_ref[rows, :] = (xhat * g + b).astype(BF16)
            rstd_ref[rows, :] = rstd
            return carry

        lax.fori_loop(0, nrc, chunk, 0)

    row = lambda i, j, k: (i, 0)
    vec = _full((1, D))
    return _mm("mix_ln1", "nn", yin, w_out, M=T, N=D, K=KY, tm=tm, tn=D, tk=_tile(KY, 1024),
               outs=[(jax.ShapeDtypeStruct((T, D), F32), pl.BlockSpec((tm, D), row)),
                     (jax.ShapeDtypeStruct((T, D), BF16), pl.BlockSpec((tm, D), row)),
                     (jax.ShapeDtypeStruct((T, 1), F32), pl.BlockSpec((tm, 1), row))],
               extras=[(x, pl.BlockSpec((tm, D), row)), (ln_g, vec), (ln_b, vec)],
               epilogue=ep, deps=deps)


def _ff_up(x1, w_up_blk, deps=()):
    T, D = x1.shape
    nb, _, fb = w_up_blk.shape
    tm = _tile(T, 1024)

    def ep(acc_ref, ex, o, i, j):
        ra = jnp.maximum(acc_ref[...], 0.0)
        o[0][...] = ra.astype(BF16)
        o[1][...] = (ra * ra).astype(BF16)

    blk = pl.BlockSpec((tm, fb), lambda i, j, k: (i, j))
    shp = jax.ShapeDtypeStruct((T, nb * fb), BF16)
    return _mm("ff_up", "nn", x1, w_up_blk, M=T, N=nb * fb, K=D, tm=tm, tn=fb, tk=D,
               b_spec=pl.BlockSpec((None, D, fb), lambda i, j, k: (j, 0, 0)),
               outs=[(shp, blk), (shp, blk)], epilogue=ep, deps=deps)


def _ff_down_loss(h2, w_down, xhat1, target, g1, b1, g2, b2):
    T, F = h2.shape
    D = w_down.shape[1]
    tm = _tile(T, 512)
    rc, nrc = _row_chunks(tm)
    inv_d = 1.0 / D

    def ep(acc_ref, ex, o, i, j):
        xh1_ref, t_ref, g1_ref, b1_ref, g2_ref, b2_ref = ex
        dh_ref, dhb_ref, gg_ref, gb_ref, loss_ref = o
        g1v, b1v, g2v, b2v = g1_ref[...], b1_ref[...], g2_ref[...], b2_ref[...]

        @pl.when(i == 0)
        def _():
            gg_ref[...] = jnp.zeros_like(gg_ref)
            gb_ref[...] = jnp.zeros_like(gb_ref)
            loss_ref[...] = jnp.zeros_like(loss_ref)

        def chunk(r, carry):
            sg, sb, sl = carry
            rows = pl.ds(pl.multiple_of(r * rc, rc), rc)
            x1 = xh1_ref[rows, :] * g1v + b1v
            xhat, rstd = _ln_fwd(DN_ALPHA * x1 + acc_ref[rows, :])
            e = xhat * g2v + b2v - t_ref[rows, :]
            dy = e * inv_d
            dh = _ln_bwd(dy, xhat, rstd, g2v)
            dh_ref[rows, :] = dh
            dhb_ref[rows, :] = dh.astype(BF16)
            sg = sg + jnp.sum(dy * xhat, axis=0, keepdims=True)
            sb = sb + jnp.sum(dy, axis=0, keepdims=True)
            sl = sl + jnp.sum(e * e, axis=0, keepdims=True)
            return sg, sb, sl

        z = jnp.zeros((1, D), F32)
        sg, sb, sl = lax.fori_loop(0, nrc, chunk, (z, z, z))
        gg_ref[...] += sg
        gb_ref[...] += sb
        loss_ref[...] += sl * (0.5 * inv_d)

    row = lambda i, j, k: (i, 0)
    vec = _full((1, D))
    vshape = jax.ShapeDtypeStruct((1, D), F32)
    return _mm("ff_down_loss", "nn", h2, w_down, M=T, N=D, K=F, tm=tm, tn=D, tk=_tile(F, 1024),
               outs=[(jax.ShapeDtypeStruct((T, D), F32), pl.BlockSpec((tm, D), row)),
                     (jax.ShapeDtypeStruct((T, D), BF16), pl.BlockSpec((tm, D), row)),
                     (vshape, vec), (vshape, vec), (vshape, vec)],
               extras=[(xhat1, pl.BlockSpec((tm, D), row)), (target, pl.BlockSpec((tm, D), row)),
                       (g1, vec), (b1, vec), (g2, vec), (b2, vec)],
               epilogue=ep)


def _ff_down_bwd_act(dh3b, w_down, ra):
    T, D = dh3b.shape
    F = w_down.shape[0]
    tm, tn = _tile(T, 1024), _tile(F, 1024)

    def ep(acc_ref, ex, o, i, j):
        o[0][...] = (acc_ref[...] * (2.0 * ex[0][...].astype(F32))).astype(BF16)

    blk = pl.BlockSpec((tm, tn), lambda i, j, k: (i, j))
    return _mm("ff_down_bwd_act", "nt", dh3b, w_down, M=T, N=F, K=D, tm=tm, tn=tn, tk=D,
               outs=[(jax.ShapeDtypeStruct((T, F), BF16), blk)], extras=[(ra, blk)], epilogue=ep)[0]


def _grad_w(name, a, b, *, a_fn=None, tm_pref=1024, tn_pref=1024, tk_pref=4096, deps=()):
    T, M = a.shape
    N = b.shape[1]
    tm, tn, tk = _tile(M, tm_pref), _tile(N, tn_pref), _tile(T, tk_pref)
    return _mm(name, "tn", a, b, M=M, N=N, K=T, tm=tm, tn=tn, tk=tk, a_fn=a_fn, deps=deps,
               outs=[(jax.ShapeDtypeStruct((M, N), F32), pl.BlockSpec((tm, tn), lambda i, j, k: (i, j)))],
               epilogue=None)[0]


def _grad_w_up_blk(x1, da, nb, deps=()):
    T, D = x1.shape
    F = da.shape[1]
    fb = F // nb
    tm, tk = _tile(D, 1024), _tile(T, 4096)
    return _mm("grad_w_up", "tn", x1, da, M=D, N=F, K=T, tm=tm, tn=fb, tk=tk, deps=deps,
               outs=[(jax.ShapeDtypeStruct((nb, D, fb), F32),
                      pl.BlockSpec((None, tm, fb), lambda i, j, k: (j, i, 0)))],
               epilogue=None)[0]


def _ff_up_bwd_ln1(da, w_up_blk, dh3, xhat1, rstd1, g1, deps=()):
    T, F = da.shape
    nb, D, fb = w_up_blk.shape
    tm = _tile(T, 512)
    rc, nrc = _row_chunks(tm)

    def ep(acc_ref, ex, o, i, j):
        dh3_ref, xh_ref, rstd_ref, g_ref = ex
        dh_ref, dhb_ref, gg_ref, gb_ref = o
        g = g_ref[...]

        @pl.when(i == 0)
        def _():
            gg_ref[...] = jnp.zeros_like(gg_ref)
            gb_ref[...] = jnp.zeros_like(gb_ref)

        def chunk(r, carry):
            sg, sb = carry
            rows = pl.ds(pl.multiple_of(r * rc, rc), rc)
            dx1 = DN_ALPHA * dh3_ref[rows, :] + acc_ref[rows, :]
            xhat = xh_ref[rows, :]
            dh = _ln_bwd(dx1, xhat, rstd_ref[rows, :], g)
            dh_ref[rows, :] = dh
            dhb_ref[rows, :] = dh.astype(BF16)
            return sg + jnp.sum(dx1 * xhat, axis=0, keepdims=True), sb + jnp.sum(dx1, axis=0, keepdims=True)

        z = jnp.zeros((1, D), F32)
        sg, sb = lax.fori_loop(0, nrc, chunk, (z, z))
        gg_ref[...] += sg
        gb_ref[...] += sb

    row = lambda i, j, k: (i, 0)
    vec = _full((1, D))
    vshape = jax.ShapeDtypeStruct((1, D), F32)
    return _mm("ff_up_bwd_ln1", "nt", da, w_up_blk, M=T, N=D, K=F, tm=tm, tn=D, tk=fb,
               b_spec=pl.BlockSpec((None, D, fb), lambda i, j, k: (k, 0, 0)),
               outs=[(jax.ShapeDtypeStruct((T, D), F32), pl.BlockSpec((tm, D), row)),
                     (jax.ShapeDtypeStruct((T, D), BF16), pl.BlockSpec((tm, D), row)),
                     (vshape, vec), (vshape, vec)],
               extras=[(dh3, pl.BlockSpec((tm, D), row)), (xhat1, pl.BlockSpec((tm, D), row)),
                       (rstd1, pl.BlockSpec((tm, 1), row)), (g1, vec)],
               epilogue=ep, deps=deps)


def _mix_bwd(dh1b, w_out, deps=()):
    T, D = dh1b.shape
    KY = w_out.shape[0]
    tm, tn = _tile(T, 1024), _tile(KY, 1024)
    return _mm("mix_bwd", "nt", dh1b, w_out, M=T, N=KY, K=D, tm=tm, tn=tn, tk=D, deps=deps,
               outs=[(jax.ShapeDtypeStruct((T, KY), F32), pl.BlockSpec((tm, tn), lambda i, j, k: (i, j)))],
               epilogue=None)[0]


def _conv_bwd(proj, dyin, conv_w8, conv_g, deps=()):
    T = proj.shape[0]
    tt = _tile(T, 256)
    nt = T // tt
    t8 = tt // 8
    nx = tt + 8

    def body(b_ref, c_ref, u_ref, d_ref, bn_ref, cn_ref, un_ref, dn_ref, cp_ref, up_ref, w_ref, g_ref, *rest):
        dp_ref, dw_ref, dg_ref = rest[len(deps):]
        i = pl.program_id(0)

        @pl.when(i == 0)
        def _():
            dw_ref[...] = jnp.zeros_like(dw_ref)
            dg_ref[...] = jnp.zeros_like(dg_ref)

        more = i < nt - 1

        def ext(cur_ref, nxt_ref):
            return jnp.concatenate([cur_ref[...], jnp.where(more, nxt_ref[...], 0.0)], axis=0)

        bx, cx, ux, dx = ext(b_ref, bn_ref), ext(c_ref, cn_ref), ext(u_ref, un_ref), ext(d_ref, dn_ref)
        hx = cx * ux
        hp = jnp.where(i > 0, cp_ref[...] * up_ref[...], 0.0)
        h1, h2 = _conv_shift(hx, hp)
        w = w_ref[...]
        g = g_ref[...]
        yx = w[0:1, :] * h2 + w[1:2, :] * h1 + w[2:3, :] * hx
        px = bx * yx
        dps, dgs = [], []
        for gi in range(CONV_GROUPS):
            sl = slice(gi * LANE, (gi + 1) * LANE)
            pg, dg_ = px[:, sl], dx[:, sl]
            r = lax.rsqrt(jnp.mean(pg * pg, axis=-1, keepdims=True) + RMS_EPS)
            gd = g[:, sl] * dg_
            dps.append(r * gd - pg * (r * r * r) * jnp.mean(pg * gd, axis=-1, keepdims=True))
            dgs.append(jnp.sum((dg_ * pg * r)[:tt, :], axis=0, keepdims=True))
        dpx = jnp.concatenate(dps, axis=1)
        dg_ref[...] += jnp.concatenate(dgs, axis=1)
        dyx = dpx * bx
        dyc = dyx[:tt, :]
        dh = (w[2:3, :] * dyx + w[1:2, :] * pltpu.roll(dyx, nx - 1, 0) + w[0:1, :] * pltpu.roll(dyx, nx - 2, 0))[:tt, :]
        dw_ref[0:1, :] += jnp.sum(dyc * h2[:tt, :], axis=0, keepdims=True)
        dw_ref[1:2, :] += jnp.sum(dyc * h1[:tt, :], axis=0, keepdims=True)
        dw_ref[2:3, :] += jnp.sum(dyc * hx[:tt, :], axis=0, keepdims=True)
        dp_ref[:, 0:D_CONV] = (dpx * yx)[:tt, :].astype(BF16)
        dp_ref[:, D_CONV:2 * D_CONV] = (dh * u_ref[...]).astype(BF16)
        dp_ref[:, 2 * D_CONV:3 * D_CONV] = (dh * c_ref[...]).astype(BF16)
        dp_ref[:, 3 * D_CONV:HALF_P] = jnp.zeros((tt, HALF_P - 3 * D_CONV), BF16)

    def col(cidx):
        return pl.BlockSpec((tt, D_CONV), lambda i: (i, cidx))

    def nxt(cidx):
        return pl.BlockSpec((8, D_CONV), lambda i: (jnp.minimum((i + 1) * t8, T // 8 - 1), cidx))

    def prev(cidx):
        return pl.BlockSpec((8, D_CONV), lambda i: (jnp.maximum(i * t8 - 1, 0), cidx))

    return pl.pallas_call(
        body,
        name="conv_bwd",
        grid=(nt,),
        in_specs=[col(0), col(1), col(2), col(0), nxt(0), nxt(1), nxt(2), nxt(0), prev(1), prev(2),
                  _full((8, D_CONV)), _full((1, D_CONV))] + _hbm_specs(len(deps)),
        out_specs=[pl.BlockSpec((tt, HALF_P), lambda i: (i, 0)), _full((8, D_CONV)), _full((1, D_CONV))],
        out_shape=[jax.ShapeDtypeStruct((T, P_INT), BF16), jax.ShapeDtypeStruct((8, D_CONV), F32),
                   jax.ShapeDtypeStruct((1, D_CONV), F32)],
        compiler_params=_params(1),
    )(proj, proj, proj, dyin, proj, proj, proj, dyin, proj, proj, conv_w8, conv_g, *deps)


def _gla_bwd(proj, wg128, gbias, gng, o_all, states, dyin, dproj):
    T = proj.shape[0]
    nch = T // CHUNK

    def body(p_ref, wg_ref, gb_ref, gn_ref, o_ref, st_ref, d_ref, dp_in_ref,
             dp_ref, dwg_ref, dgb_ref, dgn_ref, ds_ref):
        n = pl.program_id(0)

        @pl.when(n == 0)
        def _():
            ds_ref[...] = jnp.zeros_like(ds_ref)
            dwg_ref[...] = jnp.zeros_like(dwg_ref)
            dgb_ref[...] = jnp.zeros_like(dgb_ref)
            dgn_ref[...] = jnp.zeros_like(dgn_ref)

        blk = p_ref[...]
        q, k, zl, z, bcum, causal = _gla_chunk_terms(blk, wg_ref, gb_ref)
        v = blk[:, 1024:2048]
        r = blk[:, 2048:3072]
        gn = gn_ref[...]
        upper = (lax.broadcasted_iota(jnp.int32, (CHUNK, CHUNK), 0)
                 <= lax.broadcasted_iota(jnp.int32, (CHUNK, CHUNK), 1)).astype(F32)
        dlog_parts = []
        for h in range(GLA_HEADS):
            eb, enb, eend, dec, qd, ki, ke = _gla_head_terms(q, k, bcum, h)
            vs = slice(h * HEAD_V, (h + 1) * HEAD_V)
            ks = slice(h * HEAD_K, (h + 1) * HEAD_K)
            o = o_ref[:, vs]
            rh = r[:, vs]
            dyg = d_ref[:, vs]
            rinv = lax.rsqrt(jnp.mean(o * o, axis=-1, keepdims=True) + RMS_EPS)
            sg = _sigmoid(rh)
            on = o * rinv
            dr = dyg * (on * gn[:, vs]) * (sg * (1.0 + rh * (1.0 - sg)))
            don = dyg * (rh * sg)
            dgn_ref[:, vs] += jnp.sum(don * on, axis=0, keepdims=True)
            t = don * gn[:, vs]
            do = rinv * t - o * (rinv * rinv * rinv) * jnp.mean(o * t, axis=-1, keepdims=True)
            dob = do.astype(BF16)
            vb = v[:, vs].astype(BF16)
            qdb, kib, keb = qd.astype(BF16), ki.astype(BF16), ke.astype(BF16)
            a = jnp.where(causal, _dot(qdb, kib, NT), 0.0)
            st = st_ref[h]
            dst = ds_ref[h]
            dstb = dst.astype(BF16)
            da = jnp.where(causal, _dot(dob, vb, NT), 0.0)
            dab = da.astype(BF16)
            dv = _dot(a.astype(BF16), dob, TN) + _dot(keb, dstb, NT)
            dqd = _dot(dab, kib, NN) + _dot(dob, st.astype(BF16), NN)
            dki = _dot(dab, qdb, TN)
            dke = _dot(vb, dstb, NN)
            ddec = jnp.sum(st * dst, axis=0, keepdims=True)
            ds_ref[h] = dec * dst + _dot(dob, qdb, TN)
            dq = dqd * eb * (HEAD_K ** -0.5)
            dk = dki * enb + dke * eend
            db = dqd * qd - dki * ki - dke * ke
            dbl = jnp.sum(dke * ke, axis=0, keepdims=True) + dec * ddec
            dlog_parts.append(_dot(upper, db, NN, precision=lax.Precision.HIGHEST) + dbl)
            dp_ref[:, ks] = dq.astype(BF16)
            dp_ref[:, D_GLA_K + h * HEAD_K:D_GLA_K + (h + 1) * HEAD_K] = dk.astype(BF16)
            dp_ref[:, 1024 + h * HEAD_V:1024 + (h + 1) * HEAD_V] = dv.astype(BF16)
            dp_ref[:, 2048 + h * HEAD_V:2048 + (h + 1) * HEAD_V] = dr.astype(BF16)
        dlog = jnp.concatenate(dlog_parts, axis=1)
        dz = dlog * (1.0 / GATE_TAU) * (1.0 / (1.0 + jnp.exp(z)))
        dzb = dz.astype(BF16)
        dp_ref[:, 3072:3200] = _dot(dzb, wg_ref[...], NT).astype(BF16)
        dwg_ref[...] += _dot(zl.astype(BF16), dzb, TN)
        dgb_ref[...] += jnp.sum(dz, axis=0, keepdims=True)

    rev = lambda n: nch - 1 - n
    return pl.pallas_call(
        body,
        name="gla_bwd",
        grid=(nch,),
        in_specs=[pl.BlockSpec((CHUNK, HALF_P), lambda n: (rev(n), 1)), _full((LANE, D_GLA_K)), _full((1, D_GLA_K)),
                  _full((1, D_GLA_V)), pl.BlockSpec((CHUNK, D_GLA_V), lambda n: (rev(n), 0)),
                  pl.BlockSpec((None, GLA_HEADS, HEAD_V, HEAD_K), lambda n: (rev(n), 0, 0, 0)),
                  pl.BlockSpec((CHUNK, D_GLA_V), lambda n: (rev(n), 1)), pl.BlockSpec(memory_space=pl.ANY)],
        out_specs=[pl.BlockSpec((CHUNK, HALF_P), lambda n: (rev(n), 1)), _full((LANE, D_GLA_K)),
                   _full((1, D_GLA_K)), _full((1, D_GLA_V))],
        out_shape=[jax.ShapeDtypeStruct(dproj.shape, BF16), jax.ShapeDtypeStruct((LANE, D_GLA_K), F32),
                   jax.ShapeDtypeStruct((1, D_GLA_K), F32), jax.ShapeDtypeStruct((1, D_GLA_V), F32)],
        scratch_shapes=[pltpu.VMEM((GLA_HEADS, HEAD_V, HEAD_K), F32)],
        input_output_aliases={7: 0},
        compiler_params=_params(1),
    )(proj, wg128, gbias, gng, o_all, states, dyin, dproj)


def _proj_bwd_x(dproj, w_full, dh1, deps=()):
    T, P = dproj.shape
    D = w_full.shape[0]
    tm, tk = _tile(T, 512), _tile(P, 1280)

    def ep(acc_ref, ex, o, i, j):
        o[0][...] = DN_ALPHA * ex[0][...] + acc_ref[...]

    row = pl.BlockSpec((tm, D), lambda i, j, k: (i, 0))
    return _mm("proj_bwd_x", "nt", dproj, w_full, M=T, N=D, K=P, tm=tm, tn=D, tk=tk,
               outs=[(jax.ShapeDtypeStruct((T, D), F32), row)], extras=[(dh1, row)], epilogue=ep, deps=deps)[0]


def _place():
    x, y, c = lax.axis_index("x"), lax.axis_index("y"), lax.axis_index("c")
    chips = [(1 - x, y), (x, 1 - y), (1 - x, 1 - y)]
    return x, y, c, chips


def _rcopy(src, dst, ssem, rsem, dev):
    return pltpu.make_async_remote_copy(src_ref=src, dst_ref=dst, send_sem=ssem, recv_sem=rsem,
                                        device_id=dev, device_id_type=MESH)


def _all_gather(name, shards, deps=()):
    n = len(shards)

    def body(*refs):
        ins, outs = refs[:n], refs[n + len(deps):2 * n + len(deps)]
        ssem, rsem, lsem = refs[2 * n + len(deps):]
        x, y, c, chips = _place()
        me, sib = (x, y, c), (x, y, 1 - c)

        def slot(w, px, py, pc):
            return outs[w].at[4 * px + 2 * py + pc]

        started = []
        for w in range(n):
            lc = pltpu.make_async_copy(ins[w], slot(w, *me), lsem.at[w])
            lc.start()
            started.append(lc)
        sends = []
        for w in range(n):
            cp = _rcopy(ins[w], slot(w, *me), ssem.at[7 * w], rsem.at[7 * w], sib)
            cp.start()
            sends.append(cp)
            for jx, chip in enumerate(chips):
                cp = _rcopy(ins[w], slot(w, *me), ssem.at[7 * w + 1 + jx], rsem.at[7 * w + 1 + jx], (*chip, c))
                cp.start()
                sends.append(cp)
        for w in range(n):
            for jx, chip in enumerate(chips):
                blk = slot(w, *chip, c)
                _rcopy(blk, blk, ssem.at[7 * w + 1 + jx], rsem.at[7 * w + 1 + jx], me).wait_recv()
                cp = _rcopy(blk, blk, ssem.at[7 * w + 4 + jx], rsem.at[7 * w + 4 + jx], sib)
                cp.start()
                sends.append(cp)
        for w in range(n):
            blk = slot(w, x, y, 1 - c)
            _rcopy(blk, blk, ssem.at[7 * w], rsem.at[7 * w], me).wait_recv()
            for jx, chip in enumerate(chips):
                blk = slot(w, *chip, 1 - c)
                _rcopy(blk, blk, ssem.at[7 * w + 4 + jx], rsem.at[7 * w + 4 + jx], me).wait_recv()
        for cp in sends:
            cp.wait_send()
        for lc in started:
            lc.wait()

    return pl.pallas_call(
        body,
        name=name,
        in_specs=_hbm_specs(n + len(deps)),
        out_specs=_hbm_specs(n),
        out_shape=[jax.ShapeDtypeStruct((N_DEV,) + s.shape, s.dtype) for s in shards],
        scratch_shapes=[pltpu.SemaphoreType.DMA((7 * n,)), pltpu.SemaphoreType.DMA((7 * n,)),
                        pltpu.SemaphoreType.DMA((n,))],
    )(*shards, *deps)


HBM_SPEC = pl.BlockSpec(memory_space=pltpu.HBM)
SEM_SPEC = pl.BlockSpec(memory_space=pltpu.SEMAPHORE)
SIDE_EFFECT = pltpu.SideEffectType.DATAFLOW_SIDE_EFFECTING


def _cast_place(name, ids, w, dep):
    R, C = w.shape
    tr = _tile(R, 256)

    def body(ids_ref, w_ref, dep_ref, o_ref):
        o_ref[...] = w_ref[...].astype(BF16)

    return pl.pallas_call(
        body,
        name=name,
        grid_spec=pltpu.PrefetchScalarGridSpec(
            num_scalar_prefetch=1,
            grid=(R // tr,),
            in_specs=[pl.BlockSpec((tr, C), lambda r, ids: (r, 0)), pl.BlockSpec(memory_space=pl.ANY)],
            out_specs=pl.BlockSpec((None, tr, C), lambda r, ids: (ids[0], r, 0)),
        ),
        out_shape=jax.ShapeDtypeStruct((N_DEV, R, C), BF16),
        compiler_params=_params(1),
    )(ids, w, dep)


def _xfer_start(name, bufs, plan, n):
    nb = len(bufs)

    def body(*refs):
        ins = refs[:nb]
        ssem, rsem = refs[nb], refs[nb + 1]
        token = refs[2 * nb + 2]
        x, y, c, chips = _place()
        for k, (src, dst, dev, _) in enumerate(plan(ins, x, y, c, chips)):
            _rcopy(src, dst, ssem.at[k], rsem.at[k], dev).start()
        token[...] = jnp.zeros_like(token)

    res = pl.pallas_call(
        body,
        name=name,
        out_shape=(pltpu.SemaphoreType.DMA((n,)), pltpu.SemaphoreType.DMA((n,)),
                   *[pltpu.HBM(b.shape, b.dtype) for b in bufs], jax.ShapeDtypeStruct((8, LANE), F32)),
        in_specs=[HBM_SPEC] * nb,
        out_specs=(SEM_SPEC, SEM_SPEC, *[HBM_SPEC] * nb, pl.BlockSpec(memory_space=pltpu.VMEM)),
        input_output_aliases={i: 2 + i for i in range(nb)},
        compiler_params=pltpu.CompilerParams(has_side_effects=SIDE_EFFECT),
    )(*[pltpu.with_memory_space_constraint(b, pltpu.HBM) for b in bufs])
    return dict(sems=res[:2], bufs=list(res[2:2 + nb]), token=res[2 + nb], plan=plan, n=n)


def _xfer_wait(name, started, after):
    bufs, plan = started["bufs"], started["plan"]
    nb = len(bufs)

    def body(*refs):
        ins = refs[:nb]
        ssem, rsem = refs[nb], refs[nb + 1]
        x, y, c, chips = _place()
        for k, (src, _, dev, land) in enumerate(plan(ins, x, y, c, chips)):
            cp = _rcopy(src, land, ssem.at[k], rsem.at[k], dev)
            cp.wait_send()
            cp.wait_recv()

    res = pl.pallas_call(
        body,
        name=name,
        out_shape=tuple(pltpu.HBM(b.shape, b.dtype) for b in bufs),
        in_specs=[HBM_SPEC] * nb + [SEM_SPEC, SEM_SPEC, pl.BlockSpec(memory_space=pl.ANY)],
        out_specs=tuple([HBM_SPEC] * nb),
        input_output_aliases={i: i for i in range(nb)},
        compiler_params=pltpu.CompilerParams(has_side_effects=SIDE_EFFECT),
    )(*bufs, *started["sems"], after)
    return list(res)


def _plan_gather_chips(refs, x, y, c, chips):
    (land,) = refs
    mine = land.at[4 * x + 2 * y + c]
    plan = [(mine, mine, (x, y, 1 - c), land.at[4 * x + 2 * y + (1 - c)])]
    for px, py in chips:
        plan.append((mine, mine, (px, py, c), land.at[4 * px + 2 * py + c]))
    return plan


def _plan_gather_pass(refs, x, y, c, chips):
    (land,) = refs
    return [(land.at[4 * px + 2 * py + c], land.at[4 * px + 2 * py + c], (x, y, 1 - c),
             land.at[4 * px + 2 * py + (1 - c)]) for px, py in chips]


def _plan_reduce_core(refs, x, y, c, chips):
    grad, recv = refs
    return [(grad.at[2 * q + (1 - c)], recv.at[q], (x, y, 1 - c), recv.at[q]) for q in range(N_CHIP)]


def _plan_reduce_chips(refs, x, y, c, chips):
    part, land = refs
    return [(part.at[2 * px + py], land.at[2 * x + y], (px, py, c), land.at[2 * px + py]) for px, py in chips]


def _chip_sums(name, ids, grad, recv):
    _, R, C = grad.shape
    tr = _tile(R, 256)

    def body(ids_ref, g_ref, r_ref, o_ref):
        o_ref[...] = (g_ref[...] + r_ref[...]).astype(BF16)

    return pl.pallas_call(
        body,
        name=name,
        grid_spec=pltpu.PrefetchScalarGridSpec(
            num_scalar_prefetch=1,
            grid=(N_CHIP - 1, R // tr),
            in_specs=[pl.BlockSpec((None, tr, C), lambda q, r, ids: (2 * ids[3 + q] + ids[2], r, 0)),
                      pl.BlockSpec((None, tr, C), lambda q, r, ids: (ids[3 + q], r, 0))],
            out_specs=pl.BlockSpec((None, tr, C), lambda q, r, ids: (ids[3 + q], r, 0)),
        ),
        out_shape=jax.ShapeDtypeStruct((N_CHIP, R, C), BF16),
        compiler_params=_params(2),
    )(ids, grad, recv)


def _adamw(w, g, m, v):
    m = ADAM_B1 * m + (1.0 - ADAM_B1) * g
    v = ADAM_B2 * v + (1.0 - ADAM_B2) * (g * g)
    m_hat = m / (1.0 - ADAM_B1 ** ADAM_STEP)
    v_hat = v / (1.0 - ADAM_B2 ** ADAM_STEP)
    delta = -ADAM_LR * (m_hat / (jnp.sqrt(v_hat) + ADAM_EPS) + ADAM_WD * w)
    return delta, m, v


def _reduce_adamw(name, ids, grad, recv, landed, w, m, v):
    _, R, C = grad.shape
    tr = _tile(R, 256)

    def body(ids_ref, g_ref, r_ref, l1_ref, l2_ref, l3_ref, w_ref, m_ref, v_ref, go_ref, do_ref, mo_ref, vo_ref):
        g = g_ref[...] + r_ref[...]
        g = g + l1_ref[...].astype(F32)
        g = g + l2_ref[...].astype(F32)
        g = g + l3_ref[...].astype(F32)
        delta, mn, vn = _adamw(w_ref[...], g, m_ref[...], v_ref[...])
        go_ref[...] = g
        do_ref[...] = delta
        mo_ref[...] = mn
        vo_ref[...] = vn

    def pick(k):
        return pl.BlockSpec((None, tr, C), lambda r, ids: (ids[k], r, 0))

    flat = pl.BlockSpec((tr, C), lambda r, ids: (r, 0))
    shp = jax.ShapeDtypeStruct((R, C), F32)
    return pl.pallas_call(
        body,
        name=name,
        grid_spec=pltpu.PrefetchScalarGridSpec(
            num_scalar_prefetch=1,
            grid=(R // tr,),
            in_specs=[pick(0), pick(1), pick(3), pick(4), pick(5), flat, flat, flat],
            out_specs=[flat, flat, flat, flat],
        ),
        out_shape=[shp, shp, shp, shp],
        compiler_params=_params(1),
    )(ids, grad, recv, landed, landed, landed, w, m, v)


def _small_adamw(packs, w, m, v):
    def body(p_ref, w_ref, m_ref, v_ref, g_ref, d_ref, mo_ref, vo_ref):
        g = p_ref[0]
        for dvc in range(1, N_DEV):
            g = g + p_ref[dvc]
        delta, mn, vn = _adamw(w_ref[...], g, m_ref[...], v_ref[...])
        g_ref[...] = g
        d_ref[...] = delta
        mo_ref[...] = mn
        vo_ref[...] = vn

    shp = jax.ShapeDtypeStruct(w.shape, F32)
    return pl.pallas_call(
        body,
        name="small_adamw",
        in_specs=[_full(packs.shape), _full(w.shape), _full(w.shape), _full(w.shape)],
        out_specs=[_full(w.shape)] * 4,
        out_shape=[shp] * 4,
        grid=(1,),
        compiler_params=_params(1),
    )(packs, w, m, v)


def _w_in_pieces():
    cs = D_IN_PROJ // N_DEV
    pieces = []
    for d in range(N_DEV):
        lo, hi = d * cs, (d + 1) * cs
        if hi <= CONV_COLS:
            pieces.append((d, 0, cs, lo))
        elif lo >= CONV_COLS:
            pieces.append((d, 0, cs, lo - CONV_COLS + HALF_P))
        else:
            pieces.append((d, 0, CONV_COLS - lo, lo))
            pieces.append((d, CONV_COLS - lo, cs, HALF_P))
    return pieces


def _w_in_full(gathered):
    nb, D, cs = gathered.shape
    tr = _tile(D, 256)

    def body(g_ref, o_ref):
        o_ref[:, CONV_COLS:HALF_P] = jnp.zeros((tr, HALF_P - CONV_COLS), o_ref.dtype)
        o_ref[:, HALF_P + GLA_COLS:P_INT] = jnp.zeros((tr, HALF_P - GLA_COLS), o_ref.dtype)
        for d, a, b, dst in _w_in_pieces():
            o_ref[:, dst:dst + (b - a)] = g_ref[d, :, a:b]

    return pl.pallas_call(
        body,
        name="w_in_full",
        grid=(D // tr,),
        in_specs=[pl.BlockSpec((nb, tr, cs), lambda r: (0, r, 0))],
        out_specs=pl.BlockSpec((tr, P_INT), lambda r: (r, 0)),
        out_shape=jax.ShapeDtypeStruct((D, P_INT), gathered.dtype),
        compiler_params=_params(1),
    )(gathered)


def _w_in_blocks(dw):
    D = dw.shape[0]
    cs = D_IN_PROJ // N_DEV
    tr = _tile(D, 256)

    def body(w_ref, o_ref):
        for d, a, b, src in _w_in_pieces():
            o_ref[d, :, a:b] = w_ref[:, src:src + (b - a)]

    return pl.pallas_call(
        body,
        name="w_in_blocks",
        grid=(D // tr,),
        in_specs=[pl.BlockSpec((tr, P_INT), lambda r: (r, 0))],
        out_specs=pl.BlockSpec((N_DEV, tr, cs), lambda r: (0, r, 0)),
        out_shape=jax.ShapeDtypeStruct((N_DEV, D, cs), dw.dtype),
        compiler_params=_params(1),
    )(dw)


def _rows(vec, n_rows):
    flat = jnp.pad(vec.reshape(-1), (0, n_rows * SP_COLS - vec.size))
    return flat.reshape(n_rows, SP_COLS)


def _pad_cols(a):
    return jnp.pad(a, ((0, 0), (0, SP_COLS - a.shape[1])))


R_CONV_W, R_CONV_G, R_GATE_B, R_GLA_G, R_LN1_G, R_LN1_B, R_LN2_G, R_LN2_B, R_LOSS, R_GATE_W = 0, 3, 4, 5, 6, 8, 10, 12, 14, 16


def _pack(conv_w, conv_g, gate_b, gla_g, ln1_g, ln1_b, ln2_g, ln2_b, loss, gate_w):
    z = jnp.zeros((1, SP_COLS), F32)
    parts = [_pad_cols(conv_w), _pad_cols(conv_g), _pad_cols(gate_b), _pad_cols(gla_g),
             _rows(ln1_g, 2), _rows(ln1_b, 2), _rows(ln2_g, 2), _rows(ln2_b, 2),
             z if loss is None else _pad_cols(jnp.sum(loss, axis=1, keepdims=True)), z, _pad_cols(gate_w)]
    return jnp.concatenate(parts, axis=0)


def _unpack(p, D, conv_cols, gate_cols):
    return dict(
        conv_w=p[R_CONV_W:R_CONV_W + 3, :conv_cols], conv_norm_g=p[R_CONV_G:R_CONV_G + 1, :D_CONV],
        gate_bias=p[R_GATE_B:R_GATE_B + 1, :D_GLA_K], gla_norm_g=p[R_GLA_G:R_GLA_G + 1, :D_GLA_V],
        ln1_g=p[R_LN1_G:R_LN1_G + 2].reshape(1, -1)[:, :D], ln1_b=p[R_LN1_B:R_LN1_B + 2].reshape(1, -1)[:, :D],
        ln2_g=p[R_LN2_G:R_LN2_G + 2].reshape(1, -1)[:, :D], ln2_b=p[R_LN2_B:R_LN2_B + 2].reshape(1, -1)[:, :D],
        w_gate_up=p[R_GATE_W:R_GATE_W + GATE_RANK, :gate_cols])


BIG = ("w_in", "w_out", "w_ff_up", "w_ff_down")
ORDER = ("w_in", "conv_w", "conv_norm_g", "w_gate_up", "gate_bias", "gla_norm_g", "w_out", "ln1_g", "ln1_b",
         "w_ff_up", "w_ff_down", "ln2_g", "ln2_b")


def kernel(x, w_in, conv_w, conv_norm_g, w_gate_up, gate_bias, gla_norm_g, w_out, ln1_g, ln1_b, w_ff_up, w_ff_down, ln2_g, ln2_b, loss_target, m_w_in, m_conv_w, m_conv_norm_g, m_w_gate_up, m_gate_bias, m_gla_norm_g, m_w_out, m_ln1_g, m_ln1_b, m_w_ff_up, m_w_ff_down, m_ln2_g, m_ln2_b, v_w_in, v_conv_w, v_conv_norm_g, v_w_gate_up, v_gate_bias, v_gla_norm_g, v_w_out, v_ln1_g, v_ln1_b, v_w_ff_up, v_w_ff_down, v_ln2_g, v_ln2_b):
    T, D = x.shape[1], x.shape[2]
    xs, target = x[0], loss_target[0]
    xi, yi, ci = lax.axis_index("x"), lax.axis_index("y"), lax.axis_index("c")
    chip = 2 * xi + yi
    dev = 2 * chip + ci
    others = [jnp.where(chip <= q, q + 1, q) for q in range(N_CHIP - 1)]
    ids = jnp.stack([dev, chip, ci] + others).astype(jnp.int32)
    conv_cols, gate_cols = conv_w.shape[2], w_gate_up.shape[2]

    z1 = jnp.zeros((1, 1), F32)
    fwd_pack = _pack(conv_w[0], z1, z1, z1, z1, z1, z1, z1, None, w_gate_up[0])
    g_in, g_pack = _all_gather("gather_w_in", [w_in[0].astype(BF16), fwd_pack])
    conv_w_full = g_pack[:, R_CONV_W:R_CONV_W + 3, :conv_cols].transpose(1, 0, 2).reshape(3, -1)
    gate_w_full = g_pack[:, R_GATE_W:R_GATE_W + GATE_RANK, :gate_cols].transpose(1, 0, 2).reshape(GATE_RANK, -1)
    conv_w8 = jnp.pad(conv_w_full, ((0, 5), (0, 0)))
    wg128 = jnp.pad(gate_w_full, ((0, LANE - GATE_RANK), (0, 0))).astype(BF16)

    ga = [_xfer_start("gather_chips_" + nm, [_cast_place("cast_place_" + nm, ids, w[0], g_pack)], _plan_gather_chips, 4)
          for nm, w in zip(BIG[1:], (w_out, w_ff_up, w_ff_down))]

    def pass_on(nm, started, after):
        (land,) = _xfer_wait("gather_chips_wait_" + nm, started, after)
        return _xfer_start("gather_pass_" + nm, [land], _plan_gather_pass, 3)

    def landed(nm, started, after):
        return _xfer_wait("gather_pass_wait_" + nm, started, after)[0]

    w_full = _w_in_full(g_in)
    proj = _proj_fwd(xs, w_full, deps=[g["token"] for g in ga])
    yin = _conv_fwd(proj, conv_w8, conv_norm_g)
    gp_out = pass_on("w_out", ga[0], yin)
    o_all, states, yin = _gla_fwd(proj, wg128, gate_bias, gla_norm_g, yin, deps=[gp_out["token"]])
    w_out_full = landed("w_out", gp_out, o_all).reshape(-1, D)
    gp_up = pass_on("w_ff_up", ga[1], o_all)
    xhat1, x1, rstd1 = _mix_ln1(yin, w_out_full, xs, ln1_g, ln1_b, deps=[gp_up["token"]])
    w_up_blk = landed("w_ff_up", gp_up, x1)
    gp_down = pass_on("w_ff_down", ga[2], x1)
    ra, h2 = _ff_up(x1, w_up_blk, deps=[gp_down["token"]])
    w_down_full = landed("w_ff_down", gp_down, ra).reshape(-1, D)
    dh3, dh3b, g_ln2_g, g_ln2_b, loss = _ff_down_loss(h2, w_down_full, xhat1, target, ln1_g, ln1_b, ln2_g, ln2_b)

    def to_core(nm, grad):
        recv = lax.empty((N_CHIP,) + grad.shape[1:], F32)
        return _xfer_start("reduce_core_" + nm, [grad, recv], _plan_reduce_core, N_CHIP)

    def to_chips(nm, started, after):
        grad, recv = _xfer_wait("reduce_core_wait_" + nm, started, after)
        part = _chip_sums("chip_sums_" + nm, ids, grad, recv)
        land = lax.empty(part.shape, BF16)
        return grad, recv, _xfer_start("reduce_chips_" + nm, [part, land], _plan_reduce_chips, N_CHIP - 1)

    da = _ff_down_bwd_act(dh3b, w_down_full, ra)
    gw_down = _grad_w("grad_w_down", h2, dh3b).reshape(N_DEV, -1, D)
    rc_down = to_core("w_ff_down", gw_down)
    gw_up = _grad_w_up_blk(x1, da, N_DEV, deps=[rc_down["token"]])
    gw_down, rv_down, rs_down = to_chips("w_ff_down", rc_down, gw_up)
    rc_up = to_core("w_ff_up", gw_up)
    dh1, dh1b, g_ln1_g, g_ln1_b = _ff_up_bwd_ln1(da, w_up_blk, dh3, xhat1, rstd1, ln1_g,
                                                 deps=[rs_down["token"], rc_up["token"]])
    gw_up, rv_up, rs_up = to_chips("w_ff_up", rc_up, dh1b)
    dyin = _mix_bwd(dh1b, w_out_full, deps=[rs_up["token"]])
    gw_out = _grad_w("grad_w_out", yin, dh1b).reshape(N_DEV, -1, D)
    rc_out = to_core("w_out", gw_out)
    dproj, g_conv_w, g_conv_g = _conv_bwd(proj, dyin, conv_w8, conv_norm_g, deps=[rc_out["token"]])
    dproj, g_gate_w, g_gate_b, g_gla_g = _gla_bwd(proj, wg128, gate_bias, gla_norm_g, o_all, states, dyin, dproj)
    gw_out, rv_out, rs_out = to_chips("w_out", rc_out, dproj)
    gw_in = _w_in_blocks(_grad_w("grad_w_in", xs, dproj, a_fn=_to_bf16, tn_pref=1280, tk_pref=2048, deps=[rs_out["token"]]))
    rc_in = to_core("w_in", gw_in)

    big = {}

    def finish(nm, grad, recv, started, w, m, v, after):
        _, land = _xfer_wait("reduce_chips_wait_" + nm, started, after)
        res = _reduce_adamw("adamw_" + nm, ids, grad, recv, land, w[0], m[0], v[0])
        big[nm] = [a[None] for a in res]
        return res[0]

    done = finish("w_ff_down", gw_down, rv_down, rs_down, w_ff_down, m_w_ff_down, v_w_ff_down, rc_in["token"])
    done = finish("w_ff_up", gw_up, rv_up, rs_up, w_ff_up, m_w_ff_up, v_w_ff_up, done)
    gw_in, rv_in, rs_in = to_chips("w_in", rc_in, done)
    grad_x = _proj_bwd_x(dproj, w_full, dh1, deps=[rs_in["token"]])

    pack = _pack(g_conv_w[:3], g_conv_g, g_gate_b, g_gla_g, g_ln1_g, g_ln1_b, g_ln2_g, g_ln2_b, loss,
                 g_gate_w[:GATE_RANK])
    (packs,) = _all_gather("gather_small_grads", [pack], deps=[grad_x])
    done = finish("w_out", gw_out, rv_out, rs_out, w_out, m_w_out, v_w_out, packs)
    finish("w_in", gw_in, rv_in, rs_in, w_in, m_w_in, v_w_in, done)

    def own_cols(row, n_rows, width):
        cut = lax.dynamic_slice(packs, (0, row, dev * width), (N_DEV, n_rows, width))
        return jnp.pad(cut, ((0, 0), (0, 0), (0, SP_COLS - width)))

    packs_own = jnp.concatenate([own_cols(R_CONV_W, 3, conv_cols), packs[:, R_CONV_W + 3:R_GATE_W],
                                 own_cols(R_GATE_W, GATE_RANK, gate_cols)], axis=1)

    def small_pack(cw, cg, gw, gb, gg, l1g, l1b, l2g, l2b):
        return _pack(cw[0], cg, gb, gg, l1g, l1b, l2g, l2b, None, gw[0])

    w_s = small_pack(conv_w, conv_norm_g, w_gate_up, gate_bias, gla_norm_g, ln1_g, ln1_b, ln2_g, ln2_b)
    m_s = small_pack(m_conv_w, m_conv_norm_g, m_w_gate_up, m_gate_bias, m_gla_norm_g, m_ln1_g, m_ln1_b, m_ln2_g, m_ln2_b)
    v_s = small_pack(v_conv_w, v_conv_norm_g, v_w_gate_up, v_gate_bias, v_gla_norm_g, v_ln1_g, v_ln1_b, v_ln2_g, v_ln2_b)
    g_s, d_s, mn_s, vn_s = _small_adamw(packs_own, w_s, m_s, v_s)
    small = [_unpack(p, D, conv_cols, gate_cols) for p in (g_s, d_s, mn_s, vn_s)]

    def leaf(kind, name):
        if name in BIG:
            return big[name][kind]
        a = small[kind][name]
        return a[None] if name in ("conv_w", "w_gate_up") else a

    out = [g_s[R_LOSS, 0], grad_x[None]]
    for kind in range(4):
        out += [leaf(kind, nm) for nm in ORDER]
    return tuple(out)
```

```python
import jax
import jax.numpy as jnp
from jax import lax
from jax.experimental import pallas as pl
from jax.experimental.pallas import tpu as pltpu

F32 = jnp.float32
BF16 = jnp.bfloat16

D_CONV = 1024
CONV_GROUPS = 8
GLA_HEADS = 4
HEAD_K = 128
HEAD_V = 256
D_GLA_K = 512
D_GLA_V = 1024
GATE_RANK = 16
GATE_TAU = 16.0
CHUNK = 64
LN_EPS = 1e-5
RMS_EPS = 1e-6
DN_ALPHA = 2.0 ** 0.25
D_IN_PROJ = 6160
ADAM_LR = 0.001
ADAM_B1 = 0.9
ADAM_B2 = 0.999
ADAM_EPS = 1e-08
ADAM_WD = 0.01
ADAM_STEP = 10

N_DEV = 8
N_CHIP = 4
LANE = 128
HALF_P = 3200
P_INT = 2 * HALF_P
CONV_COLS = 3 * D_CONV
GLA_COLS = D_IN_PROJ - CONV_COLS
SP_ROWS = 32
SP_COLS = 1024
VMEM_LIMIT = 56 * 1024 * 1024

NN = ((1,), (0,))
NT = ((1,), (1,))
TN = ((0,), (0,))
MESH = pl.DeviceIdType.MESH


def _dot(a, b, dims, precision=None):
    return lax.dot_general(a, b, (dims, ((), ())), preferred_element_type=F32, precision=precision)


def _tile(n, pref):
    if n <= pref:
        return n
    t = (pref // LANE) * LANE
    while t > 0 and n % t:
        t -= LANE
    assert t > 0, (n, pref)
    return t


def _params(n_axes):
    return pltpu.CompilerParams(dimension_semantics=("arbitrary",) * n_axes, vmem_limit_bytes=VMEM_LIMIT)


def _full(shape):
    nd = len(shape)
    return pl.BlockSpec(shape, lambda *_: (0,) * nd)


def _hbm_specs(n):
    return [pl.BlockSpec(memory_space=pl.ANY)] * n


def _mm(name, mode, a, b, *, M, N, K, tm, tn, tk, outs, epilogue, extras=(), a_fn=None, a_spec=None, b_spec=None,
        deps=()):
    ni, nj, nk = M // tm, N // tn, K // tk
    assert ni * tm == M and nj * tn == N and nk * tk == K, (name, M, N, K, tm, tn, tk)
    if a_spec is None:
        a_spec = (pl.BlockSpec((tk, tm), lambda i, j, k: (k, i)) if mode == "tn"
                  else pl.BlockSpec((tm, tk), lambda i, j, k: (i, k)))
    if b_spec is None:
        b_spec = (pl.BlockSpec((tn, tk), lambda i, j, k: (j, k)) if mode == "nt"
                  else pl.BlockSpec((tk, tn), lambda i, j, k: (k, j)))
    dims = {"nn": NN, "nt": NT, "tn": TN}[mode]
    n_ex, n_out, n_dep = len(extras), len(outs), len(deps)

    def body(*refs):
        a_ref, b_ref = refs[0], refs[1]
        ex = refs[2:2 + n_ex]
        o = refs[2 + n_ex + n_dep:2 + n_ex + n_dep + n_out]
        acc_ref = refs[2 + n_ex + n_dep + n_out]
        i, j, k = pl.program_id(0), pl.program_id(1), pl.program_id(2)
        av = a_ref[...]
        if a_fn is not None:
            av = a_fn(av)
        part = _dot(av, b_ref[...], dims)
        if nk == 1 and epilogue is None:
            o[0][...] = part.astype(o[0].dtype)
        elif nk == 1:
            acc_ref[...] = part
            epilogue(acc_ref, ex, o, i, j)
        else:
            @pl.when(k == 0)
            def _():
                acc_ref[...] = part

            @pl.when(k > 0)
            def _():
                acc_ref[...] += part

            @pl.when(k == nk - 1)
            def _():
                if epilogue is None:
                    o[0][...] = acc_ref[...].astype(o[0].dtype)
                else:
                    epilogue(acc_ref, ex, o, i, j)

    return pl.pallas_call(
        body,
        name=name,
        grid=(ni, nj, nk),
        in_specs=[a_spec, b_spec] + [s for _, s in extras] + _hbm_specs(n_dep),
        out_specs=[s for _, s in outs],
        out_shape=[s for s, _ in outs],
        scratch_shapes=[pltpu.VMEM((8, LANE) if nk == 1 and epilogue is None else (tm, tn), F32)],
        compiler_params=_params(3),
    )(a, b, *[x for x, _ in extras], *deps)


def _mm_rows(name, mode, a, b, *, M, N, K, tm, tk, row_ins, vec_ins, row_outs, stat_outs, chunk_fn,
             b_spec=None, deps=()):
    ni, nk = M // tm, K // tk
    rc = tm // nk
    assert ni * tm == M and nk * tk == K and rc * nk == tm and rc % 8 == 0, (name, M, K, tm, tk)
    dims = {"nn": NN, "nt": NT}[mode]
    last = ni - 1

    def kk(i, k):
        return jnp.where(i < ni, k, nk - 1)

    a_spec = pl.BlockSpec((tm, tk), lambda i, k: (jnp.minimum(i, last), kk(i, k)))
    if b_spec is None:
        b_spec = (pl.BlockSpec((N, tk), lambda i, k: (0, kk(i, k))) if mode == "nt"
                  else pl.BlockSpec((tk, N), lambda i, k: (kk(i, k), 0)))
    prev_row = lambda i, k: (jnp.maximum(i - 1, 0), 0)
    n_ri, n_vi, n_ro, n_so, n_dep = len(row_ins), len(vec_ins), len(row_outs), len(stat_outs), len(deps)

    def body(*refs):
        a_ref, b_ref = refs[0], refs[1]
        pos = 2
        ri = refs[pos:pos + n_ri]; pos += n_ri
        vi = refs[pos:pos + n_vi]; pos += n_vi + n_dep
        ro = refs[pos:pos + n_ro]; pos += n_ro
        so = refs[pos:pos + n_so]; pos += n_so
        accs = refs[pos:pos + 2]
        i, k = pl.program_id(0), pl.program_id(1)

        @pl.when((i == 0) & (k == 0))
        def _():
            accs[0][...] = jnp.zeros_like(accs[0])
            accs[1][...] = jnp.zeros_like(accs[1])
            for st in so:
                st[...] = jnp.zeros_like(st)

        def finish_rows(prev_ref):
            rows = pl.ds(pl.multiple_of(k * rc, rc), rc)
            done = prev_ref[rows, :]
            prev_ref[rows, :] = jnp.zeros((rc, N), F32)
            chunk_fn(done, rows, i > 0, ri, vi, ro, so)

        for parity in (0, 1):
            @pl.when((i < ni) & (lax.rem(i, 2) == parity))
            def _(parity=parity):
                part = _dot(a_ref[...], b_ref[...], dims)
                finish_rows(accs[1 - parity])
                accs[parity][...] += part

        @pl.when(i == ni)
        def _():
            finish_rows(accs[last % 2])

    row_spec = lambda arr: pl.BlockSpec((tm, arr.shape[1]), prev_row)
    vec_spec = lambda arr: _full(arr.shape)
    return pl.pallas_call(
        body,
        name=name,
        grid=(ni + 1, nk),
        in_specs=[a_spec, b_spec] + [row_spec(x) for x in row_ins] + [vec_spec(x) for x in vec_ins]
        + _hbm_specs(n_dep),
        out_specs=[pl.BlockSpec((tm, s.shape[1]), prev_row) for s in row_outs] + [_full(s.shape) for s in stat_outs],
        out_shape=list(row_outs) + list(stat_outs),
        scratch_shapes=[pltpu.VMEM((tm, N), F32), pltpu.VMEM((tm, N), F32)],
        compiler_params=_params(2),
    )(a, b, *row_ins, *vec_ins, *deps)


def _to_bf16(v):
    return v.astype(BF16)


def _ln_bwd(dy, xhat, rstd, g):
    dxh = dy * g
    m1 = jnp.mean(dxh, axis=-1, keepdims=True)
    m2 = jnp.mean(dxh * xhat, axis=-1, keepdims=True)
    return rstd * (dxh - m1 - xhat * m2)


def _ln_fwd(h):
    mu = jnp.mean(h, axis=-1, keepdims=True)
    xc = h - mu
    var = jnp.mean(xc * xc, axis=-1, keepdims=True)
    rstd = lax.rsqrt(var + LN_EPS)
    return xc * rstd, rstd


def _proj_fwd(x, w_full, deps=()):
    T, D = x.shape
    P = w_full.shape[1]
    tm, tn = _tile(T, 512), _tile(P, 1280)
    return _mm("proj_fwd", "nn", x, w_full, M=T, N=P, K=D, tm=tm, tn=tn, tk=D,
               outs=[(jax.ShapeDtypeStruct((T, P), F32), pl.BlockSpec((tm, tn), lambda i, j, k: (i, j)))],
               epilogue=None, a_fn=_to_bf16, deps=deps)[0]


def _conv_shift(h, hp):
    row = lax.broadcasted_iota(jnp.int32, h.shape, 0)
    hm1 = hp[7:8, :]
    hm2 = hp[6:7, :]
    h1 = jnp.where(row == 0, hm1, pltpu.roll(h, 1, 0))
    h2 = jnp.where(row == 0, hm2, jnp.where(row == 1, hm1, pltpu.roll(h, 2, 0)))
    return h1, h2


def _conv_fwd(proj, conv_w8, conv_g):
    T = proj.shape[0]
    tt = _tile(T, 256)
    nt = T // tt
    t8 = tt // 8

    def body(b_ref, c_ref, u_ref, cp_ref, up_ref, w_ref, g_ref, yin_ref):
        i = pl.program_id(0)
        h = c_ref[...] * u_ref[...]
        hp = jnp.where(i > 0, cp_ref[...] * up_ref[...], 0.0)
        h1, h2 = _conv_shift(h, hp)
        w = w_ref[...]
        y = w[0:1, :] * h2 + w[1:2, :] * h1 + w[2:3, :] * h
        p = b_ref[...] * y
        parts = []
        for gi in range(CONV_GROUPS):
            pg = p[:, gi * LANE:(gi + 1) * LANE]
            r = lax.rsqrt(jnp.mean(pg * pg, axis=-1, keepdims=True) + RMS_EPS)
            parts.append(pg * r)
        yn = jnp.concatenate(parts, axis=1) * g_ref[...]
        yin_ref[...] = yn.astype(BF16)

    def col(cidx):
        return pl.BlockSpec((tt, D_CONV), lambda i: (i, cidx))

    def prev(cidx):
        return pl.BlockSpec((8, D_CONV), lambda i: (jnp.maximum(i * t8 - 1, 0), cidx))

    return pl.pallas_call(
        body,
        name="conv_fwd",
        grid=(nt,),
        in_specs=[col(0), col(1), col(2), prev(1), prev(2), _full((8, D_CONV)), _full((1, D_CONV))],
        out_specs=pl.BlockSpec((tt, D_CONV), lambda i: (i, 0)),
        out_shape=jax.ShapeDtypeStruct((T, 2 * D_CONV), BF16),
        compiler_params=_params(1),
    )(proj, proj, proj, proj, proj, conv_w8, conv_g)


def _log_sigmoid(z):
    return jnp.minimum(z, 0.0) - jnp.log(1.0 + jnp.exp(-jnp.abs(z)))


def _gla_chunk_terms(blk, wg_ref, gb_ref):
    q = blk[:, 0:512]
    k = blk[:, 512:1024]
    zl = blk[:, 3072:3200]
    z = _dot(zl.astype(BF16), wg_ref[...], NN) + gb_ref[...]
    log_a = _log_sigmoid(z) * (1.0 / GATE_TAU)
    ri = lax.broadcasted_iota(jnp.int32, (CHUNK, CHUNK), 0)
    ci = lax.broadcasted_iota(jnp.int32, (CHUNK, CHUNK), 1)
    causal = ri >= ci
    lower = causal.astype(F32)
    bcum = _dot(lower, log_a, NN, precision=lax.Precision.HIGHEST)
    return q, k, zl, z, bcum, causal


def _gla_head_terms(q, k, bcum, h):
    sl = slice(h * HEAD_K, (h + 1) * HEAD_K)
    bh = bcum[:, sl]
    bl = bh[CHUNK - 1:CHUNK, :]
    eb = jnp.exp(bh)
    enb = jnp.exp(-bh)
    eend = jnp.exp(bl - bh)
    dec = jnp.exp(bl)
    qd = q[:, sl] * (HEAD_K ** -0.5) * eb
    ki = k[:, sl] * enb
    ke = k[:, sl] * eend
    return eb, enb, eend, dec, qd, ki, ke


def _sigmoid(x):
    return 1.0 / (1.0 + jnp.exp(-x))


def _gla_fwd(proj, wg128, gbias, gng, yin, deps=()):
    T = proj.shape[0]
    nch = T // CHUNK

    def body(p_ref, wg_ref, gb_ref, gn_ref, yin_in_ref, *rest):
        o_ref, st_ref, yin_ref, s_ref = rest[len(deps):]
        n = pl.program_id(0)

        @pl.when(n == 0)
        def _():
            s_ref[...] = jnp.zeros_like(s_ref)

        blk = p_ref[...]
        q, k, _, _, bcum, causal = _gla_chunk_terms(blk, wg_ref, gb_ref)
        v = blk[:, 1024:2048]
        r = blk[:, 2048:3072]
        gn = gn_ref[...]
        for h in range(GLA_HEADS):
            _, _, _, dec, qd, ki, ke = _gla_head_terms(q, k, bcum, h)
            vs = slice(h * HEAD_V, (h + 1) * HEAD_V)
            vb = v[:, vs].astype(BF16)
            qdb = qd.astype(BF16)
            a = jnp.where(causal, _dot(qdb, ki.astype(BF16), NT), 0.0)
            st = s_ref[h]
            o = _dot(a.astype(BF16), vb, NN) + _dot(qdb, st.astype(BF16), NT)
            st_ref[h] = st
            s_ref[h] = dec * st + _dot(vb, ke.astype(BF16), TN)
            o_ref[:, vs] = o
            rinv = lax.rsqrt(jnp.mean(o * o, axis=-1, keepdims=True) + RMS_EPS)
            rh = r[:, vs]
            yin_ref[:, vs] = (o * rinv * gn[:, vs] * (rh * _sigmoid(rh))).astype(BF16)

    return pl.pallas_call(
        body,
        name="gla_fwd",
        grid=(nch,),
        in_specs=[pl.BlockSpec((CHUNK, HALF_P), lambda n: (n, 1)), _full((LANE, D_GLA_K)), _full((1, D_GLA_K)),
                  _full((1, D_GLA_V)), pl.BlockSpec(memory_space=pl.ANY)] + _hbm_specs(len(deps)),
        out_specs=[pl.BlockSpec((CHUNK, D_GLA_V), lambda n: (n, 0)),
                   pl.BlockSpec((None, GLA_HEADS, HEAD_V, HEAD_K), lambda n: (n, 0, 0, 0)),
                   pl.BlockSpec((CHUNK, D_GLA_V), lambda n: (n, 1))],
        out_shape=[jax.ShapeDtypeStruct((T, D_GLA_V), F32),
                   jax.ShapeDtypeStruct((nch, GLA_HEADS, HEAD_V, HEAD_K), F32),
                   jax.ShapeDtypeStruct(yin.shape, BF16)],
        scratch_shapes=[pltpu.VMEM((GLA_HEADS, HEAD_V, HEAD_K), F32)],
        input_output_aliases={4: 2},
        compiler_params=_params(1),
    )(proj, wg128, gbias, gng, yin, *deps)


def _mix_ln1(yin, w_out, x, ln_g, ln_b, deps=()):
    T, D = x.shape
    KY = yin.shape[1]
    tm = _tile(T, 512)

    def chunk(acc, rows, valid, ri, vi, ro, so):
        xhat, rstd = _ln_fwd(DN_ALPHA * ri[0][rows, :] + acc)
        ro[0][rows, :] = xhat
        ro[1][rows, :] = (xhat * vi[0][...] + vi[1][...]).astype(BF16)
        ro[2][rows, :] = rstd

    return _mm_rows("mix_ln1", "nn", yin, w_out, M=T, N=D, K=KY, tm=tm, tk=_tile(KY, 256),
                    row_ins=[x], vec_ins=[ln_g, ln_b],
                    row_outs=[jax.ShapeDtypeStruct((T, D), F32), jax.ShapeDtypeStruct((T, D), BF16),
                              jax.ShapeDtypeStruct((T, 1), F32)],
                    stat_outs=[], chunk_fn=chunk, deps=deps)


def _ff_up(x1, w_up_blk, deps=()):
    T, D = x1.shape
    nb, _, fb = w_up_blk.shape
    tm = _tile(T, 1024)

    def ep(acc_ref, ex, o, i, j):
        ra = jnp.maximum(acc_ref[...], 0.0)
        o[0][...] = ra.astype(BF16)
        o[1][...] = (ra * ra).astype(BF16)

    blk = pl.BlockSpec((tm, fb), lambda i, j, k: (i, j))
    shp = jax.ShapeDtypeStruct((T, nb * fb), BF16)
    return _mm("ff_up", "nn", x1, w_up_blk, M=T, N=nb * fb, K=D, tm=tm, tn=fb, tk=D,
               b_spec=pl.BlockSpec((None, D, fb), lambda i, j, k: (j, 0, 0)),
               outs=[(shp, blk), (shp, blk)], epilogue=ep, deps=deps)


def _ff_down_loss(h2, w_down, xhat1, target, g1, b1, g2, b2):
    T, F = h2.shape
    D = w_down.shape[1]
    tm = _tile(T, 512)
    inv_d = 1.0 / D

    def chunk(acc, rows, valid, ri, vi, ro, so):
        g1v, b1v, g2v, b2v = (v[...] for v in vi)
        x1 = ri[0][rows, :] * g1v + b1v
        xhat, rstd = _ln_fwd(DN_ALPHA * x1 + acc)
        e = xhat * g2v + b2v - ri[1][rows, :]
        dy = e * inv_d
        dh = _ln_bwd(dy, xhat, rstd, g2v)
        ro[0][rows, :] = dh
        ro[1][rows, :] = dh.astype(BF16)
        so[0][...] += jnp.where(valid, jnp.sum(dy * xhat, axis=0, keepdims=True), 0.0)
        so[1][...] += jnp.where(valid, jnp.sum(dy, axis=0, keepdims=True), 0.0)
        so[2][...] += jnp.where(valid, jnp.sum(e * e, axis=0, keepdims=True) * (0.5 * inv_d), 0.0)

    vshape = jax.ShapeDtypeStruct((1, D), F32)
    return _mm_rows("ff_down_loss", "nn", h2, w_down, M=T, N=D, K=F, tm=tm, tk=_tile(F, 1024),
                    row_ins=[xhat1, target], vec_ins=[g1, b1, g2, b2],
                    row_outs=[jax.ShapeDtypeStruct((T, D), F32), jax.ShapeDtypeStruct((T, D), BF16)],
                    stat_outs=[vshape, vshape, vshape], chunk_fn=chunk)


def _ff_down_bwd_act(dh3b, w_down, ra):
    T, D = dh3b.shape
    F = w_down.shape[0]
    tm, tn = _tile(T, 1024), _tile(F, 1024)

    def ep(acc_ref, ex, o, i, j):
        o[0][...] = (acc_ref[...] * (2.0 * ex[0][...].astype(F32))).astype(BF16)

    blk = pl.BlockSpec((tm, tn), lambda i, j, k: (i, j))
    return _mm("ff_down_bwd_act", "nt", dh3b, w_down, M=T, N=F, K=D, tm=tm, tn=tn, tk=D,
               outs=[(jax.ShapeDtypeStruct((T, F), BF16), blk)], extras=[(ra, blk)], epilogue=ep)[0]


def _grad_w(name, a, b, *, a_fn=None, tm_pref=1024, tn_pref=1024, tk_pref=4096, deps=()):
    T, M = a.shape
    N = b.shape[1]
    tm, tn, tk = _tile(M, tm_pref), _tile(N, tn_pref), _tile(T, tk_pref)
    return _mm(name, "tn", a, b, M=M, N=N, K=T, tm=tm, tn=tn, tk=tk, a_fn=a_fn, deps=deps,
               outs=[(jax.ShapeDtypeStruct((M, N), F32), pl.BlockSpec((tm, tn), lambda i, j, k: (i, j)))],
               epilogue=None)[0]


def _grad_w_up_blk(x1, da, nb, deps=()):
    T, D = x1.shape
    F = da.shape[1]
    fb = F // nb
    tm, tk = _tile(D, 1024), _tile(T, 4096)
    return _mm("grad_w_up", "tn", x1, da, M=D, N=F, K=T, tm=tm, tn=fb, tk=tk, deps=deps,
               outs=[(jax.ShapeDtypeStruct((nb, D, fb), F32),
                      pl.BlockSpec((None, tm, fb), lambda i, j, k: (j, i, 0)))],
               epilogue=None)[0]


def _ff_up_bwd_ln1(da, w_up_blk, dh3, xhat1, rstd1, g1, deps=()):
    T, F = da.shape
    nb, D, fb = w_up_blk.shape
    tm = _tile(T, 512)

    def chunk(acc, rows, valid, ri, vi, ro, so):
        dx1 = DN_ALPHA * ri[0][rows, :] + acc
        xhat = ri[1][rows, :]
        dh = _ln_bwd(dx1, xhat, ri[2][rows, :], vi[0][...])
        ro[0][rows, :] = dh
        ro[1][rows, :] = dh.astype(BF16)
        so[0][...] += jnp.where(valid, jnp.sum(dx1 * xhat, axis=0, keepdims=True), 0.0)
        so[1][...] += jnp.where(valid, jnp.sum(dx1, axis=0, keepdims=True), 0.0)

    nk = F // fb
    vshape = jax.ShapeDtypeStruct((1, D), F32)
    return _mm_rows("ff_up_bwd_ln1", "nt", da, w_up_blk, M=T, N=D, K=F, tm=tm, tk=fb,
                    b_spec=pl.BlockSpec((None, D, fb), lambda i, k: (jnp.where(i < T // tm, k, nk - 1), 0, 0)),
                    row_ins=[dh3, xhat1, rstd1], vec_ins=[g1],
                    row_outs=[jax.ShapeDtypeStruct((T, D), F32), jax.ShapeDtypeStruct((T, D), BF16)],
                    stat_outs=[vshape, vshape], chunk_fn=chunk, deps=deps)


def _mix_bwd(dh1b, w_out, deps=()):
    T, D = dh1b.shape
    KY = w_out.shape[0]
    tm, tn = _tile(T, 1024), _tile(KY, 1024)
    return _mm("mix_bwd", "nt", dh1b, w_out, M=T, N=KY, K=D, tm=tm, tn=tn, tk=D, deps=deps,
               outs=[(jax.ShapeDtypeStruct((T, KY), F32), pl.BlockSpec((tm, tn), lambda i, j, k: (i, j)))],
               epilogue=None)[0]


def _conv_bwd(proj, dyin, conv_w8, conv_g, deps=()):
    T = proj.shape[0]
    tt = _tile(T, 256)
    nt = T // tt
    t8 = tt // 8
    nx = tt + 8

    def body(b_ref, c_ref, u_ref, d_ref, bn_ref, cn_ref, un_ref, dn_ref, cp_ref, up_ref, w_ref, g_ref, *rest):
        dp_ref, dw_ref, dg_ref = rest[len(deps):]
        i = pl.program_id(0)

        @pl.when(i == 0)
        def _():
            dw_ref[...] = jnp.zeros_like(dw_ref)
            dg_ref[...] = jnp.zeros_like(dg_ref)

        more = i < nt - 1

        def ext(cur_ref, nxt_ref):
            return jnp.concatenate([cur_ref[...], jnp.where(more, nxt_ref[...], 0.0)], axis=0)

        bx, cx, ux, dx = ext(b_ref, bn_ref), ext(c_ref, cn_ref), ext(u_ref, un_ref), ext(d_ref, dn_ref)
        hx = cx * ux
        hp = jnp.where(i > 0, cp_ref[...] * up_ref[...], 0.0)
        h1, h2 = _conv_shift(hx, hp)
        w = w_ref[...]
        g = g_ref[...]
        yx = w[0:1, :] * h2 + w[1:2, :] * h1 + w[2:3, :] * hx
        px = bx * yx
        dps, dgs = [], []
        for gi in range(CONV_GROUPS):
            sl = slice(gi * LANE, (gi + 1) * LANE)
            pg, dg_ = px[:, sl], dx[:, sl]
            r = lax.rsqrt(jnp.mean(pg * pg, axis=-1, keepdims=True) + RMS_EPS)
            gd = g[:, sl] * dg_
            dps.append(r * gd - pg * (r * r * r) * jnp.mean(pg * gd, axis=-1, keepdims=True))
            dgs.append(jnp.sum((dg_ * pg * r)[:tt, :], axis=0, keepdims=True))
        dpx = jnp.concatenate(dps, axis=1)
        dg_ref[...] += jnp.concatenate(dgs, axis=1)
        dyx = dpx * bx
        dyc = dyx[:tt, :]
        dh = (w[2:3, :] * dyx + w[1:2, :] * pltpu.roll(dyx, nx - 1, 0) + w[0:1, :] * pltpu.roll(dyx, nx - 2, 0))[:tt, :]
        dw_ref[0:1, :] += jnp.sum(dyc * h2[:tt, :], axis=0, keepdims=True)
        dw_ref[1:2, :] += jnp.sum(dyc * h1[:tt, :], axis=0, keepdims=True)
        dw_ref[2:3, :] += jnp.sum(dyc * hx[:tt, :], axis=0, keepdims=True)
        dp_ref[:, 0:D_CONV] = (dpx * yx)[:tt, :].astype(BF16)
        dp_ref[:, D_CONV:2 * D_CONV] = (dh * u_ref[...]).astype(BF16)
        dp_ref[:, 2 * D_CONV:3 * D_CONV] = (dh * c_ref[...]).astype(BF16)
        dp_ref[:, 3 * D_CONV:HALF_P] = jnp.zeros((tt, HALF_P - 3 * D_CONV), BF16)

    def col(cidx):
        return pl.BlockSpec((tt, D_CONV), lambda i: (i, cidx))

    def nxt(cidx):
        return pl.BlockSpec((8, D_CONV), lambda i: (jnp.minimum((i + 1) * t8, T // 8 - 1), cidx))

    def prev(cidx):
        return pl.BlockSpec((8, D_CONV), lambda i: (jnp.maximum(i * t8 - 1, 0), cidx))

    return pl.pallas_call(
        body,
        name="conv_bwd",
        grid=(nt,),
        in_specs=[col(0), col(1), col(2), col(0), nxt(0), nxt(1), nxt(2), nxt(0), prev(1), prev(2),
                  _full((8, D_CONV)), _full((1, D_CONV))] + _hbm_specs(len(deps)),
        out_specs=[pl.BlockSpec((tt, HALF_P), lambda i: (i, 0)), _full((8, D_CONV)), _full((1, D_CONV))],
        out_shape=[jax.ShapeDtypeStruct((T, P_INT), BF16), jax.ShapeDtypeStruct((8, D_CONV), F32),
                   jax.ShapeDtypeStruct((1, D_CONV), F32)],
        compiler_params=_params(1),
    )(proj, proj, proj, dyin, proj, proj, proj, dyin, proj, proj, conv_w8, conv_g, *deps)


def _gla_bwd(proj, wg128, gbias, gng, o_all, states, dyin, dproj):
    T = proj.shape[0]
    nch = T // CHUNK

    def body(p_ref, wg_ref, gb_ref, gn_ref, o_ref, st_ref, d_ref, dp_in_ref,
             dp_ref, dwg_ref, dgb_ref, dgn_ref, ds_ref):
        n = pl.program_id(0)

        @pl.when(n == 0)
        def _():
            ds_ref[...] = jnp.zeros_like(ds_ref)
            dwg_ref[...] = jnp.zeros_like(dwg_ref)
            dgb_ref[...] = jnp.zeros_like(dgb_ref)
            dgn_ref[...] = jnp.zeros_like(dgn_ref)

        blk = p_ref[...]
        q, k, zl, z, bcum, causal = _gla_chunk_terms(blk, wg_ref, gb_ref)
        v = blk[:, 1024:2048]
        r = blk[:, 2048:3072]
        gn = gn_ref[...]
        upper = (lax.broadcasted_iota(jnp.int32, (CHUNK, CHUNK), 0)
                 <= lax.broadcasted_iota(jnp.int32, (CHUNK, CHUNK), 1)).astype(F32)
        dlog_parts = []
        for h in range(GLA_HEADS):
            eb, enb, eend, dec, qd, ki, ke = _gla_head_terms(q, k, bcum, h)
            vs = slice(h * HEAD_V, (h + 1) * HEAD_V)
            ks = slice(h * HEAD_K, (h + 1) * HEAD_K)
            o = o_ref[:, vs]
            rh = r[:, vs]
            dyg = d_ref[:, vs]
            rinv = lax.rsqrt(jnp.mean(o * o, axis=-1, keepdims=True) + RMS_EPS)
            sg = _sigmoid(rh)
            on = o * rinv
            dr = dyg * (on * gn[:, vs]) * (sg * (1.0 + rh * (1.0 - sg)))
            don = dyg * (rh * sg)
            dgn_ref[:, vs] += jnp.sum(don * on, axis=0, keepdims=True)
            t = don * gn[:, vs]
            do = rinv * t - o * (rinv * rinv * rinv) * jnp.mean(o * t, axis=-1, keepdims=True)
            dob = do.astype(BF16)
            vb = v[:, vs].astype(BF16)
            qdb, kib, keb = qd.astype(BF16), ki.astype(BF16), ke.astype(BF16)
            a = jnp.where(causal, _dot(qdb, kib, NT), 0.0)
            st = st_ref[h]
            dst = ds_ref[h]
            dstb = dst.astype(BF16)
            da = jnp.where(causal, _dot(dob, vb, NT), 0.0)
            dab = da.astype(BF16)
            dv = _dot(a.astype(BF16), dob, TN) + _dot(keb, dstb, NT)
            dqd = _dot(dab, kib, NN) + _dot(dob, st.astype(BF16), NN)
            dki = _dot(dab, qdb, TN)
            dke = _dot(vb, dstb, NN)
            ddec = jnp.sum(st * dst, axis=0, keepdims=True)
            ds_ref[h] = dec * dst + _dot(dob, qdb, TN)
            dq = dqd * eb * (HEAD_K ** -0.5)
            dk = dki * enb + dke * eend
            db = dqd * qd - dki * ki - dke * ke
            dbl = jnp.sum(dke * ke, axis=0, keepdims=True) + dec * ddec
            dlog_parts.append(_dot(upper, db, NN, precision=lax.Precision.HIGHEST) + dbl)
            dp_ref[:, ks] = dq.astype(BF16)
            dp_ref[:, D_GLA_K + h * HEAD_K:D_GLA_K + (h + 1) * HEAD_K] = dk.astype(BF16)
            dp_ref[:, 1024 + h * HEAD_V:1024 + (h + 1) * HEAD_V] = dv.astype(BF16)
            dp_ref[:, 2048 + h * HEAD_V:2048 + (h + 1) * HEAD_V] = dr.astype(BF16)
        dlog = jnp.concatenate(dlog_parts, axis=1)
        dz = dlog * (1.0 / GATE_TAU) * (1.0 / (1.0 + jnp.exp(z)))
        dzb = dz.astype(BF16)
        dp_ref[:, 3072:3200] = _dot(dzb, wg_ref[...], NT).astype(BF16)
        dwg_ref[...] += _dot(zl.astype(BF16), dzb, TN)
        dgb_ref[...] += jnp.sum(dz, axis=0, keepdims=True)

    rev = lambda n: nch - 1 - n
    return pl.pallas_call(
        body,
        name="gla_bwd",
        grid=(nch,),
        in_specs=[pl.BlockSpec((CHUNK, HALF_P), lambda n: (rev(n), 1)), _full((LANE, D_GLA_K)), _full((1, D_GLA_K)),
                  _full((1, D_GLA_V)), pl.BlockSpec((CHUNK, D_GLA_V), lambda n: (rev(n), 0)),
                  pl.BlockSpec((None, GLA_HEADS, HEAD_V, HEAD_K), lambda n: (rev(n), 0, 0, 0)),
                  pl.BlockSpec((CHUNK, D_GLA_V), lambda n: (rev(n), 1)), pl.BlockSpec(memory_space=pl.ANY)],
        out_specs=[pl.BlockSpec((CHUNK, HALF_P), lambda n: (rev(n), 1)), _full((LANE, D_GLA_K)),
                   _full((1, D_GLA_K)), _full((1, D_GLA_V))],
        out_shape=[jax.ShapeDtypeStruct(dproj.shape, BF16), jax.ShapeDtypeStruct((LANE, D_GLA_K), F32),
                   jax.ShapeDtypeStruct((1, D_GLA_K), F32), jax.ShapeDtypeStruct((1, D_GLA_V), F32)],
        scratch_shapes=[pltpu.VMEM((GLA_HEADS, HEAD_V, HEAD_K), F32)],
        input_output_aliases={7: 0},
        compiler_params=_params(1),
    )(proj, wg128, gbias, gng, o_all, states, dyin, dproj)


def _proj_bwd_x(dproj, w_full, dh1, deps=()):
    T, P = dproj.shape
    D = w_full.shape[0]
    tm, tk = _tile(T, 512), _tile(P, 1280)

    def ep(acc_ref, ex, o, i, j):
        o[0][...] = DN_ALPHA * ex[0][...] + acc_ref[...]

    row = pl.BlockSpec((tm, D), lambda i, j, k: (i, 0))
    return _mm("proj_bwd_x", "nt", dproj, w_full, M=T, N=D, K=P, tm=tm, tn=D, tk=tk,
               outs=[(jax.ShapeDtypeStruct((T, D), F32), row)], extras=[(dh1, row)], epilogue=ep, deps=deps)[0]


def _place():
    x, y, c = lax.axis_index("x"), lax.axis_index("y"), lax.axis_index("c")
    chips = [(1 - x, y), (x, 1 - y), (1 - x, 1 - y)]
    return x, y, c, chips


def _rcopy(src, dst, ssem, rsem, dev):
    return pltpu.make_async_remote_copy(src_ref=src, dst_ref=dst, send_sem=ssem, recv_sem=rsem,
                                        device_id=dev, device_id_type=MESH)


def _all_gather(name, shards, deps=()):
    n = len(shards)

    def body(*refs):
        ins, outs = refs[:n], refs[n + len(deps):2 * n + len(deps)]
        ssem, rsem, lsem = refs[2 * n + len(deps):]
        x, y, c, chips = _place()
        me, sib = (x, y, c), (x, y, 1 - c)

        def slot(w, px, py, pc):
            return outs[w].at[4 * px + 2 * py + pc]

        started = []
        for w in range(n):
            lc = pltpu.make_async_copy(ins[w], slot(w, *me), lsem.at[w])
            lc.start()
            started.append(lc)
        sends = []
        for w in range(n):
            cp = _rcopy(ins[w], slot(w, *me), ssem.at[7 * w], rsem.at[7 * w], sib)
            cp.start()
            sends.append(cp)
            for jx, chip in enumerate(chips):
                cp = _rcopy(ins[w], slot(w, *me), ssem.at[7 * w + 1 + jx], rsem.at[7 * w + 1 + jx], (*chip, c))
                cp.start()
                sends.append(cp)
        for w in range(n):
            for jx, chip in enumerate(chips):
                blk = slot(w, *chip, c)
                _rcopy(blk, blk, ssem.at[7 * w + 1 + jx], rsem.at[7 * w + 1 + jx], me).wait_recv()
                cp = _rcopy(blk, blk, ssem.at[7 * w + 4 + jx], rsem.at[7 * w + 4 + jx], sib)
                cp.start()
                sends.append(cp)
        for w in range(n):
            blk = slot(w, x, y, 1 - c)
            _rcopy(blk, blk, ssem.at[7 * w], rsem.at[7 * w], me).wait_recv()
            for jx, chip in enumerate(chips):
                blk = slot(w, *chip, 1 - c)
                _rcopy(blk, blk, ssem.at[7 * w + 4 + jx], rsem.at[7 * w + 4 + jx], me).wait_recv()
        for cp in sends:
            cp.wait_send()
        for lc in started:
            lc.wait()

    return pl.pallas_call(
        body,
        name=name,
        in_specs=_hbm_specs(n + len(deps)),
        out_specs=_hbm_specs(n),
        out_shape=[jax.ShapeDtypeStruct((N_DEV,) + s.shape, s.dtype) for s in shards],
        scratch_shapes=[pltpu.SemaphoreType.DMA((7 * n,)), pltpu.SemaphoreType.DMA((7 * n,)),
                        pltpu.SemaphoreType.DMA((n,))],
    )(*shards, *deps)


HBM_SPEC = pl.BlockSpec(memory_space=pltpu.HBM)
SEM_SPEC = pl.BlockSpec(memory_space=pltpu.SEMAPHORE)
SIDE_EFFECT = pltpu.SideEffectType.DATAFLOW_SIDE_EFFECTING


def _cast_place(name, ids, w, dep):
    R, C = w.shape
    tr = _tile(R, 256)

    def body(ids_ref, w_ref, dep_ref, o_ref):
        o_ref[...] = w_ref[...].astype(BF16)

    return pl.pallas_call(
        body,
        name=name,
        grid_spec=pltpu.PrefetchScalarGridSpec(
            num_scalar_prefetch=1,
            grid=(R // tr,),
            in_specs=[pl.BlockSpec((tr, C), lambda r, ids: (r, 0)), pl.BlockSpec(memory_space=pl.ANY)],
            out_specs=pl.BlockSpec((None, tr, C), lambda r, ids: (ids[0], r, 0)),
        ),
        out_shape=jax.ShapeDtypeStruct((N_DEV, R, C), BF16),
        compiler_params=_params(1),
    )(ids, w, dep)


def _xfer_start(name, bufs, plan, n):
    nb = len(bufs)

    def body(*refs):
        ins = refs[:nb]
        ssem, rsem = refs[nb], refs[nb + 1]
        token = refs[2 * nb + 2]
        x, y, c, chips = _place()
        for k, (src, dst, dev, _) in enumerate(plan(ins, x, y, c, chips)):
            _rcopy(src, dst, ssem.at[k], rsem.at[k], dev).start()
        token[...] = jnp.zeros_like(token)

    res = pl.pallas_call(
        body,
        name=name,
        out_shape=(pltpu.SemaphoreType.DMA((n,)), pltpu.SemaphoreType.DMA((n,)),
                   *[pltpu.HBM(b.shape, b.dtype) for b in bufs], jax.ShapeDtypeStruct((8, LANE), F32)),
        in_specs=[HBM_SPEC] * nb,
        out_specs=(SEM_SPEC, SEM_SPEC, *[HBM_SPEC] * nb, pl.BlockSpec(memory_space=pltpu.VMEM)),
        input_output_aliases={i: 2 + i for i in range(nb)},
        compiler_params=pltpu.CompilerParams(has_side_effects=SIDE_EFFECT),
    )(*[pltpu.with_memory_space_constraint(b, pltpu.HBM) for b in bufs])
    return dict(sems=res[:2], bufs=list(res[2:2 + nb]), token=res[2 + nb], plan=plan, n=n)


def _xfer_wait(name, started, after):
    bufs, plan = started["bufs"], started["plan"]
    nb = len(bufs)

    def body(*refs):
        ins = refs[:nb]
        ssem, rsem = refs[nb], refs[nb + 1]
        x, y, c, chips = _place()
        for k, (src, _, dev, land) in enumerate(plan(ins, x, y, c, chips)):
            cp = _rcopy(src, land, ssem.at[k], rsem.at[k], dev)
            cp.wait_send()
            cp.wait_recv()

    res = pl.pallas_call(
        body,
        name=name,
        out_shape=tuple(pltpu.HBM(b.shape, b.dtype) for b in bufs),
        in_specs=[HBM_SPEC] * nb + [SEM_SPEC, SEM_SPEC, pl.BlockSpec(memory_space=pl.ANY)],
        out_specs=tuple([HBM_SPEC] * nb),
        input_output_aliases={i: i for i in range(nb)},
        compiler_params=pltpu.CompilerParams(has_side_effects=SIDE_EFFECT),
    )(*bufs, *started["sems"], after)
    return list(res)


def _plan_gather_chips(refs, x, y, c, chips):
    (land,) = refs
    mine = land.at[4 * x + 2 * y + c]
    plan = [(mine, mine, (x, y, 1 - c), land.at[4 * x + 2 * y + (1 - c)])]
    for px, py in chips:
        plan.append((mine, mine, (px, py, c), land.at[4 * px + 2 * py + c]))
    return plan


def _plan_gather_pass(refs, x, y, c, chips):
    (land,) = refs
    return [(land.at[4 * px + 2 * py + c], land.at[4 * px + 2 * py + c], (x, y, 1 - c),
             land.at[4 * px + 2 * py + (1 - c)]) for px, py in chips]


def _plan_reduce_core(refs, x, y, c, chips):
    grad, recv = refs
    return [(grad.at[2 * q + (1 - c)], recv.at[q], (x, y, 1 - c), recv.at[q]) for q in range(N_CHIP)]


def _plan_reduce_chips(refs, x, y, c, chips):
    part, land = refs
    return [(part.at[2 * px + py], land.at[2 * x + y], (px, py, c), land.at[2 * px + py]) for px, py in chips]


def _chip_sums(name, ids, grad, recv):
    _, R, C = grad.shape
    tr = _tile(R, 256)

    def body(ids_ref, g_ref, r_ref, o_ref):
        o_ref[...] = (g_ref[...] + r_ref[...]).astype(BF16)

    return pl.pallas_call(
        body,
        name=name,
        grid_spec=pltpu.PrefetchScalarGridSpec(
            num_scalar_prefetch=1,
            grid=(N_CHIP - 1, R // tr),
            in_specs=[pl.BlockSpec((None, tr, C), lambda q, r, ids: (2 * ids[3 + q] + ids[2], r, 0)),
                      pl.BlockSpec((None, tr, C), lambda q, r, ids: (ids[3 + q], r, 0))],
            out_specs=pl.BlockSpec((None, tr, C), lambda q, r, ids: (ids[3 + q], r, 0)),
        ),
        out_shape=jax.ShapeDtypeStruct((N_CHIP, R, C), BF16),
        compiler_params=_params(2),
    )(ids, grad, recv)


def _adamw(w, g, m, v):
    m = ADAM_B1 * m + (1.0 - ADAM_B1) * g
    v = ADAM_B2 * v + (1.0 - ADAM_B2) * (g * g)
    m_hat = m / (1.0 - ADAM_B1 ** ADAM_STEP)
    v_hat = v / (1.0 - ADAM_B2 ** ADAM_STEP)
    delta = -ADAM_LR * (m_hat / (jnp.sqrt(v_hat) + ADAM_EPS) + ADAM_WD * w)
    return delta, m, v


def _reduce_adamw(name, ids, grad, recv, landed, w, m, v):
    _, R, C = grad.shape
    tr = _tile(R, 256)

    def body(ids_ref, g_ref, r_ref, l1_ref, l2_ref, l3_ref, w_ref, m_ref, v_ref, go_ref, do_ref, mo_ref, vo_ref):
        g = g_ref[...] + r_ref[...]
        g = g + l1_ref[...].astype(F32)
        g = g + l2_ref[...].astype(F32)
        g = g + l3_ref[...].astype(F32)
        delta, mn, vn = _adamw(w_ref[...], g, m_ref[...], v_ref[...])
        go_ref[...] = g
        do_ref[...] = delta
        mo_ref[...] = mn
        vo_ref[...] = vn

    def pick(k):
        return pl.BlockSpec((None, tr, C), lambda r, ids: (ids[k], r, 0))

    flat = pl.BlockSpec((tr, C), lambda r, ids: (r, 0))
    shp = jax.ShapeDtypeStruct((R, C), F32)
    return pl.pallas_call(
        body,
        name=name,
        grid_spec=pltpu.PrefetchScalarGridSpec(
            num_scalar_prefetch=1,
            grid=(R // tr,),
            in_specs=[pick(0), pick(1), pick(3), pick(4), pick(5), flat, flat, flat],
            out_specs=[flat, flat, flat, flat],
        ),
        out_shape=[shp, shp, shp, shp],
        compiler_params=_params(1),
    )(ids, grad, recv, landed, landed, landed, w, m, v)


def _small_adamw(packs, w, m, v):
    def body(p_ref, w_ref, m_ref, v_ref, g_ref, d_ref, mo_ref, vo_ref):
        g = p_ref[0]
        for dvc in range(1, N_DEV):
            g = g + p_ref[dvc]
        delta, mn, vn = _adamw(w_ref[...], g, m_ref[...], v_ref[...])
        g_ref[...] = g
        d_ref[...] = delta
        mo_ref[...] = mn
        vo_ref[...] = vn

    shp = jax.ShapeDtypeStruct(w.shape, F32)
    return pl.pallas_call(
        body,
        name="small_adamw",
        in_specs=[_full(packs.shape), _full(w.shape), _full(w.shape), _full(w.shape)],
        out_specs=[_full(w.shape)] * 4,
        out_shape=[shp] * 4,
        grid=(1,),
        compiler_params=_params(1),
    )(packs, w, m, v)


def _w_in_pieces():
    cs = D_IN_PROJ // N_DEV
    pieces = []
    for d in range(N_DEV):
        lo, hi = d * cs, (d + 1) * cs
        if hi <= CONV_COLS:
            pieces.append((d, 0, cs, lo))
        elif lo >= CONV_COLS:
            pieces.append((d, 0, cs, lo - CONV_COLS + HALF_P))
        else:
            pieces.append((d, 0, CONV_COLS - lo, lo))
            pieces.append((d, CONV_COLS - lo, cs, HALF_P))
    return pieces


def _w_in_full(gathered):
    nb, D, cs = gathered.shape
    tr = _tile(D, 256)

    def body(g_ref, o_ref):
        o_ref[:, CONV_COLS:HALF_P] = jnp.zeros((tr, HALF_P - CONV_COLS), o_ref.dtype)
        o_ref[:, HALF_P + GLA_COLS:P_INT] = jnp.zeros((tr, HALF_P - GLA_COLS), o_ref.dtype)
        for d, a, b, dst in _w_in_pieces():
            o_ref[:, dst:dst + (b - a)] = g_ref[d, :, a:b]

    return pl.pallas_call(
        body,
        name="w_in_full",
        grid=(D // tr,),
        in_specs=[pl.BlockSpec((nb, tr, cs), lambda r: (0, r, 0))],
        out_specs=pl.BlockSpec((tr, P_INT), lambda r: (r, 0)),
        out_shape=jax.ShapeDtypeStruct((D, P_INT), gathered.dtype),
        compiler_params=_params(1),
    )(gathered)


def _w_in_blocks(dw):
    D = dw.shape[0]
    cs = D_IN_PROJ // N_DEV
    tr = _tile(D, 256)

    def body(w_ref, o_ref):
        for d, a, b, src in _w_in_pieces():
            o_ref[d, :, a:b] = w_ref[:, src:src + (b - a)]

    return pl.pallas_call(
        body,
        name="w_in_blocks",
        grid=(D // tr,),
        in_specs=[pl.BlockSpec((tr, P_INT), lambda r: (r, 0))],
        out_specs=pl.BlockSpec((N_DEV, tr, cs), lambda r: (0, r, 0)),
        out_shape=jax.ShapeDtypeStruct((N_DEV, D, cs), dw.dtype),
        compiler_params=_params(1),
    )(dw)


def _rows(vec, n_rows):
    flat = jnp.pad(vec.reshape(-1), (0, n_rows * SP_COLS - vec.size))
    return flat.reshape(n_rows, SP_COLS)


def _pad_cols(a):
    return jnp.pad(a, ((0, 0), (0, SP_COLS - a.shape[1])))


R_CONV_W, R_CONV_G, R_GATE_B, R_GLA_G, R_LN1_G, R_LN1_B, R_LN2_G, R_LN2_B, R_LOSS, R_GATE_W = 0, 3, 4, 5, 6, 8, 10, 12, 14, 16


def _pack(conv_w, conv_g, gate_b, gla_g, ln1_g, ln1_b, ln2_g, ln2_b, loss, gate_w):
    z = jnp.zeros((1, SP_COLS), F32)
    parts = [_pad_cols(conv_w), _pad_cols(conv_g), _pad_cols(gate_b), _pad_cols(gla_g),
             _rows(ln1_g, 2), _rows(ln1_b, 2), _rows(ln2_g, 2), _rows(ln2_b, 2),
             z if loss is None else _pad_cols(jnp.sum(loss, axis=1, keepdims=True)), z, _pad_cols(gate_w)]
    return jnp.concatenate(parts, axis=0)


def _unpack(p, D, conv_cols, gate_cols):
    return dict(
        conv_w=p[R_CONV_W:R_CONV_W + 3, :conv_cols], conv_norm_g=p[R_CONV_G:R_CONV_G + 1, :D_CONV],
        gate_bias=p[R_GATE_B:R_GATE_B + 1, :D_GLA_K], gla_norm_g=p[R_GLA_G:R_GLA_G + 1, :D_GLA_V],
        ln1_g=p[R_LN1_G:R_LN1_G + 2].reshape(1, -1)[:, :D], ln1_b=p[R_LN1_B:R_LN1_B + 2].reshape(1, -1)[:, :D],
        ln2_g=p[R_LN2_G:R_LN2_G + 2].reshape(1, -1)[:, :D], ln2_b=p[R_LN2_B:R_LN2_B + 2].reshape(1, -1)[:, :D],
        w_gate_up=p[R_GATE_W:R_GATE_W + GATE_RANK, :gate_cols])


BIG = ("w_in", "w_out", "w_ff_up", "w_ff_down")
ORDER = ("w_in", "conv_w", "conv_norm_g", "w_gate_up", "gate_bias", "gla_norm_g", "w_out", "ln1_g", "ln1_b",
         "w_ff_up", "w_ff_down", "ln2_g", "ln2_b")


def kernel(x, w_in, conv_w, conv_norm_g, w_gate_up, gate_bias, gla_norm_g, w_out, ln1_g, ln1_b, w_ff_up, w_ff_down, ln2_g, ln2_b, loss_target, m_w_in, m_conv_w, m_conv_norm_g, m_w_gate_up, m_gate_bias, m_gla_norm_g, m_w_out, m_ln1_g, m_ln1_b, m_w_ff_up, m_w_ff_down, m_ln2_g, m_ln2_b, v_w_in, v_conv_w, v_conv_norm_g, v_w_gate_up, v_gate_bias, v_gla_norm_g, v_w_out, v_ln1_g, v_ln1_b, v_w_ff_up, v_w_ff_down, v_ln2_g, v_ln2_b):
    T, D = x.shape[1], x.shape[2]
    xs, target = x[0], loss_target[0]
    xi, yi, ci = lax.axis_index("x"), lax.axis_index("y"), lax.axis_index("c")
    chip = 2 * xi + yi
    dev = 2 * chip + ci
    others = [jnp.where(chip <= q, q + 1, q) for q in range(N_CHIP - 1)]
    ids = jnp.stack([dev, chip, ci] + others).astype(jnp.int32)
    conv_cols, gate_cols = conv_w.shape[2], w_gate_up.shape[2]

    z1 = jnp.zeros((1, 1), F32)
    fwd_pack = _pack(conv_w[0], z1, z1, z1, z1, z1, z1, z1, None, w_gate_up[0])
    g_in, g_pack = _all_gather("gather_w_in", [w_in[0].astype(BF16), fwd_pack])
    conv_w_full = g_pack[:, R_CONV_W:R_CONV_W + 3, :conv_cols].transpose(1, 0, 2).reshape(3, -1)
    gate_w_full = g_pack[:, R_GATE_W:R_GATE_W + GATE_RANK, :gate_cols].transpose(1, 0, 2).reshape(GATE_RANK, -1)
    conv_w8 = jnp.pad(conv_w_full, ((0, 5), (0, 0)))
    wg128 = jnp.pad(gate_w_full, ((0, LANE - GATE_RANK), (0, 0))).astype(BF16)

    ga = [_xfer_start("gather_chips_" + nm, [_cast_place("cast_place_" + nm, ids, w[0], g_pack)], _plan_gather_chips, 4)
          for nm, w in zip(BIG[1:], (w_out, w_ff_up, w_ff_down))]

    def pass_on(nm, started, after):
        (land,) = _xfer_wait("gather_chips_wait_" + nm, started, after)
        return _xfer_start("gather_pass_" + nm, [land], _plan_gather_pass, 3)

    def landed(nm, started, after):
        return _xfer_wait("gather_pass_wait_" + nm, started, after)[0]

    w_full = _w_in_full(g_in)
    proj = _proj_fwd(xs, w_full, deps=[g["token"] for g in ga])
    yin = _conv_fwd(proj, conv_w8, conv_norm_g)
    gp_out = pass_on("w_out", ga[0], yin)
    o_all, states, yin = _gla_fwd(proj, wg128, gate_bias, gla_norm_g, yin, deps=[gp_out["token"]])
    w_out_full = landed("w_out", gp_out, o_all).reshape(-1, D)
    gp_up = pass_on("w_ff_up", ga[1], o_all)
    xhat1, x1, rstd1 = _mix_ln1(yin, w_out_full, xs, ln1_g, ln1_b, deps=[gp_up["token"]])
    w_up_blk = landed("w_ff_up", gp_up, x1)
    gp_down = pass_on("w_ff_down", ga[2], x1)
    ra, h2 = _ff_up(x1, w_up_blk, deps=[gp_down["token"]])
    w_down_full = landed("w_ff_down", gp_down, ra).reshape(-1, D)
    dh3, dh3b, g_ln2_g, g_ln2_b, loss = _ff_down_loss(h2, w_down_full, xhat1, target, ln1_g, ln1_b, ln2_g, ln2_b)

    def to_core(nm, grad):
        recv = lax.empty((N_CHIP,) + grad.shape[1:], F32)
        return _xfer_start("reduce_core_" + nm, [grad, recv], _plan_reduce_core, N_CHIP)

    def to_chips(nm, started, after):
        grad, recv = _xfer_wait("reduce_core_wait_" + nm, started, after)
        part = _chip_sums("chip_sums_" + nm, ids, grad, recv)
        land = lax.empty(part.shape, BF16)
        return grad, recv, _xfer_start("reduce_chips_" + nm, [part, land], _plan_reduce_chips, N_CHIP - 1)

    da = _ff_down_bwd_act(dh3b, w_down_full, ra)
    gw_down = _grad_w("grad_w_down", h2, dh3b).reshape(N_DEV, -1, D)
    rc_down = to_core("w_ff_down", gw_down)
    gw_up = _grad_w_up_blk(x1, da, N_DEV, deps=[rc_down["token"]])
    gw_down, rv_down, rs_down = to_chips("w_ff_down", rc_down, gw_up)
    rc_up = to_core("w_ff_up", gw_up)
    dh1, dh1b, g_ln1_g, g_ln1_b = _ff_up_bwd_ln1(da, w_up_blk, dh3, xhat1, rstd1, ln1_g,
                                                 deps=[rs_down["token"], rc_up["token"]])
    gw_up, rv_up, rs_up = to_chips("w_ff_up", rc_up, dh1b)
    dyin = _mix_bwd(dh1b, w_out_full, deps=[rs_up["token"]])
    gw_out = _grad_w("grad_w_out", yin, dh1b).reshape(N_DEV, -1, D)
    rc_out = to_core("w_out", gw_out)
    dproj, g_conv_w, g_conv_g = _conv_bwd(proj, dyin, conv_w8, conv_norm_g, deps=[rc_out["token"]])
    dproj, g_gate_w, g_gate_b, g_gla_g = _gla_bwd(proj, wg128, gate_bias, gla_norm_g, o_all, states, dyin, dproj)
    gw_out, rv_out, rs_out = to_chips("w_out", rc_out, dproj)
    gw_in = _w_in_blocks(_grad_w("grad_w_in", xs, dproj, a_fn=_to_bf16, tn_pref=1280, tk_pref=2048, deps=[rs_out["token"]]))
    rc_in = to_core("w_in", gw_in)

    big = {}

    def finish(nm, grad, recv, started, w, m, v, after):
        _, land = _xfer_wait("reduce_chips_wait_" + nm, started, after)
        res = _reduce_adamw("adamw_" + nm, ids, grad, recv, land, w[0], m[0], v[0])
        big[nm] = [a[None] for a in res]
        return res[0]

    done = finish("w_ff_down", gw_down, rv_down, rs_down, w_ff_down, m_w_ff_down, v_w_ff_down, rc_in["token"])
    done = finish("w_ff_up", gw_up, rv_up, rs_up, w_ff_up, m_w_ff_up, v_w_ff_up, done)
    gw_in, rv_in, rs_in = to_chips("w_in", rc_in, done)
    grad_x = _proj_bwd_x(dproj, w_full, dh1, deps=[rs_in["token"]])

    pack = _pack(g_conv_w[:3], g_conv_g, g_gate_b, g_gla_g, g_ln1_g, g_ln1_b, g_ln2_g, g_ln2_b, loss,
                 g_gate_w[:GATE_RANK])
    (packs,) = _all_gather("gather_small_grads", [pack], deps=[grad_x])
    done = finish("w_out", gw_out, rv_out, rs_out, w_out, m_w_out, v_w_out, packs)
    finish("w_in", gw_in, rv_in, rs_in, w_in, m_w_in, v_w_in, done)

    def own_cols(row, n_rows, width):
        cut = lax.dynamic_slice(packs, (0, row, dev * width), (N_DEV, n_rows, width))
        return jnp.pad(cut, ((0, 0), (0, 0), (0, SP_COLS - width)))

    packs_own = jnp.concatenate([own_cols(R_CONV_W, 3, conv_cols), packs[:, R_CONV_W + 3:R_GATE_W],
                                 own_cols(R_GATE_W, GATE_RANK, gate_cols)], axis=1)

    def small_pack(cw, cg, gw, gb, gg, l1g, l1b, l2g, l2b):
        return _pack(cw[0], cg, gb, gg, l1g, l1b, l2g, l2b, None, gw[0])

    w_s = small_pack(conv_w, conv_norm_g, w_gate_up, gate_bias, gla_norm_g, ln1_g, ln1_b, ln2_g, ln2_b)
    m_s = small_pack(m_conv_w, m_conv_norm_g, m_w_gate_up, m_gate_bias, m_gla_norm_g, m_ln1_g, m_ln1_b, m_ln2_g, m_ln2_b)
    v_s = small_pack(v_conv_w, v_conv_norm_g, v_w_gate_up, v_gate_bias, v_gla_norm_g, v_ln1_g, v_ln1_b, v_ln2_g, v_ln2_b)
    g_s, d_s, mn_s, vn_s = _small_adamw(packs_own, w_s, m_s, v_s)
    small = [_unpack(p, D, conv_cols, gate_cols) for p in (g_s, d_s, mn_s, vn_s)]

    def leaf(kind, name):
        if name in BIG:
            return big[name][kind]
        a = small[kind][name]
        return a[None] if name in ("conv_w", "w_gate_up") else a

    out = [g_s[R_LOSS, 0], grad_x[None]]
    for kind in range(4):
        out += [leaf(kind, nm) for nm in ORDER]
    return tuple(out)
```

```python
import jax
import jax.numpy as jnp
from jax import lax
from jax.experimental import pallas as pl
from jax.experimental.pallas import tpu as pltpu

F32 = jnp.float32
BF16 = jnp.bfloat16

D_CONV = 1024
CONV_GROUPS = 8
GLA_HEADS = 4
HEAD_K = 128
HEAD_V = 256
D_GLA_K = 512
D_GLA_V = 1024
GATE_RANK = 16
GATE_TAU = 16.0
CHUNK = 64
LN_EPS = 1e-5
RMS_EPS = 1e-6
DN_ALPHA = 2.0 ** 0.25
D_IN_PROJ = 6160
ADAM_LR = 0.001
ADAM_B1 = 0.9
ADAM_B2 = 0.999
ADAM_EPS = 1e-08
ADAM_WD = 0.01
ADAM_STEP = 10

N_DEV = 8
N_CHIP = 4
LANE = 128
HALF_P = 3200
P_INT = 2 * HALF_P
CONV_COLS = 3 * D_CONV
GLA_COLS = D_IN_PROJ - CONV_COLS
SP_ROWS = 32
SP_COLS = 1024
VMEM_LIMIT = 56 * 1024 * 1024

NN = ((1,), (0,))
NT = ((1,), (1,))
TN = ((0,), (0,))
MESH = pl.DeviceIdType.MESH


def _dot(a, b, dims, precision=None):
    return lax.dot_general(a, b, (dims, ((), ())), preferred_element_type=F32, precision=precision)


def _tile(n, pref):
    if n <= pref:
        return n
    t = (pref // LANE) * LANE
    while t > 0 and n % t:
        t -= LANE
    assert t > 0, (n, pref)
    return t


def _params(n_axes):
    return pltpu.CompilerParams(dimension_semantics=("arbitrary",) * n_axes, vmem_limit_bytes=VMEM_LIMIT)


def _full(shape):
    nd = len(shape)
    return pl.BlockSpec(shape, lambda *_: (0,) * nd)


def _hbm_specs(n):
    return [pl.BlockSpec(memory_space=pl.ANY)] * n


def _mm(name, mode, a, b, *, M, N, K, tm, tn, tk, outs, epilogue, extras=(), a_fn=None, a_spec=None, b_spec=None,
        deps=()):
    ni, nj, nk = M // tm, N // tn, K // tk
    assert ni * tm == M and nj * tn == N and nk * tk == K, (name, M, N, K, tm, tn, tk)
    if a_spec is None:
        a_spec = (pl.BlockSpec((tk, tm), lambda i, j, k: (k, i)) if mode == "tn"
                  else pl.BlockSpec((tm, tk), lambda i, j, k: (i, k)))
    if b_spec is None:
        b_spec = (pl.BlockSpec((tn, tk), lambda i, j, k: (j, k)) if mode == "nt"
                  else pl.BlockSpec((tk, tn), lambda i, j, k: (k, j)))
    dims = {"nn": NN, "nt": NT, "tn": TN}[mode]
    n_ex, n_out, n_dep = len(extras), len(outs), len(deps)

    def body(*refs):
        a_ref, b_ref = refs[0], refs[1]
        ex = refs[2:2 + n_ex]
        o = refs[2 + n_ex + n_dep:2 + n_ex + n_dep + n_out]
        acc_ref = refs[2 + n_ex + n_dep + n_out]
        i, j, k = pl.program_id(0), pl.program_id(1), pl.program_id(2)
        av = a_ref[...]
        if a_fn is not None:
            av = a_fn(av)
        part = _dot(av, b_ref[...], dims)
        if nk == 1 and epilogue is None:
            o[0][...] = part.astype(o[0].dtype)
        elif nk == 1:
            acc_ref[...] = part
            epilogue(acc_ref, ex, o, i, j)
        else:
            @pl.when(k == 0)
            def _():
                acc_ref[...] = part

            @pl.when(k > 0)
            def _():
                acc_ref[...] += part

            @pl.when(k == nk - 1)
            def _():
                if epilogue is None:
                    o[0][...] = acc_ref[...].astype(o[0].dtype)
                else:
                    epilogue(acc_ref, ex, o, i, j)

    return pl.pallas_call(
        body,
        name=name,
        grid=(ni, nj, nk),
        in_specs=[a_spec, b_spec] + [s for _, s in extras] + _hbm_specs(n_dep),
        out_specs=[s for _, s in outs],
        out_shape=[s for s, _ in outs],
        scratch_shapes=[pltpu.VMEM((8, LANE) if nk == 1 and epilogue is None else (tm, tn), F32)],
        compiler_params=_params(3),
    )(a, b, *[x for x, _ in extras], *deps)


def _mm_rows(name, mode, a, b, *, M, N, K, tm, tk, row_ins, vec_ins, row_outs, stat_outs, chunk_fn,
             b_spec=None, deps=()):
    ni, nk = M // tm, K // tk
    rc = tm // nk
    assert ni * tm == M and nk * tk == K and rc * nk == tm and rc % 16 == 0, (name, M, K, tm, tk)
    dims = {"nn": NN, "nt": NT}[mode]
    last = ni - 1

    def kk(i, k):
        return jnp.where(i < ni, k, nk - 1)

    a_spec = pl.BlockSpec((tm, tk), lambda i, k: (jnp.minimum(i, last), kk(i, k)))
    if b_spec is None:
        b_spec = (pl.BlockSpec((N, tk), lambda i, k: (0, kk(i, k))) if mode == "nt"
                  else pl.BlockSpec((tk, N), lambda i, k: (kk(i, k), 0)))
    prev_rows = lambda i, k: (jnp.maximum((i - 1) * nk + k, 0), 0)
    n_ri, n_vi, n_ro, n_so, n_dep = len(row_ins), len(vec_ins), len(row_outs), len(stat_outs), len(deps)

    def body(*refs):
        a_ref, b_ref = refs[0], refs[1]
        pos = 2
        ri = refs[pos:pos + n_ri]; pos += n_ri
        vi = refs[pos:pos + n_vi]; pos += n_vi + n_dep
        ro = refs[pos:pos + n_ro]; pos += n_ro
        so = refs[pos:pos + n_so]; pos += n_so
        accs = refs[pos:pos + 2]
        i, k = pl.program_id(0), pl.program_id(1)

        @pl.when((i == 0) & (k == 0))
        def _():
            accs[0][...] = jnp.zeros_like(accs[0])
            accs[1][...] = jnp.zeros_like(accs[1])
            for st in so:
                st[...] = jnp.zeros_like(st)

        def finish_rows(prev_ref):
            rows = pl.ds(pl.multiple_of(k * rc, rc), rc)
            done = prev_ref[rows, :]
            prev_ref[rows, :] = jnp.zeros((rc, N), F32)
            chunk_fn(done, i > 0, ri, vi, ro, so)

        for parity in (0, 1):
            @pl.when((i < ni) & (lax.rem(i, 2) == parity))
            def _(parity=parity):
                part = _dot(a_ref[...], b_ref[...], dims)
                finish_rows(accs[1 - parity])
                accs[parity][...] += part

        @pl.when(i == ni)
        def _():
            finish_rows(accs[last % 2])

    row_spec = lambda arr: pl.BlockSpec((rc, arr.shape[1]), prev_rows)
    return pl.pallas_call(
        body,
        name=name,
        grid=(ni + 1, nk),
        in_specs=[a_spec, b_spec] + [row_spec(x) for x in row_ins] + [_full(x.shape) for x in vec_ins]
        + _hbm_specs(n_dep),
        out_specs=[row_spec(s) for s in row_outs] + [_full(s.shape) for s in stat_outs],
        out_shape=list(row_outs) + list(stat_outs),
        scratch_shapes=[pltpu.VMEM((tm, N), F32), pltpu.VMEM((tm, N), F32)],
        compiler_params=_params(2),
    )(a, b, *row_ins, *vec_ins, *deps)


SUB_ROWS = 16


def _by_sub_rows(n_rows, fn):
    sums = None
    for r0 in range(0, n_rows, SUB_ROWS):
        part = fn(slice(r0, r0 + SUB_ROWS))
        if part:
            sums = part if sums is None else tuple(x + y for x, y in zip(sums, part))
    return sums


def _to_bf16(v):
    return v.astype(BF16)


def _ln_bwd(dy, xhat, rstd, g):
    dxh = dy * g
    m1 = jnp.mean(dxh, axis=-1, keepdims=True)
    m2 = jnp.mean(dxh * xhat, axis=-1, keepdims=True)
    return rstd * (dxh - m1 - xhat * m2)


def _ln_fwd(h):
    mu = jnp.mean(h, axis=-1, keepdims=True)
    xc = h - mu
    var = jnp.mean(xc * xc, axis=-1, keepdims=True)
    rstd = lax.rsqrt(var + LN_EPS)
    return xc * rstd, rstd


def _proj_fwd(x, w_full, deps=()):
    T, D = x.shape
    P = w_full.shape[1]
    tm, tn = _tile(T, 1024), _tile(P, 1280)
    return _mm("proj_fwd", "nn", x, w_full, M=T, N=P, K=D, tm=tm, tn=tn, tk=D,
               outs=[(jax.ShapeDtypeStruct((T, P), F32), pl.BlockSpec((tm, tn), lambda i, j, k: (i, j)))],
               epilogue=None, a_fn=_to_bf16, deps=deps)[0]


def _conv_shift(h, hp):
    row = lax.broadcasted_iota(jnp.int32, h.shape, 0)
    hm1 = hp[7:8, :]
    hm2 = hp[6:7, :]
    h1 = jnp.where(row == 0, hm1, pltpu.roll(h, 1, 0))
    h2 = jnp.where(row == 0, hm2, jnp.where(row == 1, hm1, pltpu.roll(h, 2, 0)))
    return h1, h2


def _conv_fwd(proj, conv_w8, conv_g):
    T = proj.shape[0]
    tt = _tile(T, 256)
    nt = T // tt
    t8 = tt // 8

    def body(b_ref, c_ref, u_ref, cp_ref, up_ref, w_ref, g_ref, yin_ref):
        i = pl.program_id(0)
        h = c_ref[...] * u_ref[...]
        hp = jnp.where(i > 0, cp_ref[...] * up_ref[...], 0.0)
        h1, h2 = _conv_shift(h, hp)
        w = w_ref[...]
        y = w[0:1, :] * h2 + w[1:2, :] * h1 + w[2:3, :] * h
        p = b_ref[...] * y
        parts = []
        for gi in range(CONV_GROUPS):
            pg = p[:, gi * LANE:(gi + 1) * LANE]
            r = lax.rsqrt(jnp.mean(pg * pg, axis=-1, keepdims=True) + RMS_EPS)
            parts.append(pg * r)
        yn = jnp.concatenate(parts, axis=1) * g_ref[...]
        yin_ref[...] = yn.astype(BF16)

    def col(cidx):
        return pl.BlockSpec((tt, D_CONV), lambda i: (i, cidx))

    def prev(cidx):
        return pl.BlockSpec((8, D_CONV), lambda i: (jnp.maximum(i * t8 - 1, 0), cidx))

    return pl.pallas_call(
        body,
        name="conv_fwd",
        grid=(nt,),
        in_specs=[col(0), col(1), col(2), prev(1), prev(2), _full((8, D_CONV)), _full((1, D_CONV))],
        out_specs=pl.BlockSpec((tt, D_CONV), lambda i: (i, 0)),
        out_shape=jax.ShapeDtypeStruct((T, 2 * D_CONV), BF16),
        compiler_params=_params(1),
    )(proj, proj, proj, proj, proj, conv_w8, conv_g)


def _log_sigmoid(z):
    return jnp.minimum(z, 0.0) - jnp.log(1.0 + jnp.exp(-jnp.abs(z)))


def _gla_chunk_terms(blk, wg_ref, gb_ref):
    q = blk[:, 0:512]
    k = blk[:, 512:1024]
    zl = blk[:, 3072:3200]
    z = _dot(zl.astype(BF16), wg_ref[...], NN) + gb_ref[...]
    log_a = _log_sigmoid(z) * (1.0 / GATE_TAU)
    ri = lax.broadcasted_iota(jnp.int32, (CHUNK, CHUNK), 0)
    ci = lax.broadcasted_iota(jnp.int32, (CHUNK, CHUNK), 1)
    causal = ri >= ci
    lower = causal.astype(F32)
    bcum = _dot(lower, log_a, NN, precision=lax.Precision.HIGHEST)
    return q, k, zl, z, bcum, causal


def _gla_head_terms(q, k, bcum, h):
    sl = slice(h * HEAD_K, (h + 1) * HEAD_K)
    bh = bcum[:, sl]
    bl = bh[CHUNK - 1:CHUNK, :]
    eb = jnp.exp(bh)
    enb = jnp.exp(-bh)
    eend = jnp.exp(bl - bh)
    dec = jnp.exp(bl)
    qd = q[:, sl] * (HEAD_K ** -0.5) * eb
    ki = k[:, sl] * enb
    ke = k[:, sl] * eend
    return eb, enb, eend, dec, qd, ki, ke


def _sigmoid(x):
    return 1.0 / (1.0 + jnp.exp(-x))


def _gla_fwd(proj, wg128, gbias, gng, yin, deps=()):
    T = proj.shape[0]
    nch = T // CHUNK

    def body(p_ref, wg_ref, gb_ref, gn_ref, yin_in_ref, *rest):
        o_ref, st_ref, yin_ref, s_ref = rest[len(deps):]
        n = pl.program_id(0)

        @pl.when(n == 0)
        def _():
            s_ref[...] = jnp.zeros_like(s_ref)

        blk = p_ref[...]
        q, k, _, _, bcum, causal = _gla_chunk_terms(blk, wg_ref, gb_ref)
        v = blk[:, 1024:2048]
        r = blk[:, 2048:3072]
        gn = gn_ref[...]
        for h in range(GLA_HEADS):
            _, _, _, dec, qd, ki, ke = _gla_head_terms(q, k, bcum, h)
            vs = slice(h * HEAD_V, (h + 1) * HEAD_V)
            vb = v[:, vs].astype(BF16)
            qdb = qd.astype(BF16)
            a = jnp.where(causal, _dot(qdb, ki.astype(BF16), NT), 0.0)
            st = s_ref[h]
            o = _dot(a.astype(BF16), vb, NN) + _dot(qdb, st.astype(BF16), NT)
            st_ref[h] = st
            s_ref[h] = dec * st + _dot(vb, ke.astype(BF16), TN)
            o_ref[:, vs] = o
            rinv = lax.rsqrt(jnp.mean(o * o, axis=-1, keepdims=True) + RMS_EPS)
            rh = r[:, vs]
            yin_ref[:, vs] = (o * rinv * gn[:, vs] * (rh * _sigmoid(rh))).astype(BF16)

    return pl.pallas_call(
        body,
        name="gla_fwd",
        grid=(nch,),
        in_specs=[pl.BlockSpec((CHUNK, HALF_P), lambda n: (n, 1)), _full((LANE, D_GLA_K)), _full((1, D_GLA_K)),
                  _full((1, D_GLA_V)), pl.BlockSpec(memory_space=pl.ANY)] + _hbm_specs(len(deps)),
        out_specs=[pl.BlockSpec((CHUNK, D_GLA_V), lambda n: (n, 0)),
                   pl.BlockSpec((None, GLA_HEADS, HEAD_V, HEAD_K), lambda n: (n, 0, 0, 0)),
                   pl.BlockSpec((CHUNK, D_GLA_V), lambda n: (n, 1))],
        out_shape=[jax.ShapeDtypeStruct((T, D_GLA_V), F32),
                   jax.ShapeDtypeStruct((nch, GLA_HEADS, HEAD_V, HEAD_K), F32),
                   jax.ShapeDtypeStruct(yin.shape, BF16)],
        scratch_shapes=[pltpu.VMEM((GLA_HEADS, HEAD_V, HEAD_K), F32)],
        input_output_aliases={4: 2},
        compiler_params=_params(1),
    )(proj, wg128, gbias, gng, yin, *deps)


def _mix_ln1(yin, w_out, x, ln_g, ln_b, deps=()):
    T, D = x.shape
    KY = yin.shape[1]
    tm = _tile(T, 1024)

    def chunk(acc, valid, ri, vi, ro, so):
        g, b = vi[0][...], vi[1][...]

        def sub(rows):
            xhat, rstd = _ln_fwd(DN_ALPHA * ri[0][rows, :] + acc[rows, :])
            ro[0][rows, :] = xhat
            ro[1][rows, :] = (xhat * g + b).astype(BF16)
            ro[2][rows, :] = rstd

        _by_sub_rows(acc.shape[0], sub)

    return _mm_rows("mix_ln1", "nn", yin, w_out, M=T, N=D, K=KY, tm=tm, tk=_tile(KY, 512),
                    row_ins=[x], vec_ins=[ln_g, ln_b],
                    row_outs=[jax.ShapeDtypeStruct((T, D), F32), jax.ShapeDtypeStruct((T, D), BF16),
                              jax.ShapeDtypeStruct((T, 1), F32)],
                    stat_outs=[], chunk_fn=chunk, deps=deps)


def _ff_up(x1, w_up_blk, deps=()):
    T, D = x1.shape
    nb, _, fb = w_up_blk.shape
    tm = _tile(T, 1024)

    def ep(acc_ref, ex, o, i, j):
        ra = jnp.maximum(acc_ref[...], 0.0)
        o[0][...] = ra.astype(BF16)
        o[1][...] = (ra * ra).astype(BF16)

    blk = pl.BlockSpec((tm, fb), lambda i, j, k: (i, j))
    shp = jax.ShapeDtypeStruct((T, nb * fb), BF16)
    return _mm("ff_up", "nn", x1, w_up_blk, M=T, N=nb * fb, K=D, tm=tm, tn=fb, tk=D,
               b_spec=pl.BlockSpec((None, D, fb), lambda i, j, k: (j, 0, 0)),
               outs=[(shp, blk), (shp, blk)], epilogue=ep, deps=deps)


def _ff_down_loss(h2, w_down, xhat1, target, g1, b1, g2, b2):
    T, F = h2.shape
    D = w_down.shape[1]
    tm = _tile(T, 1024)
    inv_d = 1.0 / D

    def chunk(acc, valid, ri, vi, ro, so):
        g1v, b1v, g2v, b2v = (v[...] for v in vi)

        def sub(rows):
            x1 = ri[0][rows, :] * g1v + b1v
            xhat, rstd = _ln_fwd(DN_ALPHA * x1 + acc[rows, :])
            e = xhat * g2v + b2v - ri[1][rows, :]
            dy = e * inv_d
            dh = _ln_bwd(dy, xhat, rstd, g2v)
            ro[0][rows, :] = dh
            ro[1][rows, :] = dh.astype(BF16)
            return (jnp.sum(dy * xhat, axis=0, keepdims=True), jnp.sum(dy, axis=0, keepdims=True),
                    jnp.sum(e * e, axis=0, keepdims=True))

        sg, sb, sl = _by_sub_rows(acc.shape[0], sub)
        so[0][...] += jnp.where(valid, sg, 0.0)
        so[1][...] += jnp.where(valid, sb, 0.0)
        so[2][...] += jnp.where(valid, sl * (0.5 * inv_d), 0.0)

    vshape = jax.ShapeDtypeStruct((1, D), F32)
    return _mm_rows("ff_down_loss", "nn", h2, w_down, M=T, N=D, K=F, tm=tm, tk=_tile(F, 1024),
                    row_ins=[xhat1, target], vec_ins=[g1, b1, g2, b2],
                    row_outs=[jax.ShapeDtypeStruct((T, D), F32), jax.ShapeDtypeStruct((T, D), BF16)],
                    stat_outs=[vshape, vshape, vshape], chunk_fn=chunk)


def _ff_down_bwd_act(dh3b, w_down, ra):
    T, D = dh3b.shape
    F = w_down.shape[0]
    tm, tn = _tile(T, 1024), _tile(F, 1024)

    def ep(acc_ref, ex, o, i, j):
        o[0][...] = (acc_ref[...] * (2.0 * ex[0][...].astype(F32))).astype(BF16)

    blk = pl.BlockSpec((tm, tn), lambda i, j, k: (i, j))
    return _mm("ff_down_bwd_act", "nt", dh3b, w_down, M=T, N=F, K=D, tm=tm, tn=tn, tk=D,
               outs=[(jax.ShapeDtypeStruct((T, F), BF16), blk)], extras=[(ra, blk)], epilogue=ep)[0]


def _grad_w(name, a, b, *, a_fn=None, tm_pref=1024, tn_pref=1024, tk_pref=4096, deps=()):
    T, M = a.shape
    N = b.shape[1]
    tm, tn, tk = _tile(M, tm_pref), _tile(N, tn_pref), _tile(T, tk_pref)
    return _mm(name, "tn", a, b, M=M, N=N, K=T, tm=tm, tn=tn, tk=tk, a_fn=a_fn, deps=deps,
               outs=[(jax.ShapeDtypeStruct((M, N), F32), pl.BlockSpec((tm, tn), lambda i, j, k: (i, j)))],
               epilogue=None)[0]


def _grad_w_up_blk(x1, da, nb, deps=()):
    T, D = x1.shape
    F = da.shape[1]
    fb = F // nb
    tm, tk = _tile(D, 1024), _tile(T, 4096)
    return _mm("grad_w_up", "tn", x1, da, M=D, N=F, K=T, tm=tm, tn=fb, tk=tk, deps=deps,
               outs=[(jax.ShapeDtypeStruct((nb, D, fb), F32),
                      pl.BlockSpec((None, tm, fb), lambda i, j, k: (j, i, 0)))],
               epilogue=None)[0]


def _ff_up_bwd_ln1(da, w_up_blk, dh3, xhat1, rstd1, g1, deps=()):
    T, F = da.shape
    nb, D, fb = w_up_blk.shape
    tm = _tile(T, 1024)

    def chunk(acc, valid, ri, vi, ro, so):
        g = vi[0][...]

        def sub(rows):
            dx1 = DN_ALPHA * ri[0][rows, :] + acc[rows, :]
            xhat = ri[1][rows, :]
            dh = _ln_bwd(dx1, xhat, ri[2][rows, :], g)
            ro[0][rows, :] = dh
            ro[1][rows, :] = dh.astype(BF16)
            return jnp.sum(dx1 * xhat, axis=0, keepdims=True), jnp.sum(dx1, axis=0, keepdims=True)

        sg, sb = _by_sub_rows(acc.shape[0], sub)
        so[0][...] += jnp.where(valid, sg, 0.0)
        so[1][...] += jnp.where(valid, sb, 0.0)

    nk = F // fb
    vshape = jax.ShapeDtypeStruct((1, D), F32)
    return _mm_rows("ff_up_bwd_ln1", "nt", da, w_up_blk, M=T, N=D, K=F, tm=tm, tk=fb,
                    b_spec=pl.BlockSpec((None, D, fb), lambda i, k: (jnp.where(i < T // tm, k, nk - 1), 0, 0)),
                    row_ins=[dh3, xhat1, rstd1], vec_ins=[g1],
                    row_outs=[jax.ShapeDtypeStruct((T, D), F32), jax.ShapeDtypeStruct((T, D), BF16)],
                    stat_outs=[vshape, vshape], chunk_fn=chunk, deps=deps)


def _mix_bwd(dh1b, w_out, deps=()):
    T, D = dh1b.shape
    KY = w_out.shape[0]
    tm, tn = _tile(T, 1024), _tile(KY, 1024)
    return _mm("mix_bwd", "nt", dh1b, w_out, M=T, N=KY, K=D, tm=tm, tn=tn, tk=D, deps=deps,
               outs=[(jax.ShapeDtypeStruct((T, KY), F32), pl.BlockSpec((tm, tn), lambda i, j, k: (i, j)))],
               epilogue=None)[0]


def _conv_bwd(proj, dyin, conv_w8, conv_g, deps=()):
    T = proj.shape[0]
    tt = _tile(T, 256)
    nt = T // tt
    t8 = tt // 8
    nx = tt + 8

    def body(b_ref, c_ref, u_ref, d_ref, bn_ref, cn_ref, un_ref, dn_ref, cp_ref, up_ref, w_ref, g_ref, *rest):
        dp_ref, dw_ref, dg_ref = rest[len(deps):]
        i = pl.program_id(0)

        @pl.when(i == 0)
        def _():
            dw_ref[...] = jnp.zeros_like(dw_ref)
            dg_ref[...] = jnp.zeros_like(dg_ref)

        more = i < nt - 1

        def ext(cur_ref, nxt_ref):
            return jnp.concatenate([cur_ref[...], jnp.where(more, nxt_ref[...], 0.0)], axis=0)

        bx, cx, ux, dx = ext(b_ref, bn_ref), ext(c_ref, cn_ref), ext(u_ref, un_ref), ext(d_ref, dn_ref)
        hx = cx * ux
        hp = jnp.where(i > 0, cp_ref[...] * up_ref[...], 0.0)
        h1, h2 = _conv_shift(hx, hp)
        w = w_ref[...]
        g = g_ref[...]
        yx = w[0:1, :] * h2 + w[1:2, :] * h1 + w[2:3, :] * hx
        px = bx * yx
        dps, dgs = [], []
        for gi in range(CONV_GROUPS):
            sl = slice(gi * LANE, (gi + 1) * LANE)
            pg, dg_ = px[:, sl], dx[:, sl]
            r = lax.rsqrt(jnp.mean(pg * pg, axis=-1, keepdims=True) + RMS_EPS)
            gd = g[:, sl] * dg_
            dps.append(r * gd - pg * (r * r * r) * jnp.mean(pg * gd, axis=-1, keepdims=True))
            dgs.append(jnp.sum((dg_ * pg * r)[:tt, :], axis=0, keepdims=True))
        dpx = jnp.concatenate(dps, axis=1)
        dg_ref[...] += jnp.concatenate(dgs, axis=1)
        dyx = dpx * bx
        dyc = dyx[:tt, :]
        dh = (w[2:3, :] * dyx + w[1:2, :] * pltpu.roll(dyx, nx - 1, 0) + w[0:1, :] * pltpu.roll(dyx, nx - 2, 0))[:tt, :]
        dw_ref[0:1, :] += jnp.sum(dyc * h2[:tt, :], axis=0, keepdims=True)
        dw_ref[1:2, :] += jnp.sum(dyc * h1[:tt, :], axis=0, keepdims=True)
        dw_ref[2:3, :] += jnp.sum(dyc * hx[:tt, :], axis=0, keepdims=True)
        dp_ref[:, 0:D_CONV] = (dpx * yx)[:tt, :].astype(BF16)
        dp_ref[:, D_CONV:2 * D_CONV] = (dh * u_ref[...]).astype(BF16)
        dp_ref[:, 2 * D_CONV:3 * D_CONV] = (dh * c_ref[...]).astype(BF16)
        dp_ref[:, 3 * D_CONV:HALF_P] = jnp.zeros((tt, HALF_P - 3 * D_CONV), BF16)

    def col(cidx):
        return pl.BlockSpec((tt, D_CONV), lambda i: (i, cidx))

    def nxt(cidx):
        return pl.BlockSpec((8, D_CONV), lambda i: (jnp.minimum((i + 1) * t8, T // 8 - 1), cidx))

    def prev(cidx):
        return pl.BlockSpec((8, D_CONV), lambda i: (jnp.maximum(i * t8 - 1, 0), cidx))

    return pl.pallas_call(
        body,
        name="conv_bwd",
        grid=(nt,),
        in_specs=[col(0), col(1), col(2), col(0), nxt(0), nxt(1), nxt(2), nxt(0), prev(1), prev(2),
                  _full((8, D_CONV)), _full((1, D_CONV))] + _hbm_specs(len(deps)),
        out_specs=[pl.BlockSpec((tt, HALF_P), lambda i: (i, 0)), _full((8, D_CONV)), _full((1, D_CONV))],
        out_shape=[jax.ShapeDtypeStruct((T, P_INT), BF16), jax.ShapeDtypeStruct((8, D_CONV), F32),
                   jax.ShapeDtypeStruct((1, D_CONV), F32)],
        compiler_params=_params(1),
    )(proj, proj, proj, dyin, proj, proj, proj, dyin, proj, proj, conv_w8, conv_g, *deps)


def _gla_bwd(proj, wg128, gbias, gng, o_all, states, dyin, dproj):
    T = proj.shape[0]
    nch = T // CHUNK

    def body(p_ref, wg_ref, gb_ref, gn_ref, o_ref, st_ref, d_ref, dp_in_ref,
             dp_ref, dwg_ref, dgb_ref, dgn_ref, ds_ref):
        n = pl.program_id(0)

        @pl.when(n == 0)
        def _():
            ds_ref[...] = jnp.zeros_like(ds_ref)
            dwg_ref[...] = jnp.zeros_like(dwg_ref)
            dgb_ref[...] = jnp.zeros_like(dgb_ref)
            dgn_ref[...] = jnp.zeros_like(dgn_ref)

        blk = p_ref[...]
        q, k, zl, z, bcum, causal = _gla_chunk_terms(blk, wg_ref, gb_ref)
        v = blk[:, 1024:2048]
        r = blk[:, 2048:3072]
        gn = gn_ref[...]
        upper = (lax.broadcasted_iota(jnp.int32, (CHUNK, CHUNK), 0)
                 <= lax.broadcasted_iota(jnp.int32, (CHUNK, CHUNK), 1)).astype(F32)
        dlog_parts = []
        for h in range(GLA_HEADS):
            eb, enb, eend, dec, qd, ki, ke = _gla_head_terms(q, k, bcum, h)
            vs = slice(h * HEAD_V, (h + 1) * HEAD_V)
            ks = slice(h * HEAD_K, (h + 1) * HEAD_K)
            o = o_ref[:, vs]
            rh = r[:, vs]
            dyg = d_ref[:, vs]
            rinv = lax.rsqrt(jnp.mean(o * o, axis=-1, keepdims=True) + RMS_EPS)
            sg = _sigmoid(rh)
            on = o * rinv
            dr = dyg * (on * gn[:, vs]) * (sg * (1.0 + rh * (1.0 - sg)))
            don = dyg * (rh * sg)
            dgn_ref[:, vs] += jnp.sum(don * on, axis=0, keepdims=True)
            t = don * gn[:, vs]
            do = rinv * t - o * (rinv * rinv * rinv) * jnp.mean(o * t, axis=-1, keepdims=True)
            dob = do.astype(BF16)
            vb = v[:, vs].astype(BF16)
            qdb, kib, keb = qd.astype(BF16), ki.astype(BF16), ke.astype(BF16)
            a = jnp.where(causal, _dot(qdb, kib, NT), 0.0)
            st = st_ref[h]
            dst = ds_ref[h]
            dstb = dst.astype(BF16)
            da = jnp.where(causal, _dot(dob, vb, NT), 0.0)
            dab = da.astype(BF16)
            dv = _dot(a.astype(BF16), dob, TN) + _dot(keb, dstb, NT)
            dqd = _dot(dab, kib, NN) + _dot(dob, st.astype(BF16), NN)
            dki = _dot(dab, qdb, TN)
            dke = _dot(vb, dstb, NN)
            ddec = jnp.sum(st * dst, axis=0, keepdims=True)
            ds_ref[h] = dec * dst + _dot(dob, qdb, TN)
            dq = dqd * eb * (HEAD_K ** -0.5)
            dk = dki * enb + dke * eend
            db = dqd * qd - dki * ki - dke * ke
            dbl = jnp.sum(dke * ke, axis=0, keepdims=True) + dec * ddec
            dlog_parts.append(_dot(upper, db, NN, precision=lax.Precision.HIGHEST) + dbl)
            dp_ref[:, ks] = dq.astype(BF16)
            dp_ref[:, D_GLA_K + h * HEAD_K:D_GLA_K + (h + 1) * HEAD_K] = dk.astype(BF16)
            dp_ref[:, 1024 + h * HEAD_V:1024 + (h + 1) * HEAD_V] = dv.astype(BF16)
            dp_ref[:, 2048 + h * HEAD_V:2048 + (h + 1) * HEAD_V] = dr.astype(BF16)
        dlog = jnp.concatenate(dlog_parts, axis=1)
        dz = dlog * (1.0 / GATE_TAU) * (1.0 / (1.0 + jnp.exp(z)))
        dzb = dz.astype(BF16)
        dp_ref[:, 3072:3200] = _dot(dzb, wg_ref[...], NT).astype(BF16)
        dwg_ref[...] += _dot(zl.astype(BF16), dzb, TN)
        dgb_ref[...] += jnp.sum(dz, axis=0, keepdims=True)

    rev = lambda n: nch - 1 - n
    return pl.pallas_call(
        body,
        name="gla_bwd",
        grid=(nch,),
        in_specs=[pl.BlockSpec((CHUNK, HALF_P), lambda n: (rev(n), 1)), _full((LANE, D_GLA_K)), _full((1, D_GLA_K)),
                  _full((1, D_GLA_V)), pl.BlockSpec((CHUNK, D_GLA_V), lambda n: (rev(n), 0)),
                  pl.BlockSpec((None, GLA_HEADS, HEAD_V, HEAD_K), lambda n: (rev(n), 0, 0, 0)),
                  pl.BlockSpec((CHUNK, D_GLA_V), lambda n: (rev(n), 1)), pl.BlockSpec(memory_space=pl.ANY)],
        out_specs=[pl.BlockSpec((CHUNK, HALF_P), lambda n: (rev(n), 1)), _full((LANE, D_GLA_K)),
                   _full((1, D_GLA_K)), _full((1, D_GLA_V))],
        out_shape=[jax.ShapeDtypeStruct(dproj.shape, BF16), jax.ShapeDtypeStruct((LANE, D_GLA_K), F32),
                   jax.ShapeDtypeStruct((1, D_GLA_K), F32), jax.ShapeDtypeStruct((1, D_GLA_V), F32)],
        scratch_shapes=[pltpu.VMEM((GLA_HEADS, HEAD_V, HEAD_K), F32)],
        input_output_aliases={7: 0},
        compiler_params=_params(1),
    )(proj, wg128, gbias, gng, o_all, states, dyin, dproj)


def _proj_bwd_x(dproj, w_full, dh1, deps=()):
    T, P = dproj.shape
    D = w_full.shape[0]
    tm, tk = _tile(T, 512), _tile(P, 1280)

    def ep(acc_ref, ex, o, i, j):
        o[0][...] = DN_ALPHA * ex[0][...] + acc_ref[...]

    row = pl.BlockSpec((tm, D), lambda i, j, k: (i, 0))
    return _mm("proj_bwd_x", "nt", dproj, w_full, M=T, N=D, K=P, tm=tm, tn=D, tk=tk,
               outs=[(jax.ShapeDtypeStruct((T, D), F32), row)], extras=[(dh1, row)], epilogue=ep, deps=deps)[0]


def _place():
    x, y, c = lax.axis_index("x"), lax.axis_index("y"), lax.axis_index("c")
    chips = [(1 - x, y), (x, 1 - y), (1 - x, 1 - y)]
    return x, y, c, chips


def _rcopy(src, dst, ssem, rsem, dev):
    return pltpu.make_async_remote_copy(src_ref=src, dst_ref=dst, send_sem=ssem, recv_sem=rsem,
                                        device_id=dev, device_id_type=MESH)


def _all_gather(name, shards, deps=()):
    n = len(shards)

    def body(*refs):
        ins, outs = refs[:n], refs[n + len(deps):2 * n + len(deps)]
        ssem, rsem, lsem = refs[2 * n + len(deps):]
        x, y, c, chips = _place()
        me, sib = (x, y, c), (x, y, 1 - c)

        def slot(w, px, py, pc):
            return outs[w].at[4 * px + 2 * py + pc]

        started = []
        for w in range(n):
            lc = pltpu.make_async_copy(ins[w], slot(w, *me), lsem.at[w])
            lc.start()
            started.append(lc)
        sends = []
        for w in range(n):
            cp = _rcopy(ins[w], slot(w, *me), ssem.at[7 * w], rsem.at[7 * w], sib)
            cp.start()
            sends.append(cp)
            for jx, chip in enumerate(chips):
                cp = _rcopy(ins[w], slot(w, *me), ssem.at[7 * w + 1 + jx], rsem.at[7 * w + 1 + jx], (*chip, c))
                cp.start()
                sends.append(cp)
        for w in range(n):
            for jx, chip in enumerate(chips):
                blk = slot(w, *chip, c)
                _rcopy(blk, blk, ssem.at[7 * w + 1 + jx], rsem.at[7 * w + 1 + jx], me).wait_recv()
                cp = _rcopy(blk, blk, ssem.at[7 * w + 4 + jx], rsem.at[7 * w + 4 + jx], sib)
                cp.start()
                sends.append(cp)
        for w in range(n):
            blk = slot(w, x, y, 1 - c)
            _rcopy(blk, blk, ssem.at[7 * w], rsem.at[7 * w], me).wait_recv()
            for jx, chip in enumerate(chips):
                blk = slot(w, *chip, 1 - c)
                _rcopy(blk, blk, ssem.at[7 * w + 4 + jx], rsem.at[7 * w + 4 + jx], me).wait_recv()
        for cp in sends:
            cp.wait_send()
        for lc in started:
            lc.wait()

    return pl.pallas_call(
        body,
        name=name,
        in_specs=_hbm_specs(n + len(deps)),
        out_specs=_hbm_specs(n),
        out_shape=[jax.ShapeDtypeStruct((N_DEV,) + s.shape, s.dtype) for s in shards],
        scratch_shapes=[pltpu.SemaphoreType.DMA((7 * n,)), pltpu.SemaphoreType.DMA((7 * n,)),
                        pltpu.SemaphoreType.DMA((n,))],
    )(*shards, *deps)


HBM_SPEC = pl.BlockSpec(memory_space=pltpu.HBM)
SEM_SPEC = pl.BlockSpec(memory_space=pltpu.SEMAPHORE)
SIDE_EFFECT = pltpu.SideEffectType.DATAFLOW_SIDE_EFFECTING


def _cast_place(name, ids, w, dep):
    R, C = w.shape
    tr = _tile(R, 256)

    def body(ids_ref, w_ref, dep_ref, o_ref):
        o_ref[...] = w_ref[...].astype(BF16)

    return pl.pallas_call(
        body,
        name=name,
        grid_spec=pltpu.PrefetchScalarGridSpec(
            num_scalar_prefetch=1,
            grid=(R // tr,),
            in_specs=[pl.BlockSpec((tr, C), lambda r, ids: (r, 0)), pl.BlockSpec(memory_space=pl.ANY)],
            out_specs=pl.BlockSpec((None, tr, C), lambda r, ids: (ids[0], r, 0)),
        ),
        out_shape=jax.ShapeDtypeStruct((N_DEV, R, C), BF16),
        compiler_params=_params(1),
    )(ids, w, dep)


def _xfer_start(name, bufs, plan, n):
    nb = len(bufs)

    def body(*refs):
        ins = refs[:nb]
        ssem, rsem = refs[nb], refs[nb + 1]
        token = refs[2 * nb + 2]
        x, y, c, chips = _place()
        for k, (src, dst, dev, _) in enumerate(plan(ins, x, y, c, chips)):
            _rcopy(src, dst, ssem.at[k], rsem.at[k], dev).start()
        token[...] = jnp.zeros_like(token)

    res = pl.pallas_call(
        body,
        name=name,
        out_shape=(pltpu.SemaphoreType.DMA((n,)), pltpu.SemaphoreType.DMA((n,)),
                   *[pltpu.HBM(b.shape, b.dtype) for b in bufs], jax.ShapeDtypeStruct((8, LANE), F32)),
        in_specs=[HBM_SPEC] * nb,
        out_specs=(SEM_SPEC, SEM_SPEC, *[HBM_SPEC] * nb, pl.BlockSpec(memory_space=pltpu.VMEM)),
        input_output_aliases={i: 2 + i for i in range(nb)},
        compiler_params=pltpu.CompilerParams(has_side_effects=SIDE_EFFECT),
    )(*[pltpu.with_memory_space_constraint(b, pltpu.HBM) for b in bufs])
    return dict(sems=res[:2], bufs=list(res[2:2 + nb]), token=res[2 + nb], plan=plan, n=n)


def _xfer_wait(name, started, after):
    bufs, plan = started["bufs"], started["plan"]
    nb = len(bufs)

    def body(*refs):
        ins = refs[:nb]
        ssem, rsem = refs[nb], refs[nb + 1]
        x, y, c, chips = _place()
        for k, (src, _, dev, land) in enumerate(plan(ins, x, y, c, chips)):
            cp = _rcopy(src, land, ssem.at[k], rsem.at[k], dev)
            cp.wait_send()
            cp.wait_recv()

    res = pl.pallas_call(
        body,
        name=name,
        out_shape=tuple(pltpu.HBM(b.shape, b.dtype) for b in bufs),
        in_specs=[HBM_SPEC] * nb + [SEM_SPEC, SEM_SPEC, pl.BlockSpec(memory_space=pl.ANY)],
        out_specs=tuple([HBM_SPEC] * nb),
        input_output_aliases={i: i for i in range(nb)},
        compiler_params=pltpu.CompilerParams(has_side_effects=SIDE_EFFECT),
    )(*bufs, *started["sems"], after)
    return list(res)


def _plan_gather_chips(refs, x, y, c, chips):
    (land,) = refs
    mine = land.at[4 * x + 2 * y + c]
    plan = [(mine, mine, (x, y, 1 - c), land.at[4 * x + 2 * y + (1 - c)])]
    for px, py in chips:
        plan.append((mine, mine, (px, py, c), land.at[4 * px + 2 * py + c]))
    return plan


def _plan_gather_pass(refs, x, y, c, chips):
    (land,) = refs
    return [(land.at[4 * px + 2 * py + c], land.at[4 * px + 2 * py + c], (x, y, 1 - c),
             land.at[4 * px + 2 * py + (1 - c)]) for px, py in chips]


def _plan_reduce_core(refs, x, y, c, chips):
    grad, recv = refs
    return [(grad.at[2 * q + (1 - c)], recv.at[q], (x, y, 1 - c), recv.at[q]) for q in range(N_CHIP)]


def _plan_reduce_chips(refs, x, y, c, chips):
    part, land = refs
    return [(part.at[2 * px + py], land.at[2 * x + y], (px, py, c), land.at[2 * px + py]) for px, py in chips]


def _chip_sums(name, ids, grad, recv):
    _, R, C = grad.shape
    tr = _tile(R, 256)

    def body(ids_ref, g_ref, r_ref, o_ref):
        o_ref[...] = (g_ref[...] + r_ref[...]).astype(BF16)

    return pl.pallas_call(
        body,
        name=name,
        grid_spec=pltpu.PrefetchScalarGridSpec(
            num_scalar_prefetch=1,
            grid=(N_CHIP - 1, R // tr),
            in_specs=[pl.BlockSpec((None, tr, C), lambda q, r, ids: (2 * ids[3 + q] + ids[2], r, 0)),
                      pl.BlockSpec((None, tr, C), lambda q, r, ids: (ids[3 + q], r, 0))],
            out_specs=pl.BlockSpec((None, tr, C), lambda q, r, ids: (ids[3 + q], r, 0)),
        ),
        out_shape=jax.ShapeDtypeStruct((N_CHIP, R, C), BF16),
        compiler_params=_params(2),
    )(ids, grad, recv)


def _adamw(w, g, m, v):
    m = ADAM_B1 * m + (1.0 - ADAM_B1) * g
    v = ADAM_B2 * v + (1.0 - ADAM_B2) * (g * g)
    m_hat = m / (1.0 - ADAM_B1 ** ADAM_STEP)
    v_hat = v / (1.0 - ADAM_B2 ** ADAM_STEP)
    delta = -ADAM_LR * (m_hat / (jnp.sqrt(v_hat) + ADAM_EPS) + ADAM_WD * w)
    return delta, m, v


def _reduce_adamw(name, ids, grad, recv, landed, w, m, v):
    _, R, C = grad.shape
    tr = _tile(R, 256)

    def body(ids_ref, g_ref, r_ref, l1_ref, l2_ref, l3_ref, w_ref, m_ref, v_ref, go_ref, do_ref, mo_ref, vo_ref):
        g = g_ref[...] + r_ref[...]
        g = g + l1_ref[...].astype(F32)
        g = g + l2_ref[...].astype(F32)
        g = g + l3_ref[...].astype(F32)
        delta, mn, vn = _adamw(w_ref[...], g, m_ref[...], v_ref[...])
        go_ref[...] = g
        do_ref[...] = delta
        mo_ref[...] = mn
        vo_ref[...] = vn

    def pick(k):
        return pl.BlockSpec((None, tr, C), lambda r, ids: (ids[k], r, 0))

    flat = pl.BlockSpec((tr, C), lambda r, ids: (r, 0))
    shp = jax.ShapeDtypeStruct((R, C), F32)
    return pl.pallas_call(
        body,
        name=name,
        grid_spec=pltpu.PrefetchScalarGridSpec(
            num_scalar_prefetch=1,
            grid=(R // tr,),
            in_specs=[pick(0), pick(1), pick(3), pick(4), pick(5), flat, flat, flat],
            out_specs=[flat, flat, flat, flat],
        ),
        out_shape=[shp, shp, shp, shp],
        compiler_params=_params(1),
    )(ids, grad, recv, landed, landed, landed, w, m, v)


def _small_adamw(packs, w, m, v):
    def body(p_ref, w_ref, m_ref, v_ref, g_ref, d_ref, mo_ref, vo_ref):
        g = p_ref[0]
        for dvc in range(1, N_DEV):
            g = g + p_ref[dvc]
        delta, mn, vn = _adamw(w_ref[...], g, m_ref[...], v_ref[...])
        g_ref[...] = g
        d_ref[...] = delta
        mo_ref[...] = mn
        vo_ref[...] = vn

    shp = jax.ShapeDtypeStruct(w.shape, F32)
    return pl.pallas_call(
        body,
        name="small_adamw",
        in_specs=[_full(packs.shape), _full(w.shape), _full(w.shape), _full(w.shape)],
        out_specs=[_full(w.shape)] * 4,
        out_shape=[shp] * 4,
        grid=(1,),
        compiler_params=_params(1),
    )(packs, w, m, v)


def _w_in_pieces():
    cs = D_IN_PROJ // N_DEV
    pieces = []
    for d in range(N_DEV):
        lo, hi = d * cs, (d + 1) * cs
        if hi <= CONV_COLS:
            pieces.append((d, 0, cs, lo))
        elif lo >= CONV_COLS:
            pieces.append((d, 0, cs, lo - CONV_COLS + HALF_P))
        else:
            pieces.append((d, 0, CONV_COLS - lo, lo))
            pieces.append((d, CONV_COLS - lo, cs, HALF_P))
    return pieces


def _w_in_full(gathered):
    nb, D, cs = gathered.shape
    tr = _tile(D, 256)

    def body(g_ref, o_ref):
        o_ref[:, CONV_COLS:HALF_P] = jnp.zeros((tr, HALF_P - CONV_COLS), o_ref.dtype)
        o_ref[:, HALF_P + GLA_COLS:P_INT] = jnp.zeros((tr, HALF_P - GLA_COLS), o_ref.dtype)
        for d, a, b, dst in _w_in_pieces():
            o_ref[:, dst:dst + (b - a)] = g_ref[d, :, a:b]

    return pl.pallas_call(
        body,
        name="w_in_full",
        grid=(D // tr,),
        in_specs=[pl.BlockSpec((nb, tr, cs), lambda r: (0, r, 0))],
        out_specs=pl.BlockSpec((tr, P_INT), lambda r: (r, 0)),
        out_shape=jax.ShapeDtypeStruct((D, P_INT), gathered.dtype),
        compiler_params=_params(1),
    )(gathered)


def _w_in_blocks(dw):
    D = dw.shape[0]
    cs = D_IN_PROJ // N_DEV
    tr = _tile(D, 256)

    def body(w_ref, o_ref):
        for d, a, b, src in _w_in_pieces():
            o_ref[d, :, a:b] = w_ref[:, src:src + (b - a)]

    return pl.pallas_call(
        body,
        name="w_in_blocks",
        grid=(D // tr,),
        in_specs=[pl.BlockSpec((tr, P_INT), lambda r: (r, 0))],
        out_specs=pl.BlockSpec((N_DEV, tr, cs), lambda r: (0, r, 0)),
        out_shape=jax.ShapeDtypeStruct((N_DEV, D, cs), dw.dtype),
        compiler_params=_params(1),
    )(dw)


def _rows(vec, n_rows):
    flat = jnp.pad(vec.reshape(-1), (0, n_rows * SP_COLS - vec.size))
    return flat.reshape(n_rows, SP_COLS)


def _pad_cols(a):
    return jnp.pad(a, ((0, 0), (0, SP_COLS - a.shape[1])))


R_CONV_W, R_CONV_G, R_GATE_B, R_GLA_G, R_LN1_G, R_LN1_B, R_LN2_G, R_LN2_B, R_LOSS, R_GATE_W = 0, 3, 4, 5, 6, 8, 10, 12, 14, 16


def _pack(conv_w, conv_g, gate_b, gla_g, ln1_g, ln1_b, ln2_g, ln2_b, loss, gate_w):
    z = jnp.zeros((1, SP_COLS), F32)
    parts = [_pad_cols(conv_w), _pad_cols(conv_g), _pad_cols(gate_b), _pad_cols(gla_g),
             _rows(ln1_g, 2), _rows(ln1_b, 2), _rows(ln2_g, 2), _rows(ln2_b, 2),
             z if loss is None else _pad_cols(jnp.sum(loss, axis=1, keepdims=True)), z, _pad_cols(gate_w)]
    return jnp.concatenate(parts, axis=0)


def _unpack(p, D, conv_cols, gate_cols):
    return dict(
        conv_w=p[R_CONV_W:R_CONV_W + 3, :conv_cols], conv_norm_g=p[R_CONV_G:R_CONV_G + 1, :D_CONV],
        gate_bias=p[R_GATE_B:R_GATE_B + 1, :D_GLA_K], gla_norm_g=p[R_GLA_G:R_GLA_G + 1, :D_GLA_V],
        ln1_g=p[R_LN1_G:R_LN1_G + 2].reshape(1, -1)[:, :D], ln1_b=p[R_LN1_B:R_LN1_B + 2].reshape(1, -1)[:, :D],
        ln2_g=p[R_LN2_G:R_LN2_G + 2].reshape(1, -1)[:, :D], ln2_b=p[R_LN2_B:R_LN2_B + 2].reshape(1, -1)[:, :D],
        w_gate_up=p[R_GATE_W:R_GATE_W + GATE_RANK, :gate_cols])


BIG = ("w_in", "w_out", "w_ff_up", "w_ff_down")
ORDER = ("w_in", "conv_w", "conv_norm_g", "w_gate_up", "gate_bias", "gla_norm_g", "w_out", "ln1_g", "ln1_b",
         "w_ff_up", "w_ff_down", "ln2_g", "ln2_b")


def kernel(x, w_in, conv_w, conv_norm_g, w_gate_up, gate_bias, gla_norm_g, w_out, ln1_g, ln1_b, w_ff_up, w_ff_down, ln2_g, ln2_b, loss_target, m_w_in, m_conv_w, m_conv_norm_g, m_w_gate_up, m_gate_bias, m_gla_norm_g, m_w_out, m_ln1_g, m_ln1_b, m_w_ff_up, m_w_ff_down, m_ln2_g, m_ln2_b, v_w_in, v_conv_w, v_conv_norm_g, v_w_gate_up, v_gate_bias, v_gla_norm_g, v_w_out, v_ln1_g, v_ln1_b, v_w_ff_up, v_w_ff_down, v_ln2_g, v_ln2_b):
    T, D = x.shape[1], x.shape[2]
    xs, target = x[0], loss_target[0]
    xi, yi, ci = lax.axis_index("x"), lax.axis_index("y"), lax.axis_index("c")
    chip = 2 * xi + yi
    dev = 2 * chip + ci
    others = [jnp.where(chip <= q, q + 1, q) for q in range(N_CHIP - 1)]
    ids = jnp.stack([dev, chip, ci] + others).astype(jnp.int32)
    conv_cols, gate_cols = conv_w.shape[2], w_gate_up.shape[2]

    z1 = jnp.zeros((1, 1), F32)
    fwd_pack = _pack(conv_w[0], z1, z1, z1, z1, z1, z1, z1, None, w_gate_up[0])
    g_in, g_pack = _all_gather("gather_w_in", [w_in[0].astype(BF16), fwd_pack])
    conv_w_full = g_pack[:, R_CONV_W:R_CONV_W + 3, :conv_cols].transpose(1, 0, 2).reshape(3, -1)
    gate_w_full = g_pack[:, R_GATE_W:R_GATE_W + GATE_RANK, :gate_cols].transpose(1, 0, 2).reshape(GATE_RANK, -1)
    conv_w8 = jnp.pad(conv_w_full, ((0, 5), (0, 0)))
    wg128 = jnp.pad(gate_w_full, ((0, LANE - GATE_RANK), (0, 0))).astype(BF16)

    ga = [_xfer_start("gather_chips_" + nm, [_cast_place("cast_place_" + nm, ids, w[0], g_pack)], _plan_gather_chips, 4)
          for nm, w in zip(BIG[1:], (w_out, w_ff_up, w_ff_down))]

    def pass_on(nm, started, after):
        (land,) = _xfer_wait("gather_chips_wait_" + nm, started, after)
        return _xfer_start("gather_pass_" + nm, [land], _plan_gather_pass, 3)

    def landed(nm, started, after):
        return _xfer_wait("gather_pass_wait_" + nm, started, after)[0]

    w_full = _w_in_full(g_in)
    proj = _proj_fwd(xs, w_full, deps=[g["token"] for g in ga])
    yin = _conv_fwd(proj, conv_w8, conv_norm_g)
    gp_out = pass_on("w_out", ga[0], yin)
    o_all, states, yin = _gla_fwd(proj, wg128, gate_bias, gla_norm_g, yin, deps=[gp_out["token"]])
    w_out_full = landed("w_out", gp_out, o_all).reshape(-1, D)
    gp_up = pass_on("w_ff_up", ga[1], o_all)
    xhat1, x1, rstd1 = _mix_ln1(yin, w_out_full, xs, ln1_g, ln1_b, deps=[gp_up["token"]])
    w_up_blk = landed("w_ff_up", gp_up, x1)
    gp_down = pass_on("w_ff_down", ga[2], x1)
    ra, h2 = _ff_up(x1, w_up_blk, deps=[gp_down["token"]])
    w_down_full = landed("w_ff_down", gp_down, ra).reshape(-1, D)
    dh3, dh3b, g_ln2_g, g_ln2_b, loss = _ff_down_loss(h2, w_down_full, xhat1, target, ln1_g, ln1_b, ln2_g, ln2_b)

    def to_core(nm, grad):
        recv = lax.empty((N_CHIP,) + grad.shape[1:], F32)
        return _xfer_start("reduce_core_" + nm, [grad, recv], _plan_reduce_core, N_CHIP)

    def to_chips(nm, started, after):
        grad, recv = _xfer_wait("reduce_core_wait_" + nm, started, after)
        part = _chip_sums("chip_sums_" + nm, ids, grad, recv)
        land = lax.empty(part.shape, BF16)
        return grad, recv, _xfer_start("reduce_chips_" + nm, [part, land], _plan_reduce_chips, N_CHIP - 1)

    da = _ff_down_bwd_act(dh3b, w_down_full, ra)
    gw_down = _grad_w("grad_w_down", h2, dh3b).reshape(N_DEV, -1, D)
    rc_down = to_core("w_ff_down", gw_down)
    gw_up = _grad_w_up_blk(x1, da, N_DEV, deps=[rc_down["token"]])
    gw_down, rv_down, rs_down = to_chips("w_ff_down", rc_down, gw_up)
    rc_up = to_core("w_ff_up", gw_up)
    dh1, dh1b, g_ln1_g, g_ln1_b = _ff_up_bwd_ln1(da, w_up_blk, dh3, xhat1, rstd1, ln1_g,
                                                 deps=[rs_down["token"], rc_up["token"]])
    gw_up, rv_up, rs_up = to_chips("w_ff_up", rc_up, dh1b)
    dyin = _mix_bwd(dh1b, w_out_full, deps=[rs_up["token"]])
    gw_out = _grad_w("grad_w_out", yin, dh1b).reshape(N_DEV, -1, D)
    rc_out = to_core("w_out", gw_out)
    dproj, g_conv_w, g_conv_g = _conv_bwd(proj, dyin, conv_w8, conv_norm_g, deps=[rc_out["token"]])
    dproj, g_gate_w, g_gate_b, g_gla_g = _gla_bwd(proj, wg128, gate_bias, gla_norm_g, o_all, states, dyin, dproj)
    gw_out, rv_out, rs_out = to_chips("w_out", rc_out, dproj)
    gw_in = _w_in_blocks(_grad_w("grad_w_in", xs, dproj, a_fn=_to_bf16, tn_pref=1280, tk_pref=2048, deps=[rs_out["token"]]))
    rc_in = to_core("w_in", gw_in)

    big = {}

    def finish(nm, grad, recv, started, w, m, v, after):
        _, land = _xfer_wait("reduce_chips_wait_" + nm, started, after)
        res = _reduce_adamw("adamw_" + nm, ids, grad, recv, land, w[0], m[0], v[0])
        big[nm] = [a[None] for a in res]
        return res[0]

    done = finish("w_ff_down", gw_down, rv_down, rs_down, w_ff_down, m_w_ff_down, v_w_ff_down, rc_in["token"])
    done = finish("w_ff_up", gw_up, rv_up, rs_up, w_ff_up, m_w_ff_up, v_w_ff_up, done)
    gw_in, rv_in, rs_in = to_chips("w_in", rc_in, done)
    grad_x = _proj_bwd_x(dproj, w_full, dh1, deps=[rs_in["token"]])

    pack = _pack(g_conv_w[:3], g_conv_g, g_gate_b, g_gla_g, g_ln1_g, g_ln1_b, g_ln2_g, g_ln2_b, loss,
                 g_gate_w[:GATE_RANK])
    (packs,) = _all_gather("gather_small_grads", [pack], deps=[grad_x])
    done = finish("w_out", gw_out, rv_out, rs_out, w_out, m_w_out, v_w_out, packs)
    finish("w_in", gw_in, rv_in, rs_in, w_in, m_w_in, v_w_in, done)

    def own_cols(row, n_rows, width):
        cut = lax.dynamic_slice(packs, (0, row, dev * width), (N_DEV, n_rows, width))
        return jnp.pad(cut, ((0, 0), (0, 0), (0, SP_COLS - width)))

    packs_own = jnp.concatenate([own_cols(R_CONV_W, 3, conv_cols), packs[:, R_CONV_W + 3:R_GATE_W],
                                 own_cols(R_GATE_W, GATE_RANK, gate_cols)], axis=1)

    def small_pack(cw, cg, gw, gb, gg, l1g, l1b, l2g, l2b):
        return _pack(cw[0], cg, gb, gg, l1g, l1b, l2g, l2b, None, gw[0])

    w_s = small_pack(conv_w, conv_norm_g, w_gate_up, gate_bias, gla_norm_g, ln1_g, ln1_b, ln2_g, ln2_b)
    m_s = small_pack(m_conv_w, m_conv_norm_g, m_w_gate_up, m_gate_bias, m_gla_norm_g, m_ln1_g, m_ln1_b, m_ln2_g, m_ln2_b)
    v_s = small_pack(v_conv_w, v_conv_norm_g, v_w_gate_up, v_gate_bias, v_gla_norm_g, v_ln1_g, v_ln1_b, v_ln2_g, v_ln2_b)
    g_s, d_s, mn_s, vn_s = _small_adamw(packs_own, w_s, m_s, v_s)
    small = [_unpack(p, D, conv_cols, gate_cols) for p in (g_s, d_s, mn_s, vn_s)]

    def leaf(kind, name):
        if name in BIG:
            return big[name][kind]
        a = small[kind][name]
        return a[None] if name in ("conv_w", "w_gate_up") else a

    out = [g_s[R_LOSS, 0], grad_x[None]]
    for kind in range(4):
        out += [leaf(kind, nm) for nm in ORDER]
    return tuple(out)
```

```python
import jax
import jax.numpy as jnp
from jax import lax
from jax.experimental import pallas as pl
from jax.experimental.pallas import tpu as pltpu

F32 = jnp.float32
BF16 = jnp.bfloat16

D_CONV = 1024
CONV_GROUPS = 8
GLA_HEADS = 4
HEAD_K = 128
HEAD_V = 256
D_GLA_K = 512
D_GLA_V = 1024
GATE_RANK = 16
GATE_TAU = 16.0
CHUNK = 64
LN_EPS = 1e-5
RMS_EPS = 1e-6
DN_ALPHA = 2.0 ** 0.25
D_IN_PROJ = 6160
ADAM_LR = 0.001
ADAM_B1 = 0.9
ADAM_B2 = 0.999
ADAM_EPS = 1e-08
ADAM_WD = 0.01
ADAM_STEP = 10

N_DEV = 8
N_CHIP = 4
LANE = 128
HALF_P = 3200
P_INT = 2 * HALF_P
CONV_COLS = 3 * D_CONV
GLA_COLS = D_IN_PROJ - CONV_COLS
SP_ROWS = 32
SP_COLS = 1024
VMEM_LIMIT = 56 * 1024 * 1024

NN = ((1,), (0,))
NT = ((1,), (1,))
TN = ((0,), (0,))
MESH = pl.DeviceIdType.MESH


def _dot(a, b, dims, precision=None):
    return lax.dot_general(a, b, (dims, ((), ())), preferred_element_type=F32, precision=precision)


def _tile(n, pref):
    if n <= pref:
        return n
    t = (pref // LANE) * LANE
    while t > 0 and n % t:
        t -= LANE
    assert t > 0, (n, pref)
    return t


def _params(n_axes):
    return pltpu.CompilerParams(dimension_semantics=("arbitrary",) * n_axes, vmem_limit_bytes=VMEM_LIMIT)


def _full(shape):
    nd = len(shape)
    return pl.BlockSpec(shape, lambda *_: (0,) * nd)


def _hbm_specs(n):
    return [pl.BlockSpec(memory_space=pl.ANY)] * n


def _mm(name, mode, a, b, *, M, N, K, tm, tn, tk, outs, epilogue, extras=(), a_fn=None, a_spec=None, b_spec=None,
        deps=()):
    ni, nj, nk = M // tm, N // tn, K // tk
    assert ni * tm == M and nj * tn == N and nk * tk == K, (name, M, N, K, tm, tn, tk)
    if a_spec is None:
        a_spec = (pl.BlockSpec((tk, tm), lambda i, j, k: (k, i)) if mode == "tn"
                  else pl.BlockSpec((tm, tk), lambda i, j, k: (i, k)))
    if b_spec is None:
        b_spec = (pl.BlockSpec((tn, tk), lambda i, j, k: (j, k)) if mode == "nt"
                  else pl.BlockSpec((tk, tn), lambda i, j, k: (k, j)))
    dims = {"nn": NN, "nt": NT, "tn": TN}[mode]
    n_ex, n_out, n_dep = len(extras), len(outs), len(deps)

    def body(*refs):
        a_ref, b_ref = refs[0], refs[1]
        ex = refs[2:2 + n_ex]
        o = refs[2 + n_ex + n_dep:2 + n_ex + n_dep + n_out]
        acc_ref = refs[2 + n_ex + n_dep + n_out]
        i, j, k = pl.program_id(0), pl.program_id(1), pl.program_id(2)
        av = a_ref[...]
        if a_fn is not None:
            av = a_fn(av)
        part = _dot(av, b_ref[...], dims)
        if nk == 1 and epilogue is None:
            o[0][...] = part.astype(o[0].dtype)
        elif nk == 1:
            acc_ref[...] = part
            epilogue(acc_ref, ex, o, i, j)
        else:
            @pl.when(k == 0)
            def _():
                acc_ref[...] = part

            @pl.when(k > 0)
            def _():
                acc_ref[...] += part

            @pl.when(k == nk - 1)
            def _():
                if epilogue is None:
                    o[0][...] = acc_ref[...].astype(o[0].dtype)
                else:
                    epilogue(acc_ref, ex, o, i, j)

    return pl.pallas_call(
        body,
        name=name,
        grid=(ni, nj, nk),
        in_specs=[a_spec, b_spec] + [s for _, s in extras] + _hbm_specs(n_dep),
        out_specs=[s for _, s in outs],
        out_shape=[s for s, _ in outs],
        scratch_shapes=[pltpu.VMEM((8, LANE) if nk == 1 and epilogue is None else (tm, tn), F32)],
        compiler_params=_params(3),
    )(a, b, *[x for x, _ in extras], *deps)


def _mm_rows(name, mode, a, b, *, M, N, K, tm, tk, row_ins, vec_ins, row_outs, stat_outs, chunk_fn,
             b_spec=None, deps=()):
    ni, nk = M // tm, K // tk
    rc = tm // nk
    assert ni * tm == M and nk * tk == K and rc * nk == tm and rc % 16 == 0, (name, M, K, tm, tk)
    dims = {"nn": NN, "nt": NT}[mode]
    last = ni - 1

    def kk(i, k):
        return jnp.where(i < ni, k, nk - 1)

    a_spec = pl.BlockSpec((tm, tk), lambda i, k: (jnp.minimum(i, last), kk(i, k)))
    if b_spec is None:
        b_spec = (pl.BlockSpec((N, tk), lambda i, k: (0, kk(i, k))) if mode == "nt"
                  else pl.BlockSpec((tk, N), lambda i, k: (kk(i, k), 0)))
    prev_rows = lambda i, k: (jnp.maximum((i - 1) * nk + k, 0), 0)
    n_ri, n_vi, n_ro, n_so, n_dep = len(row_ins), len(vec_ins), len(row_outs), len(stat_outs), len(deps)

    def body(*refs):
        a_ref, b_ref = refs[0], refs[1]
        pos = 2
        ri = refs[pos:pos + n_ri]; pos += n_ri
        vi = refs[pos:pos + n_vi]; pos += n_vi + n_dep
        ro = refs[pos:pos + n_ro]; pos += n_ro
        so = refs[pos:pos + n_so]; pos += n_so
        accs = refs[pos:pos + 2]
        i, k = pl.program_id(0), pl.program_id(1)

        @pl.when((i == 0) & (k == 0))
        def _():
            accs[0][...] = jnp.zeros_like(accs[0])
            accs[1][...] = jnp.zeros_like(accs[1])
            for st in so:
                st[...] = jnp.zeros_like(st)

        def finish_rows(prev_ref):
            rows = pl.ds(pl.multiple_of(k * rc, rc), rc)
            done = prev_ref[rows, :]
            prev_ref[rows, :] = jnp.zeros((rc, N), F32)
            chunk_fn(done, i > 0, ri, vi, ro, so)

        for parity in (0, 1):
            @pl.when((i < ni) & (lax.rem(i, 2) == parity))
            def _(parity=parity):
                part = _dot(a_ref[...], b_ref[...], dims)
                finish_rows(accs[1 - parity])
                accs[parity][...] += part

        @pl.when(i == ni)
        def _():
            finish_rows(accs[last % 2])

    row_spec = lambda arr: pl.BlockSpec((rc, arr.shape[1]), prev_rows)
    return pl.pallas_call(
        body,
        name=name,
        grid=(ni + 1, nk),
        in_specs=[a_spec, b_spec] + [row_spec(x) for x in row_ins] + [_full(x.shape) for x in vec_ins]
        + _hbm_specs(n_dep),
        out_specs=[row_spec(s) for s in row_outs] + [_full(s.shape) for s in stat_outs],
        out_shape=list(row_outs) + list(stat_outs),
        scratch_shapes=[pltpu.VMEM((tm, N), F32), pltpu.VMEM((tm, N), F32)],
        compiler_params=_params(2),
    )(a, b, *row_ins, *vec_ins, *deps)


SUB_ROWS = 16


def _by_sub_rows(n_rows, fn):
    sums = None
    for r0 in range(0, n_rows, SUB_ROWS):
        part = fn(slice(r0, r0 + SUB_ROWS))
        if part:
            sums = part if sums is None else tuple(x + y for x, y in zip(sums, part))
    return sums


def _to_bf16(v):
    return v.astype(BF16)


def _ln_bwd(dy, xhat, rstd, g):
    dxh = dy * g
    m1 = jnp.mean(dxh, axis=-1, keepdims=True)
    m2 = jnp.mean(dxh * xhat, axis=-1, keepdims=True)
    return rstd * (dxh - m1 - xhat * m2)


def _ln_fwd(h):
    mu = jnp.mean(h, axis=-1, keepdims=True)
    xc = h - mu
    var = jnp.mean(xc * xc, axis=-1, keepdims=True)
    rstd = lax.rsqrt(var + LN_EPS)
    return xc * rstd, rstd


def _proj_fwd(x, w_full, deps=()):
    T, D = x.shape
    P = w_full.shape[1]
    tm, tn = _tile(T, 1024), _tile(P, 1280)
    return _mm("proj_fwd", "nn", x, w_full, M=T, N=P, K=D, tm=tm, tn=tn, tk=D,
               outs=[(jax.ShapeDtypeStruct((T, P), F32), pl.BlockSpec((tm, tn), lambda i, j, k: (i, j)))],
               epilogue=None, a_fn=_to_bf16, deps=deps)[0]


def _conv_shift(h, hp):
    row = lax.broadcasted_iota(jnp.int32, h.shape, 0)
    hm1 = hp[7:8, :]
    hm2 = hp[6:7, :]
    h1 = jnp.where(row == 0, hm1, pltpu.roll(h, 1, 0))
    h2 = jnp.where(row == 0, hm2, jnp.where(row == 1, hm1, pltpu.roll(h, 2, 0)))
    return h1, h2


def _conv_fwd(proj, conv_w8, conv_g):
    T = proj.shape[0]
    tt = _tile(T, 256)
    nt = T // tt
    t8 = tt // 8

    def body(b_ref, c_ref, u_ref, cp_ref, up_ref, w_ref, g_ref, yin_ref):
        i = pl.program_id(0)
        h = c_ref[...] * u_ref[...]
        hp = jnp.where(i > 0, cp_ref[...] * up_ref[...], 0.0)
        h1, h2 = _conv_shift(h, hp)
        w = w_ref[...]
        y = w[0:1, :] * h2 + w[1:2, :] * h1 + w[2:3, :] * h
        p = b_ref[...] * y
        parts = []
        for gi in range(CONV_GROUPS):
            pg = p[:, gi * LANE:(gi + 1) * LANE]
            r = lax.rsqrt(jnp.mean(pg * pg, axis=-1, keepdims=True) + RMS_EPS)
            parts.append(pg * r)
        yn = jnp.concatenate(parts, axis=1) * g_ref[...]
        yin_ref[...] = yn.astype(BF16)

    def col(cidx):
        return pl.BlockSpec((tt, D_CONV), lambda i: (i, cidx))

    def prev(cidx):
        return pl.BlockSpec((8, D_CONV), lambda i: (jnp.maximum(i * t8 - 1, 0), cidx))

    return pl.pallas_call(
        body,
        name="conv_fwd",
        grid=(nt,),
        in_specs=[col(0), col(1), col(2), prev(1), prev(2), _full((8, D_CONV)), _full((1, D_CONV))],
        out_specs=pl.BlockSpec((tt, D_CONV), lambda i: (i, 0)),
        out_shape=jax.ShapeDtypeStruct((T, 2 * D_CONV), BF16),
        compiler_params=_params(1),
    )(proj, proj, proj, proj, proj, conv_w8, conv_g)


def _log_sigmoid(z):
    return jnp.minimum(z, 0.0) - jnp.log(1.0 + jnp.exp(-jnp.abs(z)))


def _gla_chunk_terms(blk, wg_ref, gb_ref):
    q = blk[:, 0:512]
    k = blk[:, 512:1024]
    zl = blk[:, 3072:3200]
    z = _dot(zl.astype(BF16), wg_ref[...], NN) + gb_ref[...]
    log_a = _log_sigmoid(z) * (1.0 / GATE_TAU)
    ri = lax.broadcasted_iota(jnp.int32, (CHUNK, CHUNK), 0)
    ci = lax.broadcasted_iota(jnp.int32, (CHUNK, CHUNK), 1)
    causal = ri >= ci
    lower = causal.astype(F32)
    bcum = _dot(lower, log_a, NN, precision=lax.Precision.HIGHEST)
    return q, k, zl, z, bcum, causal


def _gla_head_terms(q, k, bcum, h):
    sl = slice(h * HEAD_K, (h + 1) * HEAD_K)
    bh = bcum[:, sl]
    bl = bh[CHUNK - 1:CHUNK, :]
    eb = jnp.exp(bh)
    enb = jnp.exp(-bh)
    eend = jnp.exp(bl - bh)
    dec = jnp.exp(bl)
    qd = q[:, sl] * (HEAD_K ** -0.5) * eb
    ki = k[:, sl] * enb
    ke = k[:, sl] * eend
    return eb, enb, eend, dec, qd, ki, ke


def _sigmoid(x):
    return 1.0 / (1.0 + jnp.exp(-x))


def _gla_fwd(proj, wg128, gbias, gng, yin, deps=()):
    T = proj.shape[0]
    nch = T // CHUNK

    def body(p_ref, wg_ref, gb_ref, gn_ref, yin_in_ref, *rest):
        o_ref, st_ref, yin_ref, s_ref = rest[len(deps):]
        n = pl.program_id(0)

        @pl.when(n == 0)
        def _():
            s_ref[...] = jnp.zeros_like(s_ref)

        blk = p_ref[...]
        q, k, _, _, bcum, causal = _gla_chunk_terms(blk, wg_ref, gb_ref)
        v = blk[:, 1024:2048]
        r = blk[:, 2048:3072]
        gn = gn_ref[...]
        for h in range(GLA_HEADS):
            _, _, _, dec, qd, ki, ke = _gla_head_terms(q, k, bcum, h)
            vs = slice(h * HEAD_V, (h + 1) * HEAD_V)
            vb = v[:, vs].astype(BF16)
            qdb = qd.astype(BF16)
            a = jnp.where(causal, _dot(qdb, ki.astype(BF16), NT), 0.0)
            st = s_ref[h]
            o = _dot(a.astype(BF16), vb, NN) + _dot(qdb, st.astype(BF16), NT)
            st_ref[h] = st
            s_ref[h] = dec * st + _dot(vb, ke.astype(BF16), TN)
            o_ref[:, vs] = o
            rinv = lax.rsqrt(jnp.mean(o * o, axis=-1, keepdims=True) + RMS_EPS)
            rh = r[:, vs]
            yin_ref[:, vs] = (o * rinv * gn[:, vs] * (rh * _sigmoid(rh))).astype(BF16)

    return pl.pallas_call(
        body,
        name="gla_fwd",
        grid=(nch,),
        in_specs=[pl.BlockSpec((CHUNK, HALF_P), lambda n: (n, 1)), _full((LANE, D_GLA_K)), _full((1, D_GLA_K)),
                  _full((1, D_GLA_V)), pl.BlockSpec(memory_space=pl.ANY)] + _hbm_specs(len(deps)),
        out_specs=[pl.BlockSpec((CHUNK, D_GLA_V), lambda n: (n, 0)),
                   pl.BlockSpec((None, GLA_HEADS, HEAD_V, HEAD_K), lambda n: (n, 0, 0, 0)),
                   pl.BlockSpec((CHUNK, D_GLA_V), lambda n: (n, 1))],
        out_shape=[jax.ShapeDtypeStruct((T, D_GLA_V), F32),
                   jax.ShapeDtypeStruct((nch, GLA_HEADS, HEAD_V, HEAD_K), F32),
                   jax.ShapeDtypeStruct(yin.shape, BF16)],
        scratch_shapes=[pltpu.VMEM((GLA_HEADS, HEAD_V, HEAD_K), F32)],
        input_output_aliases={4: 2},
        compiler_params=_params(1),
    )(proj, wg128, gbias, gng, yin, *deps)


def _mix_ln1(yin, w_out, x, ln_g, ln_b, deps=()):
    T, D = x.shape
    KY = yin.shape[1]
    tm = _tile(T, 1024)

    def chunk(acc, valid, ri, vi, ro, so):
        g, b = vi[0][...], vi[1][...]

        def sub(rows):
            xhat, rstd = _ln_fwd(DN_ALPHA * ri[0][rows, :] + acc[rows, :])
            ro[0][rows, :] = xhat
            ro[1][rows, :] = (xhat * g + b).astype(BF16)
            ro[2][rows, :] = rstd

        _by_sub_rows(acc.shape[0], sub)

    return _mm_rows("mix_ln1", "nn", yin, w_out, M=T, N=D, K=KY, tm=tm, tk=_tile(KY, 512),
                    row_ins=[x], vec_ins=[ln_g, ln_b],
                    row_outs=[jax.ShapeDtypeStruct((T, D), F32), jax.ShapeDtypeStruct((T, D), BF16),
                              jax.ShapeDtypeStruct((T, 1), F32)],
                    stat_outs=[], chunk_fn=chunk, deps=deps)


def _ff_up(x1, w_up_blk, deps=()):
    T, D = x1.shape
    nb, _, fb = w_up_blk.shape
    tm = _tile(T, 1024)

    def ep(acc_ref, ex, o, i, j):
        ra = jnp.maximum(acc_ref[...], 0.0)
        o[0][...] = ra.astype(BF16)
        o[1][...] = (ra * ra).astype(BF16)

    blk = pl.BlockSpec((tm, fb), lambda i, j, k: (i, j))
    shp = jax.ShapeDtypeStruct((T, nb * fb), BF16)
    return _mm("ff_up", "nn", x1, w_up_blk, M=T, N=nb * fb, K=D, tm=tm, tn=fb, tk=D,
               b_spec=pl.BlockSpec((None, D, fb), lambda i, j, k: (j, 0, 0)),
               outs=[(shp, blk), (shp, blk)], epilogue=ep, deps=deps)


def _ff_down_loss(h2, w_down, xhat1, target, g1, b1, g2, b2):
    T, F = h2.shape
    D = w_down.shape[1]
    tm = _tile(T, 1024)
    inv_d = 1.0 / D

    def chunk(acc, valid, ri, vi, ro, so):
        g1v, b1v, g2v, b2v = (v[...] for v in vi)

        def sub(rows):
            x1 = ri[0][rows, :] * g1v + b1v
            xhat, rstd = _ln_fwd(DN_ALPHA * x1 + acc[rows, :])
            e = xhat * g2v + b2v - ri[1][rows, :]
            dy = e * inv_d
            dh = _ln_bwd(dy, xhat, rstd, g2v)
            ro[0][rows, :] = dh
            ro[1][rows, :] = dh.astype(BF16)
            return (jnp.sum(dy * xhat, axis=0, keepdims=True), jnp.sum(dy, axis=0, keepdims=True),
                    jnp.sum(e * e, axis=0, keepdims=True))

        sg, sb, sl = _by_sub_rows(acc.shape[0], sub)
        so[0][...] += jnp.where(valid, sg, 0.0)
        so[1][...] += jnp.where(valid, sb, 0.0)
        so[2][...] += jnp.where(valid, sl * (0.5 * inv_d), 0.0)

    vshape = jax.ShapeDtypeStruct((1, D), F32)
    return _mm_rows("ff_down_loss", "nn", h2, w_down, M=T, N=D, K=F, tm=tm, tk=_tile(F, 1024),
                    row_ins=[xhat1, target], vec_ins=[g1, b1, g2, b2],
                    row_outs=[jax.ShapeDtypeStruct((T, D), F32), jax.ShapeDtypeStruct((T, D), BF16)],
                    stat_outs=[vshape, vshape, vshape], chunk_fn=chunk)


def _ff_down_bwd_act(dh3b, w_down, ra):
    T, D = dh3b.shape
    F = w_down.shape[0]
    tm, tn = _tile(T, 1024), _tile(F, 1024)

    def ep(acc_ref, ex, o, i, j):
        o[0][...] = (acc_ref[...] * (2.0 * ex[0][...].astype(F32))).astype(BF16)

    blk = pl.BlockSpec((tm, tn), lambda i, j, k: (i, j))
    return _mm("ff_down_bwd_act", "nt", dh3b, w_down, M=T, N=F, K=D, tm=tm, tn=tn, tk=D,
               outs=[(jax.ShapeDtypeStruct((T, F), BF16), blk)], extras=[(ra, blk)], epilogue=ep)[0]


def _grad_w(name, a, b, *, a_fn=None, tm_pref=1024, tn_pref=1024, tk_pref=4096, deps=()):
    T, M = a.shape
    N = b.shape[1]
    tm, tn, tk = _tile(M, tm_pref), _tile(N, tn_pref), _tile(T, tk_pref)
    return _mm(name, "tn", a, b, M=M, N=N, K=T, tm=tm, tn=tn, tk=tk, a_fn=a_fn, deps=deps,
               outs=[(jax.ShapeDtypeStruct((M, N), F32), pl.BlockSpec((tm, tn), lambda i, j, k: (i, j)))],
               epilogue=None)[0]


def _grad_w_up_blk(x1, da, nb, deps=()):
    T, D = x1.shape
    F = da.shape[1]
    fb = F // nb
    tm, tk = _tile(D, 1024), _tile(T, 4096)
    return _mm("grad_w_up", "tn", x1, da, M=D, N=F, K=T, tm=tm, tn=fb, tk=tk, deps=deps,
               outs=[(jax.ShapeDtypeStruct((nb, D, fb), F32),
                      pl.BlockSpec((None, tm, fb), lambda i, j, k: (j, i, 0)))],
               epilogue=None)[0]


def _ff_up_bwd_ln1(da, w_up_blk, dh3, xhat1, rstd1, g1, deps=()):
    T, F = da.shape
    nb, D, fb = w_up_blk.shape
    tm = _tile(T, 1024)

    def chunk(acc, valid, ri, vi, ro, so):
        g = vi[0][...]

        def sub(rows):
            dx1 = DN_ALPHA * ri[0][rows, :] + acc[rows, :]
            xhat = ri[1][rows, :]
            dh = _ln_bwd(dx1, xhat, ri[2][rows, :], g)
            ro[0][rows, :] = dh
            ro[1][rows, :] = dh.astype(BF16)
            return jnp.sum(dx1 * xhat, axis=0, keepdims=True), jnp.sum(dx1, axis=0, keepdims=True)

        sg, sb = _by_sub_rows(acc.shape[0], sub)
        so[0][...] += jnp.where(valid, sg, 0.0)
        so[1][...] += jnp.where(valid, sb, 0.0)

    nk = F // fb
    vshape = jax.ShapeDtypeStruct((1, D), F32)
    return _mm_rows("ff_up_bwd_ln1", "nt", da, w_up_blk, M=T, N=D, K=F, tm=tm, tk=fb,
                    b_spec=pl.BlockSpec((None, D, fb), lambda i, k: (jnp.where(i < T // tm, k, nk - 1), 0, 0)),
                    row_ins=[dh3, xhat1, rstd1], vec_ins=[g1],
                    row_outs=[jax.ShapeDtypeStruct((T, D), F32), jax.ShapeDtypeStruct((T, D), BF16)],
                    stat_outs=[vshape, vshape], chunk_fn=chunk, deps=deps)


def _mix_bwd(dh1b, w_out, deps=()):
    T, D = dh1b.shape
    KY = w_out.shape[0]
    tm, tn = _tile(T, 1024), _tile(KY, 1024)
    return _mm("mix_bwd", "nt", dh1b, w_out, M=T, N=KY, K=D, tm=tm, tn=tn, tk=D, deps=deps,
               outs=[(jax.ShapeDtypeStruct((T, KY), F32), pl.BlockSpec((tm, tn), lambda i, j, k: (i, j)))],
               epilogue=None)[0]


def _conv_bwd(proj, dyin, conv_w8, conv_g, deps=()):
    T = proj.shape[0]
    tt = _tile(T, 256)
    nt = T // tt
    t8 = tt // 8
    nx = tt + 8

    def body(b_ref, c_ref, u_ref, d_ref, bn_ref, cn_ref, un_ref, dn_ref, cp_ref, up_ref, w_ref, g_ref, *rest):
        dp_ref, dw_ref, dg_ref = rest[len(deps):]
        i = pl.program_id(0)

        @pl.when(i == 0)
        def _():
            dw_ref[...] = jnp.zeros_like(dw_ref)
            dg_ref[...] = jnp.zeros_like(dg_ref)

        more = i < nt - 1

        def ext(cur_ref, nxt_ref):
            return jnp.concatenate([cur_ref[...], jnp.where(more, nxt_ref[...], 0.0)], axis=0)

        bx, cx, ux, dx = ext(b_ref, bn_ref), ext(c_ref, cn_ref), ext(u_ref, un_ref), ext(d_ref, dn_ref)
        hx = cx * ux
        hp = jnp.where(i > 0, cp_ref[...] * up_ref[...], 0.0)
        h1, h2 = _conv_shift(hx, hp)
        w = w_ref[...]
        g = g_ref[...]
        yx = w[0:1, :] * h2 + w[1:2, :] * h1 + w[2:3, :] * hx
        px = bx * yx
        dps, dgs = [], []
        for gi in range(CONV_GROUPS):
            sl = slice(gi * LANE, (gi + 1) * LANE)
            pg, dg_ = px[:, sl], dx[:, sl]
            r = lax.rsqrt(jnp.mean(pg * pg, axis=-1, keepdims=True) + RMS_EPS)
            gd = g[:, sl] * dg_
            dps.append(r * gd - pg * (r * r * r) * jnp.mean(pg * gd, axis=-1, keepdims=True))
            dgs.append(jnp.sum((dg_ * pg * r)[:tt, :], axis=0, keepdims=True))
        dpx = jnp.concatenate(dps, axis=1)
        dg_ref[...] += jnp.concatenate(dgs, axis=1)
        dyx = dpx * bx
        dyc = dyx[:tt, :]
        dh = (w[2:3, :] * dyx + w[1:2, :] * pltpu.roll(dyx, nx - 1, 0) + w[0:1, :] * pltpu.roll(dyx, nx - 2, 0))[:tt, :]
        dw_ref[0:1, :] += jnp.sum(dyc * h2[:tt, :], axis=0, keepdims=True)
        dw_ref[1:2, :] += jnp.sum(dyc * h1[:tt, :], axis=0, keepdims=True)
        dw_ref[2:3, :] += jnp.sum(dyc * hx[:tt, :], axis=0, keepdims=True)
        dp_ref[:, 0:D_CONV] = (dpx * yx)[:tt, :].astype(BF16)
        dp_ref[:, D_CONV:2 * D_CONV] = (dh * u_ref[...]).astype(BF16)
        dp_ref[:, 2 * D_CONV:3 * D_CONV] = (dh * c_ref[...]).astype(BF16)
        dp_ref[:, 3 * D_CONV:HALF_P] = jnp.zeros((tt, HALF_P - 3 * D_CONV), BF16)

    def col(cidx):
        return pl.BlockSpec((tt, D_CONV), lambda i: (i, cidx))

    def nxt(cidx):
        return pl.BlockSpec((8, D_CONV), lambda i: (jnp.minimum((i + 1) * t8, T // 8 - 1), cidx))

    def prev(cidx):
        return pl.BlockSpec((8, D_CONV), lambda i: (jnp.maximum(i * t8 - 1, 0), cidx))

    return pl.pallas_call(
        body,
        name="conv_bwd",
        grid=(nt,),
        in_specs=[col(0), col(1), col(2), col(0), nxt(0), nxt(1), nxt(2), nxt(0), prev(1), prev(2),
                  _full((8, D_CONV)), _full((1, D_CONV))] + _hbm_specs(len(deps)),
        out_specs=[pl.BlockSpec((tt, HALF_P), lambda i: (i, 0)), _full((8, D_CONV)), _full((1, D_CONV))],
        out_shape=[jax.ShapeDtypeStruct((T, P_INT), BF16), jax.ShapeDtypeStruct((8, D_CONV), F32),
                   jax.ShapeDtypeStruct((1, D_CONV), F32)],
        compiler_params=_params(1),
    )(proj, proj, proj, dyin, proj, proj, proj, dyin, proj, proj, conv_w8, conv_g, *deps)


def _gla_bwd(proj, wg128, gbias, gng, o_all, states, dyin, dproj):
    T = proj.shape[0]
    nch = T // CHUNK

    def body(p_ref, wg_ref, gb_ref, gn_ref, o_ref, st_ref, d_ref, dp_in_ref,
             dp_ref, dwg_ref, dgb_ref, dgn_ref, ds_ref):
        n = pl.program_id(0)

        @pl.when(n == 0)
        def _():
            ds_ref[...] = jnp.zeros_like(ds_ref)
            dwg_ref[...] = jnp.zeros_like(dwg_ref)
            dgb_ref[...] = jnp.zeros_like(dgb_ref)
            dgn_ref[...] = jnp.zeros_like(dgn_ref)

        blk = p_ref[...]
        q, k, zl, z, bcum, causal = _gla_chunk_terms(blk, wg_ref, gb_ref)
        v = blk[:, 1024:2048]
        r = blk[:, 2048:3072]
        gn = gn_ref[...]
        upper = (lax.broadcasted_iota(jnp.int32, (CHUNK, CHUNK), 0)
                 <= lax.broadcasted_iota(jnp.int32, (CHUNK, CHUNK), 1)).astype(F32)
        dlog_parts = []
        for h in range(GLA_HEADS):
            eb, enb, eend, dec, qd, ki, ke = _gla_head_terms(q, k, bcum, h)
            vs = slice(h * HEAD_V, (h + 1) * HEAD_V)
            ks = slice(h * HEAD_K, (h + 1) * HEAD_K)
            o = o_ref[:, vs]
            rh = r[:, vs]
            dyg = d_ref[:, vs]
            rinv = lax.rsqrt(jnp.mean(o * o, axis=-1, keepdims=True) + RMS_EPS)
            sg = _sigmoid(rh)
            on = o * rinv
            dr = dyg * (on * gn[:, vs]) * (sg * (1.0 + rh * (1.0 - sg)))
            don = dyg * (rh * sg)
            dgn_ref[:, vs] += jnp.sum(don * on, axis=0, keepdims=True)
            t = don * gn[:, vs]
            do = rinv * t - o * (rinv * rinv * rinv) * jnp.mean(o * t, axis=-1, keepdims=True)
            dob = do.astype(BF16)
            vb = v[:, vs].astype(BF16)
            qdb, kib, keb = qd.astype(BF16), ki.astype(BF16), ke.astype(BF16)
            a = jnp.where(causal, _dot(qdb, kib, NT), 0.0)
            st = st_ref[h]
            dst = ds_ref[h]
            dstb = dst.astype(BF16)
            da = jnp.where(causal, _dot(dob, vb, NT), 0.0)
            dab = da.astype(BF16)
            dv = _dot(a.astype(BF16), dob, TN) + _dot(keb, dstb, NT)
            dqd = _dot(dab, kib, NN) + _dot(dob, st.astype(BF16), NN)
            dki = _dot(dab, qdb, TN)
            dke = _dot(vb, dstb, NN)
            ddec = jnp.sum(st * dst, axis=0, keepdims=True)
            ds_ref[h] = dec * dst + _dot(dob, qdb, TN)
            dq = dqd * eb * (HEAD_K ** -0.5)
            dk = dki * enb + dke * eend
            db = dqd * qd - dki * ki - dke * ke
            dbl = jnp.sum(dke * ke, axis=0, keepdims=True) + dec * ddec
            dlog_parts.append(_dot(upper, db, NN, precision=lax.Precision.HIGHEST) + dbl)
            dp_ref[:, ks] = dq.astype(BF16)
            dp_ref[:, D_GLA_K + h * HEAD_K:D_GLA_K + (h + 1) * HEAD_K] = dk.astype(BF16)
            dp_ref[:, 1024 + h * HEAD_V:1024 + (h + 1) * HEAD_V] = dv.astype(BF16)
            dp_ref[:, 2048 + h * HEAD_V:2048 + (h + 1) * HEAD_V] = dr.astype(BF16)
        dlog = jnp.concatenate(dlog_parts, axis=1)
        dz = dlog * (1.0 / GATE_TAU) * (1.0 / (1.0 + jnp.exp(z)))
        dzb = dz.astype(BF16)
        dp_ref[:, 3072:3200] = _dot(dzb, wg_ref[...], NT).astype(BF16)
        dwg_ref[...] += _dot(zl.astype(BF16), dzb, TN)
        dgb_ref[...] += jnp.sum(dz, axis=0, keepdims=True)

    rev = lambda n: nch - 1 - n
    return pl.pallas_call(
        body,
        name="gla_bwd",
        grid=(nch,),
        in_specs=[pl.BlockSpec((CHUNK, HALF_P), lambda n: (rev(n), 1)), _full((LANE, D_GLA_K)), _full((1, D_GLA_K)),
                  _full((1, D_GLA_V)), pl.BlockSpec((CHUNK, D_GLA_V), lambda n: (rev(n), 0)),
                  pl.BlockSpec((None, GLA_HEADS, HEAD_V, HEAD_K), lambda n: (rev(n), 0, 0, 0)),
                  pl.BlockSpec((CHUNK, D_GLA_V), lambda n: (rev(n), 1)), pl.BlockSpec(memory_space=pl.ANY)],
        out_specs=[pl.BlockSpec((CHUNK, HALF_P), lambda n: (rev(n), 1)), _full((LANE, D_GLA_K)),
                   _full((1, D_GLA_K)), _full((1, D_GLA_V))],
        out_shape=[jax.ShapeDtypeStruct(dproj.shape, BF16), jax.ShapeDtypeStruct((LANE, D_GLA_K), F32),
                   jax.ShapeDtypeStruct((1, D_GLA_K), F32), jax.ShapeDtypeStruct((1, D_GLA_V), F32)],
        scratch_shapes=[pltpu.VMEM((GLA_HEADS, HEAD_V, HEAD_K), F32)],
        input_output_aliases={7: 0},
        compiler_params=_params(1),
    )(proj, wg128, gbias, gng, o_all, states, dyin, dproj)


def _proj_bwd_x(dproj, w_full, dh1, deps=()):
    T, P = dproj.shape
    D = w_full.shape[0]
    tm, tk = _tile(T, 512), _tile(P, 1280)

    def ep(acc_ref, ex, o, i, j):
        o[0][...] = DN_ALPHA * ex[0][...] + acc_ref[...]

    row = pl.BlockSpec((tm, D), lambda i, j, k: (i, 0))
    return _mm("proj_bwd_x", "nt", dproj, w_full, M=T, N=D, K=P, tm=tm, tn=D, tk=tk,
               outs=[(jax.ShapeDtypeStruct((T, D), F32), row)], extras=[(dh1, row)], epilogue=ep, deps=deps)[0]


def _place():
    x, y, c = lax.axis_index("x"), lax.axis_index("y"), lax.axis_index("c")
    chips = [(1 - x, y), (x, 1 - y), (1 - x, 1 - y)]
    return x, y, c, chips


def _rcopy(src, dst, ssem, rsem, dev):
    return pltpu.make_async_remote_copy(src_ref=src, dst_ref=dst, send_sem=ssem, recv_sem=rsem,
                                        device_id=dev, device_id_type=MESH)


def _all_gather(name, shards, deps=()):
    n = len(shards)

    def body(*refs):
        ins, outs = refs[:n], refs[n + len(deps):2 * n + len(deps)]
        ssem, rsem, lsem = refs[2 * n + len(deps):]
        x, y, c, chips = _place()
        me, sib = (x, y, c), (x, y, 1 - c)

        def slot(w, px, py, pc):
            return outs[w].at[4 * px + 2 * py + pc]

        started = []
        for w in range(n):
            lc = pltpu.make_async_copy(ins[w], slot(w, *me), lsem.at[w])
            lc.start()
            started.append(lc)
        sends = []
        for w in range(n):
            cp = _rcopy(ins[w], slot(w, *me), ssem.at[7 * w], rsem.at[7 * w], sib)
            cp.start()
            sends.append(cp)
            for jx, chip in enumerate(chips):
                cp = _rcopy(ins[w], slot(w, *me), ssem.at[7 * w + 1 + jx], rsem.at[7 * w + 1 + jx], (*chip, c))
                cp.start()
                sends.append(cp)
        for w in range(n):
            for jx, chip in enumerate(chips):
                blk = slot(w, *chip, c)
                _rcopy(blk, blk, ssem.at[7 * w + 1 + jx], rsem.at[7 * w + 1 + jx], me).wait_recv()
                cp = _rcopy(blk, blk, ssem.at[7 * w + 4 + jx], rsem.at[7 * w + 4 + jx], sib)
                cp.start()
                sends.append(cp)
        for w in range(n):
            blk = slot(w, x, y, 1 - c)
            _rcopy(blk, blk, ssem.at[7 * w], rsem.at[7 * w], me).wait_recv()
            for jx, chip in enumerate(chips):
                blk = slot(w, *chip, 1 - c)
                _rcopy(blk, blk, ssem.at[7 * w + 4 + jx], rsem.at[7 * w + 4 + jx], me).wait_recv()
        for cp in sends:
            cp.wait_send()
        for lc in started:
            lc.wait()

    return pl.pallas_call(
        body,
        name=name,
        in_specs=_hbm_specs(n + len(deps)),
        out_specs=_hbm_specs(n),
        out_shape=[jax.ShapeDtypeStruct((N_DEV,) + s.shape, s.dtype) for s in shards],
        scratch_shapes=[pltpu.SemaphoreType.DMA((7 * n,)), pltpu.SemaphoreType.DMA((7 * n,)),
                        pltpu.SemaphoreType.DMA((n,))],
    )(*shards, *deps)


HBM_SPEC = pl.BlockSpec(memory_space=pltpu.HBM)
SEM_SPEC = pl.BlockSpec(memory_space=pltpu.SEMAPHORE)
SIDE_EFFECT = pltpu.SideEffectType.DATAFLOW_SIDE_EFFECTING


def _cast_place(name, ids, w, deps=(), dtype=BF16):
    R, C = w.shape
    tr = _tile(R, 256)

    def body(ids_ref, w_ref, *rest):
        rest[len(deps)][...] = w_ref[...].astype(dtype)

    return pl.pallas_call(
        body,
        name=name,
        grid_spec=pltpu.PrefetchScalarGridSpec(
            num_scalar_prefetch=1,
            grid=(R // tr,),
            in_specs=[pl.BlockSpec((tr, C), lambda r, ids: (r, 0))] + _hbm_specs(len(deps)),
            out_specs=pl.BlockSpec((None, tr, C), lambda r, ids: (ids[0], r, 0)),
        ),
        out_shape=jax.ShapeDtypeStruct((N_DEV, R, C), dtype),
        compiler_params=_params(1),
    )(ids, w, *deps)


def _xfer_start(name, bufs, plan, n):
    nb = len(bufs)

    def body(*refs):
        ins = refs[:nb]
        ssem, rsem = refs[nb], refs[nb + 1]
        token = refs[2 * nb + 2]
        x, y, c, chips = _place()
        for k, (src, dst, dev, _) in enumerate(plan(ins, x, y, c, chips)):
            _rcopy(src, dst, ssem.at[k], rsem.at[k], dev).start()
        token[...] = jnp.zeros_like(token)

    res = pl.pallas_call(
        body,
        name=name,
        out_shape=(pltpu.SemaphoreType.DMA((n,)), pltpu.SemaphoreType.DMA((n,)),
                   *[pltpu.HBM(b.shape, b.dtype) for b in bufs], jax.ShapeDtypeStruct((8, LANE), F32)),
        in_specs=[HBM_SPEC] * nb,
        out_specs=(SEM_SPEC, SEM_SPEC, *[HBM_SPEC] * nb, pl.BlockSpec(memory_space=pltpu.VMEM)),
        input_output_aliases={i: 2 + i for i in range(nb)},
        compiler_params=pltpu.CompilerParams(has_side_effects=SIDE_EFFECT),
    )(*[pltpu.with_memory_space_constraint(b, pltpu.HBM) for b in bufs])
    return dict(sems=res[:2], bufs=list(res[2:2 + nb]), token=res[2 + nb], plan=plan, n=n)


def _xfer_wait(name, started, after):
    bufs, plan = started["bufs"], started["plan"]
    nb = len(bufs)

    def body(*refs):
        ins = refs[:nb]
        ssem, rsem = refs[nb], refs[nb + 1]
        x, y, c, chips = _place()
        for k, (src, _, dev, land) in enumerate(plan(ins, x, y, c, chips)):
            cp = _rcopy(src, land, ssem.at[k], rsem.at[k], dev)
            cp.wait_send()
            cp.wait_recv()

    res = pl.pallas_call(
        body,
        name=name,
        out_shape=tuple(pltpu.HBM(b.shape, b.dtype) for b in bufs),
        in_specs=[HBM_SPEC] * nb + [SEM_SPEC, SEM_SPEC, pl.BlockSpec(memory_space=pl.ANY)],
        out_specs=tuple([HBM_SPEC] * nb),
        input_output_aliases={i: i for i in range(nb)},
        compiler_params=pltpu.CompilerParams(has_side_effects=SIDE_EFFECT),
    )(*bufs, *started["sems"], after)
    return list(res)


def _plan_gather_chips(refs, x, y, c, chips):
    (land,) = refs
    mine = land.at[4 * x + 2 * y + c]
    plan = [(mine, mine, (x, y, 1 - c), land.at[4 * x + 2 * y + (1 - c)])]
    for px, py in chips:
        plan.append((mine, mine, (px, py, c), land.at[4 * px + 2 * py + c]))
    return plan


def _plan_gather_pass(refs, x, y, c, chips):
    (land,) = refs
    return [(land.at[4 * px + 2 * py + c], land.at[4 * px + 2 * py + c], (x, y, 1 - c),
             land.at[4 * px + 2 * py + (1 - c)]) for px, py in chips]


def _plan_reduce_core(refs, x, y, c, chips):
    grad, recv = refs
    return [(grad.at[2 * q + (1 - c)], recv.at[q], (x, y, 1 - c), recv.at[q]) for q in range(N_CHIP)]


def _plan_reduce_chips(refs, x, y, c, chips):
    part, land = refs
    return [(part.at[2 * px + py], land.at[2 * x + y], (px, py, c), land.at[2 * px + py]) for px, py in chips]


def _chip_sums(name, ids, grad, recv):
    _, R, C = grad.shape
    tr = _tile(R, 256)

    def body(ids_ref, g_ref, r_ref, o_ref):
        o_ref[...] = (g_ref[...] + r_ref[...]).astype(BF16)

    return pl.pallas_call(
        body,
        name=name,
        grid_spec=pltpu.PrefetchScalarGridSpec(
            num_scalar_prefetch=1,
            grid=(N_CHIP - 1, R // tr),
            in_specs=[pl.BlockSpec((None, tr, C), lambda q, r, ids: (2 * ids[3 + q] + ids[2], r, 0)),
                      pl.BlockSpec((None, tr, C), lambda q, r, ids: (ids[3 + q], r, 0))],
            out_specs=pl.BlockSpec((None, tr, C), lambda q, r, ids: (ids[3 + q], r, 0)),
        ),
        out_shape=jax.ShapeDtypeStruct((N_CHIP, R, C), BF16),
        compiler_params=_params(2),
    )(ids, grad, recv)


def _adamw(w, g, m, v):
    m = ADAM_B1 * m + (1.0 - ADAM_B1) * g
    v = ADAM_B2 * v + (1.0 - ADAM_B2) * (g * g)
    m_hat = m / (1.0 - ADAM_B1 ** ADAM_STEP)
    v_hat = v / (1.0 - ADAM_B2 ** ADAM_STEP)
    delta = -ADAM_LR * (m_hat / (jnp.sqrt(v_hat) + ADAM_EPS) + ADAM_WD * w)
    return delta, m, v


def _reduce_adamw(name, ids, grad, recv, landed, w, m, v):
    _, R, C = grad.shape
    tr = _tile(R, 256)

    def body(ids_ref, g_ref, r_ref, l1_ref, l2_ref, l3_ref, w_ref, m_ref, v_ref, go_ref, do_ref, mo_ref, vo_ref):
        g = g_ref[...] + r_ref[...]
        g = g + l1_ref[...].astype(F32)
        g = g + l2_ref[...].astype(F32)
        g = g + l3_ref[...].astype(F32)
        delta, mn, vn = _adamw(w_ref[...], g, m_ref[...], v_ref[...])
        go_ref[...] = g
        do_ref[...] = delta
        mo_ref[...] = mn
        vo_ref[...] = vn

    def pick(k):
        return pl.BlockSpec((None, tr, C), lambda r, ids: (ids[k], r, 0))

    flat = pl.BlockSpec((tr, C), lambda r, ids: (r, 0))
    shp = jax.ShapeDtypeStruct((R, C), F32)
    return pl.pallas_call(
        body,
        name=name,
        grid_spec=pltpu.PrefetchScalarGridSpec(
            num_scalar_prefetch=1,
            grid=(R // tr,),
            in_specs=[pick(0), pick(1), pick(3), pick(4), pick(5), flat, flat, flat],
            out_specs=[flat, flat, flat, flat],
        ),
        out_shape=[shp, shp, shp, shp],
        compiler_params=_params(1),
    )(ids, grad, recv, landed, landed, landed, w, m, v)


def _small_adamw(packs, w, m, v):
    def body(p_ref, w_ref, m_ref, v_ref, g_ref, d_ref, mo_ref, vo_ref):
        g = p_ref[0]
        for dvc in range(1, N_DEV):
            g = g + p_ref[dvc]
        delta, mn, vn = _adamw(w_ref[...], g, m_ref[...], v_ref[...])
        g_ref[...] = g
        d_ref[...] = delta
        mo_ref[...] = mn
        vo_ref[...] = vn

    shp = jax.ShapeDtypeStruct(w.shape, F32)
    return pl.pallas_call(
        body,
        name="small_adamw",
        in_specs=[_full(packs.shape), _full(w.shape), _full(w.shape), _full(w.shape)],
        out_specs=[_full(w.shape)] * 4,
        out_shape=[shp] * 4,
        grid=(1,),
        compiler_params=_params(1),
    )(packs, w, m, v)


def _w_in_pieces():
    cs = D_IN_PROJ // N_DEV
    pieces = []
    for d in range(N_DEV):
        lo, hi = d * cs, (d + 1) * cs
        if hi <= CONV_COLS:
            pieces.append((d, 0, cs, lo))
        elif lo >= CONV_COLS:
            pieces.append((d, 0, cs, lo - CONV_COLS + HALF_P))
        else:
            pieces.append((d, 0, CONV_COLS - lo, lo))
            pieces.append((d, CONV_COLS - lo, cs, HALF_P))
    return pieces


def _w_in_full(gathered):
    nb, D, cs = gathered.shape
    tr = _tile(D, 256)

    def body(g_ref, o_ref):
        o_ref[:, CONV_COLS:HALF_P] = jnp.zeros((tr, HALF_P - CONV_COLS), o_ref.dtype)
        o_ref[:, HALF_P + GLA_COLS:P_INT] = jnp.zeros((tr, HALF_P - GLA_COLS), o_ref.dtype)
        for d, a, b, dst in _w_in_pieces():
            o_ref[:, dst:dst + (b - a)] = g_ref[d, :, a:b]

    return pl.pallas_call(
        body,
        name="w_in_full",
        grid=(D // tr,),
        in_specs=[pl.BlockSpec((nb, tr, cs), lambda r: (0, r, 0))],
        out_specs=pl.BlockSpec((tr, P_INT), lambda r: (r, 0)),
        out_shape=jax.ShapeDtypeStruct((D, P_INT), gathered.dtype),
        compiler_params=_params(1),
    )(gathered)


def _w_in_blocks(dw):
    D = dw.shape[0]
    cs = D_IN_PROJ // N_DEV
    tr = _tile(D, 256)

    def body(w_ref, o_ref):
        for d, a, b, src in _w_in_pieces():
            o_ref[d, :, a:b] = w_ref[:, src:src + (b - a)]

    return pl.pallas_call(
        body,
        name="w_in_blocks",
        grid=(D // tr,),
        in_specs=[pl.BlockSpec((tr, P_INT), lambda r: (r, 0))],
        out_specs=pl.BlockSpec((N_DEV, tr, cs), lambda r: (0, r, 0)),
        out_shape=jax.ShapeDtypeStruct((N_DEV, D, cs), dw.dtype),
        compiler_params=_params(1),
    )(dw)


def _rows(vec, n_rows):
    flat = jnp.pad(vec.reshape(-1), (0, n_rows * SP_COLS - vec.size))
    return flat.reshape(n_rows, SP_COLS)


def _pad_cols(a):
    return jnp.pad(a, ((0, 0), (0, SP_COLS - a.shape[1])))


R_CONV_W, R_CONV_G, R_GATE_B, R_GLA_G, R_LN1_G, R_LN1_B, R_LN2_G, R_LN2_B, R_LOSS, R_GATE_W = 0, 3, 4, 5, 6, 8, 10, 12, 14, 16


def _pack(conv_w, conv_g, gate_b, gla_g, ln1_g, ln1_b, ln2_g, ln2_b, loss, gate_w):
    z = jnp.zeros((1, SP_COLS), F32)
    parts = [_pad_cols(conv_w), _pad_cols(conv_g), _pad_cols(gate_b), _pad_cols(gla_g),
             _rows(ln1_g, 2), _rows(ln1_b, 2), _rows(ln2_g, 2), _rows(ln2_b, 2),
             z if loss is None else _pad_cols(jnp.sum(loss, axis=1, keepdims=True)), z, _pad_cols(gate_w)]
    return jnp.concatenate(parts, axis=0)


def _unpack(p, D, conv_cols, gate_cols):
    return dict(
        conv_w=p[R_CONV_W:R_CONV_W + 3, :conv_cols], conv_norm_g=p[R_CONV_G:R_CONV_G + 1, :D_CONV],
        gate_bias=p[R_GATE_B:R_GATE_B + 1, :D_GLA_K], gla_norm_g=p[R_GLA_G:R_GLA_G + 1, :D_GLA_V],
        ln1_g=p[R_LN1_G:R_LN1_G + 2].reshape(1, -1)[:, :D], ln1_b=p[R_LN1_B:R_LN1_B + 2].reshape(1, -1)[:, :D],
        ln2_g=p[R_LN2_G:R_LN2_G + 2].reshape(1, -1)[:, :D], ln2_b=p[R_LN2_B:R_LN2_B + 2].reshape(1, -1)[:, :D],
        w_gate_up=p[R_GATE_W:R_GATE_W + GATE_RANK, :gate_cols])


BIG = ("w_in", "w_out", "w_ff_up", "w_ff_down")
ORDER = ("w_in", "conv_w", "conv_norm_g", "w_gate_up", "gate_bias", "gla_norm_g", "w_out", "ln1_g", "ln1_b",
         "w_ff_up", "w_ff_down", "ln2_g", "ln2_b")


def kernel(x, w_in, conv_w, conv_norm_g, w_gate_up, gate_bias, gla_norm_g, w_out, ln1_g, ln1_b, w_ff_up, w_ff_down, ln2_g, ln2_b, loss_target, m_w_in, m_conv_w, m_conv_norm_g, m_w_gate_up, m_gate_bias, m_gla_norm_g, m_w_out, m_ln1_g, m_ln1_b, m_w_ff_up, m_w_ff_down, m_ln2_g, m_ln2_b, v_w_in, v_conv_w, v_conv_norm_g, v_w_gate_up, v_gate_bias, v_gla_norm_g, v_w_out, v_ln1_g, v_ln1_b, v_w_ff_up, v_w_ff_down, v_ln2_g, v_ln2_b):
    T, D = x.shape[1], x.shape[2]
    xs, target = x[0], loss_target[0]
    xi, yi, ci = lax.axis_index("x"), lax.axis_index("y"), lax.axis_index("c")
    chip = 2 * xi + yi
    dev = 2 * chip + ci
    others = [jnp.where(chip <= q, q + 1, q) for q in range(N_CHIP - 1)]
    ids = jnp.stack([dev, chip, ci] + others).astype(jnp.int32)
    conv_cols, gate_cols = conv_w.shape[2], w_gate_up.shape[2]

    def gather(nm, w, deps, dtype=BF16):
        return _xfer_start("gather_chips_" + nm, [_cast_place("cast_place_" + nm, ids, w, deps, dtype)],
                           _plan_gather_chips, 4)

    def pass_on(nm, started, after):
        (land,) = _xfer_wait("gather_chips_wait_" + nm, started, after)
        return _xfer_start("gather_pass_" + nm, [land], _plan_gather_pass, 3)

    def landed(nm, started, after):
        return _xfer_wait("gather_pass_wait_" + nm, started, after)[0]

    z1 = jnp.zeros((1, 1), F32)
    fwd_pack = _pack(conv_w[0], z1, z1, z1, z1, z1, z1, z1, None, w_gate_up[0])
    ga_in = gather("w_in", w_in[0], ())
    ga_pack = gather("pack", fwd_pack, [ga_in["token"]], F32)
    ga, dep = [], ga_pack["token"]
    for nm, w in zip(BIG[1:], (w_out, w_ff_up, w_ff_down)):
        ga.append(gather(nm, w[0], [dep]))
        dep = ga[-1]["token"]
    gp_in = pass_on("w_in", ga_in, dep)
    gp_pack = pass_on("pack", ga_pack, gp_in["token"])
    g_in = landed("w_in", gp_in, gp_pack["token"])
    w_full = _w_in_full(g_in)
    g_pack = landed("pack", gp_pack, w_full)
    conv_w_full = g_pack[:, R_CONV_W:R_CONV_W + 3, :conv_cols].transpose(1, 0, 2).reshape(3, -1)
    gate_w_full = g_pack[:, R_GATE_W:R_GATE_W + GATE_RANK, :gate_cols].transpose(1, 0, 2).reshape(GATE_RANK, -1)
    conv_w8 = jnp.pad(conv_w_full, ((0, 5), (0, 0)))
    wg128 = jnp.pad(gate_w_full, ((0, LANE - GATE_RANK), (0, 0))).astype(BF16)
    proj = _proj_fwd(xs, w_full, deps=[g_pack])
    yin = _conv_fwd(proj, conv_w8, conv_norm_g)
    gp_out = pass_on("w_out", ga[0], yin)
    o_all, states, yin = _gla_fwd(proj, wg128, gate_bias, gla_norm_g, yin, deps=[gp_out["token"]])
    w_out_full = landed("w_out", gp_out, o_all).reshape(-1, D)
    gp_up = pass_on("w_ff_up", ga[1], o_all)
    xhat1, x1, rstd1 = _mix_ln1(yin, w_out_full, xs, ln1_g, ln1_b, deps=[gp_up["token"]])
    w_up_blk = landed("w_ff_up", gp_up, x1)
    gp_down = pass_on("w_ff_down", ga[2], x1)
    ra, h2 = _ff_up(x1, w_up_blk, deps=[gp_down["token"]])
    w_down_full = landed("w_ff_down", gp_down, ra).reshape(-1, D)
    dh3, dh3b, g_ln2_g, g_ln2_b, loss = _ff_down_loss(h2, w_down_full, xhat1, target, ln1_g, ln1_b, ln2_g, ln2_b)

    def to_core(nm, grad):
        recv = lax.empty((N_CHIP,) + grad.shape[1:], F32)
        return _xfer_start("reduce_core_" + nm, [grad, recv], _plan_reduce_core, N_CHIP)

    def to_chips(nm, started, after):
        grad, recv = _xfer_wait("reduce_core_wait_" + nm, started, after)
        part = _chip_sums("chip_sums_" + nm, ids, grad, recv)
        land = lax.empty(part.shape, BF16)
        return grad, recv, _xfer_start("reduce_chips_" + nm, [part, land], _plan_reduce_chips, N_CHIP - 1)

    da = _ff_down_bwd_act(dh3b, w_down_full, ra)
    gw_down = _grad_w("grad_w_down", h2, dh3b).reshape(N_DEV, -1, D)
    rc_down = to_core("w_ff_down", gw_down)
    gw_up = _grad_w_up_blk(x1, da, N_DEV, deps=[rc_down["token"]])
    gw_down, rv_down, rs_down = to_chips("w_ff_down", rc_down, gw_up)
    rc_up = to_core("w_ff_up", gw_up)
    dh1, dh1b, g_ln1_g, g_ln1_b = _ff_up_bwd_ln1(da, w_up_blk, dh3, xhat1, rstd1, ln1_g,
                                                 deps=[rs_down["token"], rc_up["token"]])
    gw_up, rv_up, rs_up = to_chips("w_ff_up", rc_up, dh1b)
    dyin = _mix_bwd(dh1b, w_out_full, deps=[rs_up["token"]])
    gw_out = _grad_w("grad_w_out", yin, dh1b).reshape(N_DEV, -1, D)
    rc_out = to_core("w_out", gw_out)
    dproj, g_conv_w, g_conv_g = _conv_bwd(proj, dyin, conv_w8, conv_norm_g, deps=[rc_out["token"]])
    dproj, g_gate_w, g_gate_b, g_gla_g = _gla_bwd(proj, wg128, gate_bias, gla_norm_g, o_all, states, dyin, dproj)
    gw_out, rv_out, rs_out = to_chips("w_out", rc_out, dproj)
    gw_in = _w_in_blocks(_grad_w("grad_w_in", xs, dproj, a_fn=_to_bf16, tn_pref=1280, tk_pref=2048, deps=[rs_out["token"]]))
    rc_in = to_core("w_in", gw_in)

    big = {}

    def finish(nm, grad, recv, started, w, m, v, after):
        _, land = _xfer_wait("reduce_chips_wait_" + nm, started, after)
        res = _reduce_adamw("adamw_" + nm, ids, grad, recv, land, w[0], m[0], v[0])
        big[nm] = [a[None] for a in res]
        return res[0]

    done = finish("w_ff_down", gw_down, rv_down, rs_down, w_ff_down, m_w_ff_down, v_w_ff_down, rc_in["token"])
    done = finish("w_ff_up", gw_up, rv_up, rs_up, w_ff_up, m_w_ff_up, v_w_ff_up, done)
    gw_in, rv_in, rs_in = to_chips("w_in", rc_in, done)
    grad_x = _proj_bwd_x(dproj, w_full, dh1, deps=[rs_in["token"]])

    pack = _pack(g_conv_w[:3], g_conv_g, g_gate_b, g_gla_g, g_ln1_g, g_ln1_b, g_ln2_g, g_ln2_b, loss,
                 g_gate_w[:GATE_RANK])
    (packs,) = _all_gather("gather_small_grads", [pack], deps=[grad_x])
    done = finish("w_out", gw_out, rv_out, rs_out, w_out, m_w_out, v_w_out, packs)
    finish("w_in", gw_in, rv_in, rs_in, w_in, m_w_in, v_w_in, done)

    def own_cols(row, n_rows, width):
        cut = lax.dynamic_slice(packs, (0, row, dev * width), (N_DEV, n_rows, width))
        return jnp.pad(cut, ((0, 0), (0, 0), (0, SP_COLS - width)))

    packs_own = jnp.concatenate([own_cols(R_CONV_W, 3, conv_cols), packs[:, R_CONV_W + 3:R_GATE_W],
                                 own_cols(R_GATE_W, GATE_RANK, gate_cols)], axis=1)

    def small_pack(cw, cg, gw, gb, gg, l1g, l1b, l2g, l2b):
        return _pack(cw[0], cg, gb, gg, l1g, l1b, l2g, l2b, None, gw[0])

    w_s = small_pack(conv_w, conv_norm_g, w_gate_up, gate_bias, gla_norm_g, ln1_g, ln1_b, ln2_g, ln2_b)
    m_s = small_pack(m_conv_w, m_conv_norm_g, m_w_gate_up, m_gate_bias, m_gla_norm_g, m_ln1_g, m_ln1_b, m_ln2_g, m_ln2_b)
    v_s = small_pack(v_conv_w, v_conv_norm_g, v_w_gate_up, v_gate_bias, v_gla_norm_g, v_ln1_g, v_ln1_b, v_ln2_g, v_ln2_b)
    g_s, d_s, mn_s, vn_s = _small_adamw(packs_own, w_s, m_s, v_s)
    small = [_unpack(p, D, conv_cols, gate_cols) for p in (g_s, d_s, mn_s, vn_s)]

    def leaf(kind, name):
        if name in BIG:
            return big[name][kind]
        a = small[kind][name]
        return a[None] if name in ("conv_w", "w_gate_up") else a

    out = [g_s[R_LOSS, 0], grad_x[None]]
    for kind in range(4):
        out += [leaf(kind, nm) for nm in ORDER]
    return tuple(out)
```

```python
import jax
import jax.numpy as jnp
from jax import lax
from jax.experimental import pallas as pl
from jax.experimental.pallas import tpu as pltpu

F32 = jnp.float32
BF16 = jnp.bfloat16

D_CONV = 1024
CONV_GROUPS = 8
GLA_HEADS = 4
HEAD_K = 128
HEAD_V = 256
D_GLA_K = 512
D_GLA_V = 1024
GATE_RANK = 16
GATE_TAU = 16.0
CHUNK = 64
LN_EPS = 1e-5
RMS_EPS = 1e-6
DN_ALPHA = 2.0 ** 0.25
D_IN_PROJ = 6160
ADAM_LR = 0.001
ADAM_B1 = 0.9
ADAM_B2 = 0.999
ADAM_EPS = 1e-08
ADAM_WD = 0.01
ADAM_STEP = 10

N_DEV = 8
N_CHIP = 4
LANE = 128
HALF_P = 3200
P_INT = 2 * HALF_P
CONV_COLS = 3 * D_CONV
GLA_COLS = D_IN_PROJ - CONV_COLS
SP_ROWS = 32
SP_COLS = 1024
VMEM_LIMIT = 56 * 1024 * 1024

NN = ((1,), (0,))
NT = ((1,), (1,))
TN = ((0,), (0,))
MESH = pl.DeviceIdType.MESH


def _dot(a, b, dims, precision=None):
    return lax.dot_general(a, b, (dims, ((), ())), preferred_element_type=F32, precision=precision)


def _tile(n, pref):
    if n <= pref:
        return n
    t = (pref // LANE) * LANE
    while t > 0 and n % t:
        t -= LANE
    assert t > 0, (n, pref)
    return t


def _params(n_axes):
    return pltpu.CompilerParams(dimension_semantics=("arbitrary",) * n_axes, vmem_limit_bytes=VMEM_LIMIT)


def _full(shape):
    nd = len(shape)
    return pl.BlockSpec(shape, lambda *_: (0,) * nd)


def _hbm_specs(n):
    return [pl.BlockSpec(memory_space=pl.ANY)] * n


def _mm(name, mode, a, b, *, M, N, K, tm, tn, tk, outs, epilogue, extras=(), a_fn=None, a_spec=None, b_spec=None,
        deps=()):
    ni, nj, nk = M // tm, N // tn, K // tk
    assert ni * tm == M and nj * tn == N and nk * tk == K, (name, M, N, K, tm, tn, tk)
    if a_spec is None:
        a_spec = (pl.BlockSpec((tk, tm), lambda i, j, k: (k, i)) if mode == "tn"
                  else pl.BlockSpec((tm, tk), lambda i, j, k: (i, k)))
    if b_spec is None:
        b_spec = (pl.BlockSpec((tn, tk), lambda i, j, k: (j, k)) if mode == "nt"
                  else pl.BlockSpec((tk, tn), lambda i, j, k: (k, j)))
    dims = {"nn": NN, "nt": NT, "tn": TN}[mode]
    n_ex, n_out, n_dep = len(extras), len(outs), len(deps)

    def body(*refs):
        a_ref, b_ref = refs[0], refs[1]
        ex = refs[2:2 + n_ex]
        o = refs[2 + n_ex + n_dep:2 + n_ex + n_dep + n_out]
        acc_ref = refs[2 + n_ex + n_dep + n_out]
        i, j, k = pl.program_id(0), pl.program_id(1), pl.program_id(2)
        av = a_ref[...]
        if a_fn is not None:
            av = a_fn(av)
        part = _dot(av, b_ref[...], dims)
        if nk == 1 and epilogue is None:
            o[0][...] = part.astype(o[0].dtype)
        elif nk == 1:
            acc_ref[...] = part
            epilogue(acc_ref, ex, o, i, j)
        else:
            @pl.when(k == 0)
            def _():
                acc_ref[...] = part

            @pl.when(k > 0)
            def _():
                acc_ref[...] += part

            @pl.when(k == nk - 1)
            def _():
                if epilogue is None:
                    o[0][...] = acc_ref[...].astype(o[0].dtype)
                else:
                    epilogue(acc_ref, ex, o, i, j)

    return pl.pallas_call(
        body,
        name=name,
        grid=(ni, nj, nk),
        in_specs=[a_spec, b_spec] + [s for _, s in extras] + _hbm_specs(n_dep),
        out_specs=[s for _, s in outs],
        out_shape=[s for s, _ in outs],
        scratch_shapes=[pltpu.VMEM((8, LANE) if nk == 1 and epilogue is None else (tm, tn), F32)],
        compiler_params=_params(3),
    )(a, b, *[x for x, _ in extras], *deps)


def _mm_rows(name, mode, a, b, *, M, N, K, tm, tk, row_ins, vec_ins, row_outs, stat_outs, chunk_fn,
             b_spec=None, deps=()):
    ni, nk = M // tm, K // tk
    rc = tm // nk
    assert ni * tm == M and nk * tk == K and rc * nk == tm and rc % 16 == 0, (name, M, K, tm, tk)
    dims = {"nn": NN, "nt": NT}[mode]
    last = ni - 1

    def kk(i, k):
        return jnp.where(i < ni, k, nk - 1)

    a_spec = pl.BlockSpec((tm, tk), lambda i, k: (jnp.minimum(i, last), kk(i, k)))
    if b_spec is None:
        b_spec = (pl.BlockSpec((N, tk), lambda i, k: (0, kk(i, k))) if mode == "nt"
                  else pl.BlockSpec((tk, N), lambda i, k: (kk(i, k), 0)))
    prev_rows = lambda i, k: (jnp.maximum((i - 1) * nk + k, 0), 0)
    n_ri, n_vi, n_ro, n_so, n_dep = len(row_ins), len(vec_ins), len(row_outs), len(stat_outs), len(deps)

    def body(*refs):
        a_ref, b_ref = refs[0], refs[1]
        pos = 2
        ri = refs[pos:pos + n_ri]; pos += n_ri
        vi = refs[pos:pos + n_vi]; pos += n_vi + n_dep
        ro = refs[pos:pos + n_ro]; pos += n_ro
        so = refs[pos:pos + n_so]; pos += n_so
        accs = refs[pos:pos + 2]
        i, k = pl.program_id(0), pl.program_id(1)

        @pl.when((i == 0) & (k == 0))
        def _():
            accs[0][...] = jnp.zeros_like(accs[0])
            accs[1][...] = jnp.zeros_like(accs[1])
            for st in so:
                st[...] = jnp.zeros_like(st)

        def finish_rows(prev_ref):
            rows = pl.ds(pl.multiple_of(k * rc, rc), rc)
            done = prev_ref[rows, :]
            prev_ref[rows, :] = jnp.zeros((rc, N), F32)
            chunk_fn(done, i > 0, ri, vi, ro, so)

        for parity in (0, 1):
            @pl.when((i < ni) & (lax.rem(i, 2) == parity))
            def _(parity=parity):
                part = _dot(a_ref[...], b_ref[...], dims)
                finish_rows(accs[1 - parity])
                accs[parity][...] += part

        @pl.when(i == ni)
        def _():
            finish_rows(accs[last % 2])

    row_spec = lambda arr: pl.BlockSpec((rc, arr.shape[1]), prev_rows)
    return pl.pallas_call(
        body,
        name=name,
        grid=(ni + 1, nk),
        in_specs=[a_spec, b_spec] + [row_spec(x) for x in row_ins] + [_full(x.shape) for x in vec_ins]
        + _hbm_specs(n_dep),
        out_specs=[row_spec(s) for s in row_outs] + [_full(s.shape) for s in stat_outs],
        out_shape=list(row_outs) + list(stat_outs),
        scratch_shapes=[pltpu.VMEM((tm, N), F32), pltpu.VMEM((tm, N), F32)],
        compiler_params=_params(2),
    )(a, b, *row_ins, *vec_ins, *deps)


SUB_ROWS = 16


def _by_sub_rows(n_rows, fn):
    sums = None
    for r0 in range(0, n_rows, SUB_ROWS):
        part = fn(slice(r0, r0 + SUB_ROWS))
        if part:
            sums = part if sums is None else tuple(x + y for x, y in zip(sums, part))
    return sums


def _to_bf16(v):
    return v.astype(BF16)


def _ln_bwd(dy, xhat, rstd, g):
    dxh = dy * g
    m1 = jnp.mean(dxh, axis=-1, keepdims=True)
    m2 = jnp.mean(dxh * xhat, axis=-1, keepdims=True)
    return rstd * (dxh - m1 - xhat * m2)


def _ln_fwd(h):
    mu = jnp.mean(h, axis=-1, keepdims=True)
    xc = h - mu
    var = jnp.mean(xc * xc, axis=-1, keepdims=True)
    rstd = lax.rsqrt(var + LN_EPS)
    return xc * rstd, rstd


def _proj_fwd(x, w_full, deps=()):
    T, D = x.shape
    P = w_full.shape[1]
    tm, tn = _tile(T, 1024), _tile(P, 1280)
    return _mm("proj_fwd", "nn", x, w_full, M=T, N=P, K=D, tm=tm, tn=tn, tk=D,
               outs=[(jax.ShapeDtypeStruct((T, P), F32), pl.BlockSpec((tm, tn), lambda i, j, k: (i, j)))],
               epilogue=None, a_fn=_to_bf16, deps=deps)[0]


def _conv_shift(h, hp):
    row = lax.broadcasted_iota(jnp.int32, h.shape, 0)
    hm1 = hp[7:8, :]
    hm2 = hp[6:7, :]
    h1 = jnp.where(row == 0, hm1, pltpu.roll(h, 1, 0))
    h2 = jnp.where(row == 0, hm2, jnp.where(row == 1, hm1, pltpu.roll(h, 2, 0)))
    return h1, h2


def _conv_fwd(proj, conv_w8, conv_g):
    T = proj.shape[0]
    tt = _tile(T, 256)
    nt = T // tt
    t8 = tt // 8

    def body(b_ref, c_ref, u_ref, cp_ref, up_ref, w_ref, g_ref, yin_ref):
        i = pl.program_id(0)
        h = c_ref[...] * u_ref[...]
        hp = jnp.where(i > 0, cp_ref[...] * up_ref[...], 0.0)
        h1, h2 = _conv_shift(h, hp)
        w = w_ref[...]
        y = w[0:1, :] * h2 + w[1:2, :] * h1 + w[2:3, :] * h
        p = b_ref[...] * y
        parts = []
        for gi in range(CONV_GROUPS):
            pg = p[:, gi * LANE:(gi + 1) * LANE]
            r = lax.rsqrt(jnp.mean(pg * pg, axis=-1, keepdims=True) + RMS_EPS)
            parts.append(pg * r)
        yn = jnp.concatenate(parts, axis=1) * g_ref[...]
        yin_ref[...] = yn.astype(BF16)

    def col(cidx):
        return pl.BlockSpec((tt, D_CONV), lambda i: (i, cidx))

    def prev(cidx):
        return pl.BlockSpec((8, D_CONV), lambda i: (jnp.maximum(i * t8 - 1, 0), cidx))

    return pl.pallas_call(
        body,
        name="conv_fwd",
        grid=(nt,),
        in_specs=[col(0), col(1), col(2), prev(1), prev(2), _full((8, D_CONV)), _full((1, D_CONV))],
        out_specs=pl.BlockSpec((tt, D_CONV), lambda i: (i, 0)),
        out_shape=jax.ShapeDtypeStruct((T, 2 * D_CONV), BF16),
        compiler_params=_params(1),
    )(proj, proj, proj, proj, proj, conv_w8, conv_g)


def _log_sigmoid(z):
    return jnp.minimum(z, 0.0) - jnp.log(1.0 + jnp.exp(-jnp.abs(z)))


def _gla_chunk_terms(blk, wg_ref, gb_ref):
    q = blk[:, 0:512]
    k = blk[:, 512:1024]
    zl = blk[:, 3072:3200]
    z = _dot(zl.astype(BF16), wg_ref[...], NN) + gb_ref[...]
    log_a = _log_sigmoid(z) * (1.0 / GATE_TAU)
    ri = lax.broadcasted_iota(jnp.int32, (CHUNK, CHUNK), 0)
    ci = lax.broadcasted_iota(jnp.int32, (CHUNK, CHUNK), 1)
    causal = ri >= ci
    lower = causal.astype(F32)
    bcum = _dot(lower, log_a, NN, precision=lax.Precision.HIGHEST)
    return q, k, zl, z, bcum, causal


def _gla_head_terms(q, k, bcum, h):
    sl = slice(h * HEAD_K, (h + 1) * HEAD_K)
    bh = bcum[:, sl]
    bl = bh[CHUNK - 1:CHUNK, :]
    eb = jnp.exp(bh)
    enb = jnp.exp(-bh)
    eend = jnp.exp(bl - bh)
    dec = jnp.exp(bl)
    qd = q[:, sl] * (HEAD_K ** -0.5) * eb
    ki = k[:, sl] * enb
    ke = k[:, sl] * eend
    return eb, enb, eend, dec, qd, ki, ke


def _sigmoid(x):
    return 1.0 / (1.0 + jnp.exp(-x))


def _gla_fwd(proj, wg128, gbias, gng, yin, deps=()):
    T = proj.shape[0]
    nch = T // CHUNK

    def body(p_ref, wg_ref, gb_ref, gn_ref, yin_in_ref, *rest):
        o_ref, st_ref, yin_ref, s_ref = rest[len(deps):]
        n = pl.program_id(0)

        @pl.when(n == 0)
        def _():
            s_ref[...] = jnp.zeros_like(s_ref)

        blk = p_ref[...]
        q, k, _, _, bcum, causal = _gla_chunk_terms(blk, wg_ref, gb_ref)
        v = blk[:, 1024:2048]
        r = blk[:, 2048:3072]
        gn = gn_ref[...]
        for h in range(GLA_HEADS):
            _, _, _, dec, qd, ki, ke = _gla_head_terms(q, k, bcum, h)
            vs = slice(h * HEAD_V, (h + 1) * HEAD_V)
            vb = v[:, vs].astype(BF16)
            qdb = qd.astype(BF16)
            a = jnp.where(causal, _dot(qdb, ki.astype(BF16), NT), 0.0)
            st = s_ref[h]
            o = _dot(a.astype(BF16), vb, NN) + _dot(qdb, st.astype(BF16), NT)
            st_ref[h] = st
            s_ref[h] = dec * st + _dot(vb, ke.astype(BF16), TN)
            o_ref[:, vs] = o
            rinv = lax.rsqrt(jnp.mean(o * o, axis=-1, keepdims=True) + RMS_EPS)
            rh = r[:, vs]
            yin_ref[:, vs] = (o * rinv * gn[:, vs] * (rh * _sigmoid(rh))).astype(BF16)

    return pl.pallas_call(
        body,
        name="gla_fwd",
        grid=(nch,),
        in_specs=[pl.BlockSpec((CHUNK, HALF_P), lambda n: (n, 1)), _full((LANE, D_GLA_K)), _full((1, D_GLA_K)),
                  _full((1, D_GLA_V)), pl.BlockSpec(memory_space=pl.ANY)] + _hbm_specs(len(deps)),
        out_specs=[pl.BlockSpec((CHUNK, D_GLA_V), lambda n: (n, 0)),
                   pl.BlockSpec((None, GLA_HEADS, HEAD_V, HEAD_K), lambda n: (n, 0, 0, 0)),
                   pl.BlockSpec((CHUNK, D_GLA_V), lambda n: (n, 1))],
        out_shape=[jax.ShapeDtypeStruct((T, D_GLA_V), F32),
                   jax.ShapeDtypeStruct((nch, GLA_HEADS, HEAD_V, HEAD_K), F32),
                   jax.ShapeDtypeStruct(yin.shape, BF16)],
        scratch_shapes=[pltpu.VMEM((GLA_HEADS, HEAD_V, HEAD_K), F32)],
        input_output_aliases={4: 2},
        compiler_params=_params(1),
    )(proj, wg128, gbias, gng, yin, *deps)


def _mix_ln1(yin, w_out, x, ln_g, ln_b, deps=()):
    T, D = x.shape
    KY = yin.shape[1]
    tm = _tile(T, 1024)

    def chunk(acc, valid, ri, vi, ro, so):
        g, b = vi[0][...], vi[1][...]

        def sub(rows):
            xhat, rstd = _ln_fwd(DN_ALPHA * ri[0][rows, :] + acc[rows, :])
            ro[0][rows, :] = xhat
            ro[1][rows, :] = (xhat * g + b).astype(BF16)
            ro[2][rows, :] = rstd

        _by_sub_rows(acc.shape[0], sub)

    return _mm_rows("mix_ln1", "nn", yin, w_out, M=T, N=D, K=KY, tm=tm, tk=_tile(KY, 512),
                    row_ins=[x], vec_ins=[ln_g, ln_b],
                    row_outs=[jax.ShapeDtypeStruct((T, D), F32), jax.ShapeDtypeStruct((T, D), BF16),
                              jax.ShapeDtypeStruct((T, 1), F32)],
                    stat_outs=[], chunk_fn=chunk, deps=deps)


def _ff_up(x1, w_up_blk, first, count, prev=None, deps=()):
    T, D = x1.shape
    nb, _, fb = w_up_blk.shape
    tm = _tile(T, 1024)
    ni = T // tm
    n_dep = len(deps) + (2 if prev is not None else 0)

    def body(a_ref, b_ref, *rest):
        ra_ref, h2_ref = rest[n_dep:n_dep + 2]
        ra = jnp.maximum(_dot(a_ref[...], b_ref[...], NN), 0.0)
        ra_ref[...] = ra.astype(BF16)
        h2_ref[...] = (ra * ra).astype(BF16)

    blk = pl.BlockSpec((tm, fb), lambda i, j: (i, first + j))
    shp = jax.ShapeDtypeStruct((T, nb * fb), BF16)
    keep = list(prev) if prev is not None else []
    return pl.pallas_call(
        body,
        name="ff_up_%d" % first,
        grid=(ni, count),
        in_specs=[pl.BlockSpec((tm, D), lambda i, j: (i, 0)),
                  pl.BlockSpec((None, D, fb), lambda i, j: (first + j, 0, 0))] + _hbm_specs(n_dep),
        out_specs=[blk, blk],
        out_shape=[shp, shp],
        input_output_aliases=({2: 0, 3: 1} if prev is not None else {}),
        compiler_params=_params(2),
    )(x1, w_up_blk, *keep, *deps)


def _ff_down_loss(h2, w_down, xhat1, target, g1, b1, g2, b2):
    T, F = h2.shape
    D = w_down.shape[1]
    tm = _tile(T, 1024)
    inv_d = 1.0 / D

    def chunk(acc, valid, ri, vi, ro, so):
        g1v, b1v, g2v, b2v = (v[...] for v in vi)

        def sub(rows):
            x1 = ri[0][rows, :] * g1v + b1v
            xhat, rstd = _ln_fwd(DN_ALPHA * x1 + acc[rows, :])
            e = xhat * g2v + b2v - ri[1][rows, :]
            dy = e * inv_d
            dh = _ln_bwd(dy, xhat, rstd, g2v)
            ro[0][rows, :] = dh
            ro[1][rows, :] = dh.astype(BF16)
            return (jnp.sum(dy * xhat, axis=0, keepdims=True), jnp.sum(dy, axis=0, keepdims=True),
                    jnp.sum(e * e, axis=0, keepdims=True))

        sg, sb, sl = _by_sub_rows(acc.shape[0], sub)
        so[0][...] += jnp.where(valid, sg, 0.0)
        so[1][...] += jnp.where(valid, sb, 0.0)
        so[2][...] += jnp.where(valid, sl * (0.5 * inv_d), 0.0)

    vshape = jax.ShapeDtypeStruct((1, D), F32)
    return _mm_rows("ff_down_loss", "nn", h2, w_down, M=T, N=D, K=F, tm=tm, tk=_tile(F, 1024),
                    row_ins=[xhat1, target], vec_ins=[g1, b1, g2, b2],
                    row_outs=[jax.ShapeDtypeStruct((T, D), F32), jax.ShapeDtypeStruct((T, D), BF16)],
                    stat_outs=[vshape, vshape, vshape], chunk_fn=chunk)


def _ff_down_bwd_act(dh3b, w_down, ra):
    T, D = dh3b.shape
    F = w_down.shape[0]
    tm, tn = _tile(T, 1024), _tile(F, 1024)

    def ep(acc_ref, ex, o, i, j):
        o[0][...] = (acc_ref[...] * (2.0 * ex[0][...].astype(F32))).astype(BF16)

    blk = pl.BlockSpec((tm, tn), lambda i, j, k: (i, j))
    return _mm("ff_down_bwd_act", "nt", dh3b, w_down, M=T, N=F, K=D, tm=tm, tn=tn, tk=D,
               outs=[(jax.ShapeDtypeStruct((T, F), BF16), blk)], extras=[(ra, blk)], epilogue=ep)[0]


def _grad_w(name, a, b, *, a_fn=None, tm_pref=1024, tn_pref=1024, tk_pref=4096, deps=()):
    T, M = a.shape
    N = b.shape[1]
    tm, tn, tk = _tile(M, tm_pref), _tile(N, tn_pref), _tile(T, tk_pref)
    return _mm(name, "tn", a, b, M=M, N=N, K=T, tm=tm, tn=tn, tk=tk, a_fn=a_fn, deps=deps,
               outs=[(jax.ShapeDtypeStruct((M, N), F32), pl.BlockSpec((tm, tn), lambda i, j, k: (i, j)))],
               epilogue=None)[0]


def _grad_w_up_blk(x1, da, nb, deps=()):
    T, D = x1.shape
    F = da.shape[1]
    fb = F // nb
    tm, tk = _tile(D, 1024), _tile(T, 4096)
    return _mm("grad_w_up", "tn", x1, da, M=D, N=F, K=T, tm=tm, tn=fb, tk=tk, deps=deps,
               outs=[(jax.ShapeDtypeStruct((nb, D, fb), F32),
                      pl.BlockSpec((None, tm, fb), lambda i, j, k: (j, i, 0)))],
               epilogue=None)[0]


def _ff_up_bwd_ln1(da, w_up_blk, dh3, xhat1, rstd1, g1, deps=()):
    T, F = da.shape
    nb, D, fb = w_up_blk.shape
    tm = _tile(T, 1024)

    def chunk(acc, valid, ri, vi, ro, so):
        g = vi[0][...]

        def sub(rows):
            dx1 = DN_ALPHA * ri[0][rows, :] + acc[rows, :]
            xhat = ri[1][rows, :]
            dh = _ln_bwd(dx1, xhat, ri[2][rows, :], g)
            ro[0][rows, :] = dh
            ro[1][rows, :] = dh.astype(BF16)
            return jnp.sum(dx1 * xhat, axis=0, keepdims=True), jnp.sum(dx1, axis=0, keepdims=True)

        sg, sb = _by_sub_rows(acc.shape[0], sub)
        so[0][...] += jnp.where(valid, sg, 0.0)
        so[1][...] += jnp.where(valid, sb, 0.0)

    nk = F // fb
    vshape = jax.ShapeDtypeStruct((1, D), F32)
    return _mm_rows("ff_up_bwd_ln1", "nt", da, w_up_blk, M=T, N=D, K=F, tm=tm, tk=fb,
                    b_spec=pl.BlockSpec((None, D, fb), lambda i, k: (jnp.where(i < T // tm, k, nk - 1), 0, 0)),
                    row_ins=[dh3, xhat1, rstd1], vec_ins=[g1],
                    row_outs=[jax.ShapeDtypeStruct((T, D), F32), jax.ShapeDtypeStruct((T, D), BF16)],
                    stat_outs=[vshape, vshape], chunk_fn=chunk, deps=deps)


def _mix_bwd(dh1b, w_out, deps=()):
    T, D = dh1b.shape
    KY = w_out.shape[0]
    tm, tn = _tile(T, 1024), _tile(KY, 1024)
    return _mm("mix_bwd", "nt", dh1b, w_out, M=T, N=KY, K=D, tm=tm, tn=tn, tk=D, deps=deps,
               outs=[(jax.ShapeDtypeStruct((T, KY), F32), pl.BlockSpec((tm, tn), lambda i, j, k: (i, j)))],
               epilogue=None)[0]


def _conv_bwd(proj, dyin, conv_w8, conv_g, deps=()):
    T = proj.shape[0]
    tt = _tile(T, 256)
    nt = T // tt
    t8 = tt // 8
    nx = tt + 8

    def body(b_ref, c_ref, u_ref, d_ref, bn_ref, cn_ref, un_ref, dn_ref, cp_ref, up_ref, w_ref, g_ref, *rest):
        dp_ref, dw_ref, dg_ref = rest[len(deps):]
        i = pl.program_id(0)

        @pl.when(i == 0)
        def _():
            dw_ref[...] = jnp.zeros_like(dw_ref)
            dg_ref[...] = jnp.zeros_like(dg_ref)

        more = i < nt - 1

        def ext(cur_ref, nxt_ref):
            return jnp.concatenate([cur_ref[...], jnp.where(more, nxt_ref[...], 0.0)], axis=0)

        bx, cx, ux, dx = ext(b_ref, bn_ref), ext(c_ref, cn_ref), ext(u_ref, un_ref), ext(d_ref, dn_ref)
        hx = cx * ux
        hp = jnp.where(i > 0, cp_ref[...] * up_ref[...], 0.0)
        h1, h2 = _conv_shift(hx, hp)
        w = w_ref[...]
        g = g_ref[...]
        yx = w[0:1, :] * h2 + w[1:2, :] * h1 + w[2:3, :] * hx
        px = bx * yx
        dps, dgs = [], []
        for gi in range(CONV_GROUPS):
            sl = slice(gi * LANE, (gi + 1) * LANE)
            pg, dg_ = px[:, sl], dx[:, sl]
            r = lax.rsqrt(jnp.mean(pg * pg, axis=-1, keepdims=True) + RMS_EPS)
            gd = g[:, sl] * dg_
            dps.append(r * gd - pg * (r * r * r) * jnp.mean(pg * gd, axis=-1, keepdims=True))
            dgs.append(jnp.sum((dg_ * pg * r)[:tt, :], axis=0, keepdims=True))
        dpx = jnp.concatenate(dps, axis=1)
        dg_ref[...] += jnp.concatenate(dgs, axis=1)
        dyx = dpx * bx
        dyc = dyx[:tt, :]
        dh = (w[2:3, :] * dyx + w[1:2, :] * pltpu.roll(dyx, nx - 1, 0) + w[0:1, :] * pltpu.roll(dyx, nx - 2, 0))[:tt, :]
        dw_ref[0:1, :] += jnp.sum(dyc * h2[:tt, :], axis=0, keepdims=True)
        dw_ref[1:2, :] += jnp.sum(dyc * h1[:tt, :], axis=0, keepdims=True)
        dw_ref[2:3, :] += jnp.sum(dyc * hx[:tt, :], axis=0, keepdims=True)
        dp_ref[:, 0:D_CONV] = (dpx * yx)[:tt, :].astype(BF16)
        dp_ref[:, D_CONV:2 * D_CONV] = (dh * u_ref[...]).astype(BF16)
        dp_ref[:, 2 * D_CONV:3 * D_CONV] = (dh * c_ref[...]).astype(BF16)
        dp_ref[:, 3 * D_CONV:HALF_P] = jnp.zeros((tt, HALF_P - 3 * D_CONV), BF16)

    def col(cidx):
        return pl.BlockSpec((tt, D_CONV), lambda i: (i, cidx))

    def nxt(cidx):
        return pl.BlockSpec((8, D_CONV), lambda i: (jnp.minimum((i + 1) * t8, T // 8 - 1), cidx))

    def prev(cidx):
        return pl.BlockSpec((8, D_CONV), lambda i: (jnp.maximum(i * t8 - 1, 0), cidx))

    return pl.pallas_call(
        body,
        name="conv_bwd",
        grid=(nt,),
        in_specs=[col(0), col(1), col(2), col(0), nxt(0), nxt(1), nxt(2), nxt(0), prev(1), prev(2),
                  _full((8, D_CONV)), _full((1, D_CONV))] + _hbm_specs(len(deps)),
        out_specs=[pl.BlockSpec((tt, HALF_P), lambda i: (i, 0)), _full((8, D_CONV)), _full((1, D_CONV))],
        out_shape=[jax.ShapeDtypeStruct((T, P_INT), BF16), jax.ShapeDtypeStruct((8, D_CONV), F32),
                   jax.ShapeDtypeStruct((1, D_CONV), F32)],
        compiler_params=_params(1),
    )(proj, proj, proj, dyin, proj, proj, proj, dyin, proj, proj, conv_w8, conv_g, *deps)


def _gla_bwd(proj, wg128, gbias, gng, o_all, states, dyin, dproj):
    T = proj.shape[0]
    nch = T // CHUNK

    def body(p_ref, wg_ref, gb_ref, gn_ref, o_ref, st_ref, d_ref, dp_in_ref,
             dp_ref, dwg_ref, dgb_ref, dgn_ref, ds_ref):
        n = pl.program_id(0)

        @pl.when(n == 0)
        def _():
            ds_ref[...] = jnp.zeros_like(ds_ref)
            dwg_ref[...] = jnp.zeros_like(dwg_ref)
            dgb_ref[...] = jnp.zeros_like(dgb_ref)
            dgn_ref[...] = jnp.zeros_like(dgn_ref)

        blk = p_ref[...]
        q, k, zl, z, bcum, causal = _gla_chunk_terms(blk, wg_ref, gb_ref)
        v = blk[:, 1024:2048]
        r = blk[:, 2048:3072]
        gn = gn_ref[...]
        upper = (lax.broadcasted_iota(jnp.int32, (CHUNK, CHUNK), 0)
                 <= lax.broadcasted_iota(jnp.int32, (CHUNK, CHUNK), 1)).astype(F32)
        dlog_parts = []
        for h in range(GLA_HEADS):
            eb, enb, eend, dec, qd, ki, ke = _gla_head_terms(q, k, bcum, h)
            vs = slice(h * HEAD_V, (h + 1) * HEAD_V)
            ks = slice(h * HEAD_K, (h + 1) * HEAD_K)
            o = o_ref[:, vs]
            rh = r[:, vs]
            dyg = d_ref[:, vs]
            rinv = lax.rsqrt(jnp.mean(o * o, axis=-1, keepdims=True) + RMS_EPS)
            sg = _sigmoid(rh)
            on = o * rinv
            dr = dyg * (on * gn[:, vs]) * (sg * (1.0 + rh * (1.0 - sg)))
            don = dyg * (rh * sg)
            dgn_ref[:, vs] += jnp.sum(don * on, axis=0, keepdims=True)
            t = don * gn[:, vs]
            do = rinv * t - o * (rinv * rinv * rinv) * jnp.mean(o * t, axis=-1, keepdims=True)
            dob = do.astype(BF16)
            vb = v[:, vs].astype(BF16)
            qdb, kib, keb = qd.astype(BF16), ki.astype(BF16), ke.astype(BF16)
            a = jnp.where(causal, _dot(qdb, kib, NT), 0.0)
            st = st_ref[h]
            dst = ds_ref[h]
            dstb = dst.astype(BF16)
            da = jnp.where(causal, _dot(dob, vb, NT), 0.0)
            dab = da.astype(BF16)
            dv = _dot(a.astype(BF16), dob, TN) + _dot(keb, dstb, NT)
            dqd = _dot(dab, kib, NN) + _dot(dob, st.astype(BF16), NN)
            dki = _dot(dab, qdb, TN)
            dke = _dot(vb, dstb, NN)
            ddec = jnp.sum(st * dst, axis=0, keepdims=True)
            ds_ref[h] = dec * dst + _dot(dob, qdb, TN)
            dq = dqd * eb * (HEAD_K ** -0.5)
            dk = dki * enb + dke * eend
            db = dqd * qd - dki * ki - dke * ke
            dbl = jnp.sum(dke * ke, axis=0, keepdims=True) + dec * ddec
            dlog_parts.append(_dot(upper, db, NN, precision=lax.Precision.HIGHEST) + dbl)
            dp_ref[:, ks] = dq.astype(BF16)
            dp_ref[:, D_GLA_K + h * HEAD_K:D_GLA_K + (h + 1) * HEAD_K] = dk.astype(BF16)
            dp_ref[:, 1024 + h * HEAD_V:1024 + (h + 1) * HEAD_V] = dv.astype(BF16)
            dp_ref[:, 2048 + h * HEAD_V:2048 + (h + 1) * HEAD_V] = dr.astype(BF16)
        dlog = jnp.concatenate(dlog_parts, axis=1)
        dz = dlog * (1.0 / GATE_TAU) * (1.0 / (1.0 + jnp.exp(z)))
        dzb = dz.astype(BF16)
        dp_ref[:, 3072:3200] = _dot(dzb, wg_ref[...], NT).astype(BF16)
        dwg_ref[...] += _dot(zl.astype(BF16), dzb, TN)
        dgb_ref[...] += jnp.sum(dz, axis=0, keepdims=True)

    rev = lambda n: nch - 1 - n
    return pl.pallas_call(
        body,
        name="gla_bwd",
        grid=(nch,),
        in_specs=[pl.BlockSpec((CHUNK, HALF_P), lambda n: (rev(n), 1)), _full((LANE, D_GLA_K)), _full((1, D_GLA_K)),
                  _full((1, D_GLA_V)), pl.BlockSpec((CHUNK, D_GLA_V), lambda n: (rev(n), 0)),
                  pl.BlockSpec((None, GLA_HEADS, HEAD_V, HEAD_K), lambda n: (rev(n), 0, 0, 0)),
                  pl.BlockSpec((CHUNK, D_GLA_V), lambda n: (rev(n), 1)), pl.BlockSpec(memory_space=pl.ANY)],
        out_specs=[pl.BlockSpec((CHUNK, HALF_P), lambda n: (rev(n), 1)), _full((LANE, D_GLA_K)),
                   _full((1, D_GLA_K)), _full((1, D_GLA_V))],
        out_shape=[jax.ShapeDtypeStruct(dproj.shape, BF16), jax.ShapeDtypeStruct((LANE, D_GLA_K), F32),
                   jax.ShapeDtypeStruct((1, D_GLA_K), F32), jax.ShapeDtypeStruct((1, D_GLA_V), F32)],
        scratch_shapes=[pltpu.VMEM((GLA_HEADS, HEAD_V, HEAD_K), F32)],
        input_output_aliases={7: 0},
        compiler_params=_params(1),
    )(proj, wg128, gbias, gng, o_all, states, dyin, dproj)


def _proj_bwd_x(dproj, w_full, dh1, deps=()):
    T, P = dproj.shape
    D = w_full.shape[0]
    tm, tk = _tile(T, 512), _tile(P, 1280)

    def ep(acc_ref, ex, o, i, j):
        o[0][...] = DN_ALPHA * ex[0][...] + acc_ref[...]

    row = pl.BlockSpec((tm, D), lambda i, j, k: (i, 0))
    return _mm("proj_bwd_x", "nt", dproj, w_full, M=T, N=D, K=P, tm=tm, tn=D, tk=tk,
               outs=[(jax.ShapeDtypeStruct((T, D), F32), row)], extras=[(dh1, row)], epilogue=ep, deps=deps)[0]


def _place():
    x, y, c = lax.axis_index("x"), lax.axis_index("y"), lax.axis_index("c")
    chips = [(1 - x, y), (x, 1 - y), (1 - x, 1 - y)]
    return x, y, c, chips


def _rcopy(src, dst, ssem, rsem, dev):
    return pltpu.make_async_remote_copy(src_ref=src, dst_ref=dst, send_sem=ssem, recv_sem=rsem,
                                        device_id=dev, device_id_type=MESH)


def _all_gather(name, shards, deps=()):
    n = len(shards)

    def body(*refs):
        ins, outs = refs[:n], refs[n + len(deps):2 * n + len(deps)]
        ssem, rsem, lsem = refs[2 * n + len(deps):]
        x, y, c, chips = _place()
        me, sib = (x, y, c), (x, y, 1 - c)

        def slot(w, px, py, pc):
            return outs[w].at[4 * px + 2 * py + pc]

        started = []
        for w in range(n):
            lc = pltpu.make_async_copy(ins[w], slot(w, *me), lsem.at[w])
            lc.start()
            started.append(lc)
        sends = []
        for w in range(n):
            cp = _rcopy(ins[w], slot(w, *me), ssem.at[7 * w], rsem.at[7 * w], sib)
            cp.start()
            sends.append(cp)
            for jx, chip in enumerate(chips):
                cp = _rcopy(ins[w], slot(w, *me), ssem.at[7 * w + 1 + jx], rsem.at[7 * w + 1 + jx], (*chip, c))
                cp.start()
                sends.append(cp)
        for w in range(n):
            for jx, chip in enumerate(chips):
                blk = slot(w, *chip, c)
                _rcopy(blk, blk, ssem.at[7 * w + 1 + jx], rsem.at[7 * w + 1 + jx], me).wait_recv()
                cp = _rcopy(blk, blk, ssem.at[7 * w + 4 + jx], rsem.at[7 * w + 4 + jx], sib)
                cp.start()
                sends.append(cp)
        for w in range(n):
            blk = slot(w, x, y, 1 - c)
            _rcopy(blk, blk, ssem.at[7 * w], rsem.at[7 * w], me).wait_recv()
            for jx, chip in enumerate(chips):
                blk = slot(w, *chip, 1 - c)
                _rcopy(blk, blk, ssem.at[7 * w + 4 + jx], rsem.at[7 * w + 4 + jx], me).wait_recv()
        for cp in sends:
            cp.wait_send()
        for lc in started:
            lc.wait()

    return pl.pallas_call(
        body,
        name=name,
        in_specs=_hbm_specs(n + len(deps)),
        out_specs=_hbm_specs(n),
        out_shape=[jax.ShapeDtypeStruct((N_DEV,) + s.shape, s.dtype) for s in shards],
        scratch_shapes=[pltpu.SemaphoreType.DMA((7 * n,)), pltpu.SemaphoreType.DMA((7 * n,)),
                        pltpu.SemaphoreType.DMA((n,))],
    )(*shards, *deps)


HBM_SPEC = pl.BlockSpec(memory_space=pltpu.HBM)
SEM_SPEC = pl.BlockSpec(memory_space=pltpu.SEMAPHORE)
SIDE_EFFECT = pltpu.SideEffectType.DATAFLOW_SIDE_EFFECTING


def _cast_place(name, ids, w, deps=(), dtype=BF16):
    R, C = w.shape
    tr = _tile(R, 256)

    def body(ids_ref, w_ref, *rest):
        rest[len(deps)][...] = w_ref[...].astype(dtype)

    return pl.pallas_call(
        body,
        name=name,
        grid_spec=pltpu.PrefetchScalarGridSpec(
            num_scalar_prefetch=1,
            grid=(R // tr,),
            in_specs=[pl.BlockSpec((tr, C), lambda r, ids: (r, 0))] + _hbm_specs(len(deps)),
            out_specs=pl.BlockSpec((None, tr, C), lambda r, ids: (ids[0], r, 0)),
        ),
        out_shape=jax.ShapeDtypeStruct((N_DEV, R, C), dtype),
        compiler_params=_params(1),
    )(ids, w, *deps)


def _xfer_start(name, bufs, plan, n):
    nb = len(bufs)

    def body(*refs):
        ins = refs[:nb]
        ssem, rsem = refs[nb], refs[nb + 1]
        token = refs[2 * nb + 2]
        x, y, c, chips = _place()
        for k, (src, dst, dev, _) in enumerate(plan(ins, x, y, c, chips)):
            _rcopy(src, dst, ssem.at[k], rsem.at[k], dev).start()
        token[...] = jnp.zeros_like(token)

    res = pl.pallas_call(
        body,
        name=name,
        out_shape=(pltpu.SemaphoreType.DMA((n,)), pltpu.SemaphoreType.DMA((n,)),
                   *[pltpu.HBM(b.shape, b.dtype) for b in bufs], jax.ShapeDtypeStruct((8, LANE), F32)),
        in_specs=[HBM_SPEC] * nb,
        out_specs=(SEM_SPEC, SEM_SPEC, *[HBM_SPEC] * nb, pl.BlockSpec(memory_space=pltpu.VMEM)),
        input_output_aliases={i: 2 + i for i in range(nb)},
        compiler_params=pltpu.CompilerParams(has_side_effects=SIDE_EFFECT),
    )(*[pltpu.with_memory_space_constraint(b, pltpu.HBM) for b in bufs])
    return dict(sems=res[:2], bufs=list(res[2:2 + nb]), token=res[2 + nb], plan=plan, n=n)


def _xfer_wait(name, started, after):
    bufs, plan = started["bufs"], started["plan"]
    nb = len(bufs)

    def body(*refs):
        ins = refs[:nb]
        ssem, rsem = refs[nb], refs[nb + 1]
        x, y, c, chips = _place()
        for k, (src, _, dev, land) in enumerate(plan(ins, x, y, c, chips)):
            cp = _rcopy(src, land, ssem.at[k], rsem.at[k], dev)
            cp.wait_send()
            cp.wait_recv()

    res = pl.pallas_call(
        body,
        name=name,
        out_shape=tuple(pltpu.HBM(b.shape, b.dtype) for b in bufs),
        in_specs=[HBM_SPEC] * nb + [SEM_SPEC, SEM_SPEC, pl.BlockSpec(memory_space=pl.ANY)],
        out_specs=tuple([HBM_SPEC] * nb),
        input_output_aliases={i: i for i in range(nb)},
        compiler_params=pltpu.CompilerParams(has_side_effects=SIDE_EFFECT),
    )(*bufs, *started["sems"], after)
    return list(res)


def _plan_gather_chips(refs, x, y, c, chips):
    (land,) = refs
    mine = land.at[4 * x + 2 * y + c]
    plan = [(mine, mine, (x, y, 1 - c), land.at[4 * x + 2 * y + (1 - c)])]
    for px, py in chips:
        plan.append((mine, mine, (px, py, c), land.at[4 * px + 2 * py + c]))
    return plan


def _plan_gather_pass(refs, x, y, c, chips):
    (land,) = refs
    return [(land.at[4 * px + 2 * py + c], land.at[4 * px + 2 * py + c], (x, y, 1 - c),
             land.at[4 * px + 2 * py + (1 - c)]) for px, py in chips]


def _plan_reduce_core(refs, x, y, c, chips):
    grad, recv = refs
    return [(grad.at[2 * q + (1 - c)], recv.at[q], (x, y, 1 - c), recv.at[q]) for q in range(N_CHIP)]


def _plan_reduce_chips(refs, x, y, c, chips):
    part, land = refs
    return [(part.at[2 * px + py], land.at[2 * x + y], (px, py, c), land.at[2 * px + py]) for px, py in chips]


def _chip_sums(name, ids, grad, recv):
    _, R, C = grad.shape
    tr = _tile(R, 256)

    def body(ids_ref, g_ref, r_ref, o_ref):
        o_ref[...] = (g_ref[...] + r_ref[...]).astype(BF16)

    return pl.pallas_call(
        body,
        name=name,
        grid_spec=pltpu.PrefetchScalarGridSpec(
            num_scalar_prefetch=1,
            grid=(N_CHIP - 1, R // tr),
            in_specs=[pl.BlockSpec((None, tr, C), lambda q, r, ids: (2 * ids[3 + q] + ids[2], r, 0)),
                      pl.BlockSpec((None, tr, C), lambda q, r, ids: (ids[3 + q], r, 0))],
            out_specs=pl.BlockSpec((None, tr, C), lambda q, r, ids: (ids[3 + q], r, 0)),
        ),
        out_shape=jax.ShapeDtypeStruct((N_CHIP, R, C), BF16),
        compiler_params=_params(2),
    )(ids, grad, recv)


def _adamw(w, g, m, v):
    m = ADAM_B1 * m + (1.0 - ADAM_B1) * g
    v = ADAM_B2 * v + (1.0 - ADAM_B2) * (g * g)
    m_hat = m / (1.0 - ADAM_B1 ** ADAM_STEP)
    v_hat = v / (1.0 - ADAM_B2 ** ADAM_STEP)
    delta = -ADAM_LR * (m_hat / (jnp.sqrt(v_hat) + ADAM_EPS) + ADAM_WD * w)
    return delta, m, v


def _reduce_adamw(name, ids, grad, recv, landed, w, m, v):
    _, R, C = grad.shape
    tr = _tile(R, 256)

    def body(ids_ref, g_ref, r_ref, l1_ref, l2_ref, l3_ref, w_ref, m_ref, v_ref, go_ref, do_ref, mo_ref, vo_ref):
        g = g_ref[...] + r_ref[...]
        g = g + l1_ref[...].astype(F32)
        g = g + l2_ref[...].astype(F32)
        g = g + l3_ref[...].astype(F32)
        delta, mn, vn = _adamw(w_ref[...], g, m_ref[...], v_ref[...])
        go_ref[...] = g
        do_ref[...] = delta
        mo_ref[...] = mn
        vo_ref[...] = vn

    def pick(k):
        return pl.BlockSpec((None, tr, C), lambda r, ids: (ids[k], r, 0))

    flat = pl.BlockSpec((tr, C), lambda r, ids: (r, 0))
    shp = jax.ShapeDtypeStruct((R, C), F32)
    return pl.pallas_call(
        body,
        name=name,
        grid_spec=pltpu.PrefetchScalarGridSpec(
            num_scalar_prefetch=1,
            grid=(R // tr,),
            in_specs=[pick(0), pick(1), pick(3), pick(4), pick(5), flat, flat, flat],
            out_specs=[flat, flat, flat, flat],
        ),
        out_shape=[shp, shp, shp, shp],
        compiler_params=_params(1),
    )(ids, grad, recv, landed, landed, landed, w, m, v)


def _small_adamw(packs, w, m, v):
    def body(p_ref, w_ref, m_ref, v_ref, g_ref, d_ref, mo_ref, vo_ref):
        g = p_ref[0]
        for dvc in range(1, N_DEV):
            g = g + p_ref[dvc]
        delta, mn, vn = _adamw(w_ref[...], g, m_ref[...], v_ref[...])
        g_ref[...] = g
        d_ref[...] = delta
        mo_ref[...] = mn
        vo_ref[...] = vn

    shp = jax.ShapeDtypeStruct(w.shape, F32)
    return pl.pallas_call(
        body,
        name="small_adamw",
        in_specs=[_full(packs.shape), _full(w.shape), _full(w.shape), _full(w.shape)],
        out_specs=[_full(w.shape)] * 4,
        out_shape=[shp] * 4,
        grid=(1,),
        compiler_params=_params(1),
    )(packs, w, m, v)


def _w_in_pieces():
    cs = D_IN_PROJ // N_DEV
    pieces = []
    for d in range(N_DEV):
        lo, hi = d * cs, (d + 1) * cs
        if hi <= CONV_COLS:
            pieces.append((d, 0, cs, lo))
        elif lo >= CONV_COLS:
            pieces.append((d, 0, cs, lo - CONV_COLS + HALF_P))
        else:
            pieces.append((d, 0, CONV_COLS - lo, lo))
            pieces.append((d, CONV_COLS - lo, cs, HALF_P))
    return pieces


def _w_in_full(gathered):
    nb, D, cs = gathered.shape
    tr = _tile(D, 256)

    def body(g_ref, o_ref):
        o_ref[:, CONV_COLS:HALF_P] = jnp.zeros((tr, HALF_P - CONV_COLS), o_ref.dtype)
        o_ref[:, HALF_P + GLA_COLS:P_INT] = jnp.zeros((tr, HALF_P - GLA_COLS), o_ref.dtype)
        for d, a, b, dst in _w_in_pieces():
            o_ref[:, dst:dst + (b - a)] = g_ref[d, :, a:b]

    return pl.pallas_call(
        body,
        name="w_in_full",
        grid=(D // tr,),
        in_specs=[pl.BlockSpec((nb, tr, cs), lambda r: (0, r, 0))],
        out_specs=pl.BlockSpec((tr, P_INT), lambda r: (r, 0)),
        out_shape=jax.ShapeDtypeStruct((D, P_INT), gathered.dtype),
        compiler_params=_params(1),
    )(gathered)


def _w_in_blocks(dw):
    D = dw.shape[0]
    cs = D_IN_PROJ // N_DEV
    tr = _tile(D, 256)

    def body(w_ref, o_ref):
        for d, a, b, src in _w_in_pieces():
            o_ref[d, :, a:b] = w_ref[:, src:src + (b - a)]

    return pl.pallas_call(
        body,
        name="w_in_blocks",
        grid=(D // tr,),
        in_specs=[pl.BlockSpec((tr, P_INT), lambda r: (r, 0))],
        out_specs=pl.BlockSpec((N_DEV, tr, cs), lambda r: (0, r, 0)),
        out_shape=jax.ShapeDtypeStruct((N_DEV, D, cs), dw.dtype),
        compiler_params=_params(1),
    )(dw)


def _rows(vec, n_rows):
    flat = jnp.pad(vec.reshape(-1), (0, n_rows * SP_COLS - vec.size))
    return flat.reshape(n_rows, SP_COLS)


def _pad_cols(a):
    return jnp.pad(a, ((0, 0), (0, SP_COLS - a.shape[1])))


R_CONV_W, R_CONV_G, R_GATE_B, R_GLA_G, R_LN1_G, R_LN1_B, R_LN2_G, R_LN2_B, R_LOSS, R_GATE_W = 0, 3, 4, 5, 6, 8, 10, 12, 14, 16


def _pack(conv_w, conv_g, gate_b, gla_g, ln1_g, ln1_b, ln2_g, ln2_b, loss, gate_w):
    z = jnp.zeros((1, SP_COLS), F32)
    parts = [_pad_cols(conv_w), _pad_cols(conv_g), _pad_cols(gate_b), _pad_cols(gla_g),
             _rows(ln1_g, 2), _rows(ln1_b, 2), _rows(ln2_g, 2), _rows(ln2_b, 2),
             z if loss is None else _pad_cols(jnp.sum(loss, axis=1, keepdims=True)), z, _pad_cols(gate_w)]
    return jnp.concatenate(parts, axis=0)


def _unpack(p, D, conv_cols, gate_cols):
    return dict(
        conv_w=p[R_CONV_W:R_CONV_W + 3, :conv_cols], conv_norm_g=p[R_CONV_G:R_CONV_G + 1, :D_CONV],
        gate_bias=p[R_GATE_B:R_GATE_B + 1, :D_GLA_K], gla_norm_g=p[R_GLA_G:R_GLA_G + 1, :D_GLA_V],
        ln1_g=p[R_LN1_G:R_LN1_G + 2].reshape(1, -1)[:, :D], ln1_b=p[R_LN1_B:R_LN1_B + 2].reshape(1, -1)[:, :D],
        ln2_g=p[R_LN2_G:R_LN2_G + 2].reshape(1, -1)[:, :D], ln2_b=p[R_LN2_B:R_LN2_B + 2].reshape(1, -1)[:, :D],
        w_gate_up=p[R_GATE_W:R_GATE_W + GATE_RANK, :gate_cols])


BIG = ("w_in", "w_out", "w_ff_up", "w_ff_down")
ORDER = ("w_in", "conv_w", "conv_norm_g", "w_gate_up", "gate_bias", "gla_norm_g", "w_out", "ln1_g", "ln1_b",
         "w_ff_up", "w_ff_down", "ln2_g", "ln2_b")


def kernel(x, w_in, conv_w, conv_norm_g, w_gate_up, gate_bias, gla_norm_g, w_out, ln1_g, ln1_b, w_ff_up, w_ff_down, ln2_g, ln2_b, loss_target, m_w_in, m_conv_w, m_conv_norm_g, m_w_gate_up, m_gate_bias, m_gla_norm_g, m_w_out, m_ln1_g, m_ln1_b, m_w_ff_up, m_w_ff_down, m_ln2_g, m_ln2_b, v_w_in, v_conv_w, v_conv_norm_g, v_w_gate_up, v_gate_bias, v_gla_norm_g, v_w_out, v_ln1_g, v_ln1_b, v_w_ff_up, v_w_ff_down, v_ln2_g, v_ln2_b):
    T, D = x.shape[1], x.shape[2]
    xs, target = x[0], loss_target[0]
    xi, yi, ci = lax.axis_index("x"), lax.axis_index("y"), lax.axis_index("c")
    chip = 2 * xi + yi
    dev = 2 * chip + ci
    others = [jnp.where(chip <= q, q + 1, q) for q in range(N_CHIP - 1)]
    ids = jnp.stack([dev, chip, ci] + others).astype(jnp.int32)
    conv_cols, gate_cols = conv_w.shape[2], w_gate_up.shape[2]

    def gather(nm, w, deps, dtype=BF16):
        return _xfer_start("gather_chips_" + nm, [_cast_place("cast_place_" + nm, ids, w, deps, dtype)],
                           _plan_gather_chips, 4)

    def pass_on(nm, started, after):
        (land,) = _xfer_wait("gather_chips_wait_" + nm, started, after)
        return _xfer_start("gather_pass_" + nm, [land], _plan_gather_pass, 3)

    def landed(nm, started, after):
        return _xfer_wait("gather_pass_wait_" + nm, started, after)[0]

    z1 = jnp.zeros((1, 1), F32)
    fwd_pack = _pack(conv_w[0], z1, z1, z1, z1, z1, z1, z1, None, w_gate_up[0])
    ga_in = gather("w_in", w_in[0], ())
    ga_pack = gather("pack", fwd_pack, [ga_in["token"]], F32)
    ga, dep = [], ga_pack["token"]
    m_in, v_in = m_w_in[0], v_w_in[0]
    for nm, w in zip(BIG[1:], (w_out, w_ff_up, w_ff_down)):
        ga.append(gather(nm, w[0], [dep, m_in, v_in] if nm == "w_ff_down" else [dep]))
        dep = ga[-1]["token"]
    gp_in = pass_on("w_in", ga_in, dep)
    gp_pack = pass_on("pack", ga_pack, gp_in["token"])
    g_in = landed("w_in", gp_in, gp_pack["token"])
    w_full = _w_in_full(g_in)
    g_pack = landed("pack", gp_pack, w_full)
    conv_w_full = g_pack[:, R_CONV_W:R_CONV_W + 3, :conv_cols].transpose(1, 0, 2).reshape(3, -1)
    gate_w_full = g_pack[:, R_GATE_W:R_GATE_W + GATE_RANK, :gate_cols].transpose(1, 0, 2).reshape(GATE_RANK, -1)
    conv_w8 = jnp.pad(conv_w_full, ((0, 5), (0, 0)))
    wg128 = jnp.pad(gate_w_full, ((0, LANE - GATE_RANK), (0, 0))).astype(BF16)
    proj = _proj_fwd(xs, w_full, deps=[g_pack])
    yin = _conv_fwd(proj, conv_w8, conv_norm_g)
    gp_out = pass_on("w_out", ga[0], yin)
    o_all, states, yin = _gla_fwd(proj, wg128, gate_bias, gla_norm_g, yin, deps=[gp_out["token"]])
    w_out_full = landed("w_out", gp_out, o_all).reshape(-1, D)
    gp_up = pass_on("w_ff_up", ga[1], o_all)
    xhat1, x1, rstd1 = _mix_ln1(yin, w_out_full, xs, ln1_g, ln1_b, deps=[gp_up["token"]])
    w_up_blk = landed("w_ff_up", gp_up, x1)
    half = N_DEV // 2
    ra, h2 = _ff_up(x1, w_up_blk, 0, half)
    gp_down = pass_on("w_ff_down", ga[2], ra)
    ra, h2 = _ff_up(x1, w_up_blk, half, N_DEV - half, prev=(ra, h2), deps=[gp_down["token"]])
    w_down_full = landed("w_ff_down", gp_down, ra).reshape(-1, D)
    dh3, dh3b, g_ln2_g, g_ln2_b, loss = _ff_down_loss(h2, w_down_full, xhat1, target, ln1_g, ln1_b, ln2_g, ln2_b)

    def to_core(nm, grad):
        recv = lax.empty((N_CHIP,) + grad.shape[1:], F32)
        return _xfer_start("reduce_core_" + nm, [grad, recv], _plan_reduce_core, N_CHIP)

    def to_chips(nm, started, after):
        grad, recv = _xfer_wait("reduce_core_wait_" + nm, started, after)
        part = _chip_sums("chip_sums_" + nm, ids, grad, recv)
        land = lax.empty(part.shape, BF16)
        return grad, recv, _xfer_start("reduce_chips_" + nm, [part, land], _plan_reduce_chips, N_CHIP - 1)

    da = _ff_down_bwd_act(dh3b, w_down_full, ra)
    gw_down = _grad_w("grad_w_down", h2, dh3b).reshape(N_DEV, -1, D)
    rc_down = to_core("w_ff_down", gw_down)
    gw_up = _grad_w_up_blk(x1, da, N_DEV, deps=[rc_down["token"]])
    gw_down, rv_down, rs_down = to_chips("w_ff_down", rc_down, gw_up)
    rc_up = to_core("w_ff_up", gw_up)
    dh1, dh1b, g_ln1_g, g_ln1_b = _ff_up_bwd_ln1(da, w_up_blk, dh3, xhat1, rstd1, ln1_g,
                                                 deps=[rs_down["token"], rc_up["token"]])
    gw_up, rv_up, rs_up = to_chips("w_ff_up", rc_up, dh1b)
    dyin = _mix_bwd(dh1b, w_out_full, deps=[rs_up["token"]])
    gw_out = _grad_w("grad_w_out", yin, dh1b).reshape(N_DEV, -1, D)
    rc_out = to_core("w_out", gw_out)
    dproj, g_conv_w, g_conv_g = _conv_bwd(proj, dyin, conv_w8, conv_norm_g, deps=[rc_out["token"]])
    dproj, g_gate_w, g_gate_b, g_gla_g = _gla_bwd(proj, wg128, gate_bias, gla_norm_g, o_all, states, dyin, dproj)
    gw_out, rv_out, rs_out = to_chips("w_out", rc_out, dproj)
    gw_in = _w_in_blocks(_grad_w("grad_w_in", xs, dproj, a_fn=_to_bf16, tn_pref=1280, tk_pref=2048, deps=[rs_out["token"]]))
    rc_in = to_core("w_in", gw_in)

    big = {}

    def finish(nm, grad, recv, started, w, m, v, after):
        _, land = _xfer_wait("reduce_chips_wait_" + nm, started, after)
        res = _reduce_adamw("adamw_" + nm, ids, grad, recv, land, w[0], m[0], v[0])
        big[nm] = [a[None] for a in res]
        return res[0]

    done = finish("w_ff_down", gw_down, rv_down, rs_down, w_ff_down, m_w_ff_down, v_w_ff_down, rc_in["token"])
    done = finish("w_ff_up", gw_up, rv_up, rs_up, w_ff_up, m_w_ff_up, v_w_ff_up, done)
    gw_in, rv_in, rs_in = to_chips("w_in", rc_in, done)
    grad_x = _proj_bwd_x(dproj, w_full, dh1, deps=[rs_in["token"]])

    pack = _pack(g_conv_w[:3], g_conv_g, g_gate_b, g_gla_g, g_ln1_g, g_ln1_b, g_ln2_g, g_ln2_b, loss,
                 g_gate_w[:GATE_RANK])
    (packs,) = _all_gather("gather_small_grads", [pack], deps=[grad_x])
    done = finish("w_out", gw_out, rv_out, rs_out, w_out, m_w_out, v_w_out, packs)
    finish("w_in", gw_in, rv_in, rs_in, w_in, (m_in,), (v_in,), done)

    def own_cols(row, n_rows, width):
        cut = lax.dynamic_slice(packs, (0, row, dev * width), (N_DEV, n_rows, width))
        return jnp.pad(cut, ((0, 0), (0, 0), (0, SP_COLS - width)))

    packs_own = jnp.concatenate([own_cols(R_CONV_W, 3, conv_cols), packs[:, R_CONV_W + 3:R_GATE_W],
                                 own_cols(R_GATE_W, GATE_RANK, gate_cols)], axis=1)

    def small_pack(cw, cg, gw, gb, gg, l1g, l1b, l2g, l2b):
        return _pack(cw[0], cg, gb, gg, l1g, l1b, l2g, l2b, None, gw[0])

    w_s = small_pack(conv_w, conv_norm_g, w_gate_up, gate_bias, gla_norm_g, ln1_g, ln1_b, ln2_g, ln2_b)
    m_s = small_pack(m_conv_w, m_conv_norm_g, m_w_gate_up, m_gate_bias, m_gla_norm_g, m_ln1_g, m_ln1_b, m_ln2_g, m_ln2_b)
    v_s = small_pack(v_conv_w, v_conv_norm_g, v_w_gate_up, v_gate_bias, v_gla_norm_g, v_ln1_g, v_ln1_b, v_ln2_g, v_ln2_b)
    g_s, d_s, mn_s, vn_s = _small_adamw(packs_own, w_s, m_s, v_s)
    small = [_unpack(p, D, conv_cols, gate_cols) for p in (g_s, d_s, mn_s, vn_s)]

    def leaf(kind, name):
        if name in BIG:
            return big[name][kind]
        a = small[kind][name]
        return a[None] if name in ("conv_w", "w_gate_up") else a

    out = [g_s[R_LOSS, 0], grad_x[None]]
    for kind in range(4):
        out += [leaf(kind, nm) for nm in ORDER]
    return tuple(out)
```

```python
import jax
import jax.numpy as jnp
from jax import lax
from jax.experimental import pallas as pl
from jax.experimental.pallas import tpu as pltpu

F32 = jnp.float32
BF16 = jnp.bfloat16

D_CONV = 1024
CONV_GROUPS = 8
GLA_HEADS = 4
HEAD_K = 128
HEAD_V = 256
D_GLA_K = 512
D_GLA_V = 1024
GATE_RANK = 16
GATE_TAU = 16.0
CHUNK = 64
LN_EPS = 1e-5
RMS_EPS = 1e-6
DN_ALPHA = 2.0 ** 0.25
D_IN_PROJ = 6160
ADAM_LR = 0.001
ADAM_B1 = 0.9
ADAM_B2 = 0.999
ADAM_EPS = 1e-08
ADAM_WD = 0.01
ADAM_STEP = 10

N_DEV = 8
N_CHIP = 4
LANE = 128
HALF_P = 3200
P_INT = 2 * HALF_P
CONV_COLS = 3 * D_CONV
GLA_COLS = D_IN_PROJ - CONV_COLS
SP_ROWS = 32
SP_COLS = 1024
VMEM_LIMIT = 56 * 1024 * 1024

NN = ((1,), (0,))
NT = ((1,), (1,))
TN = ((0,), (0,))
MESH = pl.DeviceIdType.MESH


def _dot(a, b, dims, precision=None):
    return lax.dot_general(a, b, (dims, ((), ())), preferred_element_type=F32, precision=precision)


def _tile(n, pref):
    if n <= pref:
        return n
    t = (pref // LANE) * LANE
    while t > 0 and n % t:
        t -= LANE
    assert t > 0, (n, pref)
    return t


def _params(n_axes):
    return pltpu.CompilerParams(dimension_semantics=("arbitrary",) * n_axes, vmem_limit_bytes=VMEM_LIMIT)


def _full(shape):
    nd = len(shape)
    return pl.BlockSpec(shape, lambda *_: (0,) * nd)


def _hbm_specs(n):
    return [pl.BlockSpec(memory_space=pl.ANY)] * n


def _mm(name, mode, a, b, *, M, N, K, tm, tn, tk, outs, epilogue, extras=(), a_fn=None, a_spec=None, b_spec=None,
        deps=()):
    ni, nj, nk = M // tm, N // tn, K // tk
    assert ni * tm == M and nj * tn == N and nk * tk == K, (name, M, N, K, tm, tn, tk)
    if a_spec is None:
        a_spec = (pl.BlockSpec((tk, tm), lambda i, j, k: (k, i)) if mode == "tn"
                  else pl.BlockSpec((tm, tk), lambda i, j, k: (i, k)))
    if b_spec is None:
        b_spec = (pl.BlockSpec((tn, tk), lambda i, j, k: (j, k)) if mode == "nt"
                  else pl.BlockSpec((tk, tn), lambda i, j, k: (k, j)))
    dims = {"nn": NN, "nt": NT, "tn": TN}[mode]
    n_ex, n_out, n_dep = len(extras), len(outs), len(deps)

    def body(*refs):
        a_ref, b_ref = refs[0], refs[1]
        ex = refs[2:2 + n_ex]
        o = refs[2 + n_ex + n_dep:2 + n_ex + n_dep + n_out]
        acc_ref = refs[2 + n_ex + n_dep + n_out]
        i, j, k = pl.program_id(0), pl.program_id(1), pl.program_id(2)
        if nk > 1:
            @pl.when(k == 0)
            def _():
                acc_ref[...] = jnp.zeros_like(acc_ref)

        av = a_ref[...]
        if a_fn is not None:
            av = a_fn(av)
        part = _dot(av, b_ref[...], dims)
        if nk == 1 and epilogue is None:
            o[0][...] = part.astype(o[0].dtype)
        elif nk == 1:
            acc_ref[...] = part
            epilogue(acc_ref, ex, o, i, j)
        else:
            acc_ref[...] += part

            @pl.when(k == nk - 1)
            def _():
                if epilogue is None:
                    o[0][...] = acc_ref[...].astype(o[0].dtype)
                else:
                    epilogue(acc_ref, ex, o, i, j)

    return pl.pallas_call(
        body,
        name=name,
        grid=(ni, nj, nk),
        in_specs=[a_spec, b_spec] + [s for _, s in extras] + _hbm_specs(n_dep),
        out_specs=[s for _, s in outs],
        out_shape=[s for s, _ in outs],
        scratch_shapes=[pltpu.VMEM((8, LANE) if nk == 1 and epilogue is None else (tm, tn), F32)],
        compiler_params=_params(3),
    )(a, b, *[x for x, _ in extras], *deps)


def _mm_rows(name, mode, a, b, *, M, N, K, tm, tk, row_ins, vec_ins, row_outs, stat_outs, chunk_fn,
             b_spec=None, deps=()):
    ni, nk = M // tm, K // tk
    rc = tm // nk
    assert ni * tm == M and nk * tk == K and rc * nk == tm and rc % 16 == 0, (name, M, K, tm, tk)
    dims = {"nn": NN, "nt": NT}[mode]
    last = ni - 1
    n_split = 2 if N % (2 * 256) == 0 else 1

    def kk(i, k):
        return jnp.where(i < ni, k, nk - 1)

    a_spec = pl.BlockSpec((tm, tk), lambda i, k: (jnp.minimum(i, last), kk(i, k)))
    if b_spec is None:
        b_spec = (pl.BlockSpec((N, tk), lambda i, k: (0, kk(i, k))) if mode == "nt"
                  else pl.BlockSpec((tk, N), lambda i, k: (kk(i, k), 0)))
    prev_rows = lambda i, k: (jnp.maximum((i - 1) * nk + k, 0), 0)
    n_ri, n_vi, n_ro, n_so, n_dep = len(row_ins), len(vec_ins), len(row_outs), len(stat_outs), len(deps)

    def body(*refs):
        a_ref, b_ref = refs[0], refs[1]
        pos = 2
        ri = refs[pos:pos + n_ri]; pos += n_ri
        vi = refs[pos:pos + n_vi]; pos += n_vi + n_dep
        ro = refs[pos:pos + n_ro]; pos += n_ro
        so = refs[pos:pos + n_so]; pos += n_so
        accs = refs[pos:pos + 2]
        i, k = pl.program_id(0), pl.program_id(1)

        @pl.when((i == 0) & (k == 0))
        def _():
            accs[0][...] = jnp.zeros_like(accs[0])
            accs[1][...] = jnp.zeros_like(accs[1])
            for st in so:
                st[...] = jnp.zeros_like(st)

        def finish_rows(prev_ref):
            rows = pl.ds(pl.multiple_of(k * rc, rc), rc)
            done = prev_ref[rows, :]
            prev_ref[rows, :] = jnp.zeros((rc, N), F32)
            chunk_fn(done, i > 0, ri, vi, ro, so)

        def accumulate(acc_ref):
            av = a_ref[...]
            for c0 in range(0, N, N // n_split):
                cols = slice(c0, c0 + N // n_split)
                bv = b_ref[cols, :] if mode == "nt" else b_ref[:, cols]
                acc_ref[:, cols] += _dot(av, bv, dims)

        for parity in (0, 1):
            @pl.when((i < ni) & (lax.rem(i, 2) == parity))
            def _(parity=parity):
                finish_rows(accs[1 - parity])
                accumulate(accs[parity])

        @pl.when(i == ni)
        def _():
            finish_rows(accs[last % 2])

    row_spec = lambda arr: pl.BlockSpec((rc, arr.shape[1]), prev_rows)
    return pl.pallas_call(
        body,
        name=name,
        grid=(ni + 1, nk),
        in_specs=[a_spec, b_spec] + [row_spec(x) for x in row_ins] + [_full(x.shape) for x in vec_ins]
        + _hbm_specs(n_dep),
        out_specs=[row_spec(s) for s in row_outs] + [_full(s.shape) for s in stat_outs],
        out_shape=list(row_outs) + list(stat_outs),
        scratch_shapes=[pltpu.VMEM((tm, N), F32), pltpu.VMEM((tm, N), F32)],
        compiler_params=_params(2),
    )(a, b, *row_ins, *vec_ins, *deps)


SUB_ROWS = 16


def _by_sub_rows(n_rows, fn):
    sums = None
    for r0 in range(0, n_rows, SUB_ROWS):
        part = fn(slice(r0, r0 + SUB_ROWS))
        if part:
            sums = part if sums is None else tuple(x + y for x, y in zip(sums, part))
    return sums


def _to_bf16(v):
    return v.astype(BF16)


def _ln_bwd(dy, xhat, rstd, g):
    dxh = dy * g
    m1 = jnp.mean(dxh, axis=-1, keepdims=True)
    m2 = jnp.mean(dxh * xhat, axis=-1, keepdims=True)
    return rstd * (dxh - m1 - xhat * m2)


def _ln_fwd(h):
    mu = jnp.mean(h, axis=-1, keepdims=True)
    xc = h - mu
    var = jnp.mean(xc * xc, axis=-1, keepdims=True)
    rstd = lax.rsqrt(var + LN_EPS)
    return xc * rstd, rstd


def _proj_fwd(x, w_full, deps=()):
    T, D = x.shape
    P = w_full.shape[1]
    tm, tn = _tile(T, 1024), _tile(P, 1280)
    return _mm("proj_fwd", "nn", x, w_full, M=T, N=P, K=D, tm=tm, tn=tn, tk=D,
               outs=[(jax.ShapeDtypeStruct((T, P), F32), pl.BlockSpec((tm, tn), lambda i, j, k: (i, j)))],
               epilogue=None, a_fn=_to_bf16, deps=deps)[0]


def _conv_shift(h, hp):
    row = lax.broadcasted_iota(jnp.int32, h.shape, 0)
    hm1 = hp[7:8, :]
    hm2 = hp[6:7, :]
    h1 = jnp.where(row == 0, hm1, pltpu.roll(h, 1, 0))
    h2 = jnp.where(row == 0, hm2, jnp.where(row == 1, hm1, pltpu.roll(h, 2, 0)))
    return h1, h2


def _conv_fwd(proj, conv_w8, conv_g):
    T = proj.shape[0]
    tt = _tile(T, 256)
    nt = T // tt
    t8 = tt // 8

    def body(b_ref, c_ref, u_ref, cp_ref, up_ref, w_ref, g_ref, yin_ref):
        i = pl.program_id(0)
        h = c_ref[...] * u_ref[...]
        hp = jnp.where(i > 0, cp_ref[...] * up_ref[...], 0.0)
        h1, h2 = _conv_shift(h, hp)
        w = w_ref[...]
        y = w[0:1, :] * h2 + w[1:2, :] * h1 + w[2:3, :] * h
        p = b_ref[...] * y
        parts = []
        for gi in range(CONV_GROUPS):
            pg = p[:, gi * LANE:(gi + 1) * LANE]
            r = lax.rsqrt(jnp.mean(pg * pg, axis=-1, keepdims=True) + RMS_EPS)
            parts.append(pg * r)
        yn = jnp.concatenate(parts, axis=1) * g_ref[...]
        yin_ref[...] = yn.astype(BF16)

    def col(cidx):
        return pl.BlockSpec((tt, D_CONV), lambda i: (i, cidx))

    def prev(cidx):
        return pl.BlockSpec((8, D_CONV), lambda i: (jnp.maximum(i * t8 - 1, 0), cidx))

    return pl.pallas_call(
        body,
        name="conv_fwd",
        grid=(nt,),
        in_specs=[col(0), col(1), col(2), prev(1), prev(2), _full((8, D_CONV)), _full((1, D_CONV))],
        out_specs=pl.BlockSpec((tt, D_CONV), lambda i: (i, 0)),
        out_shape=jax.ShapeDtypeStruct((T, 2 * D_CONV), BF16),
        compiler_params=_params(1),
    )(proj, proj, proj, proj, proj, conv_w8, conv_g)


def _log_sigmoid(z):
    return jnp.minimum(z, 0.0) - jnp.log(1.0 + jnp.exp(-jnp.abs(z)))


STEP_CHUNKS = 2
STEP_ROWS = STEP_CHUNKS * CHUNK


def _gla_step_terms(blk, wg_ref, gb_ref):
    zl = blk[:, 3072:3200]
    z = _dot(zl.astype(BF16), wg_ref[...], NN) + gb_ref[...]
    log_a = _log_sigmoid(z) * (1.0 / GATE_TAU)
    ri = lax.broadcasted_iota(jnp.int32, (STEP_ROWS, STEP_ROWS), 0)
    ci = lax.broadcasted_iota(jnp.int32, (STEP_ROWS, STEP_ROWS), 1)
    same = (ri // CHUNK) == (ci // CHUNK)
    lower = (same & (ri >= ci)).astype(F32)
    bcum = _dot(lower, log_a, NN, precision=lax.Precision.HIGHEST)
    return zl, z, bcum, same


def _causal():
    return (lax.broadcasted_iota(jnp.int32, (CHUNK, CHUNK), 0) >= lax.broadcasted_iota(jnp.int32, (CHUNK, CHUNK), 1))


def _gla_head_terms(q, k, bcum, h):
    sl = slice(h * HEAD_K, (h + 1) * HEAD_K)
    bh = bcum[:, sl]
    bl = bh[CHUNK - 1:CHUNK, :]
    eb = jnp.exp(bh)
    enb = jnp.exp(-bh)
    eend = jnp.exp(bl - bh)
    dec = jnp.exp(bl)
    qd = q[:, sl] * (HEAD_K ** -0.5) * eb
    ki = k[:, sl] * enb
    ke = k[:, sl] * eend
    return eb, enb, eend, dec, qd, ki, ke


def _sigmoid(x):
    return 1.0 / (1.0 + jnp.exp(-x))


def _gla_fwd(proj, wg128, gbias, gng, yin, deps=()):
    T = proj.shape[0]
    nch = T // CHUNK
    nst = T // STEP_ROWS

    def body(p_ref, wg_ref, gb_ref, gn_ref, yin_in_ref, *rest):
        o_ref, st_ref, yin_ref, s_ref = rest[len(deps):]
        n = pl.program_id(0)

        @pl.when(n == 0)
        def _():
            s_ref[...] = jnp.zeros_like(s_ref)

        blk = p_ref[...]
        _, _, bcum_all, _ = _gla_step_terms(blk, wg_ref, gb_ref)
        causal = _causal()
        gn = gn_ref[...]
        states = [s_ref[h] for h in range(GLA_HEADS)]
        for c in range(STEP_CHUNKS):
            rows = slice(c * CHUNK, (c + 1) * CHUNK)
            q, k = blk[rows, 0:512], blk[rows, 512:1024]
            v, r = blk[rows, 1024:2048], blk[rows, 2048:3072]
            bcum = bcum_all[rows, :]
            for h in range(GLA_HEADS):
                _, _, _, dec, qd, ki, ke = _gla_head_terms(q, k, bcum, h)
                vs = slice(h * HEAD_V, (h + 1) * HEAD_V)
                vb = v[:, vs].astype(BF16)
                qdb = qd.astype(BF16)
                a = jnp.where(causal, _dot(qdb, ki.astype(BF16), NT), 0.0)
                st = states[h]
                o = _dot(a.astype(BF16), vb, NN) + _dot(qdb, st.astype(BF16), NT)
                st_ref[c, h] = st
                states[h] = dec * st + _dot(vb, ke.astype(BF16), TN)
                o_ref[rows, vs] = o
                rinv = lax.rsqrt(jnp.mean(o * o, axis=-1, keepdims=True) + RMS_EPS)
                rh = r[:, vs]
                yin_ref[rows, vs] = (o * rinv * gn[:, vs] * (rh * _sigmoid(rh))).astype(BF16)
        for h in range(GLA_HEADS):
            s_ref[h] = states[h]

    return pl.pallas_call(
        body,
        name="gla_fwd",
        grid=(nst,),
        in_specs=[pl.BlockSpec((STEP_ROWS, HALF_P), lambda n: (n, 1)), _full((LANE, D_GLA_K)), _full((1, D_GLA_K)),
                  _full((1, D_GLA_V)), pl.BlockSpec(memory_space=pl.ANY)] + _hbm_specs(len(deps)),
        out_specs=[pl.BlockSpec((STEP_ROWS, D_GLA_V), lambda n: (n, 0)),
                   pl.BlockSpec((STEP_CHUNKS, GLA_HEADS, HEAD_V, HEAD_K), lambda n: (n, 0, 0, 0)),
                   pl.BlockSpec((STEP_ROWS, D_GLA_V), lambda n: (n, 1))],
        out_shape=[jax.ShapeDtypeStruct((T, D_GLA_V), F32),
                   jax.ShapeDtypeStruct((nch, GLA_HEADS, HEAD_V, HEAD_K), F32),
                   jax.ShapeDtypeStruct(yin.shape, BF16)],
        scratch_shapes=[pltpu.VMEM((GLA_HEADS, HEAD_V, HEAD_K), F32)],
        input_output_aliases={4: 2},
        compiler_params=_params(1),
    )(proj, wg128, gbias, gng, yin, *deps)


def _mix_ln1(yin, w_out, x, ln_g, ln_b, deps=()):
    T, D = x.shape
    KY = yin.shape[1]
    tm = _tile(T, 1024)

    def chunk(acc, valid, ri, vi, ro, so):
        g, b = vi[0][...], vi[1][...]

        def sub(rows):
            xhat, rstd = _ln_fwd(DN_ALPHA * ri[0][rows, :] + acc[rows, :])
            ro[0][rows, :] = xhat
            ro[1][rows, :] = (xhat * g + b).astype(BF16)
            ro[2][rows, :] = rstd

        _by_sub_rows(acc.shape[0], sub)

    return _mm_rows("mix_ln1", "nn", yin, w_out, M=T, N=D, K=KY, tm=tm, tk=_tile(KY, 512),
                    row_ins=[x], vec_ins=[ln_g, ln_b],
                    row_outs=[jax.ShapeDtypeStruct((T, D), F32), jax.ShapeDtypeStruct((T, D), BF16),
                              jax.ShapeDtypeStruct((T, 1), F32)],
                    stat_outs=[], chunk_fn=chunk, deps=deps)


def _ff_up(x1, w_up_blk, first, count, prev=None, deps=()):
    T, D = x1.shape
    nb, _, fb = w_up_blk.shape
    tm = _tile(T, 1024)
    ni = T // tm
    n_dep = len(deps) + (2 if prev is not None else 0)

    def body(a_ref, b_ref, *rest):
        ra_ref, h2_ref = rest[n_dep:n_dep + 2]
        ra = jnp.maximum(_dot(a_ref[...], b_ref[...], NN), 0.0)
        ra_ref[...] = ra.astype(BF16)
        h2_ref[...] = (ra * ra).astype(BF16)

    blk = pl.BlockSpec((tm, fb), lambda i, j: (i, first + j))
    shp = jax.ShapeDtypeStruct((T, nb * fb), BF16)
    keep = list(prev) if prev is not None else []
    return pl.pallas_call(
        body,
        name="ff_up_%d" % first,
        grid=(ni, count),
        in_specs=[pl.BlockSpec((tm, D), lambda i, j: (i, 0)),
                  pl.BlockSpec((None, D, fb), lambda i, j: (first + j, 0, 0))] + _hbm_specs(n_dep),
        out_specs=[blk, blk],
        out_shape=[shp, shp],
        input_output_aliases=({2: 0, 3: 1} if prev is not None else {}),
        compiler_params=_params(2),
    )(x1, w_up_blk, *keep, *deps)


def _ff_down_loss(h2, w_down, xhat1, target, g1, b1, g2, b2):
    T, F = h2.shape
    D = w_down.shape[1]
    tm = _tile(T, 1024)
    inv_d = 1.0 / D

    def chunk(acc, valid, ri, vi, ro, so):
        g1v, b1v, g2v, b2v = (v[...] for v in vi)

        def sub(rows):
            x1 = ri[0][rows, :] * g1v + b1v
            xhat, rstd = _ln_fwd(DN_ALPHA * x1 + acc[rows, :])
            e = xhat * g2v + b2v - ri[1][rows, :]
            dy = e * inv_d
            dh = _ln_bwd(dy, xhat, rstd, g2v)
            ro[0][rows, :] = dh
            ro[1][rows, :] = dh.astype(BF16)
            return (jnp.sum(dy * xhat, axis=0, keepdims=True), jnp.sum(dy, axis=0, keepdims=True),
                    jnp.sum(e * e, axis=0, keepdims=True))

        sg, sb, sl = _by_sub_rows(acc.shape[0], sub)
        so[0][...] += jnp.where(valid, sg, 0.0)
        so[1][...] += jnp.where(valid, sb, 0.0)
        so[2][...] += jnp.where(valid, sl * (0.5 * inv_d), 0.0)

    vshape = jax.ShapeDtypeStruct((1, D), F32)
    return _mm_rows("ff_down_loss", "nn", h2, w_down, M=T, N=D, K=F, tm=tm, tk=_tile(F, 1024),
                    row_ins=[xhat1, target], vec_ins=[g1, b1, g2, b2],
                    row_outs=[jax.ShapeDtypeStruct((T, D), F32), jax.ShapeDtypeStruct((T, D), BF16)],
                    stat_outs=[vshape, vshape, vshape], chunk_fn=chunk)


def _ff_down_bwd_act(dh3b, w_down, ra):
    T, D = dh3b.shape
    F = w_down.shape[0]
    tm, tn = _tile(T, 1024), _tile(F, 1024)

    def ep(acc_ref, ex, o, i, j):
        o[0][...] = (acc_ref[...] * (2.0 * ex[0][...].astype(F32))).astype(BF16)

    blk = pl.BlockSpec((tm, tn), lambda i, j, k: (i, j))
    return _mm("ff_down_bwd_act", "nt", dh3b, w_down, M=T, N=F, K=D, tm=tm, tn=tn, tk=D,
               outs=[(jax.ShapeDtypeStruct((T, F), BF16), blk)], extras=[(ra, blk)], epilogue=ep)[0]


def _grad_w(name, a, b, *, a_fn=None, tm_pref=1024, tn_pref=1024, tk_pref=4096, deps=()):
    T, M = a.shape
    N = b.shape[1]
    tm, tn, tk = _tile(M, tm_pref), _tile(N, tn_pref), _tile(T, tk_pref)
    return _mm(name, "tn", a, b, M=M, N=N, K=T, tm=tm, tn=tn, tk=tk, a_fn=a_fn, deps=deps,
               outs=[(jax.ShapeDtypeStruct((M, N), F32), pl.BlockSpec((tm, tn), lambda i, j, k: (i, j)))],
               epilogue=None)[0]


def _grad_w_up_blk(x1, da, nb, deps=()):
    T, D = x1.shape
    F = da.shape[1]
    fb = F // nb
    tm, tk = _tile(D, 1024), _tile(T, 4096)
    return _mm("grad_w_up", "tn", x1, da, M=D, N=F, K=T, tm=tm, tn=fb, tk=tk, deps=deps,
               outs=[(jax.ShapeDtypeStruct((nb, D, fb), F32),
                      pl.BlockSpec((None, tm, fb), lambda i, j, k: (j, i, 0)))],
               epilogue=None)[0]


def _ff_up_bwd_ln1(da, w_up_blk, dh3, xhat1, rstd1, g1, deps=()):
    T, F = da.shape
    nb, D, fb = w_up_blk.shape
    tm = _tile(T, 1024)

    def chunk(acc, valid, ri, vi, ro, so):
        g = vi[0][...]

        def sub(rows):
            dx1 = DN_ALPHA * ri[0][rows, :] + acc[rows, :]
            xhat = ri[1][rows, :]
            dh = _ln_bwd(dx1, xhat, ri[2][rows, :], g)
            ro[0][rows, :] = dh
            ro[1][rows, :] = dh.astype(BF16)
            return jnp.sum(dx1 * xhat, axis=0, keepdims=True), jnp.sum(dx1, axis=0, keepdims=True)

        sg, sb = _by_sub_rows(acc.shape[0], sub)
        so[0][...] += jnp.where(valid, sg, 0.0)
        so[1][...] += jnp.where(valid, sb, 0.0)

    nk = F // fb
    vshape = jax.ShapeDtypeStruct((1, D), F32)
    return _mm_rows("ff_up_bwd_ln1", "nt", da, w_up_blk, M=T, N=D, K=F, tm=tm, tk=fb,
                    b_spec=pl.BlockSpec((None, D, fb), lambda i, k: (jnp.where(i < T // tm, k, nk - 1), 0, 0)),
                    row_ins=[dh3, xhat1, rstd1], vec_ins=[g1],
                    row_outs=[jax.ShapeDtypeStruct((T, D), F32), jax.ShapeDtypeStruct((T, D), BF16)],
                    stat_outs=[vshape, vshape], chunk_fn=chunk, deps=deps)


def _mix_bwd(dh1b, w_out, deps=()):
    T, D = dh1b.shape
    KY = w_out.shape[0]
    tm, tn = _tile(T, 1024), _tile(KY, 1024)
    return _mm("mix_bwd", "nt", dh1b, w_out, M=T, N=KY, K=D, tm=tm, tn=tn, tk=D, deps=deps,
               outs=[(jax.ShapeDtypeStruct((T, KY), F32), pl.BlockSpec((tm, tn), lambda i, j, k: (i, j)))],
               epilogue=None)[0]


def _conv_bwd(proj, dyin, conv_w8, conv_g, deps=()):
    T = proj.shape[0]
    tt = _tile(T, 256)
    nt = T // tt
    t8 = tt // 8
    nx = tt + 8

    def body(b_ref, c_ref, u_ref, d_ref, bn_ref, cn_ref, un_ref, dn_ref, cp_ref, up_ref, w_ref, g_ref, *rest):
        dp_ref, dw_ref, dg_ref = rest[len(deps):]
        i = pl.program_id(0)

        @pl.when(i == 0)
        def _():
            dw_ref[...] = jnp.zeros_like(dw_ref)
            dg_ref[...] = jnp.zeros_like(dg_ref)

        more = i < nt - 1

        def ext(cur_ref, nxt_ref):
            return jnp.concatenate([cur_ref[...], jnp.where(more, nxt_ref[...], 0.0)], axis=0)

        bx, cx, ux, dx = ext(b_ref, bn_ref), ext(c_ref, cn_ref), ext(u_ref, un_ref), ext(d_ref, dn_ref)
        hx = cx * ux
        hp = jnp.where(i > 0, cp_ref[...] * up_ref[...], 0.0)
        h1, h2 = _conv_shift(hx, hp)
        w = w_ref[...]
        g = g_ref[...]
        yx = w[0:1, :] * h2 + w[1:2, :] * h1 + w[2:3, :] * hx
        px = bx * yx
        dps, dgs = [], []
        for gi in range(CONV_GROUPS):
            sl = slice(gi * LANE, (gi + 1) * LANE)
            pg, dg_ = px[:, sl], dx[:, sl]
            r = lax.rsqrt(jnp.mean(pg * pg, axis=-1, keepdims=True) + RMS_EPS)
            gd = g[:, sl] * dg_
            dps.append(r * gd - pg * (r * r * r) * jnp.mean(pg * gd, axis=-1, keepdims=True))
            dgs.append(jnp.sum((dg_ * pg * r)[:tt, :], axis=0, keepdims=True))
        dpx = jnp.concatenate(dps, axis=1)
        dg_ref[...] += jnp.concatenate(dgs, axis=1)
        dyx = dpx * bx
        dyc = dyx[:tt, :]
        dh = (w[2:3, :] * dyx + w[1:2, :] * pltpu.roll(dyx, nx - 1, 0) + w[0:1, :] * pltpu.roll(dyx, nx - 2, 0))[:tt, :]
        dw_ref[0:1, :] += jnp.sum(dyc * h2[:tt, :], axis=0, keepdims=True)
        dw_ref[1:2, :] += jnp.sum(dyc * h1[:tt, :], axis=0, keepdims=True)
        dw_ref[2:3, :] += jnp.sum(dyc * hx[:tt, :], axis=0, keepdims=True)
        dp_ref[:, 0:D_CONV] = (dpx * yx)[:tt, :].astype(BF16)
        dp_ref[:, D_CONV:2 * D_CONV] = (dh * u_ref[...]).astype(BF16)
        dp_ref[:, 2 * D_CONV:3 * D_CONV] = (dh * c_ref[...]).astype(BF16)
        dp_ref[:, 3 * D_CONV:HALF_P] = jnp.zeros((tt, HALF_P - 3 * D_CONV), BF16)

    def col(cidx):
        return pl.BlockSpec((tt, D_CONV), lambda i: (i, cidx))

    def nxt(cidx):
        return pl.BlockSpec((8, D_CONV), lambda i: (jnp.minimum((i + 1) * t8, T // 8 - 1), cidx))

    def prev(cidx):
        return pl.BlockSpec((8, D_CONV), lambda i: (jnp.maximum(i * t8 - 1, 0), cidx))

    return pl.pallas_call(
        body,
        name="conv_bwd",
        grid=(nt,),
        in_specs=[col(0), col(1), col(2), col(0), nxt(0), nxt(1), nxt(2), nxt(0), prev(1), prev(2),
                  _full((8, D_CONV)), _full((1, D_CONV))] + _hbm_specs(len(deps)),
        out_specs=[pl.BlockSpec((tt, HALF_P), lambda i: (i, 0)), _full((8, D_CONV)), _full((1, D_CONV))],
        out_shape=[jax.ShapeDtypeStruct((T, P_INT), BF16), jax.ShapeDtypeStruct((8, D_CONV), F32),
                   jax.ShapeDtypeStruct((1, D_CONV), F32)],
        compiler_params=_params(1),
    )(proj, proj, proj, dyin, proj, proj, proj, dyin, proj, proj, conv_w8, conv_g, *deps)


def _gla_bwd(proj, wg128, gbias, gng, o_all, states, dyin, dproj):
    T = proj.shape[0]
    nst = T // STEP_ROWS

    def body(p_ref, wg_ref, gb_ref, gn_ref, o_ref, st_ref, d_ref, dp_in_ref,
             dp_ref, dwg_ref, dgb_ref, dgn_ref, ds_ref):
        n = pl.program_id(0)

        @pl.when(n == 0)
        def _():
            ds_ref[...] = jnp.zeros_like(ds_ref)
            dwg_ref[...] = jnp.zeros_like(dwg_ref)
            dgb_ref[...] = jnp.zeros_like(dgb_ref)
            dgn_ref[...] = jnp.zeros_like(dgn_ref)

        blk = p_ref[...]
        zl, z, bcum_all, same = _gla_step_terms(blk, wg_ref, gb_ref)
        causal = _causal()
        gn = gn_ref[...]
        ri = lax.broadcasted_iota(jnp.int32, (STEP_ROWS, STEP_ROWS), 0)
        ci = lax.broadcasted_iota(jnp.int32, (STEP_ROWS, STEP_ROWS), 1)
        upper = (same & (ri <= ci)).astype(F32)
        dstates = [ds_ref[h] for h in range(GLA_HEADS)]
        db_rows, dbl_rows, dgn_sum = [None] * STEP_CHUNKS, [None] * STEP_CHUNKS, [None] * GLA_HEADS
        for c in reversed(range(STEP_CHUNKS)):
            rows = slice(c * CHUNK, (c + 1) * CHUNK)
            q, k = blk[rows, 0:512], blk[rows, 512:1024]
            v, r = blk[rows, 1024:2048], blk[rows, 2048:3072]
            bcum = bcum_all[rows, :]
            db_parts, dbl_parts = [], []
            for h in range(GLA_HEADS):
                eb, enb, eend, dec, qd, ki, ke = _gla_head_terms(q, k, bcum, h)
                vs = slice(h * HEAD_V, (h + 1) * HEAD_V)
                ks = slice(h * HEAD_K, (h + 1) * HEAD_K)
                o = o_ref[rows, vs]
                rh = r[:, vs]
                dyg = d_ref[rows, vs]
                rinv = lax.rsqrt(jnp.mean(o * o, axis=-1, keepdims=True) + RMS_EPS)
                sg = _sigmoid(rh)
                on = o * rinv
                dr = dyg * (on * gn[:, vs]) * (sg * (1.0 + rh * (1.0 - sg)))
                don = dyg * (rh * sg)
                part = jnp.sum(don * on, axis=0, keepdims=True)
                dgn_sum[h] = part if dgn_sum[h] is None else dgn_sum[h] + part
                t = don * gn[:, vs]
                do = rinv * t - o * (rinv * rinv * rinv) * jnp.mean(o * t, axis=-1, keepdims=True)
                dob = do.astype(BF16)
                vb = v[:, vs].astype(BF16)
                qdb, kib, keb = qd.astype(BF16), ki.astype(BF16), ke.astype(BF16)
                a = jnp.where(causal, _dot(qdb, kib, NT), 0.0)
                st = st_ref[c, h]
                dst = dstates[h]
                dstb = dst.astype(BF16)
                da = jnp.where(causal, _dot(dob, vb, NT), 0.0)
                dab = da.astype(BF16)
                dv = _dot(a.astype(BF16), dob, TN) + _dot(keb, dstb, NT)
                dqd = _dot(dab, kib, NN) + _dot(dob, st.astype(BF16), NN)
                dki = _dot(dab, qdb, TN)
                dke = _dot(vb, dstb, NN)
                ddec = jnp.sum(st * dst, axis=0, keepdims=True)
                dstates[h] = dec * dst + _dot(dob, qdb, TN)
                dq = dqd * eb * (HEAD_K ** -0.5)
                dk = dki * enb + dke * eend
                db_parts.append(dqd * qd - dki * ki - dke * ke)
                dbl_parts.append(jnp.sum(dke * ke, axis=0, keepdims=True) + dec * ddec)
                dp_ref[rows, ks] = dq.astype(BF16)
                dp_ref[rows, D_GLA_K + h * HEAD_K:D_GLA_K + (h + 1) * HEAD_K] = dk.astype(BF16)
                dp_ref[rows, 1024 + h * HEAD_V:1024 + (h + 1) * HEAD_V] = dv.astype(BF16)
                dp_ref[rows, 2048 + h * HEAD_V:2048 + (h + 1) * HEAD_V] = dr.astype(BF16)
            db_rows[c] = jnp.concatenate(db_parts, axis=1)
            dbl_rows[c] = jnp.broadcast_to(jnp.concatenate(dbl_parts, axis=1), (CHUNK, D_GLA_K))
        for h in range(GLA_HEADS):
            ds_ref[h] = dstates[h]
            dgn_ref[:, h * HEAD_V:(h + 1) * HEAD_V] += dgn_sum[h]
        db = jnp.concatenate(db_rows, axis=0)
        dlog = _dot(upper, db, NN, precision=lax.Precision.HIGHEST) + jnp.concatenate(dbl_rows, axis=0)
        dz = dlog * (1.0 / GATE_TAU) * (1.0 / (1.0 + jnp.exp(z)))
        dzb = dz.astype(BF16)
        dp_ref[:, 3072:3200] = _dot(dzb, wg_ref[...], NT).astype(BF16)
        dwg_ref[...] += _dot(zl.astype(BF16), dzb, TN)
        dgb_ref[...] += jnp.sum(dz, axis=0, keepdims=True)

    rev = lambda n: nst - 1 - n
    return pl.pallas_call(
        body,
        name="gla_bwd",
        grid=(nst,),
        in_specs=[pl.BlockSpec((STEP_ROWS, HALF_P), lambda n: (rev(n), 1)), _full((LANE, D_GLA_K)),
                  _full((1, D_GLA_K)), _full((1, D_GLA_V)),
                  pl.BlockSpec((STEP_ROWS, D_GLA_V), lambda n: (rev(n), 0)),
                  pl.BlockSpec((STEP_CHUNKS, GLA_HEADS, HEAD_V, HEAD_K), lambda n: (rev(n), 0, 0, 0)),
                  pl.BlockSpec((STEP_ROWS, D_GLA_V), lambda n: (rev(n), 1)), pl.BlockSpec(memory_space=pl.ANY)],
        out_specs=[pl.BlockSpec((STEP_ROWS, HALF_P), lambda n: (rev(n), 1)), _full((LANE, D_GLA_K)),
                   _full((1, D_GLA_K)), _full((1, D_GLA_V))],
        out_shape=[jax.ShapeDtypeStruct(dproj.shape, BF16), jax.ShapeDtypeStruct((LANE, D_GLA_K), F32),
                   jax.ShapeDtypeStruct((1, D_GLA_K), F32), jax.ShapeDtypeStruct((1, D_GLA_V), F32)],
        scratch_shapes=[pltpu.VMEM((GLA_HEADS, HEAD_V, HEAD_K), F32)],
        input_output_aliases={7: 0},
        compiler_params=_params(1),
    )(proj, wg128, gbias, gng, o_all, states, dyin, dproj)


def _proj_bwd_x(dproj, w_full, dh1, deps=()):
    T, P = dproj.shape
    D = w_full.shape[0]
    tm, tk = _tile(T, 512), _tile(P, 1280)

    def ep(acc_ref, ex, o, i, j):
        o[0][...] = DN_ALPHA * ex[0][...] + acc_ref[...]

    row = pl.BlockSpec((tm, D), lambda i, j, k: (i, 0))
    return _mm("proj_bwd_x", "nt", dproj, w_full, M=T, N=D, K=P, tm=tm, tn=D, tk=tk,
               outs=[(jax.ShapeDtypeStruct((T, D), F32), row)], extras=[(dh1, row)], epilogue=ep, deps=deps)[0]


def _place():
    x, y, c = lax.axis_index("x"), lax.axis_index("y"), lax.axis_index("c")
    chips = [(1 - x, y), (x, 1 - y), (1 - x, 1 - y)]
    return x, y, c, chips


def _rcopy(src, dst, ssem, rsem, dev):
    return pltpu.make_async_remote_copy(src_ref=src, dst_ref=dst, send_sem=ssem, recv_sem=rsem,
                                        device_id=dev, device_id_type=MESH)


def _all_gather(name, shards, deps=()):
    n = len(shards)

    def body(*refs):
        ins, outs = refs[:n], refs[n + len(deps):2 * n + len(deps)]
        ssem, rsem, lsem = refs[2 * n + len(deps):]
        x, y, c, chips = _place()
        me, sib = (x, y, c), (x, y, 1 - c)

        def slot(w, px, py, pc):
            return outs[w].at[4 * px + 2 * py + pc]

        started = []
        for w in range(n):
            lc = pltpu.make_async_copy(ins[w], slot(w, *me), lsem.at[w])
            lc.start()
            started.append(lc)
        sends = []
        for w in range(n):
            cp = _rcopy(ins[w], slot(w, *me), ssem.at[7 * w], rsem.at[7 * w], sib)
            cp.start()
            sends.append(cp)
            for jx, chip in enumerate(chips):
                cp = _rcopy(ins[w], slot(w, *me), ssem.at[7 * w + 1 + jx], rsem.at[7 * w + 1 + jx], (*chip, c))
                cp.start()
                sends.append(cp)
        for w in range(n):
            for jx, chip in enumerate(chips):
                blk = slot(w, *chip, c)
                _rcopy(blk, blk, ssem.at[7 * w + 1 + jx], rsem.at[7 * w + 1 + jx], me).wait_recv()
                cp = _rcopy(blk, blk, ssem.at[7 * w + 4 + jx], rsem.at[7 * w + 4 + jx], sib)
                cp.start()
                sends.append(cp)
        for w in range(n):
            blk = slot(w, x, y, 1 - c)
            _rcopy(blk, blk, ssem.at[7 * w], rsem.at[7 * w], me).wait_recv()
            for jx, chip in enumerate(chips):
                blk = slot(w, *chip, 1 - c)
                _rcopy(blk, blk, ssem.at[7 * w + 4 + jx], rsem.at[7 * w + 4 + jx], me).wait_recv()
        for cp in sends:
            cp.wait_send()
        for lc in started:
            lc.wait()

    return pl.pallas_call(
        body,
        name=name,
        in_specs=_hbm_specs(n + len(deps)),
        out_specs=_hbm_specs(n),
        out_shape=[jax.ShapeDtypeStruct((N_DEV,) + s.shape, s.dtype) for s in shards],
        scratch_shapes=[pltpu.SemaphoreType.DMA((7 * n,)), pltpu.SemaphoreType.DMA((7 * n,)),
                        pltpu.SemaphoreType.DMA((n,))],
    )(*shards, *deps)


HBM_SPEC = pl.BlockSpec(memory_space=pltpu.HBM)
SEM_SPEC = pl.BlockSpec(memory_space=pltpu.SEMAPHORE)
SIDE_EFFECT = pltpu.SideEffectType.DATAFLOW_SIDE_EFFECTING


def _cast_place(name, ids, w, deps=(), dtype=BF16):
    R, C = w.shape
    tr = _tile(R, 256)

    def body(ids_ref, w_ref, *rest):
        rest[len(deps)][...] = w_ref[...].astype(dtype)

    return pl.pallas_call(
        body,
        name=name,
        grid_spec=pltpu.PrefetchScalarGridSpec(
            num_scalar_prefetch=1,
            grid=(R // tr,),
            in_specs=[pl.BlockSpec((tr, C), lambda r, ids: (r, 0))] + _hbm_specs(len(deps)),
            out_specs=pl.BlockSpec((None, tr, C), lambda r, ids: (ids[0], r, 0)),
        ),
        out_shape=jax.ShapeDtypeStruct((N_DEV, R, C), dtype),
        compiler_params=_params(1),
    )(ids, w, *deps)


def _xfer_start(name, bufs, plan, n):
    nb = len(bufs)

    def body(*refs):
        ins = refs[:nb]
        ssem, rsem = refs[nb], refs[nb + 1]
        token = refs[2 * nb + 2]
        x, y, c, chips = _place()
        for k, (src, dst, dev, _) in enumerate(plan(ins, x, y, c, chips)):
            _rcopy(src, dst, ssem.at[k], rsem.at[k], dev).start()
        token[...] = jnp.zeros_like(token)

    res = pl.pallas_call(
        body,
        name=name,
        out_shape=(pltpu.SemaphoreType.DMA((n,)), pltpu.SemaphoreType.DMA((n,)),
                   *[pltpu.HBM(b.shape, b.dtype) for b in bufs], jax.ShapeDtypeStruct((8, LANE), F32)),
        in_specs=[HBM_SPEC] * nb,
        out_specs=(SEM_SPEC, SEM_SPEC, *[HBM_SPEC] * nb, pl.BlockSpec(memory_space=pltpu.VMEM)),
        input_output_aliases={i: 2 + i for i in range(nb)},
        compiler_params=pltpu.CompilerParams(has_side_effects=SIDE_EFFECT),
    )(*[pltpu.with_memory_space_constraint(b, pltpu.HBM) for b in bufs])
    return dict(sems=res[:2], bufs=list(res[2:2 + nb]), token=res[2 + nb], plan=plan, n=n)


def _xfer_wait(name, started, after):
    bufs, plan = started["bufs"], started["plan"]
    nb = len(bufs)

    def body(*refs):
        ins = refs[:nb]
        ssem, rsem = refs[nb], refs[nb + 1]
        x, y, c, chips = _place()
        for k, (src, _, dev, land) in enumerate(plan(ins, x, y, c, chips)):
            cp = _rcopy(src, land, ssem.at[k], rsem.at[k], dev)
            cp.wait_send()
            cp.wait_recv()

    res = pl.pallas_call(
        body,
        name=name,
        out_shape=tuple(pltpu.HBM(b.shape, b.dtype) for b in bufs),
        in_specs=[HBM_SPEC] * nb + [SEM_SPEC, SEM_SPEC, pl.BlockSpec(memory_space=pl.ANY)],
        out_specs=tuple([HBM_SPEC] * nb),
        input_output_aliases={i: i for i in range(nb)},
        compiler_params=pltpu.CompilerParams(has_side_effects=SIDE_EFFECT),
    )(*bufs, *started["sems"], after)
    return list(res)


def _plan_gather_chips(refs, x, y, c, chips):
    (land,) = refs
    mine = land.at[4 * x + 2 * y + c]
    plan = [(mine, mine, (x, y, 1 - c), land.at[4 * x + 2 * y + (1 - c)])]
    for px, py in chips:
        plan.append((mine, mine, (px, py, c), land.at[4 * px + 2 * py + c]))
    return plan


def _plan_gather_pass(refs, x, y, c, chips):
    (land,) = refs
    return [(land.at[4 * px + 2 * py + c], land.at[4 * px + 2 * py + c], (x, y, 1 - c),
             land.at[4 * px + 2 * py + (1 - c)]) for px, py in chips]


def _plan_reduce_core(refs, x, y, c, chips):
    grad, recv = refs
    return [(grad.at[2 * q + (1 - c)], recv.at[q], (x, y, 1 - c), recv.at[q]) for q in range(N_CHIP)]


def _plan_reduce_chips(refs, x, y, c, chips):
    part, land = refs
    return [(part.at[2 * px + py], land.at[2 * x + y], (px, py, c), land.at[2 * px + py]) for px, py in chips]


def _chip_sums(name, ids, grad, recv):
    _, R, C = grad.shape
    tr = _tile(R, 256)

    def body(ids_ref, g_ref, r_ref, o_ref):
        o_ref[...] = (g_ref[...] + r_ref[...]).astype(BF16)

    return pl.pallas_call(
        body,
        name=name,
        grid_spec=pltpu.PrefetchScalarGridSpec(
            num_scalar_prefetch=1,
            grid=(N_CHIP - 1, R // tr),
            in_specs=[pl.BlockSpec((None, tr, C), lambda q, r, ids: (2 * ids[3 + q] + ids[2], r, 0)),
                      pl.BlockSpec((None, tr, C), lambda q, r, ids: (ids[3 + q], r, 0))],
            out_specs=pl.BlockSpec((None, tr, C), lambda q, r, ids: (ids[3 + q], r, 0)),
        ),
        out_shape=jax.ShapeDtypeStruct((N_CHIP, R, C), BF16),
        compiler_params=_params(2),
    )(ids, grad, recv)


def _adamw(w, g, m, v):
    m = ADAM_B1 * m + (1.0 - ADAM_B1) * g
    v = ADAM_B2 * v + (1.0 - ADAM_B2) * (g * g)
    m_hat = m / (1.0 - ADAM_B1 ** ADAM_STEP)
    v_hat = v / (1.0 - ADAM_B2 ** ADAM_STEP)
    delta = -ADAM_LR * (m_hat / (jnp.sqrt(v_hat) + ADAM_EPS) + ADAM_WD * w)
    return delta, m, v


def _reduce_adamw(name, ids, grad, recv, landed, w, m, v):
    _, R, C = grad.shape
    tr = _tile(R, 256)

    def body(ids_ref, g_ref, r_ref, l1_ref, l2_ref, l3_ref, w_ref, m_ref, v_ref, go_ref, do_ref, mo_ref, vo_ref):
        g = g_ref[...] + r_ref[...]
        g = g + l1_ref[...].astype(F32)
        g = g + l2_ref[...].astype(F32)
        g = g + l3_ref[...].astype(F32)
        delta, mn, vn = _adamw(w_ref[...], g, m_ref[...], v_ref[...])
        go_ref[...] = g
        do_ref[...] = delta
        mo_ref[...] = mn
        vo_ref[...] = vn

    def pick(k):
        return pl.BlockSpec((None, tr, C), lambda r, ids: (ids[k], r, 0))

    flat = pl.BlockSpec((tr, C), lambda r, ids: (r, 0))
    shp = jax.ShapeDtypeStruct((R, C), F32)
    return pl.pallas_call(
        body,
        name=name,
        grid_spec=pltpu.PrefetchScalarGridSpec(
            num_scalar_prefetch=1,
            grid=(R // tr,),
            in_specs=[pick(0), pick(1), pick(3), pick(4), pick(5), flat, flat, flat],
            out_specs=[flat, flat, flat, flat],
        ),
        out_shape=[shp, shp, shp, shp],
        compiler_params=_params(1),
    )(ids, grad, recv, landed, landed, landed, w, m, v)


def _small_adamw(packs, w, m, v):
    def body(p_ref, w_ref, m_ref, v_ref, g_ref, d_ref, mo_ref, vo_ref):
        g = p_ref[0]
        for dvc in range(1, N_DEV):
            g = g + p_ref[dvc]
        delta, mn, vn = _adamw(w_ref[...], g, m_ref[...], v_ref[...])
        g_ref[...] = g
        d_ref[...] = delta
        mo_ref[...] = mn
        vo_ref[...] = vn

    shp = jax.ShapeDtypeStruct(w.shape, F32)
    return pl.pallas_call(
        body,
        name="small_adamw",
        in_specs=[_full(packs.shape), _full(w.shape), _full(w.shape), _full(w.shape)],
        out_specs=[_full(w.shape)] * 4,
        out_shape=[shp] * 4,
        grid=(1,),
        compiler_params=_params(1),
    )(packs, w, m, v)


def _w_in_pieces():
    cs = D_IN_PROJ // N_DEV
    pieces = []
    for d in range(N_DEV):
        lo, hi = d * cs, (d + 1) * cs
        if hi <= CONV_COLS:
            pieces.append((d, 0, cs, lo))
        elif lo >= CONV_COLS:
            pieces.append((d, 0, cs, lo - CONV_COLS + HALF_P))
        else:
            pieces.append((d, 0, CONV_COLS - lo, lo))
            pieces.append((d, CONV_COLS - lo, cs, HALF_P))
    return pieces


def _w_in_full(gathered):
    nb, D, cs = gathered.shape
    tr = _tile(D, 256)

    def body(g_ref, o_ref):
        o_ref[:, CONV_COLS:HALF_P] = jnp.zeros((tr, HALF_P - CONV_COLS), o_ref.dtype)
        o_ref[:, HALF_P + GLA_COLS:P_INT] = jnp.zeros((tr, HALF_P - GLA_COLS), o_ref.dtype)
        for d, a, b, dst in _w_in_pieces():
            o_ref[:, dst:dst + (b - a)] = g_ref[d, :, a:b]

    return pl.pallas_call(
        body,
        name="w_in_full",
        grid=(D // tr,),
        in_specs=[pl.BlockSpec((nb, tr, cs), lambda r: (0, r, 0))],
        out_specs=pl.BlockSpec((tr, P_INT), lambda r: (r, 0)),
        out_shape=jax.ShapeDtypeStruct((D, P_INT), gathered.dtype),
        compiler_params=_params(1),
    )(gathered)


def _w_in_blocks(dw):
    D = dw.shape[0]
    cs = D_IN_PROJ // N_DEV
    tr = _tile(D, 256)

    def body(w_ref, o_ref):
        for d, a, b, src in _w_in_pieces():
            o_ref[d, :, a:b] = w_ref[:, src:src + (b - a)]

    return pl.pallas_call(
        body,
        name="w_in_blocks",
        grid=(D // tr,),
        in_specs=[pl.BlockSpec((tr, P_INT), lambda r: (r, 0))],
        out_specs=pl.BlockSpec((N_DEV, tr, cs), lambda r: (0, r, 0)),
        out_shape=jax.ShapeDtypeStruct((N_DEV, D, cs), dw.dtype),
        compiler_params=_params(1),
    )(dw)


def _rows(vec, n_rows):
    flat = jnp.pad(vec.reshape(-1), (0, n_rows * SP_COLS - vec.size))
    return flat.reshape(n_rows, SP_COLS)


def _pad_cols(a):
    return jnp.pad(a, ((0, 0), (0, SP_COLS - a.shape[1])))


R_CONV_W, R_CONV_G, R_GATE_B, R_GLA_G, R_LN1_G, R_LN1_B, R_LN2_G, R_LN2_B, R_LOSS, R_GATE_W = 0, 3, 4, 5, 6, 8, 10, 12, 14, 16


def _pack(conv_w, conv_g, gate_b, gla_g, ln1_g, ln1_b, ln2_g, ln2_b, loss, gate_w):
    z = jnp.zeros((1, SP_COLS), F32)
    parts = [_pad_cols(conv_w), _pad_cols(conv_g), _pad_cols(gate_b), _pad_cols(gla_g),
             _rows(ln1_g, 2), _rows(ln1_b, 2), _rows(ln2_g, 2), _rows(ln2_b, 2),
             z if loss is None else _pad_cols(jnp.sum(loss, axis=1, keepdims=True)), z, _pad_cols(gate_w)]
    return jnp.concatenate(parts, axis=0)


def _unpack(p, D, conv_cols, gate_cols):
    return dict(
        conv_w=p[R_CONV_W:R_CONV_W + 3, :conv_cols], conv_norm_g=p[R_CONV_G:R_CONV_G + 1, :D_CONV],
        gate_bias=p[R_GATE_B:R_GATE_B + 1, :D_GLA_K], gla_norm_g=p[R_GLA_G:R_GLA_G + 1, :D_GLA_V],
        ln1_g=p[R_LN1_G:R_LN1_G + 2].reshape(1, -1)[:, :D], ln1_b=p[R_LN1_B:R_LN1_B + 2].reshape(1, -1)[:, :D],
        ln2_g=p[R_LN2_G:R_LN2_G + 2].reshape(1, -1)[:, :D], ln2_b=p[R_LN2_B:R_LN2_B + 2].reshape(1, -1)[:, :D],
        w_gate_up=p[R_GATE_W:R_GATE_W + GATE_RANK, :gate_cols])


BIG = ("w_in", "w_out", "w_ff_up", "w_ff_down")
ORDER = ("w_in", "conv_w", "conv_norm_g", "w_gate_up", "gate_bias", "gla_norm_g", "w_out", "ln1_g", "ln1_b",
         "w_ff_up", "w_ff_down", "ln2_g", "ln2_b")


def kernel(x, w_in, conv_w, conv_norm_g, w_gate_up, gate_bias, gla_norm_g, w_out, ln1_g, ln1_b, w_ff_up, w_ff_down, ln2_g, ln2_b, loss_target, m_w_in, m_conv_w, m_conv_norm_g, m_w_gate_up, m_gate_bias, m_gla_norm_g, m_w_out, m_ln1_g, m_ln1_b, m_w_ff_up, m_w_ff_down, m_ln2_g, m_ln2_b, v_w_in, v_conv_w, v_conv_norm_g, v_w_gate_up, v_gate_bias, v_gla_norm_g, v_w_out, v_ln1_g, v_ln1_b, v_w_ff_up, v_w_ff_down, v_ln2_g, v_ln2_b):
    T, D = x.shape[1], x.shape[2]
    xs, target = x[0], loss_target[0]
    xi, yi, ci = lax.axis_index("x"), lax.axis_index("y"), lax.axis_index("c")
    chip = 2 * xi + yi
    dev = 2 * chip + ci
    others = [jnp.where(chip <= q, q + 1, q) for q in range(N_CHIP - 1)]
    ids = jnp.stack([dev, chip, ci] + others).astype(jnp.int32)
    conv_cols, gate_cols = conv_w.shape[2], w_gate_up.shape[2]

    def gather(nm, w, deps, dtype=BF16):
        return _xfer_start("gather_chips_" + nm, [_cast_place("cast_place_" + nm, ids, w, deps, dtype)],
                           _plan_gather_chips, 4)

    def pass_on(nm, started, after):
        (land,) = _xfer_wait("gather_chips_wait_" + nm, started, after)
        return _xfer_start("gather_pass_" + nm, [land], _plan_gather_pass, 3)

    def landed(nm, started, after):
        return _xfer_wait("gather_pass_wait_" + nm, started, after)[0]

    z1 = jnp.zeros((1, 1), F32)
    fwd_pack = _pack(conv_w[0], z1, z1, z1, z1, z1, z1, z1, None, w_gate_up[0])
    ga_in = gather("w_in", w_in[0], ())
    ga_pack = gather("pack", fwd_pack, [ga_in["token"]], F32)
    ga, dep = [], ga_pack["token"]
    m_in, v_in = m_w_in[0], v_w_in[0]
    for nm, w in zip(BIG[1:], (w_out, w_ff_up, w_ff_down)):
        ga.append(gather(nm, w[0], [dep, m_in, v_in] if nm == "w_ff_down" else [dep]))
        dep = ga[-1]["token"]
    gp_in = pass_on("w_in", ga_in, dep)
    gp_pack = pass_on("pack", ga_pack, gp_in["token"])
    g_in = landed("w_in", gp_in, gp_pack["token"])
    w_full = _w_in_full(g_in)
    g_pack = landed("pack", gp_pack, w_full)
    conv_w_full = g_pack[:, R_CONV_W:R_CONV_W + 3, :conv_cols].transpose(1, 0, 2).reshape(3, -1)
    gate_w_full = g_pack[:, R_GATE_W:R_GATE_W + GATE_RANK, :gate_cols].transpose(1, 0, 2).reshape(GATE_RANK, -1)
    conv_w8 = jnp.pad(conv_w_full, ((0, 5), (0, 0)))
    wg128 = jnp.pad(gate_w_full, ((0, LANE - GATE_RANK), (0, 0))).astype(BF16)
    proj = _proj_fwd(xs, w_full, deps=[g_pack])
    yin = _conv_fwd(proj, conv_w8, conv_norm_g)
    gp_out = pass_on("w_out", ga[0], yin)
    o_all, states, yin = _gla_fwd(proj, wg128, gate_bias, gla_norm_g, yin, deps=[gp_out["token"]])
    w_out_full = landed("w_out", gp_out, o_all).reshape(-1, D)
    gp_up = pass_on("w_ff_up", ga[1], o_all)
    xhat1, x1, rstd1 = _mix_ln1(yin, w_out_full, xs, ln1_g, ln1_b, deps=[gp_up["token"]])
    w_up_blk = landed("w_ff_up", gp_up, x1)
    half = N_DEV // 2
    ra, h2 = _ff_up(x1, w_up_blk, 0, half)
    gp_down = pass_on("w_ff_down", ga[2], ra)
    ra, h2 = _ff_up(x1, w_up_blk, half, N_DEV - half, prev=(ra, h2), deps=[gp_down["token"]])
    w_down_full = landed("w_ff_down", gp_down, ra).reshape(-1, D)
    dh3, dh3b, g_ln2_g, g_ln2_b, loss = _ff_down_loss(h2, w_down_full, xhat1, target, ln1_g, ln1_b, ln2_g, ln2_b)

    def to_core(nm, grad):
        recv = lax.empty((N_CHIP,) + grad.shape[1:], F32)
        return _xfer_start("reduce_core_" + nm, [grad, recv], _plan_reduce_core, N_CHIP)

    def to_chips(nm, started, after):
        grad, recv = _xfer_wait("reduce_core_wait_" + nm, started, after)
        part = _chip_sums("chip_sums_" + nm, ids, grad, recv)
        land = lax.empty(part.shape, BF16)
        return grad, recv, _xfer_start("reduce_chips_" + nm, [part, land], _plan_reduce_chips, N_CHIP - 1)

    da = _ff_down_bwd_act(dh3b, w_down_full, ra)
    gw_down = _grad_w("grad_w_down", h2, dh3b).reshape(N_DEV, -1, D)
    rc_down = to_core("w_ff_down", gw_down)
    gw_up = _grad_w_up_blk(x1, da, N_DEV, deps=[rc_down["token"]])
    gw_down, rv_down, rs_down = to_chips("w_ff_down", rc_down, gw_up)
    rc_up = to_core("w_ff_up", gw_up)
    dh1, dh1b, g_ln1_g, g_ln1_b = _ff_up_bwd_ln1(da, w_up_blk, dh3, xhat1, rstd1, ln1_g,
                                                 deps=[rs_down["token"], rc_up["token"]])
    gw_up, rv_up, rs_up = to_chips("w_ff_up", rc_up, dh1b)
    dyin = _mix_bwd(dh1b, w_out_full, deps=[rs_up["token"]])
    gw_out = _grad_w("grad_w_out", yin, dh1b).reshape(N_DEV, -1, D)
    rc_out = to_core("w_out", gw_out)
    dproj, g_conv_w, g_conv_g = _conv_bwd(proj, dyin, conv_w8, conv_norm_g, deps=[rc_out["token"]])
    dproj, g_gate_w, g_gate_b, g_gla_g = _gla_bwd(proj, wg128, gate_bias, gla_norm_g, o_all, states, dyin, dproj)
    gw_out, rv_out, rs_out = to_chips("w_out", rc_out, dproj)
    gw_in = _w_in_blocks(_grad_w("grad_w_in", xs, dproj, a_fn=_to_bf16, tn_pref=1280, tk_pref=2048, deps=[rs_out["token"]]))
    rc_in = to_core("w_in", gw_in)

    big = {}

    def finish(nm, grad, recv, started, w, m, v, after):
        _, land = _xfer_wait("reduce_chips_wait_" + nm, started, after)
        res = _reduce_adamw("adamw_" + nm, ids, grad, recv, land, w[0], m[0], v[0])
        big[nm] = [a[None] for a in res]
        return res[0]

    done = finish("w_ff_down", gw_down, rv_down, rs_down, w_ff_down, m_w_ff_down, v_w_ff_down, rc_in["token"])
    done = finish("w_ff_up", gw_up, rv_up, rs_up, w_ff_up, m_w_ff_up, v_w_ff_up, done)
    gw_in, rv_in, rs_in = to_chips("w_in", rc_in, done)
    grad_x = _proj_bwd_x(dproj, w_full, dh1, deps=[rs_in["token"]])

    pack = _pack(g_conv_w[:3], g_conv_g, g_gate_b, g_gla_g, g_ln1_g, g_ln1_b, g_ln2_g, g_ln2_b, loss,
                 g_gate_w[:GATE_RANK])
    (packs,) = _all_gather("gather_small_grads", [pack], deps=[grad_x])
    done = finish("w_out", gw_out, rv_out, rs_out, w_out, m_w_out, v_w_out, packs)
    finish("w_in", gw_in, rv_in, rs_in, w_in, (m_in,), (v_in,), done)

    def own_cols(row, n_rows, width):
        cut = lax.dynamic_slice(packs, (0, row, dev * width), (N_DEV, n_rows, width))
        return jnp.pad(cut, ((0, 0), (0, 0), (0, SP_COLS - width)))

    packs_own = jnp.concatenate([own_cols(R_CONV_W, 3, conv_cols), packs[:, R_CONV_W + 3:R_GATE_W],
                                 own_cols(R_GATE_W, GATE_RANK, gate_cols)], axis=1)

    def small_pack(cw, cg, gw, gb, gg, l1g, l1b, l2g, l2b):
        return _pack(cw[0], cg, gb, gg, l1g, l1b, l2g, l2b, None, gw[0])

    w_s = small_pack(conv_w, conv_norm_g, w_gate_up, gate_bias, gla_norm_g, ln1_g, ln1_b, ln2_g, ln2_b)
    m_s = small_pack(m_conv_w, m_conv_norm_g, m_w_gate_up, m_gate_bias, m_gla_norm_g, m_ln1_g, m_ln1_b, m_ln2_g, m_ln2_b)
    v_s = small_pack(v_conv_w, v_conv_norm_g, v_w_gate_up, v_gate_bias, v_gla_norm_g, v_ln1_g, v_ln1_b, v_ln2_g, v_ln2_b)
    g_s, d_s, mn_s, vn_s = _small_adamw(packs_own, w_s, m_s, v_s)
    small = [_unpack(p, D, conv_cols, gate_cols) for p in (g_s, d_s, mn_s, vn_s)]

    def leaf(kind, name):
        if name in BIG:
            return big[name][kind]
        a = small[kind][name]
        return a[None] if name in ("conv_w", "w_gate_up") else a

    out = [g_s[R_LOSS, 0], grad_x[None]]
    for kind in range(4):
        out += [leaf(kind, nm) for nm in ORDER]
    return tuple(out)
```

```python
import jax
import jax.numpy as jnp
from jax import lax
from jax.experimental import pallas as pl
from jax.experimental.pallas import tpu as pltpu

F32 = jnp.float32
BF16 = jnp.bfloat16

D_CONV = 1024
CONV_GROUPS = 8
GLA_HEADS = 4
HEAD_K = 128
HEAD_V = 256
D_GLA_K = 512
D_GLA_V = 1024
GATE_RANK = 16
GATE_TAU = 16.0
CHUNK = 64
LN_EPS = 1e-5
RMS_EPS = 1e-6
DN_ALPHA = 2.0 ** 0.25
D_IN_PROJ = 6160
ADAM_LR = 0.001
ADAM_B1 = 0.9
ADAM_B2 = 0.999
ADAM_EPS = 1e-08
ADAM_WD = 0.01
ADAM_STEP = 10

N_DEV = 8
N_CHIP = 4
LANE = 128
HALF_P = 3200
P_INT = 2 * HALF_P
CONV_COLS = 3 * D_CONV
GLA_COLS = D_IN_PROJ - CONV_COLS
SP_ROWS = 32
SP_COLS = 1024
VMEM_LIMIT = 56 * 1024 * 1024

NN = ((1,), (0,))
NT = ((1,), (1,))
TN = ((0,), (0,))
MESH = pl.DeviceIdType.MESH


def _dot(a, b, dims, precision=None):
    return lax.dot_general(a, b, (dims, ((), ())), preferred_element_type=F32, precision=precision)


def _tile(n, pref):
    if n <= pref:
        return n
    t = (pref // LANE) * LANE
    while t > 0 and n % t:
        t -= LANE
    assert t > 0, (n, pref)
    return t


def _params(n_axes):
    return pltpu.CompilerParams(dimension_semantics=("arbitrary",) * n_axes, vmem_limit_bytes=VMEM_LIMIT)


def _full(shape):
    nd = len(shape)
    return pl.BlockSpec(shape, lambda *_: (0,) * nd)


def _hbm_specs(n):
    return [pl.BlockSpec(memory_space=pl.ANY)] * n


def _mm(name, mode, a, b, *, M, N, K, tm, tn, tk, outs, epilogue, extras=(), a_fn=None, a_spec=None, b_spec=None,
        deps=()):
    ni, nj, nk = M // tm, N // tn, K // tk
    assert ni * tm == M and nj * tn == N and nk * tk == K, (name, M, N, K, tm, tn, tk)
    if a_spec is None:
        a_spec = (pl.BlockSpec((tk, tm), lambda i, j, k: (k, i)) if mode == "tn"
                  else pl.BlockSpec((tm, tk), lambda i, j, k: (i, k)))
    if b_spec is None:
        b_spec = (pl.BlockSpec((tn, tk), lambda i, j, k: (j, k)) if mode == "nt"
                  else pl.BlockSpec((tk, tn), lambda i, j, k: (k, j)))
    dims = {"nn": NN, "nt": NT, "tn": TN}[mode]
    n_ex, n_out, n_dep = len(extras), len(outs), len(deps)

    def body(*refs):
        a_ref, b_ref = refs[0], refs[1]
        ex = refs[2:2 + n_ex]
        o = refs[2 + n_ex + n_dep:2 + n_ex + n_dep + n_out]
        acc_ref = refs[2 + n_ex + n_dep + n_out]
        i, j, k = pl.program_id(0), pl.program_id(1), pl.program_id(2)
        if nk > 1:
            @pl.when(k == 0)
            def _():
                acc_ref[...] = jnp.zeros_like(acc_ref)

        av = a_ref[...]
        if a_fn is not None:
            av = a_fn(av)
        part = _dot(av, b_ref[...], dims)
        if nk == 1 and epilogue is None:
            o[0][...] = part.astype(o[0].dtype)
        elif nk == 1:
            acc_ref[...] = part
            epilogue(acc_ref, ex, o, i, j)
        else:
            acc_ref[...] += part

            @pl.when(k == nk - 1)
            def _():
                if epilogue is None:
                    o[0][...] = acc_ref[...].astype(o[0].dtype)
                else:
                    epilogue(acc_ref, ex, o, i, j)

    return pl.pallas_call(
        body,
        name=name,
        grid=(ni, nj, nk),
        in_specs=[a_spec, b_spec] + [s for _, s in extras] + _hbm_specs(n_dep),
        out_specs=[s for _, s in outs],
        out_shape=[s for s, _ in outs],
        scratch_shapes=[pltpu.VMEM((8, LANE) if nk == 1 and epilogue is None else (tm, tn), F32)],
        compiler_params=_params(3),
    )(a, b, *[x for x, _ in extras], *deps)


def _mm_rows(name, mode, a, b, *, M, N, K, tm, tk, row_ins, vec_ins, row_outs, stat_outs, chunk_fn,
             b_spec=None, deps=()):
    ni, nk = M // tm, K // tk
    rc = tm // nk
    assert ni * tm == M and nk * tk == K and rc * nk == tm and rc % 16 == 0, (name, M, K, tm, tk)
    dims = {"nn": NN, "nt": NT}[mode]
    last = ni - 1
    n_split = 2 if N % (2 * 256) == 0 else 1

    def kk(i, k):
        return jnp.where(i < ni, k, nk - 1)

    a_spec = pl.BlockSpec((tm, tk), lambda i, k: (jnp.minimum(i, last), kk(i, k)))
    if b_spec is None:
        b_spec = (pl.BlockSpec((N, tk), lambda i, k: (0, kk(i, k))) if mode == "nt"
                  else pl.BlockSpec((tk, N), lambda i, k: (kk(i, k), 0)))
    prev_rows = lambda i, k: (jnp.maximum((i - 1) * nk + k, 0), 0)
    n_ri, n_vi, n_ro, n_so, n_dep = len(row_ins), len(vec_ins), len(row_outs), len(stat_outs), len(deps)

    def body(*refs):
        a_ref, b_ref = refs[0], refs[1]
        pos = 2
        ri = refs[pos:pos + n_ri]; pos += n_ri
        vi = refs[pos:pos + n_vi]; pos += n_vi + n_dep
        ro = refs[pos:pos + n_ro]; pos += n_ro
        so = refs[pos:pos + n_so]; pos += n_so
        accs = refs[pos:pos + 2]
        i, k = pl.program_id(0), pl.program_id(1)

        @pl.when((i == 0) & (k == 0))
        def _():
            accs[0][...] = jnp.zeros_like(accs[0])
            accs[1][...] = jnp.zeros_like(accs[1])
            for st in so:
                st[...] = jnp.zeros_like(st)

        def finish_rows(prev_ref):
            rows = pl.ds(pl.multiple_of(k * rc, rc), rc)
            done = prev_ref[rows, :]
            prev_ref[rows, :] = jnp.zeros((rc, N), F32)
            chunk_fn(done, i > 0, ri, vi, ro, so)

        def accumulate(acc_ref):
            av = a_ref[...]
            for c0 in range(0, N, N // n_split):
                cols = slice(c0, c0 + N // n_split)
                bv = b_ref[cols, :] if mode == "nt" else b_ref[:, cols]
                acc_ref[:, cols] += _dot(av, bv, dims)

        for parity in (0, 1):
            @pl.when((i < ni) & (lax.rem(i, 2) == parity))
            def _(parity=parity):
                finish_rows(accs[1 - parity])
                accumulate(accs[parity])

        @pl.when(i == ni)
        def _():
            finish_rows(accs[last % 2])

    row_spec = lambda arr: pl.BlockSpec((rc, arr.shape[1]), prev_rows)
    return pl.pallas_call(
        body,
        name=name,
        grid=(ni + 1, nk),
        in_specs=[a_spec, b_spec] + [row_spec(x) for x in row_ins] + [_full(x.shape) for x in vec_ins]
        + _hbm_specs(n_dep),
        out_specs=[row_spec(s) for s in row_outs] + [_full(s.shape) for s in stat_outs],
        out_shape=list(row_outs) + list(stat_outs),
        scratch_shapes=[pltpu.VMEM((tm, N), F32), pltpu.VMEM((tm, N), F32)],
        compiler_params=_params(2),
    )(a, b, *row_ins, *vec_ins, *deps)


SUB_ROWS = 16


def _by_sub_rows(n_rows, fn):
    sums = None
    for r0 in range(0, n_rows, SUB_ROWS):
        part = fn(slice(r0, r0 + SUB_ROWS))
        if part:
            sums = part if sums is None else tuple(x + y for x, y in zip(sums, part))
    return sums


def _to_bf16(v):
    return v.astype(BF16)


def _ln_bwd(dy, xhat, rstd, g):
    dxh = dy * g
    m1 = jnp.mean(dxh, axis=-1, keepdims=True)
    m2 = jnp.mean(dxh * xhat, axis=-1, keepdims=True)
    return rstd * (dxh - m1 - xhat * m2)


def _ln_fwd(h):
    mu = jnp.mean(h, axis=-1, keepdims=True)
    xc = h - mu
    var = jnp.mean(xc * xc, axis=-1, keepdims=True)
    rstd = lax.rsqrt(var + LN_EPS)
    return xc * rstd, rstd


def _proj_fwd(x, w_full, deps=()):
    T, D = x.shape
    P = w_full.shape[1]
    tm, tn = _tile(T, 1024), _tile(P, 1280)
    return _mm("proj_fwd", "nn", x, w_full, M=T, N=P, K=D, tm=tm, tn=tn, tk=D,
               outs=[(jax.ShapeDtypeStruct((T, P), F32), pl.BlockSpec((tm, tn), lambda i, j, k: (i, j)))],
               epilogue=None, deps=deps)[0]


def _cast_bf16(x, deps=()):
    T, D = x.shape
    tm = _tile(T, 512)

    def body(x_ref, *rest):
        rest[len(deps)][...] = x_ref[...].astype(BF16)

    return pl.pallas_call(
        body,
        name="cast_x",
        grid=(T // tm,),
        in_specs=[pl.BlockSpec((tm, D), lambda i: (i, 0))] + _hbm_specs(len(deps)),
        out_specs=pl.BlockSpec((tm, D), lambda i: (i, 0)),
        out_shape=jax.ShapeDtypeStruct((T, D), BF16),
        compiler_params=_params(1),
    )(x, *deps)


def _conv_shift(h, hp):
    row = lax.broadcasted_iota(jnp.int32, h.shape, 0)
    hm1 = hp[7:8, :]
    hm2 = hp[6:7, :]
    h1 = jnp.where(row == 0, hm1, pltpu.roll(h, 1, 0))
    h2 = jnp.where(row == 0, hm2, jnp.where(row == 1, hm1, pltpu.roll(h, 2, 0)))
    return h1, h2


def _conv_fwd(proj, conv_w8, conv_g):
    T = proj.shape[0]
    tt = _tile(T, 256)
    nt = T // tt
    t8 = tt // 8

    def body(b_ref, c_ref, u_ref, cp_ref, up_ref, w_ref, g_ref, yin_ref):
        i = pl.program_id(0)
        h = c_ref[...] * u_ref[...]
        hp = jnp.where(i > 0, cp_ref[...] * up_ref[...], 0.0)
        h1, h2 = _conv_shift(h, hp)
        w = w_ref[...]
        y = w[0:1, :] * h2 + w[1:2, :] * h1 + w[2:3, :] * h
        p = b_ref[...] * y
        parts = []
        for gi in range(CONV_GROUPS):
            pg = p[:, gi * LANE:(gi + 1) * LANE]
            r = lax.rsqrt(jnp.mean(pg * pg, axis=-1, keepdims=True) + RMS_EPS)
            parts.append(pg * r)
        yn = jnp.concatenate(parts, axis=1) * g_ref[...]
        yin_ref[...] = yn.astype(BF16)

    def col(cidx):
        return pl.BlockSpec((tt, D_CONV), lambda i: (i, cidx))

    def prev(cidx):
        return pl.BlockSpec((8, D_CONV), lambda i: (jnp.maximum(i * t8 - 1, 0), cidx))

    return pl.pallas_call(
        body,
        name="conv_fwd",
        grid=(nt,),
        in_specs=[col(0), col(1), col(2), prev(1), prev(2), _full((8, D_CONV)), _full((1, D_CONV))],
        out_specs=pl.BlockSpec((tt, D_CONV), lambda i: (i, 0)),
        out_shape=jax.ShapeDtypeStruct((T, 2 * D_CONV), BF16),
        compiler_params=_params(1),
    )(proj, proj, proj, proj, proj, conv_w8, conv_g)


def _log_sigmoid(z):
    return jnp.minimum(z, 0.0) - jnp.log(1.0 + jnp.exp(-jnp.abs(z)))


STEP_CHUNKS = 4
STEP_ROWS = STEP_CHUNKS * CHUNK


def _gla_step_terms(blk, wg_ref, gb_ref):
    zl = blk[:, 3072:3200]
    z = _dot(zl.astype(BF16), wg_ref[...], NN) + gb_ref[...]
    log_a = _log_sigmoid(z) * (1.0 / GATE_TAU)
    ri = lax.broadcasted_iota(jnp.int32, (STEP_ROWS, STEP_ROWS), 0)
    ci = lax.broadcasted_iota(jnp.int32, (STEP_ROWS, STEP_ROWS), 1)
    same = (ri // CHUNK) == (ci // CHUNK)
    lower = (same & (ri >= ci)).astype(F32)
    bcum = _dot(lower, log_a, NN, precision=lax.Precision.HIGHEST)
    return zl, z, bcum, same


def _causal():
    return (lax.broadcasted_iota(jnp.int32, (CHUNK, CHUNK), 0) >= lax.broadcasted_iota(jnp.int32, (CHUNK, CHUNK), 1))


def _gla_head_terms(q, k, bcum, h):
    sl = slice(h * HEAD_K, (h + 1) * HEAD_K)
    bh = bcum[:, sl]
    bl = bh[CHUNK - 1:CHUNK, :]
    eb = jnp.exp(bh)
    enb = jnp.exp(-bh)
    eend = jnp.exp(bl - bh)
    dec = jnp.exp(bl)
    qd = q[:, sl] * (HEAD_K ** -0.5) * eb
    ki = k[:, sl] * enb
    ke = k[:, sl] * eend
    return eb, enb, eend, dec, qd, ki, ke


def _sigmoid(x):
    return 1.0 / (1.0 + jnp.exp(-x))


def _gla_fwd(proj, wg128, gbias, gng, yin, deps=()):
    T = proj.shape[0]
    nch = T // CHUNK
    nst = T // STEP_ROWS

    def body(p_ref, wg_ref, gb_ref, gn_ref, yin_in_ref, *rest):
        o_ref, st_ref, yin_ref, s_ref = rest[len(deps):]
        n = pl.program_id(0)

        @pl.when(n == 0)
        def _():
            s_ref[...] = jnp.zeros_like(s_ref)

        blk = p_ref[...]
        _, _, bcum_all, _ = _gla_step_terms(blk, wg_ref, gb_ref)
        causal = _causal()
        gn = gn_ref[...]
        states = [s_ref[h] for h in range(GLA_HEADS)]
        for c in range(STEP_CHUNKS):
            rows = slice(c * CHUNK, (c + 1) * CHUNK)
            q, k = blk[rows, 0:512], blk[rows, 512:1024]
            v, r = blk[rows, 1024:2048], blk[rows, 2048:3072]
            bcum = bcum_all[rows, :]
            for h in range(GLA_HEADS):
                _, _, _, dec, qd, ki, ke = _gla_head_terms(q, k, bcum, h)
                vs = slice(h * HEAD_V, (h + 1) * HEAD_V)
                vb = v[:, vs].astype(BF16)
                qdb = qd.astype(BF16)
                a = jnp.where(causal, _dot(qdb, ki.astype(BF16), NT), 0.0)
                st = states[h]
                o = _dot(a.astype(BF16), vb, NN) + _dot(qdb, st.astype(BF16), NT)
                st_ref[c, h] = st
                states[h] = dec * st + _dot(vb, ke.astype(BF16), TN)
                o_ref[rows, vs] = o
                rinv = lax.rsqrt(jnp.mean(o * o, axis=-1, keepdims=True) + RMS_EPS)
                rh = r[:, vs]
                yin_ref[rows, vs] = (o * rinv * gn[:, vs] * (rh * _sigmoid(rh))).astype(BF16)
        for h in range(GLA_HEADS):
            s_ref[h] = states[h]

    return pl.pallas_call(
        body,
        name="gla_fwd",
        grid=(nst,),
        in_specs=[pl.BlockSpec((STEP_ROWS, HALF_P), lambda n: (n, 1)), _full((LANE, D_GLA_K)), _full((1, D_GLA_K)),
                  _full((1, D_GLA_V)), pl.BlockSpec(memory_space=pl.ANY)] + _hbm_specs(len(deps)),
        out_specs=[pl.BlockSpec((STEP_ROWS, D_GLA_V), lambda n: (n, 0)),
                   pl.BlockSpec((STEP_CHUNKS, GLA_HEADS, HEAD_V, HEAD_K), lambda n: (n, 0, 0, 0)),
                   pl.BlockSpec((STEP_ROWS, D_GLA_V), lambda n: (n, 1))],
        out_shape=[jax.ShapeDtypeStruct((T, D_GLA_V), F32),
                   jax.ShapeDtypeStruct((nch, GLA_HEADS, HEAD_V, HEAD_K), F32),
                   jax.ShapeDtypeStruct(yin.shape, BF16)],
        scratch_shapes=[pltpu.VMEM((GLA_HEADS, HEAD_V, HEAD_K), F32)],
        input_output_aliases={4: 2},
        compiler_params=_params(1),
    )(proj, wg128, gbias, gng, yin, *deps)


def _mix_ln1(yin, w_out, x, ln_g, ln_b, deps=()):
    T, D = x.shape
    KY = yin.shape[1]
    tm = _tile(T, 1024)

    def chunk(acc, valid, ri, vi, ro, so):
        g, b = vi[0][...], vi[1][...]

        def sub(rows):
            xhat, rstd = _ln_fwd(DN_ALPHA * ri[0][rows, :] + acc[rows, :])
            ro[0][rows, :] = xhat
            ro[1][rows, :] = (xhat * g + b).astype(BF16)
            ro[2][rows, :] = rstd

        _by_sub_rows(acc.shape[0], sub)

    return _mm_rows("mix_ln1", "nn", yin, w_out, M=T, N=D, K=KY, tm=tm, tk=_tile(KY, 512),
                    row_ins=[x], vec_ins=[ln_g, ln_b],
                    row_outs=[jax.ShapeDtypeStruct((T, D), F32), jax.ShapeDtypeStruct((T, D), BF16),
                              jax.ShapeDtypeStruct((T, 1), F32)],
                    stat_outs=[], chunk_fn=chunk, deps=deps)


def _ff_up(x1, w_up_blk, first, count, prev=None, deps=()):
    T, D = x1.shape
    nb, _, fb = w_up_blk.shape
    tm = _tile(T, 1024)
    ni = T // tm
    n_dep = len(deps) + (2 if prev is not None else 0)

    def body(a_ref, b_ref, *rest):
        ra_ref, h2_ref = rest[n_dep:n_dep + 2]
        ra = jnp.maximum(_dot(a_ref[...], b_ref[...], NN), 0.0)
        ra_ref[...] = ra.astype(BF16)
        h2_ref[...] = (ra * ra).astype(BF16)

    blk = pl.BlockSpec((tm, fb), lambda i, j: (i, first + j))
    shp = jax.ShapeDtypeStruct((T, nb * fb), BF16)
    keep = list(prev) if prev is not None else []
    return pl.pallas_call(
        body,
        name="ff_up_%d" % first,
        grid=(ni, count),
        in_specs=[pl.BlockSpec((tm, D), lambda i, j: (i, 0)),
                  pl.BlockSpec((None, D, fb), lambda i, j: (first + j, 0, 0))] + _hbm_specs(n_dep),
        out_specs=[blk, blk],
        out_shape=[shp, shp],
        input_output_aliases=({2: 0, 3: 1} if prev is not None else {}),
        compiler_params=_params(2),
    )(x1, w_up_blk, *keep, *deps)


def _ff_down_loss(h2, w_down, xhat1, target, g1, b1, g2, b2):
    T, F = h2.shape
    D = w_down.shape[1]
    tm = _tile(T, 1024)
    inv_d = 1.0 / D

    def chunk(acc, valid, ri, vi, ro, so):
        g1v, b1v, g2v, b2v = (v[...] for v in vi)

        def sub(rows):
            x1 = ri[0][rows, :] * g1v + b1v
            xhat, rstd = _ln_fwd(DN_ALPHA * x1 + acc[rows, :])
            e = xhat * g2v + b2v - ri[1][rows, :]
            dy = e * inv_d
            dh = _ln_bwd(dy, xhat, rstd, g2v)
            ro[0][rows, :] = dh
            ro[1][rows, :] = dh.astype(BF16)
            return (jnp.sum(dy * xhat, axis=0, keepdims=True), jnp.sum(dy, axis=0, keepdims=True),
                    jnp.sum(e * e, axis=0, keepdims=True))

        sg, sb, sl = _by_sub_rows(acc.shape[0], sub)
        so[0][...] += jnp.where(valid, sg, 0.0)
        so[1][...] += jnp.where(valid, sb, 0.0)
        so[2][...] += jnp.where(valid, sl * (0.5 * inv_d), 0.0)

    vshape = jax.ShapeDtypeStruct((1, D), F32)
    return _mm_rows("ff_down_loss", "nn", h2, w_down, M=T, N=D, K=F, tm=tm, tk=_tile(F, 1024),
                    row_ins=[xhat1, target], vec_ins=[g1, b1, g2, b2],
                    row_outs=[jax.ShapeDtypeStruct((T, D), F32), jax.ShapeDtypeStruct((T, D), BF16)],
                    stat_outs=[vshape, vshape, vshape], chunk_fn=chunk)


def _ff_down_bwd_act(dh3b, w_down, ra):
    T, D = dh3b.shape
    F = w_down.shape[0]
    tm, tn = _tile(T, 1024), _tile(F, 1024)

    def ep(acc_ref, ex, o, i, j):
        o[0][...] = (acc_ref[...] * (2.0 * ex[0][...].astype(F32))).astype(BF16)

    blk = pl.BlockSpec((tm, tn), lambda i, j, k: (i, j))
    return _mm("ff_down_bwd_act", "nt", dh3b, w_down, M=T, N=F, K=D, tm=tm, tn=tn, tk=D,
               outs=[(jax.ShapeDtypeStruct((T, F), BF16), blk)], extras=[(ra, blk)], epilogue=ep)[0]


def _grad_w(name, a, b, *, a_fn=None, tm_pref=1024, tn_pref=1024, tk_pref=4096, deps=()):
    T, M = a.shape
    N = b.shape[1]
    tm, tn, tk = _tile(M, tm_pref), _tile(N, tn_pref), _tile(T, tk_pref)
    return _mm(name, "tn", a, b, M=M, N=N, K=T, tm=tm, tn=tn, tk=tk, a_fn=a_fn, deps=deps,
               outs=[(jax.ShapeDtypeStruct((M, N), F32), pl.BlockSpec((tm, tn), lambda i, j, k: (i, j)))],
               epilogue=None)[0]


def _grad_w_up_blk(x1, da, nb, deps=()):
    T, D = x1.shape
    F = da.shape[1]
    fb = F // nb
    tm, tk = _tile(D, 1024), _tile(T, 4096)
    return _mm("grad_w_up", "tn", x1, da, M=D, N=F, K=T, tm=tm, tn=fb, tk=tk, deps=deps,
               outs=[(jax.ShapeDtypeStruct((nb, D, fb), F32),
                      pl.BlockSpec((None, tm, fb), lambda i, j, k: (j, i, 0)))],
               epilogue=None)[0]


def _ff_up_bwd_ln1(da, w_up_blk, dh3, xhat1, rstd1, g1, deps=()):
    T, F = da.shape
    nb, D, fb = w_up_blk.shape
    tm = _tile(T, 1024)

    def chunk(acc, valid, ri, vi, ro, so):
        g = vi[0][...]

        def sub(rows):
            dx1 = DN_ALPHA * ri[0][rows, :] + acc[rows, :]
            xhat = ri[1][rows, :]
            dh = _ln_bwd(dx1, xhat, ri[2][rows, :], g)
            ro[0][rows, :] = dh
            ro[1][rows, :] = dh.astype(BF16)
            return jnp.sum(dx1 * xhat, axis=0, keepdims=True), jnp.sum(dx1, axis=0, keepdims=True)

        sg, sb = _by_sub_rows(acc.shape[0], sub)
        so[0][...] += jnp.where(valid, sg, 0.0)
        so[1][...] += jnp.where(valid, sb, 0.0)

    nk = F // fb
    vshape = jax.ShapeDtypeStruct((1, D), F32)
    return _mm_rows("ff_up_bwd_ln1", "nt", da, w_up_blk, M=T, N=D, K=F, tm=tm, tk=fb,
                    b_spec=pl.BlockSpec((None, D, fb), lambda i, k: (jnp.where(i < T // tm, k, nk - 1), 0, 0)),
                    row_ins=[dh3, xhat1, rstd1], vec_ins=[g1],
                    row_outs=[jax.ShapeDtypeStruct((T, D), F32), jax.ShapeDtypeStruct((T, D), BF16)],
                    stat_outs=[vshape, vshape], chunk_fn=chunk, deps=deps)


def _mix_bwd(dh1b, w_out, deps=()):
    T, D = dh1b.shape
    KY = w_out.shape[0]
    tm, tn = _tile(T, 1024), _tile(KY, 1024)
    return _mm("mix_bwd", "nt", dh1b, w_out, M=T, N=KY, K=D, tm=tm, tn=tn, tk=D, deps=deps,
               outs=[(jax.ShapeDtypeStruct((T, KY), F32), pl.BlockSpec((tm, tn), lambda i, j, k: (i, j)))],
               epilogue=None)[0]


def _conv_bwd(proj, dyin, conv_w8, conv_g, deps=()):
    T = proj.shape[0]
    tt = _tile(T, 256)
    nt = T // tt
    t8 = tt // 8
    nx = tt + 8

    def body(b_ref, c_ref, u_ref, d_ref, bn_ref, cn_ref, un_ref, dn_ref, cp_ref, up_ref, w_ref, g_ref, *rest):
        dp_ref, dw_ref, dg_ref = rest[len(deps):]
        i = pl.program_id(0)

        @pl.when(i == 0)
        def _():
            dw_ref[...] = jnp.zeros_like(dw_ref)
            dg_ref[...] = jnp.zeros_like(dg_ref)

        more = i < nt - 1

        def ext(cur_ref, nxt_ref):
            return jnp.concatenate([cur_ref[...], jnp.where(more, nxt_ref[...], 0.0)], axis=0)

        bx, cx, ux, dx = ext(b_ref, bn_ref), ext(c_ref, cn_ref), ext(u_ref, un_ref), ext(d_ref, dn_ref)
        hx = cx * ux
        hp = jnp.where(i > 0, cp_ref[...] * up_ref[...], 0.0)
        h1, h2 = _conv_shift(hx, hp)
        w = w_ref[...]
        g = g_ref[...]
        yx = w[0:1, :] * h2 + w[1:2, :] * h1 + w[2:3, :] * hx
        px = bx * yx
        dps, dgs = [], []
        for gi in range(CONV_GROUPS):
            sl = slice(gi * LANE, (gi + 1) * LANE)
            pg, dg_ = px[:, sl], dx[:, sl]
            r = lax.rsqrt(jnp.mean(pg * pg, axis=-1, keepdims=True) + RMS_EPS)
            gd = g[:, sl] * dg_
            dps.append(r * gd - pg * (r * r * r) * jnp.mean(pg * gd, axis=-1, keepdims=True))
            dgs.append(jnp.sum((dg_ * pg * r)[:tt, :], axis=0, keepdims=True))
        dpx = jnp.concatenate(dps, axis=1)
        dg_ref[...] += jnp.concatenate(dgs, axis=1)
        dyx = dpx * bx
        dyc = dyx[:tt, :]
        dh = (w[2:3, :] * dyx + w[1:2, :] * pltpu.roll(dyx, nx - 1, 0) + w[0:1, :] * pltpu.roll(dyx, nx - 2, 0))[:tt, :]
        dw_ref[0:1, :] += jnp.sum(dyc * h2[:tt, :], axis=0, keepdims=True)
        dw_ref[1:2, :] += jnp.sum(dyc * h1[:tt, :], axis=0, keepdims=True)
        dw_ref[2:3, :] += jnp.sum(dyc * hx[:tt, :], axis=0, keepdims=True)
        dp_ref[:, 0:D_CONV] = (dpx * yx)[:tt, :].astype(BF16)
        dp_ref[:, D_CONV:2 * D_CONV] = (dh * u_ref[...]).astype(BF16)
        dp_ref[:, 2 * D_CONV:3 * D_CONV] = (dh * c_ref[...]).astype(BF16)
        dp_ref[:, 3 * D_CONV:HALF_P] = jnp.zeros((tt, HALF_P - 3 * D_CONV), BF16)

    def col(cidx):
        return pl.BlockSpec((tt, D_CONV), lambda i: (i, cidx))

    def nxt(cidx):
        return pl.BlockSpec((8, D_CONV), lambda i: (jnp.minimum((i + 1) * t8, T // 8 - 1), cidx))

    def prev(cidx):
        return pl.BlockSpec((8, D_CONV), lambda i: (jnp.maximum(i * t8 - 1, 0), cidx))

    return pl.pallas_call(
        body,
        name="conv_bwd",
        grid=(nt,),
        in_specs=[col(0), col(1), col(2), col(0), nxt(0), nxt(1), nxt(2), nxt(0), prev(1), prev(2),
                  _full((8, D_CONV)), _full((1, D_CONV))] + _hbm_specs(len(deps)),
        out_specs=[pl.BlockSpec((tt, HALF_P), lambda i: (i, 0)), _full((8, D_CONV)), _full((1, D_CONV))],
        out_shape=[jax.ShapeDtypeStruct((T, P_INT), BF16), jax.ShapeDtypeStruct((8, D_CONV), F32),
                   jax.ShapeDtypeStruct((1, D_CONV), F32)],
        compiler_params=_params(1),
    )(proj, proj, proj, dyin, proj, proj, proj, dyin, proj, proj, conv_w8, conv_g, *deps)


def _gla_bwd(proj, wg128, gbias, gng, o_all, states, dyin, dproj):
    T = proj.shape[0]
    nst = T // STEP_ROWS

    def body(p_ref, wg_ref, gb_ref, gn_ref, o_ref, st_ref, d_ref, dp_in_ref,
             dp_ref, dwg_ref, dgb_ref, dgn_ref, ds_ref):
        n = pl.program_id(0)

        @pl.when(n == 0)
        def _():
            ds_ref[...] = jnp.zeros_like(ds_ref)
            dwg_ref[...] = jnp.zeros_like(dwg_ref)
            dgb_ref[...] = jnp.zeros_like(dgb_ref)
            dgn_ref[...] = jnp.zeros_like(dgn_ref)

        blk = p_ref[...]
        zl, z, bcum_all, same = _gla_step_terms(blk, wg_ref, gb_ref)
        causal = _causal()
        gn = gn_ref[...]
        ri = lax.broadcasted_iota(jnp.int32, (STEP_ROWS, STEP_ROWS), 0)
        ci = lax.broadcasted_iota(jnp.int32, (STEP_ROWS, STEP_ROWS), 1)
        upper = (same & (ri <= ci)).astype(F32)
        dstates = [ds_ref[h] for h in range(GLA_HEADS)]
        db_rows, dbl_rows, dgn_sum = [None] * STEP_CHUNKS, [None] * STEP_CHUNKS, [None] * GLA_HEADS
        for c in reversed(range(STEP_CHUNKS)):
            rows = slice(c * CHUNK, (c + 1) * CHUNK)
            q, k = blk[rows, 0:512], blk[rows, 512:1024]
            v, r = blk[rows, 1024:2048], blk[rows, 2048:3072]
            bcum = bcum_all[rows, :]
            db_parts, dbl_parts = [], []
            for h in range(GLA_HEADS):
                eb, enb, eend, dec, qd, ki, ke = _gla_head_terms(q, k, bcum, h)
                vs = slice(h * HEAD_V, (h + 1) * HEAD_V)
                ks = slice(h * HEAD_K, (h + 1) * HEAD_K)
                o = o_ref[rows, vs]
                rh = r[:, vs]
                dyg = d_ref[rows, vs]
                rinv = lax.rsqrt(jnp.mean(o * o, axis=-1, keepdims=True) + RMS_EPS)
                sg = _sigmoid(rh)
                on = o * rinv
                dr = dyg * (on * gn[:, vs]) * (sg * (1.0 + rh * (1.0 - sg)))
                don = dyg * (rh * sg)
                part = jnp.sum(don * on, axis=0, keepdims=True)
                dgn_sum[h] = part if dgn_sum[h] is None else dgn_sum[h] + part
                t = don * gn[:, vs]
                do = rinv * t - o * (rinv * rinv * rinv) * jnp.mean(o * t, axis=-1, keepdims=True)
                dob = do.astype(BF16)
                vb = v[:, vs].astype(BF16)
                qdb, kib, keb = qd.astype(BF16), ki.astype(BF16), ke.astype(BF16)
                a = jnp.where(causal, _dot(qdb, kib, NT), 0.0)
                st = st_ref[c, h]
                dst = dstates[h]
                dstb = dst.astype(BF16)
                da = jnp.where(causal, _dot(dob, vb, NT), 0.0)
                dab = da.astype(BF16)
                dv = _dot(a.astype(BF16), dob, TN) + _dot(keb, dstb, NT)
                dqd = _dot(dab, kib, NN) + _dot(dob, st.astype(BF16), NN)
                dki = _dot(dab, qdb, TN)
                dke = _dot(vb, dstb, NN)
                ddec = jnp.sum(st * dst, axis=0, keepdims=True)
                dstates[h] = dec * dst + _dot(dob, qdb, TN)
                dq = dqd * eb * (HEAD_K ** -0.5)
                dk = dki * enb + dke * eend
                db_parts.append(dqd * qd - dki * ki - dke * ke)
                dbl_parts.append(jnp.sum(dke * ke, axis=0, keepdims=True) + dec * ddec)
                dp_ref[rows, ks] = dq.astype(BF16)
                dp_ref[rows, D_GLA_K + h * HEAD_K:D_GLA_K + (h + 1) * HEAD_K] = dk.astype(BF16)
                dp_ref[rows, 1024 + h * HEAD_V:1024 + (h + 1) * HEAD_V] = dv.astype(BF16)
                dp_ref[rows, 2048 + h * HEAD_V:2048 + (h + 1) * HEAD_V] = dr.astype(BF16)
            db_rows[c] = jnp.concatenate(db_parts, axis=1)
            dbl_rows[c] = jnp.broadcast_to(jnp.concatenate(dbl_parts, axis=1), (CHUNK, D_GLA_K))
        for h in range(GLA_HEADS):
            ds_ref[h] = dstates[h]
            dgn_ref[:, h * HEAD_V:(h + 1) * HEAD_V] += dgn_sum[h]
        db = jnp.concatenate(db_rows, axis=0)
        dlog = _dot(upper, db, NN, precision=lax.Precision.HIGHEST) + jnp.concatenate(dbl_rows, axis=0)
        dz = dlog * (1.0 / GATE_TAU) * (1.0 / (1.0 + jnp.exp(z)))
        dzb = dz.astype(BF16)
        dp_ref[:, 3072:3200] = _dot(dzb, wg_ref[...], NT).astype(BF16)
        dwg_ref[...] += _dot(zl.astype(BF16), dzb, TN)
        dgb_ref[...] += jnp.sum(dz, axis=0, keepdims=True)

    rev = lambda n: nst - 1 - n
    return pl.pallas_call(
        body,
        name="gla_bwd",
        grid=(nst,),
        in_specs=[pl.BlockSpec((STEP_ROWS, HALF_P), lambda n: (rev(n), 1)), _full((LANE, D_GLA_K)),
                  _full((1, D_GLA_K)), _full((1, D_GLA_V)),
                  pl.BlockSpec((STEP_ROWS, D_GLA_V), lambda n: (rev(n), 0)),
                  pl.BlockSpec((STEP_CHUNKS, GLA_HEADS, HEAD_V, HEAD_K), lambda n: (rev(n), 0, 0, 0)),
                  pl.BlockSpec((STEP_ROWS, D_GLA_V), lambda n: (rev(n), 1)), pl.BlockSpec(memory_space=pl.ANY)],
        out_specs=[pl.BlockSpec((STEP_ROWS, HALF_P), lambda n: (rev(n), 1)), _full((LANE, D_GLA_K)),
                   _full((1, D_GLA_K)), _full((1, D_GLA_V))],
        out_shape=[jax.ShapeDtypeStruct(dproj.shape, BF16), jax.ShapeDtypeStruct((LANE, D_GLA_K), F32),
                   jax.ShapeDtypeStruct((1, D_GLA_K), F32), jax.ShapeDtypeStruct((1, D_GLA_V), F32)],
        scratch_shapes=[pltpu.VMEM((GLA_HEADS, HEAD_V, HEAD_K), F32)],
        input_output_aliases={7: 0},
        compiler_params=_params(1),
    )(proj, wg128, gbias, gng, o_all, states, dyin, dproj)


def _proj_bwd_x(dproj, w_full, dh1, deps=()):
    T, P = dproj.shape
    D = w_full.shape[0]
    tm, tk = _tile(T, 512), _tile(P, 1280)

    def ep(acc_ref, ex, o, i, j):
        o[0][...] = DN_ALPHA * ex[0][...] + acc_ref[...]

    row = pl.BlockSpec((tm, D), lambda i, j, k: (i, 0))
    return _mm("proj_bwd_x", "nt", dproj, w_full, M=T, N=D, K=P, tm=tm, tn=D, tk=tk,
               outs=[(jax.ShapeDtypeStruct((T, D), F32), row)], extras=[(dh1, row)], epilogue=ep, deps=deps)[0]


def _place():
    x, y, c = lax.axis_index("x"), lax.axis_index("y"), lax.axis_index("c")
    chips = [(1 - x, y), (x, 1 - y), (1 - x, 1 - y)]
    return x, y, c, chips


def _rcopy(src, dst, ssem, rsem, dev):
    return pltpu.make_async_remote_copy(src_ref=src, dst_ref=dst, send_sem=ssem, recv_sem=rsem,
                                        device_id=dev, device_id_type=MESH)


def _all_gather(name, shards, deps=()):
    n = len(shards)

    def body(*refs):
        ins, outs = refs[:n], refs[n + len(deps):2 * n + len(deps)]
        ssem, rsem, lsem = refs[2 * n + len(deps):]
        x, y, c, chips = _place()
        me, sib = (x, y, c), (x, y, 1 - c)

        def slot(w, px, py, pc):
            return outs[w].at[4 * px + 2 * py + pc]

        started = []
        for w in range(n):
            lc = pltpu.make_async_copy(ins[w], slot(w, *me), lsem.at[w])
            lc.start()
            started.append(lc)
        sends = []
        for w in range(n):
            cp = _rcopy(ins[w], slot(w, *me), ssem.at[7 * w], rsem.at[7 * w], sib)
            cp.start()
            sends.append(cp)
            for jx, chip in enumerate(chips):
                cp = _rcopy(ins[w], slot(w, *me), ssem.at[7 * w + 1 + jx], rsem.at[7 * w + 1 + jx], (*chip, c))
                cp.start()
                sends.append(cp)
        for w in range(n):
            for jx, chip in enumerate(chips):
                blk = slot(w, *chip, c)
                _rcopy(blk, blk, ssem.at[7 * w + 1 + jx], rsem.at[7 * w + 1 + jx], me).wait_recv()
                cp = _rcopy(blk, blk, ssem.at[7 * w + 4 + jx], rsem.at[7 * w + 4 + jx], sib)
                cp.start()
                sends.append(cp)
        for w in range(n):
            blk = slot(w, x, y, 1 - c)
            _rcopy(blk, blk, ssem.at[7 * w], rsem.at[7 * w], me).wait_recv()
            for jx, chip in enumerate(chips):
                blk = slot(w, *chip, 1 - c)
                _rcopy(blk, blk, ssem.at[7 * w + 4 + jx], rsem.at[7 * w + 4 + jx], me).wait_recv()
        for cp in sends:
            cp.wait_send()
        for lc in started:
            lc.wait()

    return pl.pallas_call(
        body,
        name=name,
        in_specs=_hbm_specs(n + len(deps)),
        out_specs=_hbm_specs(n),
        out_shape=[jax.ShapeDtypeStruct((N_DEV,) + s.shape, s.dtype) for s in shards],
        scratch_shapes=[pltpu.SemaphoreType.DMA((7 * n,)), pltpu.SemaphoreType.DMA((7 * n,)),
                        pltpu.SemaphoreType.DMA((n,))],
    )(*shards, *deps)


HBM_SPEC = pl.BlockSpec(memory_space=pltpu.HBM)
SEM_SPEC = pl.BlockSpec(memory_space=pltpu.SEMAPHORE)
SIDE_EFFECT = pltpu.SideEffectType.DATAFLOW_SIDE_EFFECTING


def _cast_place(name, ids, w, deps=(), dtype=None):
    dtype = BF16 if dtype is None else dtype
    R, C = w.shape
    tr = _tile(R, 256)

    def body(ids_ref, w_ref, *rest):
        rest[len(deps)][...] = w_ref[...].astype(dtype)

    return pl.pallas_call(
        body,
        name=name,
        grid_spec=pltpu.PrefetchScalarGridSpec(
            num_scalar_prefetch=1,
            grid=(R // tr,),
            in_specs=[pl.BlockSpec((tr, C), lambda r, ids: (r, 0))] + _hbm_specs(len(deps)),
            out_specs=pl.BlockSpec((None, tr, C), lambda r, ids: (ids[0], r, 0)),
        ),
        out_shape=jax.ShapeDtypeStruct((N_DEV, R, C), dtype),
        compiler_params=_params(1),
    )(ids, w, *deps)


def _xfer_start(name, bufs, plan, n):
    nb = len(bufs)

    def body(*refs):
        ins = refs[:nb]
        ssem, rsem = refs[nb], refs[nb + 1]
        token = refs[2 * nb + 2]
        x, y, c, chips = _place()
        for k, (src, dst, dev, _) in enumerate(plan(ins, x, y, c, chips)):
            _rcopy(src, dst, ssem.at[k], rsem.at[k], dev).start()
        token[...] = jnp.zeros_like(token)

    res = pl.pallas_call(
        body,
        name=name,
        out_shape=(pltpu.SemaphoreType.DMA((n,)), pltpu.SemaphoreType.DMA((n,)),
                   *[pltpu.HBM(b.shape, b.dtype) for b in bufs], jax.ShapeDtypeStruct((8, LANE), F32)),
        in_specs=[HBM_SPEC] * nb,
        out_specs=(SEM_SPEC, SEM_SPEC, *[HBM_SPEC] * nb, pl.BlockSpec(memory_space=pltpu.VMEM)),
        input_output_aliases={i: 2 + i for i in range(nb)},
        compiler_params=pltpu.CompilerParams(has_side_effects=SIDE_EFFECT),
    )(*[pltpu.with_memory_space_constraint(b, pltpu.HBM) for b in bufs])
    return dict(sems=res[:2], bufs=list(res[2:2 + nb]), token=res[2 + nb], plan=plan, n=n)


def _xfer_wait(name, started, after):
    bufs, plan = started["bufs"], started["plan"]
    nb = len(bufs)

    def body(*refs):
        ins = refs[:nb]
        ssem, rsem = refs[nb], refs[nb + 1]
        x, y, c, chips = _place()
        for k, (src, _, dev, land) in enumerate(plan(ins, x, y, c, chips)):
            cp = _rcopy(src, land, ssem.at[k], rsem.at[k], dev)
            cp.wait_send()
            cp.wait_recv()

    res = pl.pallas_call(
        body,
        name=name,
        out_shape=tuple(pltpu.HBM(b.shape, b.dtype) for b in bufs),
        in_specs=[HBM_SPEC] * nb + [SEM_SPEC, SEM_SPEC, pl.BlockSpec(memory_space=pl.ANY)],
        out_specs=tuple([HBM_SPEC] * nb),
        input_output_aliases={i: i for i in range(nb)},
        compiler_params=pltpu.CompilerParams(has_side_effects=SIDE_EFFECT),
    )(*bufs, *started["sems"], after)
    return list(res)


def _plan_gather_chips(refs, x, y, c, chips):
    plan = []
    for land in refs:
        mine = land.at[4 * x + 2 * y + c]
        plan.append((mine, mine, (x, y, 1 - c), land.at[4 * x + 2 * y + (1 - c)]))
        for px, py in chips:
            plan.append((mine, mine, (px, py, c), land.at[4 * px + 2 * py + c]))
    return plan


def _plan_gather_pass(refs, x, y, c, chips):
    return [(land.at[4 * px + 2 * py + c], land.at[4 * px + 2 * py + c], (x, y, 1 - c),
             land.at[4 * px + 2 * py + (1 - c)]) for land in refs for px, py in chips]


def _plan_reduce_core(refs, x, y, c, chips):
    grad, recv = refs
    return [(grad.at[2 * q + (1 - c)], recv.at[q], (x, y, 1 - c), recv.at[q]) for q in range(N_CHIP)]


def _plan_reduce_chips(refs, x, y, c, chips):
    part, land = refs
    return [(part.at[2 * px + py], land.at[2 * x + y], (px, py, c), land.at[2 * px + py]) for px, py in chips]


def _chip_sums(name, ids, grad, recv):
    _, R, C = grad.shape
    tr = _tile(R, 256)

    def body(ids_ref, g_ref, r_ref, o_ref):
        o_ref[...] = (g_ref[...] + r_ref[...]).astype(BF16)

    return pl.pallas_call(
        body,
        name=name,
        grid_spec=pltpu.PrefetchScalarGridSpec(
            num_scalar_prefetch=1,
            grid=(N_CHIP - 1, R // tr),
            in_specs=[pl.BlockSpec((None, tr, C), lambda q, r, ids: (2 * ids[3 + q] + ids[2], r, 0)),
                      pl.BlockSpec((None, tr, C), lambda q, r, ids: (ids[3 + q], r, 0))],
            out_specs=pl.BlockSpec((None, tr, C), lambda q, r, ids: (ids[3 + q], r, 0)),
        ),
        out_shape=jax.ShapeDtypeStruct((N_CHIP, R, C), BF16),
        compiler_params=_params(2),
    )(ids, grad, recv)


def _adamw(w, g, m, v):
    m = ADAM_B1 * m + (1.0 - ADAM_B1) * g
    v = ADAM_B2 * v + (1.0 - ADAM_B2) * (g * g)
    m_hat = m / (1.0 - ADAM_B1 ** ADAM_STEP)
    v_hat = v / (1.0 - ADAM_B2 ** ADAM_STEP)
    delta = -ADAM_LR * (m_hat / (jnp.sqrt(v_hat) + ADAM_EPS) + ADAM_WD * w)
    return delta, m, v


def _reduce_adamw(name, ids, grad, recv, landed, w, m, v):
    _, R, C = grad.shape
    tr = _tile(R, 256)

    def body(ids_ref, g_ref, r_ref, l1_ref, l2_ref, l3_ref, w_ref, m_ref, v_ref, go_ref, do_ref, mo_ref, vo_ref):
        g = g_ref[...] + r_ref[...]
        g = g + l1_ref[...].astype(F32)
        g = g + l2_ref[...].astype(F32)
        g = g + l3_ref[...].astype(F32)
        delta, mn, vn = _adamw(w_ref[...], g, m_ref[...], v_ref[...])
        go_ref[...] = g
        do_ref[...] = delta
        mo_ref[...] = mn
        vo_ref[...] = vn

    def pick(k):
        return pl.BlockSpec((None, tr, C), lambda r, ids: (ids[k], r, 0))

    flat = pl.BlockSpec((tr, C), lambda r, ids: (r, 0))
    shp = jax.ShapeDtypeStruct((R, C), F32)
    return pl.pallas_call(
        body,
        name=name,
        grid_spec=pltpu.PrefetchScalarGridSpec(
            num_scalar_prefetch=1,
            grid=(R // tr,),
            in_specs=[pick(0), pick(1), pick(3), pick(4), pick(5), flat, flat, flat],
            out_specs=[flat, flat, flat, flat],
        ),
        out_shape=[shp, shp, shp, shp],
        compiler_params=_params(1),
    )(ids, grad, recv, landed, landed, landed, w, m, v)


def _small_adamw(packs, w, m, v):
    def body(p_ref, w_ref, m_ref, v_ref, g_ref, d_ref, mo_ref, vo_ref):
        g = p_ref[0]
        for dvc in range(1, N_DEV):
            g = g + p_ref[dvc]
        delta, mn, vn = _adamw(w_ref[...], g, m_ref[...], v_ref[...])
        g_ref[...] = g
        d_ref[...] = delta
        mo_ref[...] = mn
        vo_ref[...] = vn

    shp = jax.ShapeDtypeStruct(w.shape, F32)
    return pl.pallas_call(
        body,
        name="small_adamw",
        in_specs=[_full(packs.shape), _full(w.shape), _full(w.shape), _full(w.shape)],
        out_specs=[_full(w.shape)] * 4,
        out_shape=[shp] * 4,
        grid=(1,),
        compiler_params=_params(1),
    )(packs, w, m, v)


def _w_in_pieces():
    cs = D_IN_PROJ // N_DEV
    pieces = []
    for d in range(N_DEV):
        lo, hi = d * cs, (d + 1) * cs
        if hi <= CONV_COLS:
            pieces.append((d, 0, cs, lo))
        elif lo >= CONV_COLS:
            pieces.append((d, 0, cs, lo - CONV_COLS + HALF_P))
        else:
            pieces.append((d, 0, CONV_COLS - lo, lo))
            pieces.append((d, CONV_COLS - lo, cs, HALF_P))
    return pieces


def _w_in_full(gathered):
    nb, D, cs = gathered.shape
    tr = _tile(D, 256)

    def body(g_ref, o_ref):
        o_ref[:, CONV_COLS:HALF_P] = jnp.zeros((tr, HALF_P - CONV_COLS), o_ref.dtype)
        o_ref[:, HALF_P + GLA_COLS:P_INT] = jnp.zeros((tr, HALF_P - GLA_COLS), o_ref.dtype)
        for d, a, b, dst in _w_in_pieces():
            o_ref[:, dst:dst + (b - a)] = g_ref[d, :, a:b]

    return pl.pallas_call(
        body,
        name="w_in_full",
        grid=(D // tr,),
        in_specs=[pl.BlockSpec((nb, tr, cs), lambda r: (0, r, 0))],
        out_specs=pl.BlockSpec((tr, P_INT), lambda r: (r, 0)),
        out_shape=jax.ShapeDtypeStruct((D, P_INT), gathered.dtype),
        compiler_params=_params(1),
    )(gathered)


def _w_in_blocks(dw):
    D = dw.shape[0]
    cs = D_IN_PROJ // N_DEV
    tr = _tile(D, 256)

    def body(w_ref, o_ref):
        for d, a, b, src in _w_in_pieces():
            o_ref[d, :, a:b] = w_ref[:, src:src + (b - a)]

    return pl.pallas_call(
        body,
        name="w_in_blocks",
        grid=(D // tr,),
        in_specs=[pl.BlockSpec((tr, P_INT), lambda r: (r, 0))],
        out_specs=pl.BlockSpec((N_DEV, tr, cs), lambda r: (0, r, 0)),
        out_shape=jax.ShapeDtypeStruct((N_DEV, D, cs), dw.dtype),
        compiler_params=_params(1),
    )(dw)


def _rows(vec, n_rows):
    flat = jnp.pad(vec.reshape(-1), (0, n_rows * SP_COLS - vec.size))
    return flat.reshape(n_rows, SP_COLS)


def _pad_cols(a):
    return jnp.pad(a, ((0, 0), (0, SP_COLS - a.shape[1])))


R_CONV_W, R_CONV_G, R_GATE_B, R_GLA_G, R_LN1_G, R_LN1_B, R_LN2_G, R_LN2_B, R_LOSS, R_GATE_W = 0, 3, 4, 5, 6, 8, 10, 12, 14, 16


def _pack(conv_w, conv_g, gate_b, gla_g, ln1_g, ln1_b, ln2_g, ln2_b, loss, gate_w):
    z = jnp.zeros((1, SP_COLS), F32)
    parts = [_pad_cols(conv_w), _pad_cols(conv_g), _pad_cols(gate_b), _pad_cols(gla_g),
             _rows(ln1_g, 2), _rows(ln1_b, 2), _rows(ln2_g, 2), _rows(ln2_b, 2),
             z if loss is None else _pad_cols(jnp.sum(loss, axis=1, keepdims=True)), z, _pad_cols(gate_w)]
    return jnp.concatenate(parts, axis=0)


def _unpack(p, D, conv_cols, gate_cols):
    return dict(
        conv_w=p[R_CONV_W:R_CONV_W + 3, :conv_cols], conv_norm_g=p[R_CONV_G:R_CONV_G + 1, :D_CONV],
        gate_bias=p[R_GATE_B:R_GATE_B + 1, :D_GLA_K], gla_norm_g=p[R_GLA_G:R_GLA_G + 1, :D_GLA_V],
        ln1_g=p[R_LN1_G:R_LN1_G + 2].reshape(1, -1)[:, :D], ln1_b=p[R_LN1_B:R_LN1_B + 2].reshape(1, -1)[:, :D],
        ln2_g=p[R_LN2_G:R_LN2_G + 2].reshape(1, -1)[:, :D], ln2_b=p[R_LN2_B:R_LN2_B + 2].reshape(1, -1)[:, :D],
        w_gate_up=p[R_GATE_W:R_GATE_W + GATE_RANK, :gate_cols])


BIG = ("w_in", "w_out", "w_ff_up", "w_ff_down")
ORDER = ("w_in", "conv_w", "conv_norm_g", "w_gate_up", "gate_bias", "gla_norm_g", "w_out", "ln1_g", "ln1_b",
         "w_ff_up", "w_ff_down", "ln2_g", "ln2_b")


def kernel(x, w_in, conv_w, conv_norm_g, w_gate_up, gate_bias, gla_norm_g, w_out, ln1_g, ln1_b, w_ff_up, w_ff_down, ln2_g, ln2_b, loss_target, m_w_in, m_conv_w, m_conv_norm_g, m_w_gate_up, m_gate_bias, m_gla_norm_g, m_w_out, m_ln1_g, m_ln1_b, m_w_ff_up, m_w_ff_down, m_ln2_g, m_ln2_b, v_w_in, v_conv_w, v_conv_norm_g, v_w_gate_up, v_gate_bias, v_gla_norm_g, v_w_out, v_ln1_g, v_ln1_b, v_w_ff_up, v_w_ff_down, v_ln2_g, v_ln2_b):
    T, D = x.shape[1], x.shape[2]
    xs, target = x[0], loss_target[0]
    xi, yi, ci = lax.axis_index("x"), lax.axis_index("y"), lax.axis_index("c")
    chip = 2 * xi + yi
    dev = 2 * chip + ci
    others = [jnp.where(chip <= q, q + 1, q) for q in range(N_CHIP - 1)]
    ids = jnp.stack([dev, chip, ci] + others).astype(jnp.int32)
    conv_cols, gate_cols = conv_w.shape[2], w_gate_up.shape[2]

    def gather(nm, lands):
        return _xfer_start("gather_chips_" + nm, lands, _plan_gather_chips, 4 * len(lands))

    def pass_on(nm, started, after):
        lands = _xfer_wait("gather_chips_wait_" + nm, started, after)
        return _xfer_start("gather_pass_" + nm, lands, _plan_gather_pass, 3 * len(lands))

    def landed(nm, started, after):
        return _xfer_wait("gather_pass_wait_" + nm, started, after)

    z1 = jnp.zeros((1, 1), F32)
    fwd_pack = _pack(conv_w[0], z1, z1, z1, z1, z1, z1, z1, None, w_gate_up[0])
    ga_in = gather("w_in", [_cast_place("cast_place_w_in", ids, w_in[0]),
                            _cast_place("cast_place_pack", ids, fwd_pack, dtype=F32)])
    ga, dep = [], ga_in["token"]
    m_in, v_in = m_w_in[0], v_w_in[0]
    for nm, w in zip(BIG[1:], (w_out, w_ff_up, w_ff_down)):
        deps = [dep, m_in, v_in] if nm == "w_ff_down" else [dep]
        ga.append(gather(nm, [_cast_place("cast_place_" + nm, ids, w[0], deps)]))
        dep = ga[-1]["token"]
    xb = _cast_bf16(xs, [dep])
    gp_in = pass_on("w_in", ga_in, xb)
    g_in, g_pack = landed("w_in", gp_in, gp_in["token"])
    w_full = _w_in_full(g_in)
    conv_w_full = g_pack[:, R_CONV_W:R_CONV_W + 3, :conv_cols].transpose(1, 0, 2).reshape(3, -1)
    gate_w_full = g_pack[:, R_GATE_W:R_GATE_W + GATE_RANK, :gate_cols].transpose(1, 0, 2).reshape(GATE_RANK, -1)
    conv_w8 = jnp.pad(conv_w_full, ((0, 5), (0, 0)))
    wg128 = jnp.pad(gate_w_full, ((0, LANE - GATE_RANK), (0, 0))).astype(BF16)
    proj = _proj_fwd(xb, w_full)
    yin = _conv_fwd(proj, conv_w8, conv_norm_g)
    gp_out = pass_on("w_out", ga[0], yin)
    o_all, states, yin = _gla_fwd(proj, wg128, gate_bias, gla_norm_g, yin, deps=[gp_out["token"]])
    w_out_full = landed("w_out", gp_out, o_all)[0].reshape(-1, D)
    gp_up = pass_on("w_ff_up", ga[1], o_all)
    xhat1, x1, rstd1 = _mix_ln1(yin, w_out_full, xs, ln1_g, ln1_b, deps=[gp_up["token"]])
    (w_up_blk,) = landed("w_ff_up", gp_up, x1)
    half = N_DEV // 2
    ra, h2 = _ff_up(x1, w_up_blk, 0, half)
    gp_down = pass_on("w_ff_down", ga[2], ra)
    ra, h2 = _ff_up(x1, w_up_blk, half, N_DEV - half, prev=(ra, h2), deps=[gp_down["token"]])
    w_down_full = landed("w_ff_down", gp_down, ra)[0].reshape(-1, D)
    dh3, dh3b, g_ln2_g, g_ln2_b, loss = _ff_down_loss(h2, w_down_full, xhat1, target, ln1_g, ln1_b, ln2_g, ln2_b)

    def to_core(nm, grad):
        recv = lax.empty((N_CHIP,) + grad.shape[1:], F32)
        return _xfer_start("reduce_core_" + nm, [grad, recv], _plan_reduce_core, N_CHIP)

    def to_chips(nm, started, after):
        grad, recv = _xfer_wait("reduce_core_wait_" + nm, started, after)
        part = _chip_sums("chip_sums_" + nm, ids, grad, recv)
        land = lax.empty(part.shape, BF16)
        return grad, recv, _xfer_start("reduce_chips_" + nm, [part, land], _plan_reduce_chips, N_CHIP - 1)

    da = _ff_down_bwd_act(dh3b, w_down_full, ra)
    gw_down = _grad_w("grad_w_down", h2, dh3b).reshape(N_DEV, -1, D)
    rc_down = to_core("w_ff_down", gw_down)
    gw_up = _grad_w_up_blk(x1, da, N_DEV, deps=[rc_down["token"]])
    gw_down, rv_down, rs_down = to_chips("w_ff_down", rc_down, gw_up)
    rc_up = to_core("w_ff_up", gw_up)
    dh1, dh1b, g_ln1_g, g_ln1_b = _ff_up_bwd_ln1(da, w_up_blk, dh3, xhat1, rstd1, ln1_g,
                                                 deps=[rs_down["token"], rc_up["token"]])
    gw_up, rv_up, rs_up = to_chips("w_ff_up", rc_up, dh1b)
    dyin = _mix_bwd(dh1b, w_out_full, deps=[rs_up["token"]])
    gw_out = _grad_w("grad_w_out", yin, dh1b).reshape(N_DEV, -1, D)
    rc_out = to_core("w_out", gw_out)
    dproj, g_conv_w, g_conv_g = _conv_bwd(proj, dyin, conv_w8, conv_norm_g, deps=[rc_out["token"]])
    dproj, g_gate_w, g_gate_b, g_gla_g = _gla_bwd(proj, wg128, gate_bias, gla_norm_g, o_all, states, dyin, dproj)
    gw_out, rv_out, rs_out = to_chips("w_out", rc_out, dproj)
    gw_in = _w_in_blocks(_grad_w("grad_w_in", xb, dproj, tn_pref=1280, tk_pref=2048, deps=[rs_out["token"]]))
    rc_in = to_core("w_in", gw_in)

    big = {}

    def finish(nm, grad, recv, started, w, m, v, after):
        _, land = _xfer_wait("reduce_chips_wait_" + nm, started, after)
        res = _reduce_adamw("adamw_" + nm, ids, grad, recv, land, w[0], m[0], v[0])
        big[nm] = [a[None] for a in res]
        return res[0]

    done = finish("w_ff_down", gw_down, rv_down, rs_down, w_ff_down, m_w_ff_down, v_w_ff_down, rc_in["token"])
    done = finish("w_ff_up", gw_up, rv_up, rs_up, w_ff_up, m_w_ff_up, v_w_ff_up, done)
    gw_in, rv_in, rs_in = to_chips("w_in", rc_in, done)
    grad_x = _proj_bwd_x(dproj, w_full, dh1, deps=[rs_in["token"]])

    pack = _pack(g_conv_w[:3], g_conv_g, g_gate_b, g_gla_g, g_ln1_g, g_ln1_b, g_ln2_g, g_ln2_b, loss,
                 g_gate_w[:GATE_RANK])
    (packs,) = _all_gather("gather_small_grads", [pack], deps=[grad_x])
    done = finish("w_out", gw_out, rv_out, rs_out, w_out, m_w_out, v_w_out, packs)
    finish("w_in", gw_in, rv_in, rs_in, w_in, (m_in,), (v_in,), done)

    def own_cols(row, n_rows, width):
        cut = lax.dynamic_slice(packs, (0, row, dev * width), (N_DEV, n_rows, width))
        return jnp.pad(cut, ((0, 0), (0, 0), (0, SP_COLS - width)))

    packs_own = jnp.concatenate([own_cols(R_CONV_W, 3, conv_cols), packs[:, R_CONV_W + 3:R_GATE_W],
                                 own_cols(R_GATE_W, GATE_RANK, gate_cols)], axis=1)

    def small_pack(cw, cg, gw, gb, gg, l1g, l1b, l2g, l2b):
        return _pack(cw[0], cg, gb, gg, l1g, l1b, l2g, l2b, None, gw[0])

    w_s = small_pack(conv_w, conv_norm_g, w_gate_up, gate_bias, gla_norm_g, ln1_g, ln1_b, ln2_g, ln2_b)
    m_s = small_pack(m_conv_w, m_conv_norm_g, m_w_gate_up, m_gate_bias, m_gla_norm_g, m_ln1_g, m_ln1_b, m_ln2_g, m_ln2_b)
    v_s = small_pack(v_conv_w, v_conv_norm_g, v_w_gate_up, v_gate_bias, v_gla_norm_g, v_ln1_g, v_ln1_b, v_ln2_g, v_ln2_b)
    g_s, d_s, mn_s, vn_s = _small_adamw(packs_own, w_s, m_s, v_s)
    small = [_unpack(p, D, conv_cols, gate_cols) for p in (g_s, d_s, mn_s, vn_s)]

    def leaf(kind, name):
        if name in BIG:
            return big[name][kind]
        a = small[kind][name]
        return a[None] if name in ("conv_w", "w_gate_up") else a

    out = [g_s[R_LOSS, 0], grad_x[None]]
    for kind in range(4):
        out += [leaf(kind, nm) for nm in ORDER]
    return tuple(out)
```

```python
import jax
import jax.numpy as jnp
from jax import lax
from jax.experimental import pallas as pl
from jax.experimental.pallas import tpu as pltpu

F32 = jnp.float32
BF16 = jnp.bfloat16

D_CONV = 1024
CONV_GROUPS = 8
GLA_HEADS = 4
HEAD_K = 128
HEAD_V = 256
D_GLA_K = 512
D_GLA_V = 1024
GATE_RANK = 16
GATE_TAU = 16.0
CHUNK = 64
LN_EPS = 1e-5
RMS_EPS = 1e-6
DN_ALPHA = 2.0 ** 0.25
D_IN_PROJ = 6160
ADAM_LR = 0.001
ADAM_B1 = 0.9
ADAM_B2 = 0.999
ADAM_EPS = 1e-08
ADAM_WD = 0.01
ADAM_STEP = 10

N_DEV = 8
N_CHIP = 4
LANE = 128
HALF_P = 3200
P_INT = 2 * HALF_P
CONV_COLS = 3 * D_CONV
GLA_COLS = D_IN_PROJ - CONV_COLS
SP_ROWS = 32
SP_COLS = 1024
VMEM_LIMIT = 56 * 1024 * 1024

NN = ((1,), (0,))
NT = ((1,), (1,))
TN = ((0,), (0,))
MESH = pl.DeviceIdType.MESH


def _dot(a, b, dims, precision=None):
    return lax.dot_general(a, b, (dims, ((), ())), preferred_element_type=F32, precision=precision)


def _tile(n, pref):
    if n <= pref:
        return n
    t = (pref // LANE) * LANE
    while t > 0 and n % t:
        t -= LANE
    assert t > 0, (n, pref)
    return t


def _params(n_axes):
    return pltpu.CompilerParams(dimension_semantics=("arbitrary",) * n_axes, vmem_limit_bytes=VMEM_LIMIT)


def _full(shape):
    nd = len(shape)
    return pl.BlockSpec(shape, lambda *_: (0,) * nd)


def _hbm_specs(n):
    return [pl.BlockSpec(memory_space=pl.ANY)] * n


def _mm(name, mode, a, b, *, M, N, K, tm, tn, tk, outs, epilogue, extras=(), a_fn=None, a_spec=None, b_spec=None,
        deps=()):
    ni, nj, nk = M // tm, N // tn, K // tk
    assert ni * tm == M and nj * tn == N and nk * tk == K, (name, M, N, K, tm, tn, tk)
    if a_spec is None:
        a_spec = (pl.BlockSpec((tk, tm), lambda i, j, k: (k, i)) if mode == "tn"
                  else pl.BlockSpec((tm, tk), lambda i, j, k: (i, k)))
    if b_spec is None:
        b_spec = (pl.BlockSpec((tn, tk), lambda i, j, k: (j, k)) if mode == "nt"
                  else pl.BlockSpec((tk, tn), lambda i, j, k: (k, j)))
    dims = {"nn": NN, "nt": NT, "tn": TN}[mode]
    n_ex, n_out, n_dep = len(extras), len(outs), len(deps)

    def body(*refs):
        a_ref, b_ref = refs[0], refs[1]
        ex = refs[2:2 + n_ex]
        o = refs[2 + n_ex + n_dep:2 + n_ex + n_dep + n_out]
        acc_ref = refs[2 + n_ex + n_dep + n_out]
        i, j, k = pl.program_id(0), pl.program_id(1), pl.program_id(2)
        if nk > 1:
            @pl.when(k == 0)
            def _():
                acc_ref[...] = jnp.zeros_like(acc_ref)

        av = a_ref[...]
        if a_fn is not None:
            av = a_fn(av)
        part = _dot(av, b_ref[...], dims)
        if nk == 1 and epilogue is None:
            o[0][...] = part.astype(o[0].dtype)
        elif nk == 1:
            acc_ref[...] = part
            epilogue(acc_ref, ex, o, i, j)
        else:
            acc_ref[...] += part

            @pl.when(k == nk - 1)
            def _():
                if epilogue is None:
                    o[0][...] = acc_ref[...].astype(o[0].dtype)
                else:
                    epilogue(acc_ref, ex, o, i, j)

    return pl.pallas_call(
        body,
        name=name,
        grid=(ni, nj, nk),
        in_specs=[a_spec, b_spec] + [s for _, s in extras] + _hbm_specs(n_dep),
        out_specs=[s for _, s in outs],
        out_shape=[s for s, _ in outs],
        scratch_shapes=[pltpu.VMEM((8, LANE) if nk == 1 and epilogue is None else (tm, tn), F32)],
        compiler_params=_params(3),
    )(a, b, *[x for x, _ in extras], *deps)


def _mm_rows(name, mode, a, b, *, M, N, K, tm, tk, row_ins, vec_ins, row_outs, stat_outs, chunk_fn,
             b_spec=None, deps=()):
    ni, nk = M // tm, K // tk
    rc = tm // nk
    assert ni * tm == M and nk * tk == K and rc * nk == tm and rc % 16 == 0, (name, M, K, tm, tk)
    dims = {"nn": NN, "nt": NT}[mode]
    last = ni - 1
    n_split = 2 if N % (2 * 256) == 0 else 1

    def kk(i, k):
        return jnp.where(i < ni, k, nk - 1)

    a_spec = pl.BlockSpec((tm, tk), lambda i, k: (jnp.minimum(i, last), kk(i, k)))
    if b_spec is None:
        b_spec = (pl.BlockSpec((N, tk), lambda i, k: (0, kk(i, k))) if mode == "nt"
                  else pl.BlockSpec((tk, N), lambda i, k: (kk(i, k), 0)))
    prev_rows = lambda i, k: (jnp.maximum((i - 1) * nk + k, 0), 0)
    n_ri, n_vi, n_ro, n_so, n_dep = len(row_ins), len(vec_ins), len(row_outs), len(stat_outs), len(deps)

    def body(*refs):
        a_ref, b_ref = refs[0], refs[1]
        pos = 2
        ri = refs[pos:pos + n_ri]; pos += n_ri
        vi = refs[pos:pos + n_vi]; pos += n_vi + n_dep
        ro = refs[pos:pos + n_ro]; pos += n_ro
        so = refs[pos:pos + n_so]; pos += n_so
        accs = refs[pos:pos + 2]
        i, k = pl.program_id(0), pl.program_id(1)

        @pl.when((i == 0) & (k == 0))
        def _():
            accs[0][...] = jnp.zeros_like(accs[0])
            accs[1][...] = jnp.zeros_like(accs[1])
            for st in so:
                st[...] = jnp.zeros_like(st)

        def finish_rows(prev_ref):
            rows = pl.ds(pl.multiple_of(k * rc, rc), rc)
            done = prev_ref[rows, :]
            prev_ref[rows, :] = jnp.zeros((rc, N), F32)
            chunk_fn(done, i > 0, ri, vi, ro, so)

        def accumulate(acc_ref):
            av = a_ref[...]
            for c0 in range(0, N, N // n_split):
                cols = slice(c0, c0 + N // n_split)
                bv = b_ref[cols, :] if mode == "nt" else b_ref[:, cols]
                acc_ref[:, cols] += _dot(av, bv, dims)

        for parity in (0, 1):
            @pl.when((i < ni) & (lax.rem(i, 2) == parity))
            def _(parity=parity):
                finish_rows(accs[1 - parity])
                accumulate(accs[parity])

        @pl.when(i == ni)
        def _():
            finish_rows(accs[last % 2])

    row_spec = lambda arr: pl.BlockSpec((rc, arr.shape[1]), prev_rows)
    return pl.pallas_call(
        body,
        name=name,
        grid=(ni + 1, nk),
        in_specs=[a_spec, b_spec] + [row_spec(x) for x in row_ins] + [_full(x.shape) for x in vec_ins]
        + _hbm_specs(n_dep),
        out_specs=[row_spec(s) for s in row_outs] + [_full(s.shape) for s in stat_outs],
        out_shape=list(row_outs) + list(stat_outs),
        scratch_shapes=[pltpu.VMEM((tm, N), F32), pltpu.VMEM((tm, N), F32)],
        compiler_params=_params(2),
    )(a, b, *row_ins, *vec_ins, *deps)


SUB_ROWS = 16


def _by_sub_rows(n_rows, fn):
    sums = None
    for r0 in range(0, n_rows, SUB_ROWS):
        part = fn(slice(r0, r0 + SUB_ROWS))
        if part:
            sums = part if sums is None else tuple(x + y for x, y in zip(sums, part))
    return sums


def _to_bf16(v):
    return v.astype(BF16)


def _ln_bwd(dy, xhat, rstd, g):
    dxh = dy * g
    m1 = jnp.mean(dxh, axis=-1, keepdims=True)
    m2 = jnp.mean(dxh * xhat, axis=-1, keepdims=True)
    return rstd * (dxh - m1 - xhat * m2)


def _ln_fwd(h):
    mu = jnp.mean(h, axis=-1, keepdims=True)
    xc = h - mu
    var = jnp.mean(xc * xc, axis=-1, keepdims=True)
    rstd = lax.rsqrt(var + LN_EPS)
    return xc * rstd, rstd


def _proj_fwd(x, w_full, deps=()):
    T, D = x.shape
    P = w_full.shape[1]
    tm, tn = _tile(T, 1024), _tile(P, 1280)
    return _mm("proj_fwd", "nn", x, w_full, M=T, N=P, K=D, tm=tm, tn=tn, tk=D,
               outs=[(jax.ShapeDtypeStruct((T, P), F32), pl.BlockSpec((tm, tn), lambda i, j, k: (i, j)))],
               epilogue=None, deps=deps)[0]


def _cast_bf16(x, deps=()):
    T, D = x.shape
    tm = _tile(T, 512)

    def body(x_ref, *rest):
        rest[len(deps)][...] = x_ref[...].astype(BF16)

    return pl.pallas_call(
        body,
        name="cast_x",
        grid=(T // tm,),
        in_specs=[pl.BlockSpec((tm, D), lambda i: (i, 0))] + _hbm_specs(len(deps)),
        out_specs=pl.BlockSpec((tm, D), lambda i: (i, 0)),
        out_shape=jax.ShapeDtypeStruct((T, D), BF16),
        compiler_params=_params(1),
    )(x, *deps)


def _conv_shift(h, hp):
    row = lax.broadcasted_iota(jnp.int32, h.shape, 0)
    hm1 = hp[7:8, :]
    hm2 = hp[6:7, :]
    h1 = jnp.where(row == 0, hm1, pltpu.roll(h, 1, 0))
    h2 = jnp.where(row == 0, hm2, jnp.where(row == 1, hm1, pltpu.roll(h, 2, 0)))
    return h1, h2


def _conv_fwd(proj, conv_w8, conv_g):
    T = proj.shape[0]
    tt = _tile(T, 256)
    nt = T // tt
    t8 = tt // 8

    def body(b_ref, c_ref, u_ref, cp_ref, up_ref, w_ref, g_ref, yin_ref):
        i = pl.program_id(0)
        h = c_ref[...] * u_ref[...]
        hp = jnp.where(i > 0, cp_ref[...] * up_ref[...], 0.0)
        h1, h2 = _conv_shift(h, hp)
        w = w_ref[...]
        y = w[0:1, :] * h2 + w[1:2, :] * h1 + w[2:3, :] * h
        p = b_ref[...] * y
        parts = []
        for gi in range(CONV_GROUPS):
            pg = p[:, gi * LANE:(gi + 1) * LANE]
            r = lax.rsqrt(jnp.mean(pg * pg, axis=-1, keepdims=True) + RMS_EPS)
            parts.append(pg * r)
        yn = jnp.concatenate(parts, axis=1) * g_ref[...]
        yin_ref[...] = yn.astype(BF16)

    def col(cidx):
        return pl.BlockSpec((tt, D_CONV), lambda i: (i, cidx))

    def prev(cidx):
        return pl.BlockSpec((8, D_CONV), lambda i: (jnp.maximum(i * t8 - 1, 0), cidx))

    return pl.pallas_call(
        body,
        name="conv_fwd",
        grid=(nt,),
        in_specs=[col(0), col(1), col(2), prev(1), prev(2), _full((8, D_CONV)), _full((1, D_CONV))],
        out_specs=pl.BlockSpec((tt, D_CONV), lambda i: (i, 0)),
        out_shape=jax.ShapeDtypeStruct((T, 2 * D_CONV), BF16),
        compiler_params=_params(1),
    )(proj, proj, proj, proj, proj, conv_w8, conv_g)


def _log_sigmoid(z):
    return jnp.minimum(z, 0.0) - jnp.log(1.0 + jnp.exp(-jnp.abs(z)))


STEP_CHUNKS = 4
STEP_ROWS = STEP_CHUNKS * CHUNK


def _gla_step_terms(blk, wg_ref, gb_ref):
    zl = blk[:, 3072:3200]
    z = _dot(zl.astype(BF16), wg_ref[...], NN) + gb_ref[...]
    log_a = _log_sigmoid(z) * (1.0 / GATE_TAU)
    ri = lax.broadcasted_iota(jnp.int32, (STEP_ROWS, STEP_ROWS), 0)
    ci = lax.broadcasted_iota(jnp.int32, (STEP_ROWS, STEP_ROWS), 1)
    same = (ri // CHUNK) == (ci // CHUNK)
    lower = (same & (ri >= ci)).astype(F32)
    bcum = _dot(lower, log_a, NN, precision=lax.Precision.HIGHEST)
    return zl, z, bcum, same


def _causal():
    return (lax.broadcasted_iota(jnp.int32, (CHUNK, CHUNK), 0) >= lax.broadcasted_iota(jnp.int32, (CHUNK, CHUNK), 1))


def _gla_head_terms(q, k, bcum, h):
    sl = slice(h * HEAD_K, (h + 1) * HEAD_K)
    bh = bcum[:, sl]
    bl = bh[CHUNK - 1:CHUNK, :]
    eb = jnp.exp(bh)
    enb = jnp.exp(-bh)
    eend = jnp.exp(bl - bh)
    dec = jnp.exp(bl)
    qd = q[:, sl] * (HEAD_K ** -0.5) * eb
    ki = k[:, sl] * enb
    ke = k[:, sl] * eend
    return eb, enb, eend, dec, qd, ki, ke


def _sigmoid(x):
    return 1.0 / (1.0 + jnp.exp(-x))


def _gla_fwd(proj, wg128, gbias, gng, yin, deps=()):
    T = proj.shape[0]
    nch = T // CHUNK
    nst = T // STEP_ROWS

    def body(p_ref, wg_ref, gb_ref, gn_ref, yin_in_ref, *rest):
        o_ref, st_ref, yin_ref, s_ref = rest[len(deps):]
        n = pl.program_id(0)

        @pl.when(n == 0)
        def _():
            s_ref[...] = jnp.zeros_like(s_ref)

        blk = p_ref[...]
        _, _, bcum_all, _ = _gla_step_terms(blk, wg_ref, gb_ref)
        causal = _causal()
        gn = gn_ref[...]
        states = [s_ref[h] for h in range(GLA_HEADS)]
        for c in range(STEP_CHUNKS):
            rows = slice(c * CHUNK, (c + 1) * CHUNK)
            q, k = blk[rows, 0:512], blk[rows, 512:1024]
            v, r = blk[rows, 1024:2048], blk[rows, 2048:3072]
            bcum = bcum_all[rows, :]
            for h in range(GLA_HEADS):
                _, _, _, dec, qd, ki, ke = _gla_head_terms(q, k, bcum, h)
                vs = slice(h * HEAD_V, (h + 1) * HEAD_V)
                vb = v[:, vs].astype(BF16)
                qdb = qd.astype(BF16)
                a = jnp.where(causal, _dot(qdb, ki.astype(BF16), NT), 0.0)
                st = states[h]
                o = _dot(a.astype(BF16), vb, NN) + _dot(qdb, st.astype(BF16), NT)
                st_ref[c, h] = st
                states[h] = dec * st + _dot(vb, ke.astype(BF16), TN)
                o_ref[rows, vs] = o
                rinv = lax.rsqrt(jnp.mean(o * o, axis=-1, keepdims=True) + RMS_EPS)
                rh = r[:, vs]
                yin_ref[rows, vs] = (o * rinv * gn[:, vs] * (rh * _sigmoid(rh))).astype(BF16)
        for h in range(GLA_HEADS):
            s_ref[h] = states[h]

    return pl.pallas_call(
        body,
        name="gla_fwd",
        grid=(nst,),
        in_specs=[pl.BlockSpec((STEP_ROWS, HALF_P), lambda n: (n, 1)), _full((LANE, D_GLA_K)), _full((1, D_GLA_K)),
                  _full((1, D_GLA_V)), pl.BlockSpec(memory_space=pl.ANY)] + _hbm_specs(len(deps)),
        out_specs=[pl.BlockSpec((STEP_ROWS, D_GLA_V), lambda n: (n, 0)),
                   pl.BlockSpec((STEP_CHUNKS, GLA_HEADS, HEAD_V, HEAD_K), lambda n: (n, 0, 0, 0)),
                   pl.BlockSpec((STEP_ROWS, D_GLA_V), lambda n: (n, 1))],
        out_shape=[jax.ShapeDtypeStruct((T, D_GLA_V), F32),
                   jax.ShapeDtypeStruct((nch, GLA_HEADS, HEAD_V, HEAD_K), F32),
                   jax.ShapeDtypeStruct(yin.shape, BF16)],
        scratch_shapes=[pltpu.VMEM((GLA_HEADS, HEAD_V, HEAD_K), F32)],
        input_output_aliases={4: 2},
        compiler_params=_params(1),
    )(proj, wg128, gbias, gng, yin, *deps)


def _mix_ln1(yin, w_out, x, ln_g, ln_b, deps=()):
    T, D = x.shape
    KY = yin.shape[1]
    tm = _tile(T, 1024)

    def chunk(acc, valid, ri, vi, ro, so):
        g, b = vi[0][...], vi[1][...]

        def sub(rows):
            xhat, rstd = _ln_fwd(DN_ALPHA * ri[0][rows, :] + acc[rows, :])
            ro[0][rows, :] = xhat
            ro[1][rows, :] = (xhat * g + b).astype(BF16)
            ro[2][rows, :] = rstd

        _by_sub_rows(acc.shape[0], sub)

    return _mm_rows("mix_ln1", "nn", yin, w_out, M=T, N=D, K=KY, tm=tm, tk=_tile(KY, 512),
                    row_ins=[x], vec_ins=[ln_g, ln_b],
                    row_outs=[jax.ShapeDtypeStruct((T, D), F32), jax.ShapeDtypeStruct((T, D), BF16),
                              jax.ShapeDtypeStruct((T, 1), F32)],
                    stat_outs=[], chunk_fn=chunk, deps=deps)


def _ff_up(x1, w_up_blk, first, count, prev=None, deps=()):
    T, D = x1.shape
    nb, _, fb = w_up_blk.shape
    tm = _tile(T, 1024)
    ni = T // tm
    n_dep = len(deps) + (2 if prev is not None else 0)

    def body(a_ref, b_ref, *rest):
        ra_ref, h2_ref = rest[n_dep:n_dep + 2]
        ra = jnp.maximum(_dot(a_ref[...], b_ref[...], NN), 0.0)
        ra_ref[...] = ra.astype(BF16)
        h2_ref[...] = (ra * ra).astype(BF16)

    blk = pl.BlockSpec((tm, fb), lambda i, j: (i, first + j))
    shp = jax.ShapeDtypeStruct((T, nb * fb), BF16)
    keep = list(prev) if prev is not None else []
    return pl.pallas_call(
        body,
        name="ff_up_%d" % first,
        grid=(ni, count),
        in_specs=[pl.BlockSpec((tm, D), lambda i, j: (i, 0)),
                  pl.BlockSpec((None, D, fb), lambda i, j: (first + j, 0, 0))] + _hbm_specs(n_dep),
        out_specs=[blk, blk],
        out_shape=[shp, shp],
        input_output_aliases=({2: 0, 3: 1} if prev is not None else {}),
        compiler_params=_params(2),
    )(x1, w_up_blk, *keep, *deps)


def _ff_down_loss(h2, w_down, xhat1, target, g1, b1, g2, b2):
    T, F = h2.shape
    D = w_down.shape[1]
    tm = _tile(T, 1024)
    inv_d = 1.0 / D

    def chunk(acc, valid, ri, vi, ro, so):
        g1v, b1v, g2v, b2v = (v[...] for v in vi)

        def sub(rows):
            x1 = ri[0][rows, :] * g1v + b1v
            xhat, rstd = _ln_fwd(DN_ALPHA * x1 + acc[rows, :])
            e = xhat * g2v + b2v - ri[1][rows, :]
            dy = e * inv_d
            dh = _ln_bwd(dy, xhat, rstd, g2v)
            ro[0][rows, :] = dh
            ro[1][rows, :] = dh.astype(BF16)
            return (jnp.sum(dy * xhat, axis=0, keepdims=True), jnp.sum(dy, axis=0, keepdims=True),
                    jnp.sum(e * e, axis=0, keepdims=True))

        sg, sb, sl = _by_sub_rows(acc.shape[0], sub)
        so[0][...] += jnp.where(valid, sg, 0.0)
        so[1][...] += jnp.where(valid, sb, 0.0)
        so[2][...] += jnp.where(valid, sl * (0.5 * inv_d), 0.0)

    vshape = jax.ShapeDtypeStruct((1, D), F32)
    return _mm_rows("ff_down_loss", "nn", h2, w_down, M=T, N=D, K=F, tm=tm, tk=_tile(F, 1024),
                    row_ins=[xhat1, target], vec_ins=[g1, b1, g2, b2],
                    row_outs=[jax.ShapeDtypeStruct((T, D), F32), jax.ShapeDtypeStruct((T, D), BF16)],
                    stat_outs=[vshape, vshape, vshape], chunk_fn=chunk)


def _ff_down_bwd_act(dh3b, w_down, ra):
    T, D = dh3b.shape
    F = w_down.shape[0]
    tm, tn = _tile(T, 1024), _tile(F, 1024)

    def ep(acc_ref, ex, o, i, j):
        o[0][...] = (acc_ref[...] * (2.0 * ex[0][...].astype(F32))).astype(BF16)

    blk = pl.BlockSpec((tm, tn), lambda i, j, k: (i, j))
    return _mm("ff_down_bwd_act", "nt", dh3b, w_down, M=T, N=F, K=D, tm=tm, tn=tn, tk=D,
               outs=[(jax.ShapeDtypeStruct((T, F), BF16), blk)], extras=[(ra, blk)], epilogue=ep)[0]


def _grad_w(name, a, b, *, a_fn=None, tm_pref=1024, tn_pref=1024, tk_pref=4096, deps=()):
    T, M = a.shape
    N = b.shape[1]
    tm, tn, tk = _tile(M, tm_pref), _tile(N, tn_pref), _tile(T, tk_pref)
    return _mm(name, "tn", a, b, M=M, N=N, K=T, tm=tm, tn=tn, tk=tk, a_fn=a_fn, deps=deps,
               outs=[(jax.ShapeDtypeStruct((M, N), F32), pl.BlockSpec((tm, tn), lambda i, j, k: (i, j)))],
               epilogue=None)[0]


def _grad_w_up_blk(x1, da, nb, deps=()):
    T, D = x1.shape
    F = da.shape[1]
    fb = F // nb
    tm, tk = _tile(D, 1024), _tile(T, 4096)
    return _mm("grad_w_up", "tn", x1, da, M=D, N=F, K=T, tm=tm, tn=fb, tk=tk, deps=deps,
               outs=[(jax.ShapeDtypeStruct((nb, D, fb), F32),
                      pl.BlockSpec((None, tm, fb), lambda i, j, k: (j, i, 0)))],
               epilogue=None)[0]


def _ff_up_bwd_ln1(da, w_up_blk, dh3, xhat1, rstd1, g1, deps=()):
    T, F = da.shape
    nb, D, fb = w_up_blk.shape
    tm = _tile(T, 1024)

    def chunk(acc, valid, ri, vi, ro, so):
        g = vi[0][...]

        def sub(rows):
            dx1 = DN_ALPHA * ri[0][rows, :] + acc[rows, :]
            xhat = ri[1][rows, :]
            dh = _ln_bwd(dx1, xhat, ri[2][rows, :], g)
            ro[0][rows, :] = dh
            ro[1][rows, :] = dh.astype(BF16)
            return jnp.sum(dx1 * xhat, axis=0, keepdims=True), jnp.sum(dx1, axis=0, keepdims=True)

        sg, sb = _by_sub_rows(acc.shape[0], sub)
        so[0][...] += jnp.where(valid, sg, 0.0)
        so[1][...] += jnp.where(valid, sb, 0.0)

    nk = F // fb
    vshape = jax.ShapeDtypeStruct((1, D), F32)
    return _mm_rows("ff_up_bwd_ln1", "nt", da, w_up_blk, M=T, N=D, K=F, tm=tm, tk=fb,
                    b_spec=pl.BlockSpec((None, D, fb), lambda i, k: (jnp.where(i < T // tm, k, nk - 1), 0, 0)),
                    row_ins=[dh3, xhat1, rstd1], vec_ins=[g1],
                    row_outs=[jax.ShapeDtypeStruct((T, D), F32), jax.ShapeDtypeStruct((T, D), BF16)],
                    stat_outs=[vshape, vshape], chunk_fn=chunk, deps=deps)


def _mix_bwd(dh1b, w_out, deps=()):
    T, D = dh1b.shape
    KY = w_out.shape[0]
    tm, tn = _tile(T, 1024), _tile(KY, 1024)
    return _mm("mix_bwd", "nt", dh1b, w_out, M=T, N=KY, K=D, tm=tm, tn=tn, tk=D, deps=deps,
               outs=[(jax.ShapeDtypeStruct((T, KY), F32), pl.BlockSpec((tm, tn), lambda i, j, k: (i, j)))],
               epilogue=None)[0]


def _conv_bwd(proj, dyin, conv_w8, conv_g, deps=()):
    T = proj.shape[0]
    tt = _tile(T, 256)
    nt = T // tt
    t8 = tt // 8
    nx = tt + 8

    def body(b_ref, c_ref, u_ref, d_ref, bn_ref, cn_ref, un_ref, dn_ref, cp_ref, up_ref, w_ref, g_ref, *rest):
        dp_ref, dw_ref, dg_ref = rest[len(deps):]
        i = pl.program_id(0)

        @pl.when(i == 0)
        def _():
            dw_ref[...] = jnp.zeros_like(dw_ref)
            dg_ref[...] = jnp.zeros_like(dg_ref)

        more = i < nt - 1

        def ext(cur_ref, nxt_ref):
            return jnp.concatenate([cur_ref[...], jnp.where(more, nxt_ref[...], 0.0)], axis=0)

        bx, cx, ux, dx = ext(b_ref, bn_ref), ext(c_ref, cn_ref), ext(u_ref, un_ref), ext(d_ref, dn_ref)
        hx = cx * ux
        hp = jnp.where(i > 0, cp_ref[...] * up_ref[...], 0.0)
        h1, h2 = _conv_shift(hx, hp)
        w = w_ref[...]
        g = g_ref[...]
        yx = w[0:1, :] * h2 + w[1:2, :] * h1 + w[2:3, :] * hx
        px = bx * yx
        dps, dgs = [], []
        for gi in range(CONV_GROUPS):
            sl = slice(gi * LANE, (gi + 1) * LANE)
            pg, dg_ = px[:, sl], dx[:, sl]
            r = lax.rsqrt(jnp.mean(pg * pg, axis=-1, keepdims=True) + RMS_EPS)
            gd = g[:, sl] * dg_
            dps.append(r * gd - pg * (r * r * r) * jnp.mean(pg * gd, axis=-1, keepdims=True))
            dgs.append(jnp.sum((dg_ * pg * r)[:tt, :], axis=0, keepdims=True))
        dpx = jnp.concatenate(dps, axis=1)
        dg_ref[...] += jnp.concatenate(dgs, axis=1)
        dyx = dpx * bx
        dyc = dyx[:tt, :]
        dh = (w[2:3, :] * dyx + w[1:2, :] * pltpu.roll(dyx, nx - 1, 0) + w[0:1, :] * pltpu.roll(dyx, nx - 2, 0))[:tt, :]
        dw_ref[0:1, :] += jnp.sum(dyc * h2[:tt, :], axis=0, keepdims=True)
        dw_ref[1:2, :] += jnp.sum(dyc * h1[:tt, :], axis=0, keepdims=True)
        dw_ref[2:3, :] += jnp.sum(dyc * hx[:tt, :], axis=0, keepdims=True)
        dp_ref[:, 0:D_CONV] = (dpx * yx)[:tt, :].astype(BF16)
        dp_ref[:, D_CONV:2 * D_CONV] = (dh * u_ref[...]).astype(BF16)
        dp_ref[:, 2 * D_CONV:3 * D_CONV] = (dh * c_ref[...]).astype(BF16)
        dp_ref[:, 3 * D_CONV:HALF_P] = jnp.zeros((tt, HALF_P - 3 * D_CONV), BF16)

    def col(cidx):
        return pl.BlockSpec((tt, D_CONV), lambda i: (i, cidx))

    def nxt(cidx):
        return pl.BlockSpec((8, D_CONV), lambda i: (jnp.minimum((i + 1) * t8, T // 8 - 1), cidx))

    def prev(cidx):
        return pl.BlockSpec((8, D_CONV), lambda i: (jnp.maximum(i * t8 - 1, 0), cidx))

    return pl.pallas_call(
        body,
        name="conv_bwd",
        grid=(nt,),
        in_specs=[col(0), col(1), col(2), col(0), nxt(0), nxt(1), nxt(2), nxt(0), prev(1), prev(2),
                  _full((8, D_CONV)), _full((1, D_CONV))] + _hbm_specs(len(deps)),
        out_specs=[pl.BlockSpec((tt, HALF_P), lambda i: (i, 0)), _full((8, D_CONV)), _full((1, D_CONV))],
        out_shape=[jax.ShapeDtypeStruct((T, P_INT), BF16), jax.ShapeDtypeStruct((8, D_CONV), F32),
                   jax.ShapeDtypeStruct((1, D_CONV), F32)],
        compiler_params=_params(1),
    )(proj, proj, proj, dyin, proj, proj, proj, dyin, proj, proj, conv_w8, conv_g, *deps)


def _gla_bwd(proj, wg128, gbias, gng, o_all, states, dyin, dproj):
    T = proj.shape[0]
    nst = T // STEP_ROWS

    def body(p_ref, wg_ref, gb_ref, gn_ref, o_ref, st_ref, d_ref, dp_in_ref,
             dp_ref, dwg_ref, dgb_ref, dgn_ref, ds_ref):
        n = pl.program_id(0)

        @pl.when(n == 0)
        def _():
            ds_ref[...] = jnp.zeros_like(ds_ref)
            dwg_ref[...] = jnp.zeros_like(dwg_ref)
            dgb_ref[...] = jnp.zeros_like(dgb_ref)
            dgn_ref[...] = jnp.zeros_like(dgn_ref)

        blk = p_ref[...]
        zl, z, bcum_all, same = _gla_step_terms(blk, wg_ref, gb_ref)
        causal = _causal()
        gn = gn_ref[...]
        ri = lax.broadcasted_iota(jnp.int32, (STEP_ROWS, STEP_ROWS), 0)
        ci = lax.broadcasted_iota(jnp.int32, (STEP_ROWS, STEP_ROWS), 1)
        upper = (same & (ri <= ci)).astype(F32)
        dstates = [ds_ref[h] for h in range(GLA_HEADS)]
        db_rows, dbl_rows, dgn_sum = [None] * STEP_CHUNKS, [None] * STEP_CHUNKS, [None] * GLA_HEADS
        for c in reversed(range(STEP_CHUNKS)):
            rows = slice(c * CHUNK, (c + 1) * CHUNK)
            q, k = blk[rows, 0:512], blk[rows, 512:1024]
            v, r = blk[rows, 1024:2048], blk[rows, 2048:3072]
            bcum = bcum_all[rows, :]
            db_parts, dbl_parts = [], []
            for h in range(GLA_HEADS):
                eb, enb, eend, dec, qd, ki, ke = _gla_head_terms(q, k, bcum, h)
                vs = slice(h * HEAD_V, (h + 1) * HEAD_V)
                ks = slice(h * HEAD_K, (h + 1) * HEAD_K)
                o = o_ref[rows, vs]
                rh = r[:, vs]
                dyg = d_ref[rows, vs]
                rinv = lax.rsqrt(jnp.mean(o * o, axis=-1, keepdims=True) + RMS_EPS)
                sg = _sigmoid(rh)
                on = o * rinv
                dr = dyg * (on * gn[:, vs]) * (sg * (1.0 + rh * (1.0 - sg)))
                don = dyg * (rh * sg)
                part = jnp.sum(don * on, axis=0, keepdims=True)
                dgn_sum[h] = part if dgn_sum[h] is None else dgn_sum[h] + part
                t = don * gn[:, vs]
                do = rinv * t - o * (rinv * rinv * rinv) * jnp.mean(o * t, axis=-1, keepdims=True)
                dob = do.astype(BF16)
                vb = v[:, vs].astype(BF16)
                qdb, kib, keb = qd.astype(BF16), ki.astype(BF16), ke.astype(BF16)
                a = jnp.where(causal, _dot(qdb, kib, NT), 0.0)
                st = st_ref[c, h]
                dst = dstates[h]
                dstb = dst.astype(BF16)
                da = jnp.where(causal, _dot(dob, vb, NT), 0.0)
                dab = da.astype(BF16)
                dv = _dot(a.astype(BF16), dob, TN) + _dot(keb, dstb, NT)
                dqd = _dot(dab, kib, NN) + _dot(dob, st.astype(BF16), NN)
                dki = _dot(dab, qdb, TN)
                dke = _dot(vb, dstb, NN)
                ddec = jnp.sum(st * dst, axis=0, keepdims=True)
                dstates[h] = dec * dst + _dot(dob, qdb, TN)
                dq = dqd * eb * (HEAD_K ** -0.5)
                dk = dki * enb + dke * eend
                db_parts.append(dqd * qd - dki * ki - dke * ke)
                dbl_parts.append(jnp.sum(dke * ke, axis=0, keepdims=True) + dec * ddec)
                dp_ref[rows, ks] = dq.astype(BF16)
                dp_ref[rows, D_GLA_K + h * HEAD_K:D_GLA_K + (h + 1) * HEAD_K] = dk.astype(BF16)
                dp_ref[rows, 1024 + h * HEAD_V:1024 + (h + 1) * HEAD_V] = dv.astype(BF16)
                dp_ref[rows, 2048 + h * HEAD_V:2048 + (h + 1) * HEAD_V] = dr.astype(BF16)
            db_rows[c] = jnp.concatenate(db_parts, axis=1)
            dbl_rows[c] = jnp.broadcast_to(jnp.concatenate(dbl_parts, axis=1), (CHUNK, D_GLA_K))
        for h in range(GLA_HEADS):
            ds_ref[h] = dstates[h]
            dgn_ref[:, h * HEAD_V:(h + 1) * HEAD_V] += dgn_sum[h]
        db = jnp.concatenate(db_rows, axis=0)
        dlog = _dot(upper, db, NN, precision=lax.Precision.HIGHEST) + jnp.concatenate(dbl_rows, axis=0)
        dz = dlog * (1.0 / GATE_TAU) * (1.0 / (1.0 + jnp.exp(z)))
        dzb = dz.astype(BF16)
        dp_ref[:, 3072:3200] = _dot(dzb, wg_ref[...], NT).astype(BF16)
        dwg_ref[...] += _dot(zl.astype(BF16), dzb, TN)
        dgb_ref[...] += jnp.sum(dz, axis=0, keepdims=True)

    rev = lambda n: nst - 1 - n
    return pl.pallas_call(
        body,
        name="gla_bwd",
        grid=(nst,),
        in_specs=[pl.BlockSpec((STEP_ROWS, HALF_P), lambda n: (rev(n), 1)), _full((LANE, D_GLA_K)),
                  _full((1, D_GLA_K)), _full((1, D_GLA_V)),
                  pl.BlockSpec((STEP_ROWS, D_GLA_V), lambda n: (rev(n), 0)),
                  pl.BlockSpec((STEP_CHUNKS, GLA_HEADS, HEAD_V, HEAD_K), lambda n: (rev(n), 0, 0, 0)),
                  pl.BlockSpec((STEP_ROWS, D_GLA_V), lambda n: (rev(n), 1)), pl.BlockSpec(memory_space=pl.ANY)],
        out_specs=[pl.BlockSpec((STEP_ROWS, HALF_P), lambda n: (rev(n), 1)), _full((LANE, D_GLA_K)),
                   _full((1, D_GLA_K)), _full((1, D_GLA_V))],
        out_shape=[jax.ShapeDtypeStruct(dproj.shape, BF16), jax.ShapeDtypeStruct((LANE, D_GLA_K), F32),
                   jax.ShapeDtypeStruct((1, D_GLA_K), F32), jax.ShapeDtypeStruct((1, D_GLA_V), F32)],
        scratch_shapes=[pltpu.VMEM((GLA_HEADS, HEAD_V, HEAD_K), F32)],
        input_output_aliases={7: 0},
        compiler_params=_params(1),
    )(proj, wg128, gbias, gng, o_all, states, dyin, dproj)


def _proj_bwd_x(dproj, w_full, dh1, deps=()):
    T, P = dproj.shape
    D = w_full.shape[0]
    tm, tk = _tile(T, 512), _tile(P, 1280)

    def ep(acc_ref, ex, o, i, j):
        o[0][...] = DN_ALPHA * ex[0][...] + acc_ref[...]

    row = pl.BlockSpec((tm, D), lambda i, j, k: (i, 0))
    return _mm("proj_bwd_x", "nt", dproj, w_full, M=T, N=D, K=P, tm=tm, tn=D, tk=tk,
               outs=[(jax.ShapeDtypeStruct((T, D), F32), row)], extras=[(dh1, row)], epilogue=ep, deps=deps)[0]


def _place():
    x, y, c = lax.axis_index("x"), lax.axis_index("y"), lax.axis_index("c")
    chips = [(1 - x, y), (x, 1 - y), (1 - x, 1 - y)]
    return x, y, c, chips


def _rcopy(src, dst, ssem, rsem, dev):
    return pltpu.make_async_remote_copy(src_ref=src, dst_ref=dst, send_sem=ssem, recv_sem=rsem,
                                        device_id=dev, device_id_type=MESH)


def _all_gather(name, shards, deps=()):
    n = len(shards)

    def body(*refs):
        ins, outs = refs[:n], refs[n + len(deps):2 * n + len(deps)]
        ssem, rsem, lsem = refs[2 * n + len(deps):]
        x, y, c, chips = _place()
        me, sib = (x, y, c), (x, y, 1 - c)

        def slot(w, px, py, pc):
            return outs[w].at[4 * px + 2 * py + pc]

        started = []
        for w in range(n):
            lc = pltpu.make_async_copy(ins[w], slot(w, *me), lsem.at[w])
            lc.start()
            started.append(lc)
        sends = []
        for w in range(n):
            cp = _rcopy(ins[w], slot(w, *me), ssem.at[7 * w], rsem.at[7 * w], sib)
            cp.start()
            sends.append(cp)
            for jx, chip in enumerate(chips):
                cp = _rcopy(ins[w], slot(w, *me), ssem.at[7 * w + 1 + jx], rsem.at[7 * w + 1 + jx], (*chip, c))
                cp.start()
                sends.append(cp)
        for w in range(n):
            for jx, chip in enumerate(chips):
                blk = slot(w, *chip, c)
                _rcopy(blk, blk, ssem.at[7 * w + 1 + jx], rsem.at[7 * w + 1 + jx], me).wait_recv()
                cp = _rcopy(blk, blk, ssem.at[7 * w + 4 + jx], rsem.at[7 * w + 4 + jx], sib)
                cp.start()
                sends.append(cp)
        for w in range(n):
            blk = slot(w, x, y, 1 - c)
            _rcopy(blk, blk, ssem.at[7 * w], rsem.at[7 * w], me).wait_recv()
            for jx, chip in enumerate(chips):
                blk = slot(w, *chip, 1 - c)
                _rcopy(blk, blk, ssem.at[7 * w + 4 + jx], rsem.at[7 * w + 4 + jx], me).wait_recv()
        for cp in sends:
            cp.wait_send()
        for lc in started:
            lc.wait()

    return pl.pallas_call(
        body,
        name=name,
        in_specs=_hbm_specs(n + len(deps)),
        out_specs=_hbm_specs(n),
        out_shape=[jax.ShapeDtypeStruct((N_DEV,) + s.shape, s.dtype) for s in shards],
        scratch_shapes=[pltpu.SemaphoreType.DMA((7 * n,)), pltpu.SemaphoreType.DMA((7 * n,)),
                        pltpu.SemaphoreType.DMA((n,))],
    )(*shards, *deps)


HBM_SPEC = pl.BlockSpec(memory_space=pltpu.HBM)
SEM_SPEC = pl.BlockSpec(memory_space=pltpu.SEMAPHORE)
SIDE_EFFECT = pltpu.SideEffectType.DATAFLOW_SIDE_EFFECTING


def _cast_place(name, ids, w, deps=(), dtype=None):
    dtype = BF16 if dtype is None else dtype
    R, C = w.shape
    tr = _tile(R, 256)

    def body(ids_ref, w_ref, *rest):
        rest[len(deps)][...] = w_ref[...].astype(dtype)

    return pl.pallas_call(
        body,
        name=name,
        grid_spec=pltpu.PrefetchScalarGridSpec(
            num_scalar_prefetch=1,
            grid=(R // tr,),
            in_specs=[pl.BlockSpec((tr, C), lambda r, ids: (r, 0))] + _hbm_specs(len(deps)),
            out_specs=pl.BlockSpec((None, tr, C), lambda r, ids: (ids[0], r, 0)),
        ),
        out_shape=jax.ShapeDtypeStruct((N_DEV, R, C), dtype),
        compiler_params=_params(1),
    )(ids, w, *deps)


def _xfer_start(name, bufs, plan, n):
    nb = len(bufs)

    def body(*refs):
        ins = refs[:nb]
        ssem, rsem = refs[nb], refs[nb + 1]
        token = refs[2 * nb + 2]
        x, y, c, chips = _place()
        for k, (src, dst, dev, _) in enumerate(plan(ins, x, y, c, chips)):
            _rcopy(src, dst, ssem.at[k], rsem.at[k], dev).start()
        token[...] = jnp.zeros_like(token)

    res = pl.pallas_call(
        body,
        name=name,
        out_shape=(pltpu.SemaphoreType.DMA((n,)), pltpu.SemaphoreType.DMA((n,)),
                   *[pltpu.HBM(b.shape, b.dtype) for b in bufs], jax.ShapeDtypeStruct((8, LANE), F32)),
        in_specs=[HBM_SPEC] * nb,
        out_specs=(SEM_SPEC, SEM_SPEC, *[HBM_SPEC] * nb, pl.BlockSpec(memory_space=pltpu.VMEM)),
        input_output_aliases={i: 2 + i for i in range(nb)},
        compiler_params=pltpu.CompilerParams(has_side_effects=SIDE_EFFECT),
    )(*[pltpu.with_memory_space_constraint(b, pltpu.HBM) for b in bufs])
    return dict(sems=res[:2], bufs=list(res[2:2 + nb]), token=res[2 + nb], plan=plan, n=n)


def _xfer_wait(name, started, after):
    bufs, plan = started["bufs"], started["plan"]
    nb = len(bufs)

    def body(*refs):
        ins = refs[:nb]
        ssem, rsem = refs[nb], refs[nb + 1]
        x, y, c, chips = _place()
        for k, (src, _, dev, land) in enumerate(plan(ins, x, y, c, chips)):
            cp = _rcopy(src, land, ssem.at[k], rsem.at[k], dev)
            cp.wait_send()
            cp.wait_recv()

    res = pl.pallas_call(
        body,
        name=name,
        out_shape=tuple(pltpu.HBM(b.shape, b.dtype) for b in bufs),
        in_specs=[HBM_SPEC] * nb + [SEM_SPEC, SEM_SPEC, pl.BlockSpec(memory_space=pl.ANY)],
        out_specs=tuple([HBM_SPEC] * nb),
        input_output_aliases={i: i for i in range(nb)},
        compiler_params=pltpu.CompilerParams(has_side_effects=SIDE_EFFECT),
    )(*bufs, *started["sems"], after)
    return list(res)


def _plan_gather_chips(refs, x, y, c, chips):
    plan = []
    for land in refs:
        mine = land.at[4 * x + 2 * y + c]
        plan.append((mine, mine, (x, y, 1 - c), land.at[4 * x + 2 * y + (1 - c)]))
        for px, py in chips:
            plan.append((mine, mine, (px, py, c), land.at[4 * px + 2 * py + c]))
    return plan


def _plan_gather_pass(refs, x, y, c, chips):
    return [(land.at[4 * px + 2 * py + c], land.at[4 * px + 2 * py + c], (x, y, 1 - c),
             land.at[4 * px + 2 * py + (1 - c)]) for land in refs for px, py in chips]


def _plan_reduce_core(refs, x, y, c, chips):
    grad, recv = refs
    return [(grad.at[2 * q + (1 - c)], recv.at[q], (x, y, 1 - c), recv.at[q]) for q in range(N_CHIP)]


def _plan_reduce_chips(refs, x, y, c, chips):
    part, land = refs
    return [(part.at[2 * px + py], land.at[2 * x + y], (px, py, c), land.at[2 * px + py]) for px, py in chips]


def _chip_sums(name, ids, grad, recv):
    _, R, C = grad.shape
    tr = _tile(R, 256)

    def body(ids_ref, g_ref, r_ref, o_ref):
        o_ref[...] = (g_ref[...] + r_ref[...]).astype(BF16)

    return pl.pallas_call(
        body,
        name=name,
        grid_spec=pltpu.PrefetchScalarGridSpec(
            num_scalar_prefetch=1,
            grid=(N_CHIP - 1, R // tr),
            in_specs=[pl.BlockSpec((None, tr, C), lambda q, r, ids: (2 * ids[3 + q] + ids[2], r, 0)),
                      pl.BlockSpec((None, tr, C), lambda q, r, ids: (ids[3 + q], r, 0))],
            out_specs=pl.BlockSpec((None, tr, C), lambda q, r, ids: (ids[3 + q], r, 0)),
        ),
        out_shape=jax.ShapeDtypeStruct((N_CHIP, R, C), BF16),
        compiler_params=_params(2),
    )(ids, grad, recv)


def _adamw(w, g, m, v):
    m = ADAM_B1 * m + (1.0 - ADAM_B1) * g
    v = ADAM_B2 * v + (1.0 - ADAM_B2) * (g * g)
    m_hat = m / (1.0 - ADAM_B1 ** ADAM_STEP)
    v_hat = v / (1.0 - ADAM_B2 ** ADAM_STEP)
    delta = -ADAM_LR * (m_hat / (jnp.sqrt(v_hat) + ADAM_EPS) + ADAM_WD * w)
    return delta, m, v


def _reduce_adamw(name, ids, grad, recv, landed, w, m, v):
    _, R, C = grad.shape
    tr = _tile(R, 256)

    def body(ids_ref, g_ref, r_ref, l1_ref, l2_ref, l3_ref, w_ref, m_ref, v_ref, go_ref, do_ref, mo_ref, vo_ref):
        g = g_ref[...] + r_ref[...]
        g = g + l1_ref[...].astype(F32)
        g = g + l2_ref[...].astype(F32)
        g = g + l3_ref[...].astype(F32)
        delta, mn, vn = _adamw(w_ref[...], g, m_ref[...], v_ref[...])
        go_ref[...] = g
        do_ref[...] = delta
        mo_ref[...] = mn
        vo_ref[...] = vn

    def pick(k):
        return pl.BlockSpec((None, tr, C), lambda r, ids: (ids[k], r, 0))

    flat = pl.BlockSpec((tr, C), lambda r, ids: (r, 0))
    shp = jax.ShapeDtypeStruct((R, C), F32)
    return pl.pallas_call(
        body,
        name=name,
        grid_spec=pltpu.PrefetchScalarGridSpec(
            num_scalar_prefetch=1,
            grid=(R // tr,),
            in_specs=[pick(0), pick(1), pick(3), pick(4), pick(5), flat, flat, flat],
            out_specs=[flat, flat, flat, flat],
        ),
        out_shape=[shp, shp, shp, shp],
        compiler_params=_params(1),
    )(ids, grad, recv, landed, landed, landed, w, m, v)


SMALL = ("conv_w", "conv_norm_g", "w_gate_up", "gate_bias", "gla_norm_g", "ln1_g", "ln1_b", "ln2_g", "ln2_b")
R_LOSS = 14


def _small_rows(D, conv_cols, gate_cols):
    nv = max(1, D // SP_COLS)
    assert nv <= 2, D
    return {"conv_w": (0, 3, conv_cols), "conv_norm_g": (3, 1, D_CONV), "gate_bias": (4, 1, D_GLA_K),
            "gla_norm_g": (5, 1, D_GLA_V), "ln1_g": (6, nv, D), "ln1_b": (8, nv, D), "ln2_g": (10, nv, D),
            "ln2_b": (12, nv, D), "w_gate_up": (16, GATE_RANK, gate_cols)}


def _put(o_ref, entry, val):
    row, n_rows, cols = entry
    if val.shape[0] == 1 and n_rows > 1:
        for r in range(n_rows):
            o_ref[row + r:row + r + 1, :] = val[:, r * SP_COLS:(r + 1) * SP_COLS]
    else:
        o_ref[row:row + n_rows, 0:cols] = val[0:n_rows, 0:cols]


def _take(g, entry):
    row, n_rows, cols = entry
    if cols > SP_COLS:
        return jnp.concatenate([g[row + r:row + r + 1, :] for r in range(n_rows)], axis=1)
    return g[row:row + n_rows, 0:cols]


def _make_pack(name, rows, pieces):
    names = list(pieces)

    def body(*refs):
        o_ref = refs[len(names)]
        o_ref[...] = jnp.zeros_like(o_ref)
        for nm, ref in zip(names, refs):
            if nm == "loss":
                o_ref[R_LOSS:R_LOSS + 1, 0:1] = jnp.sum(ref[...], axis=1, keepdims=True)
            else:
                _put(o_ref, rows[nm], ref[...])

    arrs = [pieces[nm] for nm in names]
    return pl.pallas_call(
        body,
        name=name,
        grid=(1,),
        in_specs=[_full(a.shape) for a in arrs],
        out_specs=_full((SP_ROWS, SP_COLS)),
        out_shape=jax.ShapeDtypeStruct((SP_ROWS, SP_COLS), F32),
        compiler_params=_params(1),
    )(*arrs)


def _small_adamw(packs, rows, w, m, v):
    names = list(SMALL)
    n = len(names)

    def body(p_ref, *refs):
        ins, outs = refs[:3 * n], refs[3 * n:]
        g = p_ref[0]
        for dvc in range(1, N_DEV):
            g = g + p_ref[dvc]
        for i, nm in enumerate(names):
            gp = _take(g, rows[nm])
            delta, mn, vn = _adamw(ins[i][...], gp, ins[n + i][...], ins[2 * n + i][...])
            for kind, val in enumerate((gp, delta, mn, vn)):
                outs[kind * n + i][...] = val
        outs[4 * n][...] = g[R_LOSS:R_LOSS + 1, 0:1]

    arrs = [w[nm] for nm in names] + [m[nm] for nm in names] + [v[nm] for nm in names]
    shapes = [jax.ShapeDtypeStruct(w[nm].shape, F32) for nm in names] * 4 + [jax.ShapeDtypeStruct((1, 1), F32)]
    res = pl.pallas_call(
        body,
        name="small_adamw",
        grid=(1,),
        in_specs=[_full(packs.shape)] + [_full(a.shape) for a in arrs],
        out_specs=[_full(sh.shape) for sh in shapes],
        out_shape=shapes,
        compiler_params=_params(1),
    )(packs, *arrs)
    return [dict(zip(names, res[k * n:(k + 1) * n])) for k in range(4)], res[4 * n]


def _w_in_pieces():
    cs = D_IN_PROJ // N_DEV
    pieces = []
    for d in range(N_DEV):
        lo, hi = d * cs, (d + 1) * cs
        if hi <= CONV_COLS:
            pieces.append((d, 0, cs, lo))
        elif lo >= CONV_COLS:
            pieces.append((d, 0, cs, lo - CONV_COLS + HALF_P))
        else:
            pieces.append((d, 0, CONV_COLS - lo, lo))
            pieces.append((d, CONV_COLS - lo, cs, HALF_P))
    return pieces


def _w_in_full(gathered):
    nb, D, cs = gathered.shape
    tr = _tile(D, 256)

    def body(g_ref, o_ref):
        o_ref[:, CONV_COLS:HALF_P] = jnp.zeros((tr, HALF_P - CONV_COLS), o_ref.dtype)
        o_ref[:, HALF_P + GLA_COLS:P_INT] = jnp.zeros((tr, HALF_P - GLA_COLS), o_ref.dtype)
        for d, a, b, dst in _w_in_pieces():
            o_ref[:, dst:dst + (b - a)] = g_ref[d, :, a:b]

    return pl.pallas_call(
        body,
        name="w_in_full",
        grid=(D // tr,),
        in_specs=[pl.BlockSpec((nb, tr, cs), lambda r: (0, r, 0))],
        out_specs=pl.BlockSpec((tr, P_INT), lambda r: (r, 0)),
        out_shape=jax.ShapeDtypeStruct((D, P_INT), gathered.dtype),
        compiler_params=_params(1),
    )(gathered)


def _w_in_blocks(dw):
    D = dw.shape[0]
    cs = D_IN_PROJ // N_DEV
    tr = _tile(D, 256)

    def body(w_ref, o_ref):
        for d, a, b, src in _w_in_pieces():
            o_ref[d, :, a:b] = w_ref[:, src:src + (b - a)]

    return pl.pallas_call(
        body,
        name="w_in_blocks",
        grid=(D // tr,),
        in_specs=[pl.BlockSpec((tr, P_INT), lambda r: (r, 0))],
        out_specs=pl.BlockSpec((N_DEV, tr, cs), lambda r: (0, r, 0)),
        out_shape=jax.ShapeDtypeStruct((N_DEV, D, cs), dw.dtype),
        compiler_params=_params(1),
    )(dw)


BIG = ("w_in", "w_out", "w_ff_up", "w_ff_down")
ORDER = ("w_in", "conv_w", "conv_norm_g", "w_gate_up", "gate_bias", "gla_norm_g", "w_out", "ln1_g", "ln1_b",
         "w_ff_up", "w_ff_down", "ln2_g", "ln2_b")


def kernel(x, w_in, conv_w, conv_norm_g, w_gate_up, gate_bias, gla_norm_g, w_out, ln1_g, ln1_b, w_ff_up, w_ff_down, ln2_g, ln2_b, loss_target, m_w_in, m_conv_w, m_conv_norm_g, m_w_gate_up, m_gate_bias, m_gla_norm_g, m_w_out, m_ln1_g, m_ln1_b, m_w_ff_up, m_w_ff_down, m_ln2_g, m_ln2_b, v_w_in, v_conv_w, v_conv_norm_g, v_w_gate_up, v_gate_bias, v_gla_norm_g, v_w_out, v_ln1_g, v_ln1_b, v_w_ff_up, v_w_ff_down, v_ln2_g, v_ln2_b):
    T, D = x.shape[1], x.shape[2]
    xs, target = x[0], loss_target[0]
    xi, yi, ci = lax.axis_index("x"), lax.axis_index("y"), lax.axis_index("c")
    chip = 2 * xi + yi
    dev = 2 * chip + ci
    others = [jnp.where(chip <= q, q + 1, q) for q in range(N_CHIP - 1)]
    ids = jnp.stack([dev, chip, ci] + others).astype(jnp.int32)
    conv_cols, gate_cols = conv_w.shape[2], w_gate_up.shape[2]

    def gather(nm, lands):
        return _xfer_start("gather_chips_" + nm, lands, _plan_gather_chips, 4 * len(lands))

    def pass_on(nm, started, after):
        lands = _xfer_wait("gather_chips_wait_" + nm, started, after)
        return _xfer_start("gather_pass_" + nm, lands, _plan_gather_pass, 3 * len(lands))

    def landed(nm, started, after):
        return _xfer_wait("gather_pass_wait_" + nm, started, after)

    rows = _small_rows(D, conv_cols, gate_cols)
    fwd_pack = _make_pack("pack_fwd", rows, {"conv_w": conv_w[0], "w_gate_up": w_gate_up[0]})
    ga_in = gather("w_in", [_cast_place("cast_place_w_in", ids, w_in[0]),
                            _cast_place("cast_place_pack", ids, fwd_pack, dtype=F32)])
    ga, dep = [], ga_in["token"]
    m_in, v_in = m_w_in[0], v_w_in[0]
    for nm, w in zip(BIG[1:], (w_out, w_ff_up, w_ff_down)):
        deps = [dep, m_in, v_in] if nm == "w_ff_down" else [dep]
        ga.append(gather(nm, [_cast_place("cast_place_" + nm, ids, w[0], deps)]))
        dep = ga[-1]["token"]
    xb = _cast_bf16(xs, [dep])
    gp_in = pass_on("w_in", ga_in, xb)
    g_in, g_pack = landed("w_in", gp_in, gp_in["token"])
    w_full = _w_in_full(g_in)
    r_cw, r_gw = rows["conv_w"][0], rows["w_gate_up"][0]
    conv_w_full = g_pack[:, r_cw:r_cw + 3, :conv_cols].transpose(1, 0, 2).reshape(3, -1)
    gate_w_full = g_pack[:, r_gw:r_gw + GATE_RANK, :gate_cols].transpose(1, 0, 2).reshape(GATE_RANK, -1)
    conv_w8 = jnp.pad(conv_w_full, ((0, 5), (0, 0)))
    wg128 = jnp.pad(gate_w_full, ((0, LANE - GATE_RANK), (0, 0))).astype(BF16)
    proj = _proj_fwd(xb, w_full)
    yin = _conv_fwd(proj, conv_w8, conv_norm_g)
    gp_out = pass_on("w_out", ga[0], yin)
    o_all, states, yin = _gla_fwd(proj, wg128, gate_bias, gla_norm_g, yin, deps=[gp_out["token"]])
    w_out_full = landed("w_out", gp_out, o_all)[0].reshape(-1, D)
    gp_up = pass_on("w_ff_up", ga[1], o_all)
    xhat1, x1, rstd1 = _mix_ln1(yin, w_out_full, xs, ln1_g, ln1_b, deps=[gp_up["token"]])
    (w_up_blk,) = landed("w_ff_up", gp_up, x1)
    half = N_DEV // 2
    ra, h2 = _ff_up(x1, w_up_blk, 0, half)
    gp_down = pass_on("w_ff_down", ga[2], ra)
    ra, h2 = _ff_up(x1, w_up_blk, half, N_DEV - half, prev=(ra, h2), deps=[gp_down["token"]])
    w_down_full = landed("w_ff_down", gp_down, ra)[0].reshape(-1, D)
    dh3, dh3b, g_ln2_g, g_ln2_b, loss = _ff_down_loss(h2, w_down_full, xhat1, target, ln1_g, ln1_b, ln2_g, ln2_b)

    def to_core(nm, grad):
        recv = lax.empty((N_CHIP,) + grad.shape[1:], F32)
        return _xfer_start("reduce_core_" + nm, [grad, recv], _plan_reduce_core, N_CHIP)

    def to_chips(nm, started, after):
        grad, recv = _xfer_wait("reduce_core_wait_" + nm, started, after)
        part = _chip_sums("chip_sums_" + nm, ids, grad, recv)
        land = lax.empty(part.shape, BF16)
        return grad, recv, _xfer_start("reduce_chips_" + nm, [part, land], _plan_reduce_chips, N_CHIP - 1)

    da = _ff_down_bwd_act(dh3b, w_down_full, ra)
    gw_down = _grad_w("grad_w_down", h2, dh3b).reshape(N_DEV, -1, D)
    rc_down = to_core("w_ff_down", gw_down)
    gw_up = _grad_w_up_blk(x1, da, N_DEV, deps=[rc_down["token"]])
    gw_down, rv_down, rs_down = to_chips("w_ff_down", rc_down, gw_up)
    rc_up = to_core("w_ff_up", gw_up)
    dh1, dh1b, g_ln1_g, g_ln1_b = _ff_up_bwd_ln1(da, w_up_blk, dh3, xhat1, rstd1, ln1_g,
                                                 deps=[rs_down["token"], rc_up["token"]])
    gw_up, rv_up, rs_up = to_chips("w_ff_up", rc_up, dh1b)
    dyin = _mix_bwd(dh1b, w_out_full, deps=[rs_up["token"]])
    gw_out = _grad_w("grad_w_out", yin, dh1b).reshape(N_DEV, -1, D)
    rc_out = to_core("w_out", gw_out)
    dproj, g_conv_w, g_conv_g = _conv_bwd(proj, dyin, conv_w8, conv_norm_g, deps=[rc_out["token"]])
    dproj, g_gate_w, g_gate_b, g_gla_g = _gla_bwd(proj, wg128, gate_bias, gla_norm_g, o_all, states, dyin, dproj)
    gw_out, rv_out, rs_out = to_chips("w_out", rc_out, dproj)
    gw_in = _w_in_blocks(_grad_w("grad_w_in", xb, dproj, tn_pref=1280, tk_pref=2048, deps=[rs_out["token"]]))
    rc_in = to_core("w_in", gw_in)

    big = {}

    def finish(nm, grad, recv, started, w, m, v, after):
        _, land = _xfer_wait("reduce_chips_wait_" + nm, started, after)
        res = _reduce_adamw("adamw_" + nm, ids, grad, recv, land, w[0], m[0], v[0])
        big[nm] = [a[None] for a in res]
        return res[0]

    done = finish("w_ff_down", gw_down, rv_down, rs_down, w_ff_down, m_w_ff_down, v_w_ff_down, rc_in["token"])
    done = finish("w_ff_up", gw_up, rv_up, rs_up, w_ff_up, m_w_ff_up, v_w_ff_up, done)
    done = finish("w_out", gw_out, rv_out, rs_out, w_out, m_w_out, v_w_out, done)
    full_rows = _small_rows(D, D_CONV, D_GLA_K)
    pack = _make_pack("pack_grads", full_rows, {
        "conv_w": g_conv_w, "conv_norm_g": g_conv_g, "gate_bias": g_gate_b, "gla_norm_g": g_gla_g, "ln1_g": g_ln1_g,
        "ln1_b": g_ln1_b, "ln2_g": g_ln2_g, "ln2_b": g_ln2_b, "loss": loss, "w_gate_up": g_gate_w})
    (packs,) = _all_gather("gather_small_grads", [pack], deps=[done])
    gw_in, rv_in, rs_in = to_chips("w_in", rc_in, packs)
    grad_x = _proj_bwd_x(dproj, w_full, dh1, deps=[rs_in["token"]])
    finish("w_in", gw_in, rv_in, rs_in, w_in, (m_in,), (v_in,), grad_x)

    def own_cols(row, n_rows, width):
        cut = lax.dynamic_slice(packs, (0, row, dev * width), (N_DEV, n_rows, width))
        return jnp.pad(cut, ((0, 0), (0, 0), (0, SP_COLS - width)))

    packs_own = jnp.concatenate([own_cols(r_cw, 3, conv_cols), packs[:, r_cw + 3:r_gw],
                                 own_cols(r_gw, GATE_RANK, gate_cols)], axis=1)
    as2d = lambda a: a[0] if a.ndim == 3 else a
    w_s = dict(zip(SMALL, map(as2d, (conv_w, conv_norm_g, w_gate_up, gate_bias, gla_norm_g, ln1_g, ln1_b, ln2_g, ln2_b))))
    m_s = dict(zip(SMALL, map(as2d, (m_conv_w, m_conv_norm_g, m_w_gate_up, m_gate_bias, m_gla_norm_g, m_ln1_g,
                                     m_ln1_b, m_ln2_g, m_ln2_b))))
    v_s = dict(zip(SMALL, map(as2d, (v_conv_w, v_conv_norm_g, v_w_gate_up, v_gate_bias, v_gla_norm_g, v_ln1_g,
                                     v_ln1_b, v_ln2_g, v_ln2_b))))
    small, loss_sum = _small_adamw(packs_own, rows, w_s, m_s, v_s)

    def leaf(kind, name):
        if name in BIG:
            return big[name][kind]
        a = small[kind][name]
        return a[None] if name in ("conv_w", "w_gate_up") else a

    out = [loss_sum[0, 0], grad_x[None]]
    for kind in range(4):
        out += [leaf(kind, nm) for nm in ORDER]
    return tuple(out)
```

```python
import jax
import jax.numpy as jnp
from jax import lax
from jax.experimental import pallas as pl
from jax.experimental.pallas import tpu as pltpu

F32 = jnp.float32
BF16 = jnp.bfloat16

D_CONV = 1024
CONV_GROUPS = 8
GLA_HEADS = 4
HEAD_K = 128
HEAD_V = 256
D_GLA_K = 512
D_GLA_V = 1024
GATE_RANK = 16
GATE_TAU = 16.0
CHUNK = 64
LN_EPS = 1e-5
RMS_EPS = 1e-6
DN_ALPHA = 2.0 ** 0.25
D_IN_PROJ = 6160
ADAM_LR = 0.001
ADAM_B1 = 0.9
ADAM_B2 = 0.999
ADAM_EPS = 1e-08
ADAM_WD = 0.01
ADAM_STEP = 10

N_DEV = 8
N_CHIP = 4
LANE = 128
HALF_P = 3200
P_INT = 2 * HALF_P
CONV_COLS = 3 * D_CONV
GLA_COLS = D_IN_PROJ - CONV_COLS
SP_ROWS = 32
SP_COLS = 1024
VMEM_LIMIT = 56 * 1024 * 1024

NN = ((1,), (0,))
NT = ((1,), (1,))
TN = ((0,), (0,))
MESH = pl.DeviceIdType.MESH


def _dot(a, b, dims, precision=None):
    return lax.dot_general(a, b, (dims, ((), ())), preferred_element_type=F32, precision=precision)


def _tile(n, pref):
    if n <= pref:
        return n
    t = (pref // LANE) * LANE
    while t > 0 and n % t:
        t -= LANE
    assert t > 0, (n, pref)
    return t


def _params(n_axes):
    return pltpu.CompilerParams(dimension_semantics=("arbitrary",) * n_axes, vmem_limit_bytes=VMEM_LIMIT)


def _full(shape):
    nd = len(shape)
    return pl.BlockSpec(shape, lambda *_: (0,) * nd)


def _hbm_specs(n):
    return [pl.BlockSpec(memory_space=pl.ANY)] * n


def _mm(name, mode, a, b, *, M, N, K, tm, tn, tk, outs, epilogue, extras=(), a_fn=None, a_spec=None, b_spec=None,
        deps=()):
    ni, nj, nk = M // tm, N // tn, K // tk
    assert ni * tm == M and nj * tn == N and nk * tk == K, (name, M, N, K, tm, tn, tk)
    if a_spec is None:
        a_spec = (pl.BlockSpec((tk, tm), lambda i, j, k: (k, i)) if mode == "tn"
                  else pl.BlockSpec((tm, tk), lambda i, j, k: (i, k)))
    if b_spec is None:
        b_spec = (pl.BlockSpec((tn, tk), lambda i, j, k: (j, k)) if mode == "nt"
                  else pl.BlockSpec((tk, tn), lambda i, j, k: (k, j)))
    dims = {"nn": NN, "nt": NT, "tn": TN}[mode]
    n_ex, n_out, n_dep = len(extras), len(outs), len(deps)

    def body(*refs):
        a_ref, b_ref = refs[0], refs[1]
        ex = refs[2:2 + n_ex]
        o = refs[2 + n_ex + n_dep:2 + n_ex + n_dep + n_out]
        acc_ref = refs[2 + n_ex + n_dep + n_out]
        i, j, k = pl.program_id(0), pl.program_id(1), pl.program_id(2)
        if nk > 1:
            @pl.when(k == 0)
            def _():
                acc_ref[...] = jnp.zeros_like(acc_ref)

        av = a_ref[...]
        if a_fn is not None:
            av = a_fn(av)
        part = _dot(av, b_ref[...], dims)
        if nk == 1 and epilogue is None:
            o[0][...] = part.astype(o[0].dtype)
        elif nk == 1:
            acc_ref[...] = part
            epilogue(acc_ref, ex, o, i, j)
        else:
            acc_ref[...] += part

            @pl.when(k == nk - 1)
            def _():
                if epilogue is None:
                    o[0][...] = acc_ref[...].astype(o[0].dtype)
                else:
                    epilogue(acc_ref, ex, o, i, j)

    return pl.pallas_call(
        body,
        name=name,
        grid=(ni, nj, nk),
        in_specs=[a_spec, b_spec] + [s for _, s in extras] + _hbm_specs(n_dep),
        out_specs=[s for _, s in outs],
        out_shape=[s for s, _ in outs],
        scratch_shapes=[pltpu.VMEM((8, LANE) if nk == 1 and epilogue is None else (tm, tn), F32)],
        compiler_params=_params(3),
    )(a, b, *[x for x, _ in extras], *deps)


def _mm_rows(name, mode, a, b, *, M, N, K, tm, tk, row_ins, vec_ins, row_outs, stat_outs, chunk_fn,
             b_spec=None, deps=()):
    ni, nk = M // tm, K // tk
    rc = tm // nk
    assert ni * tm == M and nk * tk == K and rc * nk == tm and rc % 16 == 0, (name, M, K, tm, tk)
    dims = {"nn": NN, "nt": NT}[mode]
    last = ni - 1
    n_split = 2 if N % (2 * 256) == 0 else 1

    def kk(i, k):
        return jnp.where(i < ni, k, nk - 1)

    a_spec = pl.BlockSpec((tm, tk), lambda i, k: (jnp.minimum(i, last), kk(i, k)))
    if b_spec is None:
        b_spec = (pl.BlockSpec((N, tk), lambda i, k: (0, kk(i, k))) if mode == "nt"
                  else pl.BlockSpec((tk, N), lambda i, k: (kk(i, k), 0)))
    prev_rows = lambda i, k: (jnp.maximum((i - 1) * nk + k, 0), 0)
    n_ri, n_vi, n_ro, n_so, n_dep = len(row_ins), len(vec_ins), len(row_outs), len(stat_outs), len(deps)

    def body(*refs):
        a_ref, b_ref = refs[0], refs[1]
        pos = 2
        ri = refs[pos:pos + n_ri]; pos += n_ri
        vi = refs[pos:pos + n_vi]; pos += n_vi + n_dep
        ro = refs[pos:pos + n_ro]; pos += n_ro
        so = refs[pos:pos + n_so]; pos += n_so
        accs = refs[pos:pos + 2]
        i, k = pl.program_id(0), pl.program_id(1)

        @pl.when((i == 0) & (k == 0))
        def _():
            accs[0][...] = jnp.zeros_like(accs[0])
            accs[1][...] = jnp.zeros_like(accs[1])
            for st in so:
                st[...] = jnp.zeros_like(st)

        def finish_rows(prev_ref):
            rows = pl.ds(pl.multiple_of(k * rc, rc), rc)
            done = prev_ref[rows, :]
            prev_ref[rows, :] = jnp.zeros((rc, N), F32)
            chunk_fn(done, i > 0, ri, vi, ro, so)

        def accumulate(acc_ref):
            av = a_ref[...]
            for c0 in range(0, N, N // n_split):
                cols = slice(c0, c0 + N // n_split)
                bv = b_ref[cols, :] if mode == "nt" else b_ref[:, cols]
                acc_ref[:, cols] += _dot(av, bv, dims)

        for parity in (0, 1):
            @pl.when((i < ni) & (lax.rem(i, 2) == parity))
            def _(parity=parity):
                finish_rows(accs[1 - parity])
                accumulate(accs[parity])

        @pl.when(i == ni)
        def _():
            finish_rows(accs[last % 2])

    row_spec = lambda arr: pl.BlockSpec((rc, arr.shape[1]), prev_rows)
    return pl.pallas_call(
        body,
        name=name,
        grid=(ni + 1, nk),
        in_specs=[a_spec, b_spec] + [row_spec(x) for x in row_ins] + [_full(x.shape) for x in vec_ins]
        + _hbm_specs(n_dep),
        out_specs=[row_spec(s) for s in row_outs] + [_full(s.shape) for s in stat_outs],
        out_shape=list(row_outs) + list(stat_outs),
        scratch_shapes=[pltpu.VMEM((tm, N), F32), pltpu.VMEM((tm, N), F32)],
        compiler_params=_params(2),
    )(a, b, *row_ins, *vec_ins, *deps)


SUB_ROWS = 16


def _by_sub_rows(n_rows, fn):
    sums = None
    for r0 in range(0, n_rows, SUB_ROWS):
        part = fn(slice(r0, r0 + SUB_ROWS))
        if part:
            sums = part if sums is None else tuple(x + y for x, y in zip(sums, part))
    return sums


def _to_bf16(v):
    return v.astype(BF16)


def _ln_bwd(dy, xhat, rstd, g):
    dxh = dy * g
    m1 = jnp.mean(dxh, axis=-1, keepdims=True)
    m2 = jnp.mean(dxh * xhat, axis=-1, keepdims=True)
    return rstd * (dxh - m1 - xhat * m2)


def _ln_fwd(h):
    mu = jnp.mean(h, axis=-1, keepdims=True)
    xc = h - mu
    var = jnp.mean(xc * xc, axis=-1, keepdims=True)
    rstd = lax.rsqrt(var + LN_EPS)
    return xc * rstd, rstd


def _proj_fwd(x, w_full, deps=()):
    T, D = x.shape
    P = w_full.shape[1]
    tm, tn = _tile(T, 1024), _tile(P, 1280)
    return _mm("proj_fwd", "nn", x, w_full, M=T, N=P, K=D, tm=tm, tn=tn, tk=D,
               outs=[(jax.ShapeDtypeStruct((T, P), F32), pl.BlockSpec((tm, tn), lambda i, j, k: (i, j)))],
               epilogue=None, deps=deps)[0]


def _cast_bf16(x, deps=()):
    T, D = x.shape
    tm = _tile(T, 512)

    def body(x_ref, *rest):
        rest[len(deps)][...] = x_ref[...].astype(BF16)

    return pl.pallas_call(
        body,
        name="cast_x",
        grid=(T // tm,),
        in_specs=[pl.BlockSpec((tm, D), lambda i: (i, 0))] + _hbm_specs(len(deps)),
        out_specs=pl.BlockSpec((tm, D), lambda i: (i, 0)),
        out_shape=jax.ShapeDtypeStruct((T, D), BF16),
        compiler_params=_params(1),
    )(x, *deps)


def _conv_shift(h, hp):
    row = lax.broadcasted_iota(jnp.int32, h.shape, 0)
    hm1 = hp[7:8, :]
    hm2 = hp[6:7, :]
    h1 = jnp.where(row == 0, hm1, pltpu.roll(h, 1, 0))
    h2 = jnp.where(row == 0, hm2, jnp.where(row == 1, hm1, pltpu.roll(h, 2, 0)))
    return h1, h2


def _conv_fwd(proj, conv_w8, conv_g):
    T = proj.shape[0]
    tt = _tile(T, 256)
    nt = T // tt
    t8 = tt // 8

    def body(b_ref, c_ref, u_ref, cp_ref, up_ref, w_ref, g_ref, yin_ref):
        i = pl.program_id(0)
        h = c_ref[...] * u_ref[...]
        hp = jnp.where(i > 0, cp_ref[...] * up_ref[...], 0.0)
        h1, h2 = _conv_shift(h, hp)
        w = w_ref[...]
        y = w[0:1, :] * h2 + w[1:2, :] * h1 + w[2:3, :] * h
        p = b_ref[...] * y
        parts = []
        for gi in range(CONV_GROUPS):
            pg = p[:, gi * LANE:(gi + 1) * LANE]
            r = lax.rsqrt(jnp.mean(pg * pg, axis=-1, keepdims=True) + RMS_EPS)
            parts.append(pg * r)
        yn = jnp.concatenate(parts, axis=1) * g_ref[...]
        yin_ref[...] = yn.astype(BF16)

    def col(cidx):
        return pl.BlockSpec((tt, D_CONV), lambda i: (i, cidx))

    def prev(cidx):
        return pl.BlockSpec((8, D_CONV), lambda i: (jnp.maximum(i * t8 - 1, 0), cidx))

    return pl.pallas_call(
        body,
        name="conv_fwd",
        grid=(nt,),
        in_specs=[col(0), col(1), col(2), prev(1), prev(2), _full((8, D_CONV)), _full((1, D_CONV))],
        out_specs=pl.BlockSpec((tt, D_CONV), lambda i: (i, 0)),
        out_shape=jax.ShapeDtypeStruct((T, 2 * D_CONV), BF16),
        compiler_params=_params(1),
    )(proj, proj, proj, proj, proj, conv_w8, conv_g)


def _log_sigmoid(z):
    return jnp.minimum(z, 0.0) - jnp.log(1.0 + jnp.exp(-jnp.abs(z)))


STEP_CHUNKS = 4
STEP_ROWS = STEP_CHUNKS * CHUNK


def _gla_step_terms(blk, wg_ref, gb_ref):
    zl = blk[:, 3072:3200]
    z = _dot(zl.astype(BF16), wg_ref[...], NN) + gb_ref[...]
    log_a = _log_sigmoid(z) * (1.0 / GATE_TAU)
    ri = lax.broadcasted_iota(jnp.int32, (STEP_ROWS, STEP_ROWS), 0)
    ci = lax.broadcasted_iota(jnp.int32, (STEP_ROWS, STEP_ROWS), 1)
    same = (ri // CHUNK) == (ci // CHUNK)
    lower = (same & (ri >= ci)).astype(F32)
    bcum = _dot(lower, log_a, NN, precision=lax.Precision.HIGHEST)
    return zl, z, bcum, same


def _causal():
    return (lax.broadcasted_iota(jnp.int32, (CHUNK, CHUNK), 0) >= lax.broadcasted_iota(jnp.int32, (CHUNK, CHUNK), 1))


def _gla_head_terms(q, k, bcum, h):
    sl = slice(h * HEAD_K, (h + 1) * HEAD_K)
    bh = bcum[:, sl]
    bl = bh[CHUNK - 1:CHUNK, :]
    eb = jnp.exp(bh)
    enb = jnp.exp(-bh)
    eend = jnp.exp(bl - bh)
    dec = jnp.exp(bl)
    qd = q[:, sl] * (HEAD_K ** -0.5) * eb
    ki = k[:, sl] * enb
    ke = k[:, sl] * eend
    return eb, enb, eend, dec, qd, ki, ke


def _sigmoid(x):
    return 1.0 / (1.0 + jnp.exp(-x))


def _gla_fwd(proj, wg128, gbias, gng, yin, deps=()):
    T = proj.shape[0]
    nch = T // CHUNK
    nst = T // STEP_ROWS

    def body(p_ref, wg_ref, gb_ref, gn_ref, yin_in_ref, *rest):
        o_ref, st_ref, yin_ref, s_ref = rest[len(deps):]
        n = pl.program_id(0)

        @pl.when(n == 0)
        def _():
            s_ref[...] = jnp.zeros_like(s_ref)

        blk = p_ref[...]
        _, _, bcum_all, _ = _gla_step_terms(blk, wg_ref, gb_ref)
        causal = _causal()
        gn = gn_ref[...]
        states = [s_ref[h] for h in range(GLA_HEADS)]
        for c in range(STEP_CHUNKS):
            rows = slice(c * CHUNK, (c + 1) * CHUNK)
            q, k = blk[rows, 0:512], blk[rows, 512:1024]
            v, r = blk[rows, 1024:2048], blk[rows, 2048:3072]
            bcum = bcum_all[rows, :]
            for h in range(GLA_HEADS):
                _, _, _, dec, qd, ki, ke = _gla_head_terms(q, k, bcum, h)
                vs = slice(h * HEAD_V, (h + 1) * HEAD_V)
                vb = v[:, vs].astype(BF16)
                qdb = qd.astype(BF16)
                a = jnp.where(causal, _dot(qdb, ki.astype(BF16), NT), 0.0)
                st = states[h]
                o = _dot(a.astype(BF16), vb, NN) + _dot(qdb, st.astype(BF16), NT)
                st_ref[c, h] = st
                states[h] = dec * st + _dot(vb, ke.astype(BF16), TN)
                o_ref[rows, vs] = o
                rinv = lax.rsqrt(jnp.mean(o * o, axis=-1, keepdims=True) + RMS_EPS)
                rh = r[:, vs]
                yin_ref[rows, vs] = (o * rinv * gn[:, vs] * (rh * _sigmoid(rh))).astype(BF16)
        for h in range(GLA_HEADS):
            s_ref[h] = states[h]

    return pl.pallas_call(
        body,
        name="gla_fwd",
        grid=(nst,),
        in_specs=[pl.BlockSpec((STEP_ROWS, HALF_P), lambda n: (n, 1)), _full((LANE, D_GLA_K)), _full((1, D_GLA_K)),
                  _full((1, D_GLA_V)), pl.BlockSpec(memory_space=pl.ANY)] + _hbm_specs(len(deps)),
        out_specs=[pl.BlockSpec((STEP_ROWS, D_GLA_V), lambda n: (n, 0)),
                   pl.BlockSpec((STEP_CHUNKS, GLA_HEADS, HEAD_V, HEAD_K), lambda n: (n, 0, 0, 0)),
                   pl.BlockSpec((STEP_ROWS, D_GLA_V), lambda n: (n, 1))],
        out_shape=[jax.ShapeDtypeStruct((T, D_GLA_V), F32),
                   jax.ShapeDtypeStruct((nch, GLA_HEADS, HEAD_V, HEAD_K), F32),
                   jax.ShapeDtypeStruct(yin.shape, BF16)],
        scratch_shapes=[pltpu.VMEM((GLA_HEADS, HEAD_V, HEAD_K), F32)],
        input_output_aliases={4: 2},
        compiler_params=_params(1),
    )(proj, wg128, gbias, gng, yin, *deps)


def _mix_ln1(yin, w_out, x, ln_g, ln_b, deps=()):
    T, D = x.shape
    KY = yin.shape[1]
    tm = _tile(T, 1024)

    def chunk(acc, valid, ri, vi, ro, so):
        g, b = vi[0][...], vi[1][...]

        def sub(rows):
            xhat, rstd = _ln_fwd(DN_ALPHA * ri[0][rows, :] + acc[rows, :])
            ro[0][rows, :] = xhat
            ro[1][rows, :] = (xhat * g + b).astype(BF16)
            ro[2][rows, :] = rstd

        _by_sub_rows(acc.shape[0], sub)

    return _mm_rows("mix_ln1", "nn", yin, w_out, M=T, N=D, K=KY, tm=tm, tk=_tile(KY, 512),
                    row_ins=[x], vec_ins=[ln_g, ln_b],
                    row_outs=[jax.ShapeDtypeStruct((T, D), F32), jax.ShapeDtypeStruct((T, D), BF16),
                              jax.ShapeDtypeStruct((T, 1), F32)],
                    stat_outs=[], chunk_fn=chunk, deps=deps)


def _ff_up(x1, w_up_blk, first, count, prev=None, deps=()):
    T, D = x1.shape
    nb, _, fb = w_up_blk.shape
    tm = _tile(T, 1024)
    ni = T // tm
    n_dep = len(deps) + (2 if prev is not None else 0)

    def body(a_ref, b_ref, *rest):
        ra_ref, h2_ref = rest[n_dep:n_dep + 2]
        ra = jnp.maximum(_dot(a_ref[...], b_ref[...], NN), 0.0)
        ra_ref[...] = ra.astype(BF16)
        h2_ref[...] = (ra * ra).astype(BF16)

    blk = pl.BlockSpec((tm, fb), lambda i, j: (i, first + j))
    shp = jax.ShapeDtypeStruct((T, nb * fb), BF16)
    keep = list(prev) if prev is not None else []
    return pl.pallas_call(
        body,
        name="ff_up_%d" % first,
        grid=(ni, count),
        in_specs=[pl.BlockSpec((tm, D), lambda i, j: (i, 0)),
                  pl.BlockSpec((None, D, fb), lambda i, j: (first + j, 0, 0))] + _hbm_specs(n_dep),
        out_specs=[blk, blk],
        out_shape=[shp, shp],
        input_output_aliases=({2: 0, 3: 1} if prev is not None else {}),
        compiler_params=_params(2),
    )(x1, w_up_blk, *keep, *deps)


def _ff_down_loss(h2, w_down, xhat1, target, g1, b1, g2, b2):
    T, F = h2.shape
    D = w_down.shape[1]
    tm = _tile(T, 1024)
    inv_d = 1.0 / D

    def chunk(acc, valid, ri, vi, ro, so):
        g1v, b1v, g2v, b2v = (v[...] for v in vi)

        def sub(rows):
            x1 = ri[0][rows, :] * g1v + b1v
            xhat, rstd = _ln_fwd(DN_ALPHA * x1 + acc[rows, :])
            e = xhat * g2v + b2v - ri[1][rows, :]
            dy = e * inv_d
            dh = _ln_bwd(dy, xhat, rstd, g2v)
            ro[0][rows, :] = dh
            ro[1][rows, :] = dh.astype(BF16)
            return (jnp.sum(dy * xhat, axis=0, keepdims=True), jnp.sum(dy, axis=0, keepdims=True),
                    jnp.sum(e * e, axis=0, keepdims=True))

        sg, sb, sl = _by_sub_rows(acc.shape[0], sub)
        so[0][...] += jnp.where(valid, sg, 0.0)
        so[1][...] += jnp.where(valid, sb, 0.0)
        so[2][...] += jnp.where(valid, sl * (0.5 * inv_d), 0.0)

    vshape = jax.ShapeDtypeStruct((1, D), F32)
    return _mm_rows("ff_down_loss", "nn", h2, w_down, M=T, N=D, K=F, tm=tm, tk=_tile(F, 1024),
                    row_ins=[xhat1, target], vec_ins=[g1, b1, g2, b2],
                    row_outs=[jax.ShapeDtypeStruct((T, D), F32), jax.ShapeDtypeStruct((T, D), BF16)],
                    stat_outs=[vshape, vshape, vshape], chunk_fn=chunk)


def _ff_down_bwd_act(dh3b, w_down, ra):
    T, D = dh3b.shape
    F = w_down.shape[0]
    tm, tn = _tile(T, 1024), _tile(F, 1024)

    def ep(acc_ref, ex, o, i, j):
        o[0][...] = (acc_ref[...] * (2.0 * ex[0][...].astype(F32))).astype(BF16)

    blk = pl.BlockSpec((tm, tn), lambda i, j, k: (i, j))
    return _mm("ff_down_bwd_act", "nt", dh3b, w_down, M=T, N=F, K=D, tm=tm, tn=tn, tk=D,
               outs=[(jax.ShapeDtypeStruct((T, F), BF16), blk)], extras=[(ra, blk)], epilogue=ep)[0]


def _grad_w(name, a, b, *, a_fn=None, tm_pref=1024, tn_pref=1024, tk_pref=4096, deps=()):
    T, M = a.shape
    N = b.shape[1]
    tm, tn, tk = _tile(M, tm_pref), _tile(N, tn_pref), _tile(T, tk_pref)
    return _mm(name, "tn", a, b, M=M, N=N, K=T, tm=tm, tn=tn, tk=tk, a_fn=a_fn, deps=deps,
               outs=[(jax.ShapeDtypeStruct((M, N), F32), pl.BlockSpec((tm, tn), lambda i, j, k: (i, j)))],
               epilogue=None)[0]


def _grad_w_up_blk(x1, da, nb, deps=()):
    T, D = x1.shape
    F = da.shape[1]
    fb = F // nb
    tm, tk = _tile(D, 1024), _tile(T, 4096)
    return _mm("grad_w_up", "tn", x1, da, M=D, N=F, K=T, tm=tm, tn=fb, tk=tk, deps=deps,
               outs=[(jax.ShapeDtypeStruct((nb, D, fb), F32),
                      pl.BlockSpec((None, tm, fb), lambda i, j, k: (j, i, 0)))],
               epilogue=None)[0]


def _ff_up_bwd_ln1(da, w_up_blk, dh3, xhat1, rstd1, g1, deps=()):
    T, F = da.shape
    nb, D, fb = w_up_blk.shape
    tm = _tile(T, 1024)

    def chunk(acc, valid, ri, vi, ro, so):
        g = vi[0][...]

        def sub(rows):
            dx1 = DN_ALPHA * ri[0][rows, :] + acc[rows, :]
            xhat = ri[1][rows, :]
            dh = _ln_bwd(dx1, xhat, ri[2][rows, :], g)
            ro[0][rows, :] = dh
            ro[1][rows, :] = dh.astype(BF16)
            return jnp.sum(dx1 * xhat, axis=0, keepdims=True), jnp.sum(dx1, axis=0, keepdims=True)

        sg, sb = _by_sub_rows(acc.shape[0], sub)
        so[0][...] += jnp.where(valid, sg, 0.0)
        so[1][...] += jnp.where(valid, sb, 0.0)

    nk = F // fb
    vshape = jax.ShapeDtypeStruct((1, D), F32)
    return _mm_rows("ff_up_bwd_ln1", "nt", da, w_up_blk, M=T, N=D, K=F, tm=tm, tk=fb,
                    b_spec=pl.BlockSpec((None, D, fb), lambda i, k: (jnp.where(i < T // tm, k, nk - 1), 0, 0)),
                    row_ins=[dh3, xhat1, rstd1], vec_ins=[g1],
                    row_outs=[jax.ShapeDtypeStruct((T, D), F32), jax.ShapeDtypeStruct((T, D), BF16)],
                    stat_outs=[vshape, vshape], chunk_fn=chunk, deps=deps)


def _mix_bwd(dh1b, w_out, deps=()):
    T, D = dh1b.shape
    KY = w_out.shape[0]
    tm, tn = _tile(T, 1024), _tile(KY, 1024)
    return _mm("mix_bwd", "nt", dh1b, w_out, M=T, N=KY, K=D, tm=tm, tn=tn, tk=D, deps=deps,
               outs=[(jax.ShapeDtypeStruct((T, KY), F32), pl.BlockSpec((tm, tn), lambda i, j, k: (i, j)))],
               epilogue=None)[0]


def _conv_bwd(proj, dyin, conv_w8, conv_g, deps=()):
    T = proj.shape[0]
    tt = _tile(T, 256)
    nt = T // tt
    t8 = tt // 8
    nx = tt + 8

    def body(b_ref, c_ref, u_ref, d_ref, bn_ref, cn_ref, un_ref, dn_ref, cp_ref, up_ref, w_ref, g_ref, *rest):
        dp_ref, dw_ref, dg_ref = rest[len(deps):]
        i = pl.program_id(0)

        @pl.when(i == 0)
        def _():
            dw_ref[...] = jnp.zeros_like(dw_ref)
            dg_ref[...] = jnp.zeros_like(dg_ref)

        more = i < nt - 1

        def ext(cur_ref, nxt_ref):
            return jnp.concatenate([cur_ref[...], jnp.where(more, nxt_ref[...], 0.0)], axis=0)

        bx, cx, ux, dx = ext(b_ref, bn_ref), ext(c_ref, cn_ref), ext(u_ref, un_ref), ext(d_ref, dn_ref)
        hx = cx * ux
        hp = jnp.where(i > 0, cp_ref[...] * up_ref[...], 0.0)
        h1, h2 = _conv_shift(hx, hp)
        w = w_ref[...]
        g = g_ref[...]
        yx = w[0:1, :] * h2 + w[1:2, :] * h1 + w[2:3, :] * hx
        px = bx * yx
        dps, dgs = [], []
        for gi in range(CONV_GROUPS):
            sl = slice(gi * LANE, (gi + 1) * LANE)
            pg, dg_ = px[:, sl], dx[:, sl]
            r = lax.rsqrt(jnp.mean(pg * pg, axis=-1, keepdims=True) + RMS_EPS)
            gd = g[:, sl] * dg_
            dps.append(r * gd - pg * (r * r * r) * jnp.mean(pg * gd, axis=-1, keepdims=True))
            dgs.append(jnp.sum((dg_ * pg * r)[:tt, :], axis=0, keepdims=True))
        dpx = jnp.concatenate(dps, axis=1)
        dg_ref[...] += jnp.concatenate(dgs, axis=1)
        dyx = dpx * bx
        dyc = dyx[:tt, :]
        dh = (w[2:3, :] * dyx + w[1:2, :] * pltpu.roll(dyx, nx - 1, 0) + w[0:1, :] * pltpu.roll(dyx, nx - 2, 0))[:tt, :]
        dw_ref[0:1, :] += jnp.sum(dyc * h2[:tt, :], axis=0, keepdims=True)
        dw_ref[1:2, :] += jnp.sum(dyc * h1[:tt, :], axis=0, keepdims=True)
        dw_ref[2:3, :] += jnp.sum(dyc * hx[:tt, :], axis=0, keepdims=True)
        dp_ref[:, 0:D_CONV] = (dpx * yx)[:tt, :].astype(BF16)
        dp_ref[:, D_CONV:2 * D_CONV] = (dh * u_ref[...]).astype(BF16)
        dp_ref[:, 2 * D_CONV:3 * D_CONV] = (dh * c_ref[...]).astype(BF16)
        dp_ref[:, 3 * D_CONV:HALF_P] = jnp.zeros((tt, HALF_P - 3 * D_CONV), BF16)

    def col(cidx):
        return pl.BlockSpec((tt, D_CONV), lambda i: (i, cidx))

    def nxt(cidx):
        return pl.BlockSpec((8, D_CONV), lambda i: (jnp.minimum((i + 1) * t8, T // 8 - 1), cidx))

    def prev(cidx):
        return pl.BlockSpec((8, D_CONV), lambda i: (jnp.maximum(i * t8 - 1, 0), cidx))

    return pl.pallas_call(
        body,
        name="conv_bwd",
        grid=(nt,),
        in_specs=[col(0), col(1), col(2), col(0), nxt(0), nxt(1), nxt(2), nxt(0), prev(1), prev(2),
                  _full((8, D_CONV)), _full((1, D_CONV))] + _hbm_specs(len(deps)),
        out_specs=[pl.BlockSpec((tt, HALF_P), lambda i: (i, 0)), _full((8, D_CONV)), _full((1, D_CONV))],
        out_shape=[jax.ShapeDtypeStruct((T, P_INT), BF16), jax.ShapeDtypeStruct((8, D_CONV), F32),
                   jax.ShapeDtypeStruct((1, D_CONV), F32)],
        compiler_params=_params(1),
    )(proj, proj, proj, dyin, proj, proj, proj, dyin, proj, proj, conv_w8, conv_g, *deps)


def _gla_bwd(proj, wg128, gbias, gng, o_all, states, dyin, dproj):
    T = proj.shape[0]
    nst = T // STEP_ROWS

    def body(p_ref, wg_ref, gb_ref, gn_ref, o_ref, st_ref, d_ref, dp_in_ref,
             dp_ref, dwg_ref, dgb_ref, dgn_ref, ds_ref):
        n = pl.program_id(0)

        @pl.when(n == 0)
        def _():
            ds_ref[...] = jnp.zeros_like(ds_ref)
            dwg_ref[...] = jnp.zeros_like(dwg_ref)
            dgb_ref[...] = jnp.zeros_like(dgb_ref)
            dgn_ref[...] = jnp.zeros_like(dgn_ref)

        blk = p_ref[...]
        zl, z, bcum_all, same = _gla_step_terms(blk, wg_ref, gb_ref)
        causal = _causal()
        gn = gn_ref[...]
        ri = lax.broadcasted_iota(jnp.int32, (STEP_ROWS, STEP_ROWS), 0)
        ci = lax.broadcasted_iota(jnp.int32, (STEP_ROWS, STEP_ROWS), 1)
        upper = (same & (ri <= ci)).astype(F32)
        dstates = [ds_ref[h] for h in range(GLA_HEADS)]
        db_rows, dbl_rows, dgn_sum = [None] * STEP_CHUNKS, [None] * STEP_CHUNKS, [None] * GLA_HEADS
        for c in reversed(range(STEP_CHUNKS)):
            rows = slice(c * CHUNK, (c + 1) * CHUNK)
            q, k = blk[rows, 0:512], blk[rows, 512:1024]
            v, r = blk[rows, 1024:2048], blk[rows, 2048:3072]
            bcum = bcum_all[rows, :]
            db_parts, dbl_parts = [], []
            for h in range(GLA_HEADS):
                eb, enb, eend, dec, qd, ki, ke = _gla_head_terms(q, k, bcum, h)
                vs = slice(h * HEAD_V, (h + 1) * HEAD_V)
                ks = slice(h * HEAD_K, (h + 1) * HEAD_K)
                o = o_ref[rows, vs]
                rh = r[:, vs]
                dyg = d_ref[rows, vs]
                rinv = lax.rsqrt(jnp.mean(o * o, axis=-1, keepdims=True) + RMS_EPS)
                sg = _sigmoid(rh)
                on = o * rinv
                dr = dyg * (on * gn[:, vs]) * (sg * (1.0 + rh * (1.0 - sg)))
                don = dyg * (rh * sg)
                part = jnp.sum(don * on, axis=0, keepdims=True)
                dgn_sum[h] = part if dgn_sum[h] is None else dgn_sum[h] + part
                t = don * gn[:, vs]
                do = rinv * t - o * (rinv * rinv * rinv) * jnp.mean(o * t, axis=-1, keepdims=True)
                dob = do.astype(BF16)
                vb = v[:, vs].astype(BF16)
                qdb, kib, keb = qd.astype(BF16), ki.astype(BF16), ke.astype(BF16)
                a = jnp.where(causal, _dot(qdb, kib, NT), 0.0)
                st = st_ref[c, h]
                dst = dstates[h]
                dstb = dst.astype(BF16)
                da = jnp.where(causal, _dot(dob, vb, NT), 0.0)
                dab = da.astype(BF16)
                dv = _dot(a.astype(BF16), dob, TN) + _dot(keb, dstb, NT)
                dqd = _dot(dab, kib, NN) + _dot(dob, st.astype(BF16), NN)
                dki = _dot(dab, qdb, TN)
                dke = _dot(vb, dstb, NN)
                ddec = jnp.sum(st * dst, axis=0, keepdims=True)
                dstates[h] = dec * dst + _dot(dob, qdb, TN)
                dq = dqd * eb * (HEAD_K ** -0.5)
                dk = dki * enb + dke * eend
                db_parts.append(dqd * qd - dki * ki - dke * ke)
                dbl_parts.append(jnp.sum(dke * ke, axis=0, keepdims=True) + dec * ddec)
                dp_ref[rows, ks] = dq.astype(BF16)
                dp_ref[rows, D_GLA_K + h * HEAD_K:D_GLA_K + (h + 1) * HEAD_K] = dk.astype(BF16)
                dp_ref[rows, 1024 + h * HEAD_V:1024 + (h + 1) * HEAD_V] = dv.astype(BF16)
                dp_ref[rows, 2048 + h * HEAD_V:2048 + (h + 1) * HEAD_V] = dr.astype(BF16)
            db_rows[c] = jnp.concatenate(db_parts, axis=1)
            dbl_rows[c] = jnp.broadcast_to(jnp.concatenate(dbl_parts, axis=1), (CHUNK, D_GLA_K))
        for h in range(GLA_HEADS):
            ds_ref[h] = dstates[h]
            dgn_ref[:, h * HEAD_V:(h + 1) * HEAD_V] += dgn_sum[h]
        db = jnp.concatenate(db_rows, axis=0)
        dlog = _dot(upper, db, NN, precision=lax.Precision.HIGHEST) + jnp.concatenate(dbl_rows, axis=0)
        dz = dlog * (1.0 / GATE_TAU) * (1.0 / (1.0 + jnp.exp(z)))
        dzb = dz.astype(BF16)
        dp_ref[:, 3072:3200] = _dot(dzb, wg_ref[...], NT).astype(BF16)
        dwg_ref[...] += _dot(zl.astype(BF16), dzb, TN)
        dgb_ref[...] += jnp.sum(dz, axis=0, keepdims=True)

    rev = lambda n: nst - 1 - n
    return pl.pallas_call(
        body,
        name="gla_bwd",
        grid=(nst,),
        in_specs=[pl.BlockSpec((STEP_ROWS, HALF_P), lambda n: (rev(n), 1)), _full((LANE, D_GLA_K)),
                  _full((1, D_GLA_K)), _full((1, D_GLA_V)),
                  pl.BlockSpec((STEP_ROWS, D_GLA_V), lambda n: (rev(n), 0)),
                  pl.BlockSpec((STEP_CHUNKS, GLA_HEADS, HEAD_V, HEAD_K), lambda n: (rev(n), 0, 0, 0)),
                  pl.BlockSpec((STEP_ROWS, D_GLA_V), lambda n: (rev(n), 1)), pl.BlockSpec(memory_space=pl.ANY)],
        out_specs=[pl.BlockSpec((STEP_ROWS, HALF_P), lambda n: (rev(n), 1)), _full((LANE, D_GLA_K)),
                   _full((1, D_GLA_K)), _full((1, D_GLA_V))],
        out_shape=[jax.ShapeDtypeStruct(dproj.shape, BF16), jax.ShapeDtypeStruct((LANE, D_GLA_K), F32),
                   jax.ShapeDtypeStruct((1, D_GLA_K), F32), jax.ShapeDtypeStruct((1, D_GLA_V), F32)],
        scratch_shapes=[pltpu.VMEM((GLA_HEADS, HEAD_V, HEAD_K), F32)],
        input_output_aliases={7: 0},
        compiler_params=_params(1),
    )(proj, wg128, gbias, gng, o_all, states, dyin, dproj)


def _proj_bwd_x(dproj, w_full, dh1, deps=()):
    T, P = dproj.shape
    D = w_full.shape[0]
    tm, tk = _tile(T, 512), _tile(P, 1280)

    def ep(acc_ref, ex, o, i, j):
        o[0][...] = DN_ALPHA * ex[0][...] + acc_ref[...]

    row = pl.BlockSpec((tm, D), lambda i, j, k: (i, 0))
    return _mm("proj_bwd_x", "nt", dproj, w_full, M=T, N=D, K=P, tm=tm, tn=D, tk=tk,
               outs=[(jax.ShapeDtypeStruct((T, D), F32), row)], extras=[(dh1, row)], epilogue=ep, deps=deps)[0]


def _place():
    x, y, c = lax.axis_index("x"), lax.axis_index("y"), lax.axis_index("c")
    chips = [(1 - x, y), (x, 1 - y), (1 - x, 1 - y)]
    return x, y, c, chips


def _rcopy(src, dst, ssem, rsem, dev):
    return pltpu.make_async_remote_copy(src_ref=src, dst_ref=dst, send_sem=ssem, recv_sem=rsem,
                                        device_id=dev, device_id_type=MESH)


def _all_gather(name, shards, deps=()):
    n = len(shards)

    def body(*refs):
        ins, outs = refs[:n], refs[n + len(deps):2 * n + len(deps)]
        ssem, rsem, lsem = refs[2 * n + len(deps):]
        x, y, c, chips = _place()
        me, sib = (x, y, c), (x, y, 1 - c)

        def slot(w, px, py, pc):
            return outs[w].at[4 * px + 2 * py + pc]

        started = []
        for w in range(n):
            lc = pltpu.make_async_copy(ins[w], slot(w, *me), lsem.at[w])
            lc.start()
            started.append(lc)
        sends = []
        for w in range(n):
            cp = _rcopy(ins[w], slot(w, *me), ssem.at[7 * w], rsem.at[7 * w], sib)
            cp.start()
            sends.append(cp)
            for jx, chip in enumerate(chips):
                cp = _rcopy(ins[w], slot(w, *me), ssem.at[7 * w + 1 + jx], rsem.at[7 * w + 1 + jx], (*chip, c))
                cp.start()
                sends.append(cp)
        for w in range(n):
            for jx, chip in enumerate(chips):
                blk = slot(w, *chip, c)
                _rcopy(blk, blk, ssem.at[7 * w + 1 + jx], rsem.at[7 * w + 1 + jx], me).wait_recv()
                cp = _rcopy(blk, blk, ssem.at[7 * w + 4 + jx], rsem.at[7 * w + 4 + jx], sib)
                cp.start()
                sends.append(cp)
        for w in range(n):
            blk = slot(w, x, y, 1 - c)
            _rcopy(blk, blk, ssem.at[7 * w], rsem.at[7 * w], me).wait_recv()
            for jx, chip in enumerate(chips):
                blk = slot(w, *chip, 1 - c)
                _rcopy(blk, blk, ssem.at[7 * w + 4 + jx], rsem.at[7 * w + 4 + jx], me).wait_recv()
        for cp in sends:
            cp.wait_send()
        for lc in started:
            lc.wait()

    return pl.pallas_call(
        body,
        name=name,
        in_specs=_hbm_specs(n + len(deps)),
        out_specs=_hbm_specs(n),
        out_shape=[jax.ShapeDtypeStruct((N_DEV,) + s.shape, s.dtype) for s in shards],
        scratch_shapes=[pltpu.SemaphoreType.DMA((7 * n,)), pltpu.SemaphoreType.DMA((7 * n,)),
                        pltpu.SemaphoreType.DMA((n,))],
    )(*shards, *deps)


HBM_SPEC = pl.BlockSpec(memory_space=pltpu.HBM)
SEM_SPEC = pl.BlockSpec(memory_space=pltpu.SEMAPHORE)
SIDE_EFFECT = pltpu.SideEffectType.DATAFLOW_SIDE_EFFECTING


def _cast_place(name, ids, w, deps=(), dtype=None):
    dtype = BF16 if dtype is None else dtype
    R, C = w.shape
    tr = _tile(R, 256)

    def body(ids_ref, w_ref, *rest):
        rest[len(deps)][...] = w_ref[...].astype(dtype)

    return pl.pallas_call(
        body,
        name=name,
        grid_spec=pltpu.PrefetchScalarGridSpec(
            num_scalar_prefetch=1,
            grid=(R // tr,),
            in_specs=[pl.BlockSpec((tr, C), lambda r, ids: (r, 0))] + _hbm_specs(len(deps)),
            out_specs=pl.BlockSpec((None, tr, C), lambda r, ids: (ids[0], r, 0)),
        ),
        out_shape=jax.ShapeDtypeStruct((N_DEV, R, C), dtype),
        compiler_params=_params(1),
    )(ids, w, *deps)


def _xfer_start(name, bufs, plan, n):
    nb = len(bufs)

    def body(*refs):
        ins = refs[:nb]
        ssem, rsem = refs[nb], refs[nb + 1]
        token = refs[2 * nb + 2]
        x, y, c, chips = _place()
        for k, (src, dst, dev, _) in enumerate(plan(ins, x, y, c, chips)):
            _rcopy(src, dst, ssem.at[k], rsem.at[k], dev).start()
        token[...] = jnp.zeros_like(token)

    res = pl.pallas_call(
        body,
        name=name,
        out_shape=(pltpu.SemaphoreType.DMA((n,)), pltpu.SemaphoreType.DMA((n,)),
                   *[pltpu.HBM(b.shape, b.dtype) for b in bufs], jax.ShapeDtypeStruct((8, LANE), F32)),
        in_specs=[HBM_SPEC] * nb,
        out_specs=(SEM_SPEC, SEM_SPEC, *[HBM_SPEC] * nb, pl.BlockSpec(memory_space=pltpu.VMEM)),
        input_output_aliases={i: 2 + i for i in range(nb)},
        compiler_params=pltpu.CompilerParams(has_side_effects=SIDE_EFFECT),
    )(*[pltpu.with_memory_space_constraint(b, pltpu.HBM) for b in bufs])
    return dict(sems=res[:2], bufs=list(res[2:2 + nb]), token=res[2 + nb], plan=plan, n=n)


def _xfer_wait(name, started, after):
    bufs, plan = started["bufs"], started["plan"]
    nb = len(bufs)

    def body(*refs):
        ins = refs[:nb]
        ssem, rsem = refs[nb], refs[nb + 1]
        x, y, c, chips = _place()
        for k, (src, _, dev, land) in enumerate(plan(ins, x, y, c, chips)):
            cp = _rcopy(src, land, ssem.at[k], rsem.at[k], dev)
            cp.wait_send()
            cp.wait_recv()

    res = pl.pallas_call(
        body,
        name=name,
        out_shape=tuple(pltpu.HBM(b.shape, b.dtype) for b in bufs),
        in_specs=[HBM_SPEC] * nb + [SEM_SPEC, SEM_SPEC, pl.BlockSpec(memory_space=pl.ANY)],
        out_specs=tuple([HBM_SPEC] * nb),
        input_output_aliases={i: i for i in range(nb)},
        compiler_params=pltpu.CompilerParams(has_side_effects=SIDE_EFFECT),
    )(*bufs, *started["sems"], after)
    return list(res)


def _plan_gather_chips(refs, x, y, c, chips):
    plan = []
    for land in refs:
        mine = land.at[4 * x + 2 * y + c]
        plan.append((mine, mine, (x, y, 1 - c), land.at[4 * x + 2 * y + (1 - c)]))
        for px, py in chips:
            plan.append((mine, mine, (px, py, c), land.at[4 * px + 2 * py + c]))
    return plan


def _plan_gather_pass(refs, x, y, c, chips):
    return [(land.at[4 * px + 2 * py + c], land.at[4 * px + 2 * py + c], (x, y, 1 - c),
             land.at[4 * px + 2 * py + (1 - c)]) for land in refs for px, py in chips]


def _plan_reduce_core(refs, x, y, c, chips):
    grad, recv = refs
    return [(grad.at[2 * q + (1 - c)], recv.at[q], (x, y, 1 - c), recv.at[q]) for q in range(N_CHIP)]


def _plan_reduce_chips(refs, x, y, c, chips):
    part, land = refs
    return [(part.at[2 * px + py], land.at[2 * x + y], (px, py, c), land.at[2 * px + py]) for px, py in chips]


def _chip_sums(name, ids, grad, recv):
    _, R, C = grad.shape
    tr = _tile(R, 128)

    def body(ids_ref, g_ref, r_ref, o_ref):
        o_ref[...] = (g_ref[...] + r_ref[...]).astype(BF16)

    return pl.pallas_call(
        body,
        name=name,
        grid_spec=pltpu.PrefetchScalarGridSpec(
            num_scalar_prefetch=1,
            grid=(N_CHIP - 1, R // tr),
            in_specs=[pl.BlockSpec((None, tr, C), lambda q, r, ids: (2 * ids[3 + q] + ids[2], r, 0)),
                      pl.BlockSpec((None, tr, C), lambda q, r, ids: (ids[3 + q], r, 0))],
            out_specs=pl.BlockSpec((None, tr, C), lambda q, r, ids: (ids[3 + q], r, 0)),
        ),
        out_shape=jax.ShapeDtypeStruct((N_CHIP, R, C), BF16),
        compiler_params=_params(2),
    )(ids, grad, recv)


def _adamw(w, g, m, v):
    m = ADAM_B1 * m + (1.0 - ADAM_B1) * g
    v = ADAM_B2 * v + (1.0 - ADAM_B2) * (g * g)
    m_hat = m / (1.0 - ADAM_B1 ** ADAM_STEP)
    v_hat = v / (1.0 - ADAM_B2 ** ADAM_STEP)
    delta = -ADAM_LR * (m_hat / (jnp.sqrt(v_hat) + ADAM_EPS) + ADAM_WD * w)
    return delta, m, v


def _reduce_adamw(name, ids, grad, recv, landed, w, m, v):
    _, R, C = grad.shape
    tr = _tile(R, 128)

    def body(ids_ref, g_ref, r_ref, l1_ref, l2_ref, l3_ref, w_ref, m_ref, v_ref, go_ref, do_ref, mo_ref, vo_ref):
        g = g_ref[...] + r_ref[...]
        g = g + l1_ref[...].astype(F32)
        g = g + l2_ref[...].astype(F32)
        g = g + l3_ref[...].astype(F32)
        delta, mn, vn = _adamw(w_ref[...], g, m_ref[...], v_ref[...])
        go_ref[...] = g
        do_ref[...] = delta
        mo_ref[...] = mn
        vo_ref[...] = vn

    def pick(k):
        return pl.BlockSpec((None, tr, C), lambda r, ids: (ids[k], r, 0))

    flat = pl.BlockSpec((tr, C), lambda r, ids: (r, 0))
    shp = jax.ShapeDtypeStruct((R, C), F32)
    return pl.pallas_call(
        body,
        name=name,
        grid_spec=pltpu.PrefetchScalarGridSpec(
            num_scalar_prefetch=1,
            grid=(R // tr,),
            in_specs=[pick(0), pick(1), pick(3), pick(4), pick(5), flat, flat, flat],
            out_specs=[flat, flat, flat, flat],
        ),
        out_shape=[shp, shp, shp, shp],
        compiler_params=_params(1),
    )(ids, grad, recv, landed, landed, landed, w, m, v)


SMALL = ("conv_w", "conv_norm_g", "w_gate_up", "gate_bias", "gla_norm_g", "ln1_g", "ln1_b", "ln2_g", "ln2_b")
R_LOSS = 14


def _small_rows(D, conv_cols, gate_cols):
    nv = max(1, D // SP_COLS)
    assert nv <= 2, D
    return {"conv_w": (0, 3, conv_cols), "conv_norm_g": (3, 1, D_CONV), "gate_bias": (4, 1, D_GLA_K),
            "gla_norm_g": (5, 1, D_GLA_V), "ln1_g": (6, nv, D), "ln1_b": (8, nv, D), "ln2_g": (10, nv, D),
            "ln2_b": (12, nv, D), "w_gate_up": (16, GATE_RANK, gate_cols)}


def _put(o_ref, entry, val):
    row, n_rows, cols = entry
    if val.shape[0] == 1 and n_rows > 1:
        for r in range(n_rows):
            o_ref[row + r:row + r + 1, :] = val[:, r * SP_COLS:(r + 1) * SP_COLS]
    else:
        o_ref[row:row + n_rows, 0:cols] = val[0:n_rows, 0:cols]


def _take(g, entry):
    row, n_rows, cols = entry
    if cols > SP_COLS:
        return jnp.concatenate([g[row + r:row + r + 1, :] for r in range(n_rows)], axis=1)
    return g[row:row + n_rows, 0:cols]


def _make_pack(name, rows, pieces):
    names = list(pieces)

    def body(*refs):
        o_ref = refs[len(names)]
        o_ref[...] = jnp.zeros_like(o_ref)
        for nm, ref in zip(names, refs):
            if nm == "loss":
                o_ref[R_LOSS:R_LOSS + 1, 0:1] = jnp.sum(ref[...], axis=1, keepdims=True)
            else:
                _put(o_ref, rows[nm], ref[...])

    arrs = [pieces[nm] for nm in names]
    return pl.pallas_call(
        body,
        name=name,
        grid=(1,),
        in_specs=[_full(a.shape) for a in arrs],
        out_specs=_full((SP_ROWS, SP_COLS)),
        out_shape=jax.ShapeDtypeStruct((SP_ROWS, SP_COLS), F32),
        compiler_params=_params(1),
    )(*arrs)


def _small_adamw(packs, rows, w, m, v):
    names = list(SMALL)
    n = len(names)

    def body(p_ref, *refs):
        ins, outs = refs[:3 * n], refs[3 * n:]
        g = p_ref[0]
        for dvc in range(1, N_DEV):
            g = g + p_ref[dvc]
        for i, nm in enumerate(names):
            gp = _take(g, rows[nm])
            delta, mn, vn = _adamw(ins[i][...], gp, ins[n + i][...], ins[2 * n + i][...])
            for kind, val in enumerate((gp, delta, mn, vn)):
                outs[kind * n + i][...] = val
        outs[4 * n][...] = g[R_LOSS:R_LOSS + 1, 0:1]

    arrs = [w[nm] for nm in names] + [m[nm] for nm in names] + [v[nm] for nm in names]
    shapes = [jax.ShapeDtypeStruct(w[nm].shape, F32) for nm in names] * 4 + [jax.ShapeDtypeStruct((1, 1), F32)]
    res = pl.pallas_call(
        body,
        name="small_adamw",
        grid=(1,),
        in_specs=[_full(packs.shape)] + [_full(a.shape) for a in arrs],
        out_specs=[_full(sh.shape) for sh in shapes],
        out_shape=shapes,
        compiler_params=_params(1),
    )(packs, *arrs)
    return [dict(zip(names, res[k * n:(k + 1) * n])) for k in range(4)], res[4 * n]


def _w_in_pieces():
    cs = D_IN_PROJ // N_DEV
    pieces = []
    for d in range(N_DEV):
        lo, hi = d * cs, (d + 1) * cs
        if hi <= CONV_COLS:
            pieces.append((d, 0, cs, lo))
        elif lo >= CONV_COLS:
            pieces.append((d, 0, cs, lo - CONV_COLS + HALF_P))
        else:
            pieces.append((d, 0, CONV_COLS - lo, lo))
            pieces.append((d, CONV_COLS - lo, cs, HALF_P))
    return pieces


def _w_in_full(gathered):
    nb, D, cs = gathered.shape
    tr = _tile(D, 256)

    def body(g_ref, o_ref):
        o_ref[:, CONV_COLS:HALF_P] = jnp.zeros((tr, HALF_P - CONV_COLS), o_ref.dtype)
        o_ref[:, HALF_P + GLA_COLS:P_INT] = jnp.zeros((tr, HALF_P - GLA_COLS), o_ref.dtype)
        for d, a, b, dst in _w_in_pieces():
            o_ref[:, dst:dst + (b - a)] = g_ref[d, :, a:b]

    return pl.pallas_call(
        body,
        name="w_in_full",
        grid=(D // tr,),
        in_specs=[pl.BlockSpec((nb, tr, cs), lambda r: (0, r, 0))],
        out_specs=pl.BlockSpec((tr, P_INT), lambda r: (r, 0)),
        out_shape=jax.ShapeDtypeStruct((D, P_INT), gathered.dtype),
        compiler_params=_params(1),
    )(gathered)


def _w_in_blocks(dw):
    D = dw.shape[0]
    cs = D_IN_PROJ // N_DEV
    tr = _tile(D, 256)

    def body(w_ref, o_ref):
        for d, a, b, src in _w_in_pieces():
            o_ref[d, :, a:b] = w_ref[:, src:src + (b - a)]

    return pl.pallas_call(
        body,
        name="w_in_blocks",
        grid=(D // tr,),
        in_specs=[pl.BlockSpec((tr, P_INT), lambda r: (r, 0))],
        out_specs=pl.BlockSpec((N_DEV, tr, cs), lambda r: (0, r, 0)),
        out_shape=jax.ShapeDtypeStruct((N_DEV, D, cs), dw.dtype),
        compiler_params=_params(1),
    )(dw)


BIG = ("w_in", "w_out", "w_ff_up", "w_ff_down")
ORDER = ("w_in", "conv_w", "conv_norm_g", "w_gate_up", "gate_bias", "gla_norm_g", "w_out", "ln1_g", "ln1_b",
         "w_ff_up", "w_ff_down", "ln2_g", "ln2_b")


def kernel(x, w_in, conv_w, conv_norm_g, w_gate_up, gate_bias, gla_norm_g, w_out, ln1_g, ln1_b, w_ff_up, w_ff_down, ln2_g, ln2_b, loss_target, m_w_in, m_conv_w, m_conv_norm_g, m_w_gate_up, m_gate_bias, m_gla_norm_g, m_w_out, m_ln1_g, m_ln1_b, m_w_ff_up, m_w_ff_down, m_ln2_g, m_ln2_b, v_w_in, v_conv_w, v_conv_norm_g, v_w_gate_up, v_gate_bias, v_gla_norm_g, v_w_out, v_ln1_g, v_ln1_b, v_w_ff_up, v_w_ff_down, v_ln2_g, v_ln2_b):
    T, D = x.shape[1], x.shape[2]
    xs, target = x[0], loss_target[0]
    xi, yi, ci = lax.axis_index("x"), lax.axis_index("y"), lax.axis_index("c")
    chip = 2 * xi + yi
    dev = 2 * chip + ci
    others = [jnp.where(chip <= q, q + 1, q) for q in range(N_CHIP - 1)]
    ids = jnp.stack([dev, chip, ci] + others).astype(jnp.int32)
    conv_cols, gate_cols = conv_w.shape[2], w_gate_up.shape[2]

    def gather(nm, lands):
        return _xfer_start("gather_chips_" + nm, lands, _plan_gather_chips, 4 * len(lands))

    def pass_on(nm, started, after):
        lands = _xfer_wait("gather_chips_wait_" + nm, started, after)
        return _xfer_start("gather_pass_" + nm, lands, _plan_gather_pass, 3 * len(lands))

    def landed(nm, started, after):
        return _xfer_wait("gather_pass_wait_" + nm, started, after)

    rows = _small_rows(D, conv_cols, gate_cols)
    fwd_pack = _make_pack("pack_fwd", rows, {"conv_w": conv_w[0], "w_gate_up": w_gate_up[0]})
    ga_in = gather("w_in", [_cast_place("cast_place_w_in", ids, w_in[0]),
                            _cast_place("cast_place_pack", ids, fwd_pack, dtype=F32)])
    ga, dep = [], ga_in["token"]
    m_in, v_in = m_w_in[0], v_w_in[0]
    for nm, w in zip(BIG[1:], (w_out, w_ff_up, w_ff_down)):
        deps = [dep, m_in, v_in] if nm == "w_ff_down" else [dep]
        ga.append(gather(nm, [_cast_place("cast_place_" + nm, ids, w[0], deps)]))
        dep = ga[-1]["token"]
    xb = _cast_bf16(xs, [dep])
    gp_in = pass_on("w_in", ga_in, xb)
    g_in, g_pack = landed("w_in", gp_in, gp_in["token"])
    w_full = _w_in_full(g_in)
    r_cw, r_gw = rows["conv_w"][0], rows["w_gate_up"][0]
    conv_w_full = g_pack[:, r_cw:r_cw + 3, :conv_cols].transpose(1, 0, 2).reshape(3, -1)
    gate_w_full = g_pack[:, r_gw:r_gw + GATE_RANK, :gate_cols].transpose(1, 0, 2).reshape(GATE_RANK, -1)
    conv_w8 = jnp.pad(conv_w_full, ((0, 5), (0, 0)))
    wg128 = jnp.pad(gate_w_full, ((0, LANE - GATE_RANK), (0, 0))).astype(BF16)
    proj = _proj_fwd(xb, w_full)
    yin = _conv_fwd(proj, conv_w8, conv_norm_g)
    gp_out = pass_on("w_out", ga[0], yin)
    o_all, states, yin = _gla_fwd(proj, wg128, gate_bias, gla_norm_g, yin, deps=[gp_out["token"]])
    w_out_full = landed("w_out", gp_out, o_all)[0].reshape(-1, D)
    gp_up = pass_on("w_ff_up", ga[1], o_all)
    xhat1, x1, rstd1 = _mix_ln1(yin, w_out_full, xs, ln1_g, ln1_b, deps=[gp_up["token"]])
    (w_up_blk,) = landed("w_ff_up", gp_up, x1)
    half = N_DEV // 2
    ra, h2 = _ff_up(x1, w_up_blk, 0, half)
    gp_down = pass_on("w_ff_down", ga[2], ra)
    ra, h2 = _ff_up(x1, w_up_blk, half, N_DEV - half, prev=(ra, h2), deps=[gp_down["token"]])
    w_down_full = landed("w_ff_down", gp_down, ra)[0].reshape(-1, D)
    dh3, dh3b, g_ln2_g, g_ln2_b, loss = _ff_down_loss(h2, w_down_full, xhat1, target, ln1_g, ln1_b, ln2_g, ln2_b)

    def to_core(nm, grad):
        recv = lax.empty((N_CHIP,) + grad.shape[1:], F32)
        return _xfer_start("reduce_core_" + nm, [grad, recv], _plan_reduce_core, N_CHIP)

    def to_chips(nm, started, after):
        grad, recv = _xfer_wait("reduce_core_wait_" + nm, started, after)
        part = _chip_sums("chip_sums_" + nm, ids, grad, recv)
        land = lax.empty(part.shape, BF16)
        return grad, recv, _xfer_start("reduce_chips_" + nm, [part, land], _plan_reduce_chips, N_CHIP - 1)

    da = _ff_down_bwd_act(dh3b, w_down_full, ra)
    gw_down = _grad_w("grad_w_down", h2, dh3b).reshape(N_DEV, -1, D)
    rc_down = to_core("w_ff_down", gw_down)
    gw_up = _grad_w_up_blk(x1, da, N_DEV, deps=[rc_down["token"]])
    gw_down, rv_down, rs_down = to_chips("w_ff_down", rc_down, gw_up)
    rc_up = to_core("w_ff_up", gw_up)
    dh1, dh1b, g_ln1_g, g_ln1_b = _ff_up_bwd_ln1(da, w_up_blk, dh3, xhat1, rstd1, ln1_g,
                                                 deps=[rs_down["token"], rc_up["token"]])
    gw_up, rv_up, rs_up = to_chips("w_ff_up", rc_up, dh1b)
    dyin = _mix_bwd(dh1b, w_out_full, deps=[rs_up["token"]])
    gw_out = _grad_w("grad_w_out", yin, dh1b).reshape(N_DEV, -1, D)
    rc_out = to_core("w_out", gw_out)
    dproj, g_conv_w, g_conv_g = _conv_bwd(proj, dyin, conv_w8, conv_norm_g, deps=[rc_out["token"]])
    dproj, g_gate_w, g_gate_b, g_gla_g = _gla_bwd(proj, wg128, gate_bias, gla_norm_g, o_all, states, dyin, dproj)
    gw_out, rv_out, rs_out = to_chips("w_out", rc_out, dproj)
    gw_in = _w_in_blocks(_grad_w("grad_w_in", xb, dproj, tn_pref=1280, tk_pref=2048, deps=[rs_out["token"]]))
    rc_in = to_core("w_in", gw_in)

    big = {}

    def finish(nm, grad, recv, started, w, m, v, after):
        _, land = _xfer_wait("reduce_chips_wait_" + nm, started, after)
        res = _reduce_adamw("adamw_" + nm, ids, grad, recv, land, w[0], m[0], v[0])
        big[nm] = [a[None] for a in res]
        return res[0]

    full_rows = _small_rows(D, D_CONV, D_GLA_K)
    pack = _make_pack("pack_grads", full_rows, {
        "conv_w": g_conv_w, "conv_norm_g": g_conv_g, "gate_bias": g_gate_b, "gla_norm_g": g_gla_g, "ln1_g": g_ln1_g,
        "ln1_b": g_ln1_b, "ln2_g": g_ln2_g, "ln2_b": g_ln2_b, "loss": loss, "w_gate_up": g_gate_w})
    (packs,) = _all_gather("gather_small_grads", [pack], deps=[rc_in["token"]])
    done = finish("w_ff_down", gw_down, rv_down, rs_down, w_ff_down, m_w_ff_down, v_w_ff_down, packs)
    gw_in, rv_in, rs_in = to_chips("w_in", rc_in, done)
    done = finish("w_ff_up", gw_up, rv_up, rs_up, w_ff_up, m_w_ff_up, v_w_ff_up, rs_in["token"])
    done = finish("w_out", gw_out, rv_out, rs_out, w_out, m_w_out, v_w_out, done)
    grad_x = _proj_bwd_x(dproj, w_full, dh1, deps=[done])
    finish("w_in", gw_in, rv_in, rs_in, w_in, (m_in,), (v_in,), grad_x)

    def own_cols(row, n_rows, width):
        cut = lax.dynamic_slice(packs, (0, row, dev * width), (N_DEV, n_rows, width))
        return jnp.pad(cut, ((0, 0), (0, 0), (0, SP_COLS - width)))

    packs_own = jnp.concatenate([own_cols(r_cw, 3, conv_cols), packs[:, r_cw + 3:r_gw],
                                 own_cols(r_gw, GATE_RANK, gate_cols)], axis=1)
    as2d = lambda a: a[0] if a.ndim == 3 else a
    w_s = dict(zip(SMALL, map(as2d, (conv_w, conv_norm_g, w_gate_up, gate_bias, gla_norm_g, ln1_g, ln1_b, ln2_g, ln2_b))))
    m_s = dict(zip(SMALL, map(as2d, (m_conv_w, m_conv_norm_g, m_w_gate_up, m_gate_bias, m_gla_norm_g, m_ln1_g,
                                     m_ln1_b, m_ln2_g, m_ln2_b))))
    v_s = dict(zip(SMALL, map(as2d, (v_conv_w, v_conv_norm_g, v_w_gate_up, v_gate_bias, v_gla_norm_g, v_ln1_g,
                                     v_ln1_b, v_ln2_g, v_ln2_b))))
    small, loss_sum = _small_adamw(packs_own, rows, w_s, m_s, v_s)

    def leaf(kind, name):
        if name in BIG:
            return big[name][kind]
        a = small[kind][name]
        return a[None] if name in ("conv_w", "w_gate_up") else a

    out = [loss_sum[0, 0], grad_x[None]]
    for kind in range(4):
        out += [leaf(kind, nm) for nm in ORDER]
    return tuple(out)
```

```python
import jax
import jax.numpy as jnp
from jax import lax
from jax.experimental import pallas as pl
from jax.experimental.pallas import tpu as pltpu

F32 = jnp.float32
BF16 = jnp.bfloat16

D_CONV = 1024
CONV_GROUPS = 8
GLA_HEADS = 4
HEAD_K = 128
HEAD_V = 256
D_GLA_K = 512
D_GLA_V = 1024
GATE_RANK = 16
GATE_TAU = 16.0
CHUNK = 64
LN_EPS = 1e-5
RMS_EPS = 1e-6
DN_ALPHA = 2.0 ** 0.25
D_IN_PROJ = 6160
ADAM_LR = 0.001
ADAM_B1 = 0.9
ADAM_B2 = 0.999
ADAM_EPS = 1e-08
ADAM_WD = 0.01
ADAM_STEP = 10

N_DEV = 8
N_CHIP = 4
LANE = 128
HALF_P = 3200
P_INT = 2 * HALF_P
CONV_COLS = 3 * D_CONV
GLA_COLS = D_IN_PROJ - CONV_COLS
SP_ROWS = 32
SP_COLS = 1024
VMEM_LIMIT = 56 * 1024 * 1024

NN = ((1,), (0,))
NT = ((1,), (1,))
TN = ((0,), (0,))
MESH = pl.DeviceIdType.MESH


def _dot(a, b, dims, precision=None):
    return lax.dot_general(a, b, (dims, ((), ())), preferred_element_type=F32, precision=precision)


def _tile(n, pref):
    if n <= pref:
        return n
    t = (pref // LANE) * LANE
    while t > 0 and n % t:
        t -= LANE
    assert t > 0, (n, pref)
    return t


def _params(n_axes):
    return pltpu.CompilerParams(dimension_semantics=("arbitrary",) * n_axes, vmem_limit_bytes=VMEM_LIMIT)


def _full(shape):
    nd = len(shape)
    return pl.BlockSpec(shape, lambda *_: (0,) * nd)


def _hbm_specs(n):
    return [pl.BlockSpec(memory_space=pl.ANY)] * n


def _mm(name, mode, a, b, *, M, N, K, tm, tn, tk, outs, epilogue, extras=(), a_fn=None, a_spec=None, b_spec=None,
        deps=()):
    ni, nj, nk = M // tm, N // tn, K // tk
    assert ni * tm == M and nj * tn == N and nk * tk == K, (name, M, N, K, tm, tn, tk)
    if a_spec is None:
        a_spec = (pl.BlockSpec((tk, tm), lambda i, j, k: (k, i)) if mode == "tn"
                  else pl.BlockSpec((tm, tk), lambda i, j, k: (i, k)))
    if b_spec is None:
        b_spec = (pl.BlockSpec((tn, tk), lambda i, j, k: (j, k)) if mode == "nt"
                  else pl.BlockSpec((tk, tn), lambda i, j, k: (k, j)))
    dims = {"nn": NN, "nt": NT, "tn": TN}[mode]
    n_ex, n_out, n_dep = len(extras), len(outs), len(deps)

    def body(*refs):
        a_ref, b_ref = refs[0], refs[1]
        ex = refs[2:2 + n_ex]
        o = refs[2 + n_ex + n_dep:2 + n_ex + n_dep + n_out]
        acc_ref = refs[2 + n_ex + n_dep + n_out]
        i, j, k = pl.program_id(0), pl.program_id(1), pl.program_id(2)
        if nk > 1:
            @pl.when(k == 0)
            def _():
                acc_ref[...] = jnp.zeros_like(acc_ref)

        av = a_ref[...]
        if a_fn is not None:
            av = a_fn(av)
        part = _dot(av, b_ref[...], dims)
        if nk == 1 and epilogue is None:
            o[0][...] = part.astype(o[0].dtype)
        elif nk == 1:
            acc_ref[...] = part
            epilogue(acc_ref, ex, o, i, j)
        else:
            acc_ref[...] += part

            @pl.when(k == nk - 1)
            def _():
                if epilogue is None:
                    o[0][...] = acc_ref[...].astype(o[0].dtype)
                else:
                    epilogue(acc_ref, ex, o, i, j)

    return pl.pallas_call(
        body,
        name=name,
        grid=(ni, nj, nk),
        in_specs=[a_spec, b_spec] + [s for _, s in extras] + _hbm_specs(n_dep),
        out_specs=[s for _, s in outs],
        out_shape=[s for s, _ in outs],
        scratch_shapes=[pltpu.VMEM((8, LANE) if nk == 1 and epilogue is None else (tm, tn), F32)],
        compiler_params=_params(3),
    )(a, b, *[x for x, _ in extras], *deps)


def _mm_rows(name, mode, a, b, *, M, N, K, tm, tk, row_ins, vec_ins, row_outs, stat_outs, chunk_fn,
             b_spec=None, deps=()):
    ni, nk = M // tm, K // tk
    rc = tm // nk
    assert ni * tm == M and nk * tk == K and rc * nk == tm and rc % 16 == 0, (name, M, K, tm, tk)
    dims = {"nn": NN, "nt": NT}[mode]
    last = ni - 1
    n_split = 2 if N % (2 * 256) == 0 else 1

    def kk(i, k):
        return jnp.where(i < ni, k, nk - 1)

    a_spec = pl.BlockSpec((tm, tk), lambda i, k: (jnp.minimum(i, last), kk(i, k)))
    if b_spec is None:
        b_spec = (pl.BlockSpec((N, tk), lambda i, k: (0, kk(i, k))) if mode == "nt"
                  else pl.BlockSpec((tk, N), lambda i, k: (kk(i, k), 0)))
    prev_rows = lambda i, k: (jnp.maximum((i - 1) * nk + k, 0), 0)
    n_ri, n_vi, n_ro, n_so, n_dep = len(row_ins), len(vec_ins), len(row_outs), len(stat_outs), len(deps)

    def body(*refs):
        a_ref, b_ref = refs[0], refs[1]
        pos = 2
        ri = refs[pos:pos + n_ri]; pos += n_ri
        vi = refs[pos:pos + n_vi]; pos += n_vi + n_dep
        ro = refs[pos:pos + n_ro]; pos += n_ro
        so = refs[pos:pos + n_so]; pos += n_so
        accs = refs[pos:pos + 2]
        i, k = pl.program_id(0), pl.program_id(1)

        @pl.when((i == 0) & (k == 0))
        def _():
            accs[0][...] = jnp.zeros_like(accs[0])
            accs[1][...] = jnp.zeros_like(accs[1])
            for st in so:
                st[...] = jnp.zeros_like(st)

        def finish_rows(prev_ref):
            rows = pl.ds(pl.multiple_of(k * rc, rc), rc)
            done = prev_ref[rows, :]
            prev_ref[rows, :] = jnp.zeros((rc, N), F32)
            chunk_fn(done, i > 0, ri, vi, ro, so)

        def accumulate(acc_ref):
            av = a_ref[...]
            for c0 in range(0, N, N // n_split):
                cols = slice(c0, c0 + N // n_split)
                bv = b_ref[cols, :] if mode == "nt" else b_ref[:, cols]
                acc_ref[:, cols] += _dot(av, bv, dims)

        for parity in (0, 1):
            @pl.when((i < ni) & (lax.rem(i, 2) == parity))
            def _(parity=parity):
                finish_rows(accs[1 - parity])
                accumulate(accs[parity])

        @pl.when(i == ni)
        def _():
            finish_rows(accs[last % 2])

    row_spec = lambda arr: pl.BlockSpec((rc, arr.shape[1]), prev_rows)
    return pl.pallas_call(
        body,
        name=name,
        grid=(ni + 1, nk),
        in_specs=[a_spec, b_spec] + [row_spec(x) for x in row_ins] + [_full(x.shape) for x in vec_ins]
        + _hbm_specs(n_dep),
        out_specs=[row_spec(s) for s in row_outs] + [_full(s.shape) for s in stat_outs],
        out_shape=list(row_outs) + list(stat_outs),
        scratch_shapes=[pltpu.VMEM((tm, N), F32), pltpu.VMEM((tm, N), F32)],
        compiler_params=_params(2),
    )(a, b, *row_ins, *vec_ins, *deps)


SUB_ROWS = 16


def _by_sub_rows(n_rows, fn):
    sums = None
    for r0 in range(0, n_rows, SUB_ROWS):
        part = fn(slice(r0, r0 + SUB_ROWS))
        if part:
            sums = part if sums is None else tuple(x + y for x, y in zip(sums, part))
    return sums


def _to_bf16(v):
    return v.astype(BF16)


def _ln_bwd(dy, xhat, rstd, g):
    dxh = dy * g
    m1 = jnp.mean(dxh, axis=-1, keepdims=True)
    m2 = jnp.mean(dxh * xhat, axis=-1, keepdims=True)
    return rstd * (dxh - m1 - xhat * m2)


def _ln_fwd(h):
    mu = jnp.mean(h, axis=-1, keepdims=True)
    xc = h - mu
    var = jnp.mean(xc * xc, axis=-1, keepdims=True)
    rstd = lax.rsqrt(var + LN_EPS)
    return xc * rstd, rstd


def _proj_fwd(x, w_full, deps=()):
    T, D = x.shape
    P = w_full.shape[1]
    tm, tn = _tile(T, 1024), _tile(P, 1280)
    return _mm("proj_fwd", "nn", x, w_full, M=T, N=P, K=D, tm=tm, tn=tn, tk=D,
               outs=[(jax.ShapeDtypeStruct((T, P), F32), pl.BlockSpec((tm, tn), lambda i, j, k: (i, j)))],
               epilogue=None, deps=deps)[0]


def _cast_bf16(x, deps=()):
    T, D = x.shape
    tm = _tile(T, 512)

    def body(x_ref, *rest):
        rest[len(deps)][...] = x_ref[...].astype(BF16)

    return pl.pallas_call(
        body,
        name="cast_x",
        grid=(T // tm,),
        in_specs=[pl.BlockSpec((tm, D), lambda i: (i, 0))] + _hbm_specs(len(deps)),
        out_specs=pl.BlockSpec((tm, D), lambda i: (i, 0)),
        out_shape=jax.ShapeDtypeStruct((T, D), BF16),
        compiler_params=_params(1),
    )(x, *deps)


def _conv_shift(h, hp):
    row = lax.broadcasted_iota(jnp.int32, h.shape, 0)
    hm1 = hp[7:8, :]
    hm2 = hp[6:7, :]
    h1 = jnp.where(row == 0, hm1, pltpu.roll(h, 1, 0))
    h2 = jnp.where(row == 0, hm2, jnp.where(row == 1, hm1, pltpu.roll(h, 2, 0)))
    return h1, h2


def _conv_fwd(proj, conv_w8, conv_g):
    T = proj.shape[0]
    tt = _tile(T, 256)
    nt = T // tt
    t8 = tt // 8

    def body(b_ref, c_ref, u_ref, cp_ref, up_ref, w_ref, g_ref, yin_ref):
        i = pl.program_id(0)
        h = c_ref[...] * u_ref[...]
        hp = jnp.where(i > 0, cp_ref[...] * up_ref[...], 0.0)
        h1, h2 = _conv_shift(h, hp)
        w = w_ref[...]
        y = w[0:1, :] * h2 + w[1:2, :] * h1 + w[2:3, :] * h
        p = b_ref[...] * y
        parts = []
        for gi in range(CONV_GROUPS):
            pg = p[:, gi * LANE:(gi + 1) * LANE]
            r = lax.rsqrt(jnp.mean(pg * pg, axis=-1, keepdims=True) + RMS_EPS)
            parts.append(pg * r)
        yn = jnp.concatenate(parts, axis=1) * g_ref[...]
        yin_ref[...] = yn.astype(BF16)

    def col(cidx):
        return pl.BlockSpec((tt, D_CONV), lambda i: (i, cidx))

    def prev(cidx):
        return pl.BlockSpec((8, D_CONV), lambda i: (jnp.maximum(i * t8 - 1, 0), cidx))

    return pl.pallas_call(
        body,
        name="conv_fwd",
        grid=(nt,),
        in_specs=[col(0), col(1), col(2), prev(1), prev(2), _full((8, D_CONV)), _full((1, D_CONV))],
        out_specs=pl.BlockSpec((tt, D_CONV), lambda i: (i, 0)),
        out_shape=jax.ShapeDtypeStruct((T, 2 * D_CONV), BF16),
        compiler_params=_params(1),
    )(proj, proj, proj, proj, proj, conv_w8, conv_g)


def _log_sigmoid(z):
    return jnp.minimum(z, 0.0) - jnp.log(1.0 + jnp.exp(-jnp.abs(z)))


STEP_CHUNKS = 4
STEP_ROWS = STEP_CHUNKS * CHUNK


def _gla_step_terms(blk, wg_ref, gb_ref):
    zl = blk[:, 3072:3200]
    z = _dot(zl.astype(BF16), wg_ref[...], NN) + gb_ref[...]
    log_a = _log_sigmoid(z) * (1.0 / GATE_TAU)
    ri = lax.broadcasted_iota(jnp.int32, (STEP_ROWS, STEP_ROWS), 0)
    ci = lax.broadcasted_iota(jnp.int32, (STEP_ROWS, STEP_ROWS), 1)
    same = (ri // CHUNK) == (ci // CHUNK)
    lower = (same & (ri >= ci)).astype(F32)
    bcum = _dot(lower, log_a, NN, precision=lax.Precision.HIGHEST)
    return zl, z, bcum, same


def _causal():
    return (lax.broadcasted_iota(jnp.int32, (CHUNK, CHUNK), 0) >= lax.broadcasted_iota(jnp.int32, (CHUNK, CHUNK), 1))


def _gla_head_terms(q, k, bcum, h):
    sl = slice(h * HEAD_K, (h + 1) * HEAD_K)
    bh = bcum[:, sl]
    bl = bh[CHUNK - 1:CHUNK, :]
    eb = jnp.exp(bh)
    enb = jnp.exp(-bh)
    eend = jnp.exp(bl - bh)
    dec = jnp.exp(bl)
    qd = q[:, sl] * (HEAD_K ** -0.5) * eb
    ki = k[:, sl] * enb
    ke = k[:, sl] * eend
    return eb, enb, eend, dec, qd, ki, ke


def _sigmoid(x):
    return 1.0 / (1.0 + jnp.exp(-x))


def _gla_fwd(proj, wg128, gbias, gng, yin, deps=()):
    T = proj.shape[0]
    nch = T // CHUNK
    nst = T // STEP_ROWS

    def body(p_ref, wg_ref, gb_ref, gn_ref, yin_in_ref, *rest):
        o_ref, st_ref, yin_ref, s_ref = rest[len(deps):]
        n = pl.program_id(0)

        @pl.when(n == 0)
        def _():
            s_ref[...] = jnp.zeros_like(s_ref)

        blk = p_ref[...]
        _, _, bcum_all, _ = _gla_step_terms(blk, wg_ref, gb_ref)
        causal = _causal()
        gn = gn_ref[...]
        states = [s_ref[h] for h in range(GLA_HEADS)]
        for c in range(STEP_CHUNKS):
            rows = slice(c * CHUNK, (c + 1) * CHUNK)
            q, k = blk[rows, 0:512], blk[rows, 512:1024]
            v, r = blk[rows, 1024:2048], blk[rows, 2048:3072]
            bcum = bcum_all[rows, :]
            for h in range(GLA_HEADS):
                _, _, _, dec, qd, ki, ke = _gla_head_terms(q, k, bcum, h)
                vs = slice(h * HEAD_V, (h + 1) * HEAD_V)
                vb = v[:, vs].astype(BF16)
                qdb = qd.astype(BF16)
                a = jnp.where(causal, _dot(qdb, ki.astype(BF16), NT), 0.0)
                st = states[h]
                o = _dot(a.astype(BF16), vb, NN) + _dot(qdb, st.astype(BF16), NT)
                st_ref[c, h] = st
                states[h] = dec * st + _dot(vb, ke.astype(BF16), TN)
                o_ref[rows, vs] = o
                rinv = lax.rsqrt(jnp.mean(o * o, axis=-1, keepdims=True) + RMS_EPS)
                rh = r[:, vs]
                yin_ref[rows, vs] = (o * rinv * gn[:, vs] * (rh * _sigmoid(rh))).astype(BF16)
        for h in range(GLA_HEADS):
            s_ref[h] = states[h]

    return pl.pallas_call(
        body,
        name="gla_fwd",
        grid=(nst,),
        in_specs=[pl.BlockSpec((STEP_ROWS, HALF_P), lambda n: (n, 1)), _full((LANE, D_GLA_K)), _full((1, D_GLA_K)),
                  _full((1, D_GLA_V)), pl.BlockSpec(memory_space=pl.ANY)] + _hbm_specs(len(deps)),
        out_specs=[pl.BlockSpec((STEP_ROWS, D_GLA_V), lambda n: (n, 0)),
                   pl.BlockSpec((STEP_CHUNKS, GLA_HEADS, HEAD_V, HEAD_K), lambda n: (n, 0, 0, 0)),
                   pl.BlockSpec((STEP_ROWS, D_GLA_V), lambda n: (n, 1))],
        out_shape=[jax.ShapeDtypeStruct((T, D_GLA_V), F32),
                   jax.ShapeDtypeStruct((nch, GLA_HEADS, HEAD_V, HEAD_K), F32),
                   jax.ShapeDtypeStruct(yin.shape, BF16)],
        scratch_shapes=[pltpu.VMEM((GLA_HEADS, HEAD_V, HEAD_K), F32)],
        input_output_aliases={4: 2},
        compiler_params=_params(1),
    )(proj, wg128, gbias, gng, yin, *deps)


def _mix_ln1(yin, w_out, x, ln_g, ln_b, deps=()):
    T, D = x.shape
    KY = yin.shape[1]
    tm = _tile(T, 1024)

    def chunk(acc, valid, ri, vi, ro, so):
        g, b = vi[0][...], vi[1][...]

        def sub(rows):
            xhat, rstd = _ln_fwd(DN_ALPHA * ri[0][rows, :] + acc[rows, :])
            ro[0][rows, :] = xhat
            ro[1][rows, :] = (xhat * g + b).astype(BF16)
            ro[2][rows, :] = rstd

        _by_sub_rows(acc.shape[0], sub)

    return _mm_rows("mix_ln1", "nn", yin, w_out, M=T, N=D, K=KY, tm=tm, tk=_tile(KY, 512),
                    row_ins=[x], vec_ins=[ln_g, ln_b],
                    row_outs=[jax.ShapeDtypeStruct((T, D), F32), jax.ShapeDtypeStruct((T, D), BF16),
                              jax.ShapeDtypeStruct((T, 1), F32)],
                    stat_outs=[], chunk_fn=chunk, deps=deps)


def _ff_up(x1, w_up_blk, first, count, prev=None, deps=()):
    T, D = x1.shape
    nb, _, fb = w_up_blk.shape
    tm = _tile(T, 1024)
    ni = T // tm
    n_dep = len(deps) + (2 if prev is not None else 0)

    def body(a_ref, b_ref, *rest):
        ra_ref, h2_ref = rest[n_dep:n_dep + 2]
        ra = jnp.maximum(_dot(a_ref[...], b_ref[...], NN), 0.0)
        ra_ref[...] = ra.astype(BF16)
        h2_ref[...] = (ra * ra).astype(BF16)

    blk = pl.BlockSpec((tm, fb), lambda i, j: (i, first + j))
    shp = jax.ShapeDtypeStruct((T, nb * fb), BF16)
    keep = list(prev) if prev is not None else []
    return pl.pallas_call(
        body,
        name="ff_up_%d" % first,
        grid=(ni, count),
        in_specs=[pl.BlockSpec((tm, D), lambda i, j: (i, 0)),
                  pl.BlockSpec((None, D, fb), lambda i, j: (first + j, 0, 0))] + _hbm_specs(n_dep),
        out_specs=[blk, blk],
        out_shape=[shp, shp],
        input_output_aliases=({2: 0, 3: 1} if prev is not None else {}),
        compiler_params=_params(2),
    )(x1, w_up_blk, *keep, *deps)


def _ff_down_loss(h2, w_down, xhat1, target, g1, b1, g2, b2):
    T, F = h2.shape
    D = w_down.shape[1]
    tm = _tile(T, 1024)
    inv_d = 1.0 / D

    def chunk(acc, valid, ri, vi, ro, so):
        g1v, b1v, g2v, b2v = (v[...] for v in vi)

        def sub(rows):
            x1 = ri[0][rows, :] * g1v + b1v
            xhat, rstd = _ln_fwd(DN_ALPHA * x1 + acc[rows, :])
            e = xhat * g2v + b2v - ri[1][rows, :]
            dy = e * inv_d
            dh = _ln_bwd(dy, xhat, rstd, g2v)
            ro[0][rows, :] = dh
            ro[1][rows, :] = dh.astype(BF16)
            return (jnp.sum(dy * xhat, axis=0, keepdims=True), jnp.sum(dy, axis=0, keepdims=True),
                    jnp.sum(e * e, axis=0, keepdims=True))

        sg, sb, sl = _by_sub_rows(acc.shape[0], sub)
        so[0][...] += jnp.where(valid, sg, 0.0)
        so[1][...] += jnp.where(valid, sb, 0.0)
        so[2][...] += jnp.where(valid, sl * (0.5 * inv_d), 0.0)

    vshape = jax.ShapeDtypeStruct((1, D), F32)
    return _mm_rows("ff_down_loss", "nn", h2, w_down, M=T, N=D, K=F, tm=tm, tk=_tile(F, 1024),
                    row_ins=[xhat1, target], vec_ins=[g1, b1, g2, b2],
                    row_outs=[jax.ShapeDtypeStruct((T, D), F32), jax.ShapeDtypeStruct((T, D), BF16)],
                    stat_outs=[vshape, vshape, vshape], chunk_fn=chunk)


def _ff_down_bwd_act(dh3b, w_down, ra):
    T, D = dh3b.shape
    F = w_down.shape[0]
    tm, tn = _tile(T, 1024), _tile(F, 1024)

    def ep(acc_ref, ex, o, i, j):
        o[0][...] = (acc_ref[...] * (2.0 * ex[0][...].astype(F32))).astype(BF16)

    blk = pl.BlockSpec((tm, tn), lambda i, j, k: (i, j))
    return _mm("ff_down_bwd_act", "nt", dh3b, w_down, M=T, N=F, K=D, tm=tm, tn=tn, tk=D,
               outs=[(jax.ShapeDtypeStruct((T, F), BF16), blk)], extras=[(ra, blk)], epilogue=ep)[0]


def _grad_w(name, a, b, *, a_fn=None, tm_pref=1024, tn_pref=1024, tk_pref=4096, deps=()):
    T, M = a.shape
    N = b.shape[1]
    tm, tn, tk = _tile(M, tm_pref), _tile(N, tn_pref), _tile(T, tk_pref)
    return _mm(name, "tn", a, b, M=M, N=N, K=T, tm=tm, tn=tn, tk=tk, a_fn=a_fn, deps=deps,
               outs=[(jax.ShapeDtypeStruct((M, N), F32), pl.BlockSpec((tm, tn), lambda i, j, k: (i, j)))],
               epilogue=None)[0]


def _grad_w_up_blk(x1, da, nb, deps=()):
    T, D = x1.shape
    F = da.shape[1]
    fb = F // nb
    tm, tk = _tile(D, 1024), _tile(T, 4096)
    return _mm("grad_w_up", "tn", x1, da, M=D, N=F, K=T, tm=tm, tn=fb, tk=tk, deps=deps,
               outs=[(jax.ShapeDtypeStruct((nb, D, fb), F32),
                      pl.BlockSpec((None, tm, fb), lambda i, j, k: (j, i, 0)))],
               epilogue=None)[0]


def _ff_up_bwd_ln1(da, w_up_blk, dh3, xhat1, rstd1, g1, deps=()):
    T, F = da.shape
    nb, D, fb = w_up_blk.shape
    tm = _tile(T, 1024)

    def chunk(acc, valid, ri, vi, ro, so):
        g = vi[0][...]

        def sub(rows):
            dx1 = DN_ALPHA * ri[0][rows, :] + acc[rows, :]
            xhat = ri[1][rows, :]
            dh = _ln_bwd(dx1, xhat, ri[2][rows, :], g)
            ro[0][rows, :] = dh
            ro[1][rows, :] = dh.astype(BF16)
            return jnp.sum(dx1 * xhat, axis=0, keepdims=True), jnp.sum(dx1, axis=0, keepdims=True)

        sg, sb = _by_sub_rows(acc.shape[0], sub)
        so[0][...] += jnp.where(valid, sg, 0.0)
        so[1][...] += jnp.where(valid, sb, 0.0)

    nk = F // fb
    vshape = jax.ShapeDtypeStruct((1, D), F32)
    return _mm_rows("ff_up_bwd_ln1", "nt", da, w_up_blk, M=T, N=D, K=F, tm=tm, tk=fb,
                    b_spec=pl.BlockSpec((None, D, fb), lambda i, k: (jnp.where(i < T // tm, k, nk - 1), 0, 0)),
                    row_ins=[dh3, xhat1, rstd1], vec_ins=[g1],
                    row_outs=[jax.ShapeDtypeStruct((T, D), F32), jax.ShapeDtypeStruct((T, D), BF16)],
                    stat_outs=[vshape, vshape], chunk_fn=chunk, deps=deps)


def _mix_bwd(dh1b, w_out, deps=()):
    T, D = dh1b.shape
    KY = w_out.shape[0]
    tm, tn = _tile(T, 1024), _tile(KY, 1024)
    return _mm("mix_bwd", "nt", dh1b, w_out, M=T, N=KY, K=D, tm=tm, tn=tn, tk=D, deps=deps,
               outs=[(jax.ShapeDtypeStruct((T, KY), F32), pl.BlockSpec((tm, tn), lambda i, j, k: (i, j)))],
               epilogue=None)[0]


def _conv_bwd(proj, dyin, conv_w8, conv_g, deps=()):
    T = proj.shape[0]
    tt = _tile(T, 256)
    nt = T // tt
    t8 = tt // 8
    nx = tt + 8

    def body(b_ref, c_ref, u_ref, d_ref, bn_ref, cn_ref, un_ref, dn_ref, cp_ref, up_ref, w_ref, g_ref, *rest):
        dp_ref, dw_ref, dg_ref = rest[len(deps):]
        i = pl.program_id(0)

        @pl.when(i == 0)
        def _():
            dw_ref[...] = jnp.zeros_like(dw_ref)
            dg_ref[...] = jnp.zeros_like(dg_ref)

        more = i < nt - 1

        def ext(cur_ref, nxt_ref):
            return jnp.concatenate([cur_ref[...], jnp.where(more, nxt_ref[...], 0.0)], axis=0)

        bx, cx, ux, dx = ext(b_ref, bn_ref), ext(c_ref, cn_ref), ext(u_ref, un_ref), ext(d_ref, dn_ref)
        hx = cx * ux
        hp = jnp.where(i > 0, cp_ref[...] * up_ref[...], 0.0)
        h1, h2 = _conv_shift(hx, hp)
        w = w_ref[...]
        g = g_ref[...]
        yx = w[0:1, :] * h2 + w[1:2, :] * h1 + w[2:3, :] * hx
        px = bx * yx
        dps, dgs = [], []
        for gi in range(CONV_GROUPS):
            sl = slice(gi * LANE, (gi + 1) * LANE)
            pg, dg_ = px[:, sl], dx[:, sl]
            r = lax.rsqrt(jnp.mean(pg * pg, axis=-1, keepdims=True) + RMS_EPS)
            gd = g[:, sl] * dg_
            dps.append(r * gd - pg * (r * r * r) * jnp.mean(pg * gd, axis=-1, keepdims=True))
            dgs.append(jnp.sum((dg_ * pg * r)[:tt, :], axis=0, keepdims=True))
        dpx = jnp.concatenate(dps, axis=1)
        dg_ref[...] += jnp.concatenate(dgs, axis=1)
        dyx = dpx * bx
        dyc = dyx[:tt, :]
        dh = (w[2:3, :] * dyx + w[1:2, :] * pltpu.roll(dyx, nx - 1, 0) + w[0:1, :] * pltpu.roll(dyx, nx - 2, 0))[:tt, :]
        dw_ref[0:1, :] += jnp.sum(dyc * h2[:tt, :], axis=0, keepdims=True)
        dw_ref[1:2, :] += jnp.sum(dyc * h1[:tt, :], axis=0, keepdims=True)
        dw_ref[2:3, :] += jnp.sum(dyc * hx[:tt, :], axis=0, keepdims=True)
        dp_ref[:, 0:D_CONV] = (dpx * yx)[:tt, :].astype(BF16)
        dp_ref[:, D_CONV:2 * D_CONV] = (dh * u_ref[...]).astype(BF16)
        dp_ref[:, 2 * D_CONV:3 * D_CONV] = (dh * c_ref[...]).astype(BF16)
        dp_ref[:, 3 * D_CONV:HALF_P] = jnp.zeros((tt, HALF_P - 3 * D_CONV), BF16)

    def col(cidx):
        return pl.BlockSpec((tt, D_CONV), lambda i: (i, cidx))

    def nxt(cidx):
        return pl.BlockSpec((8, D_CONV), lambda i: (jnp.minimum((i + 1) * t8, T // 8 - 1), cidx))

    def prev(cidx):
        return pl.BlockSpec((8, D_CONV), lambda i: (jnp.maximum(i * t8 - 1, 0), cidx))

    return pl.pallas_call(
        body,
        name="conv_bwd",
        grid=(nt,),
        in_specs=[col(0), col(1), col(2), col(0), nxt(0), nxt(1), nxt(2), nxt(0), prev(1), prev(2),
                  _full((8, D_CONV)), _full((1, D_CONV))] + _hbm_specs(len(deps)),
        out_specs=[pl.BlockSpec((tt, HALF_P), lambda i: (i, 0)), _full((8, D_CONV)), _full((1, D_CONV))],
        out_shape=[jax.ShapeDtypeStruct((T, P_INT), BF16), jax.ShapeDtypeStruct((8, D_CONV), F32),
                   jax.ShapeDtypeStruct((1, D_CONV), F32)],
        compiler_params=_params(1),
    )(proj, proj, proj, dyin, proj, proj, proj, dyin, proj, proj, conv_w8, conv_g, *deps)


def _gla_bwd(proj, wg128, gbias, gng, o_all, states, dyin, dproj):
    T = proj.shape[0]
    nst = T // STEP_ROWS

    def body(p_ref, wg_ref, gb_ref, gn_ref, o_ref, st_ref, d_ref, dp_in_ref,
             dp_ref, dwg_ref, dgb_ref, dgn_ref, ds_ref):
        n = pl.program_id(0)

        @pl.when(n == 0)
        def _():
            ds_ref[...] = jnp.zeros_like(ds_ref)
            dwg_ref[...] = jnp.zeros_like(dwg_ref)
            dgb_ref[...] = jnp.zeros_like(dgb_ref)
            dgn_ref[...] = jnp.zeros_like(dgn_ref)

        blk = p_ref[...]
        zl, z, bcum_all, same = _gla_step_terms(blk, wg_ref, gb_ref)
        causal = _causal()
        gn = gn_ref[...]
        ri = lax.broadcasted_iota(jnp.int32, (STEP_ROWS, STEP_ROWS), 0)
        ci = lax.broadcasted_iota(jnp.int32, (STEP_ROWS, STEP_ROWS), 1)
        upper = (same & (ri <= ci)).astype(F32)
        dstates = [ds_ref[h] for h in range(GLA_HEADS)]
        db_rows, dbl_rows, dgn_sum = [None] * STEP_CHUNKS, [None] * STEP_CHUNKS, [None] * GLA_HEADS
        for c in reversed(range(STEP_CHUNKS)):
            rows = slice(c * CHUNK, (c + 1) * CHUNK)
            q, k = blk[rows, 0:512], blk[rows, 512:1024]
            v, r = blk[rows, 1024:2048], blk[rows, 2048:3072]
            bcum = bcum_all[rows, :]
            db_parts, dbl_parts = [], []
            for h in range(GLA_HEADS):
                eb, enb, eend, dec, qd, ki, ke = _gla_head_terms(q, k, bcum, h)
                vs = slice(h * HEAD_V, (h + 1) * HEAD_V)
                ks = slice(h * HEAD_K, (h + 1) * HEAD_K)
                o = o_ref[rows, vs]
                rh = r[:, vs]
                dyg = d_ref[rows, vs]
                rinv = lax.rsqrt(jnp.mean(o * o, axis=-1, keepdims=True) + RMS_EPS)
                sg = _sigmoid(rh)
                on = o * rinv
                dr = dyg * (on * gn[:, vs]) * (sg * (1.0 + rh * (1.0 - sg)))
                don = dyg * (rh * sg)
                part = jnp.sum(don * on, axis=0, keepdims=True)
                dgn_sum[h] = part if dgn_sum[h] is None else dgn_sum[h] + part
                t = don * gn[:, vs]
                do = rinv * t - o * (rinv * rinv * rinv) * jnp.mean(o * t, axis=-1, keepdims=True)
                dob = do.astype(BF16)
                vb = v[:, vs].astype(BF16)
                qdb, kib, keb = qd.astype(BF16), ki.astype(BF16), ke.astype(BF16)
                a = jnp.where(causal, _dot(qdb, kib, NT), 0.0)
                st = st_ref[c, h]
                dst = dstates[h]
                dstb = dst.astype(BF16)
                da = jnp.where(causal, _dot(dob, vb, NT), 0.0)
                dab = da.astype(BF16)
                dv = _dot(a.astype(BF16), dob, TN) + _dot(keb, dstb, NT)
                dqd = _dot(dab, kib, NN) + _dot(dob, st.astype(BF16), NN)
                dki = _dot(dab, qdb, TN)
                dke = _dot(vb, dstb, NN)
                ddec = jnp.sum(st * dst, axis=0, keepdims=True)
                dstates[h] = dec * dst + _dot(dob, qdb, TN)
                dq = dqd * eb * (HEAD_K ** -0.5)
                dk = dki * enb + dke * eend
                db_parts.append(dqd * qd - dki * ki - dke * ke)
                dbl_parts.append(jnp.sum(dke * ke, axis=0, keepdims=True) + dec * ddec)
                dp_ref[rows, ks] = dq.astype(BF16)
                dp_ref[rows, D_GLA_K + h * HEAD_K:D_GLA_K + (h + 1) * HEAD_K] = dk.astype(BF16)
                dp_ref[rows, 1024 + h * HEAD_V:1024 + (h + 1) * HEAD_V] = dv.astype(BF16)
                dp_ref[rows, 2048 + h * HEAD_V:2048 + (h + 1) * HEAD_V] = dr.astype(BF16)
            db_rows[c] = jnp.concatenate(db_parts, axis=1)
            dbl_rows[c] = jnp.broadcast_to(jnp.concatenate(dbl_parts, axis=1), (CHUNK, D_GLA_K))
        for h in range(GLA_HEADS):
            ds_ref[h] = dstates[h]
            dgn_ref[:, h * HEAD_V:(h + 1) * HEAD_V] += dgn_sum[h]
        db = jnp.concatenate(db_rows, axis=0)
        dlog = _dot(upper, db, NN, precision=lax.Precision.HIGHEST) + jnp.concatenate(dbl_rows, axis=0)
        dz = dlog * (1.0 / GATE_TAU) * (1.0 / (1.0 + jnp.exp(z)))
        dzb = dz.astype(BF16)
        dp_ref[:, 3072:3200] = _dot(dzb, wg_ref[...], NT).astype(BF16)
        dwg_ref[...] += _dot(zl.astype(BF16), dzb, TN)
        dgb_ref[...] += jnp.sum(dz, axis=0, keepdims=True)

    rev = lambda n: nst - 1 - n
    return pl.pallas_call(
        body,
        name="gla_bwd",
        grid=(nst,),
        in_specs=[pl.BlockSpec((STEP_ROWS, HALF_P), lambda n: (rev(n), 1)), _full((LANE, D_GLA_K)),
                  _full((1, D_GLA_K)), _full((1, D_GLA_V)),
                  pl.BlockSpec((STEP_ROWS, D_GLA_V), lambda n: (rev(n), 0)),
                  pl.BlockSpec((STEP_CHUNKS, GLA_HEADS, HEAD_V, HEAD_K), lambda n: (rev(n), 0, 0, 0)),
                  pl.BlockSpec((STEP_ROWS, D_GLA_V), lambda n: (rev(n), 1)), pl.BlockSpec(memory_space=pl.ANY)],
        out_specs=[pl.BlockSpec((STEP_ROWS, HALF_P), lambda n: (rev(n), 1)), _full((LANE, D_GLA_K)),
                   _full((1, D_GLA_K)), _full((1, D_GLA_V))],
        out_shape=[jax.ShapeDtypeStruct(dproj.shape, BF16), jax.ShapeDtypeStruct((LANE, D_GLA_K), F32),
                   jax.ShapeDtypeStruct((1, D_GLA_K), F32), jax.ShapeDtypeStruct((1, D_GLA_V), F32)],
        scratch_shapes=[pltpu.VMEM((GLA_HEADS, HEAD_V, HEAD_K), F32)],
        input_output_aliases={7: 0},
        compiler_params=_params(1),
    )(proj, wg128, gbias, gng, o_all, states, dyin, dproj)


def _proj_bwd_x(dproj, w_full, dh1, deps=()):
    T, P = dproj.shape
    D = w_full.shape[0]
    tm, tk = _tile(T, 512), _tile(P, 1280)

    def ep(acc_ref, ex, o, i, j):
        o[0][...] = DN_ALPHA * ex[0][...] + acc_ref[...]

    row = pl.BlockSpec((tm, D), lambda i, j, k: (i, 0))
    return _mm("proj_bwd_x", "nt", dproj, w_full, M=T, N=D, K=P, tm=tm, tn=D, tk=tk,
               outs=[(jax.ShapeDtypeStruct((T, D), F32), row)], extras=[(dh1, row)], epilogue=ep, deps=deps)[0]


def _place():
    x, y, c = lax.axis_index("x"), lax.axis_index("y"), lax.axis_index("c")
    chips = [(1 - x, y), (x, 1 - y), (1 - x, 1 - y)]
    return x, y, c, chips


def _rcopy(src, dst, ssem, rsem, dev):
    return pltpu.make_async_remote_copy(src_ref=src, dst_ref=dst, send_sem=ssem, recv_sem=rsem,
                                        device_id=dev, device_id_type=MESH)


def _all_gather(name, shards, deps=()):
    n = len(shards)

    def body(*refs):
        ins, outs = refs[:n], refs[n + len(deps):2 * n + len(deps)]
        ssem, rsem, lsem = refs[2 * n + len(deps):]
        x, y, c, chips = _place()
        me, sib = (x, y, c), (x, y, 1 - c)

        def slot(w, px, py, pc):
            return outs[w].at[4 * px + 2 * py + pc]

        started = []
        for w in range(n):
            lc = pltpu.make_async_copy(ins[w], slot(w, *me), lsem.at[w])
            lc.start()
            started.append(lc)
        sends = []
        for w in range(n):
            cp = _rcopy(ins[w], slot(w, *me), ssem.at[7 * w], rsem.at[7 * w], sib)
            cp.start()
            sends.append(cp)
            for jx, chip in enumerate(chips):
                cp = _rcopy(ins[w], slot(w, *me), ssem.at[7 * w + 1 + jx], rsem.at[7 * w + 1 + jx], (*chip, c))
                cp.start()
                sends.append(cp)
        for w in range(n):
            for jx, chip in enumerate(chips):
                blk = slot(w, *chip, c)
                _rcopy(blk, blk, ssem.at[7 * w + 1 + jx], rsem.at[7 * w + 1 + jx], me).wait_recv()
                cp = _rcopy(blk, blk, ssem.at[7 * w + 4 + jx], rsem.at[7 * w + 4 + jx], sib)
                cp.start()
                sends.append(cp)
        for w in range(n):
            blk = slot(w, x, y, 1 - c)
            _rcopy(blk, blk, ssem.at[7 * w], rsem.at[7 * w], me).wait_recv()
            for jx, chip in enumerate(chips):
                blk = slot(w, *chip, 1 - c)
                _rcopy(blk, blk, ssem.at[7 * w + 4 + jx], rsem.at[7 * w + 4 + jx], me).wait_recv()
        for cp in sends:
            cp.wait_send()
        for lc in started:
            lc.wait()

    return pl.pallas_call(
        body,
        name=name,
        in_specs=_hbm_specs(n + len(deps)),
        out_specs=_hbm_specs(n),
        out_shape=[jax.ShapeDtypeStruct((N_DEV,) + s.shape, s.dtype) for s in shards],
        scratch_shapes=[pltpu.SemaphoreType.DMA((7 * n,)), pltpu.SemaphoreType.DMA((7 * n,)),
                        pltpu.SemaphoreType.DMA((n,))],
    )(*shards, *deps)


HBM_SPEC = pl.BlockSpec(memory_space=pltpu.HBM)
SEM_SPEC = pl.BlockSpec(memory_space=pltpu.SEMAPHORE)
SIDE_EFFECT = pltpu.SideEffectType.DATAFLOW_SIDE_EFFECTING


def _cast_place(name, ids, w, deps=(), dtype=None):
    dtype = BF16 if dtype is None else dtype
    R, C = w.shape
    tr = _tile(R, 256)

    def body(ids_ref, w_ref, *rest):
        rest[len(deps)][...] = w_ref[...].astype(dtype)

    return pl.pallas_call(
        body,
        name=name,
        grid_spec=pltpu.PrefetchScalarGridSpec(
            num_scalar_prefetch=1,
            grid=(R // tr,),
            in_specs=[pl.BlockSpec((tr, C), lambda r, ids: (r, 0))] + _hbm_specs(len(deps)),
            out_specs=pl.BlockSpec((None, tr, C), lambda r, ids: (ids[0], r, 0)),
        ),
        out_shape=jax.ShapeDtypeStruct((N_DEV, R, C), dtype),
        compiler_params=_params(1),
    )(ids, w, *deps)


def _xfer_start(name, bufs, plan, n):
    nb = len(bufs)

    def body(*refs):
        ins = refs[:nb]
        ssem, rsem = refs[nb], refs[nb + 1]
        token = refs[2 * nb + 2]
        x, y, c, chips = _place()
        for k, (src, dst, dev, _) in enumerate(plan(ins, x, y, c, chips)):
            _rcopy(src, dst, ssem.at[k], rsem.at[k], dev).start()
        token[...] = jnp.zeros_like(token)

    res = pl.pallas_call(
        body,
        name=name,
        out_shape=(pltpu.SemaphoreType.DMA((n,)), pltpu.SemaphoreType.DMA((n,)),
                   *[pltpu.HBM(b.shape, b.dtype) for b in bufs], jax.ShapeDtypeStruct((8, LANE), F32)),
        in_specs=[HBM_SPEC] * nb,
        out_specs=(SEM_SPEC, SEM_SPEC, *[HBM_SPEC] * nb, pl.BlockSpec(memory_space=pltpu.VMEM)),
        input_output_aliases={i: 2 + i for i in range(nb)},
        compiler_params=pltpu.CompilerParams(has_side_effects=SIDE_EFFECT),
    )(*[pltpu.with_memory_space_constraint(b, pltpu.HBM) for b in bufs])
    return dict(sems=res[:2], bufs=list(res[2:2 + nb]), token=res[2 + nb], plan=plan, n=n)


def _xfer_wait(name, started, after):
    bufs, plan = started["bufs"], started["plan"]
    nb = len(bufs)

    def body(*refs):
        ins = refs[:nb]
        ssem, rsem = refs[nb], refs[nb + 1]
        x, y, c, chips = _place()
        for k, (src, _, dev, land) in enumerate(plan(ins, x, y, c, chips)):
            cp = _rcopy(src, land, ssem.at[k], rsem.at[k], dev)
            cp.wait_send()
            cp.wait_recv()

    res = pl.pallas_call(
        body,
        name=name,
        out_shape=tuple(pltpu.HBM(b.shape, b.dtype) for b in bufs),
        in_specs=[HBM_SPEC] * nb + [SEM_SPEC, SEM_SPEC, pl.BlockSpec(memory_space=pl.ANY)],
        out_specs=tuple([HBM_SPEC] * nb),
        input_output_aliases={i: i for i in range(nb)},
        compiler_params=pltpu.CompilerParams(has_side_effects=SIDE_EFFECT),
    )(*bufs, *started["sems"], after)
    return list(res)


def _plan_gather_chips(refs, x, y, c, chips):
    plan = []
    for land in refs:
        mine = land.at[4 * x + 2 * y + c]
        plan.append((mine, mine, (x, y, 1 - c), land.at[4 * x + 2 * y + (1 - c)]))
        for px, py in chips:
            plan.append((mine, mine, (px, py, c), land.at[4 * px + 2 * py + c]))
    return plan


def _plan_gather_pass(refs, x, y, c, chips):
    return [(land.at[4 * px + 2 * py + c], land.at[4 * px + 2 * py + c], (x, y, 1 - c),
             land.at[4 * px + 2 * py + (1 - c)]) for land in refs for px, py in chips]


def _plan_reduce_core(refs, x, y, c, chips):
    grad, recv = refs
    return [(grad.at[2 * q + (1 - c)], recv.at[q], (x, y, 1 - c), recv.at[q]) for q in range(N_CHIP)]


def _plan_reduce_chips(refs, x, y, c, chips):
    part, land = refs
    return [(part.at[2 * px + py], land.at[2 * x + y], (px, py, c), land.at[2 * px + py]) for px, py in chips]


def _chip_sums(name, ids, grad, recv):
    _, R, C = grad.shape
    tr = _tile(R, 256)

    def body(ids_ref, g_ref, r_ref, o_ref):
        o_ref[...] = (g_ref[...] + r_ref[...]).astype(BF16)

    return pl.pallas_call(
        body,
        name=name,
        grid_spec=pltpu.PrefetchScalarGridSpec(
            num_scalar_prefetch=1,
            grid=(N_CHIP - 1, R // tr),
            in_specs=[pl.BlockSpec((None, tr, C), lambda q, r, ids: (2 * ids[3 + q] + ids[2], r, 0)),
                      pl.BlockSpec((None, tr, C), lambda q, r, ids: (ids[3 + q], r, 0))],
            out_specs=pl.BlockSpec((None, tr, C), lambda q, r, ids: (ids[3 + q], r, 0)),
        ),
        out_shape=jax.ShapeDtypeStruct((N_CHIP, R, C), BF16),
        compiler_params=_params(2),
    )(ids, grad, recv)


def _adamw(w, g, m, v):
    m = ADAM_B1 * m + (1.0 - ADAM_B1) * g
    v = ADAM_B2 * v + (1.0 - ADAM_B2) * (g * g)
    m_hat = m / (1.0 - ADAM_B1 ** ADAM_STEP)
    v_hat = v / (1.0 - ADAM_B2 ** ADAM_STEP)
    delta = -ADAM_LR * (m_hat / (jnp.sqrt(v_hat) + ADAM_EPS) + ADAM_WD * w)
    return delta, m, v


def _reduce_adamw(name, ids, grad, recv, landed, w, m, v):
    _, R, C = grad.shape
    tr = _tile(R, 128)

    def body(ids_ref, g_ref, r_ref, l1_ref, l2_ref, l3_ref, w_ref, m_ref, v_ref, go_ref, do_ref, mo_ref, vo_ref):
        g = g_ref[...] + r_ref[...]
        g = g + l1_ref[...].astype(F32)
        g = g + l2_ref[...].astype(F32)
        g = g + l3_ref[...].astype(F32)
        delta, mn, vn = _adamw(w_ref[...], g, m_ref[...], v_ref[...])
        go_ref[...] = g
        do_ref[...] = delta
        mo_ref[...] = mn
        vo_ref[...] = vn

    def pick(k):
        return pl.BlockSpec((None, tr, C), lambda r, ids: (ids[k], r, 0))

    flat = pl.BlockSpec((tr, C), lambda r, ids: (r, 0))
    shp = jax.ShapeDtypeStruct((R, C), F32)
    return pl.pallas_call(
        body,
        name=name,
        grid_spec=pltpu.PrefetchScalarGridSpec(
            num_scalar_prefetch=1,
            grid=(R // tr,),
            in_specs=[pick(0), pick(1), pick(3), pick(4), pick(5), flat, flat, flat],
            out_specs=[flat, flat, flat, flat],
        ),
        out_shape=[shp, shp, shp, shp],
        compiler_params=_params(1),
    )(ids, grad, recv, landed, landed, landed, w, m, v)


SMALL = ("conv_w", "conv_norm_g", "w_gate_up", "gate_bias", "gla_norm_g", "ln1_g", "ln1_b", "ln2_g", "ln2_b")
R_LOSS = 14


def _small_rows(D, conv_cols, gate_cols):
    nv = max(1, D // SP_COLS)
    assert nv <= 2, D
    return {"conv_w": (0, 3, conv_cols), "conv_norm_g": (3, 1, D_CONV), "gate_bias": (4, 1, D_GLA_K),
            "gla_norm_g": (5, 1, D_GLA_V), "ln1_g": (6, nv, D), "ln1_b": (8, nv, D), "ln2_g": (10, nv, D),
            "ln2_b": (12, nv, D), "w_gate_up": (16, GATE_RANK, gate_cols)}


def _put(o_ref, entry, val):
    row, n_rows, cols = entry
    if val.shape[0] == 1 and n_rows > 1:
        for r in range(n_rows):
            o_ref[row + r:row + r + 1, :] = val[:, r * SP_COLS:(r + 1) * SP_COLS]
    else:
        o_ref[row:row + n_rows, 0:cols] = val[0:n_rows, 0:cols]


def _take(g, entry):
    row, n_rows, cols = entry
    if cols > SP_COLS:
        return jnp.concatenate([g[row + r:row + r + 1, :] for r in range(n_rows)], axis=1)
    return g[row:row + n_rows, 0:cols]


def _make_pack(name, rows, pieces):
    names = list(pieces)

    def body(*refs):
        o_ref = refs[len(names)]
        o_ref[...] = jnp.zeros_like(o_ref)
        for nm, ref in zip(names, refs):
            if nm == "loss":
                o_ref[R_LOSS:R_LOSS + 1, 0:1] = jnp.sum(ref[...], axis=1, keepdims=True)
            else:
                _put(o_ref, rows[nm], ref[...])

    arrs = [pieces[nm] for nm in names]
    return pl.pallas_call(
        body,
        name=name,
        grid=(1,),
        in_specs=[_full(a.shape) for a in arrs],
        out_specs=_full((SP_ROWS, SP_COLS)),
        out_shape=jax.ShapeDtypeStruct((SP_ROWS, SP_COLS), F32),
        compiler_params=_params(1),
    )(*arrs)


def _small_adamw(packs, rows, w, m, v):
    names = list(SMALL)
    n = len(names)

    def body(p_ref, *refs):
        ins, outs = refs[:3 * n], refs[3 * n:]
        g = p_ref[0]
        for dvc in range(1, N_DEV):
            g = g + p_ref[dvc]
        for i, nm in enumerate(names):
            gp = _take(g, rows[nm])
            delta, mn, vn = _adamw(ins[i][...], gp, ins[n + i][...], ins[2 * n + i][...])
            for kind, val in enumerate((gp, delta, mn, vn)):
                outs[kind * n + i][...] = val
        outs[4 * n][...] = g[R_LOSS:R_LOSS + 1, 0:1]

    arrs = [w[nm] for nm in names] + [m[nm] for nm in names] + [v[nm] for nm in names]
    shapes = [jax.ShapeDtypeStruct(w[nm].shape, F32) for nm in names] * 4 + [jax.ShapeDtypeStruct((1, 1), F32)]
    res = pl.pallas_call(
        body,
        name="small_adamw",
        grid=(1,),
        in_specs=[_full(packs.shape)] + [_full(a.shape) for a in arrs],
        out_specs=[_full(sh.shape) for sh in shapes],
        out_shape=shapes,
        compiler_params=_params(1),
    )(packs, *arrs)
    return [dict(zip(names, res[k * n:(k + 1) * n])) for k in range(4)], res[4 * n]


def _w_in_pieces():
    cs = D_IN_PROJ // N_DEV
    pieces = []
    for d in range(N_DEV):
        lo, hi = d * cs, (d + 1) * cs
        if hi <= CONV_COLS:
            pieces.append((d, 0, cs, lo))
        elif lo >= CONV_COLS:
            pieces.append((d, 0, cs, lo - CONV_COLS + HALF_P))
        else:
            pieces.append((d, 0, CONV_COLS - lo, lo))
            pieces.append((d, CONV_COLS - lo, cs, HALF_P))
    return pieces


def _w_in_full(gathered):
    nb, D, cs = gathered.shape
    tr = _tile(D, 256)

    def body(g_ref, o_ref):
        o_ref[:, CONV_COLS:HALF_P] = jnp.zeros((tr, HALF_P - CONV_COLS), o_ref.dtype)
        o_ref[:, HALF_P + GLA_COLS:P_INT] = jnp.zeros((tr, HALF_P - GLA_COLS), o_ref.dtype)
        for d, a, b, dst in _w_in_pieces():
            o_ref[:, dst:dst + (b - a)] = g_ref[d, :, a:b]

    return pl.pallas_call(
        body,
        name="w_in_full",
        grid=(D // tr,),
        in_specs=[pl.BlockSpec((nb, tr, cs), lambda r: (0, r, 0))],
        out_specs=pl.BlockSpec((tr, P_INT), lambda r: (r, 0)),
        out_shape=jax.ShapeDtypeStruct((D, P_INT), gathered.dtype),
        compiler_params=_params(1),
    )(gathered)


def _w_in_blocks(dw):
    D = dw.shape[0]
    cs = D_IN_PROJ // N_DEV
    tr = _tile(D, 256)

    def body(w_ref, o_ref):
        for d, a, b, src in _w_in_pieces():
            o_ref[d, :, a:b] = w_ref[:, src:src + (b - a)]

    return pl.pallas_call(
        body,
        name="w_in_blocks",
        grid=(D // tr,),
        in_specs=[pl.BlockSpec((tr, P_INT), lambda r: (r, 0))],
        out_specs=pl.BlockSpec((N_DEV, tr, cs), lambda r: (0, r, 0)),
        out_shape=jax.ShapeDtypeStruct((N_DEV, D, cs), dw.dtype),
        compiler_params=_params(1),
    )(dw)


BIG = ("w_in", "w_out", "w_ff_up", "w_ff_down")
ORDER = ("w_in", "conv_w", "conv_norm_g", "w_gate_up", "gate_bias", "gla_norm_g", "w_out", "ln1_g", "ln1_b",
         "w_ff_up", "w_ff_down", "ln2_g", "ln2_b")


def kernel(x, w_in, conv_w, conv_norm_g, w_gate_up, gate_bias, gla_norm_g, w_out, ln1_g, ln1_b, w_ff_up, w_ff_down, ln2_g, ln2_b, loss_target, m_w_in, m_conv_w, m_conv_norm_g, m_w_gate_up, m_gate_bias, m_gla_norm_g, m_w_out, m_ln1_g, m_ln1_b, m_w_ff_up, m_w_ff_down, m_ln2_g, m_ln2_b, v_w_in, v_conv_w, v_conv_norm_g, v_w_gate_up, v_gate_bias, v_gla_norm_g, v_w_out, v_ln1_g, v_ln1_b, v_w_ff_up, v_w_ff_down, v_ln2_g, v_ln2_b):
    T, D = x.shape[1], x.shape[2]
    xs, target = x[0], loss_target[0]
    xi, yi, ci = lax.axis_index("x"), lax.axis_index("y"), lax.axis_index("c")
    chip = 2 * xi + yi
    dev = 2 * chip + ci
    others = [jnp.where(chip <= q, q + 1, q) for q in range(N_CHIP - 1)]
    ids = jnp.stack([dev, chip, ci] + others).astype(jnp.int32)
    conv_cols, gate_cols = conv_w.shape[2], w_gate_up.shape[2]

    def gather(nm, lands):
        return _xfer_start("gather_chips_" + nm, lands, _plan_gather_chips, 4 * len(lands))

    def pass_on(nm, started, after):
        lands = _xfer_wait("gather_chips_wait_" + nm, started, after)
        return _xfer_start("gather_pass_" + nm, lands, _plan_gather_pass, 3 * len(lands))

    def landed(nm, started, after):
        return _xfer_wait("gather_pass_wait_" + nm, started, after)

    rows = _small_rows(D, conv_cols, gate_cols)
    fwd_pack = _make_pack("pack_fwd", rows, {"conv_w": conv_w[0], "w_gate_up": w_gate_up[0]})
    ga_in = gather("w_in", [_cast_place("cast_place_w_in", ids, w_in[0]),
                            _cast_place("cast_place_pack", ids, fwd_pack, dtype=F32)])
    ga, dep = [], ga_in["token"]
    m_in, v_in = m_w_in[0], v_w_in[0]
    for nm, w in zip(BIG[1:], (w_out, w_ff_up, w_ff_down)):
        deps = [dep, m_in, v_in] if nm == "w_ff_down" else [dep]
        ga.append(gather(nm, [_cast_place("cast_place_" + nm, ids, w[0], deps)]))
        dep = ga[-1]["token"]
    xb = _cast_bf16(xs, [dep])
    gp_in = pass_on("w_in", ga_in, xb)
    g_in, g_pack = landed("w_in", gp_in, gp_in["token"])
    w_full = _w_in_full(g_in)
    r_cw, r_gw = rows["conv_w"][0], rows["w_gate_up"][0]
    conv_w_full = g_pack[:, r_cw:r_cw + 3, :conv_cols].transpose(1, 0, 2).reshape(3, -1)
    gate_w_full = g_pack[:, r_gw:r_gw + GATE_RANK, :gate_cols].transpose(1, 0, 2).reshape(GATE_RANK, -1)
    conv_w8 = jnp.pad(conv_w_full, ((0, 5), (0, 0)))
    wg128 = jnp.pad(gate_w_full, ((0, LANE - GATE_RANK), (0, 0))).astype(BF16)
    proj = _proj_fwd(xb, w_full)
    yin = _conv_fwd(proj, conv_w8, conv_norm_g)
    gp_out = pass_on("w_out", ga[0], yin)
    o_all, states, yin = _gla_fwd(proj, wg128, gate_bias, gla_norm_g, yin, deps=[gp_out["token"]])
    w_out_full = landed("w_out", gp_out, o_all)[0].reshape(-1, D)
    gp_up = pass_on("w_ff_up", ga[1], o_all)
    xhat1, x1, rstd1 = _mix_ln1(yin, w_out_full, xs, ln1_g, ln1_b, deps=[gp_up["token"]])
    (w_up_blk,) = landed("w_ff_up", gp_up, x1)
    half = N_DEV // 2
    ra, h2 = _ff_up(x1, w_up_blk, 0, half)
    gp_down = pass_on("w_ff_down", ga[2], ra)
    ra, h2 = _ff_up(x1, w_up_blk, half, N_DEV - half, prev=(ra, h2), deps=[gp_down["token"]])
    w_down_full = landed("w_ff_down", gp_down, ra)[0].reshape(-1, D)
    dh3, dh3b, g_ln2_g, g_ln2_b, loss = _ff_down_loss(h2, w_down_full, xhat1, target, ln1_g, ln1_b, ln2_g, ln2_b)

    def to_core(nm, grad):
        recv = lax.empty((N_CHIP,) + grad.shape[1:], F32)
        return _xfer_start("reduce_core_" + nm, [grad, recv], _plan_reduce_core, N_CHIP)

    def to_chips(nm, started, after):
        grad, recv = _xfer_wait("reduce_core_wait_" + nm, started, after)
        part = _chip_sums("chip_sums_" + nm, ids, grad, recv)
        land = lax.empty(part.shape, BF16)
        return grad, recv, _xfer_start("reduce_chips_" + nm, [part, land], _plan_reduce_chips, N_CHIP - 1)

    da = _ff_down_bwd_act(dh3b, w_down_full, ra)
    gw_down = _grad_w("grad_w_down", h2, dh3b).reshape(N_DEV, -1, D)
    rc_down = to_core("w_ff_down", gw_down)
    gw_up = _grad_w_up_blk(x1, da, N_DEV, deps=[rc_down["token"]])
    gw_down, rv_down, rs_down = to_chips("w_ff_down", rc_down, gw_up)
    rc_up = to_core("w_ff_up", gw_up)
    dh1, dh1b, g_ln1_g, g_ln1_b = _ff_up_bwd_ln1(da, w_up_blk, dh3, xhat1, rstd1, ln1_g,
                                                 deps=[rs_down["token"], rc_up["token"]])
    gw_up, rv_up, rs_up = to_chips("w_ff_up", rc_up, dh1b)
    dyin = _mix_bwd(dh1b, w_out_full, deps=[rs_up["token"]])
    gw_out = _grad_w("grad_w_out", yin, dh1b).reshape(N_DEV, -1, D)
    rc_out = to_core("w_out", gw_out)
    dproj, g_conv_w, g_conv_g = _conv_bwd(proj, dyin, conv_w8, conv_norm_g, deps=[rc_out["token"]])
    dproj, g_gate_w, g_gate_b, g_gla_g = _gla_bwd(proj, wg128, gate_bias, gla_norm_g, o_all, states, dyin, dproj)
    gw_out, rv_out, rs_out = to_chips("w_out", rc_out, dproj)
    gw_in = _w_in_blocks(_grad_w("grad_w_in", xb, dproj, tn_pref=1280, tk_pref=2048, deps=[rs_out["token"]]))
    rc_in = to_core("w_in", gw_in)

    big = {}

    def finish(nm, grad, recv, started, w, m, v, after):
        _, land = _xfer_wait("reduce_chips_wait_" + nm, started, after)
        res = _reduce_adamw("adamw_" + nm, ids, grad, recv, land, w[0], m[0], v[0])
        big[nm] = [a[None] for a in res]
        return res[0]

    done = finish("w_ff_down", gw_down, rv_down, rs_down, w_ff_down, m_w_ff_down, v_w_ff_down, rc_in["token"])
    done = finish("w_ff_up", gw_up, rv_up, rs_up, w_ff_up, m_w_ff_up, v_w_ff_up, done)
    full_rows = _small_rows(D, D_CONV, D_GLA_K)
    pack = _make_pack("pack_grads", full_rows, {
        "conv_w": g_conv_w, "conv_norm_g": g_conv_g, "gate_bias": g_gate_b, "gla_norm_g": g_gla_g, "ln1_g": g_ln1_g,
        "ln1_b": g_ln1_b, "ln2_g": g_ln2_g, "ln2_b": g_ln2_b, "loss": loss, "w_gate_up": g_gate_w})
    (packs,) = _all_gather("gather_small_grads", [pack], deps=[done])
    gw_in, rv_in, rs_in = to_chips("w_in", rc_in, packs)
    done = finish("w_out", gw_out, rv_out, rs_out, w_out, m_w_out, v_w_out, rs_in["token"])
    grad_x = _proj_bwd_x(dproj, w_full, dh1, deps=[done])
    finish("w_in", gw_in, rv_in, rs_in, w_in, (m_in,), (v_in,), grad_x)

    def own_cols(row, n_rows, width):
        cut = lax.dynamic_slice(packs, (0, row, dev * width), (N_DEV, n_rows, width))
        return jnp.pad(cut, ((0, 0), (0, 0), (0, SP_COLS - width)))

    packs_own = jnp.concatenate([own_cols(r_cw, 3, conv_cols), packs[:, r_cw + 3:r_gw],
                                 own_cols(r_gw, GATE_RANK, gate_cols)], axis=1)
    as2d = lambda a: a[0] if a.ndim == 3 else a
    w_s = dict(zip(SMALL, map(as2d, (conv_w, conv_norm_g, w_gate_up, gate_bias, gla_norm_g, ln1_g, ln1_b, ln2_g, ln2_b))))
    m_s = dict(zip(SMALL, map(as2d, (m_conv_w, m_conv_norm_g, m_w_gate_up, m_gate_bias, m_gla_norm_g, m_ln1_g,
                                     m_ln1_b, m_ln2_g, m_ln2_b))))
    v_s = dict(zip(SMALL, map(as2d, (v_conv_w, v_conv_norm_g, v_w_gate_up, v_gate_bias, v_gla_norm_g, v_ln1_g,
                                     v_ln1_b, v_ln2_g, v_ln2_b))))
    small, loss_sum = _small_adamw(packs_own, rows, w_s, m_s, v_s)

    def leaf(kind, name):
        if name in BIG:
            return big[name][kind]
        a = small[kind][name]
        return a[None] if name in ("conv_w", "w_gate_up") else a

    out = [loss_sum[0, 0], grad_x[None]]
    for kind in range(4):
        out += [leaf(kind, nm) for nm in ORDER]
    return tuple(out)
```

```python
import jax
import jax.numpy as jnp
from jax import lax
from jax.experimental import pallas as pl
from jax.experimental.pallas import tpu as pltpu

F32 = jnp.float32
BF16 = jnp.bfloat16

D_CONV = 1024
CONV_GROUPS = 8
GLA_HEADS = 4
HEAD_K = 128
HEAD_V = 256
D_GLA_K = 512
D_GLA_V = 1024
GATE_RANK = 16
GATE_TAU = 16.0
CHUNK = 64
LN_EPS = 1e-5
RMS_EPS = 1e-6
DN_ALPHA = 2.0 ** 0.25
D_IN_PROJ = 6160
ADAM_LR = 0.001
ADAM_B1 = 0.9
ADAM_B2 = 0.999
ADAM_EPS = 1e-08
ADAM_WD = 0.01
ADAM_STEP = 10

N_DEV = 8
N_CHIP = 4
LANE = 128
HALF_P = 3200
P_INT = 2 * HALF_P
CONV_COLS = 3 * D_CONV
GLA_COLS = D_IN_PROJ - CONV_COLS
SP_ROWS = 32
SP_COLS = 1024
VMEM_LIMIT = 56 * 1024 * 1024

NN = ((1,), (0,))
NT = ((1,), (1,))
TN = ((0,), (0,))
MESH = pl.DeviceIdType.MESH


def _dot(a, b, dims, precision=None):
    return lax.dot_general(a, b, (dims, ((), ())), preferred_element_type=F32, precision=precision)


def _tile(n, pref):
    if n <= pref:
        return n
    t = (pref // LANE) * LANE
    while t > 0 and n % t:
        t -= LANE
    assert t > 0, (n, pref)
    return t


def _params(n_axes):
    return pltpu.CompilerParams(dimension_semantics=("arbitrary",) * n_axes, vmem_limit_bytes=VMEM_LIMIT)


def _full(shape):
    nd = len(shape)
    return pl.BlockSpec(shape, lambda *_: (0,) * nd)


def _hbm_specs(n):
    return [pl.BlockSpec(memory_space=pl.ANY)] * n


def _mm(name, mode, a, b, *, M, N, K, tm, tn, tk, outs, epilogue, extras=(), a_fn=None, a_spec=None, b_spec=None,
        deps=()):
    ni, nj, nk = M // tm, N // tn, K // tk
    assert ni * tm == M and nj * tn == N and nk * tk == K, (name, M, N, K, tm, tn, tk)
    if a_spec is None:
        a_spec = (pl.BlockSpec((tk, tm), lambda i, j, k: (k, i)) if mode == "tn"
                  else pl.BlockSpec((tm, tk), lambda i, j, k: (i, k)))
    if b_spec is None:
        b_spec = (pl.BlockSpec((tn, tk), lambda i, j, k: (j, k)) if mode == "nt"
                  else pl.BlockSpec((tk, tn), lambda i, j, k: (k, j)))
    dims = {"nn": NN, "nt": NT, "tn": TN}[mode]
    n_ex, n_out, n_dep = len(extras), len(outs), len(deps)

    def body(*refs):
        a_ref, b_ref = refs[0], refs[1]
        ex = refs[2:2 + n_ex]
        o = refs[2 + n_ex + n_dep:2 + n_ex + n_dep + n_out]
        acc_ref = refs[2 + n_ex + n_dep + n_out]
        i, j, k = pl.program_id(0), pl.program_id(1), pl.program_id(2)
        if nk > 1:
            @pl.when(k == 0)
            def _():
                acc_ref[...] = jnp.zeros_like(acc_ref)

        av = a_ref[...]
        if a_fn is not None:
            av = a_fn(av)
        part = _dot(av, b_ref[...], dims)
        if nk == 1 and epilogue is None:
            o[0][...] = part.astype(o[0].dtype)
        elif nk == 1:
            acc_ref[...] = part
            epilogue(acc_ref, ex, o, i, j)
        else:
            acc_ref[...] += part

            @pl.when(k == nk - 1)
            def _():
                if epilogue is None:
                    o[0][...] = acc_ref[...].astype(o[0].dtype)
                else:
                    epilogue(acc_ref, ex, o, i, j)

    return pl.pallas_call(
        body,
        name=name,
        grid=(ni, nj, nk),
        in_specs=[a_spec, b_spec] + [s for _, s in extras] + _hbm_specs(n_dep),
        out_specs=[s for _, s in outs],
        out_shape=[s for s, _ in outs],
        scratch_shapes=[pltpu.VMEM((8, LANE) if nk == 1 and epilogue is None else (tm, tn), F32)],
        compiler_params=_params(3),
    )(a, b, *[x for x, _ in extras], *deps)


def _mm_rows(name, mode, a, b, *, M, N, K, tm, tk, row_ins, vec_ins, row_outs, stat_outs, chunk_fn,
             b_spec=None, deps=()):
    ni, nk = M // tm, K // tk
    rc = tm // nk
    assert ni * tm == M and nk * tk == K and rc * nk == tm and rc % 16 == 0, (name, M, K, tm, tk)
    dims = {"nn": NN, "nt": NT}[mode]
    last = ni - 1

    def kk(i, k):
        return jnp.where(i < ni, k, nk - 1)

    a_spec = pl.BlockSpec((tm, tk), lambda i, k: (jnp.minimum(i, last), kk(i, k)))
    if b_spec is None:
        b_spec = (pl.BlockSpec((N, tk), lambda i, k: (0, kk(i, k))) if mode == "nt"
                  else pl.BlockSpec((tk, N), lambda i, k: (kk(i, k), 0)))
    prev_rows = lambda i, k: (jnp.maximum((i - 1) * nk + k, 0), 0)
    n_ri, n_vi, n_ro, n_so, n_dep = len(row_ins), len(vec_ins), len(row_outs), len(stat_outs), len(deps)

    def body(*refs):
        a_ref, b_ref = refs[0], refs[1]
        pos = 2
        ri = refs[pos:pos + n_ri]; pos += n_ri
        vi = refs[pos:pos + n_vi]; pos += n_vi + n_dep
        ro = refs[pos:pos + n_ro]; pos += n_ro
        so = refs[pos:pos + n_so]; pos += n_so
        accs = refs[pos:pos + 2]
        i, k = pl.program_id(0), pl.program_id(1)

        @pl.when((i == 0) & (k == 0))
        def _():
            accs[0][...] = jnp.zeros_like(accs[0])
            accs[1][...] = jnp.zeros_like(accs[1])
            for st in so:
                st[...] = jnp.zeros_like(st)

        def finish_rows(prev_ref):
            rows = pl.ds(pl.multiple_of(k * rc, rc), rc)
            done = prev_ref[rows, :]
            prev_ref[rows, :] = jnp.zeros((rc, N), F32)
            chunk_fn(done, i > 0, ri, vi, ro, so)

        def accumulate(acc_ref, after_ref):
            rp = tm // ROW_PARTS
            bv = b_ref[...]
            for part in range(ROW_PARTS):
                av = a_ref[part * rp:(part + 1) * rp, :]
                if part == 1:
                    tail = after_ref[rc - 16:rc, 0:LANE].astype(F32)
                    sixteen = jnp.uint32(16)
                    zero = lax.bitcast_convert_type(tail, jnp.uint32)
                    zero = lax.shift_right_logical(lax.shift_right_logical(zero, sixteen), sixteen)
                    av = av + jnp.tile(zero.astype(F32).astype(av.dtype), (rp // 16, tk // LANE))
                acc_ref[part * rp:(part + 1) * rp, :] += _dot(av, bv, dims)

        for parity in (0, 1):
            @pl.when((i < ni) & (lax.rem(i, 2) == parity))
            def _(parity=parity):
                finish_rows(accs[1 - parity])
                accumulate(accs[parity], ro[0])

        @pl.when(i == ni)
        def _():
            finish_rows(accs[last % 2])

    row_spec = lambda arr: pl.BlockSpec((rc, arr.shape[1]), prev_rows)
    return pl.pallas_call(
        body,
        name=name,
        grid=(ni + 1, nk),
        in_specs=[a_spec, b_spec] + [row_spec(x) for x in row_ins] + [_full(x.shape) for x in vec_ins]
        + _hbm_specs(n_dep),
        out_specs=[row_spec(s) for s in row_outs] + [_full(s.shape) for s in stat_outs],
        out_shape=list(row_outs) + list(stat_outs),
        scratch_shapes=[pltpu.VMEM((tm, N), F32), pltpu.VMEM((tm, N), F32)],
        compiler_params=_params(2),
    )(a, b, *row_ins, *vec_ins, *deps)


ROW_PARTS = 2
SUB_ROWS = 16


def _by_sub_rows(n_rows, fn):
    sums = None
    for r0 in range(0, n_rows, SUB_ROWS):
        part = fn(slice(r0, r0 + SUB_ROWS))
        if part:
            sums = part if sums is None else tuple(x + y for x, y in zip(sums, part))
    return sums


def _to_bf16(v):
    return v.astype(BF16)


def _ln_bwd(dy, xhat, rstd, g):
    dxh = dy * g
    m1 = jnp.mean(dxh, axis=-1, keepdims=True)
    m2 = jnp.mean(dxh * xhat, axis=-1, keepdims=True)
    return rstd * (dxh - m1 - xhat * m2)


def _ln_fwd(h):
    mu = jnp.mean(h, axis=-1, keepdims=True)
    xc = h - mu
    var = jnp.mean(xc * xc, axis=-1, keepdims=True)
    rstd = lax.rsqrt(var + LN_EPS)
    return xc * rstd, rstd


def _proj_fwd(x, w_full, deps=()):
    T, D = x.shape
    P = w_full.shape[1]
    tm, tn = _tile(T, 1024), _tile(P, 1280)
    return _mm("proj_fwd", "nn", x, w_full, M=T, N=P, K=D, tm=tm, tn=tn, tk=D,
               outs=[(jax.ShapeDtypeStruct((T, P), F32), pl.BlockSpec((tm, tn), lambda i, j, k: (i, j)))],
               epilogue=None, deps=deps)[0]


def _cast_bf16(x, deps=()):
    T, D = x.shape
    tm = _tile(T, 512)

    def body(x_ref, *rest):
        rest[len(deps)][...] = x_ref[...].astype(BF16)

    return pl.pallas_call(
        body,
        name="cast_x",
        grid=(T // tm,),
        in_specs=[pl.BlockSpec((tm, D), lambda i: (i, 0))] + _hbm_specs(len(deps)),
        out_specs=pl.BlockSpec((tm, D), lambda i: (i, 0)),
        out_shape=jax.ShapeDtypeStruct((T, D), BF16),
        compiler_params=_params(1),
    )(x, *deps)


def _conv_shift(h, hp):
    row = lax.broadcasted_iota(jnp.int32, h.shape, 0)
    hm1 = hp[7:8, :]
    hm2 = hp[6:7, :]
    h1 = jnp.where(row == 0, hm1, pltpu.roll(h, 1, 0))
    h2 = jnp.where(row == 0, hm2, jnp.where(row == 1, hm1, pltpu.roll(h, 2, 0)))
    return h1, h2


def _conv_fwd(proj, conv_w8, conv_g):
    T = proj.shape[0]
    tt = _tile(T, 256)
    nt = T // tt
    t8 = tt // 8

    def body(b_ref, c_ref, u_ref, cp_ref, up_ref, w_ref, g_ref, yin_ref):
        i = pl.program_id(0)
        h = c_ref[...] * u_ref[...]
        hp = jnp.where(i > 0, cp_ref[...] * up_ref[...], 0.0)
        h1, h2 = _conv_shift(h, hp)
        w = w_ref[...]
        y = w[0:1, :] * h2 + w[1:2, :] * h1 + w[2:3, :] * h
        p = b_ref[...] * y
        parts = []
        for gi in range(CONV_GROUPS):
            pg = p[:, gi * LANE:(gi + 1) * LANE]
            r = lax.rsqrt(jnp.mean(pg * pg, axis=-1, keepdims=True) + RMS_EPS)
            parts.append(pg * r)
        yn = jnp.concatenate(parts, axis=1) * g_ref[...]
        yin_ref[...] = yn.astype(BF16)

    def col(cidx):
        return pl.BlockSpec((tt, D_CONV), lambda i: (i, cidx))

    def prev(cidx):
        return pl.BlockSpec((8, D_CONV), lambda i: (jnp.maximum(i * t8 - 1, 0), cidx))

    return pl.pallas_call(
        body,
        name="conv_fwd",
        grid=(nt,),
        in_specs=[col(0), col(1), col(2), prev(1), prev(2), _full((8, D_CONV)), _full((1, D_CONV))],
        out_specs=pl.BlockSpec((tt, D_CONV), lambda i: (i, 0)),
        out_shape=jax.ShapeDtypeStruct((T, 2 * D_CONV), BF16),
        compiler_params=_params(1),
    )(proj, proj, proj, proj, proj, conv_w8, conv_g)


def _log_sigmoid(z):
    return jnp.minimum(z, 0.0) - jnp.log(1.0 + jnp.exp(-jnp.abs(z)))


STEP_CHUNKS = 4
STEP_ROWS = STEP_CHUNKS * CHUNK


def _gla_step_terms(blk, wg_ref, gb_ref):
    zl = blk[:, 3072:3200]
    z = _dot(zl.astype(BF16), wg_ref[...], NN) + gb_ref[...]
    log_a = _log_sigmoid(z) * (1.0 / GATE_TAU)
    ri = lax.broadcasted_iota(jnp.int32, (STEP_ROWS, STEP_ROWS), 0)
    ci = lax.broadcasted_iota(jnp.int32, (STEP_ROWS, STEP_ROWS), 1)
    same = (ri // CHUNK) == (ci // CHUNK)
    lower = (same & (ri >= ci)).astype(F32)
    bcum = _dot(lower, log_a, NN, precision=lax.Precision.HIGHEST)
    return zl, z, bcum, same


def _causal():
    return (lax.broadcasted_iota(jnp.int32, (CHUNK, CHUNK), 0) >= lax.broadcasted_iota(jnp.int32, (CHUNK, CHUNK), 1))


def _gla_head_terms(q, k, bcum, h):
    sl = slice(h * HEAD_K, (h + 1) * HEAD_K)
    bh = bcum[:, sl]
    bl = bh[CHUNK - 1:CHUNK, :]
    eb = jnp.exp(bh)
    enb = jnp.exp(-bh)
    eend = jnp.exp(bl - bh)
    dec = jnp.exp(bl)
    qd = q[:, sl] * (HEAD_K ** -0.5) * eb
    ki = k[:, sl] * enb
    ke = k[:, sl] * eend
    return eb, enb, eend, dec, qd, ki, ke


def _sigmoid(x):
    return 1.0 / (1.0 + jnp.exp(-x))


def _gla_fwd(proj, wg128, gbias, gng, yin, deps=()):
    T = proj.shape[0]
    nch = T // CHUNK
    nst = T // STEP_ROWS

    def body(p_ref, wg_ref, gb_ref, gn_ref, yin_in_ref, *rest):
        o_ref, st_ref, yin_ref, s_ref = rest[len(deps):]
        n = pl.program_id(0)

        @pl.when(n == 0)
        def _():
            s_ref[...] = jnp.zeros_like(s_ref)

        blk = p_ref[...]
        _, _, bcum_all, _ = _gla_step_terms(blk, wg_ref, gb_ref)
        causal = _causal()
        gn = gn_ref[...]
        states = [s_ref[h] for h in range(GLA_HEADS)]
        for c in range(STEP_CHUNKS):
            rows = slice(c * CHUNK, (c + 1) * CHUNK)
            q, k = blk[rows, 0:512], blk[rows, 512:1024]
            v, r = blk[rows, 1024:2048], blk[rows, 2048:3072]
            bcum = bcum_all[rows, :]
            for h in range(GLA_HEADS):
                _, _, _, dec, qd, ki, ke = _gla_head_terms(q, k, bcum, h)
                vs = slice(h * HEAD_V, (h + 1) * HEAD_V)
                vb = v[:, vs].astype(BF16)
                qdb = qd.astype(BF16)
                a = jnp.where(causal, _dot(qdb, ki.astype(BF16), NT), 0.0)
                st = states[h]
                o = _dot(a.astype(BF16), vb, NN) + _dot(qdb, st.astype(BF16), NT)
                st_ref[c, h] = st
                states[h] = dec * st + _dot(vb, ke.astype(BF16), TN)
                o_ref[rows, vs] = o
                rinv = lax.rsqrt(jnp.mean(o * o, axis=-1, keepdims=True) + RMS_EPS)
                rh = r[:, vs]
                yin_ref[rows, vs] = (o * rinv * gn[:, vs] * (rh * _sigmoid(rh))).astype(BF16)
        for h in range(GLA_HEADS):
            s_ref[h] = states[h]

    return pl.pallas_call(
        body,
        name="gla_fwd",
        grid=(nst,),
        in_specs=[pl.BlockSpec((STEP_ROWS, HALF_P), lambda n: (n, 1)), _full((LANE, D_GLA_K)), _full((1, D_GLA_K)),
                  _full((1, D_GLA_V)), pl.BlockSpec(memory_space=pl.ANY)] + _hbm_specs(len(deps)),
        out_specs=[pl.BlockSpec((STEP_ROWS, D_GLA_V), lambda n: (n, 0)),
                   pl.BlockSpec((STEP_CHUNKS, GLA_HEADS, HEAD_V, HEAD_K), lambda n: (n, 0, 0, 0)),
                   pl.BlockSpec((STEP_ROWS, D_GLA_V), lambda n: (n, 1))],
        out_shape=[jax.ShapeDtypeStruct((T, D_GLA_V), F32),
                   jax.ShapeDtypeStruct((nch, GLA_HEADS, HEAD_V, HEAD_K), F32),
                   jax.ShapeDtypeStruct(yin.shape, BF16)],
        scratch_shapes=[pltpu.VMEM((GLA_HEADS, HEAD_V, HEAD_K), F32)],
        input_output_aliases={4: 2},
        compiler_params=_params(1),
    )(proj, wg128, gbias, gng, yin, *deps)


def _mix_ln1(yin, w_out, x, ln_g, ln_b, deps=()):
    T, D = x.shape
    KY = yin.shape[1]
    tm = _tile(T, 1024)

    def chunk(acc, valid, ri, vi, ro, so):
        g, b = vi[0][...], vi[1][...]

        def sub(rows):
            xhat, rstd = _ln_fwd(DN_ALPHA * ri[0][rows, :] + acc[rows, :])
            ro[0][rows, :] = xhat
            ro[1][rows, :] = (xhat * g + b).astype(BF16)
            ro[2][rows, :] = rstd

        _by_sub_rows(acc.shape[0], sub)

    return _mm_rows("mix_ln1", "nn", yin, w_out, M=T, N=D, K=KY, tm=tm, tk=_tile(KY, 512),
                    row_ins=[x], vec_ins=[ln_g, ln_b],
                    row_outs=[jax.ShapeDtypeStruct((T, D), F32), jax.ShapeDtypeStruct((T, D), BF16),
                              jax.ShapeDtypeStruct((T, 1), F32)],
                    stat_outs=[], chunk_fn=chunk, deps=deps)


def _ff_up(x1, w_up_blk, first, count, prev=None, deps=()):
    T, D = x1.shape
    nb, _, fb = w_up_blk.shape
    tm = _tile(T, 1024)
    ni = T // tm
    n_dep = len(deps) + (2 if prev is not None else 0)

    def body(a_ref, b_ref, *rest):
        ra_ref, h2_ref = rest[n_dep:n_dep + 2]
        ra = jnp.maximum(_dot(a_ref[...], b_ref[...], NN), 0.0)
        ra_ref[...] = ra.astype(BF16)
        h2_ref[...] = (ra * ra).astype(BF16)

    blk = pl.BlockSpec((tm, fb), lambda i, j: (i, first + j))
    shp = jax.ShapeDtypeStruct((T, nb * fb), BF16)
    keep = list(prev) if prev is not None else []
    return pl.pallas_call(
        body,
        name="ff_up_%d" % first,
        grid=(ni, count),
        in_specs=[pl.BlockSpec((tm, D), lambda i, j: (i, 0)),
                  pl.BlockSpec((None, D, fb), lambda i, j: (first + j, 0, 0))] + _hbm_specs(n_dep),
        out_specs=[blk, blk],
        out_shape=[shp, shp],
        input_output_aliases=({2: 0, 3: 1} if prev is not None else {}),
        compiler_params=_params(2),
    )(x1, w_up_blk, *keep, *deps)


def _ff_down_loss(h2, w_down, xhat1, target, g1, b1, g2, b2):
    T, F = h2.shape
    D = w_down.shape[1]
    tm = _tile(T, 1024)
    inv_d = 1.0 / D

    def chunk(acc, valid, ri, vi, ro, so):
        g1v, b1v, g2v, b2v = (v[...] for v in vi)

        def sub(rows):
            x1 = ri[0][rows, :] * g1v + b1v
            xhat, rstd = _ln_fwd(DN_ALPHA * x1 + acc[rows, :])
            e = xhat * g2v + b2v - ri[1][rows, :]
            dy = e * inv_d
            dh = _ln_bwd(dy, xhat, rstd, g2v)
            ro[0][rows, :] = dh
            ro[1][rows, :] = dh.astype(BF16)
            return (jnp.sum(dy * xhat, axis=0, keepdims=True), jnp.sum(dy, axis=0, keepdims=True),
                    jnp.sum(e * e, axis=0, keepdims=True))

        sg, sb, sl = _by_sub_rows(acc.shape[0], sub)
        so[0][...] += jnp.where(valid, sg, 0.0)
        so[1][...] += jnp.where(valid, sb, 0.0)
        so[2][...] += jnp.where(valid, sl * (0.5 * inv_d), 0.0)

    vshape = jax.ShapeDtypeStruct((1, D), F32)
    return _mm_rows("ff_down_loss", "nn", h2, w_down, M=T, N=D, K=F, tm=tm, tk=_tile(F, 1024),
                    row_ins=[xhat1, target], vec_ins=[g1, b1, g2, b2],
                    row_outs=[jax.ShapeDtypeStruct((T, D), F32), jax.ShapeDtypeStruct((T, D), BF16)],
                    stat_outs=[vshape, vshape, vshape], chunk_fn=chunk)


def _ff_down_bwd_act(dh3b, w_down, ra):
    T, D = dh3b.shape
    F = w_down.shape[0]
    tm, tn = _tile(T, 1024), _tile(F, 1024)

    def ep(acc_ref, ex, o, i, j):
        o[0][...] = (acc_ref[...] * (2.0 * ex[0][...].astype(F32))).astype(BF16)

    blk = pl.BlockSpec((tm, tn), lambda i, j, k: (i, j))
    return _mm("ff_down_bwd_act", "nt", dh3b, w_down, M=T, N=F, K=D, tm=tm, tn=tn, tk=D,
               outs=[(jax.ShapeDtypeStruct((T, F), BF16), blk)], extras=[(ra, blk)], epilogue=ep)[0]


def _grad_w(name, a, b, *, a_fn=None, tm_pref=1024, tn_pref=1024, tk_pref=4096, deps=()):
    T, M = a.shape
    N = b.shape[1]
    tm, tn, tk = _tile(M, tm_pref), _tile(N, tn_pref), _tile(T, tk_pref)
    return _mm(name, "tn", a, b, M=M, N=N, K=T, tm=tm, tn=tn, tk=tk, a_fn=a_fn, deps=deps,
               outs=[(jax.ShapeDtypeStruct((M, N), F32), pl.BlockSpec((tm, tn), lambda i, j, k: (i, j)))],
               epilogue=None)[0]


def _grad_w_up_blk(x1, da, nb, deps=()):
    T, D = x1.shape
    F = da.shape[1]
    fb = F // nb
    tm, tk = _tile(D, 1024), _tile(T, 4096)
    return _mm("grad_w_up", "tn", x1, da, M=D, N=F, K=T, tm=tm, tn=fb, tk=tk, deps=deps,
               outs=[(jax.ShapeDtypeStruct((nb, D, fb), F32),
                      pl.BlockSpec((None, tm, fb), lambda i, j, k: (j, i, 0)))],
               epilogue=None)[0]


def _ff_up_bwd_ln1(da, w_up_blk, dh3, xhat1, rstd1, g1, deps=()):
    T, F = da.shape
    nb, D, fb = w_up_blk.shape
    tm = _tile(T, 1024)

    def chunk(acc, valid, ri, vi, ro, so):
        g = vi[0][...]

        def sub(rows):
            dx1 = DN_ALPHA * ri[0][rows, :] + acc[rows, :]
            xhat = ri[1][rows, :]
            dh = _ln_bwd(dx1, xhat, ri[2][rows, :], g)
            ro[0][rows, :] = dh
            ro[1][rows, :] = dh.astype(BF16)
            return jnp.sum(dx1 * xhat, axis=0, keepdims=True), jnp.sum(dx1, axis=0, keepdims=True)

        sg, sb = _by_sub_rows(acc.shape[0], sub)
        so[0][...] += jnp.where(valid, sg, 0.0)
        so[1][...] += jnp.where(valid, sb, 0.0)

    nk = F // fb
    vshape = jax.ShapeDtypeStruct((1, D), F32)
    return _mm_rows("ff_up_bwd_ln1", "nt", da, w_up_blk, M=T, N=D, K=F, tm=tm, tk=fb,
                    b_spec=pl.BlockSpec((None, D, fb), lambda i, k: (jnp.where(i < T // tm, k, nk - 1), 0, 0)),
                    row_ins=[dh3, xhat1, rstd1], vec_ins=[g1],
                    row_outs=[jax.ShapeDtypeStruct((T, D), F32), jax.ShapeDtypeStruct((T, D), BF16)],
                    stat_outs=[vshape, vshape], chunk_fn=chunk, deps=deps)


def _mix_bwd(dh1b, w_out, deps=()):
    T, D = dh1b.shape
    KY = w_out.shape[0]
    tm, tn = _tile(T, 1024), _tile(KY, 1024)
    return _mm("mix_bwd", "nt", dh1b, w_out, M=T, N=KY, K=D, tm=tm, tn=tn, tk=D, deps=deps,
               outs=[(jax.ShapeDtypeStruct((T, KY), F32), pl.BlockSpec((tm, tn), lambda i, j, k: (i, j)))],
               epilogue=None)[0]


def _conv_bwd(proj, dyin, conv_w8, conv_g, deps=()):
    T = proj.shape[0]
    tt = _tile(T, 256)
    nt = T // tt
    t8 = tt // 8
    nx = tt + 8

    def body(b_ref, c_ref, u_ref, d_ref, bn_ref, cn_ref, un_ref, dn_ref, cp_ref, up_ref, w_ref, g_ref, *rest):
        dp_ref, dw_ref, dg_ref = rest[len(deps):]
        i = pl.program_id(0)

        @pl.when(i == 0)
        def _():
            dw_ref[...] = jnp.zeros_like(dw_ref)
            dg_ref[...] = jnp.zeros_like(dg_ref)

        more = i < nt - 1

        def ext(cur_ref, nxt_ref):
            return jnp.concatenate([cur_ref[...], jnp.where(more, nxt_ref[...], 0.0)], axis=0)

        bx, cx, ux, dx = ext(b_ref, bn_ref), ext(c_ref, cn_ref), ext(u_ref, un_ref), ext(d_ref, dn_ref)
        hx = cx * ux
        hp = jnp.where(i > 0, cp_ref[...] * up_ref[...], 0.0)
        h1, h2 = _conv_shift(hx, hp)
        w = w_ref[...]
        g = g_ref[...]
        yx = w[0:1, :] * h2 + w[1:2, :] * h1 + w[2:3, :] * hx
        px = bx * yx
        dps, dgs = [], []
        for gi in range(CONV_GROUPS):
            sl = slice(gi * LANE, (gi + 1) * LANE)
            pg, dg_ = px[:, sl], dx[:, sl]
            r = lax.rsqrt(jnp.mean(pg * pg, axis=-1, keepdims=True) + RMS_EPS)
            gd = g[:, sl] * dg_
            dps.append(r * gd - pg * (r * r * r) * jnp.mean(pg * gd, axis=-1, keepdims=True))
            dgs.append(jnp.sum((dg_ * pg * r)[:tt, :], axis=0, keepdims=True))
        dpx = jnp.concatenate(dps, axis=1)
        dg_ref[...] += jnp.concatenate(dgs, axis=1)
        dyx = dpx * bx
        dyc = dyx[:tt, :]
        dh = (w[2:3, :] * dyx + w[1:2, :] * pltpu.roll(dyx, nx - 1, 0) + w[0:1, :] * pltpu.roll(dyx, nx - 2, 0))[:tt, :]
        dw_ref[0:1, :] += jnp.sum(dyc * h2[:tt, :], axis=0, keepdims=True)
        dw_ref[1:2, :] += jnp.sum(dyc * h1[:tt, :], axis=0, keepdims=True)
        dw_ref[2:3, :] += jnp.sum(dyc * hx[:tt, :], axis=0, keepdims=True)
        dp_ref[:, 0:D_CONV] = (dpx * yx)[:tt, :].astype(BF16)
        dp_ref[:, D_CONV:2 * D_CONV] = (dh * u_ref[...]).astype(BF16)
        dp_ref[:, 2 * D_CONV:3 * D_CONV] = (dh * c_ref[...]).astype(BF16)
        dp_ref[:, 3 * D_CONV:HALF_P] = jnp.zeros((tt, HALF_P - 3 * D_CONV), BF16)

    def col(cidx):
        return pl.BlockSpec((tt, D_CONV), lambda i: (i, cidx))

    def nxt(cidx):
        return pl.BlockSpec((8, D_CONV), lambda i: (jnp.minimum((i + 1) * t8, T // 8 - 1), cidx))

    def prev(cidx):
        return pl.BlockSpec((8, D_CONV), lambda i: (jnp.maximum(i * t8 - 1, 0), cidx))

    return pl.pallas_call(
        body,
        name="conv_bwd",
        grid=(nt,),
        in_specs=[col(0), col(1), col(2), col(0), nxt(0), nxt(1), nxt(2), nxt(0), prev(1), prev(2),
                  _full((8, D_CONV)), _full((1, D_CONV))] + _hbm_specs(len(deps)),
        out_specs=[pl.BlockSpec((tt, HALF_P), lambda i: (i, 0)), _full((8, D_CONV)), _full((1, D_CONV))],
        out_shape=[jax.ShapeDtypeStruct((T, P_INT), BF16), jax.ShapeDtypeStruct((8, D_CONV), F32),
                   jax.ShapeDtypeStruct((1, D_CONV), F32)],
        compiler_params=_params(1),
    )(proj, proj, proj, dyin, proj, proj, proj, dyin, proj, proj, conv_w8, conv_g, *deps)


def _gla_bwd(proj, wg128, gbias, gng, o_all, states, dyin, dproj):
    T = proj.shape[0]
    nst = T // STEP_ROWS

    def body(p_ref, wg_ref, gb_ref, gn_ref, o_ref, st_ref, d_ref, dp_in_ref,
             dp_ref, dwg_ref, dgb_ref, dgn_ref, ds_ref):
        n = pl.program_id(0)

        @pl.when(n == 0)
        def _():
            ds_ref[...] = jnp.zeros_like(ds_ref)
            dwg_ref[...] = jnp.zeros_like(dwg_ref)
            dgb_ref[...] = jnp.zeros_like(dgb_ref)
            dgn_ref[...] = jnp.zeros_like(dgn_ref)

        blk = p_ref[...]
        zl, z, bcum_all, same = _gla_step_terms(blk, wg_ref, gb_ref)
        causal = _causal()
        gn = gn_ref[...]
        ri = lax.broadcasted_iota(jnp.int32, (STEP_ROWS, STEP_ROWS), 0)
        ci = lax.broadcasted_iota(jnp.int32, (STEP_ROWS, STEP_ROWS), 1)
        upper = (same & (ri <= ci)).astype(F32)
        dstates = [ds_ref[h] for h in range(GLA_HEADS)]
        db_rows, dbl_rows, dgn_sum = [None] * STEP_CHUNKS, [None] * STEP_CHUNKS, [None] * GLA_HEADS
        for c in reversed(range(STEP_CHUNKS)):
            rows = slice(c * CHUNK, (c + 1) * CHUNK)
            q, k = blk[rows, 0:512], blk[rows, 512:1024]
            v, r = blk[rows, 1024:2048], blk[rows, 2048:3072]
            bcum = bcum_all[rows, :]
            db_parts, dbl_parts = [], []
            for h in range(GLA_HEADS):
                eb, enb, eend, dec, qd, ki, ke = _gla_head_terms(q, k, bcum, h)
                vs = slice(h * HEAD_V, (h + 1) * HEAD_V)
                ks = slice(h * HEAD_K, (h + 1) * HEAD_K)
                o = o_ref[rows, vs]
                rh = r[:, vs]
                dyg = d_ref[rows, vs]
                rinv = lax.rsqrt(jnp.mean(o * o, axis=-1, keepdims=True) + RMS_EPS)
                sg = _sigmoid(rh)
                on = o * rinv
                dr = dyg * (on * gn[:, vs]) * (sg * (1.0 + rh * (1.0 - sg)))
                don = dyg * (rh * sg)
                part = jnp.sum(don * on, axis=0, keepdims=True)
                dgn_sum[h] = part if dgn_sum[h] is None else dgn_sum[h] + part
                t = don * gn[:, vs]
                do = rinv * t - o * (rinv * rinv * rinv) * jnp.mean(o * t, axis=-1, keepdims=True)
                dob = do.astype(BF16)
                vb = v[:, vs].astype(BF16)
                qdb, kib, keb = qd.astype(BF16), ki.astype(BF16), ke.astype(BF16)
                a = jnp.where(causal, _dot(qdb, kib, NT), 0.0)
                st = st_ref[c, h]
                dst = dstates[h]
                dstb = dst.astype(BF16)
                da = jnp.where(causal, _dot(dob, vb, NT), 0.0)
                dab = da.astype(BF16)
                dv = _dot(a.astype(BF16), dob, TN) + _dot(keb, dstb, NT)
                dqd = _dot(dab, kib, NN) + _dot(dob, st.astype(BF16), NN)
                dki = _dot(dab, qdb, TN)
                dke = _dot(vb, dstb, NN)
                ddec = jnp.sum(st * dst, axis=0, keepdims=True)
                dstates[h] = dec * dst + _dot(dob, qdb, TN)
                dq = dqd * eb * (HEAD_K ** -0.5)
                dk = dki * enb + dke * eend
                db_parts.append(dqd * qd - dki * ki - dke * ke)
                dbl_parts.append(jnp.sum(dke * ke, axis=0, keepdims=True) + dec * ddec)
                dp_ref[rows, ks] = dq.astype(BF16)
                dp_ref[rows, D_GLA_K + h * HEAD_K:D_GLA_K + (h + 1) * HEAD_K] = dk.astype(BF16)
                dp_ref[rows, 1024 + h * HEAD_V:1024 + (h + 1) * HEAD_V] = dv.astype(BF16)
                dp_ref[rows, 2048 + h * HEAD_V:2048 + (h + 1) * HEAD_V] = dr.astype(BF16)
            db_rows[c] = jnp.concatenate(db_parts, axis=1)
            dbl_rows[c] = jnp.broadcast_to(jnp.concatenate(dbl_parts, axis=1), (CHUNK, D_GLA_K))
        for h in range(GLA_HEADS):
            ds_ref[h] = dstates[h]
            dgn_ref[:, h * HEAD_V:(h + 1) * HEAD_V] += dgn_sum[h]
        db = jnp.concatenate(db_rows, axis=0)
        dlog = _dot(upper, db, NN, precision=lax.Precision.HIGHEST) + jnp.concatenate(dbl_rows, axis=0)
        dz = dlog * (1.0 / GATE_TAU) * (1.0 / (1.0 + jnp.exp(z)))
        dzb = dz.astype(BF16)
        dp_ref[:, 3072:3200] = _dot(dzb, wg_ref[...], NT).astype(BF16)
        dwg_ref[...] += _dot(zl.astype(BF16), dzb, TN)
        dgb_ref[...] += jnp.sum(dz, axis=0, keepdims=True)

    rev = lambda n: nst - 1 - n
    return pl.pallas_call(
        body,
        name="gla_bwd",
        grid=(nst,),
        in_specs=[pl.BlockSpec((STEP_ROWS, HALF_P), lambda n: (rev(n), 1)), _full((LANE, D_GLA_K)),
                  _full((1, D_GLA_K)), _full((1, D_GLA_V)),
                  pl.BlockSpec((STEP_ROWS, D_GLA_V), lambda n: (rev(n), 0)),
                  pl.BlockSpec((STEP_CHUNKS, GLA_HEADS, HEAD_V, HEAD_K), lambda n: (rev(n), 0, 0, 0)),
                  pl.BlockSpec((STEP_ROWS, D_GLA_V), lambda n: (rev(n), 1)), pl.BlockSpec(memory_space=pl.ANY)],
        out_specs=[pl.BlockSpec((STEP_ROWS, HALF_P), lambda n: (rev(n), 1)), _full((LANE, D_GLA_K)),
                   _full((1, D_GLA_K)), _full((1, D_GLA_V))],
        out_shape=[jax.ShapeDtypeStruct(dproj.shape, BF16), jax.ShapeDtypeStruct((LANE, D_GLA_K), F32),
                   jax.ShapeDtypeStruct((1, D_GLA_K), F32), jax.ShapeDtypeStruct((1, D_GLA_V), F32)],
        scratch_shapes=[pltpu.VMEM((GLA_HEADS, HEAD_V, HEAD_K), F32)],
        input_output_aliases={7: 0},
        compiler_params=_params(1),
    )(proj, wg128, gbias, gng, o_all, states, dyin, dproj)


def _proj_bwd_x(dproj, w_full, dh1, deps=()):
    T, P = dproj.shape
    D = w_full.shape[0]
    tm, tk = _tile(T, 512), _tile(P, 1280)

    def ep(acc_ref, ex, o, i, j):
        o[0][...] = DN_ALPHA * ex[0][...] + acc_ref[...]

    row = pl.BlockSpec((tm, D), lambda i, j, k: (i, 0))
    return _mm("proj_bwd_x", "nt", dproj, w_full, M=T, N=D, K=P, tm=tm, tn=D, tk=tk,
               outs=[(jax.ShapeDtypeStruct((T, D), F32), row)], extras=[(dh1, row)], epilogue=ep, deps=deps)[0]


def _place():
    x, y, c = lax.axis_index("x"), lax.axis_index("y"), lax.axis_index("c")
    chips = [(1 - x, y), (x, 1 - y), (1 - x, 1 - y)]
    return x, y, c, chips


def _rcopy(src, dst, ssem, rsem, dev):
    return pltpu.make_async_remote_copy(src_ref=src, dst_ref=dst, send_sem=ssem, recv_sem=rsem,
                                        device_id=dev, device_id_type=MESH)


def _all_gather(name, shards, deps=()):
    n = len(shards)

    def body(*refs):
        ins, outs = refs[:n], refs[n + len(deps):2 * n + len(deps)]
        ssem, rsem, lsem = refs[2 * n + len(deps):]
        x, y, c, chips = _place()
        me, sib = (x, y, c), (x, y, 1 - c)

        def slot(w, px, py, pc):
            return outs[w].at[4 * px + 2 * py + pc]

        started = []
        for w in range(n):
            lc = pltpu.make_async_copy(ins[w], slot(w, *me), lsem.at[w])
            lc.start()
            started.append(lc)
        sends = []
        for w in range(n):
            cp = _rcopy(ins[w], slot(w, *me), ssem.at[7 * w], rsem.at[7 * w], sib)
            cp.start()
            sends.append(cp)
            for jx, chip in enumerate(chips):
                cp = _rcopy(ins[w], slot(w, *me), ssem.at[7 * w + 1 + jx], rsem.at[7 * w + 1 + jx], (*chip, c))
                cp.start()
                sends.append(cp)
        for w in range(n):
            for jx, chip in enumerate(chips):
                blk = slot(w, *chip, c)
                _rcopy(blk, blk, ssem.at[7 * w + 1 + jx], rsem.at[7 * w + 1 + jx], me).wait_recv()
                cp = _rcopy(blk, blk, ssem.at[7 * w + 4 + jx], rsem.at[7 * w + 4 + jx], sib)
                cp.start()
                sends.append(cp)
        for w in range(n):
            blk = slot(w, x, y, 1 - c)
            _rcopy(blk, blk, ssem.at[7 * w], rsem.at[7 * w], me).wait_recv()
            for jx, chip in enumerate(chips):
                blk = slot(w, *chip, 1 - c)
                _rcopy(blk, blk, ssem.at[7 * w + 4 + jx], rsem.at[7 * w + 4 + jx], me).wait_recv()
        for cp in sends:
            cp.wait_send()
        for lc in started:
            lc.wait()

    return pl.pallas_call(
        body,
        name=name,
        in_specs=_hbm_specs(n + len(deps)),
        out_specs=_hbm_specs(n),
        out_shape=[jax.ShapeDtypeStruct((N_DEV,) + s.shape, s.dtype) for s in shards],
        scratch_shapes=[pltpu.SemaphoreType.DMA((7 * n,)), pltpu.SemaphoreType.DMA((7 * n,)),
                        pltpu.SemaphoreType.DMA((n,))],
    )(*shards, *deps)


HBM_SPEC = pl.BlockSpec(memory_space=pltpu.HBM)
SEM_SPEC = pl.BlockSpec(memory_space=pltpu.SEMAPHORE)
SIDE_EFFECT = pltpu.SideEffectType.DATAFLOW_SIDE_EFFECTING


def _cast_place(name, ids, w, deps=(), dtype=None):
    dtype = BF16 if dtype is None else dtype
    R, C = w.shape
    tr = _tile(R, 256)

    def body(ids_ref, w_ref, *rest):
        rest[len(deps)][...] = w_ref[...].astype(dtype)

    return pl.pallas_call(
        body,
        name=name,
        grid_spec=pltpu.PrefetchScalarGridSpec(
            num_scalar_prefetch=1,
            grid=(R // tr,),
            in_specs=[pl.BlockSpec((tr, C), lambda r, ids: (r, 0))] + _hbm_specs(len(deps)),
            out_specs=pl.BlockSpec((None, tr, C), lambda r, ids: (ids[0], r, 0)),
        ),
        out_shape=jax.ShapeDtypeStruct((N_DEV, R, C), dtype),
        compiler_params=_params(1),
    )(ids, w, *deps)


def _xfer_start(name, bufs, plan, n):
    nb = len(bufs)

    def body(*refs):
        ins = refs[:nb]
        ssem, rsem = refs[nb], refs[nb + 1]
        token = refs[2 * nb + 2]
        x, y, c, chips = _place()
        for k, (src, dst, dev, _) in enumerate(plan(ins, x, y, c, chips)):
            _rcopy(src, dst, ssem.at[k], rsem.at[k], dev).start()
        token[...] = jnp.zeros_like(token)

    res = pl.pallas_call(
        body,
        name=name,
        out_shape=(pltpu.SemaphoreType.DMA((n,)), pltpu.SemaphoreType.DMA((n,)),
                   *[pltpu.HBM(b.shape, b.dtype) for b in bufs], jax.ShapeDtypeStruct((8, LANE), F32)),
        in_specs=[HBM_SPEC] * nb,
        out_specs=(SEM_SPEC, SEM_SPEC, *[HBM_SPEC] * nb, pl.BlockSpec(memory_space=pltpu.VMEM)),
        input_output_aliases={i: 2 + i for i in range(nb)},
        compiler_params=pltpu.CompilerParams(has_side_effects=SIDE_EFFECT),
    )(*[pltpu.with_memory_space_constraint(b, pltpu.HBM) for b in bufs])
    return dict(sems=res[:2], bufs=list(res[2:2 + nb]), token=res[2 + nb], plan=plan, n=n)


def _xfer_wait(name, started, after):
    bufs, plan = started["bufs"], started["plan"]
    nb = len(bufs)

    def body(*refs):
        ins = refs[:nb]
        ssem, rsem = refs[nb], refs[nb + 1]
        x, y, c, chips = _place()
        for k, (src, _, dev, land) in enumerate(plan(ins, x, y, c, chips)):
            cp = _rcopy(src, land, ssem.at[k], rsem.at[k], dev)
            cp.wait_send()
            cp.wait_recv()

    res = pl.pallas_call(
        body,
        name=name,
        out_shape=tuple(pltpu.HBM(b.shape, b.dtype) for b in bufs),
        in_specs=[HBM_SPEC] * nb + [SEM_SPEC, SEM_SPEC, pl.BlockSpec(memory_space=pl.ANY)],
        out_specs=tuple([HBM_SPEC] * nb),
        input_output_aliases={i: i for i in range(nb)},
        compiler_params=pltpu.CompilerParams(has_side_effects=SIDE_EFFECT),
    )(*bufs, *started["sems"], after)
    return list(res)


def _plan_gather_chips(refs, x, y, c, chips):
    plan = []
    for land in refs:
        mine = land.at[4 * x + 2 * y + c]
        plan.append((mine, mine, (x, y, 1 - c), land.at[4 * x + 2 * y + (1 - c)]))
        for px, py in chips:
            plan.append((mine, mine, (px, py, c), land.at[4 * px + 2 * py + c]))
    return plan


def _plan_gather_pass(refs, x, y, c, chips):
    return [(land.at[4 * px + 2 * py + c], land.at[4 * px + 2 * py + c], (x, y, 1 - c),
             land.at[4 * px + 2 * py + (1 - c)]) for land in refs for px, py in chips]


def _plan_reduce_core(refs, x, y, c, chips):
    grad, recv = refs
    return [(grad.at[2 * q + (1 - c)], recv.at[q], (x, y, 1 - c), recv.at[q]) for q in range(N_CHIP)]


def _plan_reduce_chips(refs, x, y, c, chips):
    part, land = refs
    return [(part.at[2 * px + py], land.at[2 * x + y], (px, py, c), land.at[2 * px + py]) for px, py in chips]


def _chip_sums(name, ids, grad, recv):
    _, R, C = grad.shape
    tr = _tile(R, 256)

    def body(ids_ref, g_ref, r_ref, o_ref):
        o_ref[...] = (g_ref[...] + r_ref[...]).astype(BF16)

    return pl.pallas_call(
        body,
        name=name,
        grid_spec=pltpu.PrefetchScalarGridSpec(
            num_scalar_prefetch=1,
            grid=(N_CHIP - 1, R // tr),
            in_specs=[pl.BlockSpec((None, tr, C), lambda q, r, ids: (2 * ids[3 + q] + ids[2], r, 0)),
                      pl.BlockSpec((None, tr, C), lambda q, r, ids: (ids[3 + q], r, 0))],
            out_specs=pl.BlockSpec((None, tr, C), lambda q, r, ids: (ids[3 + q], r, 0)),
        ),
        out_shape=jax.ShapeDtypeStruct((N_CHIP, R, C), BF16),
        compiler_params=_params(2),
    )(ids, grad, recv)


def _adamw(w, g, m, v):
    m = ADAM_B1 * m + (1.0 - ADAM_B1) * g
    v = ADAM_B2 * v + (1.0 - ADAM_B2) * (g * g)
    m_hat = m / (1.0 - ADAM_B1 ** ADAM_STEP)
    v_hat = v / (1.0 - ADAM_B2 ** ADAM_STEP)
    delta = -ADAM_LR * (m_hat / (jnp.sqrt(v_hat) + ADAM_EPS) + ADAM_WD * w)
    return delta, m, v


def _reduce_adamw(name, ids, grad, recv, landed, w, m, v):
    _, R, C = grad.shape
    tr = _tile(R, 128)

    def body(ids_ref, g_ref, r_ref, l1_ref, l2_ref, l3_ref, w_ref, m_ref, v_ref, go_ref, do_ref, mo_ref, vo_ref):
        g = g_ref[...] + r_ref[...]
        g = g + l1_ref[...].astype(F32)
        g = g + l2_ref[...].astype(F32)
        g = g + l3_ref[...].astype(F32)
        delta, mn, vn = _adamw(w_ref[...], g, m_ref[...], v_ref[...])
        go_ref[...] = g
        do_ref[...] = delta
        mo_ref[...] = mn
        vo_ref[...] = vn

    def pick(k):
        return pl.BlockSpec((None, tr, C), lambda r, ids: (ids[k], r, 0))

    flat = pl.BlockSpec((tr, C), lambda r, ids: (r, 0))
    shp = jax.ShapeDtypeStruct((R, C), F32)
    return pl.pallas_call(
        body,
        name=name,
        grid_spec=pltpu.PrefetchScalarGridSpec(
            num_scalar_prefetch=1,
            grid=(R // tr,),
            in_specs=[pick(0), pick(1), pick(3), pick(4), pick(5), flat, flat, flat],
            out_specs=[flat, flat, flat, flat],
        ),
        out_shape=[shp, shp, shp, shp],
        compiler_params=_params(1),
    )(ids, grad, recv, landed, landed, landed, w, m, v)


SMALL = ("conv_w", "conv_norm_g", "w_gate_up", "gate_bias", "gla_norm_g", "ln1_g", "ln1_b", "ln2_g", "ln2_b")
R_LOSS = 14


def _small_rows(D, conv_cols, gate_cols):
    nv = max(1, D // SP_COLS)
    assert nv <= 2, D
    return {"conv_w": (0, 3, conv_cols), "conv_norm_g": (3, 1, D_CONV), "gate_bias": (4, 1, D_GLA_K),
            "gla_norm_g": (5, 1, D_GLA_V), "ln1_g": (6, nv, D), "ln1_b": (8, nv, D), "ln2_g": (10, nv, D),
            "ln2_b": (12, nv, D), "w_gate_up": (16, GATE_RANK, gate_cols)}


def _put(o_ref, entry, val):
    row, n_rows, cols = entry
    if val.shape[0] == 1 and n_rows > 1:
        for r in range(n_rows):
            o_ref[row + r:row + r + 1, :] = val[:, r * SP_COLS:(r + 1) * SP_COLS]
    else:
        o_ref[row:row + n_rows, 0:cols] = val[0:n_rows, 0:cols]


def _take(g, entry):
    row, n_rows, cols = entry
    if cols > SP_COLS:
        return jnp.concatenate([g[row + r:row + r + 1, :] for r in range(n_rows)], axis=1)
    return g[row:row + n_rows, 0:cols]


def _make_pack(name, rows, pieces):
    names = list(pieces)

    def body(*refs):
        o_ref = refs[len(names)]
        o_ref[...] = jnp.zeros_like(o_ref)
        for nm, ref in zip(names, refs):
            if nm == "loss":
                o_ref[R_LOSS:R_LOSS + 1, 0:1] = jnp.sum(ref[...], axis=1, keepdims=True)
            else:
                _put(o_ref, rows[nm], ref[...])

    arrs = [pieces[nm] for nm in names]
    return pl.pallas_call(
        body,
        name=name,
        grid=(1,),
        in_specs=[_full(a.shape) for a in arrs],
        out_specs=_full((SP_ROWS, SP_COLS)),
        out_shape=jax.ShapeDtypeStruct((SP_ROWS, SP_COLS), F32),
        compiler_params=_params(1),
    )(*arrs)


def _small_adamw(packs, rows, w, m, v):
    names = list(SMALL)
    n = len(names)

    def body(p_ref, *refs):
        ins, outs = refs[:3 * n], refs[3 * n:]
        g = p_ref[0]
        for dvc in range(1, N_DEV):
            g = g + p_ref[dvc]
        for i, nm in enumerate(names):
            gp = _take(g, rows[nm])
            delta, mn, vn = _adamw(ins[i][...], gp, ins[n + i][...], ins[2 * n + i][...])
            for kind, val in enumerate((gp, delta, mn, vn)):
                outs[kind * n + i][...] = val
        outs[4 * n][...] = g[R_LOSS:R_LOSS + 1, 0:1]

    arrs = [w[nm] for nm in names] + [m[nm] for nm in names] + [v[nm] for nm in names]
    shapes = [jax.ShapeDtypeStruct(w[nm].shape, F32) for nm in names] * 4 + [jax.ShapeDtypeStruct((1, 1), F32)]
    res = pl.pallas_call(
        body,
        name="small_adamw",
        grid=(1,),
        in_specs=[_full(packs.shape)] + [_full(a.shape) for a in arrs],
        out_specs=[_full(sh.shape) for sh in shapes],
        out_shape=shapes,
        compiler_params=_params(1),
    )(packs, *arrs)
    return [dict(zip(names, res[k * n:(k + 1) * n])) for k in range(4)], res[4 * n]


def _w_in_pieces():
    cs = D_IN_PROJ // N_DEV
    pieces = []
    for d in range(N_DEV):
        lo, hi = d * cs, (d + 1) * cs
        if hi <= CONV_COLS:
            pieces.append((d, 0, cs, lo))
        elif lo >= CONV_COLS:
            pieces.append((d, 0, cs, lo - CONV_COLS + HALF_P))
        else:
            pieces.append((d, 0, CONV_COLS - lo, lo))
            pieces.append((d, CONV_COLS - lo, cs, HALF_P))
    return pieces


def _w_in_full(gathered):
    nb, D, cs = gathered.shape
    tr = _tile(D, 256)

    def body(g_ref, o_ref):
        o_ref[:, CONV_COLS:HALF_P] = jnp.zeros((tr, HALF_P - CONV_COLS), o_ref.dtype)
        o_ref[:, HALF_P + GLA_COLS:P_INT] = jnp.zeros((tr, HALF_P - GLA_COLS), o_ref.dtype)
        for d, a, b, dst in _w_in_pieces():
            o_ref[:, dst:dst + (b - a)] = g_ref[d, :, a:b]

    return pl.pallas_call(
        body,
        name="w_in_full",
        grid=(D // tr,),
        in_specs=[pl.BlockSpec((nb, tr, cs), lambda r: (0, r, 0))],
        out_specs=pl.BlockSpec((tr, P_INT), lambda r: (r, 0)),
        out_shape=jax.ShapeDtypeStruct((D, P_INT), gathered.dtype),
        compiler_params=_params(1),
    )(gathered)


def _w_in_blocks(dw):
    D = dw.shape[0]
    cs = D_IN_PROJ // N_DEV
    tr = _tile(D, 256)

    def body(w_ref, o_ref):
        for d, a, b, src in _w_in_pieces():
            o_ref[d, :, a:b] = w_ref[:, src:src + (b - a)]

    return pl.pallas_call(
        body,
        name="w_in_blocks",
        grid=(D // tr,),
        in_specs=[pl.BlockSpec((tr, P_INT), lambda r: (r, 0))],
        out_specs=pl.BlockSpec((N_DEV, tr, cs), lambda r: (0, r, 0)),
        out_shape=jax.ShapeDtypeStruct((N_DEV, D, cs), dw.dtype),
        compiler_params=_params(1),
    )(dw)


BIG = ("w_in", "w_out", "w_ff_up", "w_ff_down")
ORDER = ("w_in", "conv_w", "conv_norm_g", "w_gate_up", "gate_bias", "gla_norm_g", "w_out", "ln1_g", "ln1_b",
         "w_ff_up", "w_ff_down", "ln2_g", "ln2_b")


def kernel(x, w_in, conv_w, conv_norm_g, w_gate_up, gate_bias, gla_norm_g, w_out, ln1_g, ln1_b, w_ff_up, w_ff_down, ln2_g, ln2_b, loss_target, m_w_in, m_conv_w, m_conv_norm_g, m_w_gate_up, m_gate_bias, m_gla_norm_g, m_w_out, m_ln1_g, m_ln1_b, m_w_ff_up, m_w_ff_down, m_ln2_g, m_ln2_b, v_w_in, v_conv_w, v_conv_norm_g, v_w_gate_up, v_gate_bias, v_gla_norm_g, v_w_out, v_ln1_g, v_ln1_b, v_w_ff_up, v_w_ff_down, v_ln2_g, v_ln2_b):
    T, D = x.shape[1], x.shape[2]
    xs, target = x[0], loss_target[0]
    xi, yi, ci = lax.axis_index("x"), lax.axis_index("y"), lax.axis_index("c")
    chip = 2 * xi + yi
    dev = 2 * chip + ci
    others = [jnp.where(chip <= q, q + 1, q) for q in range(N_CHIP - 1)]
    ids = jnp.stack([dev, chip, ci] + others).astype(jnp.int32)
    conv_cols, gate_cols = conv_w.shape[2], w_gate_up.shape[2]

    def gather(nm, lands):
        return _xfer_start("gather_chips_" + nm, lands, _plan_gather_chips, 4 * len(lands))

    def pass_on(nm, started, after):
        lands = _xfer_wait("gather_chips_wait_" + nm, started, after)
        return _xfer_start("gather_pass_" + nm, lands, _plan_gather_pass, 3 * len(lands))

    def landed(nm, started, after):
        return _xfer_wait("gather_pass_wait_" + nm, started, after)

    rows = _small_rows(D, conv_cols, gate_cols)
    fwd_pack = _make_pack("pack_fwd", rows, {"conv_w": conv_w[0], "w_gate_up": w_gate_up[0]})
    ga_in = gather("w_in", [_cast_place("cast_place_w_in", ids, w_in[0]),
                            _cast_place("cast_place_pack", ids, fwd_pack, dtype=F32)])
    ga, dep = [], ga_in["token"]
    m_in, v_in = m_w_in[0], v_w_in[0]
    for nm, w in zip(BIG[1:], (w_out, w_ff_up, w_ff_down)):
        deps = [dep, m_in, v_in] if nm == "w_ff_down" else [dep]
        ga.append(gather(nm, [_cast_place("cast_place_" + nm, ids, w[0], deps)]))
        dep = ga[-1]["token"]
    xb = _cast_bf16(xs, [dep])
    gp_in = pass_on("w_in", ga_in, xb)
    g_in, g_pack = landed("w_in", gp_in, gp_in["token"])
    w_full = _w_in_full(g_in)
    r_cw, r_gw = rows["conv_w"][0], rows["w_gate_up"][0]
    conv_w_full = g_pack[:, r_cw:r_cw + 3, :conv_cols].transpose(1, 0, 2).reshape(3, -1)
    gate_w_full = g_pack[:, r_gw:r_gw + GATE_RANK, :gate_cols].transpose(1, 0, 2).reshape(GATE_RANK, -1)
    conv_w8 = jnp.pad(conv_w_full, ((0, 5), (0, 0)))
    wg128 = jnp.pad(gate_w_full, ((0, LANE - GATE_RANK), (0, 0))).astype(BF16)
    proj = _proj_fwd(xb, w_full)
    yin = _conv_fwd(proj, conv_w8, conv_norm_g)
    gp_out = pass_on("w_out", ga[0], yin)
    o_all, states, yin = _gla_fwd(proj, wg128, gate_bias, gla_norm_g, yin, deps=[gp_out["token"]])
    w_out_full = landed("w_out", gp_out, o_all)[0].reshape(-1, D)
    gp_up = pass_on("w_ff_up", ga[1], o_all)
    xhat1, x1, rstd1 = _mix_ln1(yin, w_out_full, xs, ln1_g, ln1_b, deps=[gp_up["token"]])
    (w_up_blk,) = landed("w_ff_up", gp_up, x1)
    half = N_DEV // 2
    ra, h2 = _ff_up(x1, w_up_blk, 0, half)
    gp_down = pass_on("w_ff_down", ga[2], ra)
    ra, h2 = _ff_up(x1, w_up_blk, half, N_DEV - half, prev=(ra, h2), deps=[gp_down["token"]])
    w_down_full = landed("w_ff_down", gp_down, ra)[0].reshape(-1, D)
    dh3, dh3b, g_ln2_g, g_ln2_b, loss = _ff_down_loss(h2, w_down_full, xhat1, target, ln1_g, ln1_b, ln2_g, ln2_b)

    def to_core(nm, grad):
        recv = lax.empty((N_CHIP,) + grad.shape[1:], F32)
        return _xfer_start("reduce_core_" + nm, [grad, recv], _plan_reduce_core, N_CHIP)

    def to_chips(nm, started, after):
        grad, recv = _xfer_wait("reduce_core_wait_" + nm, started, after)
        part = _chip_sums("chip_sums_" + nm, ids, grad, recv)
        land = lax.empty(part.shape, BF16)
        return grad, recv, _xfer_start("reduce_chips_" + nm, [part, land], _plan_reduce_chips, N_CHIP - 1)

    da = _ff_down_bwd_act(dh3b, w_down_full, ra)
    gw_down = _grad_w("grad_w_down", h2, dh3b).reshape(N_DEV, -1, D)
    rc_down = to_core("w_ff_down", gw_down)
    gw_up = _grad_w_up_blk(x1, da, N_DEV, deps=[rc_down["token"]])
    gw_down, rv_down, rs_down = to_chips("w_ff_down", rc_down, gw_up)
    rc_up = to_core("w_ff_up", gw_up)
    dh1, dh1b, g_ln1_g, g_ln1_b = _ff_up_bwd_ln1(da, w_up_blk, dh3, xhat1, rstd1, ln1_g,
                                                 deps=[rs_down["token"], rc_up["token"]])
    gw_up, rv_up, rs_up = to_chips("w_ff_up", rc_up, dh1b)
    dyin = _mix_bwd(dh1b, w_out_full, deps=[rs_up["token"]])
    gw_out = _grad_w("grad_w_out", yin, dh1b).reshape(N_DEV, -1, D)
    rc_out = to_core("w_out", gw_out)
    dproj, g_conv_w, g_conv_g = _conv_bwd(proj, dyin, conv_w8, conv_norm_g, deps=[rc_out["token"]])
    dproj, g_gate_w, g_gate_b, g_gla_g = _gla_bwd(proj, wg128, gate_bias, gla_norm_g, o_all, states, dyin, dproj)
    gw_out, rv_out, rs_out = to_chips("w_out", rc_out, dproj)
    gw_in = _w_in_blocks(_grad_w("grad_w_in", xb, dproj, tn_pref=1280, tk_pref=2048, deps=[rs_out["token"]]))
    rc_in = to_core("w_in", gw_in)

    big = {}

    def finish(nm, grad, recv, started, w, m, v, after):
        _, land = _xfer_wait("reduce_chips_wait_" + nm, started, after)
        res = _reduce_adamw("adamw_" + nm, ids, grad, recv, land, w[0], m[0], v[0])
        big[nm] = [a[None] for a in res]
        return res[0]

    done = finish("w_ff_down", gw_down, rv_down, rs_down, w_ff_down, m_w_ff_down, v_w_ff_down, rc_in["token"])
    done = finish("w_ff_up", gw_up, rv_up, rs_up, w_ff_up, m_w_ff_up, v_w_ff_up, done)
    full_rows = _small_rows(D, D_CONV, D_GLA_K)
    pack = _make_pack("pack_grads", full_rows, {
        "conv_w": g_conv_w, "conv_norm_g": g_conv_g, "gate_bias": g_gate_b, "gla_norm_g": g_gla_g, "ln1_g": g_ln1_g,
        "ln1_b": g_ln1_b, "ln2_g": g_ln2_g, "ln2_b": g_ln2_b, "loss": loss, "w_gate_up": g_gate_w})
    (packs,) = _all_gather("gather_small_grads", [pack], deps=[done])
    gw_in, rv_in, rs_in = to_chips("w_in", rc_in, packs)
    done = finish("w_out", gw_out, rv_out, rs_out, w_out, m_w_out, v_w_out, rs_in["token"])
    grad_x = _proj_bwd_x(dproj, w_full, dh1, deps=[done])
    finish("w_in", gw_in, rv_in, rs_in, w_in, (m_in,), (v_in,), grad_x)

    def own_cols(row, n_rows, width):
        cut = lax.dynamic_slice(packs, (0, row, dev * width), (N_DEV, n_rows, width))
        return jnp.pad(cut, ((0, 0), (0, 0), (0, SP_COLS - width)))

    packs_own = jnp.concatenate([own_cols(r_cw, 3, conv_cols), packs[:, r_cw + 3:r_gw],
                                 own_cols(r_gw, GATE_RANK, gate_cols)], axis=1)
    as2d = lambda a: a[0] if a.ndim == 3 else a
    w_s = dict(zip(SMALL, map(as2d, (conv_w, conv_norm_g, w_gate_up, gate_bias, gla_norm_g, ln1_g, ln1_b, ln2_g, ln2_b))))
    m_s = dict(zip(SMALL, map(as2d, (m_conv_w, m_conv_norm_g, m_w_gate_up, m_gate_bias, m_gla_norm_g, m_ln1_g,
                                     m_ln1_b, m_ln2_g, m_ln2_b))))
    v_s = dict(zip(SMALL, map(as2d, (v_conv_w, v_conv_norm_g, v_w_gate_up, v_gate_bias, v_gla_norm_g, v_ln1_g,
                                     v_ln1_b, v_ln2_g, v_ln2_b))))
    small, loss_sum = _small_adamw(packs_own, rows, w_s, m_s, v_s)

    def leaf(kind, name):
        if name in BIG:
            return big[name][kind]
        a = small[kind][name]
        return a[None] if name in ("conv_w", "w_gate_up") else a

    out = [loss_sum[0, 0], grad_x[None]]
    for kind in range(4):
        out += [leaf(kind, nm) for nm in ORDER]
    return tuple(out)
```

```python
import jax
import jax.numpy as jnp
from jax import lax
from jax.experimental import pallas as pl
from jax.experimental.pallas import tpu as pltpu

F32 = jnp.float32
BF16 = jnp.bfloat16

D_CONV = 1024
CONV_GROUPS = 8
GLA_HEADS = 4
HEAD_K = 128
HEAD_V = 256
D_GLA_K = 512
D_GLA_V = 1024
GATE_RANK = 16
GATE_TAU = 16.0
CHUNK = 64
LN_EPS = 1e-5
RMS_EPS = 1e-6
DN_ALPHA = 2.0 ** 0.25
D_IN_PROJ = 6160
ADAM_LR = 0.001
ADAM_B1 = 0.9
ADAM_B2 = 0.999
ADAM_EPS = 1e-08
ADAM_WD = 0.01
ADAM_STEP = 10

N_DEV = 8
N_CHIP = 4
LANE = 128
HALF_P = 3200
P_INT = 2 * HALF_P
CONV_COLS = 3 * D_CONV
GLA_COLS = D_IN_PROJ - CONV_COLS
SP_ROWS = 32
SP_COLS = 1024
VMEM_LIMIT = 56 * 1024 * 1024

NN = ((1,), (0,))
NT = ((1,), (1,))
TN = ((0,), (0,))
MESH = pl.DeviceIdType.MESH


def _dot(a, b, dims, precision=None):
    return lax.dot_general(a, b, (dims, ((), ())), preferred_element_type=F32, precision=precision)


def _tile(n, pref):
    if n <= pref:
        return n
    t = (pref // LANE) * LANE
    while t > 0 and n % t:
        t -= LANE
    assert t > 0, (n, pref)
    return t


def _params(n_axes):
    return pltpu.CompilerParams(dimension_semantics=("arbitrary",) * n_axes, vmem_limit_bytes=VMEM_LIMIT)


def _full(shape):
    nd = len(shape)
    return pl.BlockSpec(shape, lambda *_: (0,) * nd)


def _hbm_specs(n):
    return [pl.BlockSpec(memory_space=pl.ANY)] * n


def _mm(name, mode, a, b, *, M, N, K, tm, tn, tk, outs, epilogue, extras=(), a_fn=None, a_spec=None, b_spec=None,
        deps=()):
    ni, nj, nk = M // tm, N // tn, K // tk
    assert ni * tm == M and nj * tn == N and nk * tk == K, (name, M, N, K, tm, tn, tk)
    if a_spec is None:
        a_spec = (pl.BlockSpec((tk, tm), lambda i, j, k: (k, i)) if mode == "tn"
                  else pl.BlockSpec((tm, tk), lambda i, j, k: (i, k)))
    if b_spec is None:
        b_spec = (pl.BlockSpec((tn, tk), lambda i, j, k: (j, k)) if mode == "nt"
                  else pl.BlockSpec((tk, tn), lambda i, j, k: (k, j)))
    dims = {"nn": NN, "nt": NT, "tn": TN}[mode]
    n_ex, n_out, n_dep = len(extras), len(outs), len(deps)

    def body(*refs):
        a_ref, b_ref = refs[0], refs[1]
        ex = refs[2:2 + n_ex]
        o = refs[2 + n_ex + n_dep:2 + n_ex + n_dep + n_out]
        acc_ref = refs[2 + n_ex + n_dep + n_out]
        i, j, k = pl.program_id(0), pl.program_id(1), pl.program_id(2)
        if nk > 1:
            @pl.when(k == 0)
            def _():
                acc_ref[...] = jnp.zeros_like(acc_ref)

        av = a_ref[...]
        if a_fn is not None:
            av = a_fn(av)
        part = _dot(av, b_ref[...], dims)
        if nk == 1 and epilogue is None:
            o[0][...] = part.astype(o[0].dtype)
        elif nk == 1:
            acc_ref[...] = part
            epilogue(acc_ref, ex, o, i, j)
        else:
            acc_ref[...] += part

            @pl.when(k == nk - 1)
            def _():
                if epilogue is None:
                    o[0][...] = acc_ref[...].astype(o[0].dtype)
                else:
                    epilogue(acc_ref, ex, o, i, j)

    return pl.pallas_call(
        body,
        name=name,
        grid=(ni, nj, nk),
        in_specs=[a_spec, b_spec] + [s for _, s in extras] + _hbm_specs(n_dep),
        out_specs=[s for _, s in outs],
        out_shape=[s for s, _ in outs],
        scratch_shapes=[pltpu.VMEM((8, LANE) if nk == 1 and epilogue is None else (tm, tn), F32)],
        compiler_params=_params(3),
    )(a, b, *[x for x, _ in extras], *deps)


def _mm_rows(name, mode, a, b, *, M, N, K, tm, tk, row_ins, vec_ins, row_outs, stat_outs, chunk_fn,
             b_spec=None, deps=()):
    ni, nk = M // tm, K // tk
    rc = tm // nk
    assert ni * tm == M and nk * tk == K and rc * nk == tm and rc % 16 == 0, (name, M, K, tm, tk)
    dims = {"nn": NN, "nt": NT}[mode]
    last = ni - 1

    def kk(i, k):
        return jnp.where(i < ni, k, nk - 1)

    a_spec = pl.BlockSpec((tm, tk), lambda i, k: (jnp.minimum(i, last), kk(i, k)))
    if b_spec is None:
        b_spec = (pl.BlockSpec((N, tk), lambda i, k: (0, kk(i, k))) if mode == "nt"
                  else pl.BlockSpec((tk, N), lambda i, k: (kk(i, k), 0)))
    prev_rows = lambda i, k: (jnp.maximum((i - 1) * nk + k, 0), 0)
    n_ri, n_vi, n_ro, n_so, n_dep = len(row_ins), len(vec_ins), len(row_outs), len(stat_outs), len(deps)

    def body(*refs):
        a_ref, b_ref = refs[0], refs[1]
        pos = 2
        ri = refs[pos:pos + n_ri]; pos += n_ri
        vi = refs[pos:pos + n_vi]; pos += n_vi + n_dep
        ro = refs[pos:pos + n_ro]; pos += n_ro
        so = refs[pos:pos + n_so]; pos += n_so
        accs = refs[pos:pos + 2]
        i, k = pl.program_id(0), pl.program_id(1)

        @pl.when((i == 0) & (k == 0))
        def _():
            accs[0][...] = jnp.zeros_like(accs[0])
            accs[1][...] = jnp.zeros_like(accs[1])
            for st in so:
                st[...] = jnp.zeros_like(st)

        def finish_rows(prev_ref):
            rows = pl.ds(pl.multiple_of(k * rc, rc), rc)
            done = prev_ref[rows, :]
            prev_ref[rows, :] = jnp.zeros((rc, N), F32)
            chunk_fn(done, i > 0, ri, vi, ro, so)

        def accumulate(acc_ref, after_ref):
            rp = tm // ROW_PARTS
            bv = b_ref[...]
            for part in range(ROW_PARTS):
                av = a_ref[part * rp:(part + 1) * rp, :]
                if part == 1:
                    tail = after_ref[rc - 16:rc, 0:LANE].astype(F32)
                    sixteen = jnp.uint32(16)
                    zero = lax.bitcast_convert_type(tail, jnp.uint32)
                    zero = lax.shift_right_logical(lax.shift_right_logical(zero, sixteen), sixteen)
                    av = av + jnp.tile(zero.astype(F32).astype(av.dtype), (rp // 16, tk // LANE))
                acc_ref[part * rp:(part + 1) * rp, :] += _dot(av, bv, dims)

        for parity in (0, 1):
            @pl.when((i < ni) & (lax.rem(i, 2) == parity))
            def _(parity=parity):
                finish_rows(accs[1 - parity])
                accumulate(accs[parity], ro[0])

        @pl.when(i == ni)
        def _():
            finish_rows(accs[last % 2])

    row_spec = lambda arr: pl.BlockSpec((rc, arr.shape[1]), prev_rows)
    return pl.pallas_call(
        body,
        name=name,
        grid=(ni + 1, nk),
        in_specs=[a_spec, b_spec] + [row_spec(x) for x in row_ins] + [_full(x.shape) for x in vec_ins]
        + _hbm_specs(n_dep),
        out_specs=[row_spec(s) for s in row_outs] + [_full(s.shape) for s in stat_outs],
        out_shape=list(row_outs) + list(stat_outs),
        scratch_shapes=[pltpu.VMEM((tm, N), F32), pltpu.VMEM((tm, N), F32)],
        compiler_params=_params(2),
    )(a, b, *row_ins, *vec_ins, *deps)


ROW_PARTS = 2
SUB_ROWS = 16


def _by_sub_rows(n_rows, fn):
    sums = None
    for r0 in range(0, n_rows, SUB_ROWS):
        part = fn(slice(r0, r0 + SUB_ROWS))
        if part:
            sums = part if sums is None else tuple(x + y for x, y in zip(sums, part))
    return sums


def _to_bf16(v):
    return v.astype(BF16)


def _ln_bwd(dy, xhat, rstd, g):
    dxh = dy * g
    m1 = jnp.mean(dxh, axis=-1, keepdims=True)
    m2 = jnp.mean(dxh * xhat, axis=-1, keepdims=True)
    return rstd * (dxh - m1 - xhat * m2)


def _ln_fwd(h):
    mu = jnp.mean(h, axis=-1, keepdims=True)
    xc = h - mu
    var = jnp.mean(xc * xc, axis=-1, keepdims=True)
    rstd = lax.rsqrt(var + LN_EPS)
    return xc * rstd, rstd


def _proj_fwd(x, w_full, deps=()):
    T, D = x.shape
    P = w_full.shape[1]
    tm, tn = _tile(T, 1024), _tile(P, 1280)
    return _mm("proj_fwd", "nn", x, w_full, M=T, N=P, K=D, tm=tm, tn=tn, tk=D,
               outs=[(jax.ShapeDtypeStruct((T, P), F32), pl.BlockSpec((tm, tn), lambda i, j, k: (i, j)))],
               epilogue=None, deps=deps)[0]


def _cast_bf16(x, deps=()):
    T, D = x.shape
    tm = _tile(T, 512)

    def body(x_ref, *rest):
        rest[len(deps)][...] = x_ref[...].astype(BF16)

    return pl.pallas_call(
        body,
        name="cast_x",
        grid=(T // tm,),
        in_specs=[pl.BlockSpec((tm, D), lambda i: (i, 0))] + _hbm_specs(len(deps)),
        out_specs=pl.BlockSpec((tm, D), lambda i: (i, 0)),
        out_shape=jax.ShapeDtypeStruct((T, D), BF16),
        compiler_params=_params(1),
    )(x, *deps)


def _conv_shift(h, hp):
    row = lax.broadcasted_iota(jnp.int32, h.shape, 0)
    hm1 = hp[7:8, :]
    hm2 = hp[6:7, :]
    h1 = jnp.where(row == 0, hm1, pltpu.roll(h, 1, 0))
    h2 = jnp.where(row == 0, hm2, jnp.where(row == 1, hm1, pltpu.roll(h, 2, 0)))
    return h1, h2


def _conv_fwd(proj, conv_w8, conv_g):
    T = proj.shape[0]
    tt = _tile(T, 256)
    nt = T // tt
    t8 = tt // 8

    def body(b_ref, c_ref, u_ref, cp_ref, up_ref, w_ref, g_ref, yin_ref):
        i = pl.program_id(0)
        h = c_ref[...] * u_ref[...]
        hp = jnp.where(i > 0, cp_ref[...] * up_ref[...], 0.0)
        h1, h2 = _conv_shift(h, hp)
        w = w_ref[...]
        y = w[0:1, :] * h2 + w[1:2, :] * h1 + w[2:3, :] * h
        p = b_ref[...] * y
        parts = []
        for gi in range(CONV_GROUPS):
            pg = p[:, gi * LANE:(gi + 1) * LANE]
            r = lax.rsqrt(jnp.mean(pg * pg, axis=-1, keepdims=True) + RMS_EPS)
            parts.append(pg * r)
        yn = jnp.concatenate(parts, axis=1) * g_ref[...]
        yin_ref[...] = yn.astype(BF16)

    def col(cidx):
        return pl.BlockSpec((tt, D_CONV), lambda i: (i, cidx))

    def prev(cidx):
        return pl.BlockSpec((8, D_CONV), lambda i: (jnp.maximum(i * t8 - 1, 0), cidx))

    return pl.pallas_call(
        body,
        name="conv_fwd",
        grid=(nt,),
        in_specs=[col(0), col(1), col(2), prev(1), prev(2), _full((8, D_CONV)), _full((1, D_CONV))],
        out_specs=pl.BlockSpec((tt, D_CONV), lambda i: (i, 0)),
        out_shape=jax.ShapeDtypeStruct((T, 2 * D_CONV), BF16),
        compiler_params=_params(1),
    )(proj, proj, proj, proj, proj, conv_w8, conv_g)


def _log_sigmoid(z):
    return jnp.minimum(z, 0.0) - jnp.log(1.0 + jnp.exp(-jnp.abs(z)))


STEP_CHUNKS = 4
STEP_ROWS = STEP_CHUNKS * CHUNK


def _gla_step_terms(blk, wg_ref, gb_ref):
    zl = blk[:, 3072:3200]
    z = _dot(zl.astype(BF16), wg_ref[...], NN) + gb_ref[...]
    log_a = _log_sigmoid(z) * (1.0 / GATE_TAU)
    ri = lax.broadcasted_iota(jnp.int32, (STEP_ROWS, STEP_ROWS), 0)
    ci = lax.broadcasted_iota(jnp.int32, (STEP_ROWS, STEP_ROWS), 1)
    same = (ri // CHUNK) == (ci // CHUNK)
    lower = (same & (ri >= ci)).astype(F32)
    bcum = _dot(lower, log_a, NN, precision=lax.Precision.HIGHEST)
    return zl, z, bcum, same


def _causal():
    return (lax.broadcasted_iota(jnp.int32, (CHUNK, CHUNK), 0) >= lax.broadcasted_iota(jnp.int32, (CHUNK, CHUNK), 1))


def _gla_head_terms(q, k, bcum, h):
    sl = slice(h * HEAD_K, (h + 1) * HEAD_K)
    bh = bcum[:, sl]
    bl = bh[CHUNK - 1:CHUNK, :]
    eb = jnp.exp(bh)
    enb = jnp.exp(-bh)
    eend = jnp.exp(bl - bh)
    dec = jnp.exp(bl)
    qd = q[:, sl] * (HEAD_K ** -0.5) * eb
    ki = k[:, sl] * enb
    ke = k[:, sl] * eend
    return eb, enb, eend, dec, qd, ki, ke


def _sigmoid(x):
    return 1.0 / (1.0 + jnp.exp(-x))


def _gla_fwd(proj, wg128, gbias, gng, yin, deps=()):
    T = proj.shape[0]
    nch = T // CHUNK
    nst = T // STEP_ROWS

    def body(p_ref, wg_ref, gb_ref, gn_ref, yin_in_ref, *rest):
        o_ref, st_ref, yin_ref, s_ref = rest[len(deps):]
        n = pl.program_id(0)

        @pl.when(n == 0)
        def _():
            s_ref[...] = jnp.zeros_like(s_ref)

        blk = p_ref[...]
        _, _, bcum_all, _ = _gla_step_terms(blk, wg_ref, gb_ref)
        causal = _causal()
        gn = gn_ref[...]
        states = [s_ref[h] for h in range(GLA_HEADS)]
        for c in range(STEP_CHUNKS):
            rows = slice(c * CHUNK, (c + 1) * CHUNK)
            q, k = blk[rows, 0:512], blk[rows, 512:1024]
            v, r = blk[rows, 1024:2048], blk[rows, 2048:3072]
            bcum = bcum_all[rows, :]
            for h in range(GLA_HEADS):
                _, _, _, dec, qd, ki, ke = _gla_head_terms(q, k, bcum, h)
                vs = slice(h * HEAD_V, (h + 1) * HEAD_V)
                vb = v[:, vs].astype(BF16)
                qdb = qd.astype(BF16)
                a = jnp.where(causal, _dot(qdb, ki.astype(BF16), NT), 0.0)
                st = states[h]
                o = _dot(a.astype(BF16), vb, NN) + _dot(qdb, st.astype(BF16), NT)
                st_ref[c, h] = st
                states[h] = dec * st + _dot(vb, ke.astype(BF16), TN)
                o_ref[rows, vs] = o
                rinv = lax.rsqrt(jnp.mean(o * o, axis=-1, keepdims=True) + RMS_EPS)
                rh = r[:, vs]
                yin_ref[rows, vs] = (o * rinv * gn[:, vs] * (rh * _sigmoid(rh))).astype(BF16)
        for h in range(GLA_HEADS):
            s_ref[h] = states[h]

    return pl.pallas_call(
        body,
        name="gla_fwd",
        grid=(nst,),
        in_specs=[pl.BlockSpec((STEP_ROWS, HALF_P), lambda n: (n, 1)), _full((LANE, D_GLA_K)), _full((1, D_GLA_K)),
                  _full((1, D_GLA_V)), pl.BlockSpec(memory_space=pl.ANY)] + _hbm_specs(len(deps)),
        out_specs=[pl.BlockSpec((STEP_ROWS, D_GLA_V), lambda n: (n, 0)),
                   pl.BlockSpec((STEP_CHUNKS, GLA_HEADS, HEAD_V, HEAD_K), lambda n: (n, 0, 0, 0)),
                   pl.BlockSpec((STEP_ROWS, D_GLA_V), lambda n: (n, 1))],
        out_shape=[jax.ShapeDtypeStruct((T, D_GLA_V), F32),
                   jax.ShapeDtypeStruct((nch, GLA_HEADS, HEAD_V, HEAD_K), F32),
                   jax.ShapeDtypeStruct(yin.shape, BF16)],
        scratch_shapes=[pltpu.VMEM((GLA_HEADS, HEAD_V, HEAD_K), F32)],
        input_output_aliases={4: 2},
        compiler_params=_params(1),
    )(proj, wg128, gbias, gng, yin, *deps)


def _mix_ln1(yin, w_out, x, ln_g, ln_b, deps=()):
    T, D = x.shape
    KY = yin.shape[1]
    tm = _tile(T, 1024)

    def chunk(acc, valid, ri, vi, ro, so):
        g, b = vi[0][...], vi[1][...]

        def sub(rows):
            xhat, rstd = _ln_fwd(DN_ALPHA * ri[0][rows, :] + acc[rows, :])
            ro[0][rows, :] = xhat
            ro[1][rows, :] = (xhat * g + b).astype(BF16)
            ro[2][rows, :] = rstd

        _by_sub_rows(acc.shape[0], sub)

    return _mm_rows("mix_ln1", "nn", yin, w_out, M=T, N=D, K=KY, tm=tm, tk=_tile(KY, 512),
                    row_ins=[x], vec_ins=[ln_g, ln_b],
                    row_outs=[jax.ShapeDtypeStruct((T, D), F32), jax.ShapeDtypeStruct((T, D), BF16),
                              jax.ShapeDtypeStruct((T, 1), F32)],
                    stat_outs=[], chunk_fn=chunk, deps=deps)


def _ff_up(x1, w_up_blk, first, count, prev=None, deps=()):
    T, D = x1.shape
    nb, _, fb = w_up_blk.shape
    tm = _tile(T, 1024)
    ni = T // tm
    n_dep = len(deps) + (2 if prev is not None else 0)

    def body(a_ref, b_ref, *rest):
        ra_ref, h2_ref = rest[n_dep:n_dep + 2]
        ra = jnp.maximum(_dot(a_ref[...], b_ref[...], NN), 0.0)
        ra_ref[...] = ra.astype(BF16)
        h2_ref[...] = (ra * ra).astype(BF16)

    blk = pl.BlockSpec((tm, fb), lambda i, j: (i, first + j))
    shp = jax.ShapeDtypeStruct((T, nb * fb), BF16)
    keep = list(prev) if prev is not None else []
    return pl.pallas_call(
        body,
        name="ff_up_%d" % first,
        grid=(ni, count),
        in_specs=[pl.BlockSpec((tm, D), lambda i, j: (i, 0)),
                  pl.BlockSpec((None, D, fb), lambda i, j: (first + j, 0, 0))] + _hbm_specs(n_dep),
        out_specs=[blk, blk],
        out_shape=[shp, shp],
        input_output_aliases=({2: 0, 3: 1} if prev is not None else {}),
        compiler_params=_params(2),
    )(x1, w_up_blk, *keep, *deps)


def _ff_down_loss(h2, w_down, xhat1, target, g1, b1, g2, b2):
    T, F = h2.shape
    D = w_down.shape[1]
    tm = _tile(T, 1024)
    inv_d = 1.0 / D

    def chunk(acc, valid, ri, vi, ro, so):
        g1v, b1v, g2v, b2v = (v[...] for v in vi)

        def sub(rows):
            x1 = ri[0][rows, :] * g1v + b1v
            xhat, rstd = _ln_fwd(DN_ALPHA * x1 + acc[rows, :])
            e = xhat * g2v + b2v - ri[1][rows, :]
            dy = e * inv_d
            dh = _ln_bwd(dy, xhat, rstd, g2v)
            ro[0][rows, :] = dh
            ro[1][rows, :] = dh.astype(BF16)
            return (jnp.sum(dy * xhat, axis=0, keepdims=True), jnp.sum(dy, axis=0, keepdims=True),
                    jnp.sum(e * e, axis=0, keepdims=True))

        sg, sb, sl = _by_sub_rows(acc.shape[0], sub)
        so[0][...] += jnp.where(valid, sg, 0.0)
        so[1][...] += jnp.where(valid, sb, 0.0)
        so[2][...] += jnp.where(valid, sl * (0.5 * inv_d), 0.0)

    vshape = jax.ShapeDtypeStruct((1, D), F32)
    return _mm_rows("ff_down_loss", "nn", h2, w_down, M=T, N=D, K=F, tm=tm, tk=_tile(F, 1024),
                    row_ins=[xhat1, target], vec_ins=[g1, b1, g2, b2],
                    row_outs=[jax.ShapeDtypeStruct((T, D), F32), jax.ShapeDtypeStruct((T, D), BF16)],
                    stat_outs=[vshape, vshape, vshape], chunk_fn=chunk)


def _ff_down_bwd_act(dh3b, w_down, ra):
    T, D = dh3b.shape
    F = w_down.shape[0]
    tm, tn = _tile(T, 1024), _tile(F, 1024)

    def ep(acc_ref, ex, o, i, j):
        o[0][...] = (acc_ref[...] * (2.0 * ex[0][...].astype(F32))).astype(BF16)

    blk = pl.BlockSpec((tm, tn), lambda i, j, k: (i, j))
    return _mm("ff_down_bwd_act", "nt", dh3b, w_down, M=T, N=F, K=D, tm=tm, tn=tn, tk=D,
               outs=[(jax.ShapeDtypeStruct((T, F), BF16), blk)], extras=[(ra, blk)], epilogue=ep)[0]


def _grad_w(name, a, b, *, a_fn=None, tm_pref=1024, tn_pref=1024, tk_pref=4096, deps=()):
    T, M = a.shape
    N = b.shape[1]
    tm, tn, tk = _tile(M, tm_pref), _tile(N, tn_pref), _tile(T, tk_pref)
    return _mm(name, "tn", a, b, M=M, N=N, K=T, tm=tm, tn=tn, tk=tk, a_fn=a_fn, deps=deps,
               outs=[(jax.ShapeDtypeStruct((M, N), F32), pl.BlockSpec((tm, tn), lambda i, j, k: (i, j)))],
               epilogue=None)[0]


def _grad_w_up_blk(x1, da, nb, deps=()):
    T, D = x1.shape
    F = da.shape[1]
    fb = F // nb
    tm, tk = _tile(D, 1024), _tile(T, 4096)
    return _mm("grad_w_up", "tn", x1, da, M=D, N=F, K=T, tm=tm, tn=fb, tk=tk, deps=deps,
               outs=[(jax.ShapeDtypeStruct((nb, D, fb), F32),
                      pl.BlockSpec((None, tm, fb), lambda i, j, k: (j, i, 0)))],
               epilogue=None)[0]


def _ff_up_bwd_ln1(da, w_up_blk, dh3, xhat1, rstd1, g1, deps=()):
    T, F = da.shape
    nb, D, fb = w_up_blk.shape
    tm = _tile(T, 1024)

    def chunk(acc, valid, ri, vi, ro, so):
        g = vi[0][...]

        def sub(rows):
            dx1 = DN_ALPHA * ri[0][rows, :] + acc[rows, :]
            xhat = ri[1][rows, :]
            dh = _ln_bwd(dx1, xhat, ri[2][rows, :], g)
            ro[0][rows, :] = dh
            ro[1][rows, :] = dh.astype(BF16)
            return jnp.sum(dx1 * xhat, axis=0, keepdims=True), jnp.sum(dx1, axis=0, keepdims=True)

        sg, sb = _by_sub_rows(acc.shape[0], sub)
        so[0][...] += jnp.where(valid, sg, 0.0)
        so[1][...] += jnp.where(valid, sb, 0.0)

    nk = F // fb
    vshape = jax.ShapeDtypeStruct((1, D), F32)
    return _mm_rows("ff_up_bwd_ln1", "nt", da, w_up_blk, M=T, N=D, K=F, tm=tm, tk=fb,
                    b_spec=pl.BlockSpec((None, D, fb), lambda i, k: (jnp.where(i < T // tm, k, nk - 1), 0, 0)),
                    row_ins=[dh3, xhat1, rstd1], vec_ins=[g1],
                    row_outs=[jax.ShapeDtypeStruct((T, D), F32), jax.ShapeDtypeStruct((T, D), BF16)],
                    stat_outs=[vshape, vshape], chunk_fn=chunk, deps=deps)


def _mix_bwd(dh1b, w_out, deps=()):
    T, D = dh1b.shape
    KY = w_out.shape[0]
    tm, tn = _tile(T, 1024), _tile(KY, 1024)
    return _mm("mix_bwd", "nt", dh1b, w_out, M=T, N=KY, K=D, tm=tm, tn=tn, tk=D, deps=deps,
               outs=[(jax.ShapeDtypeStruct((T, KY), F32), pl.BlockSpec((tm, tn), lambda i, j, k: (i, j)))],
               epilogue=None)[0]


def _conv_bwd(proj, dyin, conv_w8, conv_g, deps=()):
    T = proj.shape[0]
    tt = _tile(T, 256)
    nt = T // tt
    t8 = tt // 8
    nx = tt + 8

    def body(b_ref, c_ref, u_ref, d_ref, bn_ref, cn_ref, un_ref, dn_ref, cp_ref, up_ref, w_ref, g_ref, *rest):
        dp_ref, dw_ref, dg_ref = rest[len(deps):]
        i = pl.program_id(0)

        @pl.when(i == 0)
        def _():
            dw_ref[...] = jnp.zeros_like(dw_ref)
            dg_ref[...] = jnp.zeros_like(dg_ref)

        more = i < nt - 1

        def ext(cur_ref, nxt_ref):
            return jnp.concatenate([cur_ref[...], jnp.where(more, nxt_ref[...], 0.0)], axis=0)

        bx, cx, ux, dx = ext(b_ref, bn_ref), ext(c_ref, cn_ref), ext(u_ref, un_ref), ext(d_ref, dn_ref)
        hx = cx * ux
        hp = jnp.where(i > 0, cp_ref[...] * up_ref[...], 0.0)
        h1, h2 = _conv_shift(hx, hp)
        w = w_ref[...]
        g = g_ref[...]
        yx = w[0:1, :] * h2 + w[1:2, :] * h1 + w[2:3, :] * hx
        px = bx * yx
        dps, dgs = [], []
        for gi in range(CONV_GROUPS):
            sl = slice(gi * LANE, (gi + 1) * LANE)
            pg, dg_ = px[:, sl], dx[:, sl]
            r = lax.rsqrt(jnp.mean(pg * pg, axis=-1, keepdims=True) + RMS_EPS)
            gd = g[:, sl] * dg_
            dps.append(r * gd - pg * (r * r * r) * jnp.mean(pg * gd, axis=-1, keepdims=True))
            dgs.append(jnp.sum((dg_ * pg * r)[:tt, :], axis=0, keepdims=True))
        dpx = jnp.concatenate(dps, axis=1)
        dg_ref[...] += jnp.concatenate(dgs, axis=1)
        dyx = dpx * bx
        dyc = dyx[:tt, :]
        dh = (w[2:3, :] * dyx + w[1:2, :] * pltpu.roll(dyx, nx - 1, 0) + w[0:1, :] * pltpu.roll(dyx, nx - 2, 0))[:tt, :]
        dw_ref[0:1, :] += jnp.sum(dyc * h2[:tt, :], axis=0, keepdims=True)
        dw_ref[1:2, :] += jnp.sum(dyc * h1[:tt, :], axis=0, keepdims=True)
        dw_ref[2:3, :] += jnp.sum(dyc * hx[:tt, :], axis=0, keepdims=True)
        dp_ref[:, 0:D_CONV] = (dpx * yx)[:tt, :].astype(BF16)
        dp_ref[:, D_CONV:2 * D_CONV] = (dh * u_ref[...]).astype(BF16)
        dp_ref[:, 2 * D_CONV:3 * D_CONV] = (dh * c_ref[...]).astype(BF16)
        dp_ref[:, 3 * D_CONV:HALF_P] = jnp.zeros((tt, HALF_P - 3 * D_CONV), BF16)

    def col(cidx):
        return pl.BlockSpec((tt, D_CONV), lambda i: (i, cidx))

    def nxt(cidx):
        return pl.BlockSpec((8, D_CONV), lambda i: (jnp.minimum((i + 1) * t8, T // 8 - 1), cidx))

    def prev(cidx):
        return pl.BlockSpec((8, D_CONV), lambda i: (jnp.maximum(i * t8 - 1, 0), cidx))

    return pl.pallas_call(
        body,
        name="conv_bwd",
        grid=(nt,),
        in_specs=[col(0), col(1), col(2), col(0), nxt(0), nxt(1), nxt(2), nxt(0), prev(1), prev(2),
                  _full((8, D_CONV)), _full((1, D_CONV))] + _hbm_specs(len(deps)),
        out_specs=[pl.BlockSpec((tt, HALF_P), lambda i: (i, 0)), _full((8, D_CONV)), _full((1, D_CONV))],
        out_shape=[jax.ShapeDtypeStruct((T, P_INT), BF16), jax.ShapeDtypeStruct((8, D_CONV), F32),
                   jax.ShapeDtypeStruct((1, D_CONV), F32)],
        compiler_params=_params(1),
    )(proj, proj, proj, dyin, proj, proj, proj, dyin, proj, proj, conv_w8, conv_g, *deps)


def _gla_bwd(proj, wg128, gbias, gng, o_all, states, dyin, dproj):
    T = proj.shape[0]
    nst = T // STEP_ROWS

    def body(p_ref, wg_ref, gb_ref, gn_ref, o_ref, st_ref, d_ref, dp_in_ref,
             dp_ref, dwg_ref, dgb_ref, dgn_ref, ds_ref):
        n = pl.program_id(0)

        @pl.when(n == 0)
        def _():
            ds_ref[...] = jnp.zeros_like(ds_ref)
            dwg_ref[...] = jnp.zeros_like(dwg_ref)
            dgb_ref[...] = jnp.zeros_like(dgb_ref)
            dgn_ref[...] = jnp.zeros_like(dgn_ref)

        blk = p_ref[...]
        zl, z, bcum_all, same = _gla_step_terms(blk, wg_ref, gb_ref)
        causal = _causal()
        gn = gn_ref[...]
        ri = lax.broadcasted_iota(jnp.int32, (STEP_ROWS, STEP_ROWS), 0)
        ci = lax.broadcasted_iota(jnp.int32, (STEP_ROWS, STEP_ROWS), 1)
        upper = (same & (ri <= ci)).astype(F32)
        dstates = [ds_ref[h] for h in range(GLA_HEADS)]
        db_rows, dbl_rows, dgn_sum = [None] * STEP_CHUNKS, [None] * STEP_CHUNKS, [None] * GLA_HEADS
        for c in reversed(range(STEP_CHUNKS)):
            rows = slice(c * CHUNK, (c + 1) * CHUNK)
            q, k = blk[rows, 0:512], blk[rows, 512:1024]
            v, r = blk[rows, 1024:2048], blk[rows, 2048:3072]
            bcum = bcum_all[rows, :]
            db_parts, dbl_parts = [], []
            for h in range(GLA_HEADS):
                eb, enb, eend, dec, qd, ki, ke = _gla_head_terms(q, k, bcum, h)
                vs = slice(h * HEAD_V, (h + 1) * HEAD_V)
                ks = slice(h * HEAD_K, (h + 1) * HEAD_K)
                o = o_ref[rows, vs]
                rh = r[:, vs]
                dyg = d_ref[rows, vs]
                rinv = lax.rsqrt(jnp.mean(o * o, axis=-1, keepdims=True) + RMS_EPS)
                sg = _sigmoid(rh)
                on = o * rinv
                dr = dyg * (on * gn[:, vs]) * (sg * (1.0 + rh * (1.0 - sg)))
                don = dyg * (rh * sg)
                part = jnp.sum(don * on, axis=0, keepdims=True)
                dgn_sum[h] = part if dgn_sum[h] is None else dgn_sum[h] + part
                t = don * gn[:, vs]
                do = rinv * t - o * (rinv * rinv * rinv) * jnp.mean(o * t, axis=-1, keepdims=True)
                dob = do.astype(BF16)
                vb = v[:, vs].astype(BF16)
                qdb, kib, keb = qd.astype(BF16), ki.astype(BF16), ke.astype(BF16)
                a = jnp.where(causal, _dot(qdb, kib, NT), 0.0)
                st = st_ref[c, h]
                dst = dstates[h]
                dstb = dst.astype(BF16)
                da = jnp.where(causal, _dot(dob, vb, NT), 0.0)
                dab = da.astype(BF16)
                dv = _dot(a.astype(BF16), dob, TN) + _dot(keb, dstb, NT)
                dqd = _dot(dab, kib, NN) + _dot(dob, st.astype(BF16), NN)
                dki = _dot(dab, qdb, TN)
                dke = _dot(vb, dstb, NN)
                ddec = jnp.sum(st * dst, axis=0, keepdims=True)
                dstates[h] = dec * dst + _dot(dob, qdb, TN)
                dq = dqd * eb * (HEAD_K ** -0.5)
                dk = dki * enb + dke * eend
                db_parts.append(dqd * qd - dki * ki - dke * ke)
                dbl_parts.append(jnp.sum(dke * ke, axis=0, keepdims=True) + dec * ddec)
                dp_ref[rows, ks] = dq.astype(BF16)
                dp_ref[rows, D_GLA_K + h * HEAD_K:D_GLA_K + (h + 1) * HEAD_K] = dk.astype(BF16)
                dp_ref[rows, 1024 + h * HEAD_V:1024 + (h + 1) * HEAD_V] = dv.astype(BF16)
                dp_ref[rows, 2048 + h * HEAD_V:2048 + (h + 1) * HEAD_V] = dr.astype(BF16)
            db_rows[c] = jnp.concatenate(db_parts, axis=1)
            dbl_rows[c] = jnp.broadcast_to(jnp.concatenate(dbl_parts, axis=1), (CHUNK, D_GLA_K))
        for h in range(GLA_HEADS):
            ds_ref[h] = dstates[h]
            dgn_ref[:, h * HEAD_V:(h + 1) * HEAD_V] += dgn_sum[h]
        db = jnp.concatenate(db_rows, axis=0)
        dlog = _dot(upper, db, NN, precision=lax.Precision.HIGHEST) + jnp.concatenate(dbl_rows, axis=0)
        dz = dlog * (1.0 / GATE_TAU) * (1.0 / (1.0 + jnp.exp(z)))
        dzb = dz.astype(BF16)
        dp_ref[:, 3072:3200] = _dot(dzb, wg_ref[...], NT).astype(BF16)
        dwg_ref[...] += _dot(zl.astype(BF16), dzb, TN)
        dgb_ref[...] += jnp.sum(dz, axis=0, keepdims=True)

    rev = lambda n: nst - 1 - n
    return pl.pallas_call(
        body,
        name="gla_bwd",
        grid=(nst,),
        in_specs=[pl.BlockSpec((STEP_ROWS, HALF_P), lambda n: (rev(n), 1)), _full((LANE, D_GLA_K)),
                  _full((1, D_GLA_K)), _full((1, D_GLA_V)),
                  pl.BlockSpec((STEP_ROWS, D_GLA_V), lambda n: (rev(n), 0)),
                  pl.BlockSpec((STEP_CHUNKS, GLA_HEADS, HEAD_V, HEAD_K), lambda n: (rev(n), 0, 0, 0)),
                  pl.BlockSpec((STEP_ROWS, D_GLA_V), lambda n: (rev(n), 1)), pl.BlockSpec(memory_space=pl.ANY)],
        out_specs=[pl.BlockSpec((STEP_ROWS, HALF_P), lambda n: (rev(n), 1)), _full((LANE, D_GLA_K)),
                   _full((1, D_GLA_K)), _full((1, D_GLA_V))],
        out_shape=[jax.ShapeDtypeStruct(dproj.shape, BF16), jax.ShapeDtypeStruct((LANE, D_GLA_K), F32),
                   jax.ShapeDtypeStruct((1, D_GLA_K), F32), jax.ShapeDtypeStruct((1, D_GLA_V), F32)],
        scratch_shapes=[pltpu.VMEM((GLA_HEADS, HEAD_V, HEAD_K), F32)],
        input_output_aliases={7: 0},
        compiler_params=_params(1),
    )(proj, wg128, gbias, gng, o_all, states, dyin, dproj)


def _proj_bwd_x(dproj, w_full, dh1, deps=()):
    T, P = dproj.shape
    D = w_full.shape[0]
    tm, tk = _tile(T, 512), _tile(P, 1280)

    def ep(acc_ref, ex, o, i, j):
        o[0][...] = DN_ALPHA * ex[0][...] + acc_ref[...]

    row = pl.BlockSpec((tm, D), lambda i, j, k: (i, 0))
    return _mm("proj_bwd_x", "nt", dproj, w_full, M=T, N=D, K=P, tm=tm, tn=D, tk=tk,
               outs=[(jax.ShapeDtypeStruct((T, D), F32), row)], extras=[(dh1, row)], epilogue=ep, deps=deps)[0]


def _place():
    x, y, c = lax.axis_index("x"), lax.axis_index("y"), lax.axis_index("c")
    chips = [(1 - x, y), (x, 1 - y), (1 - x, 1 - y)]
    return x, y, c, chips


def _rcopy(src, dst, ssem, rsem, dev):
    return pltpu.make_async_remote_copy(src_ref=src, dst_ref=dst, send_sem=ssem, recv_sem=rsem,
                                        device_id=dev, device_id_type=MESH)


def _all_gather(name, shards, deps=()):
    n = len(shards)

    def body(*refs):
        ins, outs = refs[:n], refs[n + len(deps):2 * n + len(deps)]
        ssem, rsem, lsem = refs[2 * n + len(deps):]
        x, y, c, chips = _place()
        me, sib = (x, y, c), (x, y, 1 - c)

        def slot(w, px, py, pc):
            return outs[w].at[4 * px + 2 * py + pc]

        started = []
        for w in range(n):
            lc = pltpu.make_async_copy(ins[w], slot(w, *me), lsem.at[w])
            lc.start()
            started.append(lc)
        sends = []
        for w in range(n):
            cp = _rcopy(ins[w], slot(w, *me), ssem.at[7 * w], rsem.at[7 * w], sib)
            cp.start()
            sends.append(cp)
            for jx, chip in enumerate(chips):
                cp = _rcopy(ins[w], slot(w, *me), ssem.at[7 * w + 1 + jx], rsem.at[7 * w + 1 + jx], (*chip, c))
                cp.start()
                sends.append(cp)
        for w in range(n):
            for jx, chip in enumerate(chips):
                blk = slot(w, *chip, c)
                _rcopy(blk, blk, ssem.at[7 * w + 1 + jx], rsem.at[7 * w + 1 + jx], me).wait_recv()
                cp = _rcopy(blk, blk, ssem.at[7 * w + 4 + jx], rsem.at[7 * w + 4 + jx], sib)
                cp.start()
                sends.append(cp)
        for w in range(n):
            blk = slot(w, x, y, 1 - c)
            _rcopy(blk, blk, ssem.at[7 * w], rsem.at[7 * w], me).wait_recv()
            for jx, chip in enumerate(chips):
                blk = slot(w, *chip, 1 - c)
                _rcopy(blk, blk, ssem.at[7 * w + 4 + jx], rsem.at[7 * w + 4 + jx], me).wait_recv()
        for cp in sends:
            cp.wait_send()
        for lc in started:
            lc.wait()

    return pl.pallas_call(
        body,
        name=name,
        in_specs=_hbm_specs(n + len(deps)),
        out_specs=_hbm_specs(n),
        out_shape=[jax.ShapeDtypeStruct((N_DEV,) + s.shape, s.dtype) for s in shards],
        scratch_shapes=[pltpu.SemaphoreType.DMA((7 * n,)), pltpu.SemaphoreType.DMA((7 * n,)),
                        pltpu.SemaphoreType.DMA((n,))],
    )(*shards, *deps)


HBM_SPEC = pl.BlockSpec(memory_space=pltpu.HBM)
SEM_SPEC = pl.BlockSpec(memory_space=pltpu.SEMAPHORE)
SIDE_EFFECT = pltpu.SideEffectType.DATAFLOW_SIDE_EFFECTING


def _cast_place(name, ids, w, deps=(), dtype=None):
    dtype = BF16 if dtype is None else dtype
    R, C = w.shape
    tr = _tile(R, 256)

    def body(ids_ref, w_ref, *rest):
        rest[len(deps)][...] = w_ref[...].astype(dtype)

    return pl.pallas_call(
        body,
        name=name,
        grid_spec=pltpu.PrefetchScalarGridSpec(
            num_scalar_prefetch=1,
            grid=(R // tr,),
            in_specs=[pl.BlockSpec((tr, C), lambda r, ids: (r, 0))] + _hbm_specs(len(deps)),
            out_specs=pl.BlockSpec((None, tr, C), lambda r, ids: (ids[0], r, 0)),
        ),
        out_shape=jax.ShapeDtypeStruct((N_DEV, R, C), dtype),
        compiler_params=_params(1),
    )(ids, w, *deps)


def _xfer_start(name, bufs, plan, n):
    nb = len(bufs)

    def body(*refs):
        ins = refs[:nb]
        ssem, rsem = refs[nb], refs[nb + 1]
        token = refs[2 * nb + 2]
        x, y, c, chips = _place()
        for k, (src, dst, dev, _) in enumerate(plan(ins, x, y, c, chips)):
            _rcopy(src, dst, ssem.at[k], rsem.at[k], dev).start()
        token[...] = jnp.zeros_like(token)

    res = pl.pallas_call(
        body,
        name=name,
        out_shape=(pltpu.SemaphoreType.DMA((n,)), pltpu.SemaphoreType.DMA((n,)),
                   *[pltpu.HBM(b.shape, b.dtype) for b in bufs], jax.ShapeDtypeStruct((8, LANE), F32)),
        in_specs=[HBM_SPEC] * nb,
        out_specs=(SEM_SPEC, SEM_SPEC, *[HBM_SPEC] * nb, pl.BlockSpec(memory_space=pltpu.VMEM)),
        input_output_aliases={i: 2 + i for i in range(nb)},
        compiler_params=pltpu.CompilerParams(has_side_effects=SIDE_EFFECT),
    )(*[pltpu.with_memory_space_constraint(b, pltpu.HBM) for b in bufs])
    return dict(sems=res[:2], bufs=list(res[2:2 + nb]), token=res[2 + nb], plan=plan, n=n)


def _xfer_wait(name, started, after):
    bufs, plan = started["bufs"], started["plan"]
    nb = len(bufs)

    def body(*refs):
        ins = refs[:nb]
        ssem, rsem = refs[nb], refs[nb + 1]
        x, y, c, chips = _place()
        for k, (src, _, dev, land) in enumerate(plan(ins, x, y, c, chips)):
            cp = _rcopy(src, land, ssem.at[k], rsem.at[k], dev)
            cp.wait_send()
            cp.wait_recv()

    res = pl.pallas_call(
        body,
        name=name,
        out_shape=tuple(pltpu.HBM(b.shape, b.dtype) for b in bufs),
        in_specs=[HBM_SPEC] * nb + [SEM_SPEC, SEM_SPEC, pl.BlockSpec(memory_space=pl.ANY)],
        out_specs=tuple([HBM_SPEC] * nb),
        input_output_aliases={i: i for i in range(nb)},
        compiler_params=pltpu.CompilerParams(has_side_effects=SIDE_EFFECT),
    )(*bufs, *started["sems"], after)
    return list(res)


def _plan_gather_chips(refs, x, y, c, chips):
    plan = []
    for land in refs:
        mine = land.at[4 * x + 2 * y + c]
        plan.append((mine, mine, (x, y, 1 - c), land.at[4 * x + 2 * y + (1 - c)]))
        for px, py in chips:
            plan.append((mine, mine, (px, py, c), land.at[4 * px + 2 * py + c]))
    return plan


def _plan_gather_pass(refs, x, y, c, chips):
    return [(land.at[4 * px + 2 * py + c], land.at[4 * px + 2 * py + c], (x, y, 1 - c),
             land.at[4 * px + 2 * py + (1 - c)]) for land in refs for px, py in chips]


def _plan_gather_near(refs, x, y, c, chips):
    plan = []
    for land in refs:
        mine = land.at[4 * x + 2 * y + c]
        for px, py, pc in ((x, y, 1 - c), (1 - x, y, c), (x, 1 - y, c)):
            plan.append((mine, mine, (px, py, pc), land.at[4 * px + 2 * py + pc]))
    return plan


def _plan_gather_relay(refs, x, y, c, chips):
    south = c == 0
    src_x, src_y = jnp.where(south, 1 - x, x), jnp.where(south, y, 1 - y)
    dst_x, dst_y = jnp.where(south, x, 1 - x), jnp.where(south, 1 - y, y)
    plan = []
    for land in refs:
        for px, py in ((1 - x, y), (x, 1 - y)):
            plan.append((land.at[4 * px + 2 * py + c], land.at[4 * px + 2 * py + c], (x, y, 1 - c),
                         land.at[4 * px + 2 * py + (1 - c)]))
        blk = land.at[4 * src_x + 2 * src_y + c]
        plan.append((blk, blk, (dst_x, dst_y, c), land.at[4 * (1 - x) + 2 * (1 - y) + c]))
    return plan


def _plan_gather_far(refs, x, y, c, chips):
    return [(land.at[4 * (1 - x) + 2 * (1 - y) + c], land.at[4 * (1 - x) + 2 * (1 - y) + c], (x, y, 1 - c),
             land.at[4 * (1 - x) + 2 * (1 - y) + (1 - c)]) for land in refs]


def _plan_reduce_core(refs, x, y, c, chips):
    grad, recv = refs
    return [(grad.at[2 * q + (1 - c)], recv.at[q], (x, y, 1 - c), recv.at[q]) for q in range(N_CHIP)]


def _plan_reduce_chips(refs, x, y, c, chips):
    part, land = refs
    return [(part.at[2 * px + py], land.at[2 * x + y], (px, py, c), land.at[2 * px + py]) for px, py in chips]


def _chip_sums(name, ids, grad, recv):
    _, R, C = grad.shape
    tr = _tile(R, 256)

    def body(ids_ref, g_ref, r_ref, o_ref):
        o_ref[...] = (g_ref[...] + r_ref[...]).astype(BF16)

    return pl.pallas_call(
        body,
        name=name,
        grid_spec=pltpu.PrefetchScalarGridSpec(
            num_scalar_prefetch=1,
            grid=(N_CHIP - 1, R // tr),
            in_specs=[pl.BlockSpec((None, tr, C), lambda q, r, ids: (2 * ids[3 + q] + ids[2], r, 0)),
                      pl.BlockSpec((None, tr, C), lambda q, r, ids: (ids[3 + q], r, 0))],
            out_specs=pl.BlockSpec((None, tr, C), lambda q, r, ids: (ids[3 + q], r, 0)),
        ),
        out_shape=jax.ShapeDtypeStruct((N_CHIP, R, C), BF16),
        compiler_params=_params(2),
    )(ids, grad, recv)


def _adamw(w, g, m, v):
    m = ADAM_B1 * m + (1.0 - ADAM_B1) * g
    v = ADAM_B2 * v + (1.0 - ADAM_B2) * (g * g)
    m_hat = m / (1.0 - ADAM_B1 ** ADAM_STEP)
    v_hat = v / (1.0 - ADAM_B2 ** ADAM_STEP)
    delta = -ADAM_LR * (m_hat / (jnp.sqrt(v_hat) + ADAM_EPS) + ADAM_WD * w)
    return delta, m, v


def _reduce_adamw(name, ids, grad, recv, landed, w, m, v):
    _, R, C = grad.shape
    tr = _tile(R, 128)

    def body(ids_ref, g_ref, r_ref, l1_ref, l2_ref, l3_ref, w_ref, m_ref, v_ref, go_ref, do_ref, mo_ref, vo_ref):
        g = g_ref[...] + r_ref[...]
        g = g + l1_ref[...].astype(F32)
        g = g + l2_ref[...].astype(F32)
        g = g + l3_ref[...].astype(F32)
        delta, mn, vn = _adamw(w_ref[...], g, m_ref[...], v_ref[...])
        go_ref[...] = g
        do_ref[...] = delta
        mo_ref[...] = mn
        vo_ref[...] = vn

    def pick(k):
        return pl.BlockSpec((None, tr, C), lambda r, ids: (ids[k], r, 0))

    flat = pl.BlockSpec((tr, C), lambda r, ids: (r, 0))
    shp = jax.ShapeDtypeStruct((R, C), F32)
    return pl.pallas_call(
        body,
        name=name,
        grid_spec=pltpu.PrefetchScalarGridSpec(
            num_scalar_prefetch=1,
            grid=(R // tr,),
            in_specs=[pick(0), pick(1), pick(3), pick(4), pick(5), flat, flat, flat],
            out_specs=[flat, flat, flat, flat],
        ),
        out_shape=[shp, shp, shp, shp],
        compiler_params=_params(1),
    )(ids, grad, recv, landed, landed, landed, w, m, v)


SMALL = ("conv_w", "conv_norm_g", "w_gate_up", "gate_bias", "gla_norm_g", "ln1_g", "ln1_b", "ln2_g", "ln2_b")
R_LOSS = 14


def _small_rows(D, conv_cols, gate_cols):
    nv = max(1, D // SP_COLS)
    assert nv <= 2, D
    return {"conv_w": (0, 3, conv_cols), "conv_norm_g": (3, 1, D_CONV), "gate_bias": (4, 1, D_GLA_K),
            "gla_norm_g": (5, 1, D_GLA_V), "ln1_g": (6, nv, D), "ln1_b": (8, nv, D), "ln2_g": (10, nv, D),
            "ln2_b": (12, nv, D), "w_gate_up": (16, GATE_RANK, gate_cols)}


def _put(o_ref, entry, val):
    row, n_rows, cols = entry
    if val.shape[0] == 1 and n_rows > 1:
        for r in range(n_rows):
            o_ref[row + r:row + r + 1, :] = val[:, r * SP_COLS:(r + 1) * SP_COLS]
    else:
        o_ref[row:row + n_rows, 0:cols] = val[0:n_rows, 0:cols]


def _take(g, entry):
    row, n_rows, cols = entry
    if cols > SP_COLS:
        return jnp.concatenate([g[row + r:row + r + 1, :] for r in range(n_rows)], axis=1)
    return g[row:row + n_rows, 0:cols]


def _make_pack(name, rows, pieces):
    names = list(pieces)

    def body(*refs):
        o_ref = refs[len(names)]
        o_ref[...] = jnp.zeros_like(o_ref)
        for nm, ref in zip(names, refs):
            if nm == "loss":
                o_ref[R_LOSS:R_LOSS + 1, 0:1] = jnp.sum(ref[...], axis=1, keepdims=True)
            else:
                _put(o_ref, rows[nm], ref[...])

    arrs = [pieces[nm] for nm in names]
    return pl.pallas_call(
        body,
        name=name,
        grid=(1,),
        in_specs=[_full(a.shape) for a in arrs],
        out_specs=_full((SP_ROWS, SP_COLS)),
        out_shape=jax.ShapeDtypeStruct((SP_ROWS, SP_COLS), F32),
        compiler_params=_params(1),
    )(*arrs)


def _small_adamw(packs, rows, w, m, v):
    names = list(SMALL)
    n = len(names)

    def body(p_ref, *refs):
        ins, outs = refs[:3 * n], refs[3 * n:]
        g = p_ref[0]
        for dvc in range(1, N_DEV):
            g = g + p_ref[dvc]
        for i, nm in enumerate(names):
            gp = _take(g, rows[nm])
            delta, mn, vn = _adamw(ins[i][...], gp, ins[n + i][...], ins[2 * n + i][...])
            for kind, val in enumerate((gp, delta, mn, vn)):
                outs[kind * n + i][...] = val
        outs[4 * n][...] = g[R_LOSS:R_LOSS + 1, 0:1]

    arrs = [w[nm] for nm in names] + [m[nm] for nm in names] + [v[nm] for nm in names]
    shapes = [jax.ShapeDtypeStruct(w[nm].shape, F32) for nm in names] * 4 + [jax.ShapeDtypeStruct((1, 1), F32)]
    res = pl.pallas_call(
        body,
        name="small_adamw",
        grid=(1,),
        in_specs=[_full(packs.shape)] + [_full(a.shape) for a in arrs],
        out_specs=[_full(sh.shape) for sh in shapes],
        out_shape=shapes,
        compiler_params=_params(1),
    )(packs, *arrs)
    return [dict(zip(names, res[k * n:(k + 1) * n])) for k in range(4)], res[4 * n]


def _w_in_pieces():
    cs = D_IN_PROJ // N_DEV
    pieces = []
    for d in range(N_DEV):
        lo, hi = d * cs, (d + 1) * cs
        if hi <= CONV_COLS:
            pieces.append((d, 0, cs, lo))
        elif lo >= CONV_COLS:
            pieces.append((d, 0, cs, lo - CONV_COLS + HALF_P))
        else:
            pieces.append((d, 0, CONV_COLS - lo, lo))
            pieces.append((d, CONV_COLS - lo, cs, HALF_P))
    return pieces


def _w_in_full(gathered):
    nb, D, cs = gathered.shape
    tr = _tile(D, 256)

    def body(g_ref, o_ref):
        o_ref[:, CONV_COLS:HALF_P] = jnp.zeros((tr, HALF_P - CONV_COLS), o_ref.dtype)
        o_ref[:, HALF_P + GLA_COLS:P_INT] = jnp.zeros((tr, HALF_P - GLA_COLS), o_ref.dtype)
        for d, a, b, dst in _w_in_pieces():
            o_ref[:, dst:dst + (b - a)] = g_ref[d, :, a:b]

    return pl.pallas_call(
        body,
        name="w_in_full",
        grid=(D // tr,),
        in_specs=[pl.BlockSpec((nb, tr, cs), lambda r: (0, r, 0))],
        out_specs=pl.BlockSpec((tr, P_INT), lambda r: (r, 0)),
        out_shape=jax.ShapeDtypeStruct((D, P_INT), gathered.dtype),
        compiler_params=_params(1),
    )(gathered)


def _w_in_blocks(dw):
    D = dw.shape[0]
    cs = D_IN_PROJ // N_DEV
    tr = _tile(D, 256)

    def body(w_ref, o_ref):
        for d, a, b, src in _w_in_pieces():
            o_ref[d, :, a:b] = w_ref[:, src:src + (b - a)]

    return pl.pallas_call(
        body,
        name="w_in_blocks",
        grid=(D // tr,),
        in_specs=[pl.BlockSpec((tr, P_INT), lambda r: (r, 0))],
        out_specs=pl.BlockSpec((N_DEV, tr, cs), lambda r: (0, r, 0)),
        out_shape=jax.ShapeDtypeStruct((N_DEV, D, cs), dw.dtype),
        compiler_params=_params(1),
    )(dw)


BIG = ("w_in", "w_out", "w_ff_up", "w_ff_down")
ORDER = ("w_in", "conv_w", "conv_norm_g", "w_gate_up", "gate_bias", "gla_norm_g", "w_out", "ln1_g", "ln1_b",
         "w_ff_up", "w_ff_down", "ln2_g", "ln2_b")


def kernel(x, w_in, conv_w, conv_norm_g, w_gate_up, gate_bias, gla_norm_g, w_out, ln1_g, ln1_b, w_ff_up, w_ff_down, ln2_g, ln2_b, loss_target, m_w_in, m_conv_w, m_conv_norm_g, m_w_gate_up, m_gate_bias, m_gla_norm_g, m_w_out, m_ln1_g, m_ln1_b, m_w_ff_up, m_w_ff_down, m_ln2_g, m_ln2_b, v_w_in, v_conv_w, v_conv_norm_g, v_w_gate_up, v_gate_bias, v_gla_norm_g, v_w_out, v_ln1_g, v_ln1_b, v_w_ff_up, v_w_ff_down, v_ln2_g, v_ln2_b):
    T, D = x.shape[1], x.shape[2]
    xs, target = x[0], loss_target[0]
    xi, yi, ci = lax.axis_index("x"), lax.axis_index("y"), lax.axis_index("c")
    chip = 2 * xi + yi
    dev = 2 * chip + ci
    others = [jnp.where(chip <= q, q + 1, q) for q in range(N_CHIP - 1)]
    ids = jnp.stack([dev, chip, ci] + others).astype(jnp.int32)
    conv_cols, gate_cols = conv_w.shape[2], w_gate_up.shape[2]

    def gather(nm, lands):
        return _xfer_start("gather_chips_" + nm, lands, _plan_gather_chips, 4 * len(lands))

    def pass_on(nm, started, after):
        lands = _xfer_wait("gather_chips_wait_" + nm, started, after)
        return _xfer_start("gather_pass_" + nm, lands, _plan_gather_pass, 3 * len(lands))

    def landed(nm, started, after):
        return _xfer_wait("gather_pass_wait_" + nm, started, after)

    rows = _small_rows(D, conv_cols, gate_cols)
    fwd_pack = _make_pack("pack_fwd", rows, {"conv_w": conv_w[0], "w_gate_up": w_gate_up[0]})
    lands_in = [_cast_place("cast_place_w_in", ids, w_in[0]), _cast_place("cast_place_pack", ids, fwd_pack, dtype=F32)]
    near_in = _xfer_start("gather_near_w_in", lands_in, _plan_gather_near, 3 * len(lands_in))
    m_in, v_in = m_w_in[0], v_w_in[0]
    xb = _cast_bf16(xs, [near_in["token"], m_in, v_in])
    lands_in = _xfer_wait("gather_near_wait_w_in", near_in, xb)
    relay_in = _xfer_start("gather_relay_w_in", lands_in, _plan_gather_relay, 3 * len(lands_in))
    ga, dep = [], relay_in["token"]
    for nm, w in zip(BIG[1:], (w_out, w_ff_up, w_ff_down)):
        ga.append(gather(nm, [_cast_place("cast_place_" + nm, ids, w[0], [dep])]))
        dep = ga[-1]["token"]
    lands_in = _xfer_wait("gather_relay_wait_w_in", relay_in, dep)
    far_in = _xfer_start("gather_far_w_in", lands_in, _plan_gather_far, len(lands_in))
    g_in, g_pack = _xfer_wait("gather_far_wait_w_in", far_in, far_in["token"])
    w_full = _w_in_full(g_in)
    r_cw, r_gw = rows["conv_w"][0], rows["w_gate_up"][0]
    conv_w_full = g_pack[:, r_cw:r_cw + 3, :conv_cols].transpose(1, 0, 2).reshape(3, -1)
    gate_w_full = g_pack[:, r_gw:r_gw + GATE_RANK, :gate_cols].transpose(1, 0, 2).reshape(GATE_RANK, -1)
    conv_w8 = jnp.pad(conv_w_full, ((0, 5), (0, 0)))
    wg128 = jnp.pad(gate_w_full, ((0, LANE - GATE_RANK), (0, 0))).astype(BF16)
    proj = _proj_fwd(xb, w_full)
    yin = _conv_fwd(proj, conv_w8, conv_norm_g)
    gp_out = pass_on("w_out", ga[0], yin)
    o_all, states, yin = _gla_fwd(proj, wg128, gate_bias, gla_norm_g, yin, deps=[gp_out["token"]])
    w_out_full = landed("w_out", gp_out, o_all)[0].reshape(-1, D)
    gp_up = pass_on("w_ff_up", ga[1], o_all)
    xhat1, x1, rstd1 = _mix_ln1(yin, w_out_full, xs, ln1_g, ln1_b, deps=[gp_up["token"]])
    (w_up_blk,) = landed("w_ff_up", gp_up, x1)
    half = N_DEV // 2
    ra, h2 = _ff_up(x1, w_up_blk, 0, half)
    gp_down = pass_on("w_ff_down", ga[2], ra)
    ra, h2 = _ff_up(x1, w_up_blk, half, N_DEV - half, prev=(ra, h2), deps=[gp_down["token"]])
    w_down_full = landed("w_ff_down", gp_down, ra)[0].reshape(-1, D)
    dh3, dh3b, g_ln2_g, g_ln2_b, loss = _ff_down_loss(h2, w_down_full, xhat1, target, ln1_g, ln1_b, ln2_g, ln2_b)

    def to_core(nm, grad):
        recv = lax.empty((N_CHIP,) + grad.shape[1:], F32)
        return _xfer_start("reduce_core_" + nm, [grad, recv], _plan_reduce_core, N_CHIP)

    def to_chips(nm, started, after):
        grad, recv = _xfer_wait("reduce_core_wait_" + nm, started, after)
        part = _chip_sums("chip_sums_" + nm, ids, grad, recv)
        land = lax.empty(part.shape, BF16)
        return grad, recv, _xfer_start("reduce_chips_" + nm, [part, land], _plan_reduce_chips, N_CHIP - 1)

    da = _ff_down_bwd_act(dh3b, w_down_full, ra)
    gw_down = _grad_w("grad_w_down", h2, dh3b).reshape(N_DEV, -1, D)
    rc_down = to_core("w_ff_down", gw_down)
    gw_up = _grad_w_up_blk(x1, da, N_DEV, deps=[rc_down["token"]])
    gw_down, rv_down, rs_down = to_chips("w_ff_down", rc_down, gw_up)
    rc_up = to_core("w_ff_up", gw_up)
    dh1, dh1b, g_ln1_g, g_ln1_b = _ff_up_bwd_ln1(da, w_up_blk, dh3, xhat1, rstd1, ln1_g,
                                                 deps=[rs_down["token"], rc_up["token"]])
    gw_up, rv_up, rs_up = to_chips("w_ff_up", rc_up, dh1b)
    dyin = _mix_bwd(dh1b, w_out_full, deps=[rs_up["token"]])
    gw_out = _grad_w("grad_w_out", yin, dh1b).reshape(N_DEV, -1, D)
    rc_out = to_core("w_out", gw_out)
    dproj, g_conv_w, g_conv_g = _conv_bwd(proj, dyin, conv_w8, conv_norm_g, deps=[rc_out["token"]])
    dproj, g_gate_w, g_gate_b, g_gla_g = _gla_bwd(proj, wg128, gate_bias, gla_norm_g, o_all, states, dyin, dproj)
    gw_out, rv_out, rs_out = to_chips("w_out", rc_out, dproj)
    gw_in = _w_in_blocks(_grad_w("grad_w_in", xb, dproj, tn_pref=1280, tk_pref=2048, deps=[rs_out["token"]]))
    rc_in = to_core("w_in", gw_in)

    big = {}

    def finish(nm, grad, recv, started, w, m, v, after):
        _, land = _xfer_wait("reduce_chips_wait_" + nm, started, after)
        res = _reduce_adamw("adamw_" + nm, ids, grad, recv, land, w[0], m[0], v[0])
        big[nm] = [a[None] for a in res]
        return res[0]

    done = finish("w_ff_down", gw_down, rv_down, rs_down, w_ff_down, m_w_ff_down, v_w_ff_down, rc_in["token"])
    done = finish("w_ff_up", gw_up, rv_up, rs_up, w_ff_up, m_w_ff_up, v_w_ff_up, done)
    full_rows = _small_rows(D, D_CONV, D_GLA_K)
    pack = _make_pack("pack_grads", full_rows, {
        "conv_w": g_conv_w, "conv_norm_g": g_conv_g, "gate_bias": g_gate_b, "gla_norm_g": g_gla_g, "ln1_g": g_ln1_g,
        "ln1_b": g_ln1_b, "ln2_g": g_ln2_g, "ln2_b": g_ln2_b, "loss": loss, "w_gate_up": g_gate_w})
    (packs,) = _all_gather("gather_small_grads", [pack], deps=[done])
    gw_in, rv_in, rs_in = to_chips("w_in", rc_in, packs)
    done = finish("w_out", gw_out, rv_out, rs_out, w_out, m_w_out, v_w_out, rs_in["token"])
    grad_x = _proj_bwd_x(dproj, w_full, dh1, deps=[done])
    finish("w_in", gw_in, rv_in, rs_in, w_in, (m_in,), (v_in,), grad_x)

    def own_cols(row, n_rows, width):
        cut = lax.dynamic_slice(packs, (0, row, dev * width), (N_DEV, n_rows, width))
        return jnp.pad(cut, ((0, 0), (0, 0), (0, SP_COLS - width)))

    packs_own = jnp.concatenate([own_cols(r_cw, 3, conv_cols), packs[:, r_cw + 3:r_gw],
                                 own_cols(r_gw, GATE_RANK, gate_cols)], axis=1)
    as2d = lambda a: a[0] if a.ndim == 3 else a
    w_s = dict(zip(SMALL, map(as2d, (conv_w, conv_norm_g, w_gate_up, gate_bias, gla_norm_g, ln1_g, ln1_b, ln2_g, ln2_b))))
    m_s = dict(zip(SMALL, map(as2d, (m_conv_w, m_conv_norm_g, m_w_gate_up, m_gate_bias, m_gla_norm_g, m_ln1_g,
                                     m_ln1_b, m_ln2_g, m_ln2_b))))
    v_s = dict(zip(SMALL, map(as2d, (v_conv_w, v_conv_norm_g, v_w_gate_up, v_gate_bias, v_gla_norm_g, v_ln1_g,
                                     v_ln1_b, v_ln2_g, v_ln2_b))))
    small, loss_sum = _small_adamw(packs_own, rows, w_s, m_s, v_s)

    def leaf(kind, name):
        if name in BIG:
            return big[name][kind]
        a = small[kind][name]
        return a[None] if name in ("conv_w", "w_gate_up") else a

    out = [loss_sum[0, 0], grad_x[None]]
    for kind in range(4):
        out += [leaf(kind, nm) for nm in ORDER]
    return tuple(out)
```

```python
import jax
import jax.numpy as jnp
from jax import lax
from jax.experimental import pallas as pl
from jax.experimental.pallas import tpu as pltpu

F32 = jnp.float32
BF16 = jnp.bfloat16

D_CONV = 1024
CONV_GROUPS = 8
GLA_HEADS = 4
HEAD_K = 128
HEAD_V = 256
D_GLA_K = 512
D_GLA_V = 1024
GATE_RANK = 16
GATE_TAU = 16.0
CHUNK = 64
LN_EPS = 1e-5
RMS_EPS = 1e-6
DN_ALPHA = 2.0 ** 0.25
D_IN_PROJ = 6160
ADAM_LR = 0.001
ADAM_B1 = 0.9
ADAM_B2 = 0.999
ADAM_EPS = 1e-08
ADAM_WD = 0.01
ADAM_STEP = 10

N_DEV = 8
N_CHIP = 4
LANE = 128
HALF_P = 3200
P_INT = 2 * HALF_P
CONV_COLS = 3 * D_CONV
GLA_COLS = D_IN_PROJ - CONV_COLS
SP_ROWS = 32
SP_COLS = 1024
VMEM_LIMIT = 56 * 1024 * 1024

NN = ((1,), (0,))
NT = ((1,), (1,))
TN = ((0,), (0,))
MESH = pl.DeviceIdType.MESH


def _dot(a, b, dims, precision=None):
    return lax.dot_general(a, b, (dims, ((), ())), preferred_element_type=F32, precision=precision)


def _tile(n, pref):
    if n <= pref:
        return n
    t = (pref // LANE) * LANE
    while t > 0 and n % t:
        t -= LANE
    assert t > 0, (n, pref)
    return t


def _params(n_axes):
    return pltpu.CompilerParams(dimension_semantics=("arbitrary",) * n_axes, vmem_limit_bytes=VMEM_LIMIT)


def _full(shape):
    nd = len(shape)
    return pl.BlockSpec(shape, lambda *_: (0,) * nd)


def _hbm_specs(n):
    return [pl.BlockSpec(memory_space=pl.ANY)] * n


def _mm(name, mode, a, b, *, M, N, K, tm, tn, tk, outs, epilogue, extras=(), a_fn=None, a_spec=None, b_spec=None,
        deps=()):
    ni, nj, nk = M // tm, N // tn, K // tk
    assert ni * tm == M and nj * tn == N and nk * tk == K, (name, M, N, K, tm, tn, tk)
    if a_spec is None:
        a_spec = (pl.BlockSpec((tk, tm), lambda i, j, k: (k, i)) if mode == "tn"
                  else pl.BlockSpec((tm, tk), lambda i, j, k: (i, k)))
    if b_spec is None:
        b_spec = (pl.BlockSpec((tn, tk), lambda i, j, k: (j, k)) if mode == "nt"
                  else pl.BlockSpec((tk, tn), lambda i, j, k: (k, j)))
    dims = {"nn": NN, "nt": NT, "tn": TN}[mode]
    n_ex, n_out, n_dep = len(extras), len(outs), len(deps)

    def body(*refs):
        a_ref, b_ref = refs[0], refs[1]
        ex = refs[2:2 + n_ex]
        o = refs[2 + n_ex + n_dep:2 + n_ex + n_dep + n_out]
        acc_ref = refs[2 + n_ex + n_dep + n_out]
        i, j, k = pl.program_id(0), pl.program_id(1), pl.program_id(2)
        if nk > 1:
            @pl.when(k == 0)
            def _():
                acc_ref[...] = jnp.zeros_like(acc_ref)

        av = a_ref[...]
        if a_fn is not None:
            av = a_fn(av)
        part = _dot(av, b_ref[...], dims)
        if nk == 1 and epilogue is None:
            o[0][...] = part.astype(o[0].dtype)
        elif nk == 1:
            acc_ref[...] = part
            epilogue(acc_ref, ex, o, i, j)
        else:
            acc_ref[...] += part

            @pl.when(k == nk - 1)
            def _():
                if epilogue is None:
                    o[0][...] = acc_ref[...].astype(o[0].dtype)
                else:
                    epilogue(acc_ref, ex, o, i, j)

    return pl.pallas_call(
        body,
        name=name,
        grid=(ni, nj, nk),
        in_specs=[a_spec, b_spec] + [s for _, s in extras] + _hbm_specs(n_dep),
        out_specs=[s for _, s in outs],
        out_shape=[s for s, _ in outs],
        scratch_shapes=[pltpu.VMEM((8, LANE) if nk == 1 and epilogue is None else (tm, tn), F32)],
        compiler_params=_params(3),
    )(a, b, *[x for x, _ in extras], *deps)


def _mm_rows(name, mode, a, b, *, M, N, K, tm, tk, row_ins, vec_ins, row_outs, stat_outs, chunk_fn,
             b_spec=None, deps=()):
    ni, nk = M // tm, K // tk
    rc = tm // nk
    assert ni * tm == M and nk * tk == K and rc * nk == tm and rc % 16 == 0, (name, M, K, tm, tk)
    dims = {"nn": NN, "nt": NT}[mode]
    last = ni - 1

    def kk(i, k):
        return jnp.where(i < ni, k, nk - 1)

    a_spec = pl.BlockSpec((tm, tk), lambda i, k: (jnp.minimum(i, last), kk(i, k)))
    if b_spec is None:
        b_spec = (pl.BlockSpec((N, tk), lambda i, k: (0, kk(i, k))) if mode == "nt"
                  else pl.BlockSpec((tk, N), lambda i, k: (kk(i, k), 0)))
    prev_rows = lambda i, k: (jnp.maximum((i - 1) * nk + k, 0), 0)
    n_ri, n_vi, n_ro, n_so, n_dep = len(row_ins), len(vec_ins), len(row_outs), len(stat_outs), len(deps)

    def body(*refs):
        a_ref, b_ref = refs[0], refs[1]
        pos = 2
        ri = refs[pos:pos + n_ri]; pos += n_ri
        vi = refs[pos:pos + n_vi]; pos += n_vi + n_dep
        ro = refs[pos:pos + n_ro]; pos += n_ro
        so = refs[pos:pos + n_so]; pos += n_so
        accs = refs[pos:pos + 2]
        i, k = pl.program_id(0), pl.program_id(1)

        @pl.when((i == 0) & (k == 0))
        def _():
            accs[0][...] = jnp.zeros_like(accs[0])
            accs[1][...] = jnp.zeros_like(accs[1])
            for st in so:
                st[...] = jnp.zeros_like(st)

        def finish_rows(prev_ref):
            rows = pl.ds(pl.multiple_of(k * rc, rc), rc)
            done = prev_ref[rows, :]
            prev_ref[rows, :] = jnp.zeros((rc, N), F32)
            chunk_fn(done, i > 0, ri, vi, ro, so)

        def accumulate(acc_ref, after_ref):
            rp = tm // ROW_PARTS
            bv = b_ref[...]
            for part in range(ROW_PARTS):
                av = a_ref[part * rp:(part + 1) * rp, :]
                if part == 1:
                    tail = after_ref[rc - 16:rc, 0:LANE].astype(F32)
                    sixteen = jnp.uint32(16)
                    zero = lax.bitcast_convert_type(tail, jnp.uint32)
                    zero = lax.shift_right_logical(lax.shift_right_logical(zero, sixteen), sixteen)
                    av = av + jnp.tile(zero.astype(F32).astype(av.dtype), (rp // 16, tk // LANE))
                acc_ref[part * rp:(part + 1) * rp, :] += _dot(av, bv, dims)

        for parity in (0, 1):
            @pl.when((i < ni) & (lax.rem(i, 2) == parity))
            def _(parity=parity):
                finish_rows(accs[1 - parity])
                accumulate(accs[parity], ro[0])

        @pl.when(i == ni)
        def _():
            finish_rows(accs[last % 2])

    row_spec = lambda arr: pl.BlockSpec((rc, arr.shape[1]), prev_rows)
    return pl.pallas_call(
        body,
        name=name,
        grid=(ni + 1, nk),
        in_specs=[a_spec, b_spec] + [row_spec(x) for x in row_ins] + [_full(x.shape) for x in vec_ins]
        + _hbm_specs(n_dep),
        out_specs=[row_spec(s) for s in row_outs] + [_full(s.shape) for s in stat_outs],
        out_shape=list(row_outs) + list(stat_outs),
        scratch_shapes=[pltpu.VMEM((tm, N), F32), pltpu.VMEM((tm, N), F32)],
        compiler_params=_params(2),
    )(a, b, *row_ins, *vec_ins, *deps)


ROW_PARTS = 2
SUB_ROWS = 16


def _by_sub_rows(n_rows, fn):
    sums = None
    for r0 in range(0, n_rows, SUB_ROWS):
        part = fn(slice(r0, r0 + SUB_ROWS))
        if part:
            sums = part if sums is None else tuple(x + y for x, y in zip(sums, part))
    return sums


def _to_bf16(v):
    return v.astype(BF16)


def _ln_bwd(dy, xhat, rstd, g):
    dxh = dy * g
    m1 = jnp.mean(dxh, axis=-1, keepdims=True)
    m2 = jnp.mean(dxh * xhat, axis=-1, keepdims=True)
    return rstd * (dxh - m1 - xhat * m2)


def _ln_fwd(h):
    mu = jnp.mean(h, axis=-1, keepdims=True)
    xc = h - mu
    var = jnp.mean(xc * xc, axis=-1, keepdims=True)
    rstd = lax.rsqrt(var + LN_EPS)
    return xc * rstd, rstd


def _proj_fwd(x, w_full, deps=()):
    T, D = x.shape
    P = w_full.shape[1]
    tm, tn = _tile(T, 1024), _tile(P, 1280)
    return _mm("proj_fwd", "nn", x, w_full, M=T, N=P, K=D, tm=tm, tn=tn, tk=D,
               outs=[(jax.ShapeDtypeStruct((T, P), F32), pl.BlockSpec((tm, tn), lambda i, j, k: (i, j)))],
               epilogue=None, deps=deps)[0]


def _cast_bf16(x, deps=()):
    T, D = x.shape
    tm = _tile(T, 512)

    def body(x_ref, *rest):
        rest[len(deps)][...] = x_ref[...].astype(BF16)

    return pl.pallas_call(
        body,
        name="cast_x",
        grid=(T // tm,),
        in_specs=[pl.BlockSpec((tm, D), lambda i: (i, 0))] + _hbm_specs(len(deps)),
        out_specs=pl.BlockSpec((tm, D), lambda i: (i, 0)),
        out_shape=jax.ShapeDtypeStruct((T, D), BF16),
        compiler_params=_params(1),
    )(x, *deps)


def _conv_shift(h, hp):
    row = lax.broadcasted_iota(jnp.int32, h.shape, 0)
    hm1 = hp[7:8, :]
    hm2 = hp[6:7, :]
    h1 = jnp.where(row == 0, hm1, pltpu.roll(h, 1, 0))
    h2 = jnp.where(row == 0, hm2, jnp.where(row == 1, hm1, pltpu.roll(h, 2, 0)))
    return h1, h2


def _conv_fwd(proj, conv_w8, conv_g):
    T = proj.shape[0]
    tt = _tile(T, 256)
    nt = T // tt
    t8 = tt // 8

    def body(b_ref, c_ref, u_ref, cp_ref, up_ref, w_ref, g_ref, yin_ref):
        i = pl.program_id(0)
        h = c_ref[...] * u_ref[...]
        hp = jnp.where(i > 0, cp_ref[...] * up_ref[...], 0.0)
        h1, h2 = _conv_shift(h, hp)
        w = w_ref[...]
        y = w[0:1, :] * h2 + w[1:2, :] * h1 + w[2:3, :] * h
        p = b_ref[...] * y
        parts = []
        for gi in range(CONV_GROUPS):
            pg = p[:, gi * LANE:(gi + 1) * LANE]
            r = lax.rsqrt(jnp.mean(pg * pg, axis=-1, keepdims=True) + RMS_EPS)
            parts.append(pg * r)
        yn = jnp.concatenate(parts, axis=1) * g_ref[...]
        yin_ref[...] = yn.astype(BF16)

    def col(cidx):
        return pl.BlockSpec((tt, D_CONV), lambda i: (i, cidx))

    def prev(cidx):
        return pl.BlockSpec((8, D_CONV), lambda i: (jnp.maximum(i * t8 - 1, 0), cidx))

    return pl.pallas_call(
        body,
        name="conv_fwd",
        grid=(nt,),
        in_specs=[col(0), col(1), col(2), prev(1), prev(2), _full((8, D_CONV)), _full((1, D_CONV))],
        out_specs=pl.BlockSpec((tt, D_CONV), lambda i: (i, 0)),
        out_shape=jax.ShapeDtypeStruct((T, 2 * D_CONV), BF16),
        compiler_params=_params(1),
    )(proj, proj, proj, proj, proj, conv_w8, conv_g)


def _log_sigmoid(z):
    return jnp.minimum(z, 0.0) - jnp.log(1.0 + jnp.exp(-jnp.abs(z)))


STEP_CHUNKS = 4
STEP_ROWS = STEP_CHUNKS * CHUNK


def _gla_step_terms(blk, wg_ref, gb_ref):
    zl = blk[:, 3072:3200]
    z = _dot(zl.astype(BF16), wg_ref[...], NN) + gb_ref[...]
    log_a = _log_sigmoid(z) * (1.0 / GATE_TAU)
    ri = lax.broadcasted_iota(jnp.int32, (STEP_ROWS, STEP_ROWS), 0)
    ci = lax.broadcasted_iota(jnp.int32, (STEP_ROWS, STEP_ROWS), 1)
    same = (ri // CHUNK) == (ci // CHUNK)
    lower = (same & (ri >= ci)).astype(F32)
    bcum = _dot(lower, log_a, NN, precision=lax.Precision.HIGHEST)
    return zl, z, bcum, same


def _causal():
    return (lax.broadcasted_iota(jnp.int32, (CHUNK, CHUNK), 0) >= lax.broadcasted_iota(jnp.int32, (CHUNK, CHUNK), 1))


def _gla_head_terms(q, k, bcum, h):
    sl = slice(h * HEAD_K, (h + 1) * HEAD_K)
    bh = bcum[:, sl]
    bl = bh[CHUNK - 1:CHUNK, :]
    eb = jnp.exp(bh)
    enb = jnp.exp(-bh)
    eend = jnp.exp(bl - bh)
    dec = jnp.exp(bl)
    qd = q[:, sl] * (HEAD_K ** -0.5) * eb
    ki = k[:, sl] * enb
    ke = k[:, sl] * eend
    return eb, enb, eend, dec, qd, ki, ke


def _sigmoid(x):
    return 1.0 / (1.0 + jnp.exp(-x))


def _gla_fwd(proj, wg128, gbias, gng, yin, deps=()):
    T = proj.shape[0]
    nch = T // CHUNK
    nst = T // STEP_ROWS

    def body(p_ref, wg_ref, gb_ref, gn_ref, yin_in_ref, *rest):
        o_ref, st_ref, yin_ref, s_ref = rest[len(deps):]
        n = pl.program_id(0)

        @pl.when(n == 0)
        def _():
            s_ref[...] = jnp.zeros_like(s_ref)

        blk = p_ref[...]
        _, _, bcum_all, _ = _gla_step_terms(blk, wg_ref, gb_ref)
        causal = _causal()
        gn = gn_ref[...]
        states = [s_ref[h] for h in range(GLA_HEADS)]
        for c in range(STEP_CHUNKS):
            rows = slice(c * CHUNK, (c + 1) * CHUNK)
            q, k = blk[rows, 0:512], blk[rows, 512:1024]
            v, r = blk[rows, 1024:2048], blk[rows, 2048:3072]
            bcum = bcum_all[rows, :]
            for h in range(GLA_HEADS):
                _, _, _, dec, qd, ki, ke = _gla_head_terms(q, k, bcum, h)
                vs = slice(h * HEAD_V, (h + 1) * HEAD_V)
                vb = v[:, vs].astype(BF16)
                qdb = qd.astype(BF16)
                a = jnp.where(causal, _dot(qdb, ki.astype(BF16), NT), 0.0)
                st = states[h]
                o = _dot(a.astype(BF16), vb, NN) + _dot(qdb, st.astype(BF16), NT)
                st_ref[c, h] = st
                states[h] = dec * st + _dot(vb, ke.astype(BF16), TN)
                o_ref[rows, vs] = o
                rinv = lax.rsqrt(jnp.mean(o * o, axis=-1, keepdims=True) + RMS_EPS)
                rh = r[:, vs]
                yin_ref[rows, vs] = (o * rinv * gn[:, vs] * (rh * _sigmoid(rh))).astype(BF16)
        for h in range(GLA_HEADS):
            s_ref[h] = states[h]

    return pl.pallas_call(
        body,
        name="gla_fwd",
        grid=(nst,),
        in_specs=[pl.BlockSpec((STEP_ROWS, HALF_P), lambda n: (n, 1)), _full((LANE, D_GLA_K)), _full((1, D_GLA_K)),
                  _full((1, D_GLA_V)), pl.BlockSpec(memory_space=pl.ANY)] + _hbm_specs(len(deps)),
        out_specs=[pl.BlockSpec((STEP_ROWS, D_GLA_V), lambda n: (n, 0)),
                   pl.BlockSpec((STEP_CHUNKS, GLA_HEADS, HEAD_V, HEAD_K), lambda n: (n, 0, 0, 0)),
                   pl.BlockSpec((STEP_ROWS, D_GLA_V), lambda n: (n, 1))],
        out_shape=[jax.ShapeDtypeStruct((T, D_GLA_V), F32),
                   jax.ShapeDtypeStruct((nch, GLA_HEADS, HEAD_V, HEAD_K), F32),
                   jax.ShapeDtypeStruct(yin.shape, BF16)],
        scratch_shapes=[pltpu.VMEM((GLA_HEADS, HEAD_V, HEAD_K), F32)],
        input_output_aliases={4: 2},
        compiler_params=_params(1),
    )(proj, wg128, gbias, gng, yin, *deps)


def _mix_ln1(yin, w_out, x, ln_g, ln_b, deps=()):
    T, D = x.shape
    KY = yin.shape[1]
    tm = _tile(T, 1024)

    def chunk(acc, valid, ri, vi, ro, so):
        g, b = vi[0][...], vi[1][...]

        def sub(rows):
            xhat, rstd = _ln_fwd(DN_ALPHA * ri[0][rows, :] + acc[rows, :])
            ro[0][rows, :] = xhat
            ro[1][rows, :] = (xhat * g + b).astype(BF16)
            ro[2][rows, :] = rstd

        _by_sub_rows(acc.shape[0], sub)

    return _mm_rows("mix_ln1", "nn", yin, w_out, M=T, N=D, K=KY, tm=tm, tk=_tile(KY, 512),
                    row_ins=[x], vec_ins=[ln_g, ln_b],
                    row_outs=[jax.ShapeDtypeStruct((T, D), F32), jax.ShapeDtypeStruct((T, D), BF16),
                              jax.ShapeDtypeStruct((T, 1), F32)],
                    stat_outs=[], chunk_fn=chunk, deps=deps)


def _ff_up(x1, w_up_blk, first, count, prev=None, deps=()):
    T, D = x1.shape
    nb, _, fb = w_up_blk.shape
    tm = _tile(T, 1024)
    ni = T // tm
    n_dep = len(deps) + (2 if prev is not None else 0)

    def body(a_ref, b_ref, *rest):
        ra_ref, h2_ref = rest[n_dep:n_dep + 2]
        ra = jnp.maximum(_dot(a_ref[...], b_ref[...], NN), 0.0)
        ra_ref[...] = ra.astype(BF16)
        h2_ref[...] = (ra * ra).astype(BF16)

    blk = pl.BlockSpec((tm, fb), lambda i, j: (i, first + j))
    shp = jax.ShapeDtypeStruct((T, nb * fb), BF16)
    keep = list(prev) if prev is not None else []
    return pl.pallas_call(
        body,
        name="ff_up_%d" % first,
        grid=(ni, count),
        in_specs=[pl.BlockSpec((tm, D), lambda i, j: (i, 0)),
                  pl.BlockSpec((None, D, fb), lambda i, j: (first + j, 0, 0))] + _hbm_specs(n_dep),
        out_specs=[blk, blk],
        out_shape=[shp, shp],
        input_output_aliases=({2: 0, 3: 1} if prev is not None else {}),
        compiler_params=_params(2),
    )(x1, w_up_blk, *keep, *deps)


def _ff_down_loss(h2, w_down, xhat1, target, g1, b1, g2, b2):
    T, F = h2.shape
    D = w_down.shape[1]
    tm = _tile(T, 1024)
    inv_d = 1.0 / D

    def chunk(acc, valid, ri, vi, ro, so):
        g1v, b1v, g2v, b2v = (v[...] for v in vi)

        def sub(rows):
            x1 = ri[0][rows, :] * g1v + b1v
            xhat, rstd = _ln_fwd(DN_ALPHA * x1 + acc[rows, :])
            e = xhat * g2v + b2v - ri[1][rows, :]
            dy = e * inv_d
            dh = _ln_bwd(dy, xhat, rstd, g2v)
            ro[0][rows, :] = dh
            ro[1][rows, :] = dh.astype(BF16)
            return (jnp.sum(dy * xhat, axis=0, keepdims=True), jnp.sum(dy, axis=0, keepdims=True),
                    jnp.sum(e * e, axis=0, keepdims=True))

        sg, sb, sl = _by_sub_rows(acc.shape[0], sub)
        so[0][...] += jnp.where(valid, sg, 0.0)
        so[1][...] += jnp.where(valid, sb, 0.0)
        so[2][...] += jnp.where(valid, sl * (0.5 * inv_d), 0.0)

    vshape = jax.ShapeDtypeStruct((1, D), F32)
    return _mm_rows("ff_down_loss", "nn", h2, w_down, M=T, N=D, K=F, tm=tm, tk=_tile(F, 1024),
                    row_ins=[xhat1, target], vec_ins=[g1, b1, g2, b2],
                    row_outs=[jax.ShapeDtypeStruct((T, D), F32), jax.ShapeDtypeStruct((T, D), BF16)],
                    stat_outs=[vshape, vshape, vshape], chunk_fn=chunk)


def _ff_down_bwd_act(dh3b, w_down, ra):
    T, D = dh3b.shape
    F = w_down.shape[0]
    tm, tn = _tile(T, 1024), _tile(F, 1024)

    def ep(acc_ref, ex, o, i, j):
        o[0][...] = (acc_ref[...] * (2.0 * ex[0][...].astype(F32))).astype(BF16)

    blk = pl.BlockSpec((tm, tn), lambda i, j, k: (i, j))
    return _mm("ff_down_bwd_act", "nt", dh3b, w_down, M=T, N=F, K=D, tm=tm, tn=tn, tk=D,
               outs=[(jax.ShapeDtypeStruct((T, F), BF16), blk)], extras=[(ra, blk)], epilogue=ep)[0]


def _grad_w(name, a, b, *, a_fn=None, tm_pref=1024, tn_pref=1024, tk_pref=4096, deps=()):
    T, M = a.shape
    N = b.shape[1]
    tm, tn, tk = _tile(M, tm_pref), _tile(N, tn_pref), _tile(T, tk_pref)
    return _mm(name, "tn", a, b, M=M, N=N, K=T, tm=tm, tn=tn, tk=tk, a_fn=a_fn, deps=deps,
               outs=[(jax.ShapeDtypeStruct((M, N), F32), pl.BlockSpec((tm, tn), lambda i, j, k: (i, j)))],
               epilogue=None)[0]


def _grad_w_up_blk(x1, da, nb, deps=()):
    T, D = x1.shape
    F = da.shape[1]
    fb = F // nb
    tm, tk = _tile(D, 1024), _tile(T, 4096)
    return _mm("grad_w_up", "tn", x1, da, M=D, N=F, K=T, tm=tm, tn=fb, tk=tk, deps=deps,
               outs=[(jax.ShapeDtypeStruct((nb, D, fb), F32),
                      pl.BlockSpec((None, tm, fb), lambda i, j, k: (j, i, 0)))],
               epilogue=None)[0]


def _ff_up_bwd_ln1(da, w_up_blk, dh3, xhat1, rstd1, g1, deps=()):
    T, F = da.shape
    nb, D, fb = w_up_blk.shape
    tm = _tile(T, 1024)

    def chunk(acc, valid, ri, vi, ro, so):
        g = vi[0][...]

        def sub(rows):
            dx1 = DN_ALPHA * ri[0][rows, :] + acc[rows, :]
            xhat = ri[1][rows, :]
            dh = _ln_bwd(dx1, xhat, ri[2][rows, :], g)
            ro[0][rows, :] = dh
            ro[1][rows, :] = dh.astype(BF16)
            return jnp.sum(dx1 * xhat, axis=0, keepdims=True), jnp.sum(dx1, axis=0, keepdims=True)

        sg, sb = _by_sub_rows(acc.shape[0], sub)
        so[0][...] += jnp.where(valid, sg, 0.0)
        so[1][...] += jnp.where(valid, sb, 0.0)

    nk = F // fb
    vshape = jax.ShapeDtypeStruct((1, D), F32)
    return _mm_rows("ff_up_bwd_ln1", "nt", da, w_up_blk, M=T, N=D, K=F, tm=tm, tk=fb,
                    b_spec=pl.BlockSpec((None, D, fb), lambda i, k: (jnp.where(i < T // tm, k, nk - 1), 0, 0)),
                    row_ins=[dh3, xhat1, rstd1], vec_ins=[g1],
                    row_outs=[jax.ShapeDtypeStruct((T, D), F32), jax.ShapeDtypeStruct((T, D), BF16)],
                    stat_outs=[vshape, vshape], chunk_fn=chunk, deps=deps)


def _mix_bwd(dh1b, w_out, deps=()):
    T, D = dh1b.shape
    KY = w_out.shape[0]
    tm, tn = _tile(T, 1024), _tile(KY, 1024)
    return _mm("mix_bwd", "nt", dh1b, w_out, M=T, N=KY, K=D, tm=tm, tn=tn, tk=D, deps=deps,
               outs=[(jax.ShapeDtypeStruct((T, KY), F32), pl.BlockSpec((tm, tn), lambda i, j, k: (i, j)))],
               epilogue=None)[0]


def _conv_bwd(proj, dyin, conv_w8, conv_g, deps=()):
    T = proj.shape[0]
    tt = _tile(T, 256)
    nt = T // tt
    t8 = tt // 8
    nx = tt + 8

    def body(b_ref, c_ref, u_ref, d_ref, bn_ref, cn_ref, un_ref, dn_ref, cp_ref, up_ref, w_ref, g_ref, *rest):
        dp_ref, dw_ref, dg_ref = rest[len(deps):]
        i = pl.program_id(0)

        @pl.when(i == 0)
        def _():
            dw_ref[...] = jnp.zeros_like(dw_ref)
            dg_ref[...] = jnp.zeros_like(dg_ref)

        more = i < nt - 1

        def ext(cur_ref, nxt_ref):
            return jnp.concatenate([cur_ref[...], jnp.where(more, nxt_ref[...], 0.0)], axis=0)

        bx, cx, ux, dx = ext(b_ref, bn_ref), ext(c_ref, cn_ref), ext(u_ref, un_ref), ext(d_ref, dn_ref)
        hx = cx * ux
        hp = jnp.where(i > 0, cp_ref[...] * up_ref[...], 0.0)
        h1, h2 = _conv_shift(hx, hp)
        w = w_ref[...]
        g = g_ref[...]
        yx = w[0:1, :] * h2 + w[1:2, :] * h1 + w[2:3, :] * hx
        px = bx * yx
        dps, dgs = [], []
        for gi in range(CONV_GROUPS):
            sl = slice(gi * LANE, (gi + 1) * LANE)
            pg, dg_ = px[:, sl], dx[:, sl]
            r = lax.rsqrt(jnp.mean(pg * pg, axis=-1, keepdims=True) + RMS_EPS)
            gd = g[:, sl] * dg_
            dps.append(r * gd - pg * (r * r * r) * jnp.mean(pg * gd, axis=-1, keepdims=True))
            dgs.append(jnp.sum((dg_ * pg * r)[:tt, :], axis=0, keepdims=True))
        dpx = jnp.concatenate(dps, axis=1)
        dg_ref[...] += jnp.concatenate(dgs, axis=1)
        dyx = dpx * bx
        dyc = dyx[:tt, :]
        dh = (w[2:3, :] * dyx + w[1:2, :] * pltpu.roll(dyx, nx - 1, 0) + w[0:1, :] * pltpu.roll(dyx, nx - 2, 0))[:tt, :]
        dw_ref[0:1, :] += jnp.sum(dyc * h2[:tt, :], axis=0, keepdims=True)
        dw_ref[1:2, :] += jnp.sum(dyc * h1[:tt, :], axis=0, keepdims=True)
        dw_ref[2:3, :] += jnp.sum(dyc * hx[:tt, :], axis=0, keepdims=True)
        dp_ref[:, 0:D_CONV] = (dpx * yx)[:tt, :].astype(BF16)
        dp_ref[:, D_CONV:2 * D_CONV] = (dh * u_ref[...]).astype(BF16)
        dp_ref[:, 2 * D_CONV:3 * D_CONV] = (dh * c_ref[...]).astype(BF16)
        dp_ref[:, 3 * D_CONV:HALF_P] = jnp.zeros((tt, HALF_P - 3 * D_CONV), BF16)

    def col(cidx):
        return pl.BlockSpec((tt, D_CONV), lambda i: (i, cidx))

    def nxt(cidx):
        return pl.BlockSpec((8, D_CONV), lambda i: (jnp.minimum((i + 1) * t8, T // 8 - 1), cidx))

    def prev(cidx):
        return pl.BlockSpec((8, D_CONV), lambda i: (jnp.maximum(i * t8 - 1, 0), cidx))

    return pl.pallas_call(
        body,
        name="conv_bwd",
        grid=(nt,),
        in_specs=[col(0), col(1), col(2), col(0), nxt(0), nxt(1), nxt(2), nxt(0), prev(1), prev(2),
                  _full((8, D_CONV)), _full((1, D_CONV))] + _hbm_specs(len(deps)),
        out_specs=[pl.BlockSpec((tt, HALF_P), lambda i: (i, 0)), _full((8, D_CONV)), _full((1, D_CONV))],
        out_shape=[jax.ShapeDtypeStruct((T, P_INT), BF16), jax.ShapeDtypeStruct((8, D_CONV), F32),
                   jax.ShapeDtypeStruct((1, D_CONV), F32)],
        compiler_params=_params(1),
    )(proj, proj, proj, dyin, proj, proj, proj, dyin, proj, proj, conv_w8, conv_g, *deps)


def _gla_bwd(proj, wg128, gbias, gng, o_all, states, dyin, dproj):
    T = proj.shape[0]
    nst = T // STEP_ROWS

    def body(p_ref, wg_ref, gb_ref, gn_ref, o_ref, st_ref, d_ref, dp_in_ref,
             dp_ref, dwg_ref, dgb_ref, dgn_ref, ds_ref):
        n = pl.program_id(0)

        @pl.when(n == 0)
        def _():
            ds_ref[...] = jnp.zeros_like(ds_ref)
            dwg_ref[...] = jnp.zeros_like(dwg_ref)
            dgb_ref[...] = jnp.zeros_like(dgb_ref)
            dgn_ref[...] = jnp.zeros_like(dgn_ref)

        blk = p_ref[...]
        zl, z, bcum_all, same = _gla_step_terms(blk, wg_ref, gb_ref)
        causal = _causal()
        gn = gn_ref[...]
        ri = lax.broadcasted_iota(jnp.int32, (STEP_ROWS, STEP_ROWS), 0)
        ci = lax.broadcasted_iota(jnp.int32, (STEP_ROWS, STEP_ROWS), 1)
        upper = (same & (ri <= ci)).astype(F32)
        dstates = [ds_ref[h] for h in range(GLA_HEADS)]
        db_rows, dbl_rows, dgn_sum = [None] * STEP_CHUNKS, [None] * STEP_CHUNKS, [None] * GLA_HEADS
        for c in reversed(range(STEP_CHUNKS)):
            rows = slice(c * CHUNK, (c + 1) * CHUNK)
            q, k = blk[rows, 0:512], blk[rows, 512:1024]
            v, r = blk[rows, 1024:2048], blk[rows, 2048:3072]
            bcum = bcum_all[rows, :]
            db_parts, dbl_parts = [], []
            for h in range(GLA_HEADS):
                eb, enb, eend, dec, qd, ki, ke = _gla_head_terms(q, k, bcum, h)
                vs = slice(h * HEAD_V, (h + 1) * HEAD_V)
                ks = slice(h * HEAD_K, (h + 1) * HEAD_K)
                o = o_ref[rows, vs]
                rh = r[:, vs]
                dyg = d_ref[rows, vs]
                rinv = lax.rsqrt(jnp.mean(o * o, axis=-1, keepdims=True) + RMS_EPS)
                sg = _sigmoid(rh)
                on = o * rinv
                dr = dyg * (on * gn[:, vs]) * (sg * (1.0 + rh * (1.0 - sg)))
                don = dyg * (rh * sg)
                part = jnp.sum(don * on, axis=0, keepdims=True)
                dgn_sum[h] = part if dgn_sum[h] is None else dgn_sum[h] + part
                t = don * gn[:, vs]
                do = rinv * t - o * (rinv * rinv * rinv) * jnp.mean(o * t, axis=-1, keepdims=True)
                dob = do.astype(BF16)
                vb = v[:, vs].astype(BF16)
                qdb, kib, keb = qd.astype(BF16), ki.astype(BF16), ke.astype(BF16)
                a = jnp.where(causal, _dot(qdb, kib, NT), 0.0)
                st = st_ref[c, h]
                dst = dstates[h]
                dstb = dst.astype(BF16)
                da = jnp.where(causal, _dot(dob, vb, NT), 0.0)
                dab = da.astype(BF16)
                dv = _dot(a.astype(BF16), dob, TN) + _dot(keb, dstb, NT)
                dqd = _dot(dab, kib, NN) + _dot(dob, st.astype(BF16), NN)
                dki = _dot(dab, qdb, TN)
                dke = _dot(vb, dstb, NN)
                ddec = jnp.sum(st * dst, axis=0, keepdims=True)
                dstates[h] = dec * dst + _dot(dob, qdb, TN)
                dq = dqd * eb * (HEAD_K ** -0.5)
                dk = dki * enb + dke * eend
                db_parts.append(dqd * qd - dki * ki - dke * ke)
                dbl_parts.append(jnp.sum(dke * ke, axis=0, keepdims=True) + dec * ddec)
                dp_ref[rows, ks] = dq.astype(BF16)
                dp_ref[rows, D_GLA_K + h * HEAD_K:D_GLA_K + (h + 1) * HEAD_K] = dk.astype(BF16)
                dp_ref[rows, 1024 + h * HEAD_V:1024 + (h + 1) * HEAD_V] = dv.astype(BF16)
                dp_ref[rows, 2048 + h * HEAD_V:2048 + (h + 1) * HEAD_V] = dr.astype(BF16)
            db_rows[c] = jnp.concatenate(db_parts, axis=1)
            dbl_rows[c] = jnp.broadcast_to(jnp.concatenate(dbl_parts, axis=1), (CHUNK, D_GLA_K))
        for h in range(GLA_HEADS):
            ds_ref[h] = dstates[h]
            dgn_ref[:, h * HEAD_V:(h + 1) * HEAD_V] += dgn_sum[h]
        db = jnp.concatenate(db_rows, axis=0)
        dlog = _dot(upper, db, NN, precision=lax.Precision.HIGHEST) + jnp.concatenate(dbl_rows, axis=0)
        dz = dlog * (1.0 / GATE_TAU) * (1.0 / (1.0 + jnp.exp(z)))
        dzb = dz.astype(BF16)
        dp_ref[:, 3072:3200] = _dot(dzb, wg_ref[...], NT).astype(BF16)
        dwg_ref[...] += _dot(zl.astype(BF16), dzb, TN)
        dgb_ref[...] += jnp.sum(dz, axis=0, keepdims=True)

    rev = lambda n: nst - 1 - n
    return pl.pallas_call(
        body,
        name="gla_bwd",
        grid=(nst,),
        in_specs=[pl.BlockSpec((STEP_ROWS, HALF_P), lambda n: (rev(n), 1)), _full((LANE, D_GLA_K)),
                  _full((1, D_GLA_K)), _full((1, D_GLA_V)),
                  pl.BlockSpec((STEP_ROWS, D_GLA_V), lambda n: (rev(n), 0)),
                  pl.BlockSpec((STEP_CHUNKS, GLA_HEADS, HEAD_V, HEAD_K), lambda n: (rev(n), 0, 0, 0)),
                  pl.BlockSpec((STEP_ROWS, D_GLA_V), lambda n: (rev(n), 1)), pl.BlockSpec(memory_space=pl.ANY)],
        out_specs=[pl.BlockSpec((STEP_ROWS, HALF_P), lambda n: (rev(n), 1)), _full((LANE, D_GLA_K)),
                   _full((1, D_GLA_K)), _full((1, D_GLA_V))],
        out_shape=[jax.ShapeDtypeStruct(dproj.shape, BF16), jax.ShapeDtypeStruct((LANE, D_GLA_K), F32),
                   jax.ShapeDtypeStruct((1, D_GLA_K), F32), jax.ShapeDtypeStruct((1, D_GLA_V), F32)],
        scratch_shapes=[pltpu.VMEM((GLA_HEADS, HEAD_V, HEAD_K), F32)],
        input_output_aliases={7: 0},
        compiler_params=_params(1),
    )(proj, wg128, gbias, gng, o_all, states, dyin, dproj)


def _proj_bwd_x(dproj, w_full, dh1, deps=()):
    T, P = dproj.shape
    D = w_full.shape[0]
    tm, tk = _tile(T, 512), _tile(P, 1280)

    def ep(acc_ref, ex, o, i, j):
        o[0][...] = DN_ALPHA * ex[0][...] + acc_ref[...]

    row = pl.BlockSpec((tm, D), lambda i, j, k: (i, 0))
    return _mm("proj_bwd_x", "nt", dproj, w_full, M=T, N=D, K=P, tm=tm, tn=D, tk=tk,
               outs=[(jax.ShapeDtypeStruct((T, D), F32), row)], extras=[(dh1, row)], epilogue=ep, deps=deps)[0]


def _place():
    x, y, c = lax.axis_index("x"), lax.axis_index("y"), lax.axis_index("c")
    chips = [(1 - x, y), (x, 1 - y), (1 - x, 1 - y)]
    return x, y, c, chips


def _rcopy(src, dst, ssem, rsem, dev):
    return pltpu.make_async_remote_copy(src_ref=src, dst_ref=dst, send_sem=ssem, recv_sem=rsem,
                                        device_id=dev, device_id_type=MESH)


def _all_gather(name, shards, deps=()):
    n = len(shards)

    def body(*refs):
        ins, outs = refs[:n], refs[n + len(deps):2 * n + len(deps)]
        ssem, rsem, lsem = refs[2 * n + len(deps):]
        x, y, c, chips = _place()
        me, sib = (x, y, c), (x, y, 1 - c)

        def slot(w, px, py, pc):
            return outs[w].at[4 * px + 2 * py + pc]

        started = []
        for w in range(n):
            lc = pltpu.make_async_copy(ins[w], slot(w, *me), lsem.at[w])
            lc.start()
            started.append(lc)
        sends = []
        for w in range(n):
            cp = _rcopy(ins[w], slot(w, *me), ssem.at[7 * w], rsem.at[7 * w], sib)
            cp.start()
            sends.append(cp)
            for jx, chip in enumerate(chips):
                cp = _rcopy(ins[w], slot(w, *me), ssem.at[7 * w + 1 + jx], rsem.at[7 * w + 1 + jx], (*chip, c))
                cp.start()
                sends.append(cp)
        for w in range(n):
            for jx, chip in enumerate(chips):
                blk = slot(w, *chip, c)
                _rcopy(blk, blk, ssem.at[7 * w + 1 + jx], rsem.at[7 * w + 1 + jx], me).wait_recv()
                cp = _rcopy(blk, blk, ssem.at[7 * w + 4 + jx], rsem.at[7 * w + 4 + jx], sib)
                cp.start()
                sends.append(cp)
        for w in range(n):
            blk = slot(w, x, y, 1 - c)
            _rcopy(blk, blk, ssem.at[7 * w], rsem.at[7 * w], me).wait_recv()
            for jx, chip in enumerate(chips):
                blk = slot(w, *chip, 1 - c)
                _rcopy(blk, blk, ssem.at[7 * w + 4 + jx], rsem.at[7 * w + 4 + jx], me).wait_recv()
        for cp in sends:
            cp.wait_send()
        for lc in started:
            lc.wait()

    return pl.pallas_call(
        body,
        name=name,
        in_specs=_hbm_specs(n + len(deps)),
        out_specs=_hbm_specs(n),
        out_shape=[jax.ShapeDtypeStruct((N_DEV,) + s.shape, s.dtype) for s in shards],
        scratch_shapes=[pltpu.SemaphoreType.DMA((7 * n,)), pltpu.SemaphoreType.DMA((7 * n,)),
                        pltpu.SemaphoreType.DMA((n,))],
    )(*shards, *deps)


HBM_SPEC = pl.BlockSpec(memory_space=pltpu.HBM)
SEM_SPEC = pl.BlockSpec(memory_space=pltpu.SEMAPHORE)
SIDE_EFFECT = pltpu.SideEffectType.DATAFLOW_SIDE_EFFECTING


def _cast_place(name, ids, w, deps=(), dtype=None):
    dtype = BF16 if dtype is None else dtype
    R, C = w.shape
    tr = _tile(R, 256)

    def body(ids_ref, w_ref, *rest):
        rest[len(deps)][...] = w_ref[...].astype(dtype)

    return pl.pallas_call(
        body,
        name=name,
        grid_spec=pltpu.PrefetchScalarGridSpec(
            num_scalar_prefetch=1,
            grid=(R // tr,),
            in_specs=[pl.BlockSpec((tr, C), lambda r, ids: (r, 0))] + _hbm_specs(len(deps)),
            out_specs=pl.BlockSpec((None, tr, C), lambda r, ids: (ids[0], r, 0)),
        ),
        out_shape=jax.ShapeDtypeStruct((N_DEV, R, C), dtype),
        compiler_params=_params(1),
    )(ids, w, *deps)


def _xfer_start(name, bufs, plan, n):
    nb = len(bufs)

    def body(*refs):
        ins = refs[:nb]
        ssem, rsem = refs[nb], refs[nb + 1]
        token = refs[2 * nb + 2]
        x, y, c, chips = _place()
        for k, (src, dst, dev, _) in enumerate(plan(ins, x, y, c, chips)):
            _rcopy(src, dst, ssem.at[k], rsem.at[k], dev).start()
        token[...] = jnp.zeros_like(token)

    res = pl.pallas_call(
        body,
        name=name,
        out_shape=(pltpu.SemaphoreType.DMA((n,)), pltpu.SemaphoreType.DMA((n,)),
                   *[pltpu.HBM(b.shape, b.dtype) for b in bufs], jax.ShapeDtypeStruct((8, LANE), F32)),
        in_specs=[HBM_SPEC] * nb,
        out_specs=(SEM_SPEC, SEM_SPEC, *[HBM_SPEC] * nb, pl.BlockSpec(memory_space=pltpu.VMEM)),
        input_output_aliases={i: 2 + i for i in range(nb)},
        compiler_params=pltpu.CompilerParams(has_side_effects=SIDE_EFFECT),
    )(*[pltpu.with_memory_space_constraint(b, pltpu.HBM) for b in bufs])
    return dict(sems=res[:2], bufs=list(res[2:2 + nb]), token=res[2 + nb], plan=plan, n=n)


def _xfer_wait(name, started, after):
    bufs, plan = started["bufs"], started["plan"]
    nb = len(bufs)

    def body(*refs):
        ins = refs[:nb]
        ssem, rsem = refs[nb], refs[nb + 1]
        x, y, c, chips = _place()
        for k, (src, _, dev, land) in enumerate(plan(ins, x, y, c, chips)):
            cp = _rcopy(src, land, ssem.at[k], rsem.at[k], dev)
            cp.wait_send()
            cp.wait_recv()

    res = pl.pallas_call(
        body,
        name=name,
        out_shape=tuple(pltpu.HBM(b.shape, b.dtype) for b in bufs),
        in_specs=[HBM_SPEC] * nb + [SEM_SPEC, SEM_SPEC, pl.BlockSpec(memory_space=pl.ANY)],
        out_specs=tuple([HBM_SPEC] * nb),
        input_output_aliases={i: i for i in range(nb)},
        compiler_params=pltpu.CompilerParams(has_side_effects=SIDE_EFFECT),
    )(*bufs, *started["sems"], after)
    return list(res)


def _plan_gather_chips(refs, x, y, c, chips):
    plan = []
    for land in refs:
        mine = land.at[4 * x + 2 * y + c]
        plan.append((mine, mine, (x, y, 1 - c), land.at[4 * x + 2 * y + (1 - c)]))
        for px, py in chips:
            plan.append((mine, mine, (px, py, c), land.at[4 * px + 2 * py + c]))
    return plan


def _plan_gather_pass(refs, x, y, c, chips):
    return [(land.at[4 * px + 2 * py + c], land.at[4 * px + 2 * py + c], (x, y, 1 - c),
             land.at[4 * px + 2 * py + (1 - c)]) for land in refs for px, py in chips]


def _plan_gather_near(refs, x, y, c, chips):
    plan = []
    for land in refs:
        mine = land.at[4 * x + 2 * y + c]
        for px, py, pc in ((x, y, 1 - c), (1 - x, y, c), (x, 1 - y, c)):
            plan.append((mine, mine, (px, py, pc), land.at[4 * px + 2 * py + pc]))
    return plan


def _plan_gather_relay(refs, x, y, c, chips):
    south = c == 0
    src_x, src_y = jnp.where(south, 1 - x, x), jnp.where(south, y, 1 - y)
    dst_x, dst_y = jnp.where(south, x, 1 - x), jnp.where(south, 1 - y, y)
    plan = []
    for land in refs:
        for px, py in ((1 - x, y), (x, 1 - y)):
            plan.append((land.at[4 * px + 2 * py + c], land.at[4 * px + 2 * py + c], (x, y, 1 - c),
                         land.at[4 * px + 2 * py + (1 - c)]))
        blk = land.at[4 * src_x + 2 * src_y + c]
        plan.append((blk, blk, (dst_x, dst_y, c), land.at[4 * (1 - x) + 2 * (1 - y) + c]))
    return plan


def _plan_gather_far(refs, x, y, c, chips):
    return [(land.at[4 * (1 - x) + 2 * (1 - y) + c], land.at[4 * (1 - x) + 2 * (1 - y) + c], (x, y, 1 - c),
             land.at[4 * (1 - x) + 2 * (1 - y) + (1 - c)]) for land in refs]


def _plan_reduce_core(refs, x, y, c, chips):
    grad, recv = refs
    return [(grad.at[2 * q + (1 - c)], recv.at[q], (x, y, 1 - c), recv.at[q]) for q in range(N_CHIP)]


def _plan_reduce_chips(refs, x, y, c, chips):
    part, land = refs
    return [(part.at[2 * px + py], land.at[2 * x + y], (px, py, c), land.at[2 * px + py]) for px, py in chips]


def _chip_sums(name, ids, grad, recv):
    _, R, C = grad.shape
    tr = _tile(R, 256)

    def body(ids_ref, g_ref, r_ref, o_ref):
        o_ref[...] = (g_ref[...] + r_ref[...]).astype(BF16)

    return pl.pallas_call(
        body,
        name=name,
        grid_spec=pltpu.PrefetchScalarGridSpec(
            num_scalar_prefetch=1,
            grid=(N_CHIP - 1, R // tr),
            in_specs=[pl.BlockSpec((None, tr, C), lambda q, r, ids: (2 * ids[3 + q] + ids[2], r, 0)),
                      pl.BlockSpec((None, tr, C), lambda q, r, ids: (ids[3 + q], r, 0))],
            out_specs=pl.BlockSpec((None, tr, C), lambda q, r, ids: (ids[3 + q], r, 0)),
        ),
        out_shape=jax.ShapeDtypeStruct((N_CHIP, R, C), BF16),
        compiler_params=_params(2),
    )(ids, grad, recv)


def _adamw(w, g, m, v):
    m = ADAM_B1 * m + (1.0 - ADAM_B1) * g
    v = ADAM_B2 * v + (1.0 - ADAM_B2) * (g * g)
    m_hat = m / (1.0 - ADAM_B1 ** ADAM_STEP)
    v_hat = v / (1.0 - ADAM_B2 ** ADAM_STEP)
    delta = -ADAM_LR * (m_hat / (jnp.sqrt(v_hat) + ADAM_EPS) + ADAM_WD * w)
    return delta, m, v


def _reduce_adamw(name, ids, grad, recv, landed, w, m, v):
    _, R, C = grad.shape
    tr = _tile(R, 128)

    def body(ids_ref, g_ref, r_ref, l1_ref, l2_ref, l3_ref, w_ref, m_ref, v_ref, go_ref, do_ref, mo_ref, vo_ref):
        g = g_ref[...] + r_ref[...]
        g = g + l1_ref[...].astype(F32)
        g = g + l2_ref[...].astype(F32)
        g = g + l3_ref[...].astype(F32)
        delta, mn, vn = _adamw(w_ref[...], g, m_ref[...], v_ref[...])
        go_ref[...] = g
        do_ref[...] = delta
        mo_ref[...] = mn
        vo_ref[...] = vn

    def pick(k):
        return pl.BlockSpec((None, tr, C), lambda r, ids: (ids[k], r, 0))

    flat = pl.BlockSpec((tr, C), lambda r, ids: (r, 0))
    shp = jax.ShapeDtypeStruct((R, C), F32)
    return pl.pallas_call(
        body,
        name=name,
        grid_spec=pltpu.PrefetchScalarGridSpec(
            num_scalar_prefetch=1,
            grid=(R // tr,),
            in_specs=[pick(0), pick(1), pick(3), pick(4), pick(5), flat, flat, flat],
            out_specs=[flat, flat, flat, flat],
        ),
        out_shape=[shp, shp, shp, shp],
        compiler_params=_params(1),
    )(ids, grad, recv, landed, landed, landed, w, m, v)


SMALL = ("conv_w", "conv_norm_g", "w_gate_up", "gate_bias", "gla_norm_g", "ln1_g", "ln1_b", "ln2_g", "ln2_b")
R_LOSS = 14


def _small_rows(D, conv_cols, gate_cols):
    nv = max(1, D // SP_COLS)
    assert nv <= 2, D
    return {"conv_w": (0, 3, conv_cols), "conv_norm_g": (3, 1, D_CONV), "gate_bias": (4, 1, D_GLA_K),
            "gla_norm_g": (5, 1, D_GLA_V), "ln1_g": (6, nv, D), "ln1_b": (8, nv, D), "ln2_g": (10, nv, D),
            "ln2_b": (12, nv, D), "w_gate_up": (16, GATE_RANK, gate_cols)}


def _put(o_ref, entry, val):
    row, n_rows, cols = entry
    if val.shape[0] == 1 and n_rows > 1:
        for r in range(n_rows):
            o_ref[row + r:row + r + 1, :] = val[:, r * SP_COLS:(r + 1) * SP_COLS]
    else:
        o_ref[row:row + n_rows, 0:cols] = val[0:n_rows, 0:cols]


def _take(g, entry):
    row, n_rows, cols = entry
    if cols > SP_COLS:
        return jnp.concatenate([g[row + r:row + r + 1, :] for r in range(n_rows)], axis=1)
    return g[row:row + n_rows, 0:cols]


def _make_pack(name, rows, pieces):
    names = list(pieces)

    def body(*refs):
        o_ref = refs[len(names)]
        o_ref[...] = jnp.zeros_like(o_ref)
        for nm, ref in zip(names, refs):
            if nm == "loss":
                o_ref[R_LOSS:R_LOSS + 1, 0:1] = jnp.sum(ref[...], axis=1, keepdims=True)
            else:
                _put(o_ref, rows[nm], ref[...])

    arrs = [pieces[nm] for nm in names]
    return pl.pallas_call(
        body,
        name=name,
        grid=(1,),
        in_specs=[_full(a.shape) for a in arrs],
        out_specs=_full((SP_ROWS, SP_COLS)),
        out_shape=jax.ShapeDtypeStruct((SP_ROWS, SP_COLS), F32),
        compiler_params=_params(1),
    )(*arrs)


def _small_adamw(packs, rows, w, m, v):
    names = list(SMALL)
    n = len(names)

    def body(p_ref, *refs):
        ins, outs = refs[:3 * n], refs[3 * n:]
        g = p_ref[0]
        for dvc in range(1, N_DEV):
            g = g + p_ref[dvc]
        for i, nm in enumerate(names):
            gp = _take(g, rows[nm])
            delta, mn, vn = _adamw(ins[i][...], gp, ins[n + i][...], ins[2 * n + i][...])
            for kind, val in enumerate((gp, delta, mn, vn)):
                outs[kind * n + i][...] = val
        outs[4 * n][...] = g[R_LOSS:R_LOSS + 1, 0:1]

    arrs = [w[nm] for nm in names] + [m[nm] for nm in names] + [v[nm] for nm in names]
    shapes = [jax.ShapeDtypeStruct(w[nm].shape, F32) for nm in names] * 4 + [jax.ShapeDtypeStruct((1, 1), F32)]
    res = pl.pallas_call(
        body,
        name="small_adamw",
        grid=(1,),
        in_specs=[_full(packs.shape)] + [_full(a.shape) for a in arrs],
        out_specs=[_full(sh.shape) for sh in shapes],
        out_shape=shapes,
        compiler_params=_params(1),
    )(packs, *arrs)
    return [dict(zip(names, res[k * n:(k + 1) * n])) for k in range(4)], res[4 * n]


def _w_in_pieces():
    cs = D_IN_PROJ // N_DEV
    pieces = []
    for d in range(N_DEV):
        lo, hi = d * cs, (d + 1) * cs
        if hi <= CONV_COLS:
            pieces.append((d, 0, cs, lo))
        elif lo >= CONV_COLS:
            pieces.append((d, 0, cs, lo - CONV_COLS + HALF_P))
        else:
            pieces.append((d, 0, CONV_COLS - lo, lo))
            pieces.append((d, CONV_COLS - lo, cs, HALF_P))
    return pieces


def _w_in_full(gathered):
    nb, D, cs = gathered.shape
    tr = _tile(D, 256)

    def body(g_ref, o_ref):
        o_ref[:, CONV_COLS:HALF_P] = jnp.zeros((tr, HALF_P - CONV_COLS), o_ref.dtype)
        o_ref[:, HALF_P + GLA_COLS:P_INT] = jnp.zeros((tr, HALF_P - GLA_COLS), o_ref.dtype)
        for d, a, b, dst in _w_in_pieces():
            o_ref[:, dst:dst + (b - a)] = g_ref[d, :, a:b]

    return pl.pallas_call(
        body,
        name="w_in_full",
        grid=(D // tr,),
        in_specs=[pl.BlockSpec((nb, tr, cs), lambda r: (0, r, 0))],
        out_specs=pl.BlockSpec((tr, P_INT), lambda r: (r, 0)),
        out_shape=jax.ShapeDtypeStruct((D, P_INT), gathered.dtype),
        compiler_params=_params(1),
    )(gathered)


def _w_in_blocks(dw):
    D = dw.shape[0]
    cs = D_IN_PROJ // N_DEV
    tr = _tile(D, 256)

    def body(w_ref, o_ref):
        for d, a, b, src in _w_in_pieces():
            o_ref[d, :, a:b] = w_ref[:, src:src + (b - a)]

    return pl.pallas_call(
        body,
        name="w_in_blocks",
        grid=(D // tr,),
        in_specs=[pl.BlockSpec((tr, P_INT), lambda r: (r, 0))],
        out_specs=pl.BlockSpec((N_DEV, tr, cs), lambda r: (0, r, 0)),
        out_shape=jax.ShapeDtypeStruct((N_DEV, D, cs), dw.dtype),
        compiler_params=_params(1),
    )(dw)


BIG = ("w_in", "w_out", "w_ff_up", "w_ff_down")
ORDER = ("w_in", "conv_w", "conv_norm_g", "w_gate_up", "gate_bias", "gla_norm_g", "w_out", "ln1_g", "ln1_b",
         "w_ff_up", "w_ff_down", "ln2_g", "ln2_b")


def kernel(x, w_in, conv_w, conv_norm_g, w_gate_up, gate_bias, gla_norm_g, w_out, ln1_g, ln1_b, w_ff_up, w_ff_down, ln2_g, ln2_b, loss_target, m_w_in, m_conv_w, m_conv_norm_g, m_w_gate_up, m_gate_bias, m_gla_norm_g, m_w_out, m_ln1_g, m_ln1_b, m_w_ff_up, m_w_ff_down, m_ln2_g, m_ln2_b, v_w_in, v_conv_w, v_conv_norm_g, v_w_gate_up, v_gate_bias, v_gla_norm_g, v_w_out, v_ln1_g, v_ln1_b, v_w_ff_up, v_w_ff_down, v_ln2_g, v_ln2_b):
    T, D = x.shape[1], x.shape[2]
    xs, target = x[0], loss_target[0]
    xi, yi, ci = lax.axis_index("x"), lax.axis_index("y"), lax.axis_index("c")
    chip = 2 * xi + yi
    dev = 2 * chip + ci
    others = [jnp.where(chip <= q, q + 1, q) for q in range(N_CHIP - 1)]
    ids = jnp.stack([dev, chip, ci] + others).astype(jnp.int32)
    conv_cols, gate_cols = conv_w.shape[2], w_gate_up.shape[2]

    def gather(nm, lands):
        return _xfer_start("gather_chips_" + nm, lands, _plan_gather_chips, 4 * len(lands))

    def pass_on(nm, started, after):
        lands = _xfer_wait("gather_chips_wait_" + nm, started, after)
        return _xfer_start("gather_pass_" + nm, lands, _plan_gather_pass, 3 * len(lands))

    def landed(nm, started, after):
        return _xfer_wait("gather_pass_wait_" + nm, started, after)

    rows = _small_rows(D, conv_cols, gate_cols)
    fwd_pack = _make_pack("pack_fwd", rows, {"conv_w": conv_w[0], "w_gate_up": w_gate_up[0]})
    lands_in = [_cast_place("cast_place_w_in", ids, w_in[0]), _cast_place("cast_place_pack", ids, fwd_pack, dtype=F32)]
    near_in = _xfer_start("gather_near_w_in", lands_in, _plan_gather_near, 3 * len(lands_in))
    m_in, v_in = m_w_in[0], v_w_in[0]
    xb = _cast_bf16(xs, [near_in["token"]])
    lands_in = _xfer_wait("gather_near_wait_w_in", near_in, xb)
    relay_in = _xfer_start("gather_relay_w_in", lands_in, _plan_gather_relay, 3 * len(lands_in))
    others, dep = [], relay_in["token"]
    for nm, w in zip(BIG[1:], (w_out, w_ff_up, w_ff_down)):
        others.append(_cast_place("cast_place_" + nm, ids, w[0], [dep, m_in, v_in] if nm == "w_out" else [dep]))
        dep = others[-1]
    lands_in = _xfer_wait("gather_relay_wait_w_in", relay_in, dep)
    far_in = _xfer_start("gather_far_w_in", lands_in, _plan_gather_far, len(lands_in))
    ga = [gather(nm, [land]) for nm, land in zip(BIG[1:], others)]
    g_in, g_pack = _xfer_wait("gather_far_wait_w_in", far_in, ga[-1]["token"])
    w_full = _w_in_full(g_in)
    r_cw, r_gw = rows["conv_w"][0], rows["w_gate_up"][0]
    conv_w_full = g_pack[:, r_cw:r_cw + 3, :conv_cols].transpose(1, 0, 2).reshape(3, -1)
    gate_w_full = g_pack[:, r_gw:r_gw + GATE_RANK, :gate_cols].transpose(1, 0, 2).reshape(GATE_RANK, -1)
    conv_w8 = jnp.pad(conv_w_full, ((0, 5), (0, 0)))
    wg128 = jnp.pad(gate_w_full, ((0, LANE - GATE_RANK), (0, 0))).astype(BF16)
    proj = _proj_fwd(xb, w_full)
    yin = _conv_fwd(proj, conv_w8, conv_norm_g)
    gp_out = pass_on("w_out", ga[0], yin)
    o_all, states, yin = _gla_fwd(proj, wg128, gate_bias, gla_norm_g, yin, deps=[gp_out["token"]])
    w_out_full = landed("w_out", gp_out, o_all)[0].reshape(-1, D)
    gp_up = pass_on("w_ff_up", ga[1], o_all)
    xhat1, x1, rstd1 = _mix_ln1(yin, w_out_full, xs, ln1_g, ln1_b, deps=[gp_up["token"]])
    (w_up_blk,) = landed("w_ff_up", gp_up, x1)
    half = N_DEV // 2
    ra, h2 = _ff_up(x1, w_up_blk, 0, half)
    gp_down = pass_on("w_ff_down", ga[2], ra)
    ra, h2 = _ff_up(x1, w_up_blk, half, N_DEV - half, prev=(ra, h2), deps=[gp_down["token"]])
    w_down_full = landed("w_ff_down", gp_down, ra)[0].reshape(-1, D)
    dh3, dh3b, g_ln2_g, g_ln2_b, loss = _ff_down_loss(h2, w_down_full, xhat1, target, ln1_g, ln1_b, ln2_g, ln2_b)

    def to_core(nm, grad):
        recv = lax.empty((N_CHIP,) + grad.shape[1:], F32)
        return _xfer_start("reduce_core_" + nm, [grad, recv], _plan_reduce_core, N_CHIP)

    def to_chips(nm, started, after):
        grad, recv = _xfer_wait("reduce_core_wait_" + nm, started, after)
        part = _chip_sums("chip_sums_" + nm, ids, grad, recv)
        land = lax.empty(part.shape, BF16)
        return grad, recv, _xfer_start("reduce_chips_" + nm, [part, land], _plan_reduce_chips, N_CHIP - 1)

    da = _ff_down_bwd_act(dh3b, w_down_full, ra)
    gw_down = _grad_w("grad_w_down", h2, dh3b).reshape(N_DEV, -1, D)
    rc_down = to_core("w_ff_down", gw_down)
    gw_up = _grad_w_up_blk(x1, da, N_DEV, deps=[rc_down["token"]])
    gw_down, rv_down, rs_down = to_chips("w_ff_down", rc_down, gw_up)
    rc_up = to_core("w_ff_up", gw_up)
    dh1, dh1b, g_ln1_g, g_ln1_b = _ff_up_bwd_ln1(da, w_up_blk, dh3, xhat1, rstd1, ln1_g,
                                                 deps=[rs_down["token"], rc_up["token"]])
    gw_up, rv_up, rs_up = to_chips("w_ff_up", rc_up, dh1b)
    dyin = _mix_bwd(dh1b, w_out_full, deps=[rs_up["token"]])
    gw_out = _grad_w("grad_w_out", yin, dh1b).reshape(N_DEV, -1, D)
    rc_out = to_core("w_out", gw_out)
    dproj, g_conv_w, g_conv_g = _conv_bwd(proj, dyin, conv_w8, conv_norm_g, deps=[rc_out["token"]])
    dproj, g_gate_w, g_gate_b, g_gla_g = _gla_bwd(proj, wg128, gate_bias, gla_norm_g, o_all, states, dyin, dproj)
    gw_out, rv_out, rs_out = to_chips("w_out", rc_out, dproj)
    gw_in = _w_in_blocks(_grad_w("grad_w_in", xb, dproj, tn_pref=1280, tk_pref=2048, deps=[rs_out["token"]]))
    rc_in = to_core("w_in", gw_in)

    big = {}

    def finish(nm, grad, recv, started, w, m, v, after):
        _, land = _xfer_wait("reduce_chips_wait_" + nm, started, after)
        res = _reduce_adamw("adamw_" + nm, ids, grad, recv, land, w[0], m[0], v[0])
        big[nm] = [a[None] for a in res]
        return res[0]

    done = finish("w_ff_down", gw_down, rv_down, rs_down, w_ff_down, m_w_ff_down, v_w_ff_down, rc_in["token"])
    done = finish("w_ff_up", gw_up, rv_up, rs_up, w_ff_up, m_w_ff_up, v_w_ff_up, done)
    full_rows = _small_rows(D, D_CONV, D_GLA_K)
    pack = _make_pack("pack_grads", full_rows, {
        "conv_w": g_conv_w, "conv_norm_g": g_conv_g, "gate_bias": g_gate_b, "gla_norm_g": g_gla_g, "ln1_g": g_ln1_g,
        "ln1_b": g_ln1_b, "ln2_g": g_ln2_g, "ln2_b": g_ln2_b, "loss": loss, "w_gate_up": g_gate_w})
    (packs,) = _all_gather("gather_small_grads", [pack], deps=[done])
    gw_in, rv_in, rs_in = to_chips("w_in", rc_in, packs)
    done = finish("w_out", gw_out, rv_out, rs_out, w_out, m_w_out, v_w_out, rs_in["token"])
    grad_x = _proj_bwd_x(dproj, w_full, dh1, deps=[done])
    finish("w_in", gw_in, rv_in, rs_in, w_in, (m_in,), (v_in,), grad_x)

    def own_cols(row, n_rows, width):
        cut = lax.dynamic_slice(packs, (0, row, dev * width), (N_DEV, n_rows, width))
        return jnp.pad(cut, ((0, 0), (0, 0), (0, SP_COLS - width)))

    packs_own = jnp.concatenate([own_cols(r_cw, 3, conv_cols), packs[:, r_cw + 3:r_gw],
                                 own_cols(r_gw, GATE_RANK, gate_cols)], axis=1)
    as2d = lambda a: a[0] if a.ndim == 3 else a
    w_s = dict(zip(SMALL, map(as2d, (conv_w, conv_norm_g, w_gate_up, gate_bias, gla_norm_g, ln1_g, ln1_b, ln2_g, ln2_b))))
    m_s = dict(zip(SMALL, map(as2d, (m_conv_w, m_conv_norm_g, m_w_gate_up, m_gate_bias, m_gla_norm_g, m_ln1_g,
                                     m_ln1_b, m_ln2_g, m_ln2_b))))
    v_s = dict(zip(SMALL, map(as2d, (v_conv_w, v_conv_norm_g, v_w_gate_up, v_gate_bias, v_gla_norm_g, v_ln1_g,
                                     v_ln1_b, v_ln2_g, v_ln2_b))))
    small, loss_sum = _small_adamw(packs_own, rows, w_s, m_s, v_s)

    def leaf(kind, name):
        if name in BIG:
            return big[name][kind]
        a = small[kind][name]
        return a[None] if name in ("conv_w", "w_gate_up") else a

    out = [loss_sum[0, 0], grad_x[None]]
    for kind in range(4):
        out += [leaf(kind, nm) for nm in ORDER]
    return tuple(out)
```

```python
import jax
import jax.numpy as jnp
from jax import lax
from jax.experimental import pallas as pl
from jax.experimental.pallas import tpu as pltpu

F32 = jnp.float32
BF16 = jnp.bfloat16

D_CONV = 1024
CONV_GROUPS = 8
GLA_HEADS = 4
HEAD_K = 128
HEAD_V = 256
D_GLA_K = 512
D_GLA_V = 1024
GATE_RANK = 16
GATE_TAU = 16.0
CHUNK = 64
LN_EPS = 1e-5
RMS_EPS = 1e-6
DN_ALPHA = 2.0 ** 0.25
D_IN_PROJ = 6160
ADAM_LR = 0.001
ADAM_B1 = 0.9
ADAM_B2 = 0.999
ADAM_EPS = 1e-08
ADAM_WD = 0.01
ADAM_STEP = 10

N_DEV = 8
N_CHIP = 4
LANE = 128
HALF_P = 3200
P_INT = 2 * HALF_P
CONV_COLS = 3 * D_CONV
GLA_COLS = D_IN_PROJ - CONV_COLS
SP_ROWS = 32
SP_COLS = 1024
VMEM_LIMIT = 56 * 1024 * 1024

NN = ((1,), (0,))
NT = ((1,), (1,))
TN = ((0,), (0,))
MESH = pl.DeviceIdType.MESH


def _dot(a, b, dims, precision=None):
    return lax.dot_general(a, b, (dims, ((), ())), preferred_element_type=F32, precision=precision)


def _tile(n, pref):
    if n <= pref:
        return n
    t = (pref // LANE) * LANE
    while t > 0 and n % t:
        t -= LANE
    assert t > 0, (n, pref)
    return t


def _params(n_axes):
    return pltpu.CompilerParams(dimension_semantics=("arbitrary",) * n_axes, vmem_limit_bytes=VMEM_LIMIT)


def _full(shape):
    nd = len(shape)
    return pl.BlockSpec(shape, lambda *_: (0,) * nd)


def _hbm_specs(n):
    return [pl.BlockSpec(memory_space=pl.ANY)] * n


def _mm(name, mode, a, b, *, M, N, K, tm, tn, tk, outs, epilogue, extras=(), a_fn=None, a_spec=None, b_spec=None,
        deps=()):
    ni, nj, nk = M // tm, N // tn, K // tk
    assert ni * tm == M and nj * tn == N and nk * tk == K, (name, M, N, K, tm, tn, tk)
    if a_spec is None:
        a_spec = (pl.BlockSpec((tk, tm), lambda i, j, k: (k, i)) if mode == "tn"
                  else pl.BlockSpec((tm, tk), lambda i, j, k: (i, k)))
    if b_spec is None:
        b_spec = (pl.BlockSpec((tn, tk), lambda i, j, k: (j, k)) if mode == "nt"
                  else pl.BlockSpec((tk, tn), lambda i, j, k: (k, j)))
    dims = {"nn": NN, "nt": NT, "tn": TN}[mode]
    n_ex, n_out, n_dep = len(extras), len(outs), len(deps)

    def body(*refs):
        a_ref, b_ref = refs[0], refs[1]
        ex = refs[2:2 + n_ex]
        o = refs[2 + n_ex + n_dep:2 + n_ex + n_dep + n_out]
        acc_ref = refs[2 + n_ex + n_dep + n_out]
        i, j, k = pl.program_id(0), pl.program_id(1), pl.program_id(2)
        if nk > 1:
            @pl.when(k == 0)
            def _():
                acc_ref[...] = jnp.zeros_like(acc_ref)

        av = a_ref[...]
        if a_fn is not None:
            av = a_fn(av)
        part = _dot(av, b_ref[...], dims)
        if nk == 1 and epilogue is None:
            o[0][...] = part.astype(o[0].dtype)
        elif nk == 1:
            acc_ref[...] = part
            epilogue(acc_ref, ex, o, i, j)
        else:
            acc_ref[...] += part

            @pl.when(k == nk - 1)
            def _():
                if epilogue is None:
                    o[0][...] = acc_ref[...].astype(o[0].dtype)
                else:
                    epilogue(acc_ref, ex, o, i, j)

    return pl.pallas_call(
        body,
        name=name,
        grid=(ni, nj, nk),
        in_specs=[a_spec, b_spec] + [s for _, s in extras] + _hbm_specs(n_dep),
        out_specs=[s for _, s in outs],
        out_shape=[s for s, _ in outs],
        scratch_shapes=[pltpu.VMEM((8, LANE) if nk == 1 and epilogue is None else (tm, tn), F32)],
        compiler_params=_params(3),
    )(a, b, *[x for x, _ in extras], *deps)


def _mm_rows(name, mode, a, b, *, M, N, K, tm, tk, row_ins, vec_ins, row_outs, stat_outs, chunk_fn,
             b_spec=None, deps=()):
    ni, nk = M // tm, K // tk
    rc = tm // nk
    assert ni * tm == M and nk * tk == K and rc * nk == tm and rc % 16 == 0, (name, M, K, tm, tk)
    dims = {"nn": NN, "nt": NT}[mode]
    last = ni - 1

    def kk(i, k):
        return jnp.where(i < ni, k, nk - 1)

    a_spec = pl.BlockSpec((tm, tk), lambda i, k: (jnp.minimum(i, last), kk(i, k)))
    if b_spec is None:
        b_spec = (pl.BlockSpec((N, tk), lambda i, k: (0, kk(i, k))) if mode == "nt"
                  else pl.BlockSpec((tk, N), lambda i, k: (kk(i, k), 0)))
    prev_rows = lambda i, k: (jnp.maximum((i - 1) * nk + k, 0), 0)
    n_ri, n_vi, n_ro, n_so, n_dep = len(row_ins), len(vec_ins), len(row_outs), len(stat_outs), len(deps)

    def body(*refs):
        a_ref, b_ref = refs[0], refs[1]
        pos = 2
        ri = refs[pos:pos + n_ri]; pos += n_ri
        vi = refs[pos:pos + n_vi]; pos += n_vi + n_dep
        ro = refs[pos:pos + n_ro]; pos += n_ro
        so = refs[pos:pos + n_so]; pos += n_so
        accs = refs[pos:pos + 2]
        i, k = pl.program_id(0), pl.program_id(1)

        @pl.when((i == 0) & (k == 0))
        def _():
            accs[0][...] = jnp.zeros_like(accs[0])
            accs[1][...] = jnp.zeros_like(accs[1])
            for st in so:
                st[...] = jnp.zeros_like(st)

        def finish_rows(prev_ref):
            rows = pl.ds(pl.multiple_of(k * rc, rc), rc)
            done = prev_ref[rows, :]
            prev_ref[rows, :] = jnp.zeros((rc, N), F32)
            chunk_fn(done, i > 0, ri, vi, ro, so)

        def accumulate(acc_ref, after_ref):
            rp = tm // ROW_PARTS
            bv = b_ref[...]
            for part in range(ROW_PARTS):
                av = a_ref[part * rp:(part + 1) * rp, :]
                if part == 1:
                    tail = after_ref[rc - 16:rc, 0:LANE].astype(F32)
                    sixteen = jnp.uint32(16)
                    zero = lax.bitcast_convert_type(tail, jnp.uint32)
                    zero = lax.shift_right_logical(lax.shift_right_logical(zero, sixteen), sixteen)
                    av = av + jnp.tile(zero.astype(F32).astype(av.dtype), (rp // 16, tk // LANE))
                acc_ref[part * rp:(part + 1) * rp, :] += _dot(av, bv, dims)

        for parity in (0, 1):
            @pl.when((i < ni) & (lax.rem(i, 2) == parity))
            def _(parity=parity):
                finish_rows(accs[1 - parity])
                accumulate(accs[parity], ro[0])

        @pl.when(i == ni)
        def _():
            finish_rows(accs[last % 2])

    row_spec = lambda arr: pl.BlockSpec((rc, arr.shape[1]), prev_rows)
    return pl.pallas_call(
        body,
        name=name,
        grid=(ni + 1, nk),
        in_specs=[a_spec, b_spec] + [row_spec(x) for x in row_ins] + [_full(x.shape) for x in vec_ins]
        + _hbm_specs(n_dep),
        out_specs=[row_spec(s) for s in row_outs] + [_full(s.shape) for s in stat_outs],
        out_shape=list(row_outs) + list(stat_outs),
        scratch_shapes=[pltpu.VMEM((tm, N), F32), pltpu.VMEM((tm, N), F32)],
        compiler_params=_params(2),
    )(a, b, *row_ins, *vec_ins, *deps)


ROW_PARTS = 2
SUB_ROWS = 32


def _by_sub_rows(n_rows, fn):
    sums = None
    for r0 in range(0, n_rows, SUB_ROWS):
        part = fn(slice(r0, r0 + SUB_ROWS))
        if part:
            sums = part if sums is None else tuple(x + y for x, y in zip(sums, part))
    return sums


def _to_bf16(v):
    return v.astype(BF16)


def _ln_bwd(dy, xhat, rstd, g):
    dxh = dy * g
    m1 = jnp.mean(dxh, axis=-1, keepdims=True)
    m2 = jnp.mean(dxh * xhat, axis=-1, keepdims=True)
    return rstd * (dxh - m1 - xhat * m2)


def _ln_fwd(h):
    mu = jnp.mean(h, axis=-1, keepdims=True)
    xc = h - mu
    var = jnp.mean(xc * xc, axis=-1, keepdims=True)
    rstd = lax.rsqrt(var + LN_EPS)
    return xc * rstd, rstd


def _proj_fwd(x, w_full, deps=()):
    T, D = x.shape
    P = w_full.shape[1]
    tm, tn = _tile(T, 1024), _tile(P, 1280)
    return _mm("proj_fwd", "nn", x, w_full, M=T, N=P, K=D, tm=tm, tn=tn, tk=D,
               outs=[(jax.ShapeDtypeStruct((T, P), F32), pl.BlockSpec((tm, tn), lambda i, j, k: (i, j)))],
               epilogue=None, deps=deps)[0]


def _cast_bf16(x, deps=()):
    T, D = x.shape
    tm = _tile(T, 512)

    def body(x_ref, *rest):
        rest[len(deps)][...] = x_ref[...].astype(BF16)

    return pl.pallas_call(
        body,
        name="cast_x",
        grid=(T // tm,),
        in_specs=[pl.BlockSpec((tm, D), lambda i: (i, 0))] + _hbm_specs(len(deps)),
        out_specs=pl.BlockSpec((tm, D), lambda i: (i, 0)),
        out_shape=jax.ShapeDtypeStruct((T, D), BF16),
        compiler_params=_params(1),
    )(x, *deps)


def _conv_shift(h, hp):
    row = lax.broadcasted_iota(jnp.int32, h.shape, 0)
    hm1 = hp[7:8, :]
    hm2 = hp[6:7, :]
    h1 = jnp.where(row == 0, hm1, pltpu.roll(h, 1, 0))
    h2 = jnp.where(row == 0, hm2, jnp.where(row == 1, hm1, pltpu.roll(h, 2, 0)))
    return h1, h2


def _conv_fwd(proj, conv_w8, conv_g):
    T = proj.shape[0]
    tt = _tile(T, 256)
    nt = T // tt
    t8 = tt // 8

    def body(b_ref, c_ref, u_ref, cp_ref, up_ref, w_ref, g_ref, yin_ref):
        i = pl.program_id(0)
        h = c_ref[...] * u_ref[...]
        hp = jnp.where(i > 0, cp_ref[...] * up_ref[...], 0.0)
        h1, h2 = _conv_shift(h, hp)
        w = w_ref[...]
        y = w[0:1, :] * h2 + w[1:2, :] * h1 + w[2:3, :] * h
        p = b_ref[...] * y
        parts = []
        for gi in range(CONV_GROUPS):
            pg = p[:, gi * LANE:(gi + 1) * LANE]
            r = lax.rsqrt(jnp.mean(pg * pg, axis=-1, keepdims=True) + RMS_EPS)
            parts.append(pg * r)
        yn = jnp.concatenate(parts, axis=1) * g_ref[...]
        yin_ref[...] = yn.astype(BF16)

    def col(cidx):
        return pl.BlockSpec((tt, D_CONV), lambda i: (i, cidx))

    def prev(cidx):
        return pl.BlockSpec((8, D_CONV), lambda i: (jnp.maximum(i * t8 - 1, 0), cidx))

    return pl.pallas_call(
        body,
        name="conv_fwd",
        grid=(nt,),
        in_specs=[col(0), col(1), col(2), prev(1), prev(2), _full((8, D_CONV)), _full((1, D_CONV))],
        out_specs=pl.BlockSpec((tt, D_CONV), lambda i: (i, 0)),
        out_shape=jax.ShapeDtypeStruct((T, 2 * D_CONV), BF16),
        compiler_params=_params(1),
    )(proj, proj, proj, proj, proj, conv_w8, conv_g)


def _log_sigmoid(z):
    return jnp.minimum(z, 0.0) - jnp.log(1.0 + jnp.exp(-jnp.abs(z)))


STEP_CHUNKS = 4
STEP_ROWS = STEP_CHUNKS * CHUNK


def _gla_step_terms(blk, wg_ref, gb_ref):
    zl = blk[:, 3072:3200]
    z = _dot(zl.astype(BF16), wg_ref[...], NN) + gb_ref[...]
    log_a = _log_sigmoid(z) * (1.0 / GATE_TAU)
    ri = lax.broadcasted_iota(jnp.int32, (STEP_ROWS, STEP_ROWS), 0)
    ci = lax.broadcasted_iota(jnp.int32, (STEP_ROWS, STEP_ROWS), 1)
    same = (ri // CHUNK) == (ci // CHUNK)
    lower = (same & (ri >= ci)).astype(F32)
    bcum = _dot(lower, log_a, NN, precision=lax.Precision.HIGHEST)
    return zl, z, bcum, same


def _causal():
    return (lax.broadcasted_iota(jnp.int32, (CHUNK, CHUNK), 0) >= lax.broadcasted_iota(jnp.int32, (CHUNK, CHUNK), 1))


def _gla_head_terms(q, k, bcum, h):
    sl = slice(h * HEAD_K, (h + 1) * HEAD_K)
    bh = bcum[:, sl]
    bl = bh[CHUNK - 1:CHUNK, :]
    eb = jnp.exp(bh)
    enb = jnp.exp(-bh)
    eend = jnp.exp(bl - bh)
    dec = jnp.exp(bl)
    qd = q[:, sl] * (HEAD_K ** -0.5) * eb
    ki = k[:, sl] * enb
    ke = k[:, sl] * eend
    return eb, enb, eend, dec, qd, ki, ke


def _sigmoid(x):
    return 1.0 / (1.0 + jnp.exp(-x))


def _gla_fwd(proj, wg128, gbias, gng, yin, deps=()):
    T = proj.shape[0]
    nch = T // CHUNK
    nst = T // STEP_ROWS

    def body(p_ref, wg_ref, gb_ref, gn_ref, yin_in_ref, *rest):
        o_ref, st_ref, yin_ref, s_ref = rest[len(deps):]
        n = pl.program_id(0)

        @pl.when(n == 0)
        def _():
            s_ref[...] = jnp.zeros_like(s_ref)

        blk = p_ref[...]
        _, _, bcum_all, _ = _gla_step_terms(blk, wg_ref, gb_ref)
        causal = _causal()
        gn = gn_ref[...]
        states = [s_ref[h] for h in range(GLA_HEADS)]
        for c in range(STEP_CHUNKS):
            rows = slice(c * CHUNK, (c + 1) * CHUNK)
            q, k = blk[rows, 0:512], blk[rows, 512:1024]
            v, r = blk[rows, 1024:2048], blk[rows, 2048:3072]
            bcum = bcum_all[rows, :]
            for h in range(GLA_HEADS):
                _, _, _, dec, qd, ki, ke = _gla_head_terms(q, k, bcum, h)
                vs = slice(h * HEAD_V, (h + 1) * HEAD_V)
                vb = v[:, vs].astype(BF16)
                qdb = qd.astype(BF16)
                a = jnp.where(causal, _dot(qdb, ki.astype(BF16), NT), 0.0)
                st = states[h]
                o = _dot(a.astype(BF16), vb, NN) + _dot(qdb, st.astype(BF16), NT)
                st_ref[c, h] = st
                states[h] = dec * st + _dot(vb, ke.astype(BF16), TN)
                o_ref[rows, vs] = o
                rinv = lax.rsqrt(jnp.mean(o * o, axis=-1, keepdims=True) + RMS_EPS)
                rh = r[:, vs]
                yin_ref[rows, vs] = (o * rinv * gn[:, vs] * (rh * _sigmoid(rh))).astype(BF16)
        for h in range(GLA_HEADS):
            s_ref[h] = states[h]

    return pl.pallas_call(
        body,
        name="gla_fwd",
        grid=(nst,),
        in_specs=[pl.BlockSpec((STEP_ROWS, HALF_P), lambda n: (n, 1)), _full((LANE, D_GLA_K)), _full((1, D_GLA_K)),
                  _full((1, D_GLA_V)), pl.BlockSpec(memory_space=pl.ANY)] + _hbm_specs(len(deps)),
        out_specs=[pl.BlockSpec((STEP_ROWS, D_GLA_V), lambda n: (n, 0)),
                   pl.BlockSpec((STEP_CHUNKS, GLA_HEADS, HEAD_V, HEAD_K), lambda n: (n, 0, 0, 0)),
                   pl.BlockSpec((STEP_ROWS, D_GLA_V), lambda n: (n, 1))],
        out_shape=[jax.ShapeDtypeStruct((T, D_GLA_V), F32),
                   jax.ShapeDtypeStruct((nch, GLA_HEADS, HEAD_V, HEAD_K), F32),
                   jax.ShapeDtypeStruct(yin.shape, BF16)],
        scratch_shapes=[pltpu.VMEM((GLA_HEADS, HEAD_V, HEAD_K), F32)],
        input_output_aliases={4: 2},
        compiler_params=_params(1),
    )(proj, wg128, gbias, gng, yin, *deps)


def _mix_ln1(yin, w_out, x, ln_g, ln_b, deps=()):
    T, D = x.shape
    KY = yin.shape[1]
    tm = _tile(T, 1024)

    def chunk(acc, valid, ri, vi, ro, so):
        g, b = vi[0][...], vi[1][...]

        def sub(rows):
            xhat, rstd = _ln_fwd(DN_ALPHA * ri[0][rows, :] + acc[rows, :])
            ro[0][rows, :] = xhat
            ro[1][rows, :] = (xhat * g + b).astype(BF16)
            ro[2][rows, :] = rstd

        _by_sub_rows(acc.shape[0], sub)

    return _mm_rows("mix_ln1", "nn", yin, w_out, M=T, N=D, K=KY, tm=tm, tk=_tile(KY, 512),
                    row_ins=[x], vec_ins=[ln_g, ln_b],
                    row_outs=[jax.ShapeDtypeStruct((T, D), F32), jax.ShapeDtypeStruct((T, D), BF16),
                              jax.ShapeDtypeStruct((T, 1), F32)],
                    stat_outs=[], chunk_fn=chunk, deps=deps)


def _ff_up(x1, w_up_blk, first, count, prev=None, deps=()):
    T, D = x1.shape
    nb, _, fb = w_up_blk.shape
    tm = _tile(T, 1024)
    ni = T // tm
    n_dep = len(deps) + (2 if prev is not None else 0)

    def body(a_ref, b_ref, *rest):
        ra_ref, h2_ref = rest[n_dep:n_dep + 2]
        ra = jnp.maximum(_dot(a_ref[...], b_ref[...], NN), 0.0)
        ra_ref[...] = ra.astype(BF16)
        h2_ref[...] = (ra * ra).astype(BF16)

    blk = pl.BlockSpec((tm, fb), lambda i, j: (i, first + j))
    shp = jax.ShapeDtypeStruct((T, nb * fb), BF16)
    keep = list(prev) if prev is not None else []
    return pl.pallas_call(
        body,
        name="ff_up_%d" % first,
        grid=(ni, count),
        in_specs=[pl.BlockSpec((tm, D), lambda i, j: (i, 0)),
                  pl.BlockSpec((None, D, fb), lambda i, j: (first + j, 0, 0))] + _hbm_specs(n_dep),
        out_specs=[blk, blk],
        out_shape=[shp, shp],
        input_output_aliases=({2: 0, 3: 1} if prev is not None else {}),
        compiler_params=_params(2),
    )(x1, w_up_blk, *keep, *deps)


def _ff_down_loss(h2, w_down, xhat1, target, g1, b1, g2, b2):
    T, F = h2.shape
    D = w_down.shape[1]
    tm = _tile(T, 1024)
    inv_d = 1.0 / D

    def chunk(acc, valid, ri, vi, ro, so):
        g1v, b1v, g2v, b2v = (v[...] for v in vi)

        def sub(rows):
            x1 = ri[0][rows, :] * g1v + b1v
            xhat, rstd = _ln_fwd(DN_ALPHA * x1 + acc[rows, :])
            e = xhat * g2v + b2v - ri[1][rows, :]
            dy = e * inv_d
            dh = _ln_bwd(dy, xhat, rstd, g2v)
            ro[0][rows, :] = dh
            ro[1][rows, :] = dh.astype(BF16)
            return (jnp.sum(dy * xhat, axis=0, keepdims=True), jnp.sum(dy, axis=0, keepdims=True),
                    jnp.sum(e * e, axis=0, keepdims=True))

        sg, sb, sl = _by_sub_rows(acc.shape[0], sub)
        so[0][...] += jnp.where(valid, sg, 0.0)
        so[1][...] += jnp.where(valid, sb, 0.0)
        so[2][...] += jnp.where(valid, sl * (0.5 * inv_d), 0.0)

    vshape = jax.ShapeDtypeStruct((1, D), F32)
    return _mm_rows("ff_down_loss", "nn", h2, w_down, M=T, N=D, K=F, tm=tm, tk=_tile(F, 1024),
                    row_ins=[xhat1, target], vec_ins=[g1, b1, g2, b2],
                    row_outs=[jax.ShapeDtypeStruct((T, D), F32), jax.ShapeDtypeStruct((T, D), BF16)],
                    stat_outs=[vshape, vshape, vshape], chunk_fn=chunk)


def _ff_down_bwd_act(dh3b, w_down, ra):
    T, D = dh3b.shape
    F = w_down.shape[0]
    tm, tn = _tile(T, 1024), _tile(F, 1024)

    def ep(acc_ref, ex, o, i, j):
        o[0][...] = (acc_ref[...] * (2.0 * ex[0][...].astype(F32))).astype(BF16)

    blk = pl.BlockSpec((tm, tn), lambda i, j, k: (i, j))
    return _mm("ff_down_bwd_act", "nt", dh3b, w_down, M=T, N=F, K=D, tm=tm, tn=tn, tk=D,
               outs=[(jax.ShapeDtypeStruct((T, F), BF16), blk)], extras=[(ra, blk)], epilogue=ep)[0]


def _grad_w(name, a, b, *, a_fn=None, tm_pref=1024, tn_pref=1024, tk_pref=4096, deps=()):
    T, M = a.shape
    N = b.shape[1]
    tm, tn, tk = _tile(M, tm_pref), _tile(N, tn_pref), _tile(T, tk_pref)
    return _mm(name, "tn", a, b, M=M, N=N, K=T, tm=tm, tn=tn, tk=tk, a_fn=a_fn, deps=deps,
               outs=[(jax.ShapeDtypeStruct((M, N), F32), pl.BlockSpec((tm, tn), lambda i, j, k: (i, j)))],
               epilogue=None)[0]


def _grad_w_up_blk(x1, da, nb, deps=()):
    T, D = x1.shape
    F = da.shape[1]
    fb = F // nb
    tm, tk = _tile(D, 1024), _tile(T, 4096)
    return _mm("grad_w_up", "tn", x1, da, M=D, N=F, K=T, tm=tm, tn=fb, tk=tk, deps=deps,
               outs=[(jax.ShapeDtypeStruct((nb, D, fb), F32),
                      pl.BlockSpec((None, tm, fb), lambda i, j, k: (j, i, 0)))],
               epilogue=None)[0]


def _ff_up_bwd_ln1(da, w_up_blk, dh3, xhat1, rstd1, g1, deps=()):
    T, F = da.shape
    nb, D, fb = w_up_blk.shape
    tm = _tile(T, 1024)

    def chunk(acc, valid, ri, vi, ro, so):
        g = vi[0][...]

        def sub(rows):
            dx1 = DN_ALPHA * ri[0][rows, :] + acc[rows, :]
            xhat = ri[1][rows, :]
            dh = _ln_bwd(dx1, xhat, ri[2][rows, :], g)
            ro[0][rows, :] = dh
            ro[1][rows, :] = dh.astype(BF16)
            return jnp.sum(dx1 * xhat, axis=0, keepdims=True), jnp.sum(dx1, axis=0, keepdims=True)

        sg, sb = _by_sub_rows(acc.shape[0], sub)
        so[0][...] += jnp.where(valid, sg, 0.0)
        so[1][...] += jnp.where(valid, sb, 0.0)

    nk = F // fb
    vshape = jax.ShapeDtypeStruct((1, D), F32)
    return _mm_rows("ff_up_bwd_ln1", "nt", da, w_up_blk, M=T, N=D, K=F, tm=tm, tk=fb,
                    b_spec=pl.BlockSpec((None, D, fb), lambda i, k: (jnp.where(i < T // tm, k, nk - 1), 0, 0)),
                    row_ins=[dh3, xhat1, rstd1], vec_ins=[g1],
                    row_outs=[jax.ShapeDtypeStruct((T, D), F32), jax.ShapeDtypeStruct((T, D), BF16)],
                    stat_outs=[vshape, vshape], chunk_fn=chunk, deps=deps)


def _mix_bwd(dh1b, w_out, deps=()):
    T, D = dh1b.shape
    KY = w_out.shape[0]
    tm, tn = _tile(T, 1024), _tile(KY, 1024)
    return _mm("mix_bwd", "nt", dh1b, w_out, M=T, N=KY, K=D, tm=tm, tn=tn, tk=D, deps=deps,
               outs=[(jax.ShapeDtypeStruct((T, KY), F32), pl.BlockSpec((tm, tn), lambda i, j, k: (i, j)))],
               epilogue=None)[0]


def _conv_bwd(proj, dyin, conv_w8, conv_g, deps=()):
    T = proj.shape[0]
    tt = _tile(T, 256)
    nt = T // tt
    t8 = tt // 8
    nx = tt + 8

    def body(b_ref, c_ref, u_ref, d_ref, bn_ref, cn_ref, un_ref, dn_ref, cp_ref, up_ref, w_ref, g_ref, *rest):
        dp_ref, dw_ref, dg_ref = rest[len(deps):]
        i = pl.program_id(0)

        @pl.when(i == 0)
        def _():
            dw_ref[...] = jnp.zeros_like(dw_ref)
            dg_ref[...] = jnp.zeros_like(dg_ref)

        more = i < nt - 1

        def ext(cur_ref, nxt_ref):
            return jnp.concatenate([cur_ref[...], jnp.where(more, nxt_ref[...], 0.0)], axis=0)

        bx, cx, ux, dx = ext(b_ref, bn_ref), ext(c_ref, cn_ref), ext(u_ref, un_ref), ext(d_ref, dn_ref)
        hx = cx * ux
        hp = jnp.where(i > 0, cp_ref[...] * up_ref[...], 0.0)
        h1, h2 = _conv_shift(hx, hp)
        w = w_ref[...]
        g = g_ref[...]
        yx = w[0:1, :] * h2 + w[1:2, :] * h1 + w[2:3, :] * hx
        px = bx * yx
        dps, dgs = [], []
        for gi in range(CONV_GROUPS):
            sl = slice(gi * LANE, (gi + 1) * LANE)
            pg, dg_ = px[:, sl], dx[:, sl]
            r = lax.rsqrt(jnp.mean(pg * pg, axis=-1, keepdims=True) + RMS_EPS)
            gd = g[:, sl] * dg_
            dps.append(r * gd - pg * (r * r * r) * jnp.mean(pg * gd, axis=-1, keepdims=True))
            dgs.append(jnp.sum((dg_ * pg * r)[:tt, :], axis=0, keepdims=True))
        dpx = jnp.concatenate(dps, axis=1)
        dg_ref[...] += jnp.concatenate(dgs, axis=1)
        dyx = dpx * bx
        dyc = dyx[:tt, :]
        dh = (w[2:3, :] * dyx + w[1:2, :] * pltpu.roll(dyx, nx - 1, 0) + w[0:1, :] * pltpu.roll(dyx, nx - 2, 0))[:tt, :]
        dw_ref[0:1, :] += jnp.sum(dyc * h2[:tt, :], axis=0, keepdims=True)
        dw_ref[1:2, :] += jnp.sum(dyc * h1[:tt, :], axis=0, keepdims=True)
        dw_ref[2:3, :] += jnp.sum(dyc * hx[:tt, :], axis=0, keepdims=True)
        dp_ref[:, 0:D_CONV] = (dpx * yx)[:tt, :].astype(BF16)
        dp_ref[:, D_CONV:2 * D_CONV] = (dh * u_ref[...]).astype(BF16)
        dp_ref[:, 2 * D_CONV:3 * D_CONV] = (dh * c_ref[...]).astype(BF16)
        dp_ref[:, 3 * D_CONV:HALF_P] = jnp.zeros((tt, HALF_P - 3 * D_CONV), BF16)

    def col(cidx):
        return pl.BlockSpec((tt, D_CONV), lambda i: (i, cidx))

    def nxt(cidx):
        return pl.BlockSpec((8, D_CONV), lambda i: (jnp.minimum((i + 1) * t8, T // 8 - 1), cidx))

    def prev(cidx):
        return pl.BlockSpec((8, D_CONV), lambda i: (jnp.maximum(i * t8 - 1, 0), cidx))

    return pl.pallas_call(
        body,
        name="conv_bwd",
        grid=(nt,),
        in_specs=[col(0), col(1), col(2), col(0), nxt(0), nxt(1), nxt(2), nxt(0), prev(1), prev(2),
                  _full((8, D_CONV)), _full((1, D_CONV))] + _hbm_specs(len(deps)),
        out_specs=[pl.BlockSpec((tt, HALF_P), lambda i: (i, 0)), _full((8, D_CONV)), _full((1, D_CONV))],
        out_shape=[jax.ShapeDtypeStruct((T, P_INT), BF16), jax.ShapeDtypeStruct((8, D_CONV), F32),
                   jax.ShapeDtypeStruct((1, D_CONV), F32)],
        compiler_params=_params(1),
    )(proj, proj, proj, dyin, proj, proj, proj, dyin, proj, proj, conv_w8, conv_g, *deps)


def _gla_bwd(proj, wg128, gbias, gng, o_all, states, dyin, dproj):
    T = proj.shape[0]
    nst = T // STEP_ROWS

    def body(p_ref, wg_ref, gb_ref, gn_ref, o_ref, st_ref, d_ref, dp_in_ref,
             dp_ref, dwg_ref, dgb_ref, dgn_ref, ds_ref):
        n = pl.program_id(0)

        @pl.when(n == 0)
        def _():
            ds_ref[...] = jnp.zeros_like(ds_ref)
            dwg_ref[...] = jnp.zeros_like(dwg_ref)
            dgb_ref[...] = jnp.zeros_like(dgb_ref)
            dgn_ref[...] = jnp.zeros_like(dgn_ref)

        blk = p_ref[...]
        zl, z, bcum_all, same = _gla_step_terms(blk, wg_ref, gb_ref)
        causal = _causal()
        gn = gn_ref[...]
        ri = lax.broadcasted_iota(jnp.int32, (STEP_ROWS, STEP_ROWS), 0)
        ci = lax.broadcasted_iota(jnp.int32, (STEP_ROWS, STEP_ROWS), 1)
        upper = (same & (ri <= ci)).astype(F32)
        dstates = [ds_ref[h] for h in range(GLA_HEADS)]
        db_rows, dbl_rows, dgn_sum = [None] * STEP_CHUNKS, [None] * STEP_CHUNKS, [None] * GLA_HEADS
        for c in reversed(range(STEP_CHUNKS)):
            rows = slice(c * CHUNK, (c + 1) * CHUNK)
            q, k = blk[rows, 0:512], blk[rows, 512:1024]
            v, r = blk[rows, 1024:2048], blk[rows, 2048:3072]
            bcum = bcum_all[rows, :]
            db_parts, dbl_parts = [], []
            for h in range(GLA_HEADS):
                eb, enb, eend, dec, qd, ki, ke = _gla_head_terms(q, k, bcum, h)
                vs = slice(h * HEAD_V, (h + 1) * HEAD_V)
                ks = slice(h * HEAD_K, (h + 1) * HEAD_K)
                o = o_ref[rows, vs]
                rh = r[:, vs]
                dyg = d_ref[rows, vs]
                rinv = lax.rsqrt(jnp.mean(o * o, axis=-1, keepdims=True) + RMS_EPS)
                sg = _sigmoid(rh)
                on = o * rinv
                dr = dyg * (on * gn[:, vs]) * (sg * (1.0 + rh * (1.0 - sg)))
                don = dyg * (rh * sg)
                part = jnp.sum(don * on, axis=0, keepdims=True)
                dgn_sum[h] = part if dgn_sum[h] is None else dgn_sum[h] + part
                t = don * gn[:, vs]
                do = rinv * t - o * (rinv * rinv * rinv) * jnp.mean(o * t, axis=-1, keepdims=True)
                dob = do.astype(BF16)
                vb = v[:, vs].astype(BF16)
                qdb, kib, keb = qd.astype(BF16), ki.astype(BF16), ke.astype(BF16)
                a = jnp.where(causal, _dot(qdb, kib, NT), 0.0)
                st = st_ref[c, h]
                dst = dstates[h]
                dstb = dst.astype(BF16)
                da = jnp.where(causal, _dot(dob, vb, NT), 0.0)
                dab = da.astype(BF16)
                dv = _dot(a.astype(BF16), dob, TN) + _dot(keb, dstb, NT)
                dqd = _dot(dab, kib, NN) + _dot(dob, st.astype(BF16), NN)
                dki = _dot(dab, qdb, TN)
                dke = _dot(vb, dstb, NN)
                ddec = jnp.sum(st * dst, axis=0, keepdims=True)
                dstates[h] = dec * dst + _dot(dob, qdb, TN)
                dq = dqd * eb * (HEAD_K ** -0.5)
                dk = dki * enb + dke * eend
                db_parts.append(dqd * qd - dki * ki - dke * ke)
                dbl_parts.append(jnp.sum(dke * ke, axis=0, keepdims=True) + dec * ddec)
                dp_ref[rows, ks] = dq.astype(BF16)
                dp_ref[rows, D_GLA_K + h * HEAD_K:D_GLA_K + (h + 1) * HEAD_K] = dk.astype(BF16)
                dp_ref[rows, 1024 + h * HEAD_V:1024 + (h + 1) * HEAD_V] = dv.astype(BF16)
                dp_ref[rows, 2048 + h * HEAD_V:2048 + (h + 1) * HEAD_V] = dr.astype(BF16)
            db_rows[c] = jnp.concatenate(db_parts, axis=1)
            dbl_rows[c] = jnp.broadcast_to(jnp.concatenate(dbl_parts, axis=1), (CHUNK, D_GLA_K))
        for h in range(GLA_HEADS):
            ds_ref[h] = dstates[h]
            dgn_ref[:, h * HEAD_V:(h + 1) * HEAD_V] += dgn_sum[h]
        db = jnp.concatenate(db_rows, axis=0)
        dlog = _dot(upper, db, NN, precision=lax.Precision.HIGHEST) + jnp.concatenate(dbl_rows, axis=0)
        dz = dlog * (1.0 / GATE_TAU) * (1.0 / (1.0 + jnp.exp(z)))
        dzb = dz.astype(BF16)
        dp_ref[:, 3072:3200] = _dot(dzb, wg_ref[...], NT).astype(BF16)
        dwg_ref[...] += _dot(zl.astype(BF16), dzb, TN)
        dgb_ref[...] += jnp.sum(dz, axis=0, keepdims=True)

    rev = lambda n: nst - 1 - n
    return pl.pallas_call(
        body,
        name="gla_bwd",
        grid=(nst,),
        in_specs=[pl.BlockSpec((STEP_ROWS, HALF_P), lambda n: (rev(n), 1)), _full((LANE, D_GLA_K)),
                  _full((1, D_GLA_K)), _full((1, D_GLA_V)),
                  pl.BlockSpec((STEP_ROWS, D_GLA_V), lambda n: (rev(n), 0)),
                  pl.BlockSpec((STEP_CHUNKS, GLA_HEADS, HEAD_V, HEAD_K), lambda n: (rev(n), 0, 0, 0)),
                  pl.BlockSpec((STEP_ROWS, D_GLA_V), lambda n: (rev(n), 1)), pl.BlockSpec(memory_space=pl.ANY)],
        out_specs=[pl.BlockSpec((STEP_ROWS, HALF_P), lambda n: (rev(n), 1)), _full((LANE, D_GLA_K)),
                   _full((1, D_GLA_K)), _full((1, D_GLA_V))],
        out_shape=[jax.ShapeDtypeStruct(dproj.shape, BF16), jax.ShapeDtypeStruct((LANE, D_GLA_K), F32),
                   jax.ShapeDtypeStruct((1, D_GLA_K), F32), jax.ShapeDtypeStruct((1, D_GLA_V), F32)],
        scratch_shapes=[pltpu.VMEM((GLA_HEADS, HEAD_V, HEAD_K), F32)],
        input_output_aliases={7: 0},
        compiler_params=_params(1),
    )(proj, wg128, gbias, gng, o_all, states, dyin, dproj)


def _proj_bwd_x(dproj, w_full, dh1, deps=()):
    T, P = dproj.shape
    D = w_full.shape[0]
    tm, tk = _tile(T, 512), _tile(P, 1280)

    def ep(acc_ref, ex, o, i, j):
        o[0][...] = DN_ALPHA * ex[0][...] + acc_ref[...]

    row = pl.BlockSpec((tm, D), lambda i, j, k: (i, 0))
    return _mm("proj_bwd_x", "nt", dproj, w_full, M=T, N=D, K=P, tm=tm, tn=D, tk=tk,
               outs=[(jax.ShapeDtypeStruct((T, D), F32), row)], extras=[(dh1, row)], epilogue=ep, deps=deps)[0]


def _place():
    x, y, c = lax.axis_index("x"), lax.axis_index("y"), lax.axis_index("c")
    chips = [(1 - x, y), (x, 1 - y), (1 - x, 1 - y)]
    return x, y, c, chips


def _rcopy(src, dst, ssem, rsem, dev):
    return pltpu.make_async_remote_copy(src_ref=src, dst_ref=dst, send_sem=ssem, recv_sem=rsem,
                                        device_id=dev, device_id_type=MESH)


def _all_gather(name, shards, deps=()):
    n = len(shards)

    def body(*refs):
        ins, outs = refs[:n], refs[n + len(deps):2 * n + len(deps)]
        ssem, rsem, lsem = refs[2 * n + len(deps):]
        x, y, c, chips = _place()
        me, sib = (x, y, c), (x, y, 1 - c)

        def slot(w, px, py, pc):
            return outs[w].at[4 * px + 2 * py + pc]

        started = []
        for w in range(n):
            lc = pltpu.make_async_copy(ins[w], slot(w, *me), lsem.at[w])
            lc.start()
            started.append(lc)
        sends = []
        for w in range(n):
            cp = _rcopy(ins[w], slot(w, *me), ssem.at[7 * w], rsem.at[7 * w], sib)
            cp.start()
            sends.append(cp)
            for jx, chip in enumerate(chips):
                cp = _rcopy(ins[w], slot(w, *me), ssem.at[7 * w + 1 + jx], rsem.at[7 * w + 1 + jx], (*chip, c))
                cp.start()
                sends.append(cp)
        for w in range(n):
            for jx, chip in enumerate(chips):
                blk = slot(w, *chip, c)
                _rcopy(blk, blk, ssem.at[7 * w + 1 + jx], rsem.at[7 * w + 1 + jx], me).wait_recv()
                cp = _rcopy(blk, blk, ssem.at[7 * w + 4 + jx], rsem.at[7 * w + 4 + jx], sib)
                cp.start()
                sends.append(cp)
        for w in range(n):
            blk = slot(w, x, y, 1 - c)
            _rcopy(blk, blk, ssem.at[7 * w], rsem.at[7 * w], me).wait_recv()
            for jx, chip in enumerate(chips):
                blk = slot(w, *chip, 1 - c)
                _rcopy(blk, blk, ssem.at[7 * w + 4 + jx], rsem.at[7 * w + 4 + jx], me).wait_recv()
        for cp in sends:
            cp.wait_send()
        for lc in started:
            lc.wait()

    return pl.pallas_call(
        body,
        name=name,
        in_specs=_hbm_specs(n + len(deps)),
        out_specs=_hbm_specs(n),
        out_shape=[jax.ShapeDtypeStruct((N_DEV,) + s.shape, s.dtype) for s in shards],
        scratch_shapes=[pltpu.SemaphoreType.DMA((7 * n,)), pltpu.SemaphoreType.DMA((7 * n,)),
                        pltpu.SemaphoreType.DMA((n,))],
    )(*shards, *deps)


HBM_SPEC = pl.BlockSpec(memory_space=pltpu.HBM)
SEM_SPEC = pl.BlockSpec(memory_space=pltpu.SEMAPHORE)
SIDE_EFFECT = pltpu.SideEffectType.DATAFLOW_SIDE_EFFECTING


def _cast_place(name, ids, w, deps=(), dtype=None):
    dtype = BF16 if dtype is None else dtype
    R, C = w.shape
    tr = _tile(R, 256)

    def body(ids_ref, w_ref, *rest):
        rest[len(deps)][...] = w_ref[...].astype(dtype)

    return pl.pallas_call(
        body,
        name=name,
        grid_spec=pltpu.PrefetchScalarGridSpec(
            num_scalar_prefetch=1,
            grid=(R // tr,),
            in_specs=[pl.BlockSpec((tr, C), lambda r, ids: (r, 0))] + _hbm_specs(len(deps)),
            out_specs=pl.BlockSpec((None, tr, C), lambda r, ids: (ids[0], r, 0)),
        ),
        out_shape=jax.ShapeDtypeStruct((N_DEV, R, C), dtype),
        compiler_params=_params(1),
    )(ids, w, *deps)


def _xfer_start(name, bufs, plan, n):
    nb = len(bufs)

    def body(*refs):
        ins = refs[:nb]
        ssem, rsem = refs[nb], refs[nb + 1]
        token = refs[2 * nb + 2]
        x, y, c, chips = _place()
        for k, (src, dst, dev, _) in enumerate(plan(ins, x, y, c, chips)):
            _rcopy(src, dst, ssem.at[k], rsem.at[k], dev).start()
        token[...] = jnp.zeros_like(token)

    res = pl.pallas_call(
        body,
        name=name,
        out_shape=(pltpu.SemaphoreType.DMA((n,)), pltpu.SemaphoreType.DMA((n,)),
                   *[pltpu.HBM(b.shape, b.dtype) for b in bufs], jax.ShapeDtypeStruct((8, LANE), F32)),
        in_specs=[HBM_SPEC] * nb,
        out_specs=(SEM_SPEC, SEM_SPEC, *[HBM_SPEC] * nb, pl.BlockSpec(memory_space=pltpu.VMEM)),
        input_output_aliases={i: 2 + i for i in range(nb)},
        compiler_params=pltpu.CompilerParams(has_side_effects=SIDE_EFFECT),
    )(*[pltpu.with_memory_space_constraint(b, pltpu.HBM) for b in bufs])
    return dict(sems=res[:2], bufs=list(res[2:2 + nb]), token=res[2 + nb], plan=plan, n=n)


def _xfer_wait(name, started, after):
    bufs, plan = started["bufs"], started["plan"]
    nb = len(bufs)

    def body(*refs):
        ins = refs[:nb]
        ssem, rsem = refs[nb], refs[nb + 1]
        x, y, c, chips = _place()
        for k, (src, _, dev, land) in enumerate(plan(ins, x, y, c, chips)):
            cp = _rcopy(src, land, ssem.at[k], rsem.at[k], dev)
            cp.wait_send()
            cp.wait_recv()

    res = pl.pallas_call(
        body,
        name=name,
        out_shape=tuple(pltpu.HBM(b.shape, b.dtype) for b in bufs),
        in_specs=[HBM_SPEC] * nb + [SEM_SPEC, SEM_SPEC, pl.BlockSpec(memory_space=pl.ANY)],
        out_specs=tuple([HBM_SPEC] * nb),
        input_output_aliases={i: i for i in range(nb)},
        compiler_params=pltpu.CompilerParams(has_side_effects=SIDE_EFFECT),
    )(*bufs, *started["sems"], after)
    return list(res)


def _plan_gather_chips(refs, x, y, c, chips):
    plan = []
    for land in refs:
        mine = land.at[4 * x + 2 * y + c]
        plan.append((mine, mine, (x, y, 1 - c), land.at[4 * x + 2 * y + (1 - c)]))
        for px, py in chips:
            plan.append((mine, mine, (px, py, c), land.at[4 * px + 2 * py + c]))
    return plan


def _plan_gather_pass(refs, x, y, c, chips):
    return [(land.at[4 * px + 2 * py + c], land.at[4 * px + 2 * py + c], (x, y, 1 - c),
             land.at[4 * px + 2 * py + (1 - c)]) for land in refs for px, py in chips]


def _plan_reduce_core(refs, x, y, c, chips):
    grad, recv = refs
    return [(grad.at[2 * q + (1 - c)], recv.at[q], (x, y, 1 - c), recv.at[q]) for q in range(N_CHIP)]


def _plan_reduce_chips(refs, x, y, c, chips):
    part, land = refs
    return [(part.at[2 * px + py], land.at[2 * x + y], (px, py, c), land.at[2 * px + py]) for px, py in chips]


def _chip_sums(name, ids, grad, recv):
    _, R, C = grad.shape
    tr = _tile(R, 256)

    def body(ids_ref, g_ref, r_ref, o_ref):
        o_ref[...] = (g_ref[...] + r_ref[...]).astype(BF16)

    return pl.pallas_call(
        body,
        name=name,
        grid_spec=pltpu.PrefetchScalarGridSpec(
            num_scalar_prefetch=1,
            grid=(N_CHIP - 1, R // tr),
            in_specs=[pl.BlockSpec((None, tr, C), lambda q, r, ids: (2 * ids[3 + q] + ids[2], r, 0)),
                      pl.BlockSpec((None, tr, C), lambda q, r, ids: (ids[3 + q], r, 0))],
            out_specs=pl.BlockSpec((None, tr, C), lambda q, r, ids: (ids[3 + q], r, 0)),
        ),
        out_shape=jax.ShapeDtypeStruct((N_CHIP, R, C), BF16),
        compiler_params=_params(2),
    )(ids, grad, recv)


def _adamw(w, g, m, v):
    m = ADAM_B1 * m + (1.0 - ADAM_B1) * g
    v = ADAM_B2 * v + (1.0 - ADAM_B2) * (g * g)
    m_hat = m / (1.0 - ADAM_B1 ** ADAM_STEP)
    v_hat = v / (1.0 - ADAM_B2 ** ADAM_STEP)
    delta = -ADAM_LR * (m_hat / (jnp.sqrt(v_hat) + ADAM_EPS) + ADAM_WD * w)
    return delta, m, v


def _reduce_adamw(name, ids, grad, recv, landed, w, m, v):
    _, R, C = grad.shape
    tr = _tile(R, 256)

    def body(ids_ref, g_ref, r_ref, l1_ref, l2_ref, l3_ref, w_ref, m_ref, v_ref, go_ref, do_ref, mo_ref, vo_ref):
        g = g_ref[...] + r_ref[...]
        g = g + l1_ref[...].astype(F32)
        g = g + l2_ref[...].astype(F32)
        g = g + l3_ref[...].astype(F32)
        delta, mn, vn = _adamw(w_ref[...], g, m_ref[...], v_ref[...])
        go_ref[...] = g
        do_ref[...] = delta
        mo_ref[...] = mn
        vo_ref[...] = vn

    def pick(k):
        return pl.BlockSpec((None, tr, C), lambda r, ids: (ids[k], r, 0))

    flat = pl.BlockSpec((tr, C), lambda r, ids: (r, 0))
    shp = jax.ShapeDtypeStruct((R, C), F32)
    return pl.pallas_call(
        body,
        name=name,
        grid_spec=pltpu.PrefetchScalarGridSpec(
            num_scalar_prefetch=1,
            grid=(R // tr,),
            in_specs=[pick(0), pick(1), pick(3), pick(4), pick(5), flat, flat, flat],
            out_specs=[flat, flat, flat, flat],
        ),
        out_shape=[shp, shp, shp, shp],
        compiler_params=_params(1),
    )(ids, grad, recv, landed, landed, landed, w, m, v)


SMALL = ("conv_w", "conv_norm_g", "w_gate_up", "gate_bias", "gla_norm_g", "ln1_g", "ln1_b", "ln2_g", "ln2_b")
R_LOSS = 14


def _small_rows(D, conv_cols, gate_cols):
    nv = max(1, D // SP_COLS)
    assert nv <= 2, D
    return {"conv_w": (0, 3, conv_cols), "conv_norm_g": (3, 1, D_CONV), "gate_bias": (4, 1, D_GLA_K),
            "gla_norm_g": (5, 1, D_GLA_V), "ln1_g": (6, nv, D), "ln1_b": (8, nv, D), "ln2_g": (10, nv, D),
            "ln2_b": (12, nv, D), "w_gate_up": (16, GATE_RANK, gate_cols)}


def _put(o_ref, entry, val):
    row, n_rows, cols = entry
    if val.shape[0] == 1 and n_rows > 1:
        for r in range(n_rows):
            o_ref[row + r:row + r + 1, :] = val[:, r * SP_COLS:(r + 1) * SP_COLS]
    else:
        o_ref[row:row + n_rows, 0:cols] = val[0:n_rows, 0:cols]


def _take(g, entry):
    row, n_rows, cols = entry
    if cols > SP_COLS:
        return jnp.concatenate([g[row + r:row + r + 1, :] for r in range(n_rows)], axis=1)
    return g[row:row + n_rows, 0:cols]


def _make_pack(name, rows, pieces):
    names = list(pieces)

    def body(*refs):
        o_ref = refs[len(names)]
        o_ref[...] = jnp.zeros_like(o_ref)
        for nm, ref in zip(names, refs):
            if nm == "loss":
                o_ref[R_LOSS:R_LOSS + 1, 0:1] = jnp.sum(ref[...], axis=1, keepdims=True)
            else:
                _put(o_ref, rows[nm], ref[...])

    arrs = [pieces[nm] for nm in names]
    return pl.pallas_call(
        body,
        name=name,
        grid=(1,),
        in_specs=[_full(a.shape) for a in arrs],
        out_specs=_full((SP_ROWS, SP_COLS)),
        out_shape=jax.ShapeDtypeStruct((SP_ROWS, SP_COLS), F32),
        compiler_params=_params(1),
    )(*arrs)


def _small_adamw(packs, rows, w, m, v):
    names = list(SMALL)
    n = len(names)

    def body(p_ref, *refs):
        ins, outs = refs[:3 * n], refs[3 * n:]
        g = p_ref[0]
        for dvc in range(1, N_DEV):
            g = g + p_ref[dvc]
        for i, nm in enumerate(names):
            gp = _take(g, rows[nm])
            delta, mn, vn = _adamw(ins[i][...], gp, ins[n + i][...], ins[2 * n + i][...])
            for kind, val in enumerate((gp, delta, mn, vn)):
                outs[kind * n + i][...] = val
        outs[4 * n][...] = g[R_LOSS:R_LOSS + 1, 0:1]

    arrs = [w[nm] for nm in names] + [m[nm] for nm in names] + [v[nm] for nm in names]
    shapes = [jax.ShapeDtypeStruct(w[nm].shape, F32) for nm in names] * 4 + [jax.ShapeDtypeStruct((1, 1), F32)]
    res = pl.pallas_call(
        body,
        name="small_adamw",
        grid=(1,),
        in_specs=[_full(packs.shape)] + [_full(a.shape) for a in arrs],
        out_specs=[_full(sh.shape) for sh in shapes],
        out_shape=shapes,
        compiler_params=_params(1),
    )(packs, *arrs)
    return [dict(zip(names, res[k * n:(k + 1) * n])) for k in range(4)], res[4 * n]


def _w_in_pieces():
    cs = D_IN_PROJ // N_DEV
    pieces = []
    for d in range(N_DEV):
        lo, hi = d * cs, (d + 1) * cs
        if hi <= CONV_COLS:
            pieces.append((d, 0, cs, lo))
        elif lo >= CONV_COLS:
            pieces.append((d, 0, cs, lo - CONV_COLS + HALF_P))
        else:
            pieces.append((d, 0, CONV_COLS - lo, lo))
            pieces.append((d, CONV_COLS - lo, cs, HALF_P))
    return pieces


def _w_in_full(gathered):
    nb, D, cs = gathered.shape
    tr = _tile(D, 256)

    def body(g_ref, o_ref):
        o_ref[:, CONV_COLS:HALF_P] = jnp.zeros((tr, HALF_P - CONV_COLS), o_ref.dtype)
        o_ref[:, HALF_P + GLA_COLS:P_INT] = jnp.zeros((tr, HALF_P - GLA_COLS), o_ref.dtype)
        for d, a, b, dst in _w_in_pieces():
            o_ref[:, dst:dst + (b - a)] = g_ref[d, :, a:b]

    return pl.pallas_call(
        body,
        name="w_in_full",
        grid=(D // tr,),
        in_specs=[pl.BlockSpec((nb, tr, cs), lambda r: (0, r, 0))],
        out_specs=pl.BlockSpec((tr, P_INT), lambda r: (r, 0)),
        out_shape=jax.ShapeDtypeStruct((D, P_INT), gathered.dtype),
        compiler_params=_params(1),
    )(gathered)


def _w_in_blocks(dw):
    D = dw.shape[0]
    cs = D_IN_PROJ // N_DEV
    tr = _tile(D, 256)

    def body(w_ref, o_ref):
        for d, a, b, src in _w_in_pieces():
            o_ref[d, :, a:b] = w_ref[:, src:src + (b - a)]

    return pl.pallas_call(
        body,
        name="w_in_blocks",
        grid=(D // tr,),
        in_specs=[pl.BlockSpec((tr, P_INT), lambda r: (r, 0))],
        out_specs=pl.BlockSpec((N_DEV, tr, cs), lambda r: (0, r, 0)),
        out_shape=jax.ShapeDtypeStruct((N_DEV, D, cs), dw.dtype),
        compiler_params=_params(1),
    )(dw)


BIG = ("w_in", "w_out", "w_ff_up", "w_ff_down")
ORDER = ("w_in", "conv_w", "conv_norm_g", "w_gate_up", "gate_bias", "gla_norm_g", "w_out", "ln1_g", "ln1_b",
         "w_ff_up", "w_ff_down", "ln2_g", "ln2_b")


def kernel(x, w_in, conv_w, conv_norm_g, w_gate_up, gate_bias, gla_norm_g, w_out, ln1_g, ln1_b, w_ff_up, w_ff_down, ln2_g, ln2_b, loss_target, m_w_in, m_conv_w, m_conv_norm_g, m_w_gate_up, m_gate_bias, m_gla_norm_g, m_w_out, m_ln1_g, m_ln1_b, m_w_ff_up, m_w_ff_down, m_ln2_g, m_ln2_b, v_w_in, v_conv_w, v_conv_norm_g, v_w_gate_up, v_gate_bias, v_gla_norm_g, v_w_out, v_ln1_g, v_ln1_b, v_w_ff_up, v_w_ff_down, v_ln2_g, v_ln2_b):
    T, D = x.shape[1], x.shape[2]
    xs, target = x[0], loss_target[0]
    xi, yi, ci = lax.axis_index("x"), lax.axis_index("y"), lax.axis_index("c")
    chip = 2 * xi + yi
    dev = 2 * chip + ci
    others = [jnp.where(chip <= q, q + 1, q) for q in range(N_CHIP - 1)]
    ids = jnp.stack([dev, chip, ci] + others).astype(jnp.int32)
    conv_cols, gate_cols = conv_w.shape[2], w_gate_up.shape[2]

    def gather(nm, lands):
        return _xfer_start("gather_chips_" + nm, lands, _plan_gather_chips, 4 * len(lands))

    def pass_on(nm, started, after):
        lands = _xfer_wait("gather_chips_wait_" + nm, started, after)
        return _xfer_start("gather_pass_" + nm, lands, _plan_gather_pass, 3 * len(lands))

    def landed(nm, started, after):
        return _xfer_wait("gather_pass_wait_" + nm, started, after)

    rows = _small_rows(D, conv_cols, gate_cols)
    fwd_pack = _make_pack("pack_fwd", rows, {"conv_w": conv_w[0], "w_gate_up": w_gate_up[0]})
    ga_in = gather("w_in", [_cast_place("cast_place_w_in", ids, w_in[0]),
                            _cast_place("cast_place_pack", ids, fwd_pack, dtype=F32)])
    ga, dep = [], ga_in["token"]
    m_in, v_in = m_w_in[0], v_w_in[0]
    for nm, w in zip(BIG[1:], (w_out, w_ff_up, w_ff_down)):
        deps = [dep, m_in, v_in] if nm == "w_ff_down" else [dep]
        ga.append(gather(nm, [_cast_place("cast_place_" + nm, ids, w[0], deps)]))
        dep = ga[-1]["token"]
    xb = _cast_bf16(xs, [dep])
    gp_in = pass_on("w_in", ga_in, xb)
    g_in, g_pack = landed("w_in", gp_in, gp_in["token"])
    w_full = _w_in_full(g_in)
    r_cw, r_gw = rows["conv_w"][0], rows["w_gate_up"][0]
    conv_w_full = g_pack[:, r_cw:r_cw + 3, :conv_cols].transpose(1, 0, 2).reshape(3, -1)
    gate_w_full = g_pack[:, r_gw:r_gw + GATE_RANK, :gate_cols].transpose(1, 0, 2).reshape(GATE_RANK, -1)
    conv_w8 = jnp.pad(conv_w_full, ((0, 5), (0, 0)))
    wg128 = jnp.pad(gate_w_full, ((0, LANE - GATE_RANK), (0, 0))).astype(BF16)
    proj = _proj_fwd(xb, w_full)
    yin = _conv_fwd(proj, conv_w8, conv_norm_g)
    gp_out = pass_on("w_out", ga[0], yin)
    o_all, states, yin = _gla_fwd(proj, wg128, gate_bias, gla_norm_g, yin, deps=[gp_out["token"]])
    w_out_full = landed("w_out", gp_out, o_all)[0].reshape(-1, D)
    gp_up = pass_on("w_ff_up", ga[1], o_all)
    xhat1, x1, rstd1 = _mix_ln1(yin, w_out_full, xs, ln1_g, ln1_b, deps=[gp_up["token"]])
    (w_up_blk,) = landed("w_ff_up", gp_up, x1)
    half = N_DEV // 2
    ra, h2 = _ff_up(x1, w_up_blk, 0, half)
    gp_down = pass_on("w_ff_down", ga[2], ra)
    ra, h2 = _ff_up(x1, w_up_blk, half, N_DEV - half, prev=(ra, h2), deps=[gp_down["token"]])
    w_down_full = landed("w_ff_down", gp_down, ra)[0].reshape(-1, D)
    dh3, dh3b, g_ln2_g, g_ln2_b, loss = _ff_down_loss(h2, w_down_full, xhat1, target, ln1_g, ln1_b, ln2_g, ln2_b)

    def to_core(nm, grad):
        recv = lax.empty((N_CHIP,) + grad.shape[1:], F32)
        return _xfer_start("reduce_core_" + nm, [grad, recv], _plan_reduce_core, N_CHIP)

    def to_chips(nm, started, after):
        grad, recv = _xfer_wait("reduce_core_wait_" + nm, started, after)
        part = _chip_sums("chip_sums_" + nm, ids, grad, recv)
        land = lax.empty(part.shape, BF16)
        return grad, recv, _xfer_start("reduce_chips_" + nm, [part, land], _plan_reduce_chips, N_CHIP - 1)

    da = _ff_down_bwd_act(dh3b, w_down_full, ra)
    gw_down = _grad_w("grad_w_down", h2, dh3b).reshape(N_DEV, -1, D)
    rc_down = to_core("w_ff_down", gw_down)
    gw_up = _grad_w_up_blk(x1, da, N_DEV, deps=[rc_down["token"]])
    gw_down, rv_down, rs_down = to_chips("w_ff_down", rc_down, gw_up)
    rc_up = to_core("w_ff_up", gw_up)
    dh1, dh1b, g_ln1_g, g_ln1_b = _ff_up_bwd_ln1(da, w_up_blk, dh3, xhat1, rstd1, ln1_g,
                                                 deps=[rs_down["token"], rc_up["token"]])
    gw_up, rv_up, rs_up = to_chips("w_ff_up", rc_up, dh1b)
    dyin = _mix_bwd(dh1b, w_out_full, deps=[rs_up["token"]])
    gw_out = _grad_w("grad_w_out", yin, dh1b).reshape(N_DEV, -1, D)
    rc_out = to_core("w_out", gw_out)
    dproj, g_conv_w, g_conv_g = _conv_bwd(proj, dyin, conv_w8, conv_norm_g, deps=[rc_out["token"]])
    dproj, g_gate_w, g_gate_b, g_gla_g = _gla_bwd(proj, wg128, gate_bias, gla_norm_g, o_all, states, dyin, dproj)
    gw_out, rv_out, rs_out = to_chips("w_out", rc_out, dproj)
    gw_in = _w_in_blocks(_grad_w("grad_w_in", xb, dproj, tn_pref=1280, tk_pref=2048, deps=[rs_out["token"]]))
    rc_in = to_core("w_in", gw_in)

    big = {}

    def finish(nm, grad, recv, started, w, m, v, after):
        _, land = _xfer_wait("reduce_chips_wait_" + nm, started, after)
        res = _reduce_adamw("adamw_" + nm, ids, grad, recv, land, w[0], m[0], v[0])
        big[nm] = [a[None] for a in res]
        return res[0]

    done = finish("w_ff_down", gw_down, rv_down, rs_down, w_ff_down, m_w_ff_down, v_w_ff_down, rc_in["token"])
    done = finish("w_ff_up", gw_up, rv_up, rs_up, w_ff_up, m_w_ff_up, v_w_ff_up, done)
    full_rows = _small_rows(D, D_CONV, D_GLA_K)
    pack = _make_pack("pack_grads", full_rows, {
        "conv_w": g_conv_w, "conv_norm_g": g_conv_g, "gate_bias": g_gate_b, "gla_norm_g": g_gla_g, "ln1_g": g_ln1_g,
        "ln1_b": g_ln1_b, "ln2_g": g_ln2_g, "ln2_b": g_ln2_b, "loss": loss, "w_gate_up": g_gate_w})
    (packs,) = _all_gather("gather_small_grads", [pack], deps=[done])
    gw_in, rv_in, rs_in = to_chips("w_in", rc_in, packs)
    done = finish("w_out", gw_out, rv_out, rs_out, w_out, m_w_out, v_w_out, rs_in["token"])
    grad_x = _proj_bwd_x(dproj, w_full, dh1, deps=[done])
    finish("w_in", gw_in, rv_in, rs_in, w_in, (m_in,), (v_in,), grad_x)

    def own_cols(row, n_rows, width):
        cut = lax.dynamic_slice(packs, (0, row, dev * width), (N_DEV, n_rows, width))
        return jnp.pad(cut, ((0, 0), (0, 0), (0, SP_COLS - width)))

    packs_own = jnp.concatenate([own_cols(r_cw, 3, conv_cols), packs[:, r_cw + 3:r_gw],
                                 own_cols(r_gw, GATE_RANK, gate_cols)], axis=1)
    as2d = lambda a: a[0] if a.ndim == 3 else a
    w_s = dict(zip(SMALL, map(as2d, (conv_w, conv_norm_g, w_gate_up, gate_bias, gla_norm_g, ln1_g, ln1_b, ln2_g, ln2_b))))
    m_s = dict(zip(SMALL, map(as2d, (m_conv_w, m_conv_norm_g, m_w_gate_up, m_gate_bias, m_gla_norm_g, m_ln1_g,
                                     m_ln1_b, m_ln2_g, m_ln2_b))))
    v_s = dict(zip(SMALL, map(as2d, (v_conv_w, v_conv_norm_g, v_w_gate_up, v_gate_bias, v_gla_norm_g, v_ln1_g,
                                     v_ln1_b, v_ln2_g, v_ln2_b))))
    small, loss_sum = _small_adamw(packs_own, rows, w_s, m_s, v_s)

    def leaf(kind, name):
        if name in BIG:
            return big[name][kind]
        a = small[kind][name]
        return a[None] if name in ("conv_w", "w_gate_up") else a

    out = [loss_sum[0, 0], grad_x[None]]
    for kind in range(4):
        out += [leaf(kind, nm) for nm in ORDER]
    return tuple(out)
```

```python
import jax
import jax.numpy as jnp
from jax import lax
from jax.experimental import pallas as pl
from jax.experimental.pallas import tpu as pltpu

F32 = jnp.float32
BF16 = jnp.bfloat16

D_CONV = 1024
CONV_GROUPS = 8
GLA_HEADS = 4
HEAD_K = 128
HEAD_V = 256
D_GLA_K = 512
D_GLA_V = 1024
GATE_RANK = 16
GATE_TAU = 16.0
CHUNK = 64
LN_EPS = 1e-5
RMS_EPS = 1e-6
DN_ALPHA = 2.0 ** 0.25
D_IN_PROJ = 6160
ADAM_LR = 0.001
ADAM_B1 = 0.9
ADAM_B2 = 0.999
ADAM_EPS = 1e-08
ADAM_WD = 0.01
ADAM_STEP = 10

N_DEV = 8
N_CHIP = 4
LANE = 128
HALF_P = 3200
P_INT = 2 * HALF_P
CONV_COLS = 3 * D_CONV
GLA_COLS = D_IN_PROJ - CONV_COLS
SP_ROWS = 32
SP_COLS = 1024
VMEM_LIMIT = 56 * 1024 * 1024

NN = ((1,), (0,))
NT = ((1,), (1,))
TN = ((0,), (0,))
MESH = pl.DeviceIdType.MESH


def _dot(a, b, dims, precision=None):
    return lax.dot_general(a, b, (dims, ((), ())), preferred_element_type=F32, precision=precision)


def _tile(n, pref):
    if n <= pref:
        return n
    t = (pref // LANE) * LANE
    while t > 0 and n % t:
        t -= LANE
    assert t > 0, (n, pref)
    return t


def _params(n_axes):
    return pltpu.CompilerParams(dimension_semantics=("arbitrary",) * n_axes, vmem_limit_bytes=VMEM_LIMIT)


def _full(shape):
    nd = len(shape)
    return pl.BlockSpec(shape, lambda *_: (0,) * nd)


def _hbm_specs(n):
    return [pl.BlockSpec(memory_space=pl.ANY)] * n


def _mm(name, mode, a, b, *, M, N, K, tm, tn, tk, outs, epilogue, extras=(), a_fn=None, a_spec=None, b_spec=None,
        deps=()):
    ni, nj, nk = M // tm, N // tn, K // tk
    assert ni * tm == M and nj * tn == N and nk * tk == K, (name, M, N, K, tm, tn, tk)
    if a_spec is None:
        a_spec = (pl.BlockSpec((tk, tm), lambda i, j, k: (k, i)) if mode == "tn"
                  else pl.BlockSpec((tm, tk), lambda i, j, k: (i, k)))
    if b_spec is None:
        b_spec = (pl.BlockSpec((tn, tk), lambda i, j, k: (j, k)) if mode == "nt"
                  else pl.BlockSpec((tk, tn), lambda i, j, k: (k, j)))
    dims = {"nn": NN, "nt": NT, "tn": TN}[mode]
    n_ex, n_out, n_dep = len(extras), len(outs), len(deps)

    def body(*refs):
        a_ref, b_ref = refs[0], refs[1]
        ex = refs[2:2 + n_ex]
        o = refs[2 + n_ex + n_dep:2 + n_ex + n_dep + n_out]
        acc_ref = refs[2 + n_ex + n_dep + n_out]
        i, j, k = pl.program_id(0), pl.program_id(1), pl.program_id(2)
        if nk > 1:
            @pl.when(k == 0)
            def _():
                acc_ref[...] = jnp.zeros_like(acc_ref)

        av = a_ref[...]
        if a_fn is not None:
            av = a_fn(av)
        part = _dot(av, b_ref[...], dims)
        if nk == 1 and epilogue is None:
            o[0][...] = part.astype(o[0].dtype)
        elif nk == 1:
            acc_ref[...] = part
            epilogue(acc_ref, ex, o, i, j)
        else:
            acc_ref[...] += part

            @pl.when(k == nk - 1)
            def _():
                if epilogue is None:
                    o[0][...] = acc_ref[...].astype(o[0].dtype)
                else:
                    epilogue(acc_ref, ex, o, i, j)

    return pl.pallas_call(
        body,
        name=name,
        grid=(ni, nj, nk),
        in_specs=[a_spec, b_spec] + [s for _, s in extras] + _hbm_specs(n_dep),
        out_specs=[s for _, s in outs],
        out_shape=[s for s, _ in outs],
        scratch_shapes=[pltpu.VMEM((8, LANE) if nk == 1 and epilogue is None else (tm, tn), F32)],
        compiler_params=_params(3),
    )(a, b, *[x for x, _ in extras], *deps)


def _mm_rows(name, mode, a, b, *, M, N, K, tm, tk, row_ins, vec_ins, row_outs, stat_outs, chunk_fn,
             b_spec=None, deps=()):
    ni, nk = M // tm, K // tk
    rc = tm // nk
    assert ni * tm == M and nk * tk == K and rc * nk == tm and rc % 16 == 0, (name, M, K, tm, tk)
    dims = {"nn": NN, "nt": NT}[mode]
    last = ni - 1

    def kk(i, k):
        return jnp.where(i < ni, k, nk - 1)

    a_spec = pl.BlockSpec((tm, tk), lambda i, k: (jnp.minimum(i, last), kk(i, k)))
    if b_spec is None:
        b_spec = (pl.BlockSpec((N, tk), lambda i, k: (0, kk(i, k))) if mode == "nt"
                  else pl.BlockSpec((tk, N), lambda i, k: (kk(i, k), 0)))
    prev_rows = lambda i, k: (jnp.maximum((i - 1) * nk + k, 0), 0)
    n_ri, n_vi, n_ro, n_so, n_dep = len(row_ins), len(vec_ins), len(row_outs), len(stat_outs), len(deps)

    def body(*refs):
        a_ref, b_ref = refs[0], refs[1]
        pos = 2
        ri = refs[pos:pos + n_ri]; pos += n_ri
        vi = refs[pos:pos + n_vi]; pos += n_vi + n_dep
        ro = refs[pos:pos + n_ro]; pos += n_ro
        so = refs[pos:pos + n_so]; pos += n_so
        accs = refs[pos:pos + 2]
        i, k = pl.program_id(0), pl.program_id(1)

        @pl.when((i == 0) & (k == 0))
        def _():
            accs[0][...] = jnp.zeros_like(accs[0])
            accs[1][...] = jnp.zeros_like(accs[1])
            for st in so:
                st[...] = jnp.zeros_like(st)

        def finish_rows(prev_ref):
            rows = pl.ds(pl.multiple_of(k * rc, rc), rc)
            done = prev_ref[rows, :]
            prev_ref[rows, :] = jnp.zeros((rc, N), F32)
            chunk_fn(done, i > 0, ri, vi, ro, so)

        def accumulate(acc_ref, after_ref):
            rp = tm // ROW_PARTS
            bv = b_ref[...]
            for part in range(ROW_PARTS):
                av = a_ref[part * rp:(part + 1) * rp, :]
                if part == 1:
                    tail = after_ref[rc - 16:rc, 0:LANE].astype(F32)
                    sixteen = jnp.uint32(16)
                    zero = lax.bitcast_convert_type(tail, jnp.uint32)
                    zero = lax.shift_right_logical(lax.shift_right_logical(zero, sixteen), sixteen)
                    av = av + jnp.tile(zero.astype(F32).astype(av.dtype), (rp // 16, tk // LANE))
                acc_ref[part * rp:(part + 1) * rp, :] += _dot(av, bv, dims)

        for parity in (0, 1):
            @pl.when((i < ni) & (lax.rem(i, 2) == parity))
            def _(parity=parity):
                finish_rows(accs[1 - parity])
                accumulate(accs[parity], ro[0])

        @pl.when(i == ni)
        def _():
            finish_rows(accs[last % 2])

    row_spec = lambda arr: pl.BlockSpec((rc, arr.shape[1]), prev_rows)
    return pl.pallas_call(
        body,
        name=name,
        grid=(ni + 1, nk),
        in_specs=[a_spec, b_spec] + [row_spec(x) for x in row_ins] + [_full(x.shape) for x in vec_ins]
        + _hbm_specs(n_dep),
        out_specs=[row_spec(s) for s in row_outs] + [_full(s.shape) for s in stat_outs],
        out_shape=list(row_outs) + list(stat_outs),
        scratch_shapes=[pltpu.VMEM((tm, N), F32), pltpu.VMEM((tm, N), F32)],
        compiler_params=_params(2),
    )(a, b, *row_ins, *vec_ins, *deps)


ROW_PARTS = 2
SUB_ROWS = 32


def _by_sub_rows(n_rows, fn):
    sums = None
    for r0 in range(0, n_rows, SUB_ROWS):
        part = fn(slice(r0, r0 + SUB_ROWS))
        if part:
            sums = part if sums is None else tuple(x + y for x, y in zip(sums, part))
    return sums


def _to_bf16(v):
    return v.astype(BF16)


def _ln_bwd(dy, xhat, rstd, g):
    dxh = dy * g
    m1 = jnp.mean(dxh, axis=-1, keepdims=True)
    m2 = jnp.mean(dxh * xhat, axis=-1, keepdims=True)
    return rstd * (dxh - m1 - xhat * m2)


def _ln_fwd(h):
    mu = jnp.mean(h, axis=-1, keepdims=True)
    xc = h - mu
    var = jnp.mean(xc * xc, axis=-1, keepdims=True)
    rstd = lax.rsqrt(var + LN_EPS)
    return xc * rstd, rstd


def _proj_fwd(x, w_full, deps=()):
    T, D = x.shape
    P = w_full.shape[1]
    tm, tn = _tile(T, 1024), _tile(P, 1280)
    return _mm("proj_fwd", "nn", x, w_full, M=T, N=P, K=D, tm=tm, tn=tn, tk=D,
               outs=[(jax.ShapeDtypeStruct((T, P), F32), pl.BlockSpec((tm, tn), lambda i, j, k: (i, j)))],
               epilogue=None, deps=deps)[0]


def _cast_bf16(x, deps=()):
    T, D = x.shape
    tm = _tile(T, 512)

    def body(x_ref, *rest):
        rest[len(deps)][...] = x_ref[...].astype(BF16)

    return pl.pallas_call(
        body,
        name="cast_x",
        grid=(T // tm,),
        in_specs=[pl.BlockSpec((tm, D), lambda i: (i, 0))] + _hbm_specs(len(deps)),
        out_specs=pl.BlockSpec((tm, D), lambda i: (i, 0)),
        out_shape=jax.ShapeDtypeStruct((T, D), BF16),
        compiler_params=_params(1),
    )(x, *deps)


def _conv_shift(h, hp):
    row = lax.broadcasted_iota(jnp.int32, h.shape, 0)
    hm1 = hp[7:8, :]
    hm2 = hp[6:7, :]
    h1 = jnp.where(row == 0, hm1, pltpu.roll(h, 1, 0))
    h2 = jnp.where(row == 0, hm2, jnp.where(row == 1, hm1, pltpu.roll(h, 2, 0)))
    return h1, h2


def _conv_fwd(proj, conv_w8, conv_g):
    T = proj.shape[0]
    tt = _tile(T, 256)
    nt = T // tt
    t8 = tt // 8

    def body(b_ref, c_ref, u_ref, cp_ref, up_ref, w_ref, g_ref, yin_ref):
        i = pl.program_id(0)
        h = c_ref[...] * u_ref[...]
        hp = jnp.where(i > 0, cp_ref[...] * up_ref[...], 0.0)
        h1, h2 = _conv_shift(h, hp)
        w = w_ref[...]
        y = w[0:1, :] * h2 + w[1:2, :] * h1 + w[2:3, :] * h
        p = b_ref[...] * y
        parts = []
        for gi in range(CONV_GROUPS):
            pg = p[:, gi * LANE:(gi + 1) * LANE]
            r = lax.rsqrt(jnp.mean(pg * pg, axis=-1, keepdims=True) + RMS_EPS)
            parts.append(pg * r)
        yn = jnp.concatenate(parts, axis=1) * g_ref[...]
        yin_ref[...] = yn.astype(BF16)

    def col(cidx):
        return pl.BlockSpec((tt, D_CONV), lambda i: (i, cidx))

    def prev(cidx):
        return pl.BlockSpec((8, D_CONV), lambda i: (jnp.maximum(i * t8 - 1, 0), cidx))

    return pl.pallas_call(
        body,
        name="conv_fwd",
        grid=(nt,),
        in_specs=[col(0), col(1), col(2), prev(1), prev(2), _full((8, D_CONV)), _full((1, D_CONV))],
        out_specs=pl.BlockSpec((tt, D_CONV), lambda i: (i, 0)),
        out_shape=jax.ShapeDtypeStruct((T, 2 * D_CONV), BF16),
        compiler_params=_params(1),
    )(proj, proj, proj, proj, proj, conv_w8, conv_g)


def _log_sigmoid(z):
    return jnp.minimum(z, 0.0) - jnp.log(1.0 + jnp.exp(-jnp.abs(z)))


STEP_CHUNKS = 4
STEP_ROWS = STEP_CHUNKS * CHUNK


def _gla_step_terms(blk, wg_ref, gb_ref):
    zl = blk[:, 3072:3200]
    z = _dot(zl.astype(BF16), wg_ref[...], NN) + gb_ref[...]
    log_a = _log_sigmoid(z) * (1.0 / GATE_TAU)
    return zl, z, _within_chunks(_causal().astype(F32), log_a)


def _causal():
    return (lax.broadcasted_iota(jnp.int32, (CHUNK, CHUNK), 0) >= lax.broadcasted_iota(jnp.int32, (CHUNK, CHUNK), 1))


def _within_chunks(tri, vals):
    return jnp.concatenate([_dot(tri, vals[c * CHUNK:(c + 1) * CHUNK, :], NN, precision=lax.Precision.HIGHEST)
                            for c in range(STEP_CHUNKS)], axis=0)


def _gla_head_terms(q, k, bcum, h):
    sl = slice(h * HEAD_K, (h + 1) * HEAD_K)
    bh = bcum[:, sl]
    bl = bh[CHUNK - 1:CHUNK, :]
    eb = jnp.exp(bh)
    enb = jnp.exp(-bh)
    eend = jnp.exp(bl - bh)
    dec = jnp.exp(bl)
    qd = q[:, sl] * (HEAD_K ** -0.5) * eb
    ki = k[:, sl] * enb
    ke = k[:, sl] * eend
    return eb, enb, eend, dec, qd, ki, ke


def _sigmoid(x):
    return 1.0 / (1.0 + jnp.exp(-x))


def _gla_fwd(proj, wg128, gbias, gng, yin, deps=()):
    T = proj.shape[0]
    nch = T // CHUNK
    nst = T // STEP_ROWS

    def body(p_ref, wg_ref, gb_ref, gn_ref, yin_in_ref, *rest):
        o_ref, st_ref, yin_ref, s_ref = rest[len(deps):]
        n = pl.program_id(0)

        @pl.when(n == 0)
        def _():
            s_ref[...] = jnp.zeros_like(s_ref)

        blk = p_ref[...]
        _, _, bcum_all = _gla_step_terms(blk, wg_ref, gb_ref)
        causal = _causal()
        gn = gn_ref[...]
        states = [s_ref[h] for h in range(GLA_HEADS)]
        for c in range(STEP_CHUNKS):
            rows = slice(c * CHUNK, (c + 1) * CHUNK)
            q, k = blk[rows, 0:512], blk[rows, 512:1024]
            v, r = blk[rows, 1024:2048], blk[rows, 2048:3072]
            bcum = bcum_all[rows, :]
            for h in range(GLA_HEADS):
                _, _, _, dec, qd, ki, ke = _gla_head_terms(q, k, bcum, h)
                vs = slice(h * HEAD_V, (h + 1) * HEAD_V)
                vb = v[:, vs].astype(BF16)
                qdb = qd.astype(BF16)
                a = jnp.where(causal, _dot(qdb, ki.astype(BF16), NT), 0.0)
                st = states[h]
                o = _dot(a.astype(BF16), vb, NN) + _dot(qdb, st.astype(BF16), NT)
                st_ref[c, h] = st
                states[h] = dec * st + _dot(vb, ke.astype(BF16), TN)
                o_ref[rows, vs] = o
                rinv = lax.rsqrt(jnp.mean(o * o, axis=-1, keepdims=True) + RMS_EPS)
                rh = r[:, vs]
                yin_ref[rows, vs] = (o * rinv * gn[:, vs] * (rh * _sigmoid(rh))).astype(BF16)
        for h in range(GLA_HEADS):
            s_ref[h] = states[h]

    return pl.pallas_call(
        body,
        name="gla_fwd",
        grid=(nst,),
        in_specs=[pl.BlockSpec((STEP_ROWS, HALF_P), lambda n: (n, 1)), _full((LANE, D_GLA_K)), _full((1, D_GLA_K)),
                  _full((1, D_GLA_V)), pl.BlockSpec(memory_space=pl.ANY)] + _hbm_specs(len(deps)),
        out_specs=[pl.BlockSpec((STEP_ROWS, D_GLA_V), lambda n: (n, 0)),
                   pl.BlockSpec((STEP_CHUNKS, GLA_HEADS, HEAD_V, HEAD_K), lambda n: (n, 0, 0, 0)),
                   pl.BlockSpec((STEP_ROWS, D_GLA_V), lambda n: (n, 1))],
        out_shape=[jax.ShapeDtypeStruct((T, D_GLA_V), F32),
                   jax.ShapeDtypeStruct((nch, GLA_HEADS, HEAD_V, HEAD_K), F32),
                   jax.ShapeDtypeStruct(yin.shape, BF16)],
        scratch_shapes=[pltpu.VMEM((GLA_HEADS, HEAD_V, HEAD_K), F32)],
        input_output_aliases={4: 2},
        compiler_params=_params(1),
    )(proj, wg128, gbias, gng, yin, *deps)


def _mix_ln1(yin, w_out, x, ln_g, ln_b, deps=()):
    T, D = x.shape
    KY = yin.shape[1]
    tm = _tile(T, 1024)

    def chunk(acc, valid, ri, vi, ro, so):
        g, b = vi[0][...], vi[1][...]

        def sub(rows):
            xhat, rstd = _ln_fwd(DN_ALPHA * ri[0][rows, :] + acc[rows, :])
            ro[0][rows, :] = xhat
            ro[1][rows, :] = (xhat * g + b).astype(BF16)
            ro[2][rows, :] = rstd

        _by_sub_rows(acc.shape[0], sub)

    return _mm_rows("mix_ln1", "nn", yin, w_out, M=T, N=D, K=KY, tm=tm, tk=_tile(KY, 512),
                    row_ins=[x], vec_ins=[ln_g, ln_b],
                    row_outs=[jax.ShapeDtypeStruct((T, D), F32), jax.ShapeDtypeStruct((T, D), BF16),
                              jax.ShapeDtypeStruct((T, 1), F32)],
                    stat_outs=[], chunk_fn=chunk, deps=deps)


def _ff_up(x1, w_up_blk, first, count, prev=None, deps=()):
    T, D = x1.shape
    nb, _, fb = w_up_blk.shape
    tm = _tile(T, 1024)
    ni = T // tm
    n_dep = len(deps) + (2 if prev is not None else 0)

    def body(a_ref, b_ref, *rest):
        ra_ref, h2_ref = rest[n_dep:n_dep + 2]
        ra = jnp.maximum(_dot(a_ref[...], b_ref[...], NN), 0.0)
        ra_ref[...] = ra.astype(BF16)
        h2_ref[...] = (ra * ra).astype(BF16)

    blk = pl.BlockSpec((tm, fb), lambda i, j: (i, first + j))
    shp = jax.ShapeDtypeStruct((T, nb * fb), BF16)
    keep = list(prev) if prev is not None else []
    return pl.pallas_call(
        body,
        name="ff_up_%d" % first,
        grid=(ni, count),
        in_specs=[pl.BlockSpec((tm, D), lambda i, j: (i, 0)),
                  pl.BlockSpec((None, D, fb), lambda i, j: (first + j, 0, 0))] + _hbm_specs(n_dep),
        out_specs=[blk, blk],
        out_shape=[shp, shp],
        input_output_aliases=({2: 0, 3: 1} if prev is not None else {}),
        compiler_params=_params(2),
    )(x1, w_up_blk, *keep, *deps)


def _ff_down_loss(h2, w_down, xhat1, target, g1, b1, g2, b2):
    T, F = h2.shape
    D = w_down.shape[1]
    tm = _tile(T, 1024)
    inv_d = 1.0 / D

    def chunk(acc, valid, ri, vi, ro, so):
        g1v, b1v, g2v, b2v = (v[...] for v in vi)

        def sub(rows):
            x1 = ri[0][rows, :] * g1v + b1v
            xhat, rstd = _ln_fwd(DN_ALPHA * x1 + acc[rows, :])
            e = xhat * g2v + b2v - ri[1][rows, :]
            dy = e * inv_d
            dh = _ln_bwd(dy, xhat, rstd, g2v)
            ro[0][rows, :] = dh
            ro[1][rows, :] = dh.astype(BF16)
            return (jnp.sum(dy * xhat, axis=0, keepdims=True), jnp.sum(dy, axis=0, keepdims=True),
                    jnp.sum(e * e, axis=0, keepdims=True))

        sg, sb, sl = _by_sub_rows(acc.shape[0], sub)
        so[0][...] += jnp.where(valid, sg, 0.0)
        so[1][...] += jnp.where(valid, sb, 0.0)
        so[2][...] += jnp.where(valid, sl * (0.5 * inv_d), 0.0)

    vshape = jax.ShapeDtypeStruct((1, D), F32)
    return _mm_rows("ff_down_loss", "nn", h2, w_down, M=T, N=D, K=F, tm=tm, tk=_tile(F, 1024),
                    row_ins=[xhat1, target], vec_ins=[g1, b1, g2, b2],
                    row_outs=[jax.ShapeDtypeStruct((T, D), F32), jax.ShapeDtypeStruct((T, D), BF16)],
                    stat_outs=[vshape, vshape, vshape], chunk_fn=chunk)


def _ff_down_bwd_act(dh3b, w_down, ra):
    T, D = dh3b.shape
    F = w_down.shape[0]
    tm, tn = _tile(T, 1024), _tile(F, 1024)

    def ep(acc_ref, ex, o, i, j):
        o[0][...] = (acc_ref[...] * (2.0 * ex[0][...].astype(F32))).astype(BF16)

    blk = pl.BlockSpec((tm, tn), lambda i, j, k: (i, j))
    return _mm("ff_down_bwd_act", "nt", dh3b, w_down, M=T, N=F, K=D, tm=tm, tn=tn, tk=D,
               outs=[(jax.ShapeDtypeStruct((T, F), BF16), blk)], extras=[(ra, blk)], epilogue=ep)[0]


def _grad_w(name, a, b, *, a_fn=None, tm_pref=1024, tn_pref=1024, tk_pref=4096, deps=()):
    T, M = a.shape
    N = b.shape[1]
    tm, tn, tk = _tile(M, tm_pref), _tile(N, tn_pref), _tile(T, tk_pref)
    return _mm(name, "tn", a, b, M=M, N=N, K=T, tm=tm, tn=tn, tk=tk, a_fn=a_fn, deps=deps,
               outs=[(jax.ShapeDtypeStruct((M, N), F32), pl.BlockSpec((tm, tn), lambda i, j, k: (i, j)))],
               epilogue=None)[0]


def _grad_w_up_blk(x1, da, nb, deps=()):
    T, D = x1.shape
    F = da.shape[1]
    fb = F // nb
    tm, tk = _tile(D, 1024), _tile(T, 4096)
    return _mm("grad_w_up", "tn", x1, da, M=D, N=F, K=T, tm=tm, tn=fb, tk=tk, deps=deps,
               outs=[(jax.ShapeDtypeStruct((nb, D, fb), F32),
                      pl.BlockSpec((None, tm, fb), lambda i, j, k: (j, i, 0)))],
               epilogue=None)[0]


def _ff_up_bwd_ln1(da, w_up_blk, dh3, xhat1, rstd1, g1, deps=()):
    T, F = da.shape
    nb, D, fb = w_up_blk.shape
    tm = _tile(T, 1024)

    def chunk(acc, valid, ri, vi, ro, so):
        g = vi[0][...]

        def sub(rows):
            dx1 = DN_ALPHA * ri[0][rows, :] + acc[rows, :]
            xhat = ri[1][rows, :]
            dh = _ln_bwd(dx1, xhat, ri[2][rows, :], g)
            ro[0][rows, :] = dh
            ro[1][rows, :] = dh.astype(BF16)
            return jnp.sum(dx1 * xhat, axis=0, keepdims=True), jnp.sum(dx1, axis=0, keepdims=True)

        sg, sb = _by_sub_rows(acc.shape[0], sub)
        so[0][...] += jnp.where(valid, sg, 0.0)
        so[1][...] += jnp.where(valid, sb, 0.0)

    nk = F // fb
    vshape = jax.ShapeDtypeStruct((1, D), F32)
    return _mm_rows("ff_up_bwd_ln1", "nt", da, w_up_blk, M=T, N=D, K=F, tm=tm, tk=fb,
                    b_spec=pl.BlockSpec((None, D, fb), lambda i, k: (jnp.where(i < T // tm, k, nk - 1), 0, 0)),
                    row_ins=[dh3, xhat1, rstd1], vec_ins=[g1],
                    row_outs=[jax.ShapeDtypeStruct((T, D), F32), jax.ShapeDtypeStruct((T, D), BF16)],
                    stat_outs=[vshape, vshape], chunk_fn=chunk, deps=deps)


def _mix_bwd(dh1b, w_out, deps=()):
    T, D = dh1b.shape
    KY = w_out.shape[0]
    tm, tn = _tile(T, 1024), _tile(KY, 1024)
    return _mm("mix_bwd", "nt", dh1b, w_out, M=T, N=KY, K=D, tm=tm, tn=tn, tk=D, deps=deps,
               outs=[(jax.ShapeDtypeStruct((T, KY), F32), pl.BlockSpec((tm, tn), lambda i, j, k: (i, j)))],
               epilogue=None)[0]


def _conv_bwd(proj, dyin, conv_w8, conv_g, deps=()):
    T = proj.shape[0]
    tt = _tile(T, 256)
    nt = T // tt
    t8 = tt // 8
    nx = tt + 8

    def body(b_ref, c_ref, u_ref, d_ref, bn_ref, cn_ref, un_ref, dn_ref, cp_ref, up_ref, w_ref, g_ref, *rest):
        dp_ref, dw_ref, dg_ref = rest[len(deps):]
        i = pl.program_id(0)

        @pl.when(i == 0)
        def _():
            dw_ref[...] = jnp.zeros_like(dw_ref)
            dg_ref[...] = jnp.zeros_like(dg_ref)

        more = i < nt - 1

        def ext(cur_ref, nxt_ref):
            return jnp.concatenate([cur_ref[...], jnp.where(more, nxt_ref[...], 0.0)], axis=0)

        bx, cx, ux, dx = ext(b_ref, bn_ref), ext(c_ref, cn_ref), ext(u_ref, un_ref), ext(d_ref, dn_ref)
        hx = cx * ux
        hp = jnp.where(i > 0, cp_ref[...] * up_ref[...], 0.0)
        h1, h2 = _conv_shift(hx, hp)
        w = w_ref[...]
        g = g_ref[...]
        yx = w[0:1, :] * h2 + w[1:2, :] * h1 + w[2:3, :] * hx
        px = bx * yx
        dps, dgs = [], []
        for gi in range(CONV_GROUPS):
            sl = slice(gi * LANE, (gi + 1) * LANE)
            pg, dg_ = px[:, sl], dx[:, sl]
            r = lax.rsqrt(jnp.mean(pg * pg, axis=-1, keepdims=True) + RMS_EPS)
            gd = g[:, sl] * dg_
            dps.append(r * gd - pg * (r * r * r) * jnp.mean(pg * gd, axis=-1, keepdims=True))
            dgs.append(jnp.sum((dg_ * pg * r)[:tt, :], axis=0, keepdims=True))
        dpx = jnp.concatenate(dps, axis=1)
        dg_ref[...] += jnp.concatenate(dgs, axis=1)
        dyx = dpx * bx
        dyc = dyx[:tt, :]
        dh = (w[2:3, :] * dyx + w[1:2, :] * pltpu.roll(dyx, nx - 1, 0) + w[0:1, :] * pltpu.roll(dyx, nx - 2, 0))[:tt, :]
        dw_ref[0:1, :] += jnp.sum(dyc * h2[:tt, :], axis=0, keepdims=True)
        dw_ref[1:2, :] += jnp.sum(dyc * h1[:tt, :], axis=0, keepdims=True)
        dw_ref[2:3, :] += jnp.sum(dyc * hx[:tt, :], axis=0, keepdims=True)
        dp_ref[:, 0:D_CONV] = (dpx * yx)[:tt, :].astype(BF16)
        dp_ref[:, D_CONV:2 * D_CONV] = (dh * u_ref[...]).astype(BF16)
        dp_ref[:, 2 * D_CONV:3 * D_CONV] = (dh * c_ref[...]).astype(BF16)
        dp_ref[:, 3 * D_CONV:HALF_P] = jnp.zeros((tt, HALF_P - 3 * D_CONV), BF16)

    def col(cidx):
        return pl.BlockSpec((tt, D_CONV), lambda i: (i, cidx))

    def nxt(cidx):
        return pl.BlockSpec((8, D_CONV), lambda i: (jnp.minimum((i + 1) * t8, T // 8 - 1), cidx))

    def prev(cidx):
        return pl.BlockSpec((8, D_CONV), lambda i: (jnp.maximum(i * t8 - 1, 0), cidx))

    return pl.pallas_call(
        body,
        name="conv_bwd",
        grid=(nt,),
        in_specs=[col(0), col(1), col(2), col(0), nxt(0), nxt(1), nxt(2), nxt(0), prev(1), prev(2),
                  _full((8, D_CONV)), _full((1, D_CONV))] + _hbm_specs(len(deps)),
        out_specs=[pl.BlockSpec((tt, HALF_P), lambda i: (i, 0)), _full((8, D_CONV)), _full((1, D_CONV))],
        out_shape=[jax.ShapeDtypeStruct((T, P_INT), BF16), jax.ShapeDtypeStruct((8, D_CONV), F32),
                   jax.ShapeDtypeStruct((1, D_CONV), F32)],
        compiler_params=_params(1),
    )(proj, proj, proj, dyin, proj, proj, proj, dyin, proj, proj, conv_w8, conv_g, *deps)


def _gla_bwd(proj, wg128, gbias, gng, o_all, states, dyin, dproj):
    T = proj.shape[0]
    nst = T // STEP_ROWS

    def body(p_ref, wg_ref, gb_ref, gn_ref, o_ref, st_ref, d_ref, dp_in_ref,
             dp_ref, dwg_ref, dgb_ref, dgn_ref, ds_ref):
        n = pl.program_id(0)

        @pl.when(n == 0)
        def _():
            ds_ref[...] = jnp.zeros_like(ds_ref)
            dwg_ref[...] = jnp.zeros_like(dwg_ref)
            dgb_ref[...] = jnp.zeros_like(dgb_ref)
            dgn_ref[...] = jnp.zeros_like(dgn_ref)

        blk = p_ref[...]
        zl, z, bcum_all = _gla_step_terms(blk, wg_ref, gb_ref)
        causal = _causal()
        gn = gn_ref[...]
        upper = jnp.logical_not(causal).astype(F32) + (lax.broadcasted_iota(jnp.int32, (CHUNK, CHUNK), 0)
                                                       == lax.broadcasted_iota(jnp.int32, (CHUNK, CHUNK), 1)
                                                       ).astype(F32)
        dstates = [ds_ref[h] for h in range(GLA_HEADS)]
        db_rows, dbl_rows, dgn_sum = [None] * STEP_CHUNKS, [None] * STEP_CHUNKS, [None] * GLA_HEADS
        for c in reversed(range(STEP_CHUNKS)):
            rows = slice(c * CHUNK, (c + 1) * CHUNK)
            q, k = blk[rows, 0:512], blk[rows, 512:1024]
            v, r = blk[rows, 1024:2048], blk[rows, 2048:3072]
            bcum = bcum_all[rows, :]
            db_parts, dbl_parts = [], []
            for h in range(GLA_HEADS):
                eb, enb, eend, dec, qd, ki, ke = _gla_head_terms(q, k, bcum, h)
                vs = slice(h * HEAD_V, (h + 1) * HEAD_V)
                ks = slice(h * HEAD_K, (h + 1) * HEAD_K)
                o = o_ref[rows, vs]
                rh = r[:, vs]
                dyg = d_ref[rows, vs]
                rinv = lax.rsqrt(jnp.mean(o * o, axis=-1, keepdims=True) + RMS_EPS)
                sg = _sigmoid(rh)
                on = o * rinv
                dr = dyg * (on * gn[:, vs]) * (sg * (1.0 + rh * (1.0 - sg)))
                don = dyg * (rh * sg)
                part = jnp.sum(don * on, axis=0, keepdims=True)
                dgn_sum[h] = part if dgn_sum[h] is None else dgn_sum[h] + part
                t = don * gn[:, vs]
                do = rinv * t - o * (rinv * rinv * rinv) * jnp.mean(o * t, axis=-1, keepdims=True)
                dob = do.astype(BF16)
                vb = v[:, vs].astype(BF16)
                qdb, kib, keb = qd.astype(BF16), ki.astype(BF16), ke.astype(BF16)
                a = jnp.where(causal, _dot(qdb, kib, NT), 0.0)
                st = st_ref[c, h]
                dst = dstates[h]
                dstb = dst.astype(BF16)
                da = jnp.where(causal, _dot(dob, vb, NT), 0.0)
                dab = da.astype(BF16)
                dv = _dot(a.astype(BF16), dob, TN) + _dot(keb, dstb, NT)
                dqd = _dot(dab, kib, NN) + _dot(dob, st.astype(BF16), NN)
                dki = _dot(dab, qdb, TN)
                dke = _dot(vb, dstb, NN)
                ddec = jnp.sum(st * dst, axis=0, keepdims=True)
                dstates[h] = dec * dst + _dot(dob, qdb, TN)
                dq = dqd * eb * (HEAD_K ** -0.5)
                dk = dki * enb + dke * eend
                db_parts.append(dqd * qd - dki * ki - dke * ke)
                dbl_parts.append(jnp.sum(dke * ke, axis=0, keepdims=True) + dec * ddec)
                dp_ref[rows, ks] = dq.astype(BF16)
                dp_ref[rows, D_GLA_K + h * HEAD_K:D_GLA_K + (h + 1) * HEAD_K] = dk.astype(BF16)
                dp_ref[rows, 1024 + h * HEAD_V:1024 + (h + 1) * HEAD_V] = dv.astype(BF16)
                dp_ref[rows, 2048 + h * HEAD_V:2048 + (h + 1) * HEAD_V] = dr.astype(BF16)
            db_rows[c] = jnp.concatenate(db_parts, axis=1)
            dbl_rows[c] = jnp.broadcast_to(jnp.concatenate(dbl_parts, axis=1), (CHUNK, D_GLA_K))
        for h in range(GLA_HEADS):
            ds_ref[h] = dstates[h]
            dgn_ref[:, h * HEAD_V:(h + 1) * HEAD_V] += dgn_sum[h]
        db = jnp.concatenate(db_rows, axis=0)
        dlog = _within_chunks(upper, db) + jnp.concatenate(dbl_rows, axis=0)
        dz = dlog * (1.0 / GATE_TAU) * (1.0 / (1.0 + jnp.exp(z)))
        dzb = dz.astype(BF16)
        dp_ref[:, 3072:3200] = _dot(dzb, wg_ref[...], NT).astype(BF16)
        dwg_ref[...] += _dot(zl.astype(BF16), dzb, TN)
        dgb_ref[...] += jnp.sum(dz, axis=0, keepdims=True)

    rev = lambda n: nst - 1 - n
    return pl.pallas_call(
        body,
        name="gla_bwd",
        grid=(nst,),
        in_specs=[pl.BlockSpec((STEP_ROWS, HALF_P), lambda n: (rev(n), 1)), _full((LANE, D_GLA_K)),
                  _full((1, D_GLA_K)), _full((1, D_GLA_V)),
                  pl.BlockSpec((STEP_ROWS, D_GLA_V), lambda n: (rev(n), 0)),
                  pl.BlockSpec((STEP_CHUNKS, GLA_HEADS, HEAD_V, HEAD_K), lambda n: (rev(n), 0, 0, 0)),
                  pl.BlockSpec((STEP_ROWS, D_GLA_V), lambda n: (rev(n), 1)), pl.BlockSpec(memory_space=pl.ANY)],
        out_specs=[pl.BlockSpec((STEP_ROWS, HALF_P), lambda n: (rev(n), 1)), _full((LANE, D_GLA_K)),
                   _full((1, D_GLA_K)), _full((1, D_GLA_V))],
        out_shape=[jax.ShapeDtypeStruct(dproj.shape, BF16), jax.ShapeDtypeStruct((LANE, D_GLA_K), F32),
                   jax.ShapeDtypeStruct((1, D_GLA_K), F32), jax.ShapeDtypeStruct((1, D_GLA_V), F32)],
        scratch_shapes=[pltpu.VMEM((GLA_HEADS, HEAD_V, HEAD_K), F32)],
        input_output_aliases={7: 0},
        compiler_params=_params(1),
    )(proj, wg128, gbias, gng, o_all, states, dyin, dproj)


def _proj_bwd_x(dproj, w_full, dh1, deps=()):
    T, P = dproj.shape
    D = w_full.shape[0]
    tm, tk = _tile(T, 512), _tile(P, 1280)

    def ep(acc_ref, ex, o, i, j):
        o[0][...] = DN_ALPHA * ex[0][...] + acc_ref[...]

    row = pl.BlockSpec((tm, D), lambda i, j, k: (i, 0))
    return _mm("proj_bwd_x", "nt", dproj, w_full, M=T, N=D, K=P, tm=tm, tn=D, tk=tk,
               outs=[(jax.ShapeDtypeStruct((T, D), F32), row)], extras=[(dh1, row)], epilogue=ep, deps=deps)[0]


def _place():
    x, y, c = lax.axis_index("x"), lax.axis_index("y"), lax.axis_index("c")
    chips = [(1 - x, y), (x, 1 - y), (1 - x, 1 - y)]
    return x, y, c, chips


def _rcopy(src, dst, ssem, rsem, dev):
    return pltpu.make_async_remote_copy(src_ref=src, dst_ref=dst, send_sem=ssem, recv_sem=rsem,
                                        device_id=dev, device_id_type=MESH)


def _all_gather(name, shards, deps=()):
    n = len(shards)

    def body(*refs):
        ins, outs = refs[:n], refs[n + len(deps):2 * n + len(deps)]
        ssem, rsem, lsem = refs[2 * n + len(deps):]
        x, y, c, chips = _place()
        me, sib = (x, y, c), (x, y, 1 - c)

        def slot(w, px, py, pc):
            return outs[w].at[4 * px + 2 * py + pc]

        started = []
        for w in range(n):
            lc = pltpu.make_async_copy(ins[w], slot(w, *me), lsem.at[w])
            lc.start()
            started.append(lc)
        sends = []
        for w in range(n):
            cp = _rcopy(ins[w], slot(w, *me), ssem.at[7 * w], rsem.at[7 * w], sib)
            cp.start()
            sends.append(cp)
            for jx, chip in enumerate(chips):
                cp = _rcopy(ins[w], slot(w, *me), ssem.at[7 * w + 1 + jx], rsem.at[7 * w + 1 + jx], (*chip, c))
                cp.start()
                sends.append(cp)
        for w in range(n):
            for jx, chip in enumerate(chips):
                blk = slot(w, *chip, c)
                _rcopy(blk, blk, ssem.at[7 * w + 1 + jx], rsem.at[7 * w + 1 + jx], me).wait_recv()
                cp = _rcopy(blk, blk, ssem.at[7 * w + 4 + jx], rsem.at[7 * w + 4 + jx], sib)
                cp.start()
                sends.append(cp)
        for w in range(n):
            blk = slot(w, x, y, 1 - c)
            _rcopy(blk, blk, ssem.at[7 * w], rsem.at[7 * w], me).wait_recv()
            for jx, chip in enumerate(chips):
                blk = slot(w, *chip, 1 - c)
                _rcopy(blk, blk, ssem.at[7 * w + 4 + jx], rsem.at[7 * w + 4 + jx], me).wait_recv()
        for cp in sends:
            cp.wait_send()
        for lc in started:
            lc.wait()

    return pl.pallas_call(
        body,
        name=name,
        in_specs=_hbm_specs(n + len(deps)),
        out_specs=_hbm_specs(n),
        out_shape=[jax.ShapeDtypeStruct((N_DEV,) + s.shape, s.dtype) for s in shards],
        scratch_shapes=[pltpu.SemaphoreType.DMA((7 * n,)), pltpu.SemaphoreType.DMA((7 * n,)),
                        pltpu.SemaphoreType.DMA((n,))],
    )(*shards, *deps)


HBM_SPEC = pl.BlockSpec(memory_space=pltpu.HBM)
SEM_SPEC = pl.BlockSpec(memory_space=pltpu.SEMAPHORE)
SIDE_EFFECT = pltpu.SideEffectType.DATAFLOW_SIDE_EFFECTING


def _cast_place(name, ids, w, deps=(), dtype=None):
    dtype = BF16 if dtype is None else dtype
    R, C = w.shape
    tr = _tile(R, 256)

    def body(ids_ref, w_ref, *rest):
        rest[len(deps)][...] = w_ref[...].astype(dtype)

    return pl.pallas_call(
        body,
        name=name,
        grid_spec=pltpu.PrefetchScalarGridSpec(
            num_scalar_prefetch=1,
            grid=(R // tr,),
            in_specs=[pl.BlockSpec((tr, C), lambda r, ids: (r, 0))] + _hbm_specs(len(deps)),
            out_specs=pl.BlockSpec((None, tr, C), lambda r, ids: (ids[0], r, 0)),
        ),
        out_shape=jax.ShapeDtypeStruct((N_DEV, R, C), dtype),
        compiler_params=_params(1),
    )(ids, w, *deps)


def _xfer_start(name, bufs, plan, n):
    nb = len(bufs)

    def body(*refs):
        ins = refs[:nb]
        ssem, rsem = refs[nb], refs[nb + 1]
        token = refs[2 * nb + 2]
        x, y, c, chips = _place()
        for k, (src, dst, dev, _) in enumerate(plan(ins, x, y, c, chips)):
            _rcopy(src, dst, ssem.at[k], rsem.at[k], dev).start()
        token[...] = jnp.zeros_like(token)

    res = pl.pallas_call(
        body,
        name=name,
        out_shape=(pltpu.SemaphoreType.DMA((n,)), pltpu.SemaphoreType.DMA((n,)),
                   *[pltpu.HBM(b.shape, b.dtype) for b in bufs], jax.ShapeDtypeStruct((8, LANE), F32)),
        in_specs=[HBM_SPEC] * nb,
        out_specs=(SEM_SPEC, SEM_SPEC, *[HBM_SPEC] * nb, pl.BlockSpec(memory_space=pltpu.VMEM)),
        input_output_aliases={i: 2 + i for i in range(nb)},
        compiler_params=pltpu.CompilerParams(has_side_effects=SIDE_EFFECT),
    )(*[pltpu.with_memory_space_constraint(b, pltpu.HBM) for b in bufs])
    return dict(sems=res[:2], bufs=list(res[2:2 + nb]), token=res[2 + nb], plan=plan, n=n)


def _xfer_wait(name, started, after):
    bufs, plan = started["bufs"], started["plan"]
    nb = len(bufs)

    def body(*refs):
        ins = refs[:nb]
        ssem, rsem = refs[nb], refs[nb + 1]
        x, y, c, chips = _place()
        for k, (src, _, dev, land) in enumerate(plan(ins, x, y, c, chips)):
            cp = _rcopy(src, land, ssem.at[k], rsem.at[k], dev)
            cp.wait_send()
            cp.wait_recv()

    res = pl.pallas_call(
        body,
        name=name,
        out_shape=tuple(pltpu.HBM(b.shape, b.dtype) for b in bufs),
        in_specs=[HBM_SPEC] * nb + [SEM_SPEC, SEM_SPEC, pl.BlockSpec(memory_space=pl.ANY)],
        out_specs=tuple([HBM_SPEC] * nb),
        input_output_aliases={i: i for i in range(nb)},
        compiler_params=pltpu.CompilerParams(has_side_effects=SIDE_EFFECT),
    )(*bufs, *started["sems"], after)
    return list(res)


def _plan_gather_chips(refs, x, y, c, chips):
    plan = []
    for land in refs:
        mine = land.at[4 * x + 2 * y + c]
        plan.append((mine, mine, (x, y, 1 - c), land.at[4 * x + 2 * y + (1 - c)]))
        for px, py in chips:
            plan.append((mine, mine, (px, py, c), land.at[4 * px + 2 * py + c]))
    return plan


def _plan_gather_pass(refs, x, y, c, chips):
    return [(land.at[4 * px + 2 * py + c], land.at[4 * px + 2 * py + c], (x, y, 1 - c),
             land.at[4 * px + 2 * py + (1 - c)]) for land in refs for px, py in chips]


def _plan_reduce_core(refs, x, y, c, chips):
    grad, recv = refs
    return [(grad.at[2 * q + (1 - c)], recv.at[q], (x, y, 1 - c), recv.at[q]) for q in range(N_CHIP)]


def _plan_reduce_chips(refs, x, y, c, chips):
    part, land = refs
    return [(part.at[2 * px + py], land.at[2 * x + y], (px, py, c), land.at[2 * px + py]) for px, py in chips]


def _chip_sums(name, ids, grad, recv):
    _, R, C = grad.shape
    tr = _tile(R, 256)

    def body(ids_ref, g_ref, r_ref, o_ref):
        o_ref[...] = (g_ref[...] + r_ref[...]).astype(BF16)

    return pl.pallas_call(
        body,
        name=name,
        grid_spec=pltpu.PrefetchScalarGridSpec(
            num_scalar_prefetch=1,
            grid=(N_CHIP - 1, R // tr),
            in_specs=[pl.BlockSpec((None, tr, C), lambda q, r, ids: (2 * ids[3 + q] + ids[2], r, 0)),
                      pl.BlockSpec((None, tr, C), lambda q, r, ids: (ids[3 + q], r, 0))],
            out_specs=pl.BlockSpec((None, tr, C), lambda q, r, ids: (ids[3 + q], r, 0)),
        ),
        out_shape=jax.ShapeDtypeStruct((N_CHIP, R, C), BF16),
        compiler_params=_params(2),
    )(ids, grad, recv)


def _adamw(w, g, m, v):
    m = ADAM_B1 * m + (1.0 - ADAM_B1) * g
    v = ADAM_B2 * v + (1.0 - ADAM_B2) * (g * g)
    m_hat = m / (1.0 - ADAM_B1 ** ADAM_STEP)
    v_hat = v / (1.0 - ADAM_B2 ** ADAM_STEP)
    delta = -ADAM_LR * (m_hat / (jnp.sqrt(v_hat) + ADAM_EPS) + ADAM_WD * w)
    return delta, m, v


def _reduce_adamw(name, ids, grad, recv, landed, w, m, v):
    _, R, C = grad.shape
    tr = _tile(R, 256)

    def body(ids_ref, g_ref, r_ref, l1_ref, l2_ref, l3_ref, w_ref, m_ref, v_ref, go_ref, do_ref, mo_ref, vo_ref):
        g = g_ref[...] + r_ref[...]
        g = g + l1_ref[...].astype(F32)
        g = g + l2_ref[...].astype(F32)
        g = g + l3_ref[...].astype(F32)
        delta, mn, vn = _adamw(w_ref[...], g, m_ref[...], v_ref[...])
        go_ref[...] = g
        do_ref[...] = delta
        mo_ref[...] = mn
        vo_ref[...] = vn

    def pick(k):
        return pl.BlockSpec((None, tr, C), lambda r, ids: (ids[k], r, 0))

    flat = pl.BlockSpec((tr, C), lambda r, ids: (r, 0))
    shp = jax.ShapeDtypeStruct((R, C), F32)
    return pl.pallas_call(
        body,
        name=name,
        grid_spec=pltpu.PrefetchScalarGridSpec(
            num_scalar_prefetch=1,
            grid=(R // tr,),
            in_specs=[pick(0), pick(1), pick(3), pick(4), pick(5), flat, flat, flat],
            out_specs=[flat, flat, flat, flat],
        ),
        out_shape=[shp, shp, shp, shp],
        compiler_params=_params(1),
    )(ids, grad, recv, landed, landed, landed, w, m, v)


SMALL = ("conv_w", "conv_norm_g", "w_gate_up", "gate_bias", "gla_norm_g", "ln1_g", "ln1_b", "ln2_g", "ln2_b")
R_LOSS = 14


def _small_rows(D, conv_cols, gate_cols):
    nv = max(1, D // SP_COLS)
    assert nv <= 2, D
    return {"conv_w": (0, 3, conv_cols), "conv_norm_g": (3, 1, D_CONV), "gate_bias": (4, 1, D_GLA_K),
            "gla_norm_g": (5, 1, D_GLA_V), "ln1_g": (6, nv, D), "ln1_b": (8, nv, D), "ln2_g": (10, nv, D),
            "ln2_b": (12, nv, D), "w_gate_up": (16, GATE_RANK, gate_cols)}


def _put(o_ref, entry, val):
    row, n_rows, cols = entry
    if val.shape[0] == 1 and n_rows > 1:
        for r in range(n_rows):
            o_ref[row + r:row + r + 1, :] = val[:, r * SP_COLS:(r + 1) * SP_COLS]
    else:
        o_ref[row:row + n_rows, 0:cols] = val[0:n_rows, 0:cols]


def _take(g, entry):
    row, n_rows, cols = entry
    if cols > SP_COLS:
        return jnp.concatenate([g[row + r:row + r + 1, :] for r in range(n_rows)], axis=1)
    return g[row:row + n_rows, 0:cols]


def _make_pack(name, rows, pieces):
    names = list(pieces)

    def body(*refs):
        o_ref = refs[len(names)]
        o_ref[...] = jnp.zeros_like(o_ref)
        for nm, ref in zip(names, refs):
            if nm == "loss":
                o_ref[R_LOSS:R_LOSS + 1, 0:1] = jnp.sum(ref[...], axis=1, keepdims=True)
            else:
                _put(o_ref, rows[nm], ref[...])

    arrs = [pieces[nm] for nm in names]
    return pl.pallas_call(
        body,
        name=name,
        grid=(1,),
        in_specs=[_full(a.shape) for a in arrs],
        out_specs=_full((SP_ROWS, SP_COLS)),
        out_shape=jax.ShapeDtypeStruct((SP_ROWS, SP_COLS), F32),
        compiler_params=_params(1),
    )(*arrs)


def _small_adamw(packs, rows, w, m, v):
    names = list(SMALL)
    n = len(names)

    def body(p_ref, *refs):
        ins, outs = refs[:3 * n], refs[3 * n:]
        g = p_ref[0]
        for dvc in range(1, N_DEV):
            g = g + p_ref[dvc]
        for i, nm in enumerate(names):
            gp = _take(g, rows[nm])
            delta, mn, vn = _adamw(ins[i][...], gp, ins[n + i][...], ins[2 * n + i][...])
            for kind, val in enumerate((gp, delta, mn, vn)):
                outs[kind * n + i][...] = val
        outs[4 * n][...] = g[R_LOSS:R_LOSS + 1, 0:1]

    arrs = [w[nm] for nm in names] + [m[nm] for nm in names] + [v[nm] for nm in names]
    shapes = [jax.ShapeDtypeStruct(w[nm].shape, F32) for nm in names] * 4 + [jax.ShapeDtypeStruct((1, 1), F32)]
    res = pl.pallas_call(
        body,
        name="small_adamw",
        grid=(1,),
        in_specs=[_full(packs.shape)] + [_full(a.shape) for a in arrs],
        out_specs=[_full(sh.shape) for sh in shapes],
        out_shape=shapes,
        compiler_params=_params(1),
    )(packs, *arrs)
    return [dict(zip(names, res[k * n:(k + 1) * n])) for k in range(4)], res[4 * n]


def _w_in_pieces():
    cs = D_IN_PROJ // N_DEV
    pieces = []
    for d in range(N_DEV):
        lo, hi = d * cs, (d + 1) * cs
        if hi <= CONV_COLS:
            pieces.append((d, 0, cs, lo))
        elif lo >= CONV_COLS:
            pieces.append((d, 0, cs, lo - CONV_COLS + HALF_P))
        else:
            pieces.append((d, 0, CONV_COLS - lo, lo))
            pieces.append((d, CONV_COLS - lo, cs, HALF_P))
    return pieces


def _w_in_full(gathered):
    nb, D, cs = gathered.shape
    tr = _tile(D, 256)

    def body(g_ref, o_ref):
        o_ref[:, CONV_COLS:HALF_P] = jnp.zeros((tr, HALF_P - CONV_COLS), o_ref.dtype)
        o_ref[:, HALF_P + GLA_COLS:P_INT] = jnp.zeros((tr, HALF_P - GLA_COLS), o_ref.dtype)
        for d, a, b, dst in _w_in_pieces():
            o_ref[:, dst:dst + (b - a)] = g_ref[d, :, a:b]

    return pl.pallas_call(
        body,
        name="w_in_full",
        grid=(D // tr,),
        in_specs=[pl.BlockSpec((nb, tr, cs), lambda r: (0, r, 0))],
        out_specs=pl.BlockSpec((tr, P_INT), lambda r: (r, 0)),
        out_shape=jax.ShapeDtypeStruct((D, P_INT), gathered.dtype),
        compiler_params=_params(1),
    )(gathered)


def _w_in_blocks(dw):
    D = dw.shape[0]
    cs = D_IN_PROJ // N_DEV
    tr = _tile(D, 256)

    def body(w_ref, o_ref):
        for d, a, b, src in _w_in_pieces():
            o_ref[d, :, a:b] = w_ref[:, src:src + (b - a)]

    return pl.pallas_call(
        body,
        name="w_in_blocks",
        grid=(D // tr,),
        in_specs=[pl.BlockSpec((tr, P_INT), lambda r: (r, 0))],
        out_specs=pl.BlockSpec((N_DEV, tr, cs), lambda r: (0, r, 0)),
        out_shape=jax.ShapeDtypeStruct((N_DEV, D, cs), dw.dtype),
        compiler_params=_params(1),
    )(dw)


BIG = ("w_in", "w_out", "w_ff_up", "w_ff_down")
ORDER = ("w_in", "conv_w", "conv_norm_g", "w_gate_up", "gate_bias", "gla_norm_g", "w_out", "ln1_g", "ln1_b",
         "w_ff_up", "w_ff_down", "ln2_g", "ln2_b")


def kernel(x, w_in, conv_w, conv_norm_g, w_gate_up, gate_bias, gla_norm_g, w_out, ln1_g, ln1_b, w_ff_up, w_ff_down, ln2_g, ln2_b, loss_target, m_w_in, m_conv_w, m_conv_norm_g, m_w_gate_up, m_gate_bias, m_gla_norm_g, m_w_out, m_ln1_g, m_ln1_b, m_w_ff_up, m_w_ff_down, m_ln2_g, m_ln2_b, v_w_in, v_conv_w, v_conv_norm_g, v_w_gate_up, v_gate_bias, v_gla_norm_g, v_w_out, v_ln1_g, v_ln1_b, v_w_ff_up, v_w_ff_down, v_ln2_g, v_ln2_b):
    T, D = x.shape[1], x.shape[2]
    xs, target = x[0], loss_target[0]
    xi, yi, ci = lax.axis_index("x"), lax.axis_index("y"), lax.axis_index("c")
    chip = 2 * xi + yi
    dev = 2 * chip + ci
    others = [jnp.where(chip <= q, q + 1, q) for q in range(N_CHIP - 1)]
    ids = jnp.stack([dev, chip, ci] + others).astype(jnp.int32)
    conv_cols, gate_cols = conv_w.shape[2], w_gate_up.shape[2]

    def gather(nm, lands):
        return _xfer_start("gather_chips_" + nm, lands, _plan_gather_chips, 4 * len(lands))

    def pass_on(nm, started, after):
        lands = _xfer_wait("gather_chips_wait_" + nm, started, after)
        return _xfer_start("gather_pass_" + nm, lands, _plan_gather_pass, 3 * len(lands))

    def landed(nm, started, after):
        return _xfer_wait("gather_pass_wait_" + nm, started, after)

    rows = _small_rows(D, conv_cols, gate_cols)
    fwd_pack = _make_pack("pack_fwd", rows, {"conv_w": conv_w[0], "w_gate_up": w_gate_up[0]})
    ga_in = gather("w_in", [_cast_place("cast_place_w_in", ids, w_in[0]),
                            _cast_place("cast_place_pack", ids, fwd_pack, dtype=F32)])
    ga, dep = [], ga_in["token"]
    m_in, v_in = m_w_in[0], v_w_in[0]
    for nm, w in zip(BIG[1:], (w_out, w_ff_up, w_ff_down)):
        deps = [dep, m_in, v_in] if nm == "w_ff_down" else [dep]
        ga.append(gather(nm, [_cast_place("cast_place_" + nm, ids, w[0], deps)]))
        dep = ga[-1]["token"]
    xb = _cast_bf16(xs, [dep])
    gp_in = pass_on("w_in", ga_in, xb)
    g_in, g_pack = landed("w_in", gp_in, gp_in["token"])
    w_full = _w_in_full(g_in)
    r_cw, r_gw = rows["conv_w"][0], rows["w_gate_up"][0]
    conv_w_full = g_pack[:, r_cw:r_cw + 3, :conv_cols].transpose(1, 0, 2).reshape(3, -1)
    gate_w_full = g_pack[:, r_gw:r_gw + GATE_RANK, :gate_cols].transpose(1, 0, 2).reshape(GATE_RANK, -1)
    conv_w8 = jnp.pad(conv_w_full, ((0, 5), (0, 0)))
    wg128 = jnp.pad(gate_w_full, ((0, LANE - GATE_RANK), (0, 0))).astype(BF16)
    proj = _proj_fwd(xb, w_full)
    yin = _conv_fwd(proj, conv_w8, conv_norm_g)
    gp_out = pass_on("w_out", ga[0], yin)
    o_all, states, yin = _gla_fwd(proj, wg128, gate_bias, gla_norm_g, yin, deps=[gp_out["token"]])
    w_out_full = landed("w_out", gp_out, o_all)[0].reshape(-1, D)
    gp_up = pass_on("w_ff_up", ga[1], o_all)
    xhat1, x1, rstd1 = _mix_ln1(yin, w_out_full, xs, ln1_g, ln1_b, deps=[gp_up["token"]])
    (w_up_blk,) = landed("w_ff_up", gp_up, x1)
    half = N_DEV // 2
    ra, h2 = _ff_up(x1, w_up_blk, 0, half)
    gp_down = pass_on("w_ff_down", ga[2], ra)
    ra, h2 = _ff_up(x1, w_up_blk, half, N_DEV - half, prev=(ra, h2), deps=[gp_down["token"]])
    w_down_full = landed("w_ff_down", gp_down, ra)[0].reshape(-1, D)
    dh3, dh3b, g_ln2_g, g_ln2_b, loss = _ff_down_loss(h2, w_down_full, xhat1, target, ln1_g, ln1_b, ln2_g, ln2_b)

    def to_core(nm, grad):
        recv = lax.empty((N_CHIP,) + grad.shape[1:], F32)
        return _xfer_start("reduce_core_" + nm, [grad, recv], _plan_reduce_core, N_CHIP)

    def to_chips(nm, started, after):
        grad, recv = _xfer_wait("reduce_core_wait_" + nm, started, after)
        part = _chip_sums("chip_sums_" + nm, ids, grad, recv)
        land = lax.empty(part.shape, BF16)
        return grad, recv, _xfer_start("reduce_chips_" + nm, [part, land], _plan_reduce_chips, N_CHIP - 1)

    da = _ff_down_bwd_act(dh3b, w_down_full, ra)
    gw_down = _grad_w("grad_w_down", h2, dh3b).reshape(N_DEV, -1, D)
    rc_down = to_core("w_ff_down", gw_down)
    gw_up = _grad_w_up_blk(x1, da, N_DEV, deps=[rc_down["token"]])
    gw_down, rv_down, rs_down = to_chips("w_ff_down", rc_down, gw_up)
    rc_up = to_core("w_ff_up", gw_up)
    dh1, dh1b, g_ln1_g, g_ln1_b = _ff_up_bwd_ln1(da, w_up_blk, dh3, xhat1, rstd1, ln1_g,
                                                 deps=[rs_down["token"], rc_up["token"]])
    gw_up, rv_up, rs_up = to_chips("w_ff_up", rc_up, dh1b)
    dyin = _mix_bwd(dh1b, w_out_full, deps=[rs_up["token"]])
    gw_out = _grad_w("grad_w_out", yin, dh1b).reshape(N_DEV, -1, D)
    rc_out = to_core("w_out", gw_out)
    dproj, g_conv_w, g_conv_g = _conv_bwd(proj, dyin, conv_w8, conv_norm_g, deps=[rc_out["token"]])
    dproj, g_gate_w, g_gate_b, g_gla_g = _gla_bwd(proj, wg128, gate_bias, gla_norm_g, o_all, states, dyin, dproj)
    gw_out, rv_out, rs_out = to_chips("w_out", rc_out, dproj)
    gw_in = _w_in_blocks(_grad_w("grad_w_in", xb, dproj, tn_pref=1280, tk_pref=2048, deps=[rs_out["token"]]))
    rc_in = to_core("w_in", gw_in)

    big = {}

    def finish(nm, grad, recv, started, w, m, v, after):
        _, land = _xfer_wait("reduce_chips_wait_" + nm, started, after)
        res = _reduce_adamw("adamw_" + nm, ids, grad, recv, land, w[0], m[0], v[0])
        big[nm] = [a[None] for a in res]
        return res[0]

    done = finish("w_ff_down", gw_down, rv_down, rs_down, w_ff_down, m_w_ff_down, v_w_ff_down, rc_in["token"])
    done = finish("w_ff_up", gw_up, rv_up, rs_up, w_ff_up, m_w_ff_up, v_w_ff_up, done)
    full_rows = _small_rows(D, D_CONV, D_GLA_K)
    pack = _make_pack("pack_grads", full_rows, {
        "conv_w": g_conv_w, "conv_norm_g": g_conv_g, "gate_bias": g_gate_b, "gla_norm_g": g_gla_g, "ln1_g": g_ln1_g,
        "ln1_b": g_ln1_b, "ln2_g": g_ln2_g, "ln2_b": g_ln2_b, "loss": loss, "w_gate_up": g_gate_w})
    (packs,) = _all_gather("gather_small_grads", [pack], deps=[done])
    gw_in, rv_in, rs_in = to_chips("w_in", rc_in, packs)
    done = finish("w_out", gw_out, rv_out, rs_out, w_out, m_w_out, v_w_out, rs_in["token"])
    grad_x = _proj_bwd_x(dproj, w_full, dh1, deps=[done])
    finish("w_in", gw_in, rv_in, rs_in, w_in, (m_in,), (v_in,), grad_x)

    def own_cols(row, n_rows, width):
        cut = lax.dynamic_slice(packs, (0, row, dev * width), (N_DEV, n_rows, width))
        return jnp.pad(cut, ((0, 0), (0, 0), (0, SP_COLS - width)))

    packs_own = jnp.concatenate([own_cols(r_cw, 3, conv_cols), packs[:, r_cw + 3:r_gw],
                                 own_cols(r_gw, GATE_RANK, gate_cols)], axis=1)
    as2d = lambda a: a[0] if a.ndim == 3 else a
    w_s = dict(zip(SMALL, map(as2d, (conv_w, conv_norm_g, w_gate_up, gate_bias, gla_norm_g, ln1_g, ln1_b, ln2_g, ln2_b))))
    m_s = dict(zip(SMALL, map(as2d, (m_conv_w, m_conv_norm_g, m_w_gate_up, m_gate_bias, m_gla_norm_g, m_ln1_g,
                                     m_ln1_b, m_ln2_g, m_ln2_b))))
    v_s = dict(zip(SMALL, map(as2d, (v_conv_w, v_conv_norm_g, v_w_gate_up, v_gate_bias, v_gla_norm_g, v_ln1_g,
                                     v_ln1_b, v_ln2_g, v_ln2_b))))
    small, loss_sum = _small_adamw(packs_own, rows, w_s, m_s, v_s)

    def leaf(kind, name):
        if name in BIG:
            return big[name][kind]
        a = small[kind][name]
        return a[None] if name in ("conv_w", "w_gate_up") else a

    out = [loss_sum[0, 0], grad_x[None]]
    for kind in range(4):
        out += [leaf(kind, nm) for nm in ORDER]
    return tuple(out)
```

```python
import jax
import jax.numpy as jnp
from jax import lax
from jax.experimental import pallas as pl
from jax.experimental.pallas import tpu as pltpu

F32 = jnp.float32
BF16 = jnp.bfloat16

D_CONV = 1024
CONV_GROUPS = 8
GLA_HEADS = 4
HEAD_K = 128
HEAD_V = 256
D_GLA_K = 512
D_GLA_V = 1024
GATE_RANK = 16
GATE_TAU = 16.0
CHUNK = 64
LN_EPS = 1e-5
RMS_EPS = 1e-6
DN_ALPHA = 2.0 ** 0.25
D_IN_PROJ = 6160
ADAM_LR = 0.001
ADAM_B1 = 0.9
ADAM_B2 = 0.999
ADAM_EPS = 1e-08
ADAM_WD = 0.01
ADAM_STEP = 10

N_DEV = 8
N_CHIP = 4
LANE = 128
HALF_P = 3200
P_INT = 2 * HALF_P
CONV_COLS = 3 * D_CONV
GLA_COLS = D_IN_PROJ - CONV_COLS
SP_ROWS = 32
SP_COLS = 1024
VMEM_LIMIT = 56 * 1024 * 1024

NN = ((1,), (0,))
NT = ((1,), (1,))
TN = ((0,), (0,))
MESH = pl.DeviceIdType.MESH


def _dot(a, b, dims, precision=None):
    return lax.dot_general(a, b, (dims, ((), ())), preferred_element_type=F32, precision=precision)


def _tile(n, pref):
    if n <= pref:
        return n
    t = (pref // LANE) * LANE
    while t > 0 and n % t:
        t -= LANE
    assert t > 0, (n, pref)
    return t


def _params(n_axes):
    return pltpu.CompilerParams(dimension_semantics=("arbitrary",) * n_axes, vmem_limit_bytes=VMEM_LIMIT)


def _full(shape):
    nd = len(shape)
    return pl.BlockSpec(shape, lambda *_: (0,) * nd)


def _hbm_specs(n):
    return [pl.BlockSpec(memory_space=pl.ANY)] * n


def _mm(name, mode, a, b, *, M, N, K, tm, tn, tk, outs, epilogue, extras=(), a_fn=None, a_spec=None, b_spec=None,
        deps=()):
    ni, nj, nk = M // tm, N // tn, K // tk
    assert ni * tm == M and nj * tn == N and nk * tk == K, (name, M, N, K, tm, tn, tk)
    if a_spec is None:
        a_spec = (pl.BlockSpec((tk, tm), lambda i, j, k: (k, i)) if mode == "tn"
                  else pl.BlockSpec((tm, tk), lambda i, j, k: (i, k)))
    if b_spec is None:
        b_spec = (pl.BlockSpec((tn, tk), lambda i, j, k: (j, k)) if mode == "nt"
                  else pl.BlockSpec((tk, tn), lambda i, j, k: (k, j)))
    dims = {"nn": NN, "nt": NT, "tn": TN}[mode]
    n_ex, n_out, n_dep = len(extras), len(outs), len(deps)

    def body(*refs):
        a_ref, b_ref = refs[0], refs[1]
        ex = refs[2:2 + n_ex]
        o = refs[2 + n_ex + n_dep:2 + n_ex + n_dep + n_out]
        acc_ref = refs[2 + n_ex + n_dep + n_out]
        i, j, k = pl.program_id(0), pl.program_id(1), pl.program_id(2)
        if nk > 1:
            @pl.when(k == 0)
            def _():
                acc_ref[...] = jnp.zeros_like(acc_ref)

        av = a_ref[...]
        if a_fn is not None:
            av = a_fn(av)
        part = _dot(av, b_ref[...], dims)
        if nk == 1 and epilogue is None:
            o[0][...] = part.astype(o[0].dtype)
        elif nk == 1:
            acc_ref[...] = part
            epilogue(acc_ref, ex, o, i, j)
        else:
            acc_ref[...] += part

            @pl.when(k == nk - 1)
            def _():
                if epilogue is None:
                    o[0][...] = acc_ref[...].astype(o[0].dtype)
                else:
                    epilogue(acc_ref, ex, o, i, j)

    return pl.pallas_call(
        body,
        name=name,
        grid=(ni, nj, nk),
        in_specs=[a_spec, b_spec] + [s for _, s in extras] + _hbm_specs(n_dep),
        out_specs=[s for _, s in outs],
        out_shape=[s for s, _ in outs],
        scratch_shapes=[pltpu.VMEM((8, LANE) if nk == 1 and epilogue is None else (tm, tn), F32)],
        compiler_params=_params(3),
    )(a, b, *[x for x, _ in extras], *deps)


def _mm_rows(name, mode, a, b, *, M, N, K, tm, tk, row_ins, vec_ins, row_outs, stat_outs, chunk_fn,
             b_spec=None, deps=()):
    ni, nk = M // tm, K // tk
    rc = tm // nk
    assert ni * tm == M and nk * tk == K and rc * nk == tm and rc % 16 == 0, (name, M, K, tm, tk)
    dims = {"nn": NN, "nt": NT}[mode]
    last = ni - 1

    def kk(i, k):
        return jnp.where(i < ni, k, nk - 1)

    a_spec = pl.BlockSpec((tm, tk), lambda i, k: (jnp.minimum(i, last), kk(i, k)))
    if b_spec is None:
        b_spec = (pl.BlockSpec((N, tk), lambda i, k: (0, kk(i, k))) if mode == "nt"
                  else pl.BlockSpec((tk, N), lambda i, k: (kk(i, k), 0)))
    prev_rows = lambda i, k: (jnp.maximum((i - 1) * nk + k, 0), 0)
    n_ri, n_vi, n_ro, n_so, n_dep = len(row_ins), len(vec_ins), len(row_outs), len(stat_outs), len(deps)

    def body(*refs):
        a_ref, b_ref = refs[0], refs[1]
        pos = 2
        ri = refs[pos:pos + n_ri]; pos += n_ri
        vi = refs[pos:pos + n_vi]; pos += n_vi + n_dep
        ro = refs[pos:pos + n_ro]; pos += n_ro
        so = refs[pos:pos + n_so]; pos += n_so
        accs = refs[pos:pos + 2]
        i, k = pl.program_id(0), pl.program_id(1)

        @pl.when((i == 0) & (k == 0))
        def _():
            accs[0][...] = jnp.zeros_like(accs[0])
            accs[1][...] = jnp.zeros_like(accs[1])
            for st in so:
                st[...] = jnp.zeros_like(st)

        def finish_rows(prev_ref):
            rows = pl.ds(pl.multiple_of(k * rc, rc), rc)
            done = prev_ref[rows, :]
            prev_ref[rows, :] = jnp.zeros((rc, N), F32)
            chunk_fn(done, i > 0, ri, vi, ro, so)

        def accumulate(acc_ref, after_ref):
            rp = tm // ROW_PARTS
            bv = b_ref[...]
            for part in range(ROW_PARTS):
                av = a_ref[part * rp:(part + 1) * rp, :]
                if part == 1:
                    tail = after_ref[rc - 16:rc, 0:LANE].astype(F32)
                    sixteen = jnp.uint32(16)
                    zero = lax.bitcast_convert_type(tail, jnp.uint32)
                    zero = lax.shift_right_logical(lax.shift_right_logical(zero, sixteen), sixteen)
                    av = av + jnp.tile(zero.astype(F32).astype(av.dtype), (rp // 16, tk // LANE))
                acc_ref[part * rp:(part + 1) * rp, :] += _dot(av, bv, dims)

        for parity in (0, 1):
            @pl.when((i < ni) & (lax.rem(i, 2) == parity))
            def _(parity=parity):
                finish_rows(accs[1 - parity])
                accumulate(accs[parity], ro[0])

        @pl.when(i == ni)
        def _():
            finish_rows(accs[last % 2])

    row_spec = lambda arr: pl.BlockSpec((rc, arr.shape[1]), prev_rows)
    return pl.pallas_call(
        body,
        name=name,
        grid=(ni + 1, nk),
        in_specs=[a_spec, b_spec] + [row_spec(x) for x in row_ins] + [_full(x.shape) for x in vec_ins]
        + _hbm_specs(n_dep),
        out_specs=[row_spec(s) for s in row_outs] + [_full(s.shape) for s in stat_outs],
        out_shape=list(row_outs) + list(stat_outs),
        scratch_shapes=[pltpu.VMEM((tm, N), F32), pltpu.VMEM((tm, N), F32)],
        compiler_params=_params(2),
    )(a, b, *row_ins, *vec_ins, *deps)


ROW_PARTS = 2
SUB_ROWS = 32


def _by_sub_rows(n_rows, fn):
    sums = None
    for r0 in range(0, n_rows, SUB_ROWS):
        part = fn(slice(r0, r0 + SUB_ROWS))
        if part:
            sums = part if sums is None else tuple(x + y for x, y in zip(sums, part))
    return sums


def _to_bf16(v):
    return v.astype(BF16)


def _ln_bwd(dy, xhat, rstd, g):
    dxh = dy * g
    m1 = jnp.mean(dxh, axis=-1, keepdims=True)
    m2 = jnp.mean(dxh * xhat, axis=-1, keepdims=True)
    return rstd * (dxh - m1 - xhat * m2)


def _ln_fwd(h):
    mu = jnp.mean(h, axis=-1, keepdims=True)
    xc = h - mu
    var = jnp.mean(xc * xc, axis=-1, keepdims=True)
    rstd = lax.rsqrt(var + LN_EPS)
    return xc * rstd, rstd


def _proj_fwd(x, w_full, deps=()):
    T, D = x.shape
    P = w_full.shape[1]
    tm, tn = _tile(T, 1024), _tile(P, 1280)
    return _mm("proj_fwd", "nn", x, w_full, M=T, N=P, K=D, tm=tm, tn=tn, tk=D,
               outs=[(jax.ShapeDtypeStruct((T, P), F32), pl.BlockSpec((tm, tn), lambda i, j, k: (i, j)))],
               epilogue=None, deps=deps)[0]


def _cast_bf16(x, deps=()):
    T, D = x.shape
    tm = _tile(T, 512)

    def body(x_ref, *rest):
        rest[len(deps)][...] = x_ref[...].astype(BF16)

    return pl.pallas_call(
        body,
        name="cast_x",
        grid=(T // tm,),
        in_specs=[pl.BlockSpec((tm, D), lambda i: (i, 0))] + _hbm_specs(len(deps)),
        out_specs=pl.BlockSpec((tm, D), lambda i: (i, 0)),
        out_shape=jax.ShapeDtypeStruct((T, D), BF16),
        compiler_params=_params(1),
    )(x, *deps)


def _conv_shift(h, hp):
    row = lax.broadcasted_iota(jnp.int32, h.shape, 0)
    hm1 = hp[7:8, :]
    hm2 = hp[6:7, :]
    h1 = jnp.where(row == 0, hm1, pltpu.roll(h, 1, 0))
    h2 = jnp.where(row == 0, hm2, jnp.where(row == 1, hm1, pltpu.roll(h, 2, 0)))
    return h1, h2


def _conv_fwd(proj, conv_w8, conv_g):
    T = proj.shape[0]
    tt = _tile(T, 256)
    nt = T // tt
    t8 = tt // 8

    def body(b_ref, c_ref, u_ref, cp_ref, up_ref, w_ref, g_ref, yin_ref):
        i = pl.program_id(0)
        h = c_ref[...] * u_ref[...]
        hp = jnp.where(i > 0, cp_ref[...] * up_ref[...], 0.0)
        h1, h2 = _conv_shift(h, hp)
        w = w_ref[...]
        y = w[0:1, :] * h2 + w[1:2, :] * h1 + w[2:3, :] * h
        p = b_ref[...] * y
        parts = []
        for gi in range(CONV_GROUPS):
            pg = p[:, gi * LANE:(gi + 1) * LANE]
            r = lax.rsqrt(jnp.mean(pg * pg, axis=-1, keepdims=True) + RMS_EPS)
            parts.append(pg * r)
        yn = jnp.concatenate(parts, axis=1) * g_ref[...]
        yin_ref[...] = yn.astype(BF16)

    def col(cidx):
        return pl.BlockSpec((tt, D_CONV), lambda i: (i, cidx))

    def prev(cidx):
        return pl.BlockSpec((8, D_CONV), lambda i: (jnp.maximum(i * t8 - 1, 0), cidx))

    return pl.pallas_call(
        body,
        name="conv_fwd",
        grid=(nt,),
        in_specs=[col(0), col(1), col(2), prev(1), prev(2), _full((8, D_CONV)), _full((1, D_CONV))],
        out_specs=pl.BlockSpec((tt, D_CONV), lambda i: (i, 0)),
        out_shape=jax.ShapeDtypeStruct((T, 2 * D_CONV), BF16),
        compiler_params=_params(1),
    )(proj, proj, proj, proj, proj, conv_w8, conv_g)


def _log_sigmoid(z):
    return jnp.minimum(z, 0.0) - jnp.log(1.0 + jnp.exp(-jnp.abs(z)))


STEP_CHUNKS = 8
STEP_ROWS = STEP_CHUNKS * CHUNK


def _gla_step_terms(blk, wg_ref, gb_ref):
    zl = blk[:, 3072:3200]
    z = _dot(zl.astype(BF16), wg_ref[...], NN) + gb_ref[...]
    log_a = _log_sigmoid(z) * (1.0 / GATE_TAU)
    return zl, z, _within_chunks(_causal().astype(F32), log_a)


def _causal():
    return (lax.broadcasted_iota(jnp.int32, (CHUNK, CHUNK), 0) >= lax.broadcasted_iota(jnp.int32, (CHUNK, CHUNK), 1))


def _within_chunks(tri, vals):
    return jnp.concatenate([_dot(tri, vals[c * CHUNK:(c + 1) * CHUNK, :], NN, precision=lax.Precision.HIGHEST)
                            for c in range(STEP_CHUNKS)], axis=0)


def _gla_head_terms(q, k, bcum, h):
    sl = slice(h * HEAD_K, (h + 1) * HEAD_K)
    bh = bcum[:, sl]
    bl = bh[CHUNK - 1:CHUNK, :]
    eb = jnp.exp(bh)
    enb = jnp.exp(-bh)
    eend = jnp.exp(bl - bh)
    dec = jnp.exp(bl)
    qd = q[:, sl] * (HEAD_K ** -0.5) * eb
    ki = k[:, sl] * enb
    ke = k[:, sl] * eend
    return eb, enb, eend, dec, qd, ki, ke


def _sigmoid(x):
    return 1.0 / (1.0 + jnp.exp(-x))


def _gla_fwd(proj, wg128, gbias, gng, yin, deps=()):
    T = proj.shape[0]
    nch = T // CHUNK
    nst = T // STEP_ROWS

    def body(p_ref, wg_ref, gb_ref, gn_ref, yin_in_ref, *rest):
        o_ref, st_ref, yin_ref, s_ref = rest[len(deps):]
        n = pl.program_id(0)

        @pl.when(n == 0)
        def _():
            s_ref[...] = jnp.zeros_like(s_ref)

        blk = p_ref[...]
        _, _, bcum_all = _gla_step_terms(blk, wg_ref, gb_ref)
        causal = _causal()
        gn = gn_ref[...]
        states = [s_ref[h] for h in range(GLA_HEADS)]
        for c in range(STEP_CHUNKS):
            rows = slice(c * CHUNK, (c + 1) * CHUNK)
            q, k = blk[rows, 0:512], blk[rows, 512:1024]
            v, r = blk[rows, 1024:2048], blk[rows, 2048:3072]
            bcum = bcum_all[rows, :]
            for h in range(GLA_HEADS):
                _, _, _, dec, qd, ki, ke = _gla_head_terms(q, k, bcum, h)
                vs = slice(h * HEAD_V, (h + 1) * HEAD_V)
                vb = v[:, vs].astype(BF16)
                qdb = qd.astype(BF16)
                a = jnp.where(causal, _dot(qdb, ki.astype(BF16), NT), 0.0)
                st = states[h]
                o = _dot(a.astype(BF16), vb, NN) + _dot(qdb, st.astype(BF16), NT)
                st_ref[c, h] = st
                states[h] = dec * st + _dot(vb, ke.astype(BF16), TN)
                o_ref[rows, vs] = o
                rinv = lax.rsqrt(jnp.mean(o * o, axis=-1, keepdims=True) + RMS_EPS)
                rh = r[:, vs]
                yin_ref[rows, vs] = (o * rinv * gn[:, vs] * (rh * _sigmoid(rh))).astype(BF16)
        for h in range(GLA_HEADS):
            s_ref[h] = states[h]

    return pl.pallas_call(
        body,
        name="gla_fwd",
        grid=(nst,),
        in_specs=[pl.BlockSpec((STEP_ROWS, HALF_P), lambda n: (n, 1)), _full((LANE, D_GLA_K)), _full((1, D_GLA_K)),
                  _full((1, D_GLA_V)), pl.BlockSpec(memory_space=pl.ANY)] + _hbm_specs(len(deps)),
        out_specs=[pl.BlockSpec((STEP_ROWS, D_GLA_V), lambda n: (n, 0)),
                   pl.BlockSpec((STEP_CHUNKS, GLA_HEADS, HEAD_V, HEAD_K), lambda n: (n, 0, 0, 0)),
                   pl.BlockSpec((STEP_ROWS, D_GLA_V), lambda n: (n, 1))],
        out_shape=[jax.ShapeDtypeStruct((T, D_GLA_V), F32),
                   jax.ShapeDtypeStruct((nch, GLA_HEADS, HEAD_V, HEAD_K), F32),
                   jax.ShapeDtypeStruct(yin.shape, BF16)],
        scratch_shapes=[pltpu.VMEM((GLA_HEADS, HEAD_V, HEAD_K), F32)],
        input_output_aliases={4: 2},
        compiler_params=_params(1),
    )(proj, wg128, gbias, gng, yin, *deps)


def _mix_ln1(yin, w_out, x, ln_g, ln_b, deps=()):
    T, D = x.shape
    KY = yin.shape[1]
    tm = _tile(T, 1024)

    def chunk(acc, valid, ri, vi, ro, so):
        g, b = vi[0][...], vi[1][...]

        def sub(rows):
            xhat, rstd = _ln_fwd(DN_ALPHA * ri[0][rows, :] + acc[rows, :])
            ro[0][rows, :] = xhat
            ro[1][rows, :] = (xhat * g + b).astype(BF16)
            ro[2][rows, :] = rstd

        _by_sub_rows(acc.shape[0], sub)

    return _mm_rows("mix_ln1", "nn", yin, w_out, M=T, N=D, K=KY, tm=tm, tk=_tile(KY, 512),
                    row_ins=[x], vec_ins=[ln_g, ln_b],
                    row_outs=[jax.ShapeDtypeStruct((T, D), F32), jax.ShapeDtypeStruct((T, D), BF16),
                              jax.ShapeDtypeStruct((T, 1), F32)],
                    stat_outs=[], chunk_fn=chunk, deps=deps)


def _ff_up(x1, w_up_blk, first, count, prev=None, deps=()):
    T, D = x1.shape
    nb, _, fb = w_up_blk.shape
    tm = _tile(T, 1024)
    ni = T // tm
    n_dep = len(deps) + (2 if prev is not None else 0)

    def body(a_ref, b_ref, *rest):
        ra_ref, h2_ref = rest[n_dep:n_dep + 2]
        ra = jnp.maximum(_dot(a_ref[...], b_ref[...], NN), 0.0)
        ra_ref[...] = ra.astype(BF16)
        h2_ref[...] = (ra * ra).astype(BF16)

    blk = pl.BlockSpec((tm, fb), lambda i, j: (i, first + j))
    shp = jax.ShapeDtypeStruct((T, nb * fb), BF16)
    keep = list(prev) if prev is not None else []
    return pl.pallas_call(
        body,
        name="ff_up_%d" % first,
        grid=(ni, count),
        in_specs=[pl.BlockSpec((tm, D), lambda i, j: (i, 0)),
                  pl.BlockSpec((None, D, fb), lambda i, j: (first + j, 0, 0))] + _hbm_specs(n_dep),
        out_specs=[blk, blk],
        out_shape=[shp, shp],
        input_output_aliases=({2: 0, 3: 1} if prev is not None else {}),
        compiler_params=_params(2),
    )(x1, w_up_blk, *keep, *deps)


def _ff_down_loss(h2, w_down, xhat1, target, g1, b1, g2, b2):
    T, F = h2.shape
    D = w_down.shape[1]
    tm = _tile(T, 1024)
    inv_d = 1.0 / D

    def chunk(acc, valid, ri, vi, ro, so):
        g1v, b1v, g2v, b2v = (v[...] for v in vi)

        def sub(rows):
            x1 = ri[0][rows, :] * g1v + b1v
            xhat, rstd = _ln_fwd(DN_ALPHA * x1 + acc[rows, :])
            e = xhat * g2v + b2v - ri[1][rows, :]
            dy = e * inv_d
            dh = _ln_bwd(dy, xhat, rstd, g2v)
            ro[0][rows, :] = dh
            ro[1][rows, :] = dh.astype(BF16)
            return (jnp.sum(dy * xhat, axis=0, keepdims=True), jnp.sum(dy, axis=0, keepdims=True),
                    jnp.sum(e * e, axis=0, keepdims=True))

        sg, sb, sl = _by_sub_rows(acc.shape[0], sub)
        so[0][...] += jnp.where(valid, sg, 0.0)
        so[1][...] += jnp.where(valid, sb, 0.0)
        so[2][...] += jnp.where(valid, sl * (0.5 * inv_d), 0.0)

    vshape = jax.ShapeDtypeStruct((1, D), F32)
    return _mm_rows("ff_down_loss", "nn", h2, w_down, M=T, N=D, K=F, tm=tm, tk=_tile(F, 1024),
                    row_ins=[xhat1, target], vec_ins=[g1, b1, g2, b2],
                    row_outs=[jax.ShapeDtypeStruct((T, D), F32), jax.ShapeDtypeStruct((T, D), BF16)],
                    stat_outs=[vshape, vshape, vshape], chunk_fn=chunk)


def _ff_down_bwd_act(dh3b, w_down, ra):
    T, D = dh3b.shape
    F = w_down.shape[0]
    tm, tn = _tile(T, 1024), _tile(F, 1024)

    def ep(acc_ref, ex, o, i, j):
        o[0][...] = (acc_ref[...] * (2.0 * ex[0][...].astype(F32))).astype(BF16)

    blk = pl.BlockSpec((tm, tn), lambda i, j, k: (i, j))
    return _mm("ff_down_bwd_act", "nt", dh3b, w_down, M=T, N=F, K=D, tm=tm, tn=tn, tk=D,
               outs=[(jax.ShapeDtypeStruct((T, F), BF16), blk)], extras=[(ra, blk)], epilogue=ep)[0]


def _grad_w(name, a, b, *, a_fn=None, tm_pref=1024, tn_pref=1024, tk_pref=4096, deps=()):
    T, M = a.shape
    N = b.shape[1]
    tm, tn, tk = _tile(M, tm_pref), _tile(N, tn_pref), _tile(T, tk_pref)
    return _mm(name, "tn", a, b, M=M, N=N, K=T, tm=tm, tn=tn, tk=tk, a_fn=a_fn, deps=deps,
               outs=[(jax.ShapeDtypeStruct((M, N), F32), pl.BlockSpec((tm, tn), lambda i, j, k: (i, j)))],
               epilogue=None)[0]


def _grad_w_up_blk(x1, da, nb, deps=()):
    T, D = x1.shape
    F = da.shape[1]
    fb = F // nb
    tm, tk = _tile(D, 1024), _tile(T, 4096)
    return _mm("grad_w_up", "tn", x1, da, M=D, N=F, K=T, tm=tm, tn=fb, tk=tk, deps=deps,
               outs=[(jax.ShapeDtypeStruct((nb, D, fb), F32),
                      pl.BlockSpec((None, tm, fb), lambda i, j, k: (j, i, 0)))],
               epilogue=None)[0]


def _ff_up_bwd_ln1(da, w_up_blk, dh3, xhat1, rstd1, g1, deps=()):
    T, F = da.shape
    nb, D, fb = w_up_blk.shape
    tm = _tile(T, 1024)

    def chunk(acc, valid, ri, vi, ro, so):
        g = vi[0][...]

        def sub(rows):
            dx1 = DN_ALPHA * ri[0][rows, :] + acc[rows, :]
            xhat = ri[1][rows, :]
            dh = _ln_bwd(dx1, xhat, ri[2][rows, :], g)
            ro[0][rows, :] = dh
            ro[1][rows, :] = dh.astype(BF16)
            return jnp.sum(dx1 * xhat, axis=0, keepdims=True), jnp.sum(dx1, axis=0, keepdims=True)

        sg, sb = _by_sub_rows(acc.shape[0], sub)
        so[0][...] += jnp.where(valid, sg, 0.0)
        so[1][...] += jnp.where(valid, sb, 0.0)

    nk = F // fb
    vshape = jax.ShapeDtypeStruct((1, D), F32)
    return _mm_rows("ff_up_bwd_ln1", "nt", da, w_up_blk, M=T, N=D, K=F, tm=tm, tk=fb,
                    b_spec=pl.BlockSpec((None, D, fb), lambda i, k: (jnp.where(i < T // tm, k, nk - 1), 0, 0)),
                    row_ins=[dh3, xhat1, rstd1], vec_ins=[g1],
                    row_outs=[jax.ShapeDtypeStruct((T, D), F32), jax.ShapeDtypeStruct((T, D), BF16)],
                    stat_outs=[vshape, vshape], chunk_fn=chunk, deps=deps)


def _mix_bwd(dh1b, w_out, deps=()):
    T, D = dh1b.shape
    KY = w_out.shape[0]
    tm, tn = _tile(T, 1024), _tile(KY, 1024)
    return _mm("mix_bwd", "nt", dh1b, w_out, M=T, N=KY, K=D, tm=tm, tn=tn, tk=D, deps=deps,
               outs=[(jax.ShapeDtypeStruct((T, KY), F32), pl.BlockSpec((tm, tn), lambda i, j, k: (i, j)))],
               epilogue=None)[0]


def _conv_bwd(proj, dyin, conv_w8, conv_g, deps=()):
    T = proj.shape[0]
    tt = _tile(T, 256)
    nt = T // tt
    t8 = tt // 8
    nx = tt + 8

    def body(b_ref, c_ref, u_ref, d_ref, bn_ref, cn_ref, un_ref, dn_ref, cp_ref, up_ref, w_ref, g_ref, *rest):
        dp_ref, dw_ref, dg_ref = rest[len(deps):]
        i = pl.program_id(0)

        @pl.when(i == 0)
        def _():
            dw_ref[...] = jnp.zeros_like(dw_ref)
            dg_ref[...] = jnp.zeros_like(dg_ref)

        more = i < nt - 1

        def ext(cur_ref, nxt_ref):
            return jnp.concatenate([cur_ref[...], jnp.where(more, nxt_ref[...], 0.0)], axis=0)

        bx, cx, ux, dx = ext(b_ref, bn_ref), ext(c_ref, cn_ref), ext(u_ref, un_ref), ext(d_ref, dn_ref)
        hx = cx * ux
        hp = jnp.where(i > 0, cp_ref[...] * up_ref[...], 0.0)
        h1, h2 = _conv_shift(hx, hp)
        w = w_ref[...]
        g = g_ref[...]
        yx = w[0:1, :] * h2 + w[1:2, :] * h1 + w[2:3, :] * hx
        px = bx * yx
        dps, dgs = [], []
        for gi in range(CONV_GROUPS):
            sl = slice(gi * LANE, (gi + 1) * LANE)
            pg, dg_ = px[:, sl], dx[:, sl]
            r = lax.rsqrt(jnp.mean(pg * pg, axis=-1, keepdims=True) + RMS_EPS)
            gd = g[:, sl] * dg_
            dps.append(r * gd - pg * (r * r * r) * jnp.mean(pg * gd, axis=-1, keepdims=True))
            dgs.append(jnp.sum((dg_ * pg * r)[:tt, :], axis=0, keepdims=True))
        dpx = jnp.concatenate(dps, axis=1)
        dg_ref[...] += jnp.concatenate(dgs, axis=1)
        dyx = dpx * bx
        dyc = dyx[:tt, :]
        dh = (w[2:3, :] * dyx + w[1:2, :] * pltpu.roll(dyx, nx - 1, 0) + w[0:1, :] * pltpu.roll(dyx, nx - 2, 0))[:tt, :]
        dw_ref[0:1, :] += jnp.sum(dyc * h2[:tt, :], axis=0, keepdims=True)
        dw_ref[1:2, :] += jnp.sum(dyc * h1[:tt, :], axis=0, keepdims=True)
        dw_ref[2:3, :] += jnp.sum(dyc * hx[:tt, :], axis=0, keepdims=True)
        dp_ref[:, 0:D_CONV] = (dpx * yx)[:tt, :].astype(BF16)
        dp_ref[:, D_CONV:2 * D_CONV] = (dh * u_ref[...]).astype(BF16)
        dp_ref[:, 2 * D_CONV:3 * D_CONV] = (dh * c_ref[...]).astype(BF16)
        dp_ref[:, 3 * D_CONV:HALF_P] = jnp.zeros((tt, HALF_P - 3 * D_CONV), BF16)

    def col(cidx):
        return pl.BlockSpec((tt, D_CONV), lambda i: (i, cidx))

    def nxt(cidx):
        return pl.BlockSpec((8, D_CONV), lambda i: (jnp.minimum((i + 1) * t8, T // 8 - 1), cidx))

    def prev(cidx):
        return pl.BlockSpec((8, D_CONV), lambda i: (jnp.maximum(i * t8 - 1, 0), cidx))

    return pl.pallas_call(
        body,
        name="conv_bwd",
        grid=(nt,),
        in_specs=[col(0), col(1), col(2), col(0), nxt(0), nxt(1), nxt(2), nxt(0), prev(1), prev(2),
                  _full((8, D_CONV)), _full((1, D_CONV))] + _hbm_specs(len(deps)),
        out_specs=[pl.BlockSpec((tt, HALF_P), lambda i: (i, 0)), _full((8, D_CONV)), _full((1, D_CONV))],
        out_shape=[jax.ShapeDtypeStruct((T, P_INT), BF16), jax.ShapeDtypeStruct((8, D_CONV), F32),
                   jax.ShapeDtypeStruct((1, D_CONV), F32)],
        compiler_params=_params(1),
    )(proj, proj, proj, dyin, proj, proj, proj, dyin, proj, proj, conv_w8, conv_g, *deps)


def _gla_bwd(proj, wg128, gbias, gng, o_all, states, dyin, dproj):
    T = proj.shape[0]
    nst = T // STEP_ROWS

    def body(p_ref, wg_ref, gb_ref, gn_ref, o_ref, st_ref, d_ref, dp_in_ref,
             dp_ref, dwg_ref, dgb_ref, dgn_ref, ds_ref):
        n = pl.program_id(0)

        @pl.when(n == 0)
        def _():
            ds_ref[...] = jnp.zeros_like(ds_ref)
            dwg_ref[...] = jnp.zeros_like(dwg_ref)
            dgb_ref[...] = jnp.zeros_like(dgb_ref)
            dgn_ref[...] = jnp.zeros_like(dgn_ref)

        blk = p_ref[...]
        zl, z, bcum_all = _gla_step_terms(blk, wg_ref, gb_ref)
        causal = _causal()
        gn = gn_ref[...]
        upper = jnp.logical_not(causal).astype(F32) + (lax.broadcasted_iota(jnp.int32, (CHUNK, CHUNK), 0)
                                                       == lax.broadcasted_iota(jnp.int32, (CHUNK, CHUNK), 1)
                                                       ).astype(F32)
        dstates = [ds_ref[h] for h in range(GLA_HEADS)]
        db_rows, dbl_rows, dgn_sum = [None] * STEP_CHUNKS, [None] * STEP_CHUNKS, [None] * GLA_HEADS
        for c in reversed(range(STEP_CHUNKS)):
            rows = slice(c * CHUNK, (c + 1) * CHUNK)
            q, k = blk[rows, 0:512], blk[rows, 512:1024]
            v, r = blk[rows, 1024:2048], blk[rows, 2048:3072]
            bcum = bcum_all[rows, :]
            db_parts, dbl_parts = [], []
            for h in range(GLA_HEADS):
                eb, enb, eend, dec, qd, ki, ke = _gla_head_terms(q, k, bcum, h)
                vs = slice(h * HEAD_V, (h + 1) * HEAD_V)
                ks = slice(h * HEAD_K, (h + 1) * HEAD_K)
                o = o_ref[rows, vs]
                rh = r[:, vs]
                dyg = d_ref[rows, vs]
                rinv = lax.rsqrt(jnp.mean(o * o, axis=-1, keepdims=True) + RMS_EPS)
                sg = _sigmoid(rh)
                on = o * rinv
                dr = dyg * (on * gn[:, vs]) * (sg * (1.0 + rh * (1.0 - sg)))
                don = dyg * (rh * sg)
                part = jnp.sum(don * on, axis=0, keepdims=True)
                dgn_sum[h] = part if dgn_sum[h] is None else dgn_sum[h] + part
                t = don * gn[:, vs]
                do = rinv * t - o * (rinv * rinv * rinv) * jnp.mean(o * t, axis=-1, keepdims=True)
                dob = do.astype(BF16)
                vb = v[:, vs].astype(BF16)
                qdb, kib, keb = qd.astype(BF16), ki.astype(BF16), ke.astype(BF16)
                a = jnp.where(causal, _dot(qdb, kib, NT), 0.0)
                st = st_ref[c, h]
                dst = dstates[h]
                dstb = dst.astype(BF16)
                da = jnp.where(causal, _dot(dob, vb, NT), 0.0)
                dab = da.astype(BF16)
                dv = _dot(a.astype(BF16), dob, TN) + _dot(keb, dstb, NT)
                dqd = _dot(dab, kib, NN) + _dot(dob, st.astype(BF16), NN)
                dki = _dot(dab, qdb, TN)
                dke = _dot(vb, dstb, NN)
                ddec = jnp.sum(st * dst, axis=0, keepdims=True)
                dstates[h] = dec * dst + _dot(dob, qdb, TN)
                dq = dqd * eb * (HEAD_K ** -0.5)
                dk = dki * enb + dke * eend
                db_parts.append(dqd * qd - dki * ki - dke * ke)
                dbl_parts.append(jnp.sum(dke * ke, axis=0, keepdims=True) + dec * ddec)
                dp_ref[rows, ks] = dq.astype(BF16)
                dp_ref[rows, D_GLA_K + h * HEAD_K:D_GLA_K + (h + 1) * HEAD_K] = dk.astype(BF16)
                dp_ref[rows, 1024 + h * HEAD_V:1024 + (h + 1) * HEAD_V] = dv.astype(BF16)
                dp_ref[rows, 2048 + h * HEAD_V:2048 + (h + 1) * HEAD_V] = dr.astype(BF16)
            db_rows[c] = jnp.concatenate(db_parts, axis=1)
            dbl_rows[c] = jnp.broadcast_to(jnp.concatenate(dbl_parts, axis=1), (CHUNK, D_GLA_K))
        for h in range(GLA_HEADS):
            ds_ref[h] = dstates[h]
            dgn_ref[:, h * HEAD_V:(h + 1) * HEAD_V] += dgn_sum[h]
        db = jnp.concatenate(db_rows, axis=0)
        dlog = _within_chunks(upper, db) + jnp.concatenate(dbl_rows, axis=0)
        dz = dlog * (1.0 / GATE_TAU) * (1.0 / (1.0 + jnp.exp(z)))
        dzb = dz.astype(BF16)
        dp_ref[:, 3072:3200] = _dot(dzb, wg_ref[...], NT).astype(BF16)
        dwg_ref[...] += _dot(zl.astype(BF16), dzb, TN)
        dgb_ref[...] += jnp.sum(dz, axis=0, keepdims=True)

    rev = lambda n: nst - 1 - n
    return pl.pallas_call(
        body,
        name="gla_bwd",
        grid=(nst,),
        in_specs=[pl.BlockSpec((STEP_ROWS, HALF_P), lambda n: (rev(n), 1)), _full((LANE, D_GLA_K)),
                  _full((1, D_GLA_K)), _full((1, D_GLA_V)),
                  pl.BlockSpec((STEP_ROWS, D_GLA_V), lambda n: (rev(n), 0)),
                  pl.BlockSpec((STEP_CHUNKS, GLA_HEADS, HEAD_V, HEAD_K), lambda n: (rev(n), 0, 0, 0)),
                  pl.BlockSpec((STEP_ROWS, D_GLA_V), lambda n: (rev(n), 1)), pl.BlockSpec(memory_space=pl.ANY)],
        out_specs=[pl.BlockSpec((STEP_ROWS, HALF_P), lambda n: (rev(n), 1)), _full((LANE, D_GLA_K)),
                   _full((1, D_GLA_K)), _full((1, D_GLA_V))],
        out_shape=[jax.ShapeDtypeStruct(dproj.shape, BF16), jax.ShapeDtypeStruct((LANE, D_GLA_K), F32),
                   jax.ShapeDtypeStruct((1, D_GLA_K), F32), jax.ShapeDtypeStruct((1, D_GLA_V), F32)],
        scratch_shapes=[pltpu.VMEM((GLA_HEADS, HEAD_V, HEAD_K), F32)],
        input_output_aliases={7: 0},
        compiler_params=_params(1),
    )(proj, wg128, gbias, gng, o_all, states, dyin, dproj)


def _proj_bwd_x(dproj, w_full, dh1, deps=()):
    T, P = dproj.shape
    D = w_full.shape[0]
    tm, tk = _tile(T, 512), _tile(P, 1280)

    def ep(acc_ref, ex, o, i, j):
        o[0][...] = DN_ALPHA * ex[0][...] + acc_ref[...]

    row = pl.BlockSpec((tm, D), lambda i, j, k: (i, 0))
    return _mm("proj_bwd_x", "nt", dproj, w_full, M=T, N=D, K=P, tm=tm, tn=D, tk=tk,
               outs=[(jax.ShapeDtypeStruct((T, D), F32), row)], extras=[(dh1, row)], epilogue=ep, deps=deps)[0]


def _place():
    x, y, c = lax.axis_index("x"), lax.axis_index("y"), lax.axis_index("c")
    chips = [(1 - x, y), (x, 1 - y), (1 - x, 1 - y)]
    return x, y, c, chips


def _rcopy(src, dst, ssem, rsem, dev):
    return pltpu.make_async_remote_copy(src_ref=src, dst_ref=dst, send_sem=ssem, recv_sem=rsem,
                                        device_id=dev, device_id_type=MESH)


def _all_gather(name, shards, deps=()):
    n = len(shards)

    def body(*refs):
        ins, outs = refs[:n], refs[n + len(deps):2 * n + len(deps)]
        ssem, rsem, lsem = refs[2 * n + len(deps):]
        x, y, c, chips = _place()
        me, sib = (x, y, c), (x, y, 1 - c)

        def slot(w, px, py, pc):
            return outs[w].at[4 * px + 2 * py + pc]

        started = []
        for w in range(n):
            lc = pltpu.make_async_copy(ins[w], slot(w, *me), lsem.at[w])
            lc.start()
            started.append(lc)
        sends = []
        for w in range(n):
            cp = _rcopy(ins[w], slot(w, *me), ssem.at[7 * w], rsem.at[7 * w], sib)
            cp.start()
            sends.append(cp)
            for jx, chip in enumerate(chips):
                cp = _rcopy(ins[w], slot(w, *me), ssem.at[7 * w + 1 + jx], rsem.at[7 * w + 1 + jx], (*chip, c))
                cp.start()
                sends.append(cp)
        for w in range(n):
            for jx, chip in enumerate(chips):
                blk = slot(w, *chip, c)
                _rcopy(blk, blk, ssem.at[7 * w + 1 + jx], rsem.at[7 * w + 1 + jx], me).wait_recv()
                cp = _rcopy(blk, blk, ssem.at[7 * w + 4 + jx], rsem.at[7 * w + 4 + jx], sib)
                cp.start()
                sends.append(cp)
        for w in range(n):
            blk = slot(w, x, y, 1 - c)
            _rcopy(blk, blk, ssem.at[7 * w], rsem.at[7 * w], me).wait_recv()
            for jx, chip in enumerate(chips):
                blk = slot(w, *chip, 1 - c)
                _rcopy(blk, blk, ssem.at[7 * w + 4 + jx], rsem.at[7 * w + 4 + jx], me).wait_recv()
        for cp in sends:
            cp.wait_send()
        for lc in started:
            lc.wait()

    return pl.pallas_call(
        body,
        name=name,
        in_specs=_hbm_specs(n + len(deps)),
        out_specs=_hbm_specs(n),
        out_shape=[jax.ShapeDtypeStruct((N_DEV,) + s.shape, s.dtype) for s in shards],
        scratch_shapes=[pltpu.SemaphoreType.DMA((7 * n,)), pltpu.SemaphoreType.DMA((7 * n,)),
                        pltpu.SemaphoreType.DMA((n,))],
    )(*shards, *deps)


HBM_SPEC = pl.BlockSpec(memory_space=pltpu.HBM)
SEM_SPEC = pl.BlockSpec(memory_space=pltpu.SEMAPHORE)
SIDE_EFFECT = pltpu.SideEffectType.DATAFLOW_SIDE_EFFECTING


def _cast_place(name, ids, w, deps=(), dtype=None):
    dtype = BF16 if dtype is None else dtype
    R, C = w.shape
    tr = _tile(R, 256)

    def body(ids_ref, w_ref, *rest):
        rest[len(deps)][...] = w_ref[...].astype(dtype)

    return pl.pallas_call(
        body,
        name=name,
        grid_spec=pltpu.PrefetchScalarGridSpec(
            num_scalar_prefetch=1,
            grid=(R // tr,),
            in_specs=[pl.BlockSpec((tr, C), lambda r, ids: (r, 0))] + _hbm_specs(len(deps)),
            out_specs=pl.BlockSpec((None, tr, C), lambda r, ids: (ids[0], r, 0)),
        ),
        out_shape=jax.ShapeDtypeStruct((N_DEV, R, C), dtype),
        compiler_params=_params(1),
    )(ids, w, *deps)


def _xfer_start(name, bufs, plan, n):
    nb = len(bufs)

    def body(*refs):
        ins = refs[:nb]
        ssem, rsem = refs[nb], refs[nb + 1]
        token = refs[2 * nb + 2]
        x, y, c, chips = _place()
        for k, (src, dst, dev, _) in enumerate(plan(ins, x, y, c, chips)):
            _rcopy(src, dst, ssem.at[k], rsem.at[k], dev).start()
        token[...] = jnp.zeros_like(token)

    res = pl.pallas_call(
        body,
        name=name,
        out_shape=(pltpu.SemaphoreType.DMA((n,)), pltpu.SemaphoreType.DMA((n,)),
                   *[pltpu.HBM(b.shape, b.dtype) for b in bufs], jax.ShapeDtypeStruct((8, LANE), F32)),
        in_specs=[HBM_SPEC] * nb,
        out_specs=(SEM_SPEC, SEM_SPEC, *[HBM_SPEC] * nb, pl.BlockSpec(memory_space=pltpu.VMEM)),
        input_output_aliases={i: 2 + i for i in range(nb)},
        compiler_params=pltpu.CompilerParams(has_side_effects=SIDE_EFFECT),
    )(*[pltpu.with_memory_space_constraint(b, pltpu.HBM) for b in bufs])
    return dict(sems=res[:2], bufs=list(res[2:2 + nb]), token=res[2 + nb], plan=plan, n=n)


def _xfer_wait(name, started, after):
    bufs, plan = started["bufs"], started["plan"]
    nb = len(bufs)

    def body(*refs):
        ins = refs[:nb]
        ssem, rsem = refs[nb], refs[nb + 1]
        x, y, c, chips = _place()
        for k, (src, _, dev, land) in enumerate(plan(ins, x, y, c, chips)):
            cp = _rcopy(src, land, ssem.at[k], rsem.at[k], dev)
            cp.wait_send()
            cp.wait_recv()

    res = pl.pallas_call(
        body,
        name=name,
        out_shape=tuple(pltpu.HBM(b.shape, b.dtype) for b in bufs),
        in_specs=[HBM_SPEC] * nb + [SEM_SPEC, SEM_SPEC, pl.BlockSpec(memory_space=pl.ANY)],
        out_specs=tuple([HBM_SPEC] * nb),
        input_output_aliases={i: i for i in range(nb)},
        compiler_params=pltpu.CompilerParams(has_side_effects=SIDE_EFFECT),
    )(*bufs, *started["sems"], after)
    return list(res)


def _plan_gather_chips(refs, x, y, c, chips):
    plan = []
    for land in refs:
        mine = land.at[4 * x + 2 * y + c]
        plan.append((mine, mine, (x, y, 1 - c), land.at[4 * x + 2 * y + (1 - c)]))
        for px, py in chips:
            plan.append((mine, mine, (px, py, c), land.at[4 * px + 2 * py + c]))
    return plan


def _plan_gather_pass(refs, x, y, c, chips):
    return [(land.at[4 * px + 2 * py + c], land.at[4 * px + 2 * py + c], (x, y, 1 - c),
             land.at[4 * px + 2 * py + (1 - c)]) for land in refs for px, py in chips]


def _plan_reduce_core(refs, x, y, c, chips):
    grad, recv = refs
    return [(grad.at[2 * q + (1 - c)], recv.at[q], (x, y, 1 - c), recv.at[q]) for q in range(N_CHIP)]


def _plan_reduce_chips(refs, x, y, c, chips):
    part, land = refs
    return [(part.at[2 * px + py], land.at[2 * x + y], (px, py, c), land.at[2 * px + py]) for px, py in chips]


def _chip_sums(name, ids, grad, recv):
    _, R, C = grad.shape
    tr = _tile(R, 256)

    def body(ids_ref, g_ref, r_ref, o_ref):
        o_ref[...] = (g_ref[...] + r_ref[...]).astype(BF16)

    return pl.pallas_call(
        body,
        name=name,
        grid_spec=pltpu.PrefetchScalarGridSpec(
            num_scalar_prefetch=1,
            grid=(N_CHIP - 1, R // tr),
            in_specs=[pl.BlockSpec((None, tr, C), lambda q, r, ids: (2 * ids[3 + q] + ids[2], r, 0)),
                      pl.BlockSpec((None, tr, C), lambda q, r, ids: (ids[3 + q], r, 0))],
            out_specs=pl.BlockSpec((None, tr, C), lambda q, r, ids: (ids[3 + q], r, 0)),
        ),
        out_shape=jax.ShapeDtypeStruct((N_CHIP, R, C), BF16),
        compiler_params=_params(2),
    )(ids, grad, recv)


def _adamw(w, g, m, v):
    m = ADAM_B1 * m + (1.0 - ADAM_B1) * g
    v = ADAM_B2 * v + (1.0 - ADAM_B2) * (g * g)
    m_hat = m / (1.0 - ADAM_B1 ** ADAM_STEP)
    v_hat = v / (1.0 - ADAM_B2 ** ADAM_STEP)
    delta = -ADAM_LR * (m_hat / (jnp.sqrt(v_hat) + ADAM_EPS) + ADAM_WD * w)
    return delta, m, v


def _reduce_adamw(name, ids, grad, recv, landed, w, m, v):
    _, R, C = grad.shape
    tr = _tile(R, 256)

    def body(ids_ref, g_ref, r_ref, l1_ref, l2_ref, l3_ref, w_ref, m_ref, v_ref, go_ref, do_ref, mo_ref, vo_ref):
        g = g_ref[...] + r_ref[...]
        g = g + l1_ref[...].astype(F32)
        g = g + l2_ref[...].astype(F32)
        g = g + l3_ref[...].astype(F32)
        delta, mn, vn = _adamw(w_ref[...], g, m_ref[...], v_ref[...])
        go_ref[...] = g
        do_ref[...] = delta
        mo_ref[...] = mn
        vo_ref[...] = vn

    def pick(k):
        return pl.BlockSpec((None, tr, C), lambda r, ids: (ids[k], r, 0))

    flat = pl.BlockSpec((tr, C), lambda r, ids: (r, 0))
    shp = jax.ShapeDtypeStruct((R, C), F32)
    return pl.pallas_call(
        body,
        name=name,
        grid_spec=pltpu.PrefetchScalarGridSpec(
            num_scalar_prefetch=1,
            grid=(R // tr,),
            in_specs=[pick(0), pick(1), pick(3), pick(4), pick(5), flat, flat, flat],
            out_specs=[flat, flat, flat, flat],
        ),
        out_shape=[shp, shp, shp, shp],
        compiler_params=_params(1),
    )(ids, grad, recv, landed, landed, landed, w, m, v)


SMALL = ("conv_w", "conv_norm_g", "w_gate_up", "gate_bias", "gla_norm_g", "ln1_g", "ln1_b", "ln2_g", "ln2_b")
R_LOSS = 14


def _small_rows(D, conv_cols, gate_cols):
    nv = max(1, D // SP_COLS)
    assert nv <= 2, D
    return {"conv_w": (0, 3, conv_cols), "conv_norm_g": (3, 1, D_CONV), "gate_bias": (4, 1, D_GLA_K),
            "gla_norm_g": (5, 1, D_GLA_V), "ln1_g": (6, nv, D), "ln1_b": (8, nv, D), "ln2_g": (10, nv, D),
            "ln2_b": (12, nv, D), "w_gate_up": (16, GATE_RANK, gate_cols)}


def _put(o_ref, entry, val):
    row, n_rows, cols = entry
    if val.shape[0] == 1 and n_rows > 1:
        for r in range(n_rows):
            o_ref[row + r:row + r + 1, :] = val[:, r * SP_COLS:(r + 1) * SP_COLS]
    else:
        o_ref[row:row + n_rows, 0:cols] = val[0:n_rows, 0:cols]


def _take(g, entry):
    row, n_rows, cols = entry
    if cols > SP_COLS:
        return jnp.concatenate([g[row + r:row + r + 1, :] for r in range(n_rows)], axis=1)
    return g[row:row + n_rows, 0:cols]


def _make_pack(name, rows, pieces):
    names = list(pieces)

    def body(*refs):
        o_ref = refs[len(names)]
        o_ref[...] = jnp.zeros_like(o_ref)
        for nm, ref in zip(names, refs):
            if nm == "loss":
                o_ref[R_LOSS:R_LOSS + 1, 0:1] = jnp.sum(ref[...], axis=1, keepdims=True)
            else:
                _put(o_ref, rows[nm], ref[...])

    arrs = [pieces[nm] for nm in names]
    return pl.pallas_call(
        body,
        name=name,
        grid=(1,),
        in_specs=[_full(a.shape) for a in arrs],
        out_specs=_full((SP_ROWS, SP_COLS)),
        out_shape=jax.ShapeDtypeStruct((SP_ROWS, SP_COLS), F32),
        compiler_params=_params(1),
    )(*arrs)


def _small_adamw(packs, rows, w, m, v):
    names = list(SMALL)
    n = len(names)

    def body(p_ref, *refs):
        ins, outs = refs[:3 * n], refs[3 * n:]
        g = p_ref[0]
        for dvc in range(1, N_DEV):
            g = g + p_ref[dvc]
        for i, nm in enumerate(names):
            gp = _take(g, rows[nm])
            delta, mn, vn = _adamw(ins[i][...], gp, ins[n + i][...], ins[2 * n + i][...])
            for kind, val in enumerate((gp, delta, mn, vn)):
                outs[kind * n + i][...] = val
        outs[4 * n][...] = g[R_LOSS:R_LOSS + 1, 0:1]

    arrs = [w[nm] for nm in names] + [m[nm] for nm in names] + [v[nm] for nm in names]
    shapes = [jax.ShapeDtypeStruct(w[nm].shape, F32) for nm in names] * 4 + [jax.ShapeDtypeStruct((1, 1), F32)]
    res = pl.pallas_call(
        body,
        name="small_adamw",
        grid=(1,),
        in_specs=[_full(packs.shape)] + [_full(a.shape) for a in arrs],
        out_specs=[_full(sh.shape) for sh in shapes],
        out_shape=shapes,
        compiler_params=_params(1),
    )(packs, *arrs)
    return [dict(zip(names, res[k * n:(k + 1) * n])) for k in range(4)], res[4 * n]


def _w_in_pieces():
    cs = D_IN_PROJ // N_DEV
    pieces = []
    for d in range(N_DEV):
        lo, hi = d * cs, (d + 1) * cs
        if hi <= CONV_COLS:
            pieces.append((d, 0, cs, lo))
        elif lo >= CONV_COLS:
            pieces.append((d, 0, cs, lo - CONV_COLS + HALF_P))
        else:
            pieces.append((d, 0, CONV_COLS - lo, lo))
            pieces.append((d, CONV_COLS - lo, cs, HALF_P))
    return pieces


def _w_in_full(gathered):
    nb, D, cs = gathered.shape
    tr = _tile(D, 256)

    def body(g_ref, o_ref):
        o_ref[:, CONV_COLS:HALF_P] = jnp.zeros((tr, HALF_P - CONV_COLS), o_ref.dtype)
        o_ref[:, HALF_P + GLA_COLS:P_INT] = jnp.zeros((tr, HALF_P - GLA_COLS), o_ref.dtype)
        for d, a, b, dst in _w_in_pieces():
            o_ref[:, dst:dst + (b - a)] = g_ref[d, :, a:b]

    return pl.pallas_call(
        body,
        name="w_in_full",
        grid=(D // tr,),
        in_specs=[pl.BlockSpec((nb, tr, cs), lambda r: (0, r, 0))],
        out_specs=pl.BlockSpec((tr, P_INT), lambda r: (r, 0)),
        out_shape=jax.ShapeDtypeStruct((D, P_INT), gathered.dtype),
        compiler_params=_params(1),
    )(gathered)


def _w_in_blocks(dw):
    D = dw.shape[0]
    cs = D_IN_PROJ // N_DEV
    tr = _tile(D, 256)

    def body(w_ref, o_ref):
        for d, a, b, src in _w_in_pieces():
            o_ref[d, :, a:b] = w_ref[:, src:src + (b - a)]

    return pl.pallas_call(
        body,
        name="w_in_blocks",
        grid=(D // tr,),
        in_specs=[pl.BlockSpec((tr, P_INT), lambda r: (r, 0))],
        out_specs=pl.BlockSpec((N_DEV, tr, cs), lambda r: (0, r, 0)),
        out_shape=jax.ShapeDtypeStruct((N_DEV, D, cs), dw.dtype),
        compiler_params=_params(1),
    )(dw)


BIG = ("w_in", "w_out", "w_ff_up", "w_ff_down")
ORDER = ("w_in", "conv_w", "conv_norm_g", "w_gate_up", "gate_bias", "gla_norm_g", "w_out", "ln1_g", "ln1_b",
         "w_ff_up", "w_ff_down", "ln2_g", "ln2_b")


def kernel(x, w_in, conv_w, conv_norm_g, w_gate_up, gate_bias, gla_norm_g, w_out, ln1_g, ln1_b, w_ff_up, w_ff_down, ln2_g, ln2_b, loss_target, m_w_in, m_conv_w, m_conv_norm_g, m_w_gate_up, m_gate_bias, m_gla_norm_g, m_w_out, m_ln1_g, m_ln1_b, m_w_ff_up, m_w_ff_down, m_ln2_g, m_ln2_b, v_w_in, v_conv_w, v_conv_norm_g, v_w_gate_up, v_gate_bias, v_gla_norm_g, v_w_out, v_ln1_g, v_ln1_b, v_w_ff_up, v_w_ff_down, v_ln2_g, v_ln2_b):
    T, D = x.shape[1], x.shape[2]
    xs, target = x[0], loss_target[0]
    xi, yi, ci = lax.axis_index("x"), lax.axis_index("y"), lax.axis_index("c")
    chip = 2 * xi + yi
    dev = 2 * chip + ci
    others = [jnp.where(chip <= q, q + 1, q) for q in range(N_CHIP - 1)]
    ids = jnp.stack([dev, chip, ci] + others).astype(jnp.int32)
    conv_cols, gate_cols = conv_w.shape[2], w_gate_up.shape[2]

    def gather(nm, lands):
        return _xfer_start("gather_chips_" + nm, lands, _plan_gather_chips, 4 * len(lands))

    def pass_on(nm, started, after):
        lands = _xfer_wait("gather_chips_wait_" + nm, started, after)
        return _xfer_start("gather_pass_" + nm, lands, _plan_gather_pass, 3 * len(lands))

    def landed(nm, started, after):
        return _xfer_wait("gather_pass_wait_" + nm, started, after)

    rows = _small_rows(D, conv_cols, gate_cols)
    fwd_pack = _make_pack("pack_fwd", rows, {"conv_w": conv_w[0], "w_gate_up": w_gate_up[0]})
    ga_in = gather("w_in", [_cast_place("cast_place_w_in", ids, w_in[0]),
                            _cast_place("cast_place_pack", ids, fwd_pack, dtype=F32)])
    ga, dep = [], ga_in["token"]
    m_in, v_in = m_w_in[0], v_w_in[0]
    for nm, w in zip(BIG[1:], (w_out, w_ff_up, w_ff_down)):
        deps = [dep, m_in, v_in] if nm == "w_ff_down" else [dep]
        ga.append(gather(nm, [_cast_place("cast_place_" + nm, ids, w[0], deps)]))
        dep = ga[-1]["token"]
    xb = _cast_bf16(xs, [dep])
    gp_in = pass_on("w_in", ga_in, xb)
    g_in, g_pack = landed("w_in", gp_in, gp_in["token"])
    w_full = _w_in_full(g_in)
    r_cw, r_gw = rows["conv_w"][0], rows["w_gate_up"][0]
    conv_w_full = g_pack[:, r_cw:r_cw + 3, :conv_cols].transpose(1, 0, 2).reshape(3, -1)
    gate_w_full = g_pack[:, r_gw:r_gw + GATE_RANK, :gate_cols].transpose(1, 0, 2).reshape(GATE_RANK, -1)
    conv_w8 = jnp.pad(conv_w_full, ((0, 5), (0, 0)))
    wg128 = jnp.pad(gate_w_full, ((0, LANE - GATE_RANK), (0, 0))).astype(BF16)
    proj = _proj_fwd(xb, w_full)
    yin = _conv_fwd(proj, conv_w8, conv_norm_g)
    gp_out = pass_on("w_out", ga[0], yin)
    o_all, states, yin = _gla_fwd(proj, wg128, gate_bias, gla_norm_g, yin, deps=[gp_out["token"]])
    w_out_full = landed("w_out", gp_out, o_all)[0].reshape(-1, D)
    gp_up = pass_on("w_ff_up", ga[1], o_all)
    xhat1, x1, rstd1 = _mix_ln1(yin, w_out_full, xs, ln1_g, ln1_b, deps=[gp_up["token"]])
    (w_up_blk,) = landed("w_ff_up", gp_up, x1)
    half = N_DEV // 2
    ra, h2 = _ff_up(x1, w_up_blk, 0, half)
    gp_down = pass_on("w_ff_down", ga[2], ra)
    ra, h2 = _ff_up(x1, w_up_blk, half, N_DEV - half, prev=(ra, h2), deps=[gp_down["token"]])
    w_down_full = landed("w_ff_down", gp_down, ra)[0].reshape(-1, D)
    dh3, dh3b, g_ln2_g, g_ln2_b, loss = _ff_down_loss(h2, w_down_full, xhat1, target, ln1_g, ln1_b, ln2_g, ln2_b)

    def to_core(nm, grad):
        recv = lax.empty((N_CHIP,) + grad.shape[1:], F32)
        return _xfer_start("reduce_core_" + nm, [grad, recv], _plan_reduce_core, N_CHIP)

    def to_chips(nm, started, after):
        grad, recv = _xfer_wait("reduce_core_wait_" + nm, started, after)
        part = _chip_sums("chip_sums_" + nm, ids, grad, recv)
        land = lax.empty(part.shape, BF16)
        return grad, recv, _xfer_start("reduce_chips_" + nm, [part, land], _plan_reduce_chips, N_CHIP - 1)

    da = _ff_down_bwd_act(dh3b, w_down_full, ra)
    gw_down = _grad_w("grad_w_down", h2, dh3b).reshape(N_DEV, -1, D)
    rc_down = to_core("w_ff_down", gw_down)
    gw_up = _grad_w_up_blk(x1, da, N_DEV, deps=[rc_down["token"]])
    gw_down, rv_down, rs_down = to_chips("w_ff_down", rc_down, gw_up)
    rc_up = to_core("w_ff_up", gw_up)
    dh1, dh1b, g_ln1_g, g_ln1_b = _ff_up_bwd_ln1(da, w_up_blk, dh3, xhat1, rstd1, ln1_g,
                                                 deps=[rs_down["token"], rc_up["token"]])
    gw_up, rv_up, rs_up = to_chips("w_ff_up", rc_up, dh1b)
    dyin = _mix_bwd(dh1b, w_out_full, deps=[rs_up["token"]])
    gw_out = _grad_w("grad_w_out", yin, dh1b).reshape(N_DEV, -1, D)
    rc_out = to_core("w_out", gw_out)
    dproj, g_conv_w, g_conv_g = _conv_bwd(proj, dyin, conv_w8, conv_norm_g, deps=[rc_out["token"]])
    dproj, g_gate_w, g_gate_b, g_gla_g = _gla_bwd(proj, wg128, gate_bias, gla_norm_g, o_all, states, dyin, dproj)
    gw_out, rv_out, rs_out = to_chips("w_out", rc_out, dproj)
    gw_in = _w_in_blocks(_grad_w("grad_w_in", xb, dproj, tn_pref=1280, tk_pref=2048, deps=[rs_out["token"]]))
    rc_in = to_core("w_in", gw_in)

    big = {}

    def finish(nm, grad, recv, started, w, m, v, after):
        _, land = _xfer_wait("reduce_chips_wait_" + nm, started, after)
        res = _reduce_adamw("adamw_" + nm, ids, grad, recv, land, w[0], m[0], v[0])
        big[nm] = [a[None] for a in res]
        return res[0]

    done = finish("w_ff_down", gw_down, rv_down, rs_down, w_ff_down, m_w_ff_down, v_w_ff_down, rc_in["token"])
    done = finish("w_ff_up", gw_up, rv_up, rs_up, w_ff_up, m_w_ff_up, v_w_ff_up, done)
    full_rows = _small_rows(D, D_CONV, D_GLA_K)
    pack = _make_pack("pack_grads", full_rows, {
        "conv_w": g_conv_w, "conv_norm_g": g_conv_g, "gate_bias": g_gate_b, "gla_norm_g": g_gla_g, "ln1_g": g_ln1_g,
        "ln1_b": g_ln1_b, "ln2_g": g_ln2_g, "ln2_b": g_ln2_b, "loss": loss, "w_gate_up": g_gate_w})
    (packs,) = _all_gather("gather_small_grads", [pack], deps=[done])
    gw_in, rv_in, rs_in = to_chips("w_in", rc_in, packs)
    done = finish("w_out", gw_out, rv_out, rs_out, w_out, m_w_out, v_w_out, rs_in["token"])
    grad_x = _proj_bwd_x(dproj, w_full, dh1, deps=[done])
    finish("w_in", gw_in, rv_in, rs_in, w_in, (m_in,), (v_in,), grad_x)

    def own_cols(row, n_rows, width):
        cut = lax.dynamic_slice(packs, (0, row, dev * width), (N_DEV, n_rows, width))
        return jnp.pad(cut, ((0, 0), (0, 0), (0, SP_COLS - width)))

    packs_own = jnp.concatenate([own_cols(r_cw, 3, conv_cols), packs[:, r_cw + 3:r_gw],
                                 own_cols(r_gw, GATE_RANK, gate_cols)], axis=1)
    as2d = lambda a: a[0] if a.ndim == 3 else a
    w_s = dict(zip(SMALL, map(as2d, (conv_w, conv_norm_g, w_gate_up, gate_bias, gla_norm_g, ln1_g, ln1_b, ln2_g, ln2_b))))
    m_s = dict(zip(SMALL, map(as2d, (m_conv_w, m_conv_norm_g, m_w_gate_up, m_gate_bias, m_gla_norm_g, m_ln1_g,
                                     m_ln1_b, m_ln2_g, m_ln2_b))))
    v_s = dict(zip(SMALL, map(as2d, (v_conv_w, v_conv_norm_g, v_w_gate_up, v_gate_bias, v_gla_norm_g, v_ln1_g,
                                     v_ln1_b, v_ln2_g, v_ln2_b))))
    small, loss_sum = _small_adamw(packs_own, rows, w_s, m_s, v_s)

    def leaf(kind, name):
        if name in BIG:
            return big[name][kind]
        a = small[kind][name]
        return a[None] if name in ("conv_w", "w_gate_up") else a

    out = [loss_sum[0, 0], grad_x[None]]
    for kind in range(4):
        out += [leaf(kind, nm) for nm in ORDER]
    return tuple(out)
```
